```python
import jax, jax.numpy as jnp
from jax import lax
import numpy as np

D_MODEL = 1024
BATCH = 1
SEQ = 16384
DEPTH = 2

POOL_WINDOWS = (2, 4, 8, 16)
POOL_GROUP = 64
POOL_WIDTH = POOL_GROUP * len(POOL_WINDOWS)
ATTN_HEADS = 4
ATTN_HEAD_DIM = 64
ATTN_WIDTH = ATTN_HEADS * ATTN_HEAD_DIM
IDX_HEADS = 8
IDX_DIM = 64
TOPK_MAX = 256
Q_BLOCK = 128
HGRN_HEADS = 4
HGRN_EXPAND = 128
HGRN_HEAD_V = 128
HGRN_KW = HGRN_HEADS * HGRN_EXPAND
HGRN_VW = HGRN_HEADS * HGRN_HEAD_V
HGRN_CHUNK = 64
KEY_MAX = 1.0 - 1e-6
REL_BUCKETS = 32
REL_MAX_DIST = 128
D_FF = 2816
N_SUB = 3
DN_ALPHA = (2 * DEPTH) ** 0.25
DN_BETA = (8 * DEPTH) ** -0.25
LN_EPS = 1e-5
RMS_EPS = 1e-6

MIX_WIDTH = POOL_WIDTH + ATTN_WIDTH + HGRN_VW
MIX_IN_WIDTHS = (POOL_WIDTH, ATTN_WIDTH, ATTN_WIDTH, ATTN_WIDTH, IDX_HEADS * IDX_DIM, IDX_DIM, IDX_HEADS,
                 HGRN_KW, HGRN_KW, HGRN_VW, HGRN_VW, 3 * D_MODEL)
MIX_IN_WIDTH = sum(MIX_IN_WIDTHS)
MIX_SPLITS = tuple(int(s) for s in np.cumsum(MIX_IN_WIDTHS)[:-1])

kernel_name = 'hybrid_pool_dsa_hgrn2_block'


def layer_norm(x, g, b):
    x32 = x.astype(jnp.float32)
    mu = jnp.mean(x32, axis=-1, keepdims=True)
    var = jnp.mean(jnp.square(x32 - mu), axis=-1, keepdims=True)
    y = (x32 - mu) * lax.rsqrt(var + LN_EPS)
    return (y * g.astype(jnp.float32) + b.astype(jnp.float32)).astype(x.dtype)


def swiglu(h, w_in, w_out):
    gate, up = jnp.split(h @ w_in, 2, axis=-1)
    return (jax.nn.silu(gate) * up) @ w_out


def modulate(h, mod, j):
    return h * (1 + mod[:, j, 1][:, None, :]) + mod[:, j, 0][:, None, :]


def sub_gate(mod, j):
    return 1 + mod[:, j, 2][:, None, :]


def pool_mixer(a, pool_w, pool_scale):
    B, S, _ = a.shape
    a32 = a.astype(jnp.float32)
    P = jnp.pad(jnp.cumsum(a32, axis=1), ((0, 0), (1, 0), (0, 0)))
    pos1 = jnp.arange(1, S + 1)
    means = []
    for g, w in enumerate(POOL_WINDOWS):
        Pg = P[..., g * POOL_GROUP:(g + 1) * POOL_GROUP]
        lead = Pg[:, 1:]
        lag = jnp.concatenate([jnp.zeros_like(Pg[:, :w - 1]), Pg[:, :S + 1 - w]], axis=1)
        cnt = jnp.minimum(pos1, w).astype(jnp.float32)
        means.append((lead - lag) / cnt[None, :, None])
    pooled = jnp.stack(means, axis=2)
    d = (pooled - a32.reshape(B, S, len(POOL_WINDOWS), POOL_GROUP)).astype(a.dtype)
    y = jnp.einsum('bsgc,gcd->bsgd', d, pool_w)
    return y.reshape(B, S, POOL_WIDTH) * pool_scale


def t5_bucket(dist):
    max_exact = REL_BUCKETS // 2
    d32 = jnp.maximum(dist, 1).astype(jnp.float32)
    large = max_exact + (jnp.log(d32 / max_exact) / np.log(REL_MAX_DIST / max_exact)
                         * (REL_BUCKETS - max_exact)).astype(jnp.int32)
    large = jnp.minimum(large, REL_BUCKETS - 1)
    return jnp.where(dist < max_exact, dist, large)


def sparse_attention(q, k, v, q_idx, k_idx, w_idx, rel_bias):
    B, S, H, dh = q.shape
    top_k = min(TOPK_MAX, S // 4)
    n_blocks = S // Q_BLOCK
    scale = dh ** -0.5
    idx_scale = (IDX_DIM ** -0.5) * (IDX_HEADS ** -0.5)
    key_pos = jnp.arange(S)

    def block(i):
        q0 = i * Q_BLOCK
        qb = lax.dynamic_slice_in_dim(q, q0, Q_BLOCK, axis=1)
        qib = lax.dynamic_slice_in_dim(q_idx, q0, Q_BLOCK, axis=1)
        wb = lax.dynamic_slice_in_dim(w_idx, q0, Q_BLOCK, axis=1)
        qpos = q0 + jnp.arange(Q_BLOCK)
        logit_idx = jnp.einsum('bqhd,bsd->bqhs', qib, k_idx)
        score = jnp.einsum('bqh,bqhs->bqs', wb, jax.nn.relu(logit_idx)).astype(jnp.float32) * idx_scale
        causal = key_pos[None, :] <= qpos[:, None]
        score = jnp.where(causal[None], score, -jnp.inf)
        _, sel = lax.top_k(score, top_k)
        k_sel = jax.vmap(lambda kk, ii: kk[ii])(k, sel)
        v_sel = jax.vmap(lambda vv, ii: vv[ii])(v, sel)
        logits = jnp.einsum('bqhd,bqkhd->bhqk', qb, k_sel).astype(jnp.float32) * scale
        dist = qpos[None, :, None] - sel
        bias = rel_bias[t5_bucket(jnp.maximum(dist, 0))].astype(jnp.float32)
        logits = logits + jnp.transpose(bias, (0, 3, 1, 2))
        logits = jnp.where((dist >= 0)[:, None], logits, -jnp.inf)
        p = jax.nn.softmax(logits, axis=-1).astype(v.dtype)
        return jnp.einsum('bhqk,bqkhd->bqhd', p, v_sel)

    out = lax.map(block, jnp.arange(n_blocks))
    return jnp.transpose(out, (1, 0, 2, 3, 4)).reshape(B, S, H * dh)


def hgrn2(q, f_logit, i_val, g_out, lb, norm_g):
    B, S, _ = q.shape
    C = HGRN_CHUNK
    NC = S // C
    z = f_logit.astype(jnp.float32)
    lb = lb.astype(jnp.float32)
    key = (1 - lb) * jax.nn.sigmoid(-z)
    log_f = jnp.log1p(-jnp.minimum(key, KEY_MAX))
    qf = jax.nn.silu(q.astype(jnp.float32))

    def to_chunks(t, d):
        return jnp.transpose(t.reshape(B, NC, C, HGRN_HEADS, d), (1, 0, 3, 2, 4))

    qc = to_chunks(qf, HGRN_EXPAND)
    kc = to_chunks(key, HGRN_EXPAND)
    gc = to_chunks(log_f, HGRN_EXPAND)
    vc = to_chunks(i_val.astype(jnp.float32), HGRN_HEAD_V)
    tri = jnp.tril(jnp.ones((C, C), dtype=bool))

    def step(state, inp):
        qq, kk, vv, gg = inp
        A = jnp.cumsum(gg, axis=2)
        o_inter = jnp.einsum('bhtd,bhdv->bhtv', qq * jnp.exp(A), state)
        diff = A[:, :, :, None, :] - A[:, :, None, :, :]
        decay = jnp.exp(jnp.where(tri[None, None, :, :, None], diff, -jnp.inf))
        scores = jnp.einsum('bhtd,bhsd,bhtsd->bhts', qq, kk, decay)
        o_intra = jnp.einsum('bhts,bhsv->bhtv', scores, vv)
        k_dec = kk * jnp.exp(A[:, :, -1:, :] - A)
        new_state = jnp.exp(A[:, :, -1, :])[..., None] * state + jnp.einsum('bhsd,bhsv->bhdv', k_dec, vv)
        return new_state, o_inter + o_intra

    s0 = jnp.zeros((B, HGRN_HEADS, HGRN_EXPAND, HGRN_HEAD_V), jnp.float32)
    _, o = lax.scan(step, s0, (qc, kc, vc, gc))
    o = jnp.transpose(o, (1, 0, 3, 2, 4)).reshape(B, S, HGRN_HEADS, HGRN_HEAD_V)
    o = o * lax.rsqrt(jnp.mean(jnp.square(o), axis=-1, keepdims=True) + RMS_EPS)
    o = o.reshape(B, S, HGRN_VW).astype(q.dtype)
    return o * norm_g * jax.nn.silu(g_out)


def token_mixer(h, w_in, pool_w, pool_scale, rel_bias, lb, norm_g, w_branch, w_out):
    B, S, _ = h.shape
    proj = h @ w_in
    (a, q, k, v, qi, ki, wi, hq, hf, hi, hg, gates) = jnp.split(proj, MIX_SPLITS, axis=-1)
    y_a = pool_mixer(a, pool_w, pool_scale)
    y_b = sparse_attention(q.reshape(B, S, ATTN_HEADS, ATTN_HEAD_DIM),
                           k.reshape(B, S, ATTN_HEADS, ATTN_HEAD_DIM),
                           v.reshape(B, S, ATTN_HEADS, ATTN_HEAD_DIM),
                           qi.reshape(B, S, IDX_HEADS, IDX_DIM), ki, wi, rel_bias)
    y_c = hgrn2(hq, hf, hi, hg, lb, norm_g)
    g = jax.nn.sigmoid(gates.reshape(B, S, 3, D_MODEL))
    merged = (g[:, :, 0] * (y_a @ w_branch[:POOL_WIDTH])
              + g[:, :, 1] * (y_b @ w_branch[POOL_WIDTH:POOL_WIDTH + ATTN_WIDTH])
              + g[:, :, 2] * (y_c @ w_branch[POOL_WIDTH + ATTN_WIDTH:]))
    return merged @ w_out


def setup_inputs(seed: int = 0) -> dict:
    key = jax.random.key(seed)
    ks = jax.random.split(key, 16)
    f32 = jnp.float32

    def n(k, shape):
        return jax.random.normal(k, shape, f32)

    x = n(ks[0], (BATCH, SEQ, D_MODEL))
    c = n(ks[1], (BATCH, D_MODEL))
    w_ada = n(ks[2], (DEPTH, D_MODEL, N_SUB * 3 * D_MODEL)) * (0.3 * D_MODEL ** -0.5)
    b_ada = 0.01 * n(ks[3], (DEPTH, N_SUB * 3 * D_MODEL))
    ln_g = 1.0 + 0.02 * n(ks[4], (DEPTH, N_SUB, D_MODEL))
    ln_b = 0.02 * n(ks[5], (DEPTH, N_SUB, D_MODEL))
    ffn_w_in = n(ks[6], (DEPTH, 2, D_MODEL, 2 * D_FF)) * D_MODEL ** -0.5
    ffn_w_out = n(ks[7], (DEPTH, 2, D_FF, D_MODEL)) * (DN_BETA * D_FF ** -0.5)
    mix_w_in = n(ks[8], (DEPTH, D_MODEL, MIX_IN_WIDTH)) * D_MODEL ** -0.5
    pool_w = n(ks[9], (DEPTH, len(POOL_WINDOWS), POOL_GROUP, POOL_GROUP)) * POOL_GROUP ** -0.5
    pool_scale = 1.0 + 0.02 * n(ks[10], (DEPTH, POOL_WIDTH))
    rel_bias = 0.5 * n(ks[11], (REL_BUCKETS, ATTN_HEADS))
    hgrn_lb = 1.0 + 0.1 * n(ks[12], (DEPTH, HGRN_KW))
    hgrn_norm_g = 1.0 + 0.02 * n(ks[13], (DEPTH, HGRN_VW))
    row_scale = jnp.concatenate([jnp.full((POOL_WIDTH,), POOL_WIDTH ** -0.5, f32),
                                 jnp.full((ATTN_WIDTH,), ATTN_WIDTH ** -0.5, f32),
                                 jnp.full((HGRN_VW,), HGRN_VW ** -0.5, f32)])
    w_branch = n(ks[14], (DEPTH, MIX_WIDTH, D_MODEL)) * row_scale[None, :, None]
    w_out = n(ks[15], (DEPTH, D_MODEL, D_MODEL)) * (DN_BETA * D_MODEL ** -0.5)
    return {'x': x, 'c': c, 'w_ada': w_ada, 'b_ada': b_ada, 'ln_g': ln_g, 'ln_b': ln_b,
            'ffn_w_in': ffn_w_in, 'ffn_w_out': ffn_w_out, 'mix_w_in': mix_w_in,
            'pool_w': pool_w, 'pool_scale': pool_scale, 'rel_bias': rel_bias,
            'hgrn_lb': hgrn_lb, 'hgrn_norm_g': hgrn_norm_g, 'w_branch': w_branch, 'w_out': w_out}


def reference(x, c, w_ada, b_ada, ln_g, ln_b, ffn_w_in, ffn_w_out, mix_w_in, pool_w, pool_scale,
              rel_bias, hgrn_lb, hgrn_norm_g, w_branch, w_out):
    B = x.shape[0]
    lb_sm = jax.nn.softmax(hgrn_lb.astype(jnp.float32), axis=0)
    lbs = jnp.cumsum(lb_sm, axis=0) - lb_sm[0:1]
    cond = jax.nn.silu(c)
    for l in range(DEPTH):
        mod = (cond @ w_ada[l] + b_ada[l]).reshape(B, N_SUB, 3, D_MODEL)
        y = swiglu(modulate(x, mod, 0), ffn_w_in[l, 0], ffn_w_out[l, 0])
        x = layer_norm(DN_ALPHA * x + 0.5 * sub_gate(mod, 0) * y, ln_g[l, 0], ln_b[l, 0])
        y = token_mixer(modulate(x, mod, 1), mix_w_in[l], pool_w[l], pool_scale[l], rel_bias,
                        lbs[l], hgrn_norm_g[l], w_branch[l], w_out[l])
        x = layer_norm(DN_ALPHA * x + sub_gate(mod, 1) * y, ln_g[l, 1], ln_b[l, 1])
        y = swiglu(modulate(x, mod, 2), ffn_w_in[l, 1], ffn_w_out[l, 1])
        x = layer_norm(DN_ALPHA * x + 0.5 * sub_gate(mod, 2) * y, ln_g[l, 2], ln_b[l, 2])
    return x
```

```python
import functools
import math

import jax
import jax.numpy as jnp
from jax import lax
from jax.experimental import pallas as pl
from jax.experimental.pallas import tpu as pltpu

F32 = jnp.float32
BF16 = jnp.bfloat16
I32 = jnp.int32
HIGHEST = lax.Precision.HIGHEST

POOL_WINDOWS = (2, 4, 8, 16)
POOL_GROUP = 64
POOL_WIDTH = POOL_GROUP * len(POOL_WINDOWS)
ATTN_HEADS = 4
ATTN_HEAD_DIM = 64
ATTN_WIDTH = ATTN_HEADS * ATTN_HEAD_DIM
IDX_HEADS = 8
IDX_DIM = 64
TOPK_MAX = 256
HGRN_HEADS = 4
HGRN_DIM = 128
HGRN_WIDTH = HGRN_HEADS * HGRN_DIM
KEY_MAX = 1.0 - 1e-6
REL_BUCKETS = 32
REL_MAX_DIST = 128
N_SUB = 3
LN_EPS = 1e-5
RMS_EPS = 1e-6

LANES = 128
SUBLANES = 8
VMEM_LIMIT_BYTES = 58 * 1024 * 1024

NEG_INF = float("-inf")
POS_INF = float("inf")


def _cparams(*sem):
    return pltpu.CompilerParams(dimension_semantics=sem, vmem_limit_bytes=VMEM_LIMIT_BYTES)


def _dot(a, b):
    return jnp.dot(a, b, preferred_element_type=F32)


def _dot_nt(a, b):
    return lax.dot_general(a, b, (((1,), (1,)), ((), ())), preferred_element_type=F32)


def _dot_tn(a, b):
    return lax.dot_general(a, b, (((0,), (0,)), ((), ())), preferred_element_type=F32)


def _log2(n):
    assert n & (n - 1) == 0
    return n.bit_length() - 1


def _silu(x):
    return x * jax.nn.sigmoid(x)


def _mod_rows(mod_ref, j):
    return mod_ref[3 * j:3 * j + 1, :], mod_ref[3 * j + 1:3 * j + 2, :], mod_ref[3 * j + 2:3 * j + 3, :]


def _layer_norm(z, g, b):
    mu = jnp.mean(z, axis=-1, keepdims=True)
    zc = z - mu
    var = jnp.mean(zc * zc, axis=-1, keepdims=True)
    return zc * lax.rsqrt(var + LN_EPS) * g + b


def _ada_kernel(c_ref, w_ref, b_ref, o_ref):
    c = c_ref[...]
    r = jnp.dot(_silu(c), w_ref[0], precision=HIGHEST, preferred_element_type=F32)
    o_ref[0] = r[0:1, :] + b_ref[0]


def _ada_call(c, w_ada, b_ada):
    depth, d, n = w_ada.shape
    tn = n // 8
    c8 = jnp.broadcast_to(c, (SUBLANES, d))
    return pl.pallas_call(
        _ada_kernel,
        grid=(depth, n // tn),
        in_specs=[pl.BlockSpec((SUBLANES, d), lambda l, j: (0, 0)),
                  pl.BlockSpec((1, d, tn), lambda l, j: (l, 0, j)),
                  pl.BlockSpec((1, 1, tn), lambda l, j: (l, 0, j))],
        out_specs=pl.BlockSpec((1, 1, tn), lambda l, j: (l, 0, j)),
        out_shape=jax.ShapeDtypeStruct((depth, 1, n), F32),
        compiler_params=_cparams("arbitrary", "arbitrary"),
        name="ada_mod",
    )(c8, w_ada, b_ada.reshape(depth, 1, n))


def _ffn_kernel(x_ref, mod_ref, wg_ref, wu_ref, wo_ref, lng_ref, lnb_ref, o_ref, h_sc, acc_sc,
                *, sub, n_f, alpha):
    f = pl.program_id(1)
    shift, scale, gate = _mod_rows(mod_ref, sub)

    @pl.when(f == 0)
    def _():
        h_sc[...] = (x_ref[...] * (1.0 + scale) + shift).astype(BF16)
        acc_sc[...] = jnp.zeros_like(acc_sc)

    h = h_sc[...]
    g = _dot(h, wg_ref[...])
    u = _dot(h, wu_ref[...])
    act = (_silu(g) * u).astype(BF16)
    acc_sc[...] += _dot(act, wo_ref[...])

    @pl.when(f == n_f - 1)
    def _():
        z = alpha * x_ref[...] + (0.5 * (1.0 + gate)) * acc_sc[...]
        o_ref[...] = _layer_norm(z, lng_ref[sub:sub + 1, :], lnb_ref[sub:sub + 1, :])


def _ffn_call(x, mod, w_in, w_out, ln_g, ln_b, *, sub, alpha, tm, tf):
    s, d = x.shape
    ff = w_out.shape[0]
    n_f = ff // tf
    kern = functools.partial(_ffn_kernel, sub=sub, n_f=n_f, alpha=alpha)
    return pl.pallas_call(
        kern,
        grid=(s // tm, n_f),
        in_specs=[pl.BlockSpec((tm, d), lambda i, f: (i, 0)),
                  pl.BlockSpec((3 * N_SUB, d), lambda i, f: (0, 0)),
                  pl.BlockSpec((d, tf), lambda i, f: (0, f)),
                  pl.BlockSpec((d, tf), lambda i, f: (0, f + n_f)),
                  pl.BlockSpec((tf, d), lambda i, f: (f, 0)),
                  pl.BlockSpec((N_SUB, d), lambda i, f: (0, 0)),
                  pl.BlockSpec((N_SUB, d), lambda i, f: (0, 0))],
        out_specs=pl.BlockSpec((tm, d), lambda i, f: (i, 0)),
        out_shape=jax.ShapeDtypeStruct((s, d), F32),
        scratch_shapes=[pltpu.VMEM((tm, d), BF16), pltpu.VMEM((tm, d), F32)],
        compiler_params=_cparams("arbitrary", "arbitrary"),
        name="ffn",
    )(x, mod, w_in, w_in, w_out, ln_g, ln_b)


_PROJ_OUTS = (("gates", 3 * 1024, F32), ("a", POOL_WIDTH, F32), ("h4", 4 * HGRN_WIDTH, F32),
              ("qi", IDX_HEADS * IDX_DIM, BF16), ("q", ATTN_WIDTH, BF16), ("k", ATTN_WIDTH, BF16),
              ("v", ATTN_WIDTH, BF16), ("ki2", 2 * IDX_DIM, BF16), ("wi", LANES, F32))
_PROJ_CHUNK = 1024


def _proj_kernel(x_ref, mod_ref, w_ref, *o_refs):
    shift, scale, _ = _mod_rows(mod_ref, 1)
    h = (x_ref[...] * (1.0 + scale) + shift).astype(BF16)
    c0 = 0
    for (_, width, dt), o_ref in zip(_PROJ_OUTS, o_refs):
        for cc in range(0, width, _PROJ_CHUNK):
            cw = min(_PROJ_CHUNK, width - cc)
            o_ref[:, cc:cc + cw] = _dot(h, w_ref[:, c0 + cc:c0 + cc + cw]).astype(dt)
        c0 += width


def _proj_weight(w):
    d = w.shape[0]
    widths = (POOL_WIDTH, ATTN_WIDTH, ATTN_WIDTH, ATTN_WIDTH, IDX_HEADS * IDX_DIM, IDX_DIM, IDX_HEADS,
              HGRN_WIDTH, HGRN_WIDTH, HGRN_WIDTH, HGRN_WIDTH, 3 * d)
    parts, c0 = [], 0
    for wd in widths:
        parts.append(w[:, c0:c0 + wd])
        c0 += wd
    a, q, k, v, qi, ki, wi, hq, hf, hi, hg, gates = parts
    wi_pad = jnp.pad(wi, ((0, 0), (0, LANES - IDX_HEADS)))
    return jnp.concatenate([gates, a, hq, hf, hi, hg, qi, q, k, v, ki, ki, wi_pad], axis=1).astype(BF16)


def _proj_call(x, mod, w, *, tm):
    s, d = x.shape
    n = w.shape[1]
    assert n == sum(wd for _, wd, _ in _PROJ_OUTS)
    return pl.pallas_call(
        _proj_kernel,
        grid=(s // tm,),
        in_specs=[pl.BlockSpec((tm, d), lambda i: (i, 0)),
                  pl.BlockSpec((3 * N_SUB, d), lambda i: (0, 0)),
                  pl.BlockSpec((d, n), lambda i: (0, 0))],
        out_specs=[pl.BlockSpec((tm, wd), lambda i: (i, 0)) for _, wd, _ in _PROJ_OUTS],
        out_shape=[jax.ShapeDtypeStruct((s, wd), dt) for _, wd, dt in _PROJ_OUTS],
        compiler_params=_cparams("arbitrary"),
        name="mix_proj",
    )(x, mod, w)


_BISECT_MAX_ITERS = 320


def _t5_bucket(dist):
    max_exact = REL_BUCKETS // 2
    d32 = jnp.maximum(dist, 1).astype(F32)
    large = max_exact + (jnp.log(d32 / max_exact) / math.log(REL_MAX_DIST / max_exact)
                         * (REL_BUCKETS - max_exact)).astype(I32)
    large = jnp.minimum(large, REL_BUCKETS - 1)
    return jnp.where(dist < max_exact, dist, large)


def _dsa_kernel(relb_ref, qi_ref, wi_ref, q_ref, ki2_ref, k_ref, v_ref, o_ref,
                s_sc, bias_sc, qim_sc, qm_sc, m_sc, l_sc, acc_sc, tie_sc, *, tq, topk, idx_scale, scale):
    tk = tq
    i = pl.program_id(0)
    row = lax.broadcasted_iota(I32, (tq, tk), 0)
    col = lax.broadcasted_iota(I32, (tq, tk), 1)
    causal = col <= row
    lane = lax.broadcasted_iota(I32, (tq, LANES), 1)
    lo_half = lane < ATTN_HEAD_DIM

    @pl.when(i == 0)
    def _():
        for which in range(2):
            bucket = _t5_bucket(jnp.maximum(row - col + which * tq, 0))
            for h in range(ATTN_HEADS):
                b = jnp.zeros((tq, tk), F32)
                for bk in range(REL_BUCKETS):
                    b = jnp.where(bucket == bk, relb_ref[bk, h], b)
                bias_sc[h, which] = b

    qi = qi_ref[...]
    for h in range(IDX_HEADS):
        pair = qi[:, LANES * (h // 2):LANES * (h // 2 + 1)]
        qim_sc[h] = jnp.where(lo_half if h % 2 == 0 else ~lo_half, pair, jnp.zeros_like(pair))
    qs = (q_ref[...].astype(F32) * scale).astype(BF16)
    for h in range(ATTN_HEADS):
        pair = qs[:, LANES * (h // 2):LANES * (h // 2 + 1)]
        qm_sc[h] = jnp.where(lo_half if h % 2 == 0 else ~lo_half, pair, jnp.zeros_like(pair))

    w = wi_ref[...]

    def key_rows(jb):
        return pl.ds(pl.multiple_of(jb * tk, tk), tk)

    def fold(x):
        return [x[:, c:c + LANES] for c in range(0, tk, LANES)]

    def index_scores(jb):
        kb = ki2_ref[key_rows(jb), :]
        acc = jnp.zeros((tq, tk), F32)
        for h in range(IDX_HEADS):
            acc = acc + w[:, h:h + 1] * jnp.maximum(_dot_nt(qim_sc[h], kb), 0.0)
        return acc * idx_scale

    def p1_body(jb, carry):
        rmin, rmax = carry
        sc = index_scores(jb)
        s_sc[jb] = sc
        for part in fold(sc):
            rmin = jnp.minimum(rmin, part)
            rmax = jnp.maximum(rmax, part)
        return rmin, rmax

    rmin, rmax = lax.fori_loop(0, i, p1_body, (jnp.full((tq, LANES), POS_INF, F32),
                                               jnp.full((tq, LANES), NEG_INF, F32)))
    sc = index_scores(i)
    s_sc[i] = jnp.where(causal, sc, NEG_INF)
    for part in fold(jnp.where(causal, sc, POS_INF)):
        rmin = jnp.minimum(rmin, part)
    for part in fold(jnp.where(causal, sc, NEG_INF)):
        rmax = jnp.maximum(rmax, part)
    rowmin = jnp.min(rmin, axis=1, keepdims=True)
    rowmax = jnp.max(rmax, axis=1, keepdims=True)

    kf = float(topk)
    nfin = (i * tq + lax.broadcasted_iota(I32, (tq, 1), 0) + 1).astype(F32)
    small = nfin <= kf

    def count_ge(x):
        def body(jb, cnt):
            for part in fold(jnp.where(s_sc[jb] >= x, 1.0, 0.0)):
                cnt = cnt + part
            return cnt
        cnt = lax.fori_loop(0, i + 1, body, jnp.zeros((tq, LANES), F32))
        return jnp.sum(cnt, axis=1, keepdims=True)

    def bis_cond(st):
        it, _, _, _, _, done = st
        return jnp.logical_and(it < _BISECT_MAX_ITERS, jnp.min(done) < 0.5)

    def bis_body(st):
        it, lo, hi, clo, chi, done = st
        mid = jnp.where(it == 0, rowmax, 0.5 * lo + 0.5 * hi)
        stuck = jnp.logical_or(mid <= lo, mid >= hi)
        c = count_ge(mid)
        active = jnp.logical_and(done < 0.5, jnp.logical_not(stuck))
        up = jnp.logical_and(active, c >= kf)
        dn = jnp.logical_and(active, c < kf)
        lo = jnp.where(up, mid, lo)
        clo = jnp.where(up, c, clo)
        hi = jnp.where(dn, mid, hi)
        chi = jnp.where(dn, c, chi)
        fin = jnp.logical_or(stuck, c == kf)
        done = jnp.where(fin, 1.0, done)
        return it + 1, lo, hi, clo, chi, done

    st0 = (jnp.int32(0), jnp.where(small, NEG_INF, rowmin), jnp.full((tq, 1), POS_INF, F32), nfin,
           jnp.zeros((tq, 1), F32), jnp.where(small, 1.0, 0.0))
    _, lo, hi, clo, chi, _ = lax.while_loop(bis_cond, bis_body, st0)
    budget = kf - chi
    need_ties = jnp.max((clo - chi) - budget) > 0.5

    def attend(jb, kind, with_ties):
        s = s_sc[jb]
        if with_ties:
            cand = jnp.logical_and(s >= lo, s < hi)
            candf = jnp.where(cand, 1.0, 0.0)
            before = (row < col).astype(BF16)
            rank = tie_sc[...] + _dot(candf.astype(BF16), before)
            tie_sc[...] += jnp.sum(candf, axis=1, keepdims=True)
            sel = jnp.logical_or(s >= hi, jnp.logical_and(cand, rank < budget))
        else:
            sel = s >= lo
        if kind == "diag":
            sel = jnp.logical_and(sel, causal)
        kb = k_ref[key_rows(jb), :]
        vb = v_ref[key_rows(jb), :]
        for h in range(ATTN_HEADS):
            pr = slice(LANES * (h // 2), LANES * (h // 2 + 1))
            lg = _dot_nt(qm_sc[h], kb[:, pr])
            if kind == "far":
                lg = lg + relb_ref[REL_BUCKETS - 1, h]
            else:
                lg = lg + bias_sc[h, 0 if kind == "diag" else 1]
            lg = jnp.where(sel, lg, NEG_INF)
            m_old = m_sc[h]
            m_new = jnp.maximum(m_old, jnp.max(lg, axis=1, keepdims=True))
            m_safe = jnp.where(m_new == NEG_INF, 0.0, m_new)
            alpha = jnp.exp(m_old - m_safe)
            p = jnp.exp(lg - m_safe)
            l_sc[h] = alpha * l_sc[h] + jnp.sum(p, axis=1, keepdims=True)
            acc_sc[h] = alpha * acc_sc[h] + _dot(p.astype(BF16), vb[:, pr])
            m_sc[h] = m_new

    def run(with_ties):
        m_sc[...] = jnp.full(m_sc.shape, NEG_INF, F32)
        l_sc[...] = jnp.zeros_like(l_sc)
        acc_sc[...] = jnp.zeros_like(acc_sc)
        tie_sc[...] = jnp.zeros_like(tie_sc)

        def far_body(jb, carry):
            attend(jb, "far", with_ties)
            return carry
        lax.fori_loop(0, jnp.maximum(i - 1, 0), far_body, 0)

        @pl.when(i >= 1)
        def _():
            attend(i - 1, "sub", with_ties)

        attend(i, "diag", with_ties)

    @pl.when(need_ties)
    def _():
        run(True)

    @pl.when(jnp.logical_not(need_ties))
    def _():
        run(False)

    for p in range(ATTN_HEADS // 2):
        even = acc_sc[2 * p] / l_sc[2 * p]
        odd = acc_sc[2 * p + 1] / l_sc[2 * p + 1]
        o_ref[:, LANES * p:LANES * (p + 1)] = jnp.where(lo_half, even, odd)


def _dsa_call(rel_bias, qi, wi, q, ki2, k, v, *, tq):
    s = q.shape[0]
    assert tq > REL_MAX_DIST and s % tq == 0
    nb = s // tq
    topk = min(TOPK_MAX, s // 4)
    kern = functools.partial(_dsa_kernel, tq=tq, topk=topk,
                             idx_scale=(IDX_DIM ** -0.5) * (IDX_HEADS ** -0.5), scale=ATTN_HEAD_DIM ** -0.5)
    whole = lambda width: pl.BlockSpec((s, width), lambda i: (0, 0), pipeline_mode=pl.Buffered(1))
    return pl.pallas_call(
        kern,
        grid=(nb,),
        in_specs=[pl.BlockSpec(memory_space=pltpu.SMEM),
                  pl.BlockSpec((tq, IDX_HEADS * IDX_DIM), lambda i: (i, 0)),
                  pl.BlockSpec((tq, LANES), lambda i: (i, 0)),
                  pl.BlockSpec((tq, ATTN_WIDTH), lambda i: (i, 0)),
                  whole(2 * IDX_DIM), whole(ATTN_WIDTH), whole(ATTN_WIDTH)],
        out_specs=pl.BlockSpec((tq, ATTN_WIDTH), lambda i: (i, 0)),
        out_shape=jax.ShapeDtypeStruct((s, ATTN_WIDTH), F32),
        scratch_shapes=[pltpu.VMEM((nb, tq, tq), F32),
                        pltpu.VMEM((ATTN_HEADS, 2, tq, tq), F32),
                        pltpu.VMEM((IDX_HEADS, tq, LANES), BF16),
                        pltpu.VMEM((ATTN_HEADS, tq, LANES), BF16),
                        pltpu.VMEM((ATTN_HEADS, tq, 1), F32),
                        pltpu.VMEM((ATTN_HEADS, tq, 1), F32),
                        pltpu.VMEM((ATTN_HEADS, tq, LANES), F32),
                        pltpu.VMEM((tq, 1), F32)],
        compiler_params=_cparams("arbitrary"),
        name="dsa_attention",
    )(rel_bias, qi, wi, q, ki2, k, v)


_HGRN_CHUNK = 128
_HGRN_BASE = SUBLANES


def _hgrn_kernel(h4_ref, lb_ref, ng_ref, o_ref, st_sc, *, layer, rows):
    c_len = _HGRN_CHUNK

    @pl.when(pl.program_id(0) == 0)
    def _():
        st_sc[...] = jnp.zeros_like(st_sc)

    lb_all = lb_ref[...]
    e = jnp.exp(lb_all - jnp.max(lb_all, axis=0, keepdims=True))
    sm = e / jnp.sum(e, axis=0, keepdims=True)
    lbs = jnp.sum(sm[0:layer + 1, :], axis=0, keepdims=True) - sm[0:1, :]

    r1 = lax.broadcasted_iota(I32, (c_len, 1), 0)
    rr = lax.broadcasted_iota(I32, (c_len, c_len), 0)
    cc = lax.broadcasted_iota(I32, (c_len, c_len), 1)

    def block_ref(a, blk, r):
        x = a.reshape(c_len // blk, blk, a.shape[-1])[:, r:r + 1, :]
        return jnp.broadcast_to(x, (c_len // blk, blk, a.shape[-1])).reshape(c_len, a.shape[-1])

    w = HGRN_WIDTH
    for c in range(rows // c_len):
        rs = slice(c * c_len, (c + 1) * c_len)
        hq = h4_ref[rs, 0:w]
        z = h4_ref[rs, w:2 * w]
        val = h4_ref[rs, 2 * w:3 * w]
        hg = h4_ref[rs, 3 * w:4 * w]
        key = (1.0 - lbs) * jax.nn.sigmoid(-z)
        g = jnp.log1p(-jnp.minimum(key, KEY_MAX))
        a_cum = g
        sh = 1
        while sh < c_len:
            a_cum = a_cum + jnp.where(r1 >= sh, pltpu.roll(a_cum, sh, 0), 0.0)
            sh *= 2
        qf = _silu(hq)
        outs = []
        for h in range(HGRN_HEADS):
            hs = slice(h * HGRN_DIM, (h + 1) * HGRN_DIM)
            ah, qh, kh, vh = a_cum[:, hs], qf[:, hs], key[:, hs], val[:, hs]
            vh16 = vh.astype(BF16)
            st = st_sc[h]
            o = _dot_nt((qh * jnp.exp(ah)).astype(BF16), st.astype(BF16))
            ref = block_ref(ah, _HGRN_BASE, _HGRN_BASE // 2 - 1)
            sc = _dot_nt((qh * jnp.exp(ah - ref)).astype(BF16), (kh * jnp.exp(ref - ah)).astype(BF16))
            same = (rr >> _log2(_HGRN_BASE)) == (cc >> _log2(_HGRN_BASE))
            scores = jnp.where(jnp.logical_and(same, cc <= rr), sc, 0.0)
            half = _HGRN_BASE
            while half < c_len:
                blk = 2 * half
                ref = block_ref(ah, blk, half - 1)
                second = (r1 & (blk - 1)) >= half
                ql = jnp.where(second, qh * jnp.exp(jnp.minimum(ah - ref, 0.0)), 0.0)
                kl = jnp.where(second, 0.0, kh * jnp.exp(jnp.minimum(ref - ah, 0.0)))
                sc = _dot_nt(ql.astype(BF16), kl.astype(BF16))
                scores = scores + jnp.where((rr >> _log2(blk)) == (cc >> _log2(blk)), sc, 0.0)
                half = blk
            o = o + _dot(scores.astype(BF16), vh16)
            a_last = ah[c_len - 1:c_len, :]
            kd = (kh * jnp.exp(a_last - ah)).astype(BF16)
            st_sc[h] = st * jnp.exp(a_last) + _dot_tn(vh16, kd)
            o = o * lax.rsqrt(jnp.mean(o * o, axis=-1, keepdims=True) + RMS_EPS)
            outs.append(o)
        o_ref[rs, :] = jnp.concatenate(outs, axis=1) * ng_ref[...] * _silu(hg)


def _hgrn_call(h4, lb, ng, *, layer, rows):
    s = h4.shape[0]
    kern = functools.partial(_hgrn_kernel, layer=layer, rows=rows)
    return pl.pallas_call(
        kern,
        grid=(s // rows,),
        in_specs=[pl.BlockSpec((rows, 4 * HGRN_WIDTH), lambda i: (i, 0)),
                  pl.BlockSpec(lb.shape, lambda i: (0, 0)),
                  pl.BlockSpec((1, HGRN_WIDTH), lambda i: (0, 0))],
        out_specs=pl.BlockSpec((rows, HGRN_WIDTH), lambda i: (i, 0)),
        out_shape=jax.ShapeDtypeStruct((s, HGRN_WIDTH), F32),
        scratch_shapes=[pltpu.VMEM((HGRN_HEADS, HGRN_DIM, HGRN_DIM), F32)],
        compiler_params=_cparams("arbitrary"),
        name="hgrn2",
    )(h4, lb, ng)


_POOL_HALO = 16


def _merge_kernel(x_ref, mod_ref, a_ref, ah_ref, yb_ref, yc_ref, g_ref, pw_ref, ps_ref, wb_ref, wo_ref,
                  lng_ref, lnb_ref, o_ref, *, tm, alpha):
    i = pl.program_id(0)
    d = x_ref.shape[1]
    _, _, gate = _mod_rows(mod_ref, 1)
    a = a_ref[...]
    halo = jnp.where(i == 0, 0.0, ah_ref[...])
    ext = jnp.concatenate([halo, a], axis=0)
    lane = lax.broadcasted_iota(I32, (tm, POOL_WIDTH), 1)
    grp = lane >> _log2(POOL_GROUP)
    pos1 = (i * tm + lax.broadcasted_iota(I32, (tm, POOL_WIDTH), 0) + 1).astype(F32)
    win = jnp.zeros((tm, POOL_WIDTH), F32)
    cnt = jnp.zeros((tm, POOL_WIDTH), F32)
    ssum, have = ext, 1
    for gi, wd in enumerate(POOL_WINDOWS):
        while have < wd:
            ssum = ssum + pltpu.roll(ssum, have, 0)
            have *= 2
        assert have == wd
        win = jnp.where(grp == gi, ssum[_POOL_HALO:, :], win)
        cnt = jnp.where(grp == gi, jnp.minimum(pos1, float(wd)), cnt)
    dlt = (win / cnt - a).astype(BF16)
    ya = _dot(dlt, pw_ref[...]) * ps_ref[...]
    za = _dot(ya.astype(BF16), wb_ref[0:POOL_WIDTH, :])
    zb = _dot(yb_ref[...].astype(BF16), wb_ref[POOL_WIDTH:POOL_WIDTH + ATTN_WIDTH, :])
    zc = _dot(yc_ref[...].astype(BF16), wb_ref[POOL_WIDTH + ATTN_WIDTH:, :])
    merged = (jax.nn.sigmoid(g_ref[:, 0:d]) * za + jax.nn.sigmoid(g_ref[:, d:2 * d]) * zb
              + jax.nn.sigmoid(g_ref[:, 2 * d:3 * d]) * zc)
    y = _dot(merged.astype(BF16), wo_ref[...])
    z = alpha * x_ref[...] + (1.0 + gate) * y
    o_ref[...] = _layer_norm(z, lng_ref[1:2, :], lnb_ref[1:2, :])


def _merge_call(x, mod, a, yb, yc, gates, pw_bd, ps, wb, wo, ln_g, ln_b, *, tm, alpha):
    s, d = x.shape
    kern = functools.partial(_merge_kernel, tm=tm, alpha=alpha)
    full = lambda arr: pl.BlockSpec(arr.shape, lambda i: (0, 0))
    hb = tm // _POOL_HALO
    return pl.pallas_call(
        kern,
        grid=(s // tm,),
        in_specs=[pl.BlockSpec((tm, d), lambda i: (i, 0)),
                  full(mod),
                  pl.BlockSpec((tm, POOL_WIDTH), lambda i: (i, 0)),
                  pl.BlockSpec((_POOL_HALO, POOL_WIDTH), lambda i: (jnp.maximum(i * hb - 1, 0), 0)),
                  pl.BlockSpec((tm, ATTN_WIDTH), lambda i: (i, 0)),
                  pl.BlockSpec((tm, HGRN_WIDTH), lambda i: (i, 0)),
                  pl.BlockSpec((tm, 3 * d), lambda i: (i, 0)),
                  full(pw_bd), full(ps), full(wb), full(wo), full(ln_g), full(ln_b)],
        out_specs=pl.BlockSpec((tm, d), lambda i: (i, 0)),
        out_shape=jax.ShapeDtypeStruct((s, d), F32),
        compiler_params=_cparams("arbitrary"),
        name="pool_merge",
    )(x, mod, a, a, yb, yc, gates, pw_bd, ps, wb, wo, ln_g, ln_b)


def _pool_blockdiag(pool_w):
    n = pool_w.shape[0]
    rows = [jnp.concatenate([pool_w[g] if g == j else jnp.zeros_like(pool_w[g]) for j in range(n)], axis=1)
            for g in range(n)]
    return jnp.concatenate(rows, axis=0)


def kernel(x, c, w_ada, b_ada, ln_g, ln_b, ffn_w_in, ffn_w_out, mix_w_in, pool_w, pool_scale, rel_bias,
           hgrn_lb, hgrn_norm_g, w_branch, w_out):
    batch, s, d = x.shape
    assert batch == 1, "kernels are written for a single sequence"
    depth = w_ada.shape[0]
    alpha = (2 * depth) ** 0.25
    tm_ffn = min(1024, s)
    tf = 256
    tm_proj = min(256, s)
    tq = min(256, s)
    tm_merge = min(256, s)
    hgrn_rows = min(256, s)

    mods = _ada_call(c, w_ada, b_ada)
    xs = x[0]
    for l in range(depth):
        mod = mods[l].reshape(3 * N_SUB, d)
        xs = _ffn_call(xs, mod, ffn_w_in[l, 0].astype(BF16), ffn_w_out[l, 0].astype(BF16), ln_g[l], ln_b[l],
                       sub=0, alpha=alpha, tm=tm_ffn, tf=tf)
        gates, a, h4, qi, q, k, v, ki2, wi = _proj_call(xs, mod, _proj_weight(mix_w_in[l]), tm=tm_proj)
        yb = _dsa_call(rel_bias, qi, wi, q, ki2, k, v, tq=tq)
        yc = _hgrn_call(h4, hgrn_lb, hgrn_norm_g[l].reshape(1, -1), layer=l, rows=hgrn_rows)
        xs = _merge_call(xs, mod, a, yb, yc, gates, _pool_blockdiag(pool_w[l]).astype(BF16),
                         pool_scale[l].reshape(1, -1), w_branch[l].astype(BF16), w_out[l].astype(BF16),
                         ln_g[l], ln_b[l], tm=tm_merge, alpha=alpha)
        xs = _ffn_call(xs, mod, ffn_w_in[l, 1].astype(BF16), ffn_w_out[l, 1].astype(BF16), ln_g[l], ln_b[l],
                       sub=2, alpha=alpha, tm=tm_ffn, tf=tf)
    return xs[None]
```

```python
import functools
import math

import jax
import jax.numpy as jnp
from jax import lax
from jax.experimental import pallas as pl
from jax.experimental.pallas import tpu as pltpu

F32 = jnp.float32
BF16 = jnp.bfloat16
I32 = jnp.int32
HIGHEST = lax.Precision.HIGHEST

POOL_WINDOWS = (2, 4, 8, 16)
POOL_GROUP = 64
POOL_WIDTH = POOL_GROUP * len(POOL_WINDOWS)
ATTN_HEADS = 4
ATTN_HEAD_DIM = 64
ATTN_WIDTH = ATTN_HEADS * ATTN_HEAD_DIM
IDX_HEADS = 8
IDX_DIM = 64
TOPK_MAX = 256
HGRN_HEADS = 4
HGRN_DIM = 128
HGRN_WIDTH = HGRN_HEADS * HGRN_DIM
KEY_MAX = 1.0 - 1e-6
REL_BUCKETS = 32
REL_MAX_DIST = 128
N_SUB = 3
LN_EPS = 1e-5
RMS_EPS = 1e-6

LANES = 128
SUBLANES = 8
VMEM_LIMIT_BYTES = 58 * 1024 * 1024

NEG_INF = float("-inf")
POS_INF = float("inf")


def _cparams(*sem):
    return pltpu.CompilerParams(dimension_semantics=sem, vmem_limit_bytes=VMEM_LIMIT_BYTES)


def _dot(a, b):
    return jnp.dot(a, b, preferred_element_type=F32)


def _dot_nt(a, b):
    return lax.dot_general(a, b, (((1,), (1,)), ((), ())), preferred_element_type=F32)


def _dot_tn(a, b):
    return lax.dot_general(a, b, (((0,), (0,)), ((), ())), preferred_element_type=F32)


def _log2(n):
    assert n & (n - 1) == 0
    return n.bit_length() - 1


def _silu(x):
    return x * jax.nn.sigmoid(x)


def _mod_rows(mod_ref, j):
    return mod_ref[3 * j:3 * j + 1, :], mod_ref[3 * j + 1:3 * j + 2, :], mod_ref[3 * j + 2:3 * j + 3, :]


def _layer_norm(z, g, b):
    mu = jnp.mean(z, axis=-1, keepdims=True)
    zc = z - mu
    var = jnp.mean(zc * zc, axis=-1, keepdims=True)
    return zc * lax.rsqrt(var + LN_EPS) * g + b


def _ada_kernel(c_ref, w_ref, b_ref, o_ref):
    c = c_ref[...]
    r = jnp.dot(_silu(c), w_ref[0], precision=HIGHEST, preferred_element_type=F32)
    o_ref[0] = r[0:1, :] + b_ref[0]


def _ada_call(c, w_ada, b_ada):
    depth, d, n = w_ada.shape
    tn = n // 8
    c8 = jnp.broadcast_to(c, (SUBLANES, d))
    return pl.pallas_call(
        _ada_kernel,
        grid=(depth, n // tn),
        in_specs=[pl.BlockSpec((SUBLANES, d), lambda l, j: (0, 0)),
                  pl.BlockSpec((1, d, tn), lambda l, j: (l, 0, j)),
                  pl.BlockSpec((1, 1, tn), lambda l, j: (l, 0, j))],
        out_specs=pl.BlockSpec((1, 1, tn), lambda l, j: (l, 0, j)),
        out_shape=jax.ShapeDtypeStruct((depth, 1, n), F32),
        compiler_params=_cparams("arbitrary", "arbitrary"),
        name="ada_mod",
    )(c8, w_ada, b_ada.reshape(depth, 1, n))


def _ffn_kernel(x_ref, mod_ref, wg_ref, wu_ref, wo_ref, lng_ref, lnb_ref, o_ref, h_sc, acc_sc,
                *, sub, n_f, alpha):
    f = pl.program_id(1)
    shift, scale, gate = _mod_rows(mod_ref, sub)

    @pl.when(f == 0)
    def _():
        h_sc[...] = (x_ref[...] * (1.0 + scale) + shift).astype(BF16)
        acc_sc[...] = jnp.zeros_like(acc_sc)

    h = h_sc[...]
    g = _dot(h, wg_ref[...])
    u = _dot(h, wu_ref[...])
    act = (_silu(g) * u).astype(BF16)
    acc_sc[...] += _dot(act, wo_ref[...])

    @pl.when(f == n_f - 1)
    def _():
        z = alpha * x_ref[...] + (0.5 * (1.0 + gate)) * acc_sc[...]
        o_ref[...] = _layer_norm(z, lng_ref[sub:sub + 1, :], lnb_ref[sub:sub + 1, :])


def _ffn_call(x, mod, w_in, w_out, ln_g, ln_b, *, sub, alpha, tm, tf):
    s, d = x.shape
    ff = w_out.shape[0]
    n_f = ff // tf
    kern = functools.partial(_ffn_kernel, sub=sub, n_f=n_f, alpha=alpha)
    return pl.pallas_call(
        kern,
        grid=(s // tm, n_f),
        in_specs=[pl.BlockSpec((tm, d), lambda i, f: (i, 0)),
                  pl.BlockSpec((3 * N_SUB, d), lambda i, f: (0, 0)),
                  pl.BlockSpec((d, tf), lambda i, f: (0, f)),
                  pl.BlockSpec((d, tf), lambda i, f: (0, f + n_f)),
                  pl.BlockSpec((tf, d), lambda i, f: (f, 0)),
                  pl.BlockSpec((N_SUB, d), lambda i, f: (0, 0)),
                  pl.BlockSpec((N_SUB, d), lambda i, f: (0, 0))],
        out_specs=pl.BlockSpec((tm, d), lambda i, f: (i, 0)),
        out_shape=jax.ShapeDtypeStruct((s, d), F32),
        scratch_shapes=[pltpu.VMEM((tm, d), BF16), pltpu.VMEM((tm, d), F32)],
        compiler_params=_cparams("arbitrary", "arbitrary"),
        name="ffn",
    )(x, mod, w_in, w_in, w_out, ln_g, ln_b)


_PROJ_OUTS = (("gates", 3 * 1024, F32), ("a", POOL_WIDTH, F32), ("h4", 4 * HGRN_WIDTH, F32),
              ("qi", IDX_HEADS * IDX_DIM, BF16), ("q", ATTN_WIDTH, BF16), ("k", ATTN_WIDTH, BF16),
              ("v", ATTN_WIDTH, BF16), ("ki2", 2 * IDX_DIM, BF16), ("wi", LANES, F32))
_PROJ_CHUNK = 1024


def _proj_kernel(x_ref, mod_ref, w_ref, *o_refs):
    shift, scale, _ = _mod_rows(mod_ref, 1)
    h = (x_ref[...] * (1.0 + scale) + shift).astype(BF16)
    c0 = 0
    for (_, width, dt), o_ref in zip(_PROJ_OUTS, o_refs):
        for cc in range(0, width, _PROJ_CHUNK):
            cw = min(_PROJ_CHUNK, width - cc)
            o_ref[:, cc:cc + cw] = _dot(h, w_ref[:, c0 + cc:c0 + cc + cw]).astype(dt)
        c0 += width


def _proj_weight(w):
    d = w.shape[0]
    widths = (POOL_WIDTH, ATTN_WIDTH, ATTN_WIDTH, ATTN_WIDTH, IDX_HEADS * IDX_DIM, IDX_DIM, IDX_HEADS,
              HGRN_WIDTH, HGRN_WIDTH, HGRN_WIDTH, HGRN_WIDTH, 3 * d)
    parts, c0 = [], 0
    for wd in widths:
        parts.append(w[:, c0:c0 + wd])
        c0 += wd
    a, q, k, v, qi, ki, wi, hq, hf, hi, hg, gates = parts
    wi_pad = jnp.pad(wi, ((0, 0), (0, LANES - IDX_HEADS)))
    return jnp.concatenate([gates, a, hq, hf, hi, hg, qi, q, k, v, ki, ki, wi_pad], axis=1).astype(BF16)


def _proj_call(x, mod, w, *, tm):
    s, d = x.shape
    n = w.shape[1]
    assert n == sum(wd for _, wd, _ in _PROJ_OUTS)
    return pl.pallas_call(
        _proj_kernel,
        grid=(s // tm,),
        in_specs=[pl.BlockSpec((tm, d), lambda i: (i, 0)),
                  pl.BlockSpec((3 * N_SUB, d), lambda i: (0, 0)),
                  pl.BlockSpec((d, n), lambda i: (0, 0))],
        out_specs=[pl.BlockSpec((tm, wd), lambda i: (i, 0)) for _, wd, _ in _PROJ_OUTS],
        out_shape=[jax.ShapeDtypeStruct((s, wd), dt) for _, wd, dt in _PROJ_OUTS],
        compiler_params=_cparams("arbitrary"),
        name="mix_proj",
    )(x, mod, w)


_BISECT_MAX_ITERS = 320
_F32_TINY = float(jnp.finfo(jnp.float32).tiny)


def _t5_bucket(dist):
    max_exact = REL_BUCKETS // 2
    d32 = jnp.maximum(dist, 1).astype(F32)
    large = max_exact + (jnp.log(d32 / max_exact) / math.log(REL_MAX_DIST / max_exact)
                         * (REL_BUCKETS - max_exact)).astype(I32)
    large = jnp.minimum(large, REL_BUCKETS - 1)
    return jnp.where(dist < max_exact, dist, large)


def _dsa_kernel(relb_ref, qi_ref, wi_ref, q_ref, ki2_ref, k_ref, v_ref, o_ref,
                s_sc, bias_sc, qim_sc, qm_sc, m_sc, l_sc, acc_sc, tie_sc, *, tq, topk, idx_scale, scale):
    tk = tq
    i = pl.program_id(0)
    row = lax.broadcasted_iota(I32, (tq, tk), 0)
    col = lax.broadcasted_iota(I32, (tq, tk), 1)
    causal = col <= row
    lane = lax.broadcasted_iota(I32, (tq, LANES), 1)
    lo_half = lane < ATTN_HEAD_DIM

    @pl.when(i == 0)
    def _():
        for which in range(2):
            bucket = _t5_bucket(jnp.maximum(row - col + which * tq, 0))
            for h in range(ATTN_HEADS):
                b = jnp.zeros((tq, tk), F32)
                for bk in range(REL_BUCKETS):
                    b = jnp.where(bucket == bk, relb_ref[bk, h], b)
                bias_sc[h, which] = b

    qi = qi_ref[...]
    for h in range(IDX_HEADS):
        pair = qi[:, LANES * (h // 2):LANES * (h // 2 + 1)]
        qim_sc[h] = jnp.where(lo_half if h % 2 == 0 else ~lo_half, pair, jnp.zeros_like(pair))
    qs = (q_ref[...].astype(F32) * scale).astype(BF16)
    for h in range(ATTN_HEADS):
        pair = qs[:, LANES * (h // 2):LANES * (h // 2 + 1)]
        qm_sc[h] = jnp.where(lo_half if h % 2 == 0 else ~lo_half, pair, jnp.zeros_like(pair))

    w = wi_ref[...]

    def key_rows(jb):
        return pl.ds(pl.multiple_of(jb * tk, tk), tk)

    def fold(x):
        return [x[:, c:c + LANES] for c in range(0, tk, LANES)]

    def index_scores(jb):
        kb = ki2_ref[key_rows(jb), :]
        acc = jnp.zeros((tq, tk), F32)
        for h in range(IDX_HEADS):
            acc = acc + w[:, h:h + 1] * jnp.maximum(_dot_nt(qim_sc[h], kb), 0.0)
        return acc * idx_scale

    def p1_body(jb, carry):
        rmin, rmax = carry
        sc = index_scores(jb)
        s_sc[jb] = sc
        for part in fold(sc):
            rmin = jnp.minimum(rmin, part)
            rmax = jnp.maximum(rmax, part)
        return rmin, rmax

    rmin, rmax = lax.fori_loop(0, i, p1_body, (jnp.full((tq, LANES), POS_INF, F32),
                                               jnp.full((tq, LANES), NEG_INF, F32)))
    sc = index_scores(i)
    s_sc[i] = jnp.where(causal, sc, NEG_INF)
    for part in fold(jnp.where(causal, sc, POS_INF)):
        rmin = jnp.minimum(rmin, part)
    for part in fold(jnp.where(causal, sc, NEG_INF)):
        rmax = jnp.maximum(rmax, part)
    rowmin = jnp.min(rmin, axis=1, keepdims=True)
    rowmax = jnp.max(rmax, axis=1, keepdims=True)

    kf = float(topk)
    nfin = (i * tq + lax.broadcasted_iota(I32, (tq, 1), 0) + 1).astype(F32)
    small = nfin <= kf

    def count_ge(x):
        def body(jb, cnt):
            for part in fold(jnp.where(s_sc[jb] >= x, 1.0, 0.0)):
                cnt = cnt + part
            return cnt
        cnt = lax.fori_loop(0, i + 1, body, jnp.zeros((tq, LANES), F32))
        return jnp.sum(cnt, axis=1, keepdims=True)

    def bis_cond(st):
        it, _, _, _, _, done = st
        return jnp.logical_and(it < _BISECT_MAX_ITERS, jnp.min(done) < 0.5)

    def bis_body(st):
        it, lo, hi, clo, chi, done = st
        mid = 0.5 * lo + 0.5 * hi
        mid = jnp.where(jnp.logical_and(lo == 0.0, hi > _F32_TINY), _F32_TINY, mid)
        mid = jnp.where(jnp.logical_and(lo < 0.0, hi > 0.0), 0.0, mid)
        mid = jnp.where(it == 0, rowmax, mid)
        stuck = jnp.logical_or(jnp.logical_or(mid <= lo, mid >= hi),
                               jnp.logical_and(lo == 0.0, hi <= _F32_TINY))
        c = count_ge(mid)
        active = jnp.logical_and(done < 0.5, jnp.logical_not(stuck))
        up = jnp.logical_and(active, c >= kf)
        dn = jnp.logical_and(active, c < kf)
        lo = jnp.where(up, mid, lo)
        clo = jnp.where(up, c, clo)
        hi = jnp.where(dn, mid, hi)
        chi = jnp.where(dn, c, chi)
        fin = jnp.logical_or(stuck, c == kf)
        done = jnp.where(fin, 1.0, done)
        return it + 1, lo, hi, clo, chi, done

    st0 = (jnp.int32(0), jnp.where(small, NEG_INF, rowmin), jnp.full((tq, 1), POS_INF, F32), nfin,
           jnp.zeros((tq, 1), F32), jnp.where(small, 1.0, 0.0))
    _, lo, hi, clo, chi, _ = lax.while_loop(bis_cond, bis_body, st0)
    budget = kf - chi
    need_ties = jnp.max((clo - chi) - budget) > 0.5

    def masked_logits(jb, kind, with_ties):
        s = s_sc[jb]
        if with_ties:
            cand = jnp.logical_and(s >= lo, s < hi)
            candf = jnp.where(cand, 1.0, 0.0)
            before = (row < col).astype(BF16)
            rank = tie_sc[...] + _dot(candf.astype(BF16), before)
            tie_sc[...] += jnp.sum(candf, axis=1, keepdims=True)
            sel = jnp.logical_or(s >= hi, jnp.logical_and(cand, rank < budget))
        else:
            sel = s >= lo
        if kind == "diag":
            sel = jnp.logical_and(sel, causal)
        kb = k_ref[key_rows(jb), :]
        for h in range(ATTN_HEADS):
            lg = _dot_nt(qm_sc[h], kb[:, LANES * (h // 2):LANES * (h // 2 + 1)])
            if kind == "far":
                lg = lg + relb_ref[REL_BUCKETS - 1, h]
            else:
                lg = lg + bias_sc[h, 0 if kind == "diag" else 1]
            yield h, jnp.where(sel, lg, NEG_INF)

    def sweep_max(jb, kind, with_ties):
        for h, lg in masked_logits(jb, kind, with_ties):
            part = m_sc[h]
            for x in fold(lg):
                part = jnp.maximum(part, x)
            m_sc[h] = part

    def sweep_acc(jb, kind, with_ties):
        vb = v_ref[key_rows(jb), :]
        for h, lg in masked_logits(jb, kind, with_ties):
            p = jnp.exp(lg - jnp.concatenate([m_sc[h]] * (tk // LANES), axis=1))
            part = l_sc[h]
            for x in fold(p):
                part = part + x
            l_sc[h] = part
            acc_sc[h] += _dot(p.astype(BF16), vb[:, LANES * (h // 2):LANES * (h // 2 + 1)])

    def sweep(fn, with_ties):
        tie_sc[...] = jnp.zeros_like(tie_sc)

        def far_body(jb, carry):
            fn(jb, "far", with_ties)
            return carry
        lax.fori_loop(0, jnp.maximum(i - 1, 0), far_body, 0)

        @pl.when(i >= 1)
        def _():
            fn(i - 1, "sub", with_ties)

        fn(i, "diag", with_ties)

    def run(with_ties):
        m_sc[...] = jnp.full(m_sc.shape, NEG_INF, F32)
        l_sc[...] = jnp.zeros_like(l_sc)
        acc_sc[...] = jnp.zeros_like(acc_sc)
        sweep(sweep_max, with_ties)
        for h in range(ATTN_HEADS):
            m = jnp.max(m_sc[h], axis=1, keepdims=True)
            m_sc[h] = jnp.broadcast_to(jnp.where(m == NEG_INF, 0.0, m), (tq, LANES))
        sweep(sweep_acc, with_ties)

    @pl.when(need_ties)
    def _():
        run(True)

    @pl.when(jnp.logical_not(need_ties))
    def _():
        run(False)

    for p in range(ATTN_HEADS // 2):
        even = acc_sc[2 * p] / jnp.sum(l_sc[2 * p], axis=1, keepdims=True)
        odd = acc_sc[2 * p + 1] / jnp.sum(l_sc[2 * p + 1], axis=1, keepdims=True)
        o_ref[:, LANES * p:LANES * (p + 1)] = jnp.where(lo_half, even, odd)


def _dsa_call(rel_bias, qi, wi, q, ki2, k, v, *, tq):
    s = q.shape[0]
    assert tq > REL_MAX_DIST and s % tq == 0
    nb = s // tq
    topk = min(TOPK_MAX, s // 4)
    kern = functools.partial(_dsa_kernel, tq=tq, topk=topk,
                             idx_scale=(IDX_DIM ** -0.5) * (IDX_HEADS ** -0.5), scale=ATTN_HEAD_DIM ** -0.5)
    whole = lambda width: pl.BlockSpec((s, width), lambda i: (0, 0), pipeline_mode=pl.Buffered(1))
    return pl.pallas_call(
        kern,
        grid=(nb,),
        in_specs=[pl.BlockSpec(memory_space=pltpu.SMEM),
                  pl.BlockSpec((tq, IDX_HEADS * IDX_DIM), lambda i: (i, 0)),
                  pl.BlockSpec((tq, LANES), lambda i: (i, 0)),
                  pl.BlockSpec((tq, ATTN_WIDTH), lambda i: (i, 0)),
                  whole(2 * IDX_DIM), whole(ATTN_WIDTH), whole(ATTN_WIDTH)],
        out_specs=pl.BlockSpec((tq, ATTN_WIDTH), lambda i: (i, 0)),
        out_shape=jax.ShapeDtypeStruct((s, ATTN_WIDTH), F32),
        scratch_shapes=[pltpu.VMEM((nb, tq, tq), F32),
                        pltpu.VMEM((ATTN_HEADS, 2, tq, tq), F32),
                        pltpu.VMEM((IDX_HEADS, tq, LANES), BF16),
                        pltpu.VMEM((ATTN_HEADS, tq, LANES), BF16),
                        pltpu.VMEM((ATTN_HEADS, tq, LANES), F32),
                        pltpu.VMEM((ATTN_HEADS, tq, LANES), F32),
                        pltpu.VMEM((ATTN_HEADS, tq, LANES), F32),
                        pltpu.VMEM((tq, 1), F32)],
        compiler_params=_cparams("arbitrary"),
        name="dsa_attention",
    )(rel_bias, qi, wi, q, ki2, k, v)


_HGRN_CHUNK = 128
_HGRN_BASE = SUBLANES


def _hgrn_kernel(h4_ref, lb_ref, ng_ref, o_ref, st_sc, *, layer, rows):
    c_len = _HGRN_CHUNK

    @pl.when(pl.program_id(0) == 0)
    def _():
        st_sc[...] = jnp.zeros_like(st_sc)

    lb_all = lb_ref[...]
    e = jnp.exp(lb_all - jnp.max(lb_all, axis=0, keepdims=True))
    sm = e / jnp.sum(e, axis=0, keepdims=True)
    lbs = jnp.sum(sm[0:layer + 1, :], axis=0, keepdims=True) - sm[0:1, :]

    r1 = lax.broadcasted_iota(I32, (c_len, 1), 0)
    rr = lax.broadcasted_iota(I32, (c_len, c_len), 0)
    cc = lax.broadcasted_iota(I32, (c_len, c_len), 1)

    def block_ref(a, blk, r):
        x = a.reshape(c_len // blk, blk, a.shape[-1])[:, r:r + 1, :]
        return jnp.broadcast_to(x, (c_len // blk, blk, a.shape[-1])).reshape(c_len, a.shape[-1])

    w = HGRN_WIDTH
    for c in range(rows // c_len):
        rs = slice(c * c_len, (c + 1) * c_len)
        hq = h4_ref[rs, 0:w]
        z = h4_ref[rs, w:2 * w]
        val = h4_ref[rs, 2 * w:3 * w]
        hg = h4_ref[rs, 3 * w:4 * w]
        key = (1.0 - lbs) * jax.nn.sigmoid(-z)
        g = jnp.log1p(-jnp.minimum(key, KEY_MAX))
        a_cum = g
        sh = 1
        while sh < c_len:
            a_cum = a_cum + jnp.where(r1 >= sh, pltpu.roll(a_cum, sh, 0), 0.0)
            sh *= 2
        qf = _silu(hq)
        outs = []
        for h in range(HGRN_HEADS):
            hs = slice(h * HGRN_DIM, (h + 1) * HGRN_DIM)
            ah, qh, kh, vh = a_cum[:, hs], qf[:, hs], key[:, hs], val[:, hs]
            vh16 = vh.astype(BF16)
            st = st_sc[h]
            o = _dot_nt((qh * jnp.exp(ah)).astype(BF16), st.astype(BF16))
            ref = block_ref(ah, _HGRN_BASE, _HGRN_BASE // 2 - 1)
            sc = _dot_nt((qh * jnp.exp(ah - ref)).astype(BF16), (kh * jnp.exp(ref - ah)).astype(BF16))
            same = (rr >> _log2(_HGRN_BASE)) == (cc >> _log2(_HGRN_BASE))
            scores = jnp.where(jnp.logical_and(same, cc <= rr), sc, 0.0)
            half = _HGRN_BASE
            while half < c_len:
                blk = 2 * half
                ref = block_ref(ah, blk, half - 1)
                second = (r1 & (blk - 1)) >= half
                ql = jnp.where(second, qh * jnp.exp(jnp.minimum(ah - ref, 0.0)), 0.0)
                kl = jnp.where(second, 0.0, kh * jnp.exp(jnp.minimum(ref - ah, 0.0)))
                sc = _dot_nt(ql.astype(BF16), kl.astype(BF16))
                scores = scores + jnp.where((rr >> _log2(blk)) == (cc >> _log2(blk)), sc, 0.0)
                half = blk
            o = o + _dot(scores.astype(BF16), vh16)
            a_last = ah[c_len - 1:c_len, :]
            kd = (kh * jnp.exp(a_last - ah)).astype(BF16)
            st_sc[h] = st * jnp.exp(a_last) + _dot_tn(vh16, kd)
            o = o * lax.rsqrt(jnp.mean(o * o, axis=-1, keepdims=True) + RMS_EPS)
            outs.append(o)
        o_ref[rs, :] = jnp.concatenate(outs, axis=1) * ng_ref[...] * _silu(hg)


def _hgrn_call(h4, lb, ng, *, layer, rows):
    s = h4.shape[0]
    kern = functools.partial(_hgrn_kernel, layer=layer, rows=rows)
    return pl.pallas_call(
        kern,
        grid=(s // rows,),
        in_specs=[pl.BlockSpec((rows, 4 * HGRN_WIDTH), lambda i: (i, 0)),
                  pl.BlockSpec(lb.shape, lambda i: (0, 0)),
                  pl.BlockSpec((1, HGRN_WIDTH), lambda i: (0, 0))],
        out_specs=pl.BlockSpec((rows, HGRN_WIDTH), lambda i: (i, 0)),
        out_shape=jax.ShapeDtypeStruct((s, HGRN_WIDTH), F32),
        scratch_shapes=[pltpu.VMEM((HGRN_HEADS, HGRN_DIM, HGRN_DIM), F32)],
        compiler_params=_cparams("arbitrary"),
        name="hgrn2",
    )(h4, lb, ng)


_POOL_HALO = 16


def _merge_kernel(x_ref, mod_ref, a_ref, ah_ref, yb_ref, yc_ref, g_ref, pw_ref, ps_ref, wb_ref, wo_ref,
                  lng_ref, lnb_ref, o_ref, *, tm, alpha):
    i = pl.program_id(0)
    d = x_ref.shape[1]
    _, _, gate = _mod_rows(mod_ref, 1)
    a = a_ref[...]
    halo = jnp.where(i == 0, 0.0, ah_ref[...])
    ext = jnp.concatenate([halo, a], axis=0)
    lane = lax.broadcasted_iota(I32, (tm, POOL_WIDTH), 1)
    grp = lane >> _log2(POOL_GROUP)
    pos1 = (i * tm + lax.broadcasted_iota(I32, (tm, POOL_WIDTH), 0) + 1).astype(F32)
    win = jnp.zeros((tm, POOL_WIDTH), F32)
    cnt = jnp.zeros((tm, POOL_WIDTH), F32)
    ssum, have = ext, 1
    for gi, wd in enumerate(POOL_WINDOWS):
        while have < wd:
            ssum = ssum + pltpu.roll(ssum, have, 0)
            have *= 2
        assert have == wd
        win = jnp.where(grp == gi, ssum[_POOL_HALO:, :], win)
        cnt = jnp.where(grp == gi, jnp.minimum(pos1, float(wd)), cnt)
    dlt = (win / cnt - a).astype(BF16)
    ya = _dot(dlt, pw_ref[...]) * ps_ref[...]
    za = _dot(ya.astype(BF16), wb_ref[0:POOL_WIDTH, :])
    zb = _dot(yb_ref[...].astype(BF16), wb_ref[POOL_WIDTH:POOL_WIDTH + ATTN_WIDTH, :])
    zc = _dot(yc_ref[...].astype(BF16), wb_ref[POOL_WIDTH + ATTN_WIDTH:, :])
    merged = (jax.nn.sigmoid(g_ref[:, 0:d]) * za + jax.nn.sigmoid(g_ref[:, d:2 * d]) * zb
              + jax.nn.sigmoid(g_ref[:, 2 * d:3 * d]) * zc)
    y = _dot(merged.astype(BF16), wo_ref[...])
    z = alpha * x_ref[...] + (1.0 + gate) * y
    o_ref[...] = _layer_norm(z, lng_ref[1:2, :], lnb_ref[1:2, :])


def _merge_call(x, mod, a, yb, yc, gates, pw_bd, ps, wb, wo, ln_g, ln_b, *, tm, alpha):
    s, d = x.shape
    kern = functools.partial(_merge_kernel, tm=tm, alpha=alpha)
    full = lambda arr: pl.BlockSpec(arr.shape, lambda i: (0, 0))
    hb = tm // _POOL_HALO
    return pl.pallas_call(
        kern,
        grid=(s // tm,),
        in_specs=[pl.BlockSpec((tm, d), lambda i: (i, 0)),
                  full(mod),
                  pl.BlockSpec((tm, POOL_WIDTH), lambda i: (i, 0)),
                  pl.BlockSpec((_POOL_HALO, POOL_WIDTH), lambda i: (jnp.maximum(i * hb - 1, 0), 0)),
                  pl.BlockSpec((tm, ATTN_WIDTH), lambda i: (i, 0)),
                  pl.BlockSpec((tm, HGRN_WIDTH), lambda i: (i, 0)),
                  pl.BlockSpec((tm, 3 * d), lambda i: (i, 0)),
                  full(pw_bd), full(ps), full(wb), full(wo), full(ln_g), full(ln_b)],
        out_specs=pl.BlockSpec((tm, d), lambda i: (i, 0)),
        out_shape=jax.ShapeDtypeStruct((s, d), F32),
        compiler_params=_cparams("arbitrary"),
        name="pool_merge",
    )(x, mod, a, a, yb, yc, gates, pw_bd, ps, wb, wo, ln_g, ln_b)


def _pool_blockdiag(pool_w):
    n = pool_w.shape[0]
    rows = [jnp.concatenate([pool_w[g] if g == j else jnp.zeros_like(pool_w[g]) for j in range(n)], axis=1)
            for g in range(n)]
    return jnp.concatenate(rows, axis=0)


def kernel(x, c, w_ada, b_ada, ln_g, ln_b, ffn_w_in, ffn_w_out, mix_w_in, pool_w, pool_scale, rel_bias,
           hgrn_lb, hgrn_norm_g, w_branch, w_out):
    batch, s, d = x.shape
    assert batch == 1, "kernels are written for a single sequence"
    depth = w_ada.shape[0]
    alpha = (2 * depth) ** 0.25
    tm_ffn = min(1024, s)
    tf = 256
    tm_proj = min(256, s)
    tq = min(256, s)
    tm_merge = min(256, s)
    hgrn_rows = min(256, s)

    mods = _ada_call(c, w_ada, b_ada)
    xs = x[0]
    for l in range(depth):
        mod = mods[l].reshape(3 * N_SUB, d)
        xs = _ffn_call(xs, mod, ffn_w_in[l, 0].astype(BF16), ffn_w_out[l, 0].astype(BF16), ln_g[l], ln_b[l],
                       sub=0, alpha=alpha, tm=tm_ffn, tf=tf)
        gates, a, h4, qi, q, k, v, ki2, wi = _proj_call(xs, mod, _proj_weight(mix_w_in[l]), tm=tm_proj)
        yb = _dsa_call(rel_bias, qi, wi, q, ki2, k, v, tq=tq)
        yc = _hgrn_call(h4, hgrn_lb, hgrn_norm_g[l].reshape(1, -1), layer=l, rows=hgrn_rows)
        xs = _merge_call(xs, mod, a, yb, yc, gates, _pool_blockdiag(pool_w[l]).astype(BF16),
                         pool_scale[l].reshape(1, -1), w_branch[l].astype(BF16), w_out[l].astype(BF16),
                         ln_g[l], ln_b[l], tm=tm_merge, alpha=alpha)
        xs = _ffn_call(xs, mod, ffn_w_in[l, 1].astype(BF16), ffn_w_out[l, 1].astype(BF16), ln_g[l], ln_b[l],
                       sub=2, alpha=alpha, tm=tm_ffn, tf=tf)
    return xs[None]
```

```python
import functools
import math

import jax
import jax.numpy as jnp
from jax import lax
from jax.experimental import pallas as pl
from jax.experimental.pallas import tpu as pltpu

F32 = jnp.float32
BF16 = jnp.bfloat16
I32 = jnp.int32
HIGHEST = lax.Precision.HIGHEST

POOL_WINDOWS = (2, 4, 8, 16)
POOL_GROUP = 64
POOL_WIDTH = POOL_GROUP * len(POOL_WINDOWS)
ATTN_HEADS = 4
ATTN_HEAD_DIM = 64
ATTN_WIDTH = ATTN_HEADS * ATTN_HEAD_DIM
IDX_HEADS = 8
IDX_DIM = 64
TOPK_MAX = 256
HGRN_HEADS = 4
HGRN_DIM = 128
HGRN_WIDTH = HGRN_HEADS * HGRN_DIM
KEY_MAX = 1.0 - 1e-6
REL_BUCKETS = 32
REL_MAX_DIST = 128
N_SUB = 3
LN_EPS = 1e-5
RMS_EPS = 1e-6

LANES = 128
SUBLANES = 8
VMEM_LIMIT_BYTES = 58 * 1024 * 1024

NEG_INF = float("-inf")
POS_INF = float("inf")


def _cparams(*sem):
    return pltpu.CompilerParams(dimension_semantics=sem, vmem_limit_bytes=VMEM_LIMIT_BYTES)


def _dot(a, b):
    return jnp.dot(a, b, preferred_element_type=F32)


def _dot_nt(a, b):
    return lax.dot_general(a, b, (((1,), (1,)), ((), ())), preferred_element_type=F32)


def _dot_tn(a, b):
    return lax.dot_general(a, b, (((0,), (0,)), ((), ())), preferred_element_type=F32)


def _log2(n):
    assert n & (n - 1) == 0
    return n.bit_length() - 1


def _silu(x):
    return x * jax.nn.sigmoid(x)


def _mod_rows(mod_ref, j):
    return mod_ref[3 * j:3 * j + 1, :], mod_ref[3 * j + 1:3 * j + 2, :], mod_ref[3 * j + 2:3 * j + 3, :]


def _layer_norm(z, g, b):
    mu = jnp.mean(z, axis=-1, keepdims=True)
    zc = z - mu
    var = jnp.mean(zc * zc, axis=-1, keepdims=True)
    return zc * lax.rsqrt(var + LN_EPS) * g + b


def _ada_kernel(c_ref, w_ref, b_ref, o_ref):
    c = c_ref[...]
    r = jnp.dot(_silu(c), w_ref[0], precision=HIGHEST, preferred_element_type=F32)
    o_ref[0] = r[0:1, :] + b_ref[0]


def _ada_call(c, w_ada, b_ada):
    depth, d, n = w_ada.shape
    tn = n // 8
    c8 = jnp.broadcast_to(c, (SUBLANES, d))
    return pl.pallas_call(
        _ada_kernel,
        grid=(depth, n // tn),
        in_specs=[pl.BlockSpec((SUBLANES, d), lambda l, j: (0, 0)),
                  pl.BlockSpec((1, d, tn), lambda l, j: (l, 0, j)),
                  pl.BlockSpec((1, 1, tn), lambda l, j: (l, 0, j))],
        out_specs=pl.BlockSpec((1, 1, tn), lambda l, j: (l, 0, j)),
        out_shape=jax.ShapeDtypeStruct((depth, 1, n), F32),
        compiler_params=_cparams("arbitrary", "arbitrary"),
        name="ada_mod",
    )(c8, w_ada, b_ada.reshape(depth, 1, n))


def _ffn_kernel(x_ref, mod_ref, wg_ref, wu_ref, wo_ref, lng_ref, lnb_ref, o_ref, h_sc, acc_sc,
                *, sub, n_f, alpha):
    f = pl.program_id(1)
    shift, scale, gate = _mod_rows(mod_ref, sub)

    @pl.when(f == 0)
    def _():
        h_sc[...] = (x_ref[...] * (1.0 + scale) + shift).astype(BF16)
        acc_sc[...] = jnp.zeros_like(acc_sc)

    h = h_sc[...]
    g = _dot(h, wg_ref[...])
    u = _dot(h, wu_ref[...])
    act = (_silu(g) * u).astype(BF16)
    acc_sc[...] += _dot(act, wo_ref[...])

    @pl.when(f == n_f - 1)
    def _():
        z = alpha * x_ref[...] + (0.5 * (1.0 + gate)) * acc_sc[...]
        o_ref[...] = _layer_norm(z, lng_ref[sub:sub + 1, :], lnb_ref[sub:sub + 1, :])


def _ffn_call(x, mod, w_in, w_out, ln_g, ln_b, *, sub, alpha, tm, tf):
    s, d = x.shape
    ff = w_out.shape[0]
    n_f = ff // tf
    kern = functools.partial(_ffn_kernel, sub=sub, n_f=n_f, alpha=alpha)
    return pl.pallas_call(
        kern,
        grid=(s // tm, n_f),
        in_specs=[pl.BlockSpec((tm, d), lambda i, f: (i, 0)),
                  pl.BlockSpec((3 * N_SUB, d), lambda i, f: (0, 0)),
                  pl.BlockSpec((d, tf), lambda i, f: (0, f)),
                  pl.BlockSpec((d, tf), lambda i, f: (0, f + n_f)),
                  pl.BlockSpec((tf, d), lambda i, f: (f, 0)),
                  pl.BlockSpec((N_SUB, d), lambda i, f: (0, 0)),
                  pl.BlockSpec((N_SUB, d), lambda i, f: (0, 0))],
        out_specs=pl.BlockSpec((tm, d), lambda i, f: (i, 0)),
        out_shape=jax.ShapeDtypeStruct((s, d), F32),
        scratch_shapes=[pltpu.VMEM((tm, d), BF16), pltpu.VMEM((tm, d), F32)],
        compiler_params=_cparams("arbitrary", "arbitrary"),
        name="ffn",
    )(x, mod, w_in, w_in, w_out, ln_g, ln_b)


_PROJ_OUTS = (("gates", 3 * 1024, F32), ("a", POOL_WIDTH, F32), ("h4", 4 * HGRN_WIDTH, F32),
              ("qi", IDX_HEADS * IDX_DIM, BF16), ("q", ATTN_WIDTH, BF16), ("k", ATTN_WIDTH, BF16),
              ("v", ATTN_WIDTH, BF16), ("ki2", 2 * IDX_DIM, BF16), ("wi", LANES, F32))
_PROJ_CHUNK = 1024


def _proj_kernel(x_ref, mod_ref, w_ref, *o_refs):
    shift, scale, _ = _mod_rows(mod_ref, 1)
    h = (x_ref[...] * (1.0 + scale) + shift).astype(BF16)
    c0 = 0
    for (_, width, dt), o_ref in zip(_PROJ_OUTS, o_refs):
        for cc in range(0, width, _PROJ_CHUNK):
            cw = min(_PROJ_CHUNK, width - cc)
            o_ref[:, cc:cc + cw] = _dot(h, w_ref[:, c0 + cc:c0 + cc + cw]).astype(dt)
        c0 += width


def _proj_weight(w):
    d = w.shape[0]
    widths = (POOL_WIDTH, ATTN_WIDTH, ATTN_WIDTH, ATTN_WIDTH, IDX_HEADS * IDX_DIM, IDX_DIM, IDX_HEADS,
              HGRN_WIDTH, HGRN_WIDTH, HGRN_WIDTH, HGRN_WIDTH, 3 * d)
    parts, c0 = [], 0
    for wd in widths:
        parts.append(w[:, c0:c0 + wd])
        c0 += wd
    a, q, k, v, qi, ki, wi, hq, hf, hi, hg, gates = parts
    wi_pad = jnp.pad(wi, ((0, 0), (0, LANES - IDX_HEADS)))
    return jnp.concatenate([gates, a, hq, hf, hi, hg, qi, q, k, v, ki, ki, wi_pad], axis=1).astype(BF16)


def _proj_call(x, mod, w, *, tm):
    s, d = x.shape
    n = w.shape[1]
    assert n == sum(wd for _, wd, _ in _PROJ_OUTS)
    return pl.pallas_call(
        _proj_kernel,
        grid=(s // tm,),
        in_specs=[pl.BlockSpec((tm, d), lambda i: (i, 0)),
                  pl.BlockSpec((3 * N_SUB, d), lambda i: (0, 0)),
                  pl.BlockSpec((d, n), lambda i: (0, 0))],
        out_specs=[pl.BlockSpec((tm, wd), lambda i: (i, 0)) for _, wd, _ in _PROJ_OUTS],
        out_shape=[jax.ShapeDtypeStruct((s, wd), dt) for _, wd, dt in _PROJ_OUTS],
        compiler_params=_cparams("arbitrary"),
        name="mix_proj",
    )(x, mod, w)


_BISECT_MAX_ITERS = 320
_F32_TINY = float(jnp.finfo(jnp.float32).tiny)
_KIND_FAR, _KIND_PAD, _N_KINDS = 2, 3, 4


def _t5_bucket(dist):
    max_exact = REL_BUCKETS // 2
    d32 = jnp.maximum(dist, 1).astype(F32)
    large = max_exact + (jnp.log(d32 / max_exact) / math.log(REL_MAX_DIST / max_exact)
                         * (REL_BUCKETS - max_exact)).astype(I32)
    large = jnp.minimum(large, REL_BUCKETS - 1)
    return jnp.where(dist < max_exact, dist, large)


def _dsa_kernel(relb_ref, qi_ref, wi_ref, q_ref, ki2_ref, k_ref, vt_ref, o_ref,
                s_sc, bias_sc, lg_sc, qim_sc, qm_sc, m_sc, l_sc, acc_sc, tie_sc,
                *, tq, topk, idx_scale, scale):
    tk = tq
    i = pl.program_id(0)
    krow = lax.broadcasted_iota(I32, (tk, tq), 0)
    qcol = lax.broadcasted_iota(I32, (tk, tq), 1)
    causal = krow <= qcol
    lane = lax.broadcasted_iota(I32, (tq, LANES), 1)
    lo_half = lane < ATTN_HEAD_DIM

    @pl.when(i == 0)
    def _():
        for which in range(2):
            bucket = _t5_bucket(jnp.maximum(qcol - krow + which * tq, 0))
            for h in range(ATTN_HEADS):
                b = jnp.zeros((tk, tq), F32)
                for bk in range(REL_BUCKETS):
                    b = jnp.where(bucket == bk, relb_ref[bk, h], b)
                bias_sc[h, which] = jnp.where(causal, b, NEG_INF) if which == 0 else b
        for h in range(ATTN_HEADS):
            bias_sc[h, _KIND_FAR] = jnp.full((tk, tq), relb_ref[REL_BUCKETS - 1, h], F32)
            bias_sc[h, _KIND_PAD] = jnp.full((tk, tq), NEG_INF, F32)

    qi = qi_ref[...]
    for h in range(IDX_HEADS):
        pair = qi[:, LANES * (h // 2):LANES * (h // 2 + 1)]
        qim_sc[h] = jnp.where(lo_half if h % 2 == 0 else ~lo_half, pair, jnp.zeros_like(pair))
    qs = (q_ref[...].astype(F32) * scale).astype(BF16)
    for h in range(ATTN_HEADS):
        pair = qs[:, LANES * (h // 2):LANES * (h // 2 + 1)]
        qm_sc[h] = jnp.where(lo_half if h % 2 == 0 else ~lo_half, pair, jnp.zeros_like(pair))

    wt = jnp.transpose(wi_ref[...])

    def key_rows(jb):
        return pl.ds(pl.multiple_of(jb * tk, tk), tk)

    def fold(x):
        return x.reshape(tk // SUBLANES, SUBLANES, tq)

    def index_scores(jb):
        kb = ki2_ref[key_rows(jb), :]
        acc = jnp.zeros((tk, tq), F32)
        for h in range(IDX_HEADS):
            acc = acc + wt[h:h + 1, :] * jnp.maximum(_dot_nt(kb, qim_sc[h]), 0.0)
        return acc * idx_scale

    def p1_body(jb, carry):
        rmin, rmax = carry
        sc = index_scores(jb)
        s_sc[jb] = sc
        return (jnp.minimum(rmin, jnp.min(fold(sc), axis=0)), jnp.maximum(rmax, jnp.max(fold(sc), axis=0)))

    rmin, rmax = lax.fori_loop(0, i, p1_body, (jnp.full((SUBLANES, tq), POS_INF, F32),
                                               jnp.full((SUBLANES, tq), NEG_INF, F32)))
    sc = index_scores(i)
    s_sc[i] = jnp.where(causal, sc, NEG_INF)

    @pl.when(i % 2 == 0)
    def _():
        s_sc[i + 1] = jnp.full((tk, tq), NEG_INF, F32)
    npair = (i + 2) // 2

    rmin = jnp.minimum(rmin, jnp.min(fold(jnp.where(causal, sc, POS_INF)), axis=0))
    rmax = jnp.maximum(rmax, jnp.max(fold(jnp.where(causal, sc, NEG_INF)), axis=0))
    rowmin = jnp.min(rmin, axis=0, keepdims=True)
    rowmax = jnp.max(rmax, axis=0, keepdims=True)

    kf = float(topk)
    nfin = (i * tq + lax.broadcasted_iota(I32, (1, tq), 1) + 1).astype(F32)
    small = nfin <= kf

    def count_ge(x):
        def body(j, cnt):
            for jb in (2 * j, 2 * j + 1):
                cnt = cnt + jnp.sum(fold(jnp.where(s_sc[jb] >= x, 1.0, 0.0)), axis=0)
            return cnt
        cnt = lax.fori_loop(0, npair, body, jnp.zeros((SUBLANES, tq), F32))
        return jnp.sum(cnt, axis=0, keepdims=True)

    def bis_cond(st):
        it, _, _, _, _, done = st
        return jnp.logical_and(it < _BISECT_MAX_ITERS, jnp.min(done) < 0.5)

    def bis_body(st):
        it, lo, hi, clo, chi, done = st
        mid = 0.5 * lo + 0.5 * hi
        mid = jnp.where(jnp.logical_and(lo == 0.0, hi > _F32_TINY), _F32_TINY, mid)
        mid = jnp.where(jnp.logical_and(lo < 0.0, hi > 0.0), 0.0, mid)
        mid = jnp.where(it == 0, rowmax, mid)
        stuck = jnp.logical_or(jnp.logical_or(mid <= lo, mid >= hi),
                               jnp.logical_and(lo == 0.0, hi <= _F32_TINY))
        c = count_ge(mid)
        active = jnp.logical_and(done < 0.5, jnp.logical_not(stuck))
        up = jnp.logical_and(active, c >= kf)
        dn = jnp.logical_and(active, c < kf)
        lo = jnp.where(up, mid, lo)
        clo = jnp.where(up, c, clo)
        hi = jnp.where(dn, mid, hi)
        chi = jnp.where(dn, c, chi)
        fin = jnp.logical_or(stuck, c == kf)
        done = jnp.where(fin, 1.0, done)
        return it + 1, lo, hi, clo, chi, done

    st0 = (jnp.int32(0), jnp.where(small, NEG_INF, rowmin), jnp.full((1, tq), POS_INF, F32), nfin,
           jnp.zeros((1, tq), F32), jnp.where(small, 1.0, 0.0))
    _, lo, hi, clo, chi, _ = lax.while_loop(bis_cond, bis_body, st0)
    budget = kf - chi
    need_ties = jnp.max((clo - chi) - budget) > 0.5

    pairs = [slice(LANES * (h // 2), LANES * (h // 2 + 1)) for h in range(ATTN_HEADS)]
    last_kv = k_ref.shape[0] // tk - 1

    def issue_qk(jb, slot):
        kb = k_ref[key_rows(jnp.minimum(jb, last_kv)), :]
        for h in range(ATTN_HEADS):
            lg_sc[slot, h] = _dot_nt(kb[:, pairs[h]], qm_sc[h])

    def attend(jb, slot, with_ties):
        s = s_sc[jb]
        if with_ties:
            cand = jnp.logical_and(s >= lo, s < hi)
            candf = jnp.where(cand, 1.0, 0.0)
            before = (qcol < krow).astype(BF16)
            rank = tie_sc[...] + _dot(before, candf.astype(BF16))
            tie_sc[...] += jnp.sum(candf, axis=0, keepdims=True)
            sel = jnp.logical_or(s >= hi, jnp.logical_and(cand, rank < budget))
        else:
            sel = s >= lo
        kind = jnp.where(jb > i, _KIND_PAD, jnp.minimum(i - jb, _KIND_FAR))
        probs, alphas = [], []
        for h in range(ATTN_HEADS):
            lg = jnp.where(sel, lg_sc[slot, h] + bias_sc[h, kind], NEG_INF)
            m_old = m_sc[h]
            m_new = jnp.maximum(m_old, jnp.max(lg, axis=0, keepdims=True))
            m_safe = jnp.where(m_new == NEG_INF, 0.0, m_new)
            alpha = jnp.exp(m_old - m_safe)
            p = jnp.exp(lg - m_safe)
            l_sc[h] = alpha * l_sc[h] + jnp.sum(p, axis=0, keepdims=True)
            m_sc[h] = m_new
            probs.append(p.astype(BF16))
            alphas.append(alpha)
        vb = vt_ref[jnp.minimum(jb, last_kv)]
        for h in range(ATTN_HEADS):
            acc_sc[h] = alphas[h] * acc_sc[h] + _dot(vb[pairs[h], :], probs[h])

    def run(with_ties):
        m_sc[...] = jnp.full(m_sc.shape, NEG_INF, F32)
        l_sc[...] = jnp.zeros_like(l_sc)
        acc_sc[...] = jnp.zeros_like(acc_sc)
        tie_sc[...] = jnp.zeros_like(tie_sc)
        issue_qk(0, 0)

        def pair_body(j, carry):
            issue_qk(2 * j + 1, 1)
            attend(2 * j, 0, with_ties)
            issue_qk(2 * j + 2, 0)
            attend(2 * j + 1, 1, with_ties)
            return carry
        lax.fori_loop(0, npair, pair_body, 0)

    @pl.when(need_ties)
    def _():
        run(True)

    @pl.when(jnp.logical_not(need_ties))
    def _():
        run(False)

    first_head = lax.broadcasted_iota(I32, (LANES, tq), 0) < ATTN_HEAD_DIM
    for p in range(ATTN_HEADS // 2):
        pair_t = jnp.where(first_head, acc_sc[2 * p] / l_sc[2 * p], acc_sc[2 * p + 1] / l_sc[2 * p + 1])
        o_ref[:, LANES * p:LANES * (p + 1)] = jnp.transpose(pair_t)


def _dsa_call(rel_bias, qi, wi, q, ki2, k, v, *, tq):
    s = q.shape[0]
    assert tq > REL_MAX_DIST and s % tq == 0
    nb = s // tq
    topk = min(TOPK_MAX, s // 4)
    kern = functools.partial(_dsa_kernel, tq=tq, topk=topk,
                             idx_scale=(IDX_DIM ** -0.5) * (IDX_HEADS ** -0.5), scale=ATTN_HEAD_DIM ** -0.5)
    whole = lambda width: pl.BlockSpec((s, width), lambda i: (0, 0), pipeline_mode=pl.Buffered(1))
    vt = jnp.transpose(v.reshape(nb, tq, ATTN_WIDTH), (0, 2, 1))
    return pl.pallas_call(
        kern,
        grid=(nb,),
        in_specs=[pl.BlockSpec(memory_space=pltpu.SMEM),
                  pl.BlockSpec((tq, IDX_HEADS * IDX_DIM), lambda i: (i, 0)),
                  pl.BlockSpec((tq, LANES), lambda i: (i, 0)),
                  pl.BlockSpec((tq, ATTN_WIDTH), lambda i: (i, 0)),
                  whole(2 * IDX_DIM), whole(ATTN_WIDTH),
                  pl.BlockSpec((nb, ATTN_WIDTH, tq), lambda i: (0, 0, 0), pipeline_mode=pl.Buffered(1))],
        out_specs=pl.BlockSpec((tq, ATTN_WIDTH), lambda i: (i, 0)),
        out_shape=jax.ShapeDtypeStruct((s, ATTN_WIDTH), F32),
        scratch_shapes=[pltpu.VMEM((nb + nb % 2, tq, tq), F32),
                        pltpu.VMEM((ATTN_HEADS, _N_KINDS, tq, tq), F32),
                        pltpu.VMEM((2, ATTN_HEADS, tq, tq), F32),
                        pltpu.VMEM((IDX_HEADS, tq, LANES), BF16),
                        pltpu.VMEM((ATTN_HEADS, tq, LANES), BF16),
                        pltpu.VMEM((ATTN_HEADS, 1, tq), F32),
                        pltpu.VMEM((ATTN_HEADS, 1, tq), F32),
                        pltpu.VMEM((ATTN_HEADS, LANES, tq), F32),
                        pltpu.VMEM((1, tq), F32)],
        compiler_params=_cparams("arbitrary"),
        name="dsa_attention",
    )(rel_bias, qi, wi, q, ki2, k, vt)


_HGRN_CHUNK = 128
_HGRN_BASE = SUBLANES


def _hgrn_kernel(h4_ref, lb_ref, ng_ref, o_ref, st_sc, *, layer, rows):
    c_len = _HGRN_CHUNK

    @pl.when(pl.program_id(0) == 0)
    def _():
        st_sc[...] = jnp.zeros_like(st_sc)

    lb_all = lb_ref[...]
    e = jnp.exp(lb_all - jnp.max(lb_all, axis=0, keepdims=True))
    sm = e / jnp.sum(e, axis=0, keepdims=True)
    lbs = jnp.sum(sm[0:layer + 1, :], axis=0, keepdims=True) - sm[0:1, :]

    r1 = lax.broadcasted_iota(I32, (c_len, 1), 0)
    rr = lax.broadcasted_iota(I32, (c_len, c_len), 0)
    cc = lax.broadcasted_iota(I32, (c_len, c_len), 1)

    def block_ref(a, blk, r):
        x = a.reshape(c_len // blk, blk, a.shape[-1])[:, r:r + 1, :]
        return jnp.broadcast_to(x, (c_len // blk, blk, a.shape[-1])).reshape(c_len, a.shape[-1])

    w = HGRN_WIDTH
    for c in range(rows // c_len):
        rs = slice(c * c_len, (c + 1) * c_len)
        hq = h4_ref[rs, 0:w]
        z = h4_ref[rs, w:2 * w]
        val = h4_ref[rs, 2 * w:3 * w]
        hg = h4_ref[rs, 3 * w:4 * w]
        key = (1.0 - lbs) * jax.nn.sigmoid(-z)
        g = jnp.log1p(-jnp.minimum(key, KEY_MAX))
        a_cum = g
        sh = 1
        while sh < c_len:
            a_cum = a_cum + jnp.where(r1 >= sh, pltpu.roll(a_cum, sh, 0), 0.0)
            sh *= 2
        qf = _silu(hq)
        outs = []
        for h in range(HGRN_HEADS):
            hs = slice(h * HGRN_DIM, (h + 1) * HGRN_DIM)
            ah, qh, kh, vh = a_cum[:, hs], qf[:, hs], key[:, hs], val[:, hs]
            vh16 = vh.astype(BF16)
            st = st_sc[h]
            o = _dot_nt((qh * jnp.exp(ah)).astype(BF16), st.astype(BF16))
            ref = block_ref(ah, _HGRN_BASE, _HGRN_BASE // 2 - 1)
            sc = _dot_nt((qh * jnp.exp(ah - ref)).astype(BF16), (kh * jnp.exp(ref - ah)).astype(BF16))
            same = (rr >> _log2(_HGRN_BASE)) == (cc >> _log2(_HGRN_BASE))
            scores = jnp.where(jnp.logical_and(same, cc <= rr), sc, 0.0)
            half = _HGRN_BASE
            while half < c_len:
                blk = 2 * half
                ref = block_ref(ah, blk, half - 1)
                second = (r1 & (blk - 1)) >= half
                ql = jnp.where(second, qh * jnp.exp(jnp.minimum(ah - ref, 0.0)), 0.0)
                kl = jnp.where(second, 0.0, kh * jnp.exp(jnp.minimum(ref - ah, 0.0)))
                sc = _dot_nt(ql.astype(BF16), kl.astype(BF16))
                scores = scores + jnp.where((rr >> _log2(blk)) == (cc >> _log2(blk)), sc, 0.0)
                half = blk
            o = o + _dot(scores.astype(BF16), vh16)
            a_last = ah[c_len - 1:c_len, :]
            kd = (kh * jnp.exp(a_last - ah)).astype(BF16)
            st_sc[h] = st * jnp.exp(a_last) + _dot_tn(vh16, kd)
            o = o * lax.rsqrt(jnp.mean(o * o, axis=-1, keepdims=True) + RMS_EPS)
            outs.append(o)
        o_ref[rs, :] = jnp.concatenate(outs, axis=1) * ng_ref[...] * _silu(hg)


def _hgrn_call(h4, lb, ng, *, layer, rows):
    s = h4.shape[0]
    kern = functools.partial(_hgrn_kernel, layer=layer, rows=rows)
    return pl.pallas_call(
        kern,
        grid=(s // rows,),
        in_specs=[pl.BlockSpec((rows, 4 * HGRN_WIDTH), lambda i: (i, 0)),
                  pl.BlockSpec(lb.shape, lambda i: (0, 0)),
                  pl.BlockSpec((1, HGRN_WIDTH), lambda i: (0, 0))],
        out_specs=pl.BlockSpec((rows, HGRN_WIDTH), lambda i: (i, 0)),
        out_shape=jax.ShapeDtypeStruct((s, HGRN_WIDTH), F32),
        scratch_shapes=[pltpu.VMEM((HGRN_HEADS, HGRN_DIM, HGRN_DIM), F32)],
        compiler_params=_cparams("arbitrary"),
        name="hgrn2",
    )(h4, lb, ng)


_POOL_HALO = 16


def _merge_kernel(x_ref, mod_ref, a_ref, ah_ref, yb_ref, yc_ref, g_ref, pw_ref, ps_ref, wb_ref, wo_ref,
                  lng_ref, lnb_ref, o_ref, *, tm, alpha):
    i = pl.program_id(0)
    d = x_ref.shape[1]
    _, _, gate = _mod_rows(mod_ref, 1)
    a = a_ref[...]
    halo = jnp.where(i == 0, 0.0, ah_ref[...])
    ext = jnp.concatenate([halo, a], axis=0)
    lane = lax.broadcasted_iota(I32, (tm, POOL_WIDTH), 1)
    grp = lane >> _log2(POOL_GROUP)
    pos1 = (i * tm + lax.broadcasted_iota(I32, (tm, POOL_WIDTH), 0) + 1).astype(F32)
    win = jnp.zeros((tm, POOL_WIDTH), F32)
    cnt = jnp.zeros((tm, POOL_WIDTH), F32)
    ssum, have = ext, 1
    for gi, wd in enumerate(POOL_WINDOWS):
        while have < wd:
            ssum = ssum + pltpu.roll(ssum, have, 0)
            have *= 2
        assert have == wd
        win = jnp.where(grp == gi, ssum[_POOL_HALO:, :], win)
        cnt = jnp.where(grp == gi, jnp.minimum(pos1, float(wd)), cnt)
    dlt = (win / cnt - a).astype(BF16)
    ya = _dot(dlt, pw_ref[...]) * ps_ref[...]
    za = _dot(ya.astype(BF16), wb_ref[0:POOL_WIDTH, :])
    zb = _dot(yb_ref[...].astype(BF16), wb_ref[POOL_WIDTH:POOL_WIDTH + ATTN_WIDTH, :])
    zc = _dot(yc_ref[...].astype(BF16), wb_ref[POOL_WIDTH + ATTN_WIDTH:, :])
    merged = (jax.nn.sigmoid(g_ref[:, 0:d]) * za + jax.nn.sigmoid(g_ref[:, d:2 * d]) * zb
              + jax.nn.sigmoid(g_ref[:, 2 * d:3 * d]) * zc)
    y = _dot(merged.astype(BF16), wo_ref[...])
    z = alpha * x_ref[...] + (1.0 + gate) * y
    o_ref[...] = _layer_norm(z, lng_ref[1:2, :], lnb_ref[1:2, :])


def _merge_call(x, mod, a, yb, yc, gates, pw_bd, ps, wb, wo, ln_g, ln_b, *, tm, alpha):
    s, d = x.shape
    kern = functools.partial(_merge_kernel, tm=tm, alpha=alpha)
    full = lambda arr: pl.BlockSpec(arr.shape, lambda i: (0, 0))
    hb = tm // _POOL_HALO
    return pl.pallas_call(
        kern,
        grid=(s // tm,),
        in_specs=[pl.BlockSpec((tm, d), lambda i: (i, 0)),
                  full(mod),
                  pl.BlockSpec((tm, POOL_WIDTH), lambda i: (i, 0)),
                  pl.BlockSpec((_POOL_HALO, POOL_WIDTH), lambda i: (jnp.maximum(i * hb - 1, 0), 0)),
                  pl.BlockSpec((tm, ATTN_WIDTH), lambda i: (i, 0)),
                  pl.BlockSpec((tm, HGRN_WIDTH), lambda i: (i, 0)),
                  pl.BlockSpec((tm, 3 * d), lambda i: (i, 0)),
                  full(pw_bd), full(ps), full(wb), full(wo), full(ln_g), full(ln_b)],
        out_specs=pl.BlockSpec((tm, d), lambda i: (i, 0)),
        out_shape=jax.ShapeDtypeStruct((s, d), F32),
        compiler_params=_cparams("arbitrary"),
        name="pool_merge",
    )(x, mod, a, a, yb, yc, gates, pw_bd, ps, wb, wo, ln_g, ln_b)


def _pool_blockdiag(pool_w):
    n = pool_w.shape[0]
    rows = [jnp.concatenate([pool_w[g] if g == j else jnp.zeros_like(pool_w[g]) for j in range(n)], axis=1)
            for g in range(n)]
    return jnp.concatenate(rows, axis=0)


def kernel(x, c, w_ada, b_ada, ln_g, ln_b, ffn_w_in, ffn_w_out, mix_w_in, pool_w, pool_scale, rel_bias,
           hgrn_lb, hgrn_norm_g, w_branch, w_out):
    batch, s, d = x.shape
    assert batch == 1, "kernels are written for a single sequence"
    depth = w_ada.shape[0]
    alpha = (2 * depth) ** 0.25
    tm_ffn = min(1024, s)
    tf = 256
    tm_proj = min(256, s)
    tq = min(256, s)
    tm_merge = min(256, s)
    hgrn_rows = min(256, s)

    mods = _ada_call(c, w_ada, b_ada)
    xs = x[0]
    for l in range(depth):
        mod = mods[l].reshape(3 * N_SUB, d)
        xs = _ffn_call(xs, mod, ffn_w_in[l, 0].astype(BF16), ffn_w_out[l, 0].astype(BF16), ln_g[l], ln_b[l],
                       sub=0, alpha=alpha, tm=tm_ffn, tf=tf)
        gates, a, h4, qi, q, k, v, ki2, wi = _proj_call(xs, mod, _proj_weight(mix_w_in[l]), tm=tm_proj)
        yb = _dsa_call(rel_bias, qi, wi, q, ki2, k, v, tq=tq)
        yc = _hgrn_call(h4, hgrn_lb, hgrn_norm_g[l].reshape(1, -1), layer=l, rows=hgrn_rows)
        xs = _merge_call(xs, mod, a, yb, yc, gates, _pool_blockdiag(pool_w[l]).astype(BF16),
                         pool_scale[l].reshape(1, -1), w_branch[l].astype(BF16), w_out[l].astype(BF16),
                         ln_g[l], ln_b[l], tm=tm_merge, alpha=alpha)
        xs = _ffn_call(xs, mod, ffn_w_in[l, 1].astype(BF16), ffn_w_out[l, 1].astype(BF16), ln_g[l], ln_b[l],
                       sub=2, alpha=alpha, tm=tm_ffn, tf=tf)
    return xs[None]
```

```python
import functools
import math

import jax
import jax.numpy as jnp
from jax import lax
from jax.experimental import pallas as pl
from jax.experimental.pallas import tpu as pltpu

F32 = jnp.float32
BF16 = jnp.bfloat16
I32 = jnp.int32
HIGHEST = lax.Precision.HIGHEST

POOL_WINDOWS = (2, 4, 8, 16)
POOL_GROUP = 64
POOL_WIDTH = POOL_GROUP * len(POOL_WINDOWS)
ATTN_HEADS = 4
ATTN_HEAD_DIM = 64
ATTN_WIDTH = ATTN_HEADS * ATTN_HEAD_DIM
IDX_HEADS = 8
IDX_DIM = 64
TOPK_MAX = 256
HGRN_HEADS = 4
HGRN_DIM = 128
HGRN_WIDTH = HGRN_HEADS * HGRN_DIM
KEY_MAX = 1.0 - 1e-6
REL_BUCKETS = 32
REL_MAX_DIST = 128
N_SUB = 3
LN_EPS = 1e-5
RMS_EPS = 1e-6

LANES = 128
SUBLANES = 8
VMEM_LIMIT_BYTES = 58 * 1024 * 1024

NEG_INF = float("-inf")
POS_INF = float("inf")


def _cparams(*sem):
    return pltpu.CompilerParams(dimension_semantics=sem, vmem_limit_bytes=VMEM_LIMIT_BYTES)


def _dot(a, b):
    return jnp.dot(a, b, preferred_element_type=F32)


def _dot_nt(a, b):
    return lax.dot_general(a, b, (((1,), (1,)), ((), ())), preferred_element_type=F32)


def _dot_tn(a, b):
    return lax.dot_general(a, b, (((0,), (0,)), ((), ())), preferred_element_type=F32)


def _log2(n):
    assert n & (n - 1) == 0
    return n.bit_length() - 1


def _silu(x):
    return x * jax.nn.sigmoid(x)


def _mod_rows(mod_ref, j):
    return mod_ref[3 * j:3 * j + 1, :], mod_ref[3 * j + 1:3 * j + 2, :], mod_ref[3 * j + 2:3 * j + 3, :]


def _layer_norm(z, g, b):
    mu = jnp.mean(z, axis=-1, keepdims=True)
    zc = z - mu
    var = jnp.mean(zc * zc, axis=-1, keepdims=True)
    return zc * lax.rsqrt(var + LN_EPS) * g + b


def _ada_kernel(c_ref, w_ref, b_ref, o_ref):
    c = c_ref[...]
    r = jnp.dot(_silu(c), w_ref[0], precision=HIGHEST, preferred_element_type=F32)
    o_ref[0] = r[0:1, :] + b_ref[0]


def _ada_call(c, w_ada, b_ada):
    depth, d, n = w_ada.shape
    tn = n // 8
    c8 = jnp.broadcast_to(c, (SUBLANES, d))
    return pl.pallas_call(
        _ada_kernel,
        grid=(depth, n // tn),
        in_specs=[pl.BlockSpec((SUBLANES, d), lambda l, j: (0, 0)),
                  pl.BlockSpec((1, d, tn), lambda l, j: (l, 0, j)),
                  pl.BlockSpec((1, 1, tn), lambda l, j: (l, 0, j))],
        out_specs=pl.BlockSpec((1, 1, tn), lambda l, j: (l, 0, j)),
        out_shape=jax.ShapeDtypeStruct((depth, 1, n), F32),
        compiler_params=_cparams("arbitrary", "arbitrary"),
        name="ada_mod",
    )(c8, w_ada, b_ada.reshape(depth, 1, n))


def _ffn_kernel(x_ref, mod_ref, wi_ref, wo_ref, lng_ref, lnb_ref, o_ref, *, sub, tf, alpha):
    shift, scale, gate = _mod_rows(mod_ref, sub)
    ff = wo_ref.shape[0]
    x = x_ref[...]
    h = (x * (1.0 + scale) + shift).astype(BF16)
    y = jnp.zeros(x.shape, F32)
    for c0 in range(0, ff, tf):
        g = _dot(h, wi_ref[:, c0:c0 + tf])
        u = _dot(h, wi_ref[:, ff + c0:ff + c0 + tf])
        y = y + _dot((_silu(g) * u).astype(BF16), wo_ref[c0:c0 + tf, :])
    z = alpha * x + (0.5 * (1.0 + gate)) * y
    o_ref[...] = _layer_norm(z, lng_ref[sub:sub + 1, :], lnb_ref[sub:sub + 1, :])


def _ffn_call(x, mod, w_in, w_out, ln_g, ln_b, *, sub, alpha, tm, tf):
    s, d = x.shape
    ff = w_out.shape[0]
    assert ff % tf == 0
    kern = functools.partial(_ffn_kernel, sub=sub, tf=tf, alpha=alpha)
    resident = lambda arr: pl.BlockSpec(arr.shape, lambda i: (0, 0), pipeline_mode=pl.Buffered(1))
    return pl.pallas_call(
        kern,
        grid=(s // tm,),
        in_specs=[pl.BlockSpec((tm, d), lambda i: (i, 0)),
                  pl.BlockSpec((3 * N_SUB, d), lambda i: (0, 0)),
                  resident(w_in), resident(w_out),
                  pl.BlockSpec((N_SUB, d), lambda i: (0, 0)),
                  pl.BlockSpec((N_SUB, d), lambda i: (0, 0))],
        out_specs=pl.BlockSpec((tm, d), lambda i: (i, 0)),
        out_shape=jax.ShapeDtypeStruct((s, d), F32),
        compiler_params=_cparams("arbitrary"),
        name="ffn",
    )(x, mod, w_in, w_out, ln_g, ln_b)


_PROJ_OUTS = (("gates", 3 * 1024, F32), ("a", POOL_WIDTH, F32), ("h4", 4 * HGRN_WIDTH, F32),
              ("qi", IDX_HEADS * IDX_DIM, BF16), ("q", ATTN_WIDTH, BF16), ("k", ATTN_WIDTH, BF16),
              ("v", ATTN_WIDTH, BF16), ("ki2", 2 * IDX_DIM, BF16), ("wi", LANES, F32))
_PROJ_CHUNK = 1024


def _proj_kernel(x_ref, mod_ref, w_ref, *o_refs):
    shift, scale, _ = _mod_rows(mod_ref, 1)
    h = (x_ref[...] * (1.0 + scale) + shift).astype(BF16)
    c0 = 0
    for (_, width, dt), o_ref in zip(_PROJ_OUTS, o_refs):
        for cc in range(0, width, _PROJ_CHUNK):
            cw = min(_PROJ_CHUNK, width - cc)
            o_ref[:, cc:cc + cw] = _dot(h, w_ref[:, c0 + cc:c0 + cc + cw]).astype(dt)
        c0 += width


def _proj_weight(w):
    d = w.shape[0]
    widths = (POOL_WIDTH, ATTN_WIDTH, ATTN_WIDTH, ATTN_WIDTH, IDX_HEADS * IDX_DIM, IDX_DIM, IDX_HEADS,
              HGRN_WIDTH, HGRN_WIDTH, HGRN_WIDTH, HGRN_WIDTH, 3 * d)
    parts, c0 = [], 0
    for wd in widths:
        parts.append(w[:, c0:c0 + wd])
        c0 += wd
    a, q, k, v, qi, ki, wi, hq, hf, hi, hg, gates = parts
    wi_pad = jnp.pad(wi, ((0, 0), (0, LANES - IDX_HEADS)))
    q = q * (ATTN_HEAD_DIM ** -0.5 * _LOG2E)
    return jnp.concatenate([gates, a, hq, hf, hi, hg, qi, q, k, v, ki, ki, wi_pad], axis=1).astype(BF16)


def _proj_call(x, mod, w, *, tm):
    s, d = x.shape
    n = w.shape[1]
    assert n == sum(wd for _, wd, _ in _PROJ_OUTS)
    return pl.pallas_call(
        _proj_kernel,
        grid=(s // tm,),
        in_specs=[pl.BlockSpec((tm, d), lambda i: (i, 0)),
                  pl.BlockSpec((3 * N_SUB, d), lambda i: (0, 0)),
                  pl.BlockSpec((d, n), lambda i: (0, 0))],
        out_specs=[pl.BlockSpec((tm, wd), lambda i: (i, 0)) for _, wd, _ in _PROJ_OUTS],
        out_shape=[jax.ShapeDtypeStruct((s, wd), dt) for _, wd, dt in _PROJ_OUTS],
        compiler_params=_cparams("arbitrary"),
        name="mix_proj",
    )(x, mod, w)


_BISECT_MAX_ITERS = 320
_F32_TINY = float(jnp.finfo(jnp.float32).tiny)
_KIND_FAR, _KIND_PAD, _N_KINDS = 2, 3, 4
_SMASK_FULL, _SMASK_DIAG, _SMASK_PAD = 0, 1, 2
_LOG2E = math.log2(math.e)


def _t5_bucket(dist):
    max_exact = REL_BUCKETS // 2
    d32 = jnp.maximum(dist, 1).astype(F32)
    large = max_exact + (jnp.log(d32 / max_exact) / math.log(REL_MAX_DIST / max_exact)
                         * (REL_BUCKETS - max_exact)).astype(I32)
    large = jnp.minimum(large, REL_BUCKETS - 1)
    return jnp.where(dist < max_exact, dist, large)


def _dsa_kernel(relb_ref, qi_ref, wi_ref, q_ref, ki2_ref, k_ref, vt_ref, o_ref,
                s_sc, bias_sc, smask_sc, lg_sc, qim_sc, qm_sc, m_sc, l_sc, acc_sc, tie_sc,
                *, tq, topk, idx_scale):
    tk = tq
    i = pl.program_id(0)
    krow = lax.broadcasted_iota(I32, (tk, tq), 0)
    qcol = lax.broadcasted_iota(I32, (tk, tq), 1)
    causal = krow <= qcol
    lane = lax.broadcasted_iota(I32, (tq, LANES), 1)
    lo_half = lane < ATTN_HEAD_DIM

    @pl.when(i == 0)
    def _():
        for which in range(2):
            bucket = _t5_bucket(jnp.maximum(qcol - krow + which * tq, 0))
            for h in range(ATTN_HEADS):
                b = jnp.zeros((tk, tq), F32)
                for bk in range(REL_BUCKETS):
                    b = jnp.where(bucket == bk, relb_ref[bk, h] * _LOG2E, b)
                bias_sc[h, which] = jnp.where(causal, b, NEG_INF) if which == 0 else b
        for h in range(ATTN_HEADS):
            bias_sc[h, _KIND_FAR] = jnp.full((tk, tq), relb_ref[REL_BUCKETS - 1, h] * _LOG2E, F32)
            bias_sc[h, _KIND_PAD] = jnp.full((tk, tq), NEG_INF, F32)
        smask_sc[_SMASK_FULL] = jnp.zeros((tk, tq), F32)
        smask_sc[_SMASK_DIAG] = jnp.where(causal, 0.0, NEG_INF)
        smask_sc[_SMASK_PAD] = jnp.full((tk, tq), NEG_INF, F32)

    qi = qi_ref[...]
    for h in range(IDX_HEADS):
        pair = qi[:, LANES * (h // 2):LANES * (h // 2 + 1)]
        qim_sc[h] = jnp.where(lo_half if h % 2 == 0 else ~lo_half, pair, jnp.zeros_like(pair))
    qs = q_ref[...]
    for h in range(ATTN_HEADS):
        pair = qs[:, LANES * (h // 2):LANES * (h // 2 + 1)]
        qm_sc[h] = jnp.where(lo_half if h % 2 == 0 else ~lo_half, pair, jnp.zeros_like(pair))

    wt = jnp.transpose(wi_ref[...])

    def key_rows(jb):
        return pl.ds(pl.multiple_of(jb * tk, tk), tk)

    def fold(x):
        return x.reshape(tk // SUBLANES, SUBLANES, tq)

    def index_scores(jb):
        kb = ki2_ref[key_rows(jb), :]
        acc = jnp.zeros((tk, tq), F32)
        for h in range(IDX_HEADS):
            acc = acc + wt[h:h + 1, :] * jnp.maximum(_dot_nt(kb, qim_sc[h]), 0.0)
        return acc * idx_scale

    npair = (i + 2) // 2
    last_kv = k_ref.shape[0] // tk - 1

    def p1_pair(j, carry):
        rmin, rmax = carry
        for jb in (2 * j, 2 * j + 1):
            raw = index_scores(jnp.minimum(jb, last_kv))
            kind = jnp.where(jb > i, _SMASK_PAD, jnp.where(jb == i, _SMASK_DIAG, _SMASK_FULL))
            s_sc[jb] = raw + smask_sc[kind]
            rmin = jnp.minimum(rmin, jnp.min(fold(raw), axis=0))
            rmax = jnp.maximum(rmax, jnp.max(fold(raw), axis=0))
        return rmin, rmax

    rmin, rmax = lax.fori_loop(0, npair, p1_pair, (jnp.full((SUBLANES, tq), POS_INF, F32),
                                                   jnp.full((SUBLANES, tq), NEG_INF, F32)))
    rowmin = jnp.min(rmin, axis=0, keepdims=True)
    rowmax = jnp.max(rmax, axis=0, keepdims=True)

    kf = float(topk)
    nfin = (i * tq + lax.broadcasted_iota(I32, (1, tq), 1) + 1).astype(F32)
    small = nfin <= kf

    def count_ge(x):
        def body(j, cnt):
            for jb in (2 * j, 2 * j + 1):
                cnt = cnt + jnp.sum(fold(jnp.where(s_sc[jb] >= x, 1.0, 0.0)), axis=0)
            return cnt
        cnt = lax.fori_loop(0, npair, body, jnp.zeros((SUBLANES, tq), F32))
        return jnp.sum(cnt, axis=0, keepdims=True)

    def bis_cond(st):
        it, _, _, _, _, done = st
        return jnp.logical_and(it < _BISECT_MAX_ITERS, jnp.min(done) < 0.5)

    def bis_body(st):
        it, lo, hi, clo, chi, done = st
        mid = 0.5 * lo + 0.5 * hi
        mid = jnp.where(jnp.logical_and(lo == 0.0, hi > _F32_TINY), _F32_TINY, mid)
        mid = jnp.where(jnp.logical_and(lo < 0.0, hi > 0.0), 0.0, mid)
        mid = jnp.where(it == 0, rowmax, mid)
        stuck = jnp.logical_or(jnp.logical_or(mid <= lo, mid >= hi),
                               jnp.logical_and(lo == 0.0, hi <= _F32_TINY))
        c = count_ge(mid)
        active = jnp.logical_and(done < 0.5, jnp.logical_not(stuck))
        up = jnp.logical_and(active, c >= kf)
        dn = jnp.logical_and(active, c < kf)
        lo = jnp.where(up, mid, lo)
        clo = jnp.where(up, c, clo)
        hi = jnp.where(dn, mid, hi)
        chi = jnp.where(dn, c, chi)
        fin = jnp.logical_or(stuck, c == kf)
        done = jnp.where(fin, 1.0, done)
        return it + 1, lo, hi, clo, chi, done

    st0 = (jnp.int32(0), jnp.where(small, NEG_INF, rowmin), jnp.full((1, tq), POS_INF, F32), nfin,
           jnp.zeros((1, tq), F32), jnp.where(small, 1.0, 0.0))
    _, lo, hi, clo, chi, _ = lax.while_loop(bis_cond, bis_body, st0)
    budget = kf - chi
    need_ties = jnp.max((clo - chi) - budget) > 0.5

    pairs = [slice(LANES * (h // 2), LANES * (h // 2 + 1)) for h in range(ATTN_HEADS)]

    def issue_qk(jb, slot):
        kb = k_ref[key_rows(jnp.minimum(jb, last_kv)), :]
        for h in range(ATTN_HEADS):
            lg_sc[slot, h] = _dot_nt(kb[:, pairs[h]], qm_sc[h])

    def attend(jb, slot, with_ties):
        s = s_sc[jb]
        if with_ties:
            cand = jnp.logical_and(s >= lo, s < hi)
            candf = jnp.where(cand, 1.0, 0.0)
            before = (qcol < krow).astype(BF16)
            rank = tie_sc[...] + _dot(before, candf.astype(BF16))
            tie_sc[...] += jnp.sum(candf, axis=0, keepdims=True)
            sel = jnp.logical_or(s >= hi, jnp.logical_and(cand, rank < budget))
        else:
            sel = s >= lo
        kind = jnp.where(jb > i, _KIND_PAD, jnp.minimum(i - jb, _KIND_FAR))
        probs, alphas = [], []
        for h in range(ATTN_HEADS):
            lg = jnp.where(sel, lg_sc[slot, h] + bias_sc[h, kind], NEG_INF)
            m_old = m_sc[h]
            m_new = jnp.maximum(m_old, jnp.max(lg, axis=0, keepdims=True))
            m_safe = jnp.where(m_new == NEG_INF, 0.0, m_new)
            alpha = jnp.exp2(m_old - m_safe)
            p = jnp.exp2(lg - m_safe)
            l_sc[h] = alpha * l_sc[h] + jnp.sum(p, axis=0, keepdims=True)
            m_sc[h] = m_new
            probs.append(p.astype(BF16))
            alphas.append(alpha)
        vb = vt_ref[jnp.minimum(jb, last_kv)]
        for h in range(ATTN_HEADS):
            acc_sc[h] = alphas[h] * acc_sc[h] + _dot(vb[pairs[h], :], probs[h])

    def run(with_ties):
        m_sc[...] = jnp.full(m_sc.shape, NEG_INF, F32)
        l_sc[...] = jnp.zeros_like(l_sc)
        acc_sc[...] = jnp.zeros_like(acc_sc)
        tie_sc[...] = jnp.zeros_like(tie_sc)
        issue_qk(0, 0)

        def pair_body(j, carry):
            issue_qk(2 * j + 1, 1)
            attend(2 * j, 0, with_ties)
            issue_qk(2 * j + 2, 0)
            attend(2 * j + 1, 1, with_ties)
            return carry
        lax.fori_loop(0, npair, pair_body, 0)

    @pl.when(need_ties)
    def _():
        run(True)

    @pl.when(jnp.logical_not(need_ties))
    def _():
        run(False)

    first_head = lax.broadcasted_iota(I32, (LANES, tq), 0) < ATTN_HEAD_DIM
    for p in range(ATTN_HEADS // 2):
        pair_t = jnp.where(first_head, acc_sc[2 * p] / l_sc[2 * p], acc_sc[2 * p + 1] / l_sc[2 * p + 1])
        o_ref[:, LANES * p:LANES * (p + 1)] = jnp.transpose(pair_t)


def _dsa_call(rel_bias, qi, wi, q, ki2, k, v, *, tq):
    s = q.shape[0]
    assert tq > REL_MAX_DIST and s % tq == 0
    nb = s // tq
    topk = min(TOPK_MAX, s // 4)
    kern = functools.partial(_dsa_kernel, tq=tq, topk=topk,
                             idx_scale=(IDX_DIM ** -0.5) * (IDX_HEADS ** -0.5))
    whole = lambda width: pl.BlockSpec((s, width), lambda i: (0, 0), pipeline_mode=pl.Buffered(1))
    vt = jnp.transpose(v.reshape(nb, tq, ATTN_WIDTH), (0, 2, 1))
    return pl.pallas_call(
        kern,
        grid=(nb,),
        in_specs=[pl.BlockSpec(memory_space=pltpu.SMEM),
                  pl.BlockSpec((tq, IDX_HEADS * IDX_DIM), lambda i: (i, 0)),
                  pl.BlockSpec((tq, LANES), lambda i: (i, 0)),
                  pl.BlockSpec((tq, ATTN_WIDTH), lambda i: (i, 0)),
                  whole(2 * IDX_DIM), whole(ATTN_WIDTH),
                  pl.BlockSpec((nb, ATTN_WIDTH, tq), lambda i: (0, 0, 0), pipeline_mode=pl.Buffered(1))],
        out_specs=pl.BlockSpec((tq, ATTN_WIDTH), lambda i: (i, 0)),
        out_shape=jax.ShapeDtypeStruct((s, ATTN_WIDTH), F32),
        scratch_shapes=[pltpu.VMEM((nb + nb % 2, tq, tq), F32),
                        pltpu.VMEM((ATTN_HEADS, _N_KINDS, tq, tq), F32),
                        pltpu.VMEM((3, tq, tq), F32),
                        pltpu.VMEM((2, ATTN_HEADS, tq, tq), F32),
                        pltpu.VMEM((IDX_HEADS, tq, LANES), BF16),
                        pltpu.VMEM((ATTN_HEADS, tq, LANES), BF16),
                        pltpu.VMEM((ATTN_HEADS, 1, tq), F32),
                        pltpu.VMEM((ATTN_HEADS, 1, tq), F32),
                        pltpu.VMEM((ATTN_HEADS, LANES, tq), F32),
                        pltpu.VMEM((1, tq), F32)],
        compiler_params=_cparams("arbitrary"),
        name="dsa_attention",
    )(rel_bias, qi, wi, q, ki2, k, vt)


_HGRN_CHUNK = 128
_HGRN_BASE = SUBLANES


def _hgrn_kernel(h4_ref, lb_ref, ng_ref, o_ref, st_sc, *, layer, rows):
    c_len = _HGRN_CHUNK

    @pl.when(pl.program_id(0) == 0)
    def _():
        st_sc[...] = jnp.zeros_like(st_sc)

    lb_all = lb_ref[...]
    e = jnp.exp(lb_all - jnp.max(lb_all, axis=0, keepdims=True))
    sm = e / jnp.sum(e, axis=0, keepdims=True)
    lbs = jnp.sum(sm[0:layer + 1, :], axis=0, keepdims=True) - sm[0:1, :]

    r1 = lax.broadcasted_iota(I32, (c_len, 1), 0)
    rr = lax.broadcasted_iota(I32, (c_len, c_len), 0)
    cc = lax.broadcasted_iota(I32, (c_len, c_len), 1)

    def block_ref(a, blk, r):
        x = a.reshape(c_len // blk, blk, a.shape[-1])[:, r:r + 1, :]
        return jnp.broadcast_to(x, (c_len // blk, blk, a.shape[-1])).reshape(c_len, a.shape[-1])

    w = HGRN_WIDTH
    for c in range(rows // c_len):
        rs = slice(c * c_len, (c + 1) * c_len)
        hq = h4_ref[rs, 0:w]
        z = h4_ref[rs, w:2 * w]
        val = h4_ref[rs, 2 * w:3 * w]
        hg = h4_ref[rs, 3 * w:4 * w]
        key = (1.0 - lbs) * jax.nn.sigmoid(-z)
        g = jnp.log1p(-jnp.minimum(key, KEY_MAX))
        a_cum = g
        sh = 1
        while sh < c_len:
            a_cum = a_cum + jnp.where(r1 >= sh, pltpu.roll(a_cum, sh, 0), 0.0)
            sh *= 2
        qf = _silu(hq)
        outs = []
        for h in range(HGRN_HEADS):
            hs = slice(h * HGRN_DIM, (h + 1) * HGRN_DIM)
            ah, qh, kh, vh = a_cum[:, hs], qf[:, hs], key[:, hs], val[:, hs]
            vh16 = vh.astype(BF16)
            st = st_sc[h]
            o = _dot_nt((qh * jnp.exp(ah)).astype(BF16), st.astype(BF16))
            ref = block_ref(ah, _HGRN_BASE, _HGRN_BASE // 2 - 1)
            sc = _dot_nt((qh * jnp.exp(ah - ref)).astype(BF16), (kh * jnp.exp(ref - ah)).astype(BF16))
            same = (rr >> _log2(_HGRN_BASE)) == (cc >> _log2(_HGRN_BASE))
            scores = jnp.where(jnp.logical_and(same, cc <= rr), sc, 0.0)
            half = _HGRN_BASE
            while half < c_len:
                blk = 2 * half
                ref = block_ref(ah, blk, half - 1)
                second = (r1 & (blk - 1)) >= half
                ql = jnp.where(second, qh * jnp.exp(jnp.minimum(ah - ref, 0.0)), 0.0)
                kl = jnp.where(second, 0.0, kh * jnp.exp(jnp.minimum(ref - ah, 0.0)))
                sc = _dot_nt(ql.astype(BF16), kl.astype(BF16))
                scores = scores + jnp.where((rr >> _log2(blk)) == (cc >> _log2(blk)), sc, 0.0)
                half = blk
            o = o + _dot(scores.astype(BF16), vh16)
            a_last = ah[c_len - 1:c_len, :]
            kd = (kh * jnp.exp(a_last - ah)).astype(BF16)
            st_sc[h] = st * jnp.exp(a_last) + _dot_tn(vh16, kd)
            o = o * lax.rsqrt(jnp.mean(o * o, axis=-1, keepdims=True) + RMS_EPS)
            outs.append(o)
        o_ref[rs, :] = jnp.concatenate(outs, axis=1) * ng_ref[...] * _silu(hg)


def _hgrn_call(h4, lb, ng, *, layer, rows):
    s = h4.shape[0]
    kern = functools.partial(_hgrn_kernel, layer=layer, rows=rows)
    return pl.pallas_call(
        kern,
        grid=(s // rows,),
        in_specs=[pl.BlockSpec((rows, 4 * HGRN_WIDTH), lambda i: (i, 0)),
                  pl.BlockSpec(lb.shape, lambda i: (0, 0)),
                  pl.BlockSpec((1, HGRN_WIDTH), lambda i: (0, 0))],
        out_specs=pl.BlockSpec((rows, HGRN_WIDTH), lambda i: (i, 0)),
        out_shape=jax.ShapeDtypeStruct((s, HGRN_WIDTH), F32),
        scratch_shapes=[pltpu.VMEM((HGRN_HEADS, HGRN_DIM, HGRN_DIM), F32)],
        compiler_params=_cparams("arbitrary"),
        name="hgrn2",
    )(h4, lb, ng)


_POOL_HALO = 16


def _merge_kernel(x_ref, mod_ref, a_ref, ah_ref, yb_ref, yc_ref, g_ref, pw_ref, ps_ref, wb_ref, wo_ref,
                  lng_ref, lnb_ref, o_ref, *, tm, alpha):
    i = pl.program_id(0)
    d = x_ref.shape[1]
    _, _, gate = _mod_rows(mod_ref, 1)
    a = a_ref[...]
    halo = jnp.where(i == 0, 0.0, ah_ref[...])
    ext = jnp.concatenate([halo, a], axis=0)
    lane = lax.broadcasted_iota(I32, (tm, POOL_WIDTH), 1)
    grp = lane >> _log2(POOL_GROUP)
    pos1 = (i * tm + lax.broadcasted_iota(I32, (tm, POOL_WIDTH), 0) + 1).astype(F32)
    win = jnp.zeros((tm, POOL_WIDTH), F32)
    cnt = jnp.zeros((tm, POOL_WIDTH), F32)
    ssum, have = ext, 1
    for gi, wd in enumerate(POOL_WINDOWS):
        while have < wd:
            ssum = ssum + pltpu.roll(ssum, have, 0)
            have *= 2
        assert have == wd
        win = jnp.where(grp == gi, ssum[_POOL_HALO:, :], win)
        cnt = jnp.where(grp == gi, jnp.minimum(pos1, float(wd)), cnt)
    dlt = (win / cnt - a).astype(BF16)
    ya = _dot(dlt, pw_ref[...]) * ps_ref[...]
    za = _dot(ya.astype(BF16), wb_ref[0:POOL_WIDTH, :])
    zb = _dot(yb_ref[...].astype(BF16), wb_ref[POOL_WIDTH:POOL_WIDTH + ATTN_WIDTH, :])
    zc = _dot(yc_ref[...].astype(BF16), wb_ref[POOL_WIDTH + ATTN_WIDTH:, :])
    merged = (jax.nn.sigmoid(g_ref[:, 0:d]) * za + jax.nn.sigmoid(g_ref[:, d:2 * d]) * zb
              + jax.nn.sigmoid(g_ref[:, 2 * d:3 * d]) * zc)
    y = _dot(merged.astype(BF16), wo_ref[...])
    z = alpha * x_ref[...] + (1.0 + gate) * y
    o_ref[...] = _layer_norm(z, lng_ref[1:2, :], lnb_ref[1:2, :])


def _merge_call(x, mod, a, yb, yc, gates, pw_bd, ps, wb, wo, ln_g, ln_b, *, tm, alpha):
    s, d = x.shape
    kern = functools.partial(_merge_kernel, tm=tm, alpha=alpha)
    full = lambda arr: pl.BlockSpec(arr.shape, lambda i: (0, 0))
    hb = tm // _POOL_HALO
    return pl.pallas_call(
        kern,
        grid=(s // tm,),
        in_specs=[pl.BlockSpec((tm, d), lambda i: (i, 0)),
                  full(mod),
                  pl.BlockSpec((tm, POOL_WIDTH), lambda i: (i, 0)),
                  pl.BlockSpec((_POOL_HALO, POOL_WIDTH), lambda i: (jnp.maximum(i * hb - 1, 0), 0)),
                  pl.BlockSpec((tm, ATTN_WIDTH), lambda i: (i, 0)),
                  pl.BlockSpec((tm, HGRN_WIDTH), lambda i: (i, 0)),
                  pl.BlockSpec((tm, 3 * d), lambda i: (i, 0)),
                  full(pw_bd), full(ps), full(wb), full(wo), full(ln_g), full(ln_b)],
        out_specs=pl.BlockSpec((tm, d), lambda i: (i, 0)),
        out_shape=jax.ShapeDtypeStruct((s, d), F32),
        compiler_params=_cparams("arbitrary"),
        name="pool_merge",
    )(x, mod, a, a, yb, yc, gates, pw_bd, ps, wb, wo, ln_g, ln_b)


def _pool_blockdiag(pool_w):
    n = pool_w.shape[0]
    rows = [jnp.concatenate([pool_w[g] if g == j else jnp.zeros_like(pool_w[g]) for j in range(n)], axis=1)
            for g in range(n)]
    return jnp.concatenate(rows, axis=0)


def kernel(x, c, w_ada, b_ada, ln_g, ln_b, ffn_w_in, ffn_w_out, mix_w_in, pool_w, pool_scale, rel_bias,
           hgrn_lb, hgrn_norm_g, w_branch, w_out):
    batch, s, d = x.shape
    assert batch == 1, "kernels are written for a single sequence"
    depth = w_ada.shape[0]
    alpha = (2 * depth) ** 0.25
    tm_ffn = min(512, s)
    tf = 256
    tm_proj = min(256, s)
    tq = min(256, s)
    tm_merge = min(256, s)
    hgrn_rows = min(256, s)

    mods = _ada_call(c, w_ada, b_ada)
    xs = x[0]
    for l in range(depth):
        mod = mods[l].reshape(3 * N_SUB, d)
        xs = _ffn_call(xs, mod, ffn_w_in[l, 0].astype(BF16), ffn_w_out[l, 0].astype(BF16), ln_g[l], ln_b[l],
                       sub=0, alpha=alpha, tm=tm_ffn, tf=tf)
        gates, a, h4, qi, q, k, v, ki2, wi = _proj_call(xs, mod, _proj_weight(mix_w_in[l]), tm=tm_proj)
        yb = _dsa_call(rel_bias, qi, wi, q, ki2, k, v, tq=tq)
        yc = _hgrn_call(h4, hgrn_lb, hgrn_norm_g[l].reshape(1, -1), layer=l, rows=hgrn_rows)
        xs = _merge_call(xs, mod, a, yb, yc, gates, _pool_blockdiag(pool_w[l]).astype(BF16),
                         pool_scale[l].reshape(1, -1), w_branch[l].astype(BF16), w_out[l].astype(BF16),
                         ln_g[l], ln_b[l], tm=tm_merge, alpha=alpha)
        xs = _ffn_call(xs, mod, ffn_w_in[l, 1].astype(BF16), ffn_w_out[l, 1].astype(BF16), ln_g[l], ln_b[l],
                       sub=2, alpha=alpha, tm=tm_ffn, tf=tf)
    return xs[None]
```

```python
import functools
import math

import jax
import jax.numpy as jnp
from jax import lax
from jax.experimental import pallas as pl
from jax.experimental.pallas import tpu as pltpu

F32 = jnp.float32
BF16 = jnp.bfloat16
I32 = jnp.int32
HIGHEST = lax.Precision.HIGHEST

POOL_WINDOWS = (2, 4, 8, 16)
POOL_GROUP = 64
POOL_WIDTH = POOL_GROUP * len(POOL_WINDOWS)
ATTN_HEADS = 4
ATTN_HEAD_DIM = 64
ATTN_WIDTH = ATTN_HEADS * ATTN_HEAD_DIM
IDX_HEADS = 8
IDX_DIM = 64
TOPK_MAX = 256
HGRN_HEADS = 4
HGRN_DIM = 128
HGRN_WIDTH = HGRN_HEADS * HGRN_DIM
KEY_MAX = 1.0 - 1e-6
REL_BUCKETS = 32
REL_MAX_DIST = 128
N_SUB = 3
LN_EPS = 1e-5
RMS_EPS = 1e-6

LANES = 128
SUBLANES = 8
VMEM_LIMIT_BYTES = 58 * 1024 * 1024

NEG_INF = float("-inf")
POS_INF = float("inf")


def _cparams(*sem):
    return pltpu.CompilerParams(dimension_semantics=sem, vmem_limit_bytes=VMEM_LIMIT_BYTES)


def _dot(a, b):
    return jnp.dot(a, b, preferred_element_type=F32)


def _dot_nt(a, b):
    return lax.dot_general(a, b, (((1,), (1,)), ((), ())), preferred_element_type=F32)


def _dot_tn(a, b):
    return lax.dot_general(a, b, (((0,), (0,)), ((), ())), preferred_element_type=F32)


def _log2(n):
    assert n & (n - 1) == 0
    return n.bit_length() - 1


def _silu(x):
    return x * jax.nn.sigmoid(x)


def _mod_rows(mod_ref, j):
    return mod_ref[3 * j:3 * j + 1, :], mod_ref[3 * j + 1:3 * j + 2, :], mod_ref[3 * j + 2:3 * j + 3, :]


def _layer_norm(z, g, b):
    mu = jnp.mean(z, axis=-1, keepdims=True)
    zc = z - mu
    var = jnp.mean(zc * zc, axis=-1, keepdims=True)
    return zc * lax.rsqrt(var + LN_EPS) * g + b


def _ada_kernel(c_ref, w_ref, b_ref, o_ref):
    c = c_ref[...]
    r = jnp.dot(_silu(c), w_ref[0], precision=HIGHEST, preferred_element_type=F32)
    o_ref[0] = r[0:1, :] + b_ref[0]


def _ada_call(c, w_ada, b_ada):
    depth, d, n = w_ada.shape
    tn = n // 8
    c8 = jnp.broadcast_to(c, (SUBLANES, d))
    return pl.pallas_call(
        _ada_kernel,
        grid=(depth, n // tn),
        in_specs=[pl.BlockSpec((SUBLANES, d), lambda l, j: (0, 0)),
                  pl.BlockSpec((1, d, tn), lambda l, j: (l, 0, j)),
                  pl.BlockSpec((1, 1, tn), lambda l, j: (l, 0, j))],
        out_specs=pl.BlockSpec((1, 1, tn), lambda l, j: (l, 0, j)),
        out_shape=jax.ShapeDtypeStruct((depth, 1, n), F32),
        compiler_params=_cparams("arbitrary", "arbitrary"),
        name="ada_mod",
    )(c8, w_ada, b_ada.reshape(depth, 1, n))


def _ffn_kernel(x_ref, mod_ref, wi_ref, wo_ref, lng_ref, lnb_ref, o_ref, *, sub, tf, alpha):
    shift, scale, gate = _mod_rows(mod_ref, sub)
    ff = wo_ref.shape[0]
    x = x_ref[...]
    h = (x * (1.0 + scale) + shift).astype(BF16)
    y = jnp.zeros(x.shape, F32)
    for c0 in range(0, ff, tf):
        g = _dot(h, wi_ref[:, c0:c0 + tf])
        u = _dot(h, wi_ref[:, ff + c0:ff + c0 + tf])
        y = y + _dot((_silu(g) * u).astype(BF16), wo_ref[c0:c0 + tf, :])
    z = alpha * x + (0.5 * (1.0 + gate)) * y
    o_ref[...] = _layer_norm(z, lng_ref[sub:sub + 1, :], lnb_ref[sub:sub + 1, :])


def _ffn_call(x, mod, w_in, w_out, ln_g, ln_b, *, sub, alpha, tm, tf):
    s, d = x.shape
    ff = w_out.shape[0]
    assert ff % tf == 0
    kern = functools.partial(_ffn_kernel, sub=sub, tf=tf, alpha=alpha)
    resident = lambda arr: pl.BlockSpec(arr.shape, lambda i: (0, 0), pipeline_mode=pl.Buffered(1))
    return pl.pallas_call(
        kern,
        grid=(s // tm,),
        in_specs=[pl.BlockSpec((tm, d), lambda i: (i, 0)),
                  pl.BlockSpec((3 * N_SUB, d), lambda i: (0, 0)),
                  resident(w_in), resident(w_out),
                  pl.BlockSpec((N_SUB, d), lambda i: (0, 0)),
                  pl.BlockSpec((N_SUB, d), lambda i: (0, 0))],
        out_specs=pl.BlockSpec((tm, d), lambda i: (i, 0)),
        out_shape=jax.ShapeDtypeStruct((s, d), F32),
        compiler_params=_cparams("arbitrary"),
        name="ffn",
    )(x, mod, w_in, w_out, ln_g, ln_b)


_PROJ_OUTS = (("gates", 3 * 1024, F32), ("a", POOL_WIDTH, F32), ("h4", 4 * HGRN_WIDTH, F32),
              ("qi", IDX_HEADS * IDX_DIM, BF16), ("q", ATTN_WIDTH, BF16), ("k", ATTN_WIDTH, BF16),
              ("v", ATTN_WIDTH, BF16), ("ki2", 2 * IDX_DIM, BF16), ("wi", LANES, F32))
_PROJ_CHUNK = 1024


def _proj_kernel(x_ref, mod_ref, w_ref, *o_refs):
    shift, scale, _ = _mod_rows(mod_ref, 1)
    h = (x_ref[...] * (1.0 + scale) + shift).astype(BF16)
    c0 = 0
    for (_, width, dt), o_ref in zip(_PROJ_OUTS, o_refs):
        for cc in range(0, width, _PROJ_CHUNK):
            cw = min(_PROJ_CHUNK, width - cc)
            o_ref[:, cc:cc + cw] = _dot(h, w_ref[:, c0 + cc:c0 + cc + cw]).astype(dt)
        c0 += width


def _proj_weight(w):
    d = w.shape[0]
    widths = (POOL_WIDTH, ATTN_WIDTH, ATTN_WIDTH, ATTN_WIDTH, IDX_HEADS * IDX_DIM, IDX_DIM, IDX_HEADS,
              HGRN_WIDTH, HGRN_WIDTH, HGRN_WIDTH, HGRN_WIDTH, 3 * d)
    parts, c0 = [], 0
    for wd in widths:
        parts.append(w[:, c0:c0 + wd])
        c0 += wd
    a, q, k, v, qi, ki, wi, hq, hf, hi, hg, gates = parts
    wi_pad = jnp.pad(wi, ((0, 0), (0, LANES - IDX_HEADS)))
    q = q * (ATTN_HEAD_DIM ** -0.5 * _LOG2E)
    return jnp.concatenate([gates, a, hq, hf, hi, hg, qi, q, k, v, ki, ki, wi_pad], axis=1).astype(BF16)


def _proj_call(x, mod, w, *, tm):
    s, d = x.shape
    n = w.shape[1]
    assert n == sum(wd for _, wd, _ in _PROJ_OUTS)
    return pl.pallas_call(
        _proj_kernel,
        grid=(s // tm,),
        in_specs=[pl.BlockSpec((tm, d), lambda i: (i, 0)),
                  pl.BlockSpec((3 * N_SUB, d), lambda i: (0, 0)),
                  pl.BlockSpec((d, n), lambda i: (0, 0))],
        out_specs=[pl.BlockSpec((tm, wd), lambda i: (i, 0)) for _, wd, _ in _PROJ_OUTS],
        out_shape=[jax.ShapeDtypeStruct((s, wd), dt) for _, wd, dt in _PROJ_OUTS],
        compiler_params=_cparams("arbitrary"),
        name="mix_proj",
    )(x, mod, w)


_BISECT_MAX_ITERS = 320
_F32_TINY = float(jnp.finfo(jnp.float32).tiny)
_KIND_FAR, _KIND_PAD, _N_KINDS = 2, 3, 4
_SMASK_FULL, _SMASK_DIAG, _SMASK_PAD = 0, 1, 2
_LOG2E = math.log2(math.e)


def _t5_bucket(dist):
    max_exact = REL_BUCKETS // 2
    d32 = jnp.maximum(dist, 1).astype(F32)
    large = max_exact + (jnp.log(d32 / max_exact) / math.log(REL_MAX_DIST / max_exact)
                         * (REL_BUCKETS - max_exact)).astype(I32)
    large = jnp.minimum(large, REL_BUCKETS - 1)
    return jnp.where(dist < max_exact, dist, large)


def _dsa_kernel(relb_ref, qi_ref, wi_ref, q_ref, ki2_ref, k_ref, vt_ref, o_ref,
                s_sc, bias_sc, smask_sc, lg_sc, qim_sc, qm_sc, m_sc, l_sc, acc_sc, tie_sc, cnt_sc,
                *, tq, topk, idx_scale):
    tk = tq
    i = pl.program_id(0)
    krow = lax.broadcasted_iota(I32, (tk, tq), 0)
    qcol = lax.broadcasted_iota(I32, (tk, tq), 1)
    causal = krow <= qcol
    lane = lax.broadcasted_iota(I32, (tq, LANES), 1)
    lo_half = lane < ATTN_HEAD_DIM

    @pl.when(i == 0)
    def _():
        for which in range(2):
            bucket = _t5_bucket(jnp.maximum(qcol - krow + which * tq, 0))
            for h in range(ATTN_HEADS):
                b = jnp.zeros((tk, tq), F32)
                for bk in range(REL_BUCKETS):
                    b = jnp.where(bucket == bk, relb_ref[bk, h] * _LOG2E, b)
                bias_sc[h, which] = jnp.where(causal, b, NEG_INF) if which == 0 else b
        for h in range(ATTN_HEADS):
            bias_sc[h, _KIND_FAR] = jnp.full((tk, tq), relb_ref[REL_BUCKETS - 1, h] * _LOG2E, F32)
            bias_sc[h, _KIND_PAD] = jnp.full((tk, tq), NEG_INF, F32)
        smask_sc[_SMASK_FULL] = jnp.zeros((tk, tq), F32)
        smask_sc[_SMASK_DIAG] = jnp.where(causal, 0.0, NEG_INF)
        smask_sc[_SMASK_PAD] = jnp.full((tk, tq), NEG_INF, F32)

    qi = qi_ref[...]
    for h in range(IDX_HEADS):
        pair = qi[:, LANES * (h // 2):LANES * (h // 2 + 1)]
        qim_sc[h] = jnp.where(lo_half if h % 2 == 0 else ~lo_half, pair, jnp.zeros_like(pair))
    qs = q_ref[...]
    for h in range(ATTN_HEADS):
        pair = qs[:, LANES * (h // 2):LANES * (h // 2 + 1)]
        qm_sc[h] = jnp.where(lo_half if h % 2 == 0 else ~lo_half, pair, jnp.zeros_like(pair))

    wt = jnp.transpose(wi_ref[...])

    def key_rows(jb):
        return pl.ds(pl.multiple_of(jb * tk, tk), tk)

    def fold(x):
        return x.reshape(tk // SUBLANES, SUBLANES, tq)

    def index_scores(jb):
        kb = ki2_ref[key_rows(jb), :]
        acc = jnp.zeros((tk, tq), F32)
        for h in range(IDX_HEADS):
            acc = acc + wt[h:h + 1, :] * jnp.maximum(_dot_nt(kb, qim_sc[h]), 0.0)
        return acc * idx_scale

    npair = (i + 2) // 2
    last_kv = k_ref.shape[0] // tk - 1

    def p1_pair(j, carry):
        rmin, rmax = carry
        for jb in (2 * j, 2 * j + 1):
            raw = index_scores(jnp.minimum(jb, last_kv))
            kind = jnp.where(jb > i, _SMASK_PAD, jnp.where(jb == i, _SMASK_DIAG, _SMASK_FULL))
            s_sc[jb] = raw + smask_sc[kind]
            rmin = jnp.minimum(rmin, jnp.min(fold(raw), axis=0))
            rmax = jnp.maximum(rmax, jnp.max(fold(raw), axis=0))
        return rmin, rmax

    rmin, rmax = lax.fori_loop(0, npair, p1_pair, (jnp.full((SUBLANES, tq), POS_INF, F32),
                                                   jnp.full((SUBLANES, tq), NEG_INF, F32)))
    rowmin = jnp.min(rmin, axis=0, keepdims=True)
    rowmax = jnp.max(rmax, axis=0, keepdims=True)

    kf = float(topk)
    nfin = (i * tq + lax.broadcasted_iota(I32, (1, tq), 1) + 1).astype(F32)
    small = nfin <= kf

    def count_ge(x):
        def body(j, cnt):
            for jb in (2 * j, 2 * j + 1):
                cnt = cnt + jnp.sum(fold(jnp.where(s_sc[jb] >= x, 1.0, 0.0)), axis=0)
            return cnt
        cnt = lax.fori_loop(0, npair, body, jnp.zeros((SUBLANES, tq), F32))
        return jnp.sum(cnt, axis=0, keepdims=True)

    def bis_cond(st):
        it, _, _, _, _, done = st
        return jnp.logical_and(it < _BISECT_MAX_ITERS, jnp.min(done) < 0.5)

    def bis_body(st):
        it, lo, hi, clo, chi, done = st
        mid = 0.5 * lo + 0.5 * hi
        mid = jnp.where(jnp.logical_and(lo == 0.0, hi > _F32_TINY), _F32_TINY, mid)
        mid = jnp.where(jnp.logical_and(lo < 0.0, hi > 0.0), 0.0, mid)
        mid = jnp.where(it == 0, rowmax, mid)
        stuck = jnp.logical_or(jnp.logical_or(mid <= lo, mid >= hi),
                               jnp.logical_and(lo == 0.0, hi <= _F32_TINY))
        c = count_ge(mid)
        active = jnp.logical_and(done < 0.5, jnp.logical_not(stuck))
        up = jnp.logical_and(active, c >= kf)
        dn = jnp.logical_and(active, c < kf)
        lo = jnp.where(up, mid, lo)
        clo = jnp.where(up, c, clo)
        hi = jnp.where(dn, mid, hi)
        chi = jnp.where(dn, c, chi)
        fin = jnp.logical_or(stuck, c == kf)
        done = jnp.where(fin, 1.0, done)
        return it + 1, lo, hi, clo, chi, done

    st0 = (jnp.int32(0), jnp.where(small, NEG_INF, rowmin), jnp.full((1, tq), POS_INF, F32), nfin,
           jnp.zeros((1, tq), F32), jnp.where(small, 1.0, 0.0))
    _, lo, hi, clo, chi, _ = lax.while_loop(bis_cond, bis_body, st0)
    budget = kf - chi
    need_ties = jnp.max((clo - chi) - budget) > 0.5

    pairs = [slice(LANES * (h // 2), LANES * (h // 2 + 1)) for h in range(ATTN_HEADS)]

    def issue_qk(jb, slot):
        kb = k_ref[key_rows(jnp.minimum(jb, last_kv)), :]
        for h in range(ATTN_HEADS):
            lg_sc[slot, h] = _dot_nt(kb[:, pairs[h]], qm_sc[h])

    def attend(jb, slot, thr, cum=None):
        s = s_sc[jb]
        if cum is None:
            sel = s >= thr
        else:
            cand = jnp.logical_and(s >= lo, s < hi)
            before = (qcol < krow).astype(BF16)
            rank = cum + _dot(before, jnp.where(cand, 1.0, 0.0).astype(BF16))
            sel = jnp.logical_or(s >= hi, jnp.logical_and(cand, rank < budget))
        kind = jnp.where(jb > i, _KIND_PAD, jnp.minimum(i - jb, _KIND_FAR))
        probs, alphas = [], []
        for h in range(ATTN_HEADS):
            lg = jnp.where(sel, lg_sc[slot, h] + bias_sc[h, kind], NEG_INF)
            m_old = m_sc[h]
            m_new = jnp.maximum(m_old, jnp.max(lg, axis=0, keepdims=True))
            m_safe = jnp.where(m_new == NEG_INF, 0.0, m_new)
            alpha = jnp.exp2(m_old - m_safe)
            p = jnp.exp2(lg - m_safe)
            l_sc[h] = alpha * l_sc[h] + jnp.sum(p, axis=0, keepdims=True)
            m_sc[h] = m_new
            probs.append(p.astype(BF16))
            alphas.append(alpha)
        vb = vt_ref[jnp.minimum(jb, last_kv)]
        for h in range(ATTN_HEADS):
            acc_sc[h] = alphas[h] * acc_sc[h] + _dot(vb[pairs[h], :], probs[h])

    def run(with_ties):
        m_sc[...] = jnp.full(m_sc.shape, NEG_INF, F32)
        l_sc[...] = jnp.zeros_like(l_sc)
        acc_sc[...] = jnp.zeros_like(acc_sc)
        tie_sc[...] = jnp.zeros_like(tie_sc)
        if with_ties:
            def cand_body(j, carry):
                for jb in (2 * j, 2 * j + 1):
                    s = s_sc[jb]
                    cand = jnp.logical_and(s >= lo, s < hi)
                    cnt_sc[jb] = jnp.sum(jnp.where(cand, 1.0, 0.0), axis=0, keepdims=True)
                return carry
            lax.fori_loop(0, npair, cand_body, 0)
        issue_qk(0, 0)

        def pair(j, thr0, thr1, cum0=None, cum1=None):
            issue_qk(2 * j + 1, 1)
            attend(2 * j, 0, thr0, cum0)
            issue_qk(2 * j + 2, 0)
            attend(2 * j + 1, 1, thr1, cum1)

        def pair_body(j, carry):
            if not with_ties:
                pair(j, lo, lo)
                return carry
            cum0 = tie_sc[...]
            cum1 = cum0 + cnt_sc[2 * j]
            cum2 = cum1 + cnt_sc[2 * j + 1]
            tie_sc[...] = cum2
            all0, all1 = cum1 <= budget, cum2 <= budget
            runs_out = jnp.logical_or(jnp.logical_and(jnp.logical_not(all0), cum0 < budget),
                                      jnp.logical_and(jnp.logical_not(all1), cum1 < budget))
            ranked = jnp.max(jnp.where(runs_out, 1.0, 0.0)) > 0.5

            @pl.when(ranked)
            def _():
                pair(j, None, None, cum0, cum1)

            @pl.when(jnp.logical_not(ranked))
            def _():
                pair(j, jnp.where(all0, lo, hi), jnp.where(all1, lo, hi))
            return carry
        lax.fori_loop(0, npair, pair_body, 0)

    @pl.when(need_ties)
    def _():
        run(True)

    @pl.when(jnp.logical_not(need_ties))
    def _():
        run(False)

    first_head = lax.broadcasted_iota(I32, (LANES, tq), 0) < ATTN_HEAD_DIM
    for p in range(ATTN_HEADS // 2):
        pair_t = jnp.where(first_head, acc_sc[2 * p] / l_sc[2 * p], acc_sc[2 * p + 1] / l_sc[2 * p + 1])
        o_ref[:, LANES * p:LANES * (p + 1)] = jnp.transpose(pair_t)


def _dsa_call(rel_bias, qi, wi, q, ki2, k, v, *, tq):
    s = q.shape[0]
    assert tq > REL_MAX_DIST and s % tq == 0
    nb = s // tq
    topk = min(TOPK_MAX, s // 4)
    kern = functools.partial(_dsa_kernel, tq=tq, topk=topk,
                             idx_scale=(IDX_DIM ** -0.5) * (IDX_HEADS ** -0.5))
    whole = lambda width: pl.BlockSpec((s, width), lambda i: (0, 0), pipeline_mode=pl.Buffered(1))
    vt = jnp.transpose(v.reshape(nb, tq, ATTN_WIDTH), (0, 2, 1))
    return pl.pallas_call(
        kern,
        grid=(nb,),
        in_specs=[pl.BlockSpec(memory_space=pltpu.SMEM),
                  pl.BlockSpec((tq, IDX_HEADS * IDX_DIM), lambda i: (i, 0)),
                  pl.BlockSpec((tq, LANES), lambda i: (i, 0)),
                  pl.BlockSpec((tq, ATTN_WIDTH), lambda i: (i, 0)),
                  whole(2 * IDX_DIM), whole(ATTN_WIDTH),
                  pl.BlockSpec((nb, ATTN_WIDTH, tq), lambda i: (0, 0, 0), pipeline_mode=pl.Buffered(1))],
        out_specs=pl.BlockSpec((tq, ATTN_WIDTH), lambda i: (i, 0)),
        out_shape=jax.ShapeDtypeStruct((s, ATTN_WIDTH), F32),
        scratch_shapes=[pltpu.VMEM((nb + nb % 2, tq, tq), F32),
                        pltpu.VMEM((ATTN_HEADS, _N_KINDS, tq, tq), F32),
                        pltpu.VMEM((3, tq, tq), F32),
                        pltpu.VMEM((2, ATTN_HEADS, tq, tq), F32),
                        pltpu.VMEM((IDX_HEADS, tq, LANES), BF16),
                        pltpu.VMEM((ATTN_HEADS, tq, LANES), BF16),
                        pltpu.VMEM((ATTN_HEADS, 1, tq), F32),
                        pltpu.VMEM((ATTN_HEADS, 1, tq), F32),
                        pltpu.VMEM((ATTN_HEADS, LANES, tq), F32),
                        pltpu.VMEM((1, tq), F32),
                        pltpu.VMEM((nb + nb % 2, 1, tq), F32)],
        compiler_params=_cparams("arbitrary"),
        name="dsa_attention",
    )(rel_bias, qi, wi, q, ki2, k, vt)


_HGRN_CHUNK = 128
_HGRN_BASE = SUBLANES


def _hgrn_kernel(h4_ref, lb_ref, ng_ref, o_ref, st_sc, *, layer, rows):
    c_len = _HGRN_CHUNK

    @pl.when(pl.program_id(0) == 0)
    def _():
        st_sc[...] = jnp.zeros_like(st_sc)

    lb_all = lb_ref[...]
    e = jnp.exp(lb_all - jnp.max(lb_all, axis=0, keepdims=True))
    sm = e / jnp.sum(e, axis=0, keepdims=True)
    lbs = jnp.sum(sm[0:layer + 1, :], axis=0, keepdims=True) - sm[0:1, :]

    r1 = lax.broadcasted_iota(I32, (c_len, 1), 0)
    rr = lax.broadcasted_iota(I32, (c_len, c_len), 0)
    cc = lax.broadcasted_iota(I32, (c_len, c_len), 1)

    def block_ref(a, blk, r):
        x = a.reshape(c_len // blk, blk, a.shape[-1])[:, r:r + 1, :]
        return jnp.broadcast_to(x, (c_len // blk, blk, a.shape[-1])).reshape(c_len, a.shape[-1])

    w = HGRN_WIDTH
    for c in range(rows // c_len):
        rs = slice(c * c_len, (c + 1) * c_len)
        hq = h4_ref[rs, 0:w]
        z = h4_ref[rs, w:2 * w]
        val = h4_ref[rs, 2 * w:3 * w]
        hg = h4_ref[rs, 3 * w:4 * w]
        key = (1.0 - lbs) * jax.nn.sigmoid(-z)
        g = jnp.log1p(-jnp.minimum(key, KEY_MAX))
        a_cum = g
        sh = 1
        while sh < c_len:
            a_cum = a_cum + jnp.where(r1 >= sh, pltpu.roll(a_cum, sh, 0), 0.0)
            sh *= 2
        qf = _silu(hq)
        outs = []
        for h in range(HGRN_HEADS):
            hs = slice(h * HGRN_DIM, (h + 1) * HGRN_DIM)
            ah, qh, kh, vh = a_cum[:, hs], qf[:, hs], key[:, hs], val[:, hs]
            vh16 = vh.astype(BF16)
            st = st_sc[h]
            o = _dot_nt((qh * jnp.exp(ah)).astype(BF16), st.astype(BF16))
            ref = block_ref(ah, _HGRN_BASE, _HGRN_BASE // 2 - 1)
            sc = _dot_nt((qh * jnp.exp(ah - ref)).astype(BF16), (kh * jnp.exp(ref - ah)).astype(BF16))
            same = (rr >> _log2(_HGRN_BASE)) == (cc >> _log2(_HGRN_BASE))
            scores = jnp.where(jnp.logical_and(same, cc <= rr), sc, 0.0)
            half = _HGRN_BASE
            while half < c_len:
                blk = 2 * half
                ref = block_ref(ah, blk, half - 1)
                second = (r1 & (blk - 1)) >= half
                ql = jnp.where(second, qh * jnp.exp(jnp.minimum(ah - ref, 0.0)), 0.0)
                kl = jnp.where(second, 0.0, kh * jnp.exp(jnp.minimum(ref - ah, 0.0)))
                sc = _dot_nt(ql.astype(BF16), kl.astype(BF16))
                scores = scores + jnp.where((rr >> _log2(blk)) == (cc >> _log2(blk)), sc, 0.0)
                half = blk
            o = o + _dot(scores.astype(BF16), vh16)
            a_last = ah[c_len - 1:c_len, :]
            kd = (kh * jnp.exp(a_last - ah)).astype(BF16)
            st_sc[h] = st * jnp.exp(a_last) + _dot_tn(vh16, kd)
            o = o * lax.rsqrt(jnp.mean(o * o, axis=-1, keepdims=True) + RMS_EPS)
            outs.append(o)
        o_ref[rs, :] = jnp.concatenate(outs, axis=1) * ng_ref[...] * _silu(hg)


def _hgrn_call(h4, lb, ng, *, layer, rows):
    s = h4.shape[0]
    kern = functools.partial(_hgrn_kernel, layer=layer, rows=rows)
    return pl.pallas_call(
        kern,
        grid=(s // rows,),
        in_specs=[pl.BlockSpec((rows, 4 * HGRN_WIDTH), lambda i: (i, 0)),
                  pl.BlockSpec(lb.shape, lambda i: (0, 0)),
                  pl.BlockSpec((1, HGRN_WIDTH), lambda i: (0, 0))],
        out_specs=pl.BlockSpec((rows, HGRN_WIDTH), lambda i: (i, 0)),
        out_shape=jax.ShapeDtypeStruct((s, HGRN_WIDTH), F32),
        scratch_shapes=[pltpu.VMEM((HGRN_HEADS, HGRN_DIM, HGRN_DIM), F32)],
        compiler_params=_cparams("arbitrary"),
        name="hgrn2",
    )(h4, lb, ng)


_POOL_HALO = 16


def _merge_kernel(x_ref, mod_ref, a_ref, ah_ref, yb_ref, yc_ref, g_ref, pw_ref, ps_ref, wb_ref, wo_ref,
                  lng_ref, lnb_ref, o_ref, *, tm, alpha):
    i = pl.program_id(0)
    d = x_ref.shape[1]
    _, _, gate = _mod_rows(mod_ref, 1)
    a = a_ref[...]
    halo = jnp.where(i == 0, 0.0, ah_ref[...])
    ext = jnp.concatenate([halo, a], axis=0)
    lane = lax.broadcasted_iota(I32, (tm, POOL_WIDTH), 1)
    grp = lane >> _log2(POOL_GROUP)
    pos1 = (i * tm + lax.broadcasted_iota(I32, (tm, POOL_WIDTH), 0) + 1).astype(F32)
    win = jnp.zeros((tm, POOL_WIDTH), F32)
    cnt = jnp.zeros((tm, POOL_WIDTH), F32)
    ssum, have = ext, 1
    for gi, wd in enumerate(POOL_WINDOWS):
        while have < wd:
            ssum = ssum + pltpu.roll(ssum, have, 0)
            have *= 2
        assert have == wd
        win = jnp.where(grp == gi, ssum[_POOL_HALO:, :], win)
        cnt = jnp.where(grp == gi, jnp.minimum(pos1, float(wd)), cnt)
    dlt = (win / cnt - a).astype(BF16)
    ya = _dot(dlt, pw_ref[...]) * ps_ref[...]
    za = _dot(ya.astype(BF16), wb_ref[0:POOL_WIDTH, :])
    zb = _dot(yb_ref[...].astype(BF16), wb_ref[POOL_WIDTH:POOL_WIDTH + ATTN_WIDTH, :])
    zc = _dot(yc_ref[...].astype(BF16), wb_ref[POOL_WIDTH + ATTN_WIDTH:, :])
    merged = (jax.nn.sigmoid(g_ref[:, 0:d]) * za + jax.nn.sigmoid(g_ref[:, d:2 * d]) * zb
              + jax.nn.sigmoid(g_ref[:, 2 * d:3 * d]) * zc)
    y = _dot(merged.astype(BF16), wo_ref[...])
    z = alpha * x_ref[...] + (1.0 + gate) * y
    o_ref[...] = _layer_norm(z, lng_ref[1:2, :], lnb_ref[1:2, :])


def _merge_call(x, mod, a, yb, yc, gates, pw_bd, ps, wb, wo, ln_g, ln_b, *, tm, alpha):
    s, d = x.shape
    kern = functools.partial(_merge_kernel, tm=tm, alpha=alpha)
    full = lambda arr: pl.BlockSpec(arr.shape, lambda i: (0, 0))
    hb = tm // _POOL_HALO
    return pl.pallas_call(
        kern,
        grid=(s // tm,),
        in_specs=[pl.BlockSpec((tm, d), lambda i: (i, 0)),
                  full(mod),
                  pl.BlockSpec((tm, POOL_WIDTH), lambda i: (i, 0)),
                  pl.BlockSpec((_POOL_HALO, POOL_WIDTH), lambda i: (jnp.maximum(i * hb - 1, 0), 0)),
                  pl.BlockSpec((tm, ATTN_WIDTH), lambda i: (i, 0)),
                  pl.BlockSpec((tm, HGRN_WIDTH), lambda i: (i, 0)),
                  pl.BlockSpec((tm, 3 * d), lambda i: (i, 0)),
                  full(pw_bd), full(ps), full(wb), full(wo), full(ln_g), full(ln_b)],
        out_specs=pl.BlockSpec((tm, d), lambda i: (i, 0)),
        out_shape=jax.ShapeDtypeStruct((s, d), F32),
        compiler_params=_cparams("arbitrary"),
        name="pool_merge",
    )(x, mod, a, a, yb, yc, gates, pw_bd, ps, wb, wo, ln_g, ln_b)


def _pool_blockdiag(pool_w):
    n = pool_w.shape[0]
    rows = [jnp.concatenate([pool_w[g] if g == j else jnp.zeros_like(pool_w[g]) for j in range(n)], axis=1)
            for g in range(n)]
    return jnp.concatenate(rows, axis=0)


def kernel(x, c, w_ada, b_ada, ln_g, ln_b, ffn_w_in, ffn_w_out, mix_w_in, pool_w, pool_scale, rel_bias,
           hgrn_lb, hgrn_norm_g, w_branch, w_out):
    batch, s, d = x.shape
    assert batch == 1, "kernels are written for a single sequence"
    depth = w_ada.shape[0]
    alpha = (2 * depth) ** 0.25
    tm_ffn = min(512, s)
    tf = 256
    tm_proj = min(256, s)
    tq = min(256, s)
    tm_merge = min(256, s)
    hgrn_rows = min(256, s)

    mods = _ada_call(c, w_ada, b_ada)
    xs = x[0]
    for l in range(depth):
        mod = mods[l].reshape(3 * N_SUB, d)
        xs = _ffn_call(xs, mod, ffn_w_in[l, 0].astype(BF16), ffn_w_out[l, 0].astype(BF16), ln_g[l], ln_b[l],
                       sub=0, alpha=alpha, tm=tm_ffn, tf=tf)
        gates, a, h4, qi, q, k, v, ki2, wi = _proj_call(xs, mod, _proj_weight(mix_w_in[l]), tm=tm_proj)
        yb = _dsa_call(rel_bias, qi, wi, q, ki2, k, v, tq=tq)
        yc = _hgrn_call(h4, hgrn_lb, hgrn_norm_g[l].reshape(1, -1), layer=l, rows=hgrn_rows)
        xs = _merge_call(xs, mod, a, yb, yc, gates, _pool_blockdiag(pool_w[l]).astype(BF16),
                         pool_scale[l].reshape(1, -1), w_branch[l].astype(BF16), w_out[l].astype(BF16),
                         ln_g[l], ln_b[l], tm=tm_merge, alpha=alpha)
        xs = _ffn_call(xs, mod, ffn_w_in[l, 1].astype(BF16), ffn_w_out[l, 1].astype(BF16), ln_g[l], ln_b[l],
                       sub=2, alpha=alpha, tm=tm_ffn, tf=tf)
    return xs[None]
```

```python
import functools
import math

import jax
import jax.numpy as jnp
from jax import lax
from jax.experimental import pallas as pl
from jax.experimental.pallas import tpu as pltpu

F32 = jnp.float32
BF16 = jnp.bfloat16
I32 = jnp.int32
HIGHEST = lax.Precision.HIGHEST

POOL_WINDOWS = (2, 4, 8, 16)
POOL_GROUP = 64
POOL_WIDTH = POOL_GROUP * len(POOL_WINDOWS)
ATTN_HEADS = 4
ATTN_HEAD_DIM = 64
ATTN_WIDTH = ATTN_HEADS * ATTN_HEAD_DIM
IDX_HEADS = 8
IDX_DIM = 64
TOPK_MAX = 256
HGRN_HEADS = 4
HGRN_DIM = 128
HGRN_WIDTH = HGRN_HEADS * HGRN_DIM
KEY_MAX = 1.0 - 1e-6
REL_BUCKETS = 32
REL_MAX_DIST = 128
N_SUB = 3
LN_EPS = 1e-5
RMS_EPS = 1e-6

LANES = 128
SUBLANES = 8
VMEM_LIMIT_BYTES = 58 * 1024 * 1024

NEG_INF = float("-inf")
POS_INF = float("inf")


def _cparams(*sem):
    return pltpu.CompilerParams(dimension_semantics=sem, vmem_limit_bytes=VMEM_LIMIT_BYTES)


def _dot(a, b):
    return jnp.dot(a, b, preferred_element_type=F32)


def _dot_nt(a, b):
    return lax.dot_general(a, b, (((1,), (1,)), ((), ())), preferred_element_type=F32)


def _dot_tn(a, b):
    return lax.dot_general(a, b, (((0,), (0,)), ((), ())), preferred_element_type=F32)


def _log2(n):
    assert n & (n - 1) == 0
    return n.bit_length() - 1


def _silu(x):
    return x * jax.nn.sigmoid(x)


def _mod_rows(mod_ref, j):
    return mod_ref[3 * j:3 * j + 1, :], mod_ref[3 * j + 1:3 * j + 2, :], mod_ref[3 * j + 2:3 * j + 3, :]


def _layer_norm(z, g, b):
    mu = jnp.mean(z, axis=-1, keepdims=True)
    zc = z - mu
    var = jnp.mean(zc * zc, axis=-1, keepdims=True)
    return zc * lax.rsqrt(var + LN_EPS) * g + b


def _ada_kernel(c_ref, w_ref, b_ref, o_ref):
    c = c_ref[...]
    r = jnp.dot(_silu(c), w_ref[0], precision=HIGHEST, preferred_element_type=F32)
    o_ref[0] = r[0:1, :] + b_ref[0]


def _ada_call(c, w_ada, b_ada):
    depth, d, n = w_ada.shape
    tn = n // 8
    c8 = jnp.broadcast_to(c, (SUBLANES, d))
    return pl.pallas_call(
        _ada_kernel,
        grid=(depth, n // tn),
        in_specs=[pl.BlockSpec((SUBLANES, d), lambda l, j: (0, 0)),
                  pl.BlockSpec((1, d, tn), lambda l, j: (l, 0, j)),
                  pl.BlockSpec((1, 1, tn), lambda l, j: (l, 0, j))],
        out_specs=pl.BlockSpec((1, 1, tn), lambda l, j: (l, 0, j)),
        out_shape=jax.ShapeDtypeStruct((depth, 1, n), F32),
        compiler_params=_cparams("arbitrary", "arbitrary"),
        name="ada_mod",
    )(c8, w_ada, b_ada.reshape(depth, 1, n))


def _ffn_kernel(x_ref, mod_ref, wi_ref, wo_ref, lng_ref, lnb_ref, o_ref, *, sub, tf, alpha):
    shift, scale, gate = _mod_rows(mod_ref, sub)
    ff = wo_ref.shape[0]
    x = x_ref[...]
    h = (x * (1.0 + scale) + shift).astype(BF16)
    y = jnp.zeros(x.shape, F32)
    for c0 in range(0, ff, tf):
        g = _dot(h, wi_ref[:, c0:c0 + tf])
        u = _dot(h, wi_ref[:, ff + c0:ff + c0 + tf])
        y = y + _dot((_silu(g) * u).astype(BF16), wo_ref[c0:c0 + tf, :])
    z = alpha * x + (0.5 * (1.0 + gate)) * y
    o_ref[...] = _layer_norm(z, lng_ref[sub:sub + 1, :], lnb_ref[sub:sub + 1, :])


def _ffn_call(x, mod, w_in, w_out, ln_g, ln_b, *, sub, alpha, tm, tf):
    s, d = x.shape
    ff = w_out.shape[0]
    assert ff % tf == 0
    kern = functools.partial(_ffn_kernel, sub=sub, tf=tf, alpha=alpha)
    resident = lambda arr: pl.BlockSpec(arr.shape, lambda i: (0, 0), pipeline_mode=pl.Buffered(1))
    return pl.pallas_call(
        kern,
        grid=(s // tm,),
        in_specs=[pl.BlockSpec((tm, d), lambda i: (i, 0)),
                  pl.BlockSpec((3 * N_SUB, d), lambda i: (0, 0)),
                  resident(w_in), resident(w_out),
                  pl.BlockSpec((N_SUB, d), lambda i: (0, 0)),
                  pl.BlockSpec((N_SUB, d), lambda i: (0, 0))],
        out_specs=pl.BlockSpec((tm, d), lambda i: (i, 0)),
        out_shape=jax.ShapeDtypeStruct((s, d), F32),
        compiler_params=_cparams("arbitrary"),
        name="ffn",
    )(x, mod, w_in, w_out, ln_g, ln_b)


_PROJ_OUTS = (("gates", 3 * 1024, F32), ("a", POOL_WIDTH, F32), ("h4", 4 * HGRN_WIDTH, F32),
              ("qi", IDX_HEADS * IDX_DIM, BF16), ("q", ATTN_WIDTH, BF16), ("k", ATTN_WIDTH, BF16),
              ("v", ATTN_WIDTH, BF16), ("ki2", 2 * IDX_DIM, BF16), ("wi", LANES, F32))
_PROJ_CHUNK = 1024


def _proj_kernel(x_ref, mod_ref, w_ref, *o_refs):
    shift, scale, _ = _mod_rows(mod_ref, 1)
    h = (x_ref[...] * (1.0 + scale) + shift).astype(BF16)
    c0 = 0
    for (_, width, dt), o_ref in zip(_PROJ_OUTS, o_refs):
        for cc in range(0, width, _PROJ_CHUNK):
            cw = min(_PROJ_CHUNK, width - cc)
            o_ref[:, cc:cc + cw] = _dot(h, w_ref[:, c0 + cc:c0 + cc + cw]).astype(dt)
        c0 += width


def _proj_weight(w):
    d = w.shape[0]
    widths = (POOL_WIDTH, ATTN_WIDTH, ATTN_WIDTH, ATTN_WIDTH, IDX_HEADS * IDX_DIM, IDX_DIM, IDX_HEADS,
              HGRN_WIDTH, HGRN_WIDTH, HGRN_WIDTH, HGRN_WIDTH, 3 * d)
    parts, c0 = [], 0
    for wd in widths:
        parts.append(w[:, c0:c0 + wd])
        c0 += wd
    a, q, k, v, qi, ki, wi, hq, hf, hi, hg, gates = parts
    wi_pad = jnp.pad(wi, ((0, 0), (0, LANES - IDX_HEADS)))
    q = q * (ATTN_HEAD_DIM ** -0.5 * _LOG2E)
    return jnp.concatenate([gates, a, hq, hf, hi, hg, qi, q, k, v, ki, ki, wi_pad], axis=1).astype(BF16)


def _proj_call(x, mod, w, *, tm):
    s, d = x.shape
    n = w.shape[1]
    assert n == sum(wd for _, wd, _ in _PROJ_OUTS)
    return pl.pallas_call(
        _proj_kernel,
        grid=(s // tm,),
        in_specs=[pl.BlockSpec((tm, d), lambda i: (i, 0)),
                  pl.BlockSpec((3 * N_SUB, d), lambda i: (0, 0)),
                  pl.BlockSpec((d, n), lambda i: (0, 0))],
        out_specs=[pl.BlockSpec((tm, wd), lambda i: (i, 0)) for _, wd, _ in _PROJ_OUTS],
        out_shape=[jax.ShapeDtypeStruct((s, wd), dt) for _, wd, dt in _PROJ_OUTS],
        compiler_params=_cparams("arbitrary"),
        name="mix_proj",
    )(x, mod, w)


_BISECT_MAX_ITERS = 320
_F32_TINY = float(jnp.finfo(jnp.float32).tiny)
_KIND_FAR, _KIND_PAD, _N_KINDS = 2, 3, 4
_VT_ROWS = LANES + 16
_SMASK_FULL, _SMASK_DIAG, _SMASK_PAD = 0, 1, 2
_LOG2E = math.log2(math.e)


def _t5_bucket(dist):
    max_exact = REL_BUCKETS // 2
    d32 = jnp.maximum(dist, 1).astype(F32)
    large = max_exact + (jnp.log(d32 / max_exact) / math.log(REL_MAX_DIST / max_exact)
                         * (REL_BUCKETS - max_exact)).astype(I32)
    large = jnp.minimum(large, REL_BUCKETS - 1)
    return jnp.where(dist < max_exact, dist, large)


def _dsa_kernel(relb_ref, qi_ref, wi_ref, q_ref, ki2_ref, k_ref, vt_ref, o_ref,
                s_sc, bias_sc, smask_sc, lg_sc, qim_sc, qm_sc, m_sc, acc_sc, tie_sc, cnt_sc,
                *, tq, topk, idx_scale):
    tk = tq
    i = pl.program_id(0)
    krow = lax.broadcasted_iota(I32, (tk, tq), 0)
    qcol = lax.broadcasted_iota(I32, (tk, tq), 1)
    causal = krow <= qcol
    lane = lax.broadcasted_iota(I32, (tq, LANES), 1)
    lo_half = lane < ATTN_HEAD_DIM

    @pl.when(i == 0)
    def _():
        for which in range(2):
            bucket = _t5_bucket(jnp.maximum(qcol - krow + which * tq, 0))
            for h in range(ATTN_HEADS):
                b = jnp.zeros((tk, tq), F32)
                for bk in range(REL_BUCKETS):
                    b = jnp.where(bucket == bk, relb_ref[bk, h] * _LOG2E, b)
                bias_sc[h, which] = (jnp.where(causal, b, NEG_INF) if which == 0 else b).astype(BF16)
        for h in range(ATTN_HEADS):
            bias_sc[h, _KIND_FAR] = jnp.full((tk, tq), relb_ref[REL_BUCKETS - 1, h] * _LOG2E, F32).astype(BF16)
            bias_sc[h, _KIND_PAD] = jnp.full((tk, tq), NEG_INF, BF16)
        smask_sc[_SMASK_FULL] = jnp.zeros((tk, tq), F32)
        smask_sc[_SMASK_DIAG] = jnp.where(causal, 0.0, NEG_INF)
        smask_sc[_SMASK_PAD] = jnp.full((tk, tq), NEG_INF, F32)

    qi = qi_ref[...]
    for h in range(IDX_HEADS):
        pair = qi[:, LANES * (h // 2):LANES * (h // 2 + 1)]
        qim_sc[h] = jnp.where(lo_half if h % 2 == 0 else ~lo_half, pair, jnp.zeros_like(pair))
    qs = q_ref[...]
    for h in range(ATTN_HEADS):
        pair = qs[:, LANES * (h // 2):LANES * (h // 2 + 1)]
        qm_sc[h] = jnp.where(lo_half if h % 2 == 0 else ~lo_half, pair, jnp.zeros_like(pair))

    wt = jnp.transpose(wi_ref[...])

    def key_rows(jb):
        return pl.ds(pl.multiple_of(jb * tk, tk), tk)

    def fold(x):
        return x.reshape(tk // SUBLANES, SUBLANES, tq)

    def index_scores(jb):
        kb = ki2_ref[key_rows(jb), :]
        acc = jnp.zeros((tk, tq), F32)
        for h in range(IDX_HEADS):
            acc = acc + wt[h:h + 1, :] * jnp.maximum(_dot_nt(kb, qim_sc[h]), 0.0)
        return acc * idx_scale

    npair = (i + 2) // 2
    last_kv = k_ref.shape[0] // tk - 1

    def p1_pair(j, carry):
        rmin, rmax = carry
        for jb in (2 * j, 2 * j + 1):
            raw = index_scores(jnp.minimum(jb, last_kv))
            kind = jnp.where(jb > i, _SMASK_PAD, jnp.where(jb == i, _SMASK_DIAG, _SMASK_FULL))
            s_sc[jb] = raw + smask_sc[kind]
            rmin = jnp.minimum(rmin, jnp.min(fold(raw), axis=0))
            rmax = jnp.maximum(rmax, jnp.max(fold(raw), axis=0))
        return rmin, rmax

    rmin, rmax = lax.fori_loop(0, npair, p1_pair, (jnp.full((SUBLANES, tq), POS_INF, F32),
                                                   jnp.full((SUBLANES, tq), NEG_INF, F32)))
    rowmin = jnp.min(rmin, axis=0, keepdims=True)
    rowmax = jnp.max(rmax, axis=0, keepdims=True)

    kf = float(topk)
    nfin = (i * tq + lax.broadcasted_iota(I32, (1, tq), 1) + 1).astype(F32)
    small = nfin <= kf

    def count_ge(x):
        def body(j, cnt):
            for jb in (2 * j, 2 * j + 1):
                cnt = cnt + jnp.sum(fold(jnp.where(s_sc[jb] >= x, 1.0, 0.0)), axis=0)
            return cnt
        cnt = lax.fori_loop(0, npair, body, jnp.zeros((SUBLANES, tq), F32))
        return jnp.sum(cnt, axis=0, keepdims=True)

    def bis_cond(st):
        it, _, _, _, _, done = st
        return jnp.logical_and(it < _BISECT_MAX_ITERS, jnp.min(done) < 0.5)

    def bis_body(st):
        it, lo, hi, clo, chi, done = st
        mid = 0.5 * lo + 0.5 * hi
        mid = jnp.where(jnp.logical_and(lo == 0.0, hi > _F32_TINY), _F32_TINY, mid)
        mid = jnp.where(jnp.logical_and(lo < 0.0, hi > 0.0), 0.0, mid)
        mid = jnp.where(it == 0, rowmax, mid)
        stuck = jnp.logical_or(jnp.logical_or(mid <= lo, mid >= hi),
                               jnp.logical_and(lo == 0.0, hi <= _F32_TINY))
        c = count_ge(mid)
        active = jnp.logical_and(done < 0.5, jnp.logical_not(stuck))
        up = jnp.logical_and(active, c >= kf)
        dn = jnp.logical_and(active, c < kf)
        lo = jnp.where(up, mid, lo)
        clo = jnp.where(up, c, clo)
        hi = jnp.where(dn, mid, hi)
        chi = jnp.where(dn, c, chi)
        fin = jnp.logical_or(stuck, c == kf)
        done = jnp.where(fin, 1.0, done)
        return it + 1, lo, hi, clo, chi, done

    st0 = (jnp.int32(0), jnp.where(small, NEG_INF, rowmin), jnp.full((1, tq), POS_INF, F32), nfin,
           jnp.zeros((1, tq), F32), jnp.where(small, 1.0, 0.0))
    _, lo, hi, clo, chi, _ = lax.while_loop(bis_cond, bis_body, st0)
    budget = kf - chi
    need_ties = jnp.max((clo - chi) - budget) > 0.5

    pairs = [slice(LANES * (h // 2), LANES * (h // 2 + 1)) for h in range(ATTN_HEADS)]

    def issue_qk(jb, slot):
        kb = k_ref[key_rows(jnp.minimum(jb, last_kv)), :]
        for h in range(ATTN_HEADS):
            lg_sc[slot, h] = _dot_nt(kb[:, pairs[h]], qm_sc[h]).astype(BF16)

    def attend(jb, slot, thr, cum=None):
        s = s_sc[jb]
        if cum is None:
            sel = s >= thr
        else:
            cand = jnp.logical_and(s >= lo, s < hi)
            before = (qcol < krow).astype(BF16)
            rank = cum + _dot(before, jnp.where(cand, 1.0, 0.0).astype(BF16))
            sel = jnp.logical_or(s >= hi, jnp.logical_and(cand, rank < budget))
        kind = jnp.where(jb > i, _KIND_PAD, jnp.minimum(i - jb, _KIND_FAR))
        selm = jnp.where(sel, 0.0, NEG_INF).astype(BF16)
        probs, alphas = [], []
        for h in range(ATTN_HEADS):
            lg = lg_sc[slot, h] + bias_sc[h, kind] + selm
            m_old = m_sc[h]
            m_new = jnp.maximum(m_old, jnp.max(lg, axis=0, keepdims=True).astype(F32))
            m_safe = jnp.where(m_new == NEG_INF, 0.0, m_new)
            alphas.append(jnp.exp2(m_old - m_safe))
            probs.append(jnp.exp2(lg - m_safe.astype(BF16)))
            m_sc[h] = m_new
        vb = vt_ref[jnp.minimum(jb, last_kv)]
        for h in range(ATTN_HEADS):
            vrows = slice(_VT_ROWS * (h // 2), _VT_ROWS * (h // 2 + 1))
            acc_sc[h] = alphas[h] * acc_sc[h] + _dot(vb[vrows, :], probs[h])

    def run(with_ties):
        m_sc[...] = jnp.full(m_sc.shape, NEG_INF, F32)
        acc_sc[...] = jnp.zeros_like(acc_sc)
        tie_sc[...] = jnp.zeros_like(tie_sc)
        if with_ties:
            def cand_body(j, carry):
                for jb in (2 * j, 2 * j + 1):
                    s = s_sc[jb]
                    cand = jnp.logical_and(s >= lo, s < hi)
                    cnt_sc[jb] = jnp.sum(jnp.where(cand, 1.0, 0.0), axis=0, keepdims=True)
                return carry
            lax.fori_loop(0, npair, cand_body, 0)
        issue_qk(0, 0)

        def pair(j, thr0, thr1, cum0=None, cum1=None):
            issue_qk(2 * j + 1, 1)
            attend(2 * j, 0, thr0, cum0)
            issue_qk(2 * j + 2, 0)
            attend(2 * j + 1, 1, thr1, cum1)

        def pair_body(j, carry):
            if not with_ties:
                pair(j, lo, lo)
                return carry
            cum0 = tie_sc[...]
            cum1 = cum0 + cnt_sc[2 * j]
            cum2 = cum1 + cnt_sc[2 * j + 1]
            tie_sc[...] = cum2
            all0, all1 = cum1 <= budget, cum2 <= budget
            runs_out = jnp.logical_or(jnp.logical_and(jnp.logical_not(all0), cum0 < budget),
                                      jnp.logical_and(jnp.logical_not(all1), cum1 < budget))
            ranked = jnp.max(jnp.where(runs_out, 1.0, 0.0)) > 0.5

            @pl.when(ranked)
            def _():
                pair(j, None, None, cum0, cum1)

            @pl.when(jnp.logical_not(ranked))
            def _():
                pair(j, jnp.where(all0, lo, hi), jnp.where(all1, lo, hi))
            return carry
        lax.fori_loop(0, npair, pair_body, 0)

    @pl.when(need_ties)
    def _():
        run(True)

    @pl.when(jnp.logical_not(need_ties))
    def _():
        run(False)

    first_head = lax.broadcasted_iota(I32, (LANES, tq), 0) < ATTN_HEAD_DIM
    normed = [acc_sc[h, 0:LANES, :] / acc_sc[h, LANES:LANES + 1, :] for h in range(ATTN_HEADS)]
    for p in range(ATTN_HEADS // 2):
        pair_t = jnp.where(first_head, normed[2 * p], normed[2 * p + 1])
        o_ref[:, LANES * p:LANES * (p + 1)] = jnp.transpose(pair_t)


def _dsa_call(rel_bias, qi, wi, q, ki2, k, v, *, tq):
    s = q.shape[0]
    assert tq > REL_MAX_DIST and s % tq == 0
    nb = s // tq
    topk = min(TOPK_MAX, s // 4)
    kern = functools.partial(_dsa_kernel, tq=tq, topk=topk,
                             idx_scale=(IDX_DIM ** -0.5) * (IDX_HEADS ** -0.5))
    whole = lambda width: pl.BlockSpec((s, width), lambda i: (0, 0), pipeline_mode=pl.Buffered(1))
    n_pairs = ATTN_WIDTH // LANES
    vt = jnp.transpose(v.reshape(nb, tq, n_pairs, LANES), (0, 2, 3, 1))
    vt = jnp.concatenate([vt, jnp.ones((nb, n_pairs, _VT_ROWS - LANES, tq), BF16)], axis=2)
    vt = vt.reshape(nb, n_pairs * _VT_ROWS, tq)
    return pl.pallas_call(
        kern,
        grid=(nb,),
        in_specs=[pl.BlockSpec(memory_space=pltpu.SMEM),
                  pl.BlockSpec((tq, IDX_HEADS * IDX_DIM), lambda i: (i, 0)),
                  pl.BlockSpec((tq, LANES), lambda i: (i, 0)),
                  pl.BlockSpec((tq, ATTN_WIDTH), lambda i: (i, 0)),
                  whole(2 * IDX_DIM), whole(ATTN_WIDTH),
                  pl.BlockSpec(vt.shape, lambda i: (0, 0, 0), pipeline_mode=pl.Buffered(1))],
        out_specs=pl.BlockSpec((tq, ATTN_WIDTH), lambda i: (i, 0)),
        out_shape=jax.ShapeDtypeStruct((s, ATTN_WIDTH), F32),
        scratch_shapes=[pltpu.VMEM((nb + nb % 2, tq, tq), F32),
                        pltpu.VMEM((ATTN_HEADS, _N_KINDS, tq, tq), BF16),
                        pltpu.VMEM((3, tq, tq), F32),
                        pltpu.VMEM((2, ATTN_HEADS, tq, tq), BF16),
                        pltpu.VMEM((IDX_HEADS, tq, LANES), BF16),
                        pltpu.VMEM((ATTN_HEADS, tq, LANES), BF16),
                        pltpu.VMEM((ATTN_HEADS, 1, tq), F32),
                        pltpu.VMEM((ATTN_HEADS, _VT_ROWS, tq), F32),
                        pltpu.VMEM((1, tq), F32),
                        pltpu.VMEM((nb + nb % 2, 1, tq), F32)],
        compiler_params=_cparams("arbitrary"),
        name="dsa_attention",
    )(rel_bias, qi, wi, q, ki2, k, vt)


_HGRN_CHUNK = 128
_HGRN_BASE = SUBLANES


def _hgrn_kernel(h4_ref, lb_ref, ng_ref, o_ref, st_sc, *, layer, rows):
    c_len = _HGRN_CHUNK

    @pl.when(pl.program_id(0) == 0)
    def _():
        st_sc[...] = jnp.zeros_like(st_sc)

    lb_all = lb_ref[...]
    e = jnp.exp(lb_all - jnp.max(lb_all, axis=0, keepdims=True))
    sm = e / jnp.sum(e, axis=0, keepdims=True)
    lbs = jnp.sum(sm[0:layer + 1, :], axis=0, keepdims=True) - sm[0:1, :]

    r1 = lax.broadcasted_iota(I32, (c_len, 1), 0)
    rr = lax.broadcasted_iota(I32, (c_len, c_len), 0)
    cc = lax.broadcasted_iota(I32, (c_len, c_len), 1)

    def block_ref(a, blk, r):
        x = a.reshape(c_len // blk, blk, a.shape[-1])[:, r:r + 1, :]
        return jnp.broadcast_to(x, (c_len // blk, blk, a.shape[-1])).reshape(c_len, a.shape[-1])

    w = HGRN_WIDTH
    for c in range(rows // c_len):
        rs = slice(c * c_len, (c + 1) * c_len)
        hq = h4_ref[rs, 0:w]
        z = h4_ref[rs, w:2 * w]
        val = h4_ref[rs, 2 * w:3 * w]
        hg = h4_ref[rs, 3 * w:4 * w]
        key = (1.0 - lbs) * jax.nn.sigmoid(-z)
        g = jnp.log1p(-jnp.minimum(key, KEY_MAX))
        a_cum = g
        sh = 1
        while sh < c_len:
            a_cum = a_cum + jnp.where(r1 >= sh, pltpu.roll(a_cum, sh, 0), 0.0)
            sh *= 2
        qf = _silu(hq)
        outs = []
        for h in range(HGRN_HEADS):
            hs = slice(h * HGRN_DIM, (h + 1) * HGRN_DIM)
            ah, qh, kh, vh = a_cum[:, hs], qf[:, hs], key[:, hs], val[:, hs]
            vh16 = vh.astype(BF16)
            st = st_sc[h]
            o = _dot_nt((qh * jnp.exp(ah)).astype(BF16), st.astype(BF16))
            ref = block_ref(ah, _HGRN_BASE, _HGRN_BASE // 2 - 1)
            sc = _dot_nt((qh * jnp.exp(ah - ref)).astype(BF16), (kh * jnp.exp(ref - ah)).astype(BF16))
            same = (rr >> _log2(_HGRN_BASE)) == (cc >> _log2(_HGRN_BASE))
            scores = jnp.where(jnp.logical_and(same, cc <= rr), sc, 0.0)
            half = _HGRN_BASE
            while half < c_len:
                blk = 2 * half
                ref = block_ref(ah, blk, half - 1)
                second = (r1 & (blk - 1)) >= half
                ql = jnp.where(second, qh * jnp.exp(jnp.minimum(ah - ref, 0.0)), 0.0)
                kl = jnp.where(second, 0.0, kh * jnp.exp(jnp.minimum(ref - ah, 0.0)))
                sc = _dot_nt(ql.astype(BF16), kl.astype(BF16))
                scores = scores + jnp.where((rr >> _log2(blk)) == (cc >> _log2(blk)), sc, 0.0)
                half = blk
            o = o + _dot(scores.astype(BF16), vh16)
            a_last = ah[c_len - 1:c_len, :]
            kd = (kh * jnp.exp(a_last - ah)).astype(BF16)
            st_sc[h] = st * jnp.exp(a_last) + _dot_tn(vh16, kd)
            o = o * lax.rsqrt(jnp.mean(o * o, axis=-1, keepdims=True) + RMS_EPS)
            outs.append(o)
        o_ref[rs, :] = jnp.concatenate(outs, axis=1) * ng_ref[...] * _silu(hg)


def _hgrn_call(h4, lb, ng, *, layer, rows):
    s = h4.shape[0]
    kern = functools.partial(_hgrn_kernel, layer=layer, rows=rows)
    return pl.pallas_call(
        kern,
        grid=(s // rows,),
        in_specs=[pl.BlockSpec((rows, 4 * HGRN_WIDTH), lambda i: (i, 0)),
                  pl.BlockSpec(lb.shape, lambda i: (0, 0)),
                  pl.BlockSpec((1, HGRN_WIDTH), lambda i: (0, 0))],
        out_specs=pl.BlockSpec((rows, HGRN_WIDTH), lambda i: (i, 0)),
        out_shape=jax.ShapeDtypeStruct((s, HGRN_WIDTH), F32),
        scratch_shapes=[pltpu.VMEM((HGRN_HEADS, HGRN_DIM, HGRN_DIM), F32)],
        compiler_params=_cparams("arbitrary"),
        name="hgrn2",
    )(h4, lb, ng)


_POOL_HALO = 16


def _merge_kernel(x_ref, mod_ref, a_ref, ah_ref, yb_ref, yc_ref, g_ref, pw_ref, ps_ref, wb_ref, wo_ref,
                  lng_ref, lnb_ref, o_ref, *, tm, alpha):
    i = pl.program_id(0)
    d = x_ref.shape[1]
    _, _, gate = _mod_rows(mod_ref, 1)
    a = a_ref[...]
    halo = jnp.where(i == 0, 0.0, ah_ref[...])
    ext = jnp.concatenate([halo, a], axis=0)
    lane = lax.broadcasted_iota(I32, (tm, POOL_WIDTH), 1)
    grp = lane >> _log2(POOL_GROUP)
    pos1 = (i * tm + lax.broadcasted_iota(I32, (tm, POOL_WIDTH), 0) + 1).astype(F32)
    win = jnp.zeros((tm, POOL_WIDTH), F32)
    cnt = jnp.zeros((tm, POOL_WIDTH), F32)
    ssum, have = ext, 1
    for gi, wd in enumerate(POOL_WINDOWS):
        while have < wd:
            ssum = ssum + pltpu.roll(ssum, have, 0)
            have *= 2
        assert have == wd
        win = jnp.where(grp == gi, ssum[_POOL_HALO:, :], win)
        cnt = jnp.where(grp == gi, jnp.minimum(pos1, float(wd)), cnt)
    dlt = (win / cnt - a).astype(BF16)
    ya = _dot(dlt, pw_ref[...]) * ps_ref[...]
    za = _dot(ya.astype(BF16), wb_ref[0:POOL_WIDTH, :])
    zb = _dot(yb_ref[...].astype(BF16), wb_ref[POOL_WIDTH:POOL_WIDTH + ATTN_WIDTH, :])
    zc = _dot(yc_ref[...].astype(BF16), wb_ref[POOL_WIDTH + ATTN_WIDTH:, :])
    merged = (jax.nn.sigmoid(g_ref[:, 0:d]) * za + jax.nn.sigmoid(g_ref[:, d:2 * d]) * zb
              + jax.nn.sigmoid(g_ref[:, 2 * d:3 * d]) * zc)
    y = _dot(merged.astype(BF16), wo_ref[...])
    z = alpha * x_ref[...] + (1.0 + gate) * y
    o_ref[...] = _layer_norm(z, lng_ref[1:2, :], lnb_ref[1:2, :])


def _merge_call(x, mod, a, yb, yc, gates, pw_bd, ps, wb, wo, ln_g, ln_b, *, tm, alpha):
    s, d = x.shape
    kern = functools.partial(_merge_kernel, tm=tm, alpha=alpha)
    full = lambda arr: pl.BlockSpec(arr.shape, lambda i: (0, 0))
    hb = tm // _POOL_HALO
    return pl.pallas_call(
        kern,
        grid=(s // tm,),
        in_specs=[pl.BlockSpec((tm, d), lambda i: (i, 0)),
                  full(mod),
                  pl.BlockSpec((tm, POOL_WIDTH), lambda i: (i, 0)),
                  pl.BlockSpec((_POOL_HALO, POOL_WIDTH), lambda i: (jnp.maximum(i * hb - 1, 0), 0)),
                  pl.BlockSpec((tm, ATTN_WIDTH), lambda i: (i, 0)),
                  pl.BlockSpec((tm, HGRN_WIDTH), lambda i: (i, 0)),
                  pl.BlockSpec((tm, 3 * d), lambda i: (i, 0)),
                  full(pw_bd), full(ps), full(wb), full(wo), full(ln_g), full(ln_b)],
        out_specs=pl.BlockSpec((tm, d), lambda i: (i, 0)),
        out_shape=jax.ShapeDtypeStruct((s, d), F32),
        compiler_params=_cparams("arbitrary"),
        name="pool_merge",
    )(x, mod, a, a, yb, yc, gates, pw_bd, ps, wb, wo, ln_g, ln_b)


def _pool_blockdiag(pool_w):
    n = pool_w.shape[0]
    rows = [jnp.concatenate([pool_w[g] if g == j else jnp.zeros_like(pool_w[g]) for j in range(n)], axis=1)
            for g in range(n)]
    return jnp.concatenate(rows, axis=0)


def kernel(x, c, w_ada, b_ada, ln_g, ln_b, ffn_w_in, ffn_w_out, mix_w_in, pool_w, pool_scale, rel_bias,
           hgrn_lb, hgrn_norm_g, w_branch, w_out):
    batch, s, d = x.shape
    assert batch == 1, "kernels are written for a single sequence"
    depth = w_ada.shape[0]
    alpha = (2 * depth) ** 0.25
    tm_ffn = min(512, s)
    tf = 256
    tm_proj = min(256, s)
    tq = min(256, s)
    tm_merge = min(256, s)
    hgrn_rows = min(256, s)

    mods = _ada_call(c, w_ada, b_ada)
    xs = x[0]
    for l in range(depth):
        mod = mods[l].reshape(3 * N_SUB, d)
        xs = _ffn_call(xs, mod, ffn_w_in[l, 0].astype(BF16), ffn_w_out[l, 0].astype(BF16), ln_g[l], ln_b[l],
                       sub=0, alpha=alpha, tm=tm_ffn, tf=tf)
        gates, a, h4, qi, q, k, v, ki2, wi = _proj_call(xs, mod, _proj_weight(mix_w_in[l]), tm=tm_proj)
        yb = _dsa_call(rel_bias, qi, wi, q, ki2, k, v, tq=tq)
        yc = _hgrn_call(h4, hgrn_lb, hgrn_norm_g[l].reshape(1, -1), layer=l, rows=hgrn_rows)
        xs = _merge_call(xs, mod, a, yb, yc, gates, _pool_blockdiag(pool_w[l]).astype(BF16),
                         pool_scale[l].reshape(1, -1), w_branch[l].astype(BF16), w_out[l].astype(BF16),
                         ln_g[l], ln_b[l], tm=tm_merge, alpha=alpha)
        xs = _ffn_call(xs, mod, ffn_w_in[l, 1].astype(BF16), ffn_w_out[l, 1].astype(BF16), ln_g[l], ln_b[l],
                       sub=2, alpha=alpha, tm=tm_ffn, tf=tf)
    return xs[None]
```

```python
import functools
import math

import jax
import jax.numpy as jnp
from jax import lax
from jax.experimental import pallas as pl
from jax.experimental.pallas import tpu as pltpu

F32 = jnp.float32
BF16 = jnp.bfloat16
I32 = jnp.int32
HIGHEST = lax.Precision.HIGHEST

POOL_WINDOWS = (2, 4, 8, 16)
POOL_GROUP = 64
POOL_WIDTH = POOL_GROUP * len(POOL_WINDOWS)
ATTN_HEADS = 4
ATTN_HEAD_DIM = 64
ATTN_WIDTH = ATTN_HEADS * ATTN_HEAD_DIM
IDX_HEADS = 8
IDX_DIM = 64
TOPK_MAX = 256
HGRN_HEADS = 4
HGRN_DIM = 128
HGRN_WIDTH = HGRN_HEADS * HGRN_DIM
KEY_MAX = 1.0 - 1e-6
REL_BUCKETS = 32
REL_MAX_DIST = 128
N_SUB = 3
LN_EPS = 1e-5
RMS_EPS = 1e-6

LANES = 128
SUBLANES = 8
VMEM_LIMIT_BYTES = 58 * 1024 * 1024

NEG_INF = float("-inf")
POS_INF = float("inf")


def _cparams(*sem):
    return pltpu.CompilerParams(dimension_semantics=sem, vmem_limit_bytes=VMEM_LIMIT_BYTES)


def _dot(a, b):
    return jnp.dot(a, b, preferred_element_type=F32)


def _dot_nt(a, b):
    return lax.dot_general(a, b, (((1,), (1,)), ((), ())), preferred_element_type=F32)


def _dot_tn(a, b):
    return lax.dot_general(a, b, (((0,), (0,)), ((), ())), preferred_element_type=F32)


def _log2(n):
    assert n & (n - 1) == 0
    return n.bit_length() - 1


def _silu(x):
    return x * jax.nn.sigmoid(x)


def _mod_rows(mod_ref, j):
    return mod_ref[3 * j:3 * j + 1, :], mod_ref[3 * j + 1:3 * j + 2, :], mod_ref[3 * j + 2:3 * j + 3, :]


def _layer_norm(z, g, b):
    mu = jnp.mean(z, axis=-1, keepdims=True)
    zc = z - mu
    var = jnp.mean(zc * zc, axis=-1, keepdims=True)
    return zc * lax.rsqrt(var + LN_EPS) * g + b


def _ada_kernel(c_ref, w_ref, b_ref, o_ref):
    c = c_ref[...]
    r = jnp.dot(_silu(c), w_ref[0], precision=HIGHEST, preferred_element_type=F32)
    o_ref[0] = r[0:1, :] + b_ref[0]


def _ada_call(c, w_ada, b_ada):
    depth, d, n = w_ada.shape
    tn = n // 8
    c8 = jnp.broadcast_to(c, (SUBLANES, d))
    return pl.pallas_call(
        _ada_kernel,
        grid=(depth, n // tn),
        in_specs=[pl.BlockSpec((SUBLANES, d), lambda l, j: (0, 0)),
                  pl.BlockSpec((1, d, tn), lambda l, j: (l, 0, j)),
                  pl.BlockSpec((1, 1, tn), lambda l, j: (l, 0, j))],
        out_specs=pl.BlockSpec((1, 1, tn), lambda l, j: (l, 0, j)),
        out_shape=jax.ShapeDtypeStruct((depth, 1, n), F32),
        compiler_params=_cparams("arbitrary", "arbitrary"),
        name="ada_mod",
    )(c8, w_ada, b_ada.reshape(depth, 1, n))


def _ffn_kernel(x_ref, mod_ref, wi_ref, wo_ref, lng_ref, lnb_ref, o_ref, *, sub, tf, alpha):
    shift, scale, gate = _mod_rows(mod_ref, sub)
    ff = wo_ref.shape[0]
    x = x_ref[...]
    h = (x * (1.0 + scale) + shift).astype(BF16)
    y = jnp.zeros(x.shape, F32)
    for c0 in range(0, ff, tf):
        g = _dot(h, wi_ref[:, c0:c0 + tf])
        u = _dot(h, wi_ref[:, ff + c0:ff + c0 + tf])
        y = y + _dot((_silu(g) * u).astype(BF16), wo_ref[c0:c0 + tf, :])
    z = alpha * x + (0.5 * (1.0 + gate)) * y
    o_ref[...] = _layer_norm(z, lng_ref[sub:sub + 1, :], lnb_ref[sub:sub + 1, :])


def _ffn_call(x, mod, w_in, w_out, ln_g, ln_b, *, sub, alpha, tm, tf):
    s, d = x.shape
    ff = w_out.shape[0]
    assert ff % tf == 0
    kern = functools.partial(_ffn_kernel, sub=sub, tf=tf, alpha=alpha)
    resident = lambda arr: pl.BlockSpec(arr.shape, lambda i: (0, 0), pipeline_mode=pl.Buffered(1))
    return pl.pallas_call(
        kern,
        grid=(s // tm,),
        in_specs=[pl.BlockSpec((tm, d), lambda i: (i, 0)),
                  pl.BlockSpec((3 * N_SUB, d), lambda i: (0, 0)),
                  resident(w_in), resident(w_out),
                  pl.BlockSpec((N_SUB, d), lambda i: (0, 0)),
                  pl.BlockSpec((N_SUB, d), lambda i: (0, 0))],
        out_specs=pl.BlockSpec((tm, d), lambda i: (i, 0)),
        out_shape=jax.ShapeDtypeStruct((s, d), F32),
        compiler_params=_cparams("arbitrary"),
        name="ffn",
    )(x, mod, w_in, w_out, ln_g, ln_b)


_PROJ_OUTS = (("gates", 3 * 1024, F32), ("a", POOL_WIDTH, F32), ("h4", 4 * HGRN_WIDTH, F32),
              ("qi", IDX_HEADS * IDX_DIM, BF16), ("q", ATTN_WIDTH, BF16), ("k", ATTN_WIDTH, BF16),
              ("v", ATTN_WIDTH, BF16), ("ki2", 2 * IDX_DIM, BF16), ("wi", LANES, F32))
_PROJ_CHUNK = 1024


def _proj_kernel(x_ref, mod_ref, w_ref, *o_refs):
    shift, scale, _ = _mod_rows(mod_ref, 1)
    h = (x_ref[...] * (1.0 + scale) + shift).astype(BF16)
    c0 = 0
    for (_, width, dt), o_ref in zip(_PROJ_OUTS, o_refs):
        for cc in range(0, width, _PROJ_CHUNK):
            cw = min(_PROJ_CHUNK, width - cc)
            o_ref[:, cc:cc + cw] = _dot(h, w_ref[:, c0 + cc:c0 + cc + cw]).astype(dt)
        c0 += width


def _proj_weight(w):
    d = w.shape[0]
    widths = (POOL_WIDTH, ATTN_WIDTH, ATTN_WIDTH, ATTN_WIDTH, IDX_HEADS * IDX_DIM, IDX_DIM, IDX_HEADS,
              HGRN_WIDTH, HGRN_WIDTH, HGRN_WIDTH, HGRN_WIDTH, 3 * d)
    parts, c0 = [], 0
    for wd in widths:
        parts.append(w[:, c0:c0 + wd])
        c0 += wd
    a, q, k, v, qi, ki, wi, hq, hf, hi, hg, gates = parts
    wi_pad = jnp.pad(wi, ((0, 0), (0, LANES - IDX_HEADS)))
    q = q * (ATTN_HEAD_DIM ** -0.5 * _LOG2E)
    return jnp.concatenate([gates, a, hq, hf, hi, hg, qi, q, k, v, ki, ki, wi_pad], axis=1).astype(BF16)


def _proj_call(x, mod, w, *, tm):
    s, d = x.shape
    n = w.shape[1]
    assert n == sum(wd for _, wd, _ in _PROJ_OUTS)
    return pl.pallas_call(
        _proj_kernel,
        grid=(s // tm,),
        in_specs=[pl.BlockSpec((tm, d), lambda i: (i, 0)),
                  pl.BlockSpec((3 * N_SUB, d), lambda i: (0, 0)),
                  pl.BlockSpec((d, n), lambda i: (0, 0))],
        out_specs=[pl.BlockSpec((tm, wd), lambda i: (i, 0)) for _, wd, _ in _PROJ_OUTS],
        out_shape=[jax.ShapeDtypeStruct((s, wd), dt) for _, wd, dt in _PROJ_OUTS],
        compiler_params=_cparams("arbitrary"),
        name="mix_proj",
    )(x, mod, w)


_BISECT_MAX_ITERS = 320
_F32_TINY = float(jnp.finfo(jnp.float32).tiny)
_COARSE_PASSES = 8
_COARSE_MIN_ABS, _COARSE_MAX_ABS = 1e-30, 1e30
_KIND_FAR, _KIND_PAD, _N_KINDS = 2, 3, 4
_VT_ROWS = LANES + 16
_SMASK_FULL, _SMASK_DIAG, _SMASK_PAD = 0, 1, 2
_LOG2E = math.log2(math.e)


def _t5_bucket(dist):
    max_exact = REL_BUCKETS // 2
    d32 = jnp.maximum(dist, 1).astype(F32)
    large = max_exact + (jnp.log(d32 / max_exact) / math.log(REL_MAX_DIST / max_exact)
                         * (REL_BUCKETS - max_exact)).astype(I32)
    large = jnp.minimum(large, REL_BUCKETS - 1)
    return jnp.where(dist < max_exact, dist, large)


def _dsa_kernel(relb_ref, qi_ref, wi_ref, q_ref, ki2_ref, k_ref, vt_ref, o_ref,
                s_sc, sb_sc, bias_sc, smask_sc, lg_sc, qim_sc, qm_sc, m_sc, acc_sc, tie_sc, cnt_sc,
                *, tq, topk, idx_scale):
    tk = tq
    i = pl.program_id(0)
    krow = lax.broadcasted_iota(I32, (tk, tq), 0)
    qcol = lax.broadcasted_iota(I32, (tk, tq), 1)
    causal = krow <= qcol
    lane = lax.broadcasted_iota(I32, (tq, LANES), 1)
    lo_half = lane < ATTN_HEAD_DIM

    @pl.when(i == 0)
    def _():
        for which in range(2):
            bucket = _t5_bucket(jnp.maximum(qcol - krow + which * tq, 0))
            for h in range(ATTN_HEADS):
                b = jnp.zeros((tk, tq), F32)
                for bk in range(REL_BUCKETS):
                    b = jnp.where(bucket == bk, relb_ref[bk, h] * _LOG2E, b)
                bias_sc[h, which] = (jnp.where(causal, b, NEG_INF) if which == 0 else b).astype(BF16)
        for h in range(ATTN_HEADS):
            bias_sc[h, _KIND_FAR] = jnp.full((tk, tq), relb_ref[REL_BUCKETS - 1, h] * _LOG2E, F32).astype(BF16)
            bias_sc[h, _KIND_PAD] = jnp.full((tk, tq), NEG_INF, BF16)
        smask_sc[_SMASK_FULL] = jnp.zeros((tk, tq), F32)
        smask_sc[_SMASK_DIAG] = jnp.where(causal, 0.0, NEG_INF)
        smask_sc[_SMASK_PAD] = jnp.full((tk, tq), NEG_INF, F32)

    qi = qi_ref[...]
    for h in range(IDX_HEADS):
        pair = qi[:, LANES * (h // 2):LANES * (h // 2 + 1)]
        qim_sc[h] = jnp.where(lo_half if h % 2 == 0 else ~lo_half, pair, jnp.zeros_like(pair))
    qs = q_ref[...]
    for h in range(ATTN_HEADS):
        pair = qs[:, LANES * (h // 2):LANES * (h // 2 + 1)]
        qm_sc[h] = jnp.where(lo_half if h % 2 == 0 else ~lo_half, pair, jnp.zeros_like(pair))

    wt = jnp.transpose(wi_ref[...])

    def key_rows(jb):
        return pl.ds(pl.multiple_of(jb * tk, tk), tk)

    def fold(x):
        return x.reshape(tk // SUBLANES, SUBLANES, tq)

    def index_scores(jb):
        kb = ki2_ref[key_rows(jb), :]
        acc = jnp.zeros((tk, tq), F32)
        for h in range(IDX_HEADS):
            acc = acc + wt[h:h + 1, :] * jnp.maximum(_dot_nt(kb, qim_sc[h]), 0.0)
        return acc * idx_scale

    npair = (i + 2) // 2
    last_kv = k_ref.shape[0] // tk - 1

    def p1_pair(j, carry):
        rmin, rmax = carry
        for jb in (2 * j, 2 * j + 1):
            raw = index_scores(jnp.minimum(jb, last_kv))
            kind = jnp.where(jb > i, _SMASK_PAD, jnp.where(jb == i, _SMASK_DIAG, _SMASK_FULL))
            sc = raw + smask_sc[kind]
            s_sc[jb] = sc
            sb_sc[jb] = sc.astype(BF16)
            rmin = jnp.minimum(rmin, jnp.min(fold(raw), axis=0))
            rmax = jnp.maximum(rmax, jnp.max(fold(raw), axis=0))
        return rmin, rmax

    rmin, rmax = lax.fori_loop(0, npair, p1_pair, (jnp.full((SUBLANES, tq), POS_INF, F32),
                                                   jnp.full((SUBLANES, tq), NEG_INF, F32)))
    rowmin = jnp.min(rmin, axis=0, keepdims=True)
    rowmax = jnp.max(rmax, axis=0, keepdims=True)

    kf = float(topk)
    nfin = (i * tq + lax.broadcasted_iota(I32, (1, tq), 1) + 1).astype(F32)
    small = nfin <= kf

    def count_ge(x):
        def body(j, cnt):
            for jb in (2 * j, 2 * j + 1):
                cnt = cnt + jnp.sum(fold(jnp.where(s_sc[jb] >= x, 1.0, 0.0)), axis=0)
            return cnt
        cnt = lax.fori_loop(0, npair, body, jnp.zeros((SUBLANES, tq), F32))
        return jnp.sum(cnt, axis=0, keepdims=True)

    def count_ge16(xb):
        one, zero = jnp.ones((), BF16), jnp.zeros((), BF16)
        rows16 = 2 * SUBLANES

        def body(j, cnt):
            parts = []
            for jb in (2 * j, 2 * j + 1):
                c = jnp.where(sb_sc[jb] >= xb, one, zero)
                parts += [c[r0:r0 + rows16, :] for r0 in range(0, tk, rows16)]
            while len(parts) > 1:
                parts = [a + b for a, b in zip(parts[0::2], parts[1::2])]
            return cnt + parts[0].astype(F32)
        cnt = lax.fori_loop(0, npair, body, jnp.zeros((rows16, tq), F32))
        return jnp.sum(cnt, axis=0, keepdims=True)

    def coarse_probe(mid):
        xb = mid.astype(BF16)
        xbits = lax.bitcast_convert_type(xb.astype(F32), I32)
        neg = xbits < 0
        mu = jnp.where(neg, xbits + 0x8000, xbits - 0x8000)
        odd = (xbits & 0x10000) != 0
        tbits = jnp.where(odd, jnp.where(neg, mu - 1, mu + 1), mu)
        t = lax.bitcast_convert_type(tbits, F32)
        ok = jnp.logical_and(jnp.abs(xb.astype(F32)) > _COARSE_MIN_ABS, jnp.abs(xb.astype(F32)) < _COARSE_MAX_ABS)
        return xb, t, ok

    def bis_cond(st):
        it, _, _, _, _, done = st
        return jnp.logical_and(it < _BISECT_MAX_ITERS, jnp.min(done) < 0.5)

    def bis_body(st):
        it, lo, hi, clo, chi, done = st
        mid = 0.5 * lo + 0.5 * hi
        mid = jnp.where(jnp.logical_and(lo == 0.0, hi > _F32_TINY), _F32_TINY, mid)
        mid = jnp.where(jnp.logical_and(lo < 0.0, hi > 0.0), 0.0, mid)
        coarse = it < _COARSE_PASSES
        xb, t, ok = coarse_probe(mid)
        ok = jnp.logical_and(ok, jnp.logical_and(t > lo, t < hi))
        mid = jnp.where(coarse, t, mid)
        stuck = jnp.logical_and(jnp.logical_not(coarse),
                                jnp.logical_or(jnp.logical_or(mid <= lo, mid >= hi),
                                               jnp.logical_and(lo == 0.0, hi <= _F32_TINY)))
        c = lax.cond(coarse, lambda: count_ge16(xb), lambda: count_ge(mid))
        active = jnp.logical_and(done < 0.5, jnp.logical_not(stuck))
        active = jnp.logical_and(active, jnp.logical_or(jnp.logical_not(coarse), ok))
        up = jnp.logical_and(active, c >= kf)
        dn = jnp.logical_and(active, c < kf)
        lo = jnp.where(up, mid, lo)
        clo = jnp.where(up, c, clo)
        hi = jnp.where(dn, mid, hi)
        chi = jnp.where(dn, c, chi)
        fin = jnp.logical_or(stuck, jnp.logical_and(active, c == kf))
        done = jnp.where(fin, 1.0, done)
        return it + 1, lo, hi, clo, chi, done

    mbits = lax.bitcast_convert_type(rowmax, I32)
    above_max = jnp.where(rowmax == 0.0, _F32_TINY,
                          lax.bitcast_convert_type(jnp.where(rowmax > 0.0, mbits + 1, mbits - 1), F32))
    st0 = (jnp.int32(0), jnp.where(small, NEG_INF, rowmin), above_max, nfin,
           jnp.zeros((1, tq), F32), jnp.where(small, 1.0, 0.0))
    _, lo, hi, clo, chi, _ = lax.while_loop(bis_cond, bis_body, st0)
    budget = kf - chi
    need_ties = jnp.max((clo - chi) - budget) > 0.5

    pairs = [slice(LANES * (h // 2), LANES * (h // 2 + 1)) for h in range(ATTN_HEADS)]

    def issue_qk(jb, slot):
        kb = k_ref[key_rows(jnp.minimum(jb, last_kv)), :]
        for h in range(ATTN_HEADS):
            lg_sc[slot, h] = _dot_nt(kb[:, pairs[h]], qm_sc[h]).astype(BF16)

    def attend(jb, slot, thr, cum=None):
        s = s_sc[jb]
        if cum is None:
            sel = s >= thr
        else:
            cand = jnp.logical_and(s >= lo, s < hi)
            before = (qcol < krow).astype(BF16)
            rank = cum + _dot(before, jnp.where(cand, 1.0, 0.0).astype(BF16))
            sel = jnp.logical_or(s >= hi, jnp.logical_and(cand, rank < budget))
        kind = jnp.where(jb > i, _KIND_PAD, jnp.minimum(i - jb, _KIND_FAR))
        selm = jnp.where(sel, 0.0, NEG_INF).astype(BF16)
        probs, alphas = [], []
        for h in range(ATTN_HEADS):
            lg = lg_sc[slot, h] + bias_sc[h, kind] + selm
            m_old = m_sc[h]
            m_new = jnp.maximum(m_old, jnp.max(lg, axis=0, keepdims=True).astype(F32))
            m_safe = jnp.where(m_new == NEG_INF, 0.0, m_new)
            alphas.append(jnp.exp2(m_old - m_safe))
            probs.append(jnp.exp2(lg - m_safe.astype(BF16)))
            m_sc[h] = m_new
        vb = vt_ref[jnp.minimum(jb, last_kv)]
        for h in range(ATTN_HEADS):
            vrows = slice(_VT_ROWS * (h // 2), _VT_ROWS * (h // 2 + 1))
            acc_sc[h] = alphas[h] * acc_sc[h] + _dot(vb[vrows, :], probs[h])

    def run(with_ties):
        m_sc[...] = jnp.full(m_sc.shape, NEG_INF, F32)
        acc_sc[...] = jnp.zeros_like(acc_sc)
        tie_sc[...] = jnp.zeros_like(tie_sc)
        if with_ties:
            def cand_body(j, carry):
                for jb in (2 * j, 2 * j + 1):
                    s = s_sc[jb]
                    cand = jnp.logical_and(s >= lo, s < hi)
                    cnt_sc[jb] = jnp.sum(jnp.where(cand, 1.0, 0.0), axis=0, keepdims=True)
                return carry
            lax.fori_loop(0, npair, cand_body, 0)
        issue_qk(0, 0)

        def pair(j, thr0, thr1, cum0=None, cum1=None):
            issue_qk(2 * j + 1, 1)
            attend(2 * j, 0, thr0, cum0)
            issue_qk(2 * j + 2, 0)
            attend(2 * j + 1, 1, thr1, cum1)

        def pair_body(j, carry):
            if not with_ties:
                pair(j, lo, lo)
                return carry
            cum0 = tie_sc[...]
            cum1 = cum0 + cnt_sc[2 * j]
            cum2 = cum1 + cnt_sc[2 * j + 1]
            tie_sc[...] = cum2
            all0, all1 = cum1 <= budget, cum2 <= budget
            runs_out = jnp.logical_or(jnp.logical_and(jnp.logical_not(all0), cum0 < budget),
                                      jnp.logical_and(jnp.logical_not(all1), cum1 < budget))
            ranked = jnp.max(jnp.where(runs_out, 1.0, 0.0)) > 0.5

            @pl.when(ranked)
            def _():
                pair(j, None, None, cum0, cum1)

            @pl.when(jnp.logical_not(ranked))
            def _():
                pair(j, jnp.where(all0, lo, hi), jnp.where(all1, lo, hi))
            return carry
        lax.fori_loop(0, npair, pair_body, 0)

    @pl.when(need_ties)
    def _():
        run(True)

    @pl.when(jnp.logical_not(need_ties))
    def _():
        run(False)

    first_head = lax.broadcasted_iota(I32, (LANES, tq), 0) < ATTN_HEAD_DIM
    normed = [acc_sc[h, 0:LANES, :] / acc_sc[h, LANES:LANES + 1, :] for h in range(ATTN_HEADS)]
    for p in range(ATTN_HEADS // 2):
        pair_t = jnp.where(first_head, normed[2 * p], normed[2 * p + 1])
        o_ref[:, LANES * p:LANES * (p + 1)] = jnp.transpose(pair_t)


def _dsa_call(rel_bias, qi, wi, q, ki2, k, v, *, tq):
    s = q.shape[0]
    assert tq > REL_MAX_DIST and s % tq == 0
    nb = s // tq
    topk = min(TOPK_MAX, s // 4)
    kern = functools.partial(_dsa_kernel, tq=tq, topk=topk,
                             idx_scale=(IDX_DIM ** -0.5) * (IDX_HEADS ** -0.5))
    whole = lambda width: pl.BlockSpec((s, width), lambda i: (0, 0), pipeline_mode=pl.Buffered(1))
    n_pairs = ATTN_WIDTH // LANES
    vt = jnp.transpose(v.reshape(nb, tq, n_pairs, LANES), (0, 2, 3, 1))
    vt = jnp.concatenate([vt, jnp.ones((nb, n_pairs, _VT_ROWS - LANES, tq), BF16)], axis=2)
    vt = vt.reshape(nb, n_pairs * _VT_ROWS, tq)
    return pl.pallas_call(
        kern,
        grid=(nb,),
        in_specs=[pl.BlockSpec(memory_space=pltpu.SMEM),
                  pl.BlockSpec((tq, IDX_HEADS * IDX_DIM), lambda i: (i, 0)),
                  pl.BlockSpec((tq, LANES), lambda i: (i, 0)),
                  pl.BlockSpec((tq, ATTN_WIDTH), lambda i: (i, 0)),
                  whole(2 * IDX_DIM), whole(ATTN_WIDTH),
                  pl.BlockSpec(vt.shape, lambda i: (0, 0, 0), pipeline_mode=pl.Buffered(1))],
        out_specs=pl.BlockSpec((tq, ATTN_WIDTH), lambda i: (i, 0)),
        out_shape=jax.ShapeDtypeStruct((s, ATTN_WIDTH), F32),
        scratch_shapes=[pltpu.VMEM((nb + nb % 2, tq, tq), F32),
                        pltpu.VMEM((nb + nb % 2, tq, tq), BF16),
                        pltpu.VMEM((ATTN_HEADS, _N_KINDS, tq, tq), BF16),
                        pltpu.VMEM((3, tq, tq), F32),
                        pltpu.VMEM((2, ATTN_HEADS, tq, tq), BF16),
                        pltpu.VMEM((IDX_HEADS, tq, LANES), BF16),
                        pltpu.VMEM((ATTN_HEADS, tq, LANES), BF16),
                        pltpu.VMEM((ATTN_HEADS, 1, tq), F32),
                        pltpu.VMEM((ATTN_HEADS, _VT_ROWS, tq), F32),
                        pltpu.VMEM((1, tq), F32),
                        pltpu.VMEM((nb + nb % 2, 1, tq), F32)],
        compiler_params=_cparams("arbitrary"),
        name="dsa_attention",
    )(rel_bias, qi, wi, q, ki2, k, vt)


_HGRN_CHUNK = 128
_HGRN_BASE = SUBLANES


def _hgrn_kernel(h4_ref, lb_ref, ng_ref, o_ref, st_sc, *, layer, rows):
    c_len = _HGRN_CHUNK

    @pl.when(pl.program_id(0) == 0)
    def _():
        st_sc[...] = jnp.zeros_like(st_sc)

    lb_all = lb_ref[...]
    e = jnp.exp(lb_all - jnp.max(lb_all, axis=0, keepdims=True))
    sm = e / jnp.sum(e, axis=0, keepdims=True)
    lbs = jnp.sum(sm[0:layer + 1, :], axis=0, keepdims=True) - sm[0:1, :]

    r1 = lax.broadcasted_iota(I32, (c_len, 1), 0)
    rr = lax.broadcasted_iota(I32, (c_len, c_len), 0)
    cc = lax.broadcasted_iota(I32, (c_len, c_len), 1)

    def block_ref(a, blk, r):
        x = a.reshape(c_len // blk, blk, a.shape[-1])[:, r:r + 1, :]
        return jnp.broadcast_to(x, (c_len // blk, blk, a.shape[-1])).reshape(c_len, a.shape[-1])

    w = HGRN_WIDTH
    for c in range(rows // c_len):
        rs = slice(c * c_len, (c + 1) * c_len)
        hq = h4_ref[rs, 0:w]
        z = h4_ref[rs, w:2 * w]
        val = h4_ref[rs, 2 * w:3 * w]
        hg = h4_ref[rs, 3 * w:4 * w]
        key = (1.0 - lbs) * jax.nn.sigmoid(-z)
        g = jnp.log1p(-jnp.minimum(key, KEY_MAX))
        a_cum = g
        sh = 1
        while sh < c_len:
            a_cum = a_cum + jnp.where(r1 >= sh, pltpu.roll(a_cum, sh, 0), 0.0)
            sh *= 2
        qf = _silu(hq)
        outs = []
        for h in range(HGRN_HEADS):
            hs = slice(h * HGRN_DIM, (h + 1) * HGRN_DIM)
            ah, qh, kh, vh = a_cum[:, hs], qf[:, hs], key[:, hs], val[:, hs]
            vh16 = vh.astype(BF16)
            st = st_sc[h]
            o = _dot_nt((qh * jnp.exp(ah)).astype(BF16), st.astype(BF16))
            ref = block_ref(ah, _HGRN_BASE, _HGRN_BASE // 2 - 1)
            sc = _dot_nt((qh * jnp.exp(ah - ref)).astype(BF16), (kh * jnp.exp(ref - ah)).astype(BF16))
            same = (rr >> _log2(_HGRN_BASE)) == (cc >> _log2(_HGRN_BASE))
            scores = jnp.where(jnp.logical_and(same, cc <= rr), sc, 0.0)
            half = _HGRN_BASE
            while half < c_len:
                blk = 2 * half
                ref = block_ref(ah, blk, half - 1)
                second = (r1 & (blk - 1)) >= half
                ql = jnp.where(second, qh * jnp.exp(jnp.minimum(ah - ref, 0.0)), 0.0)
                kl = jnp.where(second, 0.0, kh * jnp.exp(jnp.minimum(ref - ah, 0.0)))
                sc = _dot_nt(ql.astype(BF16), kl.astype(BF16))
                scores = scores + jnp.where((rr >> _log2(blk)) == (cc >> _log2(blk)), sc, 0.0)
                half = blk
            o = o + _dot(scores.astype(BF16), vh16)
            a_last = ah[c_len - 1:c_len, :]
            kd = (kh * jnp.exp(a_last - ah)).astype(BF16)
            st_sc[h] = st * jnp.exp(a_last) + _dot_tn(vh16, kd)
            o = o * lax.rsqrt(jnp.mean(o * o, axis=-1, keepdims=True) + RMS_EPS)
            outs.append(o)
        o_ref[rs, :] = jnp.concatenate(outs, axis=1) * ng_ref[...] * _silu(hg)


def _hgrn_call(h4, lb, ng, *, layer, rows):
    s = h4.shape[0]
    kern = functools.partial(_hgrn_kernel, layer=layer, rows=rows)
    return pl.pallas_call(
        kern,
        grid=(s // rows,),
        in_specs=[pl.BlockSpec((rows, 4 * HGRN_WIDTH), lambda i: (i, 0)),
                  pl.BlockSpec(lb.shape, lambda i: (0, 0)),
                  pl.BlockSpec((1, HGRN_WIDTH), lambda i: (0, 0))],
        out_specs=pl.BlockSpec((rows, HGRN_WIDTH), lambda i: (i, 0)),
        out_shape=jax.ShapeDtypeStruct((s, HGRN_WIDTH), F32),
        scratch_shapes=[pltpu.VMEM((HGRN_HEADS, HGRN_DIM, HGRN_DIM), F32)],
        compiler_params=_cparams("arbitrary"),
        name="hgrn2",
    )(h4, lb, ng)


_POOL_HALO = 16


def _merge_kernel(x_ref, mod_ref, a_ref, ah_ref, yb_ref, yc_ref, g_ref, pw_ref, ps_ref, wb_ref, wo_ref,
                  lng_ref, lnb_ref, o_ref, *, tm, alpha):
    i = pl.program_id(0)
    d = x_ref.shape[1]
    _, _, gate = _mod_rows(mod_ref, 1)
    a = a_ref[...]
    halo = jnp.where(i == 0, 0.0, ah_ref[...])
    ext = jnp.concatenate([halo, a], axis=0)
    lane = lax.broadcasted_iota(I32, (tm, POOL_WIDTH), 1)
    grp = lane >> _log2(POOL_GROUP)
    pos1 = (i * tm + lax.broadcasted_iota(I32, (tm, POOL_WIDTH), 0) + 1).astype(F32)
    win = jnp.zeros((tm, POOL_WIDTH), F32)
    cnt = jnp.zeros((tm, POOL_WIDTH), F32)
    ssum, have = ext, 1
    for gi, wd in enumerate(POOL_WINDOWS):
        while have < wd:
            ssum = ssum + pltpu.roll(ssum, have, 0)
            have *= 2
        assert have == wd
        win = jnp.where(grp == gi, ssum[_POOL_HALO:, :], win)
        cnt = jnp.where(grp == gi, jnp.minimum(pos1, float(wd)), cnt)
    dlt = (win / cnt - a).astype(BF16)
    ya = _dot(dlt, pw_ref[...]) * ps_ref[...]
    za = _dot(ya.astype(BF16), wb_ref[0:POOL_WIDTH, :])
    zb = _dot(yb_ref[...].astype(BF16), wb_ref[POOL_WIDTH:POOL_WIDTH + ATTN_WIDTH, :])
    zc = _dot(yc_ref[...].astype(BF16), wb_ref[POOL_WIDTH + ATTN_WIDTH:, :])
    merged = (jax.nn.sigmoid(g_ref[:, 0:d]) * za + jax.nn.sigmoid(g_ref[:, d:2 * d]) * zb
              + jax.nn.sigmoid(g_ref[:, 2 * d:3 * d]) * zc)
    y = _dot(merged.astype(BF16), wo_ref[...])
    z = alpha * x_ref[...] + (1.0 + gate) * y
    o_ref[...] = _layer_norm(z, lng_ref[1:2, :], lnb_ref[1:2, :])


def _merge_call(x, mod, a, yb, yc, gates, pw_bd, ps, wb, wo, ln_g, ln_b, *, tm, alpha):
    s, d = x.shape
    kern = functools.partial(_merge_kernel, tm=tm, alpha=alpha)
    full = lambda arr: pl.BlockSpec(arr.shape, lambda i: (0, 0))
    hb = tm // _POOL_HALO
    return pl.pallas_call(
        kern,
        grid=(s // tm,),
        in_specs=[pl.BlockSpec((tm, d), lambda i: (i, 0)),
                  full(mod),
                  pl.BlockSpec((tm, POOL_WIDTH), lambda i: (i, 0)),
                  pl.BlockSpec((_POOL_HALO, POOL_WIDTH), lambda i: (jnp.maximum(i * hb - 1, 0), 0)),
                  pl.BlockSpec((tm, ATTN_WIDTH), lambda i: (i, 0)),
                  pl.BlockSpec((tm, HGRN_WIDTH), lambda i: (i, 0)),
                  pl.BlockSpec((tm, 3 * d), lambda i: (i, 0)),
                  full(pw_bd), full(ps), full(wb), full(wo), full(ln_g), full(ln_b)],
        out_specs=pl.BlockSpec((tm, d), lambda i: (i, 0)),
        out_shape=jax.ShapeDtypeStruct((s, d), F32),
        compiler_params=_cparams("arbitrary"),
        name="pool_merge",
    )(x, mod, a, a, yb, yc, gates, pw_bd, ps, wb, wo, ln_g, ln_b)


def _pool_blockdiag(pool_w):
    n = pool_w.shape[0]
    rows = [jnp.concatenate([pool_w[g] if g == j else jnp.zeros_like(pool_w[g]) for j in range(n)], axis=1)
            for g in range(n)]
    return jnp.concatenate(rows, axis=0)


def kernel(x, c, w_ada, b_ada, ln_g, ln_b, ffn_w_in, ffn_w_out, mix_w_in, pool_w, pool_scale, rel_bias,
           hgrn_lb, hgrn_norm_g, w_branch, w_out):
    batch, s, d = x.shape
    assert batch == 1, "kernels are written for a single sequence"
    depth = w_ada.shape[0]
    alpha = (2 * depth) ** 0.25
    tm_ffn = min(512, s)
    tf = 256
    tm_proj = min(256, s)
    tq = min(256, s)
    tm_merge = min(256, s)
    hgrn_rows = min(256, s)

    mods = _ada_call(c, w_ada, b_ada)
    xs = x[0]
    for l in range(depth):
        mod = mods[l].reshape(3 * N_SUB, d)
        xs = _ffn_call(xs, mod, ffn_w_in[l, 0].astype(BF16), ffn_w_out[l, 0].astype(BF16), ln_g[l], ln_b[l],
                       sub=0, alpha=alpha, tm=tm_ffn, tf=tf)
        gates, a, h4, qi, q, k, v, ki2, wi = _proj_call(xs, mod, _proj_weight(mix_w_in[l]), tm=tm_proj)
        yb = _dsa_call(rel_bias, qi, wi, q, ki2, k, v, tq=tq)
        yc = _hgrn_call(h4, hgrn_lb, hgrn_norm_g[l].reshape(1, -1), layer=l, rows=hgrn_rows)
        xs = _merge_call(xs, mod, a, yb, yc, gates, _pool_blockdiag(pool_w[l]).astype(BF16),
                         pool_scale[l].reshape(1, -1), w_branch[l].astype(BF16), w_out[l].astype(BF16),
                         ln_g[l], ln_b[l], tm=tm_merge, alpha=alpha)
        xs = _ffn_call(xs, mod, ffn_w_in[l, 1].astype(BF16), ffn_w_out[l, 1].astype(BF16), ln_g[l], ln_b[l],
                       sub=2, alpha=alpha, tm=tm_ffn, tf=tf)
    return xs[None]
```

```python
import functools
import math

import jax
import jax.numpy as jnp
from jax import lax
from jax.experimental import pallas as pl
from jax.experimental.pallas import tpu as pltpu

F32 = jnp.float32
BF16 = jnp.bfloat16
I32 = jnp.int32
HIGHEST = lax.Precision.HIGHEST

POOL_WINDOWS = (2, 4, 8, 16)
POOL_GROUP = 64
POOL_WIDTH = POOL_GROUP * len(POOL_WINDOWS)
ATTN_HEADS = 4
ATTN_HEAD_DIM = 64
ATTN_WIDTH = ATTN_HEADS * ATTN_HEAD_DIM
IDX_HEADS = 8
IDX_DIM = 64
TOPK_MAX = 256
HGRN_HEADS = 4
HGRN_DIM = 128
HGRN_WIDTH = HGRN_HEADS * HGRN_DIM
KEY_MAX = 1.0 - 1e-6
REL_BUCKETS = 32
REL_MAX_DIST = 128
N_SUB = 3
LN_EPS = 1e-5
RMS_EPS = 1e-6

LANES = 128
SUBLANES = 8
VMEM_LIMIT_BYTES = 58 * 1024 * 1024

NEG_INF = float("-inf")
POS_INF = float("inf")


def _cparams(*sem):
    return pltpu.CompilerParams(dimension_semantics=sem, vmem_limit_bytes=VMEM_LIMIT_BYTES)


def _dot(a, b):
    return jnp.dot(a, b, preferred_element_type=F32)


def _dot_nt(a, b):
    return lax.dot_general(a, b, (((1,), (1,)), ((), ())), preferred_element_type=F32)


def _dot_tn(a, b):
    return lax.dot_general(a, b, (((0,), (0,)), ((), ())), preferred_element_type=F32)


def _log2(n):
    assert n & (n - 1) == 0
    return n.bit_length() - 1


def _silu(x):
    return x * jax.nn.sigmoid(x)


def _mod_rows(mod_ref, j):
    return mod_ref[3 * j:3 * j + 1, :], mod_ref[3 * j + 1:3 * j + 2, :], mod_ref[3 * j + 2:3 * j + 3, :]


def _layer_norm(z, g, b):
    mu = jnp.mean(z, axis=-1, keepdims=True)
    zc = z - mu
    var = jnp.mean(zc * zc, axis=-1, keepdims=True)
    return zc * lax.rsqrt(var + LN_EPS) * g + b


def _ada_kernel(c_ref, w_ref, b_ref, o_ref):
    c = c_ref[...]
    r = jnp.dot(_silu(c), w_ref[0], precision=HIGHEST, preferred_element_type=F32)
    o_ref[0] = r[0:1, :] + b_ref[0]


def _ada_call(c, w_ada, b_ada):
    depth, d, n = w_ada.shape
    tn = n // 8
    c8 = jnp.broadcast_to(c, (SUBLANES, d))
    return pl.pallas_call(
        _ada_kernel,
        grid=(depth, n // tn),
        in_specs=[pl.BlockSpec((SUBLANES, d), lambda l, j: (0, 0)),
                  pl.BlockSpec((1, d, tn), lambda l, j: (l, 0, j)),
                  pl.BlockSpec((1, 1, tn), lambda l, j: (l, 0, j))],
        out_specs=pl.BlockSpec((1, 1, tn), lambda l, j: (l, 0, j)),
        out_shape=jax.ShapeDtypeStruct((depth, 1, n), F32),
        compiler_params=_cparams("arbitrary", "arbitrary"),
        name="ada_mod",
    )(c8, w_ada, b_ada.reshape(depth, 1, n))


def _ffn_kernel(x_ref, mod_ref, wi_ref, wo_ref, lng_ref, lnb_ref, o_ref, *, sub, tf, alpha):
    shift, scale, gate = _mod_rows(mod_ref, sub)
    ff = wo_ref.shape[0]
    x = x_ref[...]
    h = (x * (1.0 + scale) + shift).astype(BF16)
    y = jnp.zeros(x.shape, F32)
    for c0 in range(0, ff, tf):
        g = _dot(h, wi_ref[:, c0:c0 + tf])
        u = _dot(h, wi_ref[:, ff + c0:ff + c0 + tf])
        y = y + _dot((_silu(g) * u).astype(BF16), wo_ref[c0:c0 + tf, :])
    z = alpha * x + (0.5 * (1.0 + gate)) * y
    o_ref[...] = _layer_norm(z, lng_ref[sub:sub + 1, :], lnb_ref[sub:sub + 1, :])


def _ffn_call(x, mod, w_in, w_out, ln_g, ln_b, *, sub, alpha, tm, tf):
    s, d = x.shape
    ff = w_out.shape[0]
    assert ff % tf == 0
    kern = functools.partial(_ffn_kernel, sub=sub, tf=tf, alpha=alpha)
    resident = lambda arr: pl.BlockSpec(arr.shape, lambda i: (0, 0), pipeline_mode=pl.Buffered(1))
    return pl.pallas_call(
        kern,
        grid=(s // tm,),
        in_specs=[pl.BlockSpec((tm, d), lambda i: (i, 0)),
                  pl.BlockSpec((3 * N_SUB, d), lambda i: (0, 0)),
                  resident(w_in), resident(w_out),
                  pl.BlockSpec((N_SUB, d), lambda i: (0, 0)),
                  pl.BlockSpec((N_SUB, d), lambda i: (0, 0))],
        out_specs=pl.BlockSpec((tm, d), lambda i: (i, 0)),
        out_shape=jax.ShapeDtypeStruct((s, d), F32),
        compiler_params=_cparams("arbitrary"),
        name="ffn",
    )(x, mod, w_in, w_out, ln_g, ln_b)


_PROJ_OUTS = (("gates", 3 * 1024, F32), ("a", POOL_WIDTH, F32), ("h4", 4 * HGRN_WIDTH, F32),
              ("qi", IDX_HEADS * IDX_DIM, BF16), ("q", ATTN_WIDTH, BF16), ("k", ATTN_WIDTH, BF16),
              ("v", ATTN_WIDTH, BF16), ("ki2", 2 * IDX_DIM, BF16), ("wi", LANES, F32))
_PROJ_CHUNK = 1024


def _proj_kernel(x_ref, mod_ref, w_ref, *o_refs):
    shift, scale, _ = _mod_rows(mod_ref, 1)
    h = (x_ref[...] * (1.0 + scale) + shift).astype(BF16)
    c0 = 0
    for (_, width, dt), o_ref in zip(_PROJ_OUTS, o_refs):
        for cc in range(0, width, _PROJ_CHUNK):
            cw = min(_PROJ_CHUNK, width - cc)
            o_ref[:, cc:cc + cw] = _dot(h, w_ref[:, c0 + cc:c0 + cc + cw]).astype(dt)
        c0 += width


def _proj_weight(w):
    d = w.shape[0]
    widths = (POOL_WIDTH, ATTN_WIDTH, ATTN_WIDTH, ATTN_WIDTH, IDX_HEADS * IDX_DIM, IDX_DIM, IDX_HEADS,
              HGRN_WIDTH, HGRN_WIDTH, HGRN_WIDTH, HGRN_WIDTH, 3 * d)
    parts, c0 = [], 0
    for wd in widths:
        parts.append(w[:, c0:c0 + wd])
        c0 += wd
    a, q, k, v, qi, ki, wi, hq, hf, hi, hg, gates = parts
    wi_pad = jnp.pad(wi, ((0, 0), (0, LANES - IDX_HEADS)))
    q = q * (ATTN_HEAD_DIM ** -0.5 * _LOG2E)
    return jnp.concatenate([gates, a, hq, hf, hi, hg, qi, q, k, v, ki, ki, wi_pad], axis=1).astype(BF16)


def _proj_call(x, mod, w, *, tm):
    s, d = x.shape
    n = w.shape[1]
    assert n == sum(wd for _, wd, _ in _PROJ_OUTS)
    return pl.pallas_call(
        _proj_kernel,
        grid=(s // tm,),
        in_specs=[pl.BlockSpec((tm, d), lambda i: (i, 0)),
                  pl.BlockSpec((3 * N_SUB, d), lambda i: (0, 0)),
                  pl.BlockSpec((d, n), lambda i: (0, 0))],
        out_specs=[pl.BlockSpec((tm, wd), lambda i: (i, 0)) for _, wd, _ in _PROJ_OUTS],
        out_shape=[jax.ShapeDtypeStruct((s, wd), dt) for _, wd, dt in _PROJ_OUTS],
        compiler_params=_cparams("arbitrary"),
        name="mix_proj",
    )(x, mod, w)


_BISECT_MAX_ITERS = 320
_F32_TINY = float(jnp.finfo(jnp.float32).tiny)
_COARSE_PASSES = 8
_COARSE_MIN_ABS, _COARSE_MAX_ABS = 1e-30, 1e30
_KIND_FAR, _KIND_PAD, _N_KINDS = 2, 3, 4
_VT_ROWS = LANES + 16
_SMASK_FULL, _SMASK_DIAG, _SMASK_PAD = 0, 1, 2
_LOG2E = math.log2(math.e)


def _t5_bucket(dist):
    max_exact = REL_BUCKETS // 2
    d32 = jnp.maximum(dist, 1).astype(F32)
    large = max_exact + (jnp.log(d32 / max_exact) / math.log(REL_MAX_DIST / max_exact)
                         * (REL_BUCKETS - max_exact)).astype(I32)
    large = jnp.minimum(large, REL_BUCKETS - 1)
    return jnp.where(dist < max_exact, dist, large)


def _dsa_kernel(relb_ref, qi_ref, wi_ref, q_ref, ki2_ref, k_ref, vt_ref, o_ref,
                s_sc, sb_sc, bias_sc, smask_sc, lg_sc, qim_sc, qm_sc, m_sc, acc_sc, tie_sc, cnt_sc,
                *, tq, topk, idx_scale):
    tk = tq
    i = pl.program_id(0)
    krow = lax.broadcasted_iota(I32, (tk, tq), 0)
    qcol = lax.broadcasted_iota(I32, (tk, tq), 1)
    causal = krow <= qcol
    lane = lax.broadcasted_iota(I32, (tq, LANES), 1)
    lo_half = lane < ATTN_HEAD_DIM

    @pl.when(i == 0)
    def _():
        for which in range(2):
            bucket = _t5_bucket(jnp.maximum(qcol - krow + which * tq, 0))
            for h in range(ATTN_HEADS):
                b = jnp.zeros((tk, tq), F32)
                for bk in range(REL_BUCKETS):
                    b = jnp.where(bucket == bk, relb_ref[bk, h] * _LOG2E, b)
                bias_sc[h, which] = (jnp.where(causal, b, NEG_INF) if which == 0 else b).astype(BF16)
        for h in range(ATTN_HEADS):
            bias_sc[h, _KIND_FAR] = jnp.full((tk, tq), relb_ref[REL_BUCKETS - 1, h] * _LOG2E, F32).astype(BF16)
            bias_sc[h, _KIND_PAD] = jnp.full((tk, tq), NEG_INF, BF16)
        smask_sc[_SMASK_FULL] = jnp.zeros((tk, tq), F32)
        smask_sc[_SMASK_DIAG] = jnp.where(causal, 0.0, NEG_INF)
        smask_sc[_SMASK_PAD] = jnp.full((tk, tq), NEG_INF, F32)

    qi = qi_ref[...]
    for h in range(IDX_HEADS):
        pair = qi[:, LANES * (h // 2):LANES * (h // 2 + 1)]
        qim_sc[h] = jnp.where(lo_half if h % 2 == 0 else ~lo_half, pair, jnp.zeros_like(pair))
    qs = q_ref[...]
    for h in range(ATTN_HEADS):
        pair = qs[:, LANES * (h // 2):LANES * (h // 2 + 1)]
        qm_sc[h] = jnp.where(lo_half if h % 2 == 0 else ~lo_half, pair, jnp.zeros_like(pair))

    wt = jnp.transpose(wi_ref[...])

    def key_rows(jb):
        return pl.ds(pl.multiple_of(jb * tk, tk), tk)

    def fold(x):
        return x.reshape(tk // SUBLANES, SUBLANES, tq)

    def index_scores(jb):
        kb = ki2_ref[key_rows(jb), :]
        acc = jnp.zeros((tk, tq), F32)
        for h in range(IDX_HEADS):
            acc = acc + wt[h:h + 1, :] * jnp.maximum(_dot_nt(kb, qim_sc[h]), 0.0)
        return acc * idx_scale

    npair = (i + 2) // 2
    last_kv = k_ref.shape[0] // tk - 1

    def p1_pair(j, carry):
        rmin, rmax = carry
        for jb in (2 * j, 2 * j + 1):
            raw = index_scores(jnp.minimum(jb, last_kv))
            kind = jnp.where(jb > i, _SMASK_PAD, jnp.where(jb == i, _SMASK_DIAG, _SMASK_FULL))
            sc = raw + smask_sc[kind]
            s_sc[jb] = sc
            sb_sc[jb] = sc.astype(BF16)
            rmin = jnp.minimum(rmin, jnp.min(fold(raw), axis=0))
            rmax = jnp.maximum(rmax, jnp.max(fold(raw), axis=0))
        return rmin, rmax

    rmin, rmax = lax.fori_loop(0, npair, p1_pair, (jnp.full((SUBLANES, tq), POS_INF, F32),
                                                   jnp.full((SUBLANES, tq), NEG_INF, F32)))
    rowmin = jnp.min(rmin, axis=0, keepdims=True)
    rowmax = jnp.max(rmax, axis=0, keepdims=True)

    kf = float(topk)
    nfin = (i * tq + lax.broadcasted_iota(I32, (1, tq), 1) + 1).astype(F32)
    small = nfin <= kf

    def count_ge(x):
        def body(j, cnt):
            for jb in (2 * j, 2 * j + 1):
                cnt = cnt + jnp.sum(fold(jnp.where(s_sc[jb] >= x, 1.0, 0.0)), axis=0)
            return cnt
        cnt = lax.fori_loop(0, npair, body, jnp.zeros((SUBLANES, tq), F32))
        return jnp.sum(cnt, axis=0, keepdims=True)

    def count_ge16(xb):
        one, zero = jnp.ones((), BF16), jnp.zeros((), BF16)
        rows16 = 2 * SUBLANES

        def body(j, cnt):
            parts = []
            for jb in (2 * j, 2 * j + 1):
                c = jnp.where(sb_sc[jb] >= xb, one, zero)
                parts += [c[r0:r0 + rows16, :] for r0 in range(0, tk, rows16)]
            while len(parts) > 1:
                parts = [a + b for a, b in zip(parts[0::2], parts[1::2])]
            return cnt + parts[0].astype(F32)
        cnt = lax.fori_loop(0, npair, body, jnp.zeros((rows16, tq), F32))
        return jnp.sum(cnt, axis=0, keepdims=True)

    def coarse_probe(mid):
        xb = mid.astype(BF16)
        xbits = lax.bitcast_convert_type(xb.astype(F32), I32)
        neg = xbits < 0
        mu = jnp.where(neg, xbits + 0x8000, xbits - 0x8000)
        odd = (xbits & 0x10000) != 0
        tbits = jnp.where(odd, jnp.where(neg, mu - 1, mu + 1), mu)
        t = lax.bitcast_convert_type(tbits, F32)
        ok = jnp.logical_and(jnp.abs(xb.astype(F32)) > _COARSE_MIN_ABS, jnp.abs(xb.astype(F32)) < _COARSE_MAX_ABS)
        return xb, t, ok

    def bis_cond(st):
        it, _, _, _, _, done = st
        return jnp.logical_and(it < _BISECT_MAX_ITERS, jnp.min(done) < 0.5)

    def bis_body(st):
        it, lo, hi, clo, chi, done = st
        mid = 0.5 * lo + 0.5 * hi
        coarse = it < _COARSE_PASSES
        xb, t, ok = coarse_probe(mid)
        ok = jnp.logical_and(ok, jnp.logical_and(t > lo, t < hi))
        fine_mid = jnp.where(jnp.logical_and(lo == 0.0, hi > _F32_TINY), _F32_TINY, mid)
        fine_mid = jnp.where(jnp.logical_and(lo < 0.0, hi > 0.0), 0.0, fine_mid)
        mid = jnp.where(coarse, t, fine_mid)
        stuck = jnp.logical_and(jnp.logical_not(coarse),
                                jnp.logical_or(jnp.logical_or(mid <= lo, mid >= hi),
                                               jnp.logical_and(lo == 0.0, hi <= _F32_TINY)))
        c = lax.cond(coarse, lambda: count_ge16(xb), lambda: count_ge(mid))
        active = jnp.logical_and(done < 0.5, jnp.logical_not(stuck))
        active = jnp.logical_and(active, jnp.logical_or(jnp.logical_not(coarse), ok))
        up = jnp.logical_and(active, c >= kf)
        dn = jnp.logical_and(active, c < kf)
        lo = jnp.where(up, mid, lo)
        clo = jnp.where(up, c, clo)
        hi = jnp.where(dn, mid, hi)
        chi = jnp.where(dn, c, chi)
        fin = jnp.logical_or(stuck, jnp.logical_and(active, c == kf))
        done = jnp.where(fin, 1.0, done)
        return it + 1, lo, hi, clo, chi, done

    mbits = lax.bitcast_convert_type(rowmax, I32)
    above_max = jnp.where(rowmax == 0.0, _F32_TINY,
                          lax.bitcast_convert_type(jnp.where(rowmax > 0.0, mbits + 1, mbits - 1), F32))
    st0 = (jnp.int32(0), jnp.where(small, NEG_INF, rowmin), above_max, nfin,
           jnp.zeros((1, tq), F32), jnp.where(small, 1.0, 0.0))
    _, lo, hi, clo, chi, _ = lax.while_loop(bis_cond, bis_body, st0)
    budget = kf - chi
    need_ties = jnp.max((clo - chi) - budget) > 0.5

    pairs = [slice(LANES * (h // 2), LANES * (h // 2 + 1)) for h in range(ATTN_HEADS)]

    def issue_qk(jb, slot):
        kb = k_ref[key_rows(jnp.minimum(jb, last_kv)), :]
        for h in range(ATTN_HEADS):
            lg_sc[slot, h] = _dot_nt(kb[:, pairs[h]], qm_sc[h]).astype(BF16)

    def attend(jb, slot, thr, cum=None):
        s = s_sc[jb]
        if cum is None:
            sel = s >= thr
        else:
            cand = jnp.logical_and(s >= lo, s < hi)
            before = (qcol < krow).astype(BF16)
            rank = cum + _dot(before, jnp.where(cand, 1.0, 0.0).astype(BF16))
            sel = jnp.logical_or(s >= hi, jnp.logical_and(cand, rank < budget))
        kind = jnp.where(jb > i, _KIND_PAD, jnp.minimum(i - jb, _KIND_FAR))
        selm = jnp.where(sel, 0.0, NEG_INF).astype(BF16)
        probs, alphas = [], []
        for h in range(ATTN_HEADS):
            lg = lg_sc[slot, h] + bias_sc[h, kind] + selm
            m_old = m_sc[h]
            m_new = jnp.maximum(m_old, jnp.max(lg, axis=0, keepdims=True).astype(F32))
            m_safe = jnp.where(m_new == NEG_INF, 0.0, m_new)
            alphas.append(jnp.exp2(m_old - m_safe))
            probs.append(jnp.exp2(lg - m_safe.astype(BF16)))
            m_sc[h] = m_new
        vb = vt_ref[jnp.minimum(jb, last_kv)]
        for h in range(ATTN_HEADS):
            vrows = slice(_VT_ROWS * (h // 2), _VT_ROWS * (h // 2 + 1))
            acc_sc[h] = alphas[h] * acc_sc[h] + _dot(vb[vrows, :], probs[h])

    def run(with_ties):
        m_sc[...] = jnp.full(m_sc.shape, NEG_INF, F32)
        acc_sc[...] = jnp.zeros_like(acc_sc)
        tie_sc[...] = jnp.zeros_like(tie_sc)
        if with_ties:
            def cand_body(j, carry):
                for jb in (2 * j, 2 * j + 1):
                    s = s_sc[jb]
                    cand = jnp.logical_and(s >= lo, s < hi)
                    cnt_sc[jb] = jnp.sum(jnp.where(cand, 1.0, 0.0), axis=0, keepdims=True)
                return carry
            lax.fori_loop(0, npair, cand_body, 0)
        issue_qk(0, 0)

        def pair(j, thr0, thr1, cum0=None, cum1=None):
            issue_qk(2 * j + 1, 1)
            attend(2 * j, 0, thr0, cum0)
            issue_qk(2 * j + 2, 0)
            attend(2 * j + 1, 1, thr1, cum1)

        def pair_body(j, carry):
            if not with_ties:
                pair(j, lo, lo)
                return carry
            cum0 = tie_sc[...]
            cum1 = cum0 + cnt_sc[2 * j]
            cum2 = cum1 + cnt_sc[2 * j + 1]
            tie_sc[...] = cum2
            all0, all1 = cum1 <= budget, cum2 <= budget
            runs_out = jnp.logical_or(jnp.logical_and(jnp.logical_not(all0), cum0 < budget),
                                      jnp.logical_and(jnp.logical_not(all1), cum1 < budget))
            ranked = jnp.max(jnp.where(runs_out, 1.0, 0.0)) > 0.5

            @pl.when(ranked)
            def _():
                pair(j, None, None, cum0, cum1)

            @pl.when(jnp.logical_not(ranked))
            def _():
                pair(j, jnp.where(all0, lo, hi), jnp.where(all1, lo, hi))
            return carry
        lax.fori_loop(0, npair, pair_body, 0)

    @pl.when(need_ties)
    def _():
        run(True)

    @pl.when(jnp.logical_not(need_ties))
    def _():
        run(False)

    first_head = lax.broadcasted_iota(I32, (LANES, tq), 0) < ATTN_HEAD_DIM
    normed = [acc_sc[h, 0:LANES, :] / acc_sc[h, LANES:LANES + 1, :] for h in range(ATTN_HEADS)]
    for p in range(ATTN_HEADS // 2):
        pair_t = jnp.where(first_head, normed[2 * p], normed[2 * p + 1])
        o_ref[:, LANES * p:LANES * (p + 1)] = jnp.transpose(pair_t)


def _dsa_call(rel_bias, qi, wi, q, ki2, k, v, *, tq):
    s = q.shape[0]
    assert tq > REL_MAX_DIST and s % tq == 0
    nb = s // tq
    topk = min(TOPK_MAX, s // 4)
    kern = functools.partial(_dsa_kernel, tq=tq, topk=topk,
                             idx_scale=(IDX_DIM ** -0.5) * (IDX_HEADS ** -0.5))
    whole = lambda width: pl.BlockSpec((s, width), lambda i: (0, 0), pipeline_mode=pl.Buffered(1))
    n_pairs = ATTN_WIDTH // LANES
    vt = jnp.transpose(v.reshape(nb, tq, n_pairs, LANES), (0, 2, 3, 1))
    vt = jnp.concatenate([vt, jnp.ones((nb, n_pairs, _VT_ROWS - LANES, tq), BF16)], axis=2)
    vt = vt.reshape(nb, n_pairs * _VT_ROWS, tq)
    return pl.pallas_call(
        kern,
        grid=(nb,),
        in_specs=[pl.BlockSpec(memory_space=pltpu.SMEM),
                  pl.BlockSpec((tq, IDX_HEADS * IDX_DIM), lambda i: (i, 0)),
                  pl.BlockSpec((tq, LANES), lambda i: (i, 0)),
                  pl.BlockSpec((tq, ATTN_WIDTH), lambda i: (i, 0)),
                  whole(2 * IDX_DIM), whole(ATTN_WIDTH),
                  pl.BlockSpec(vt.shape, lambda i: (0, 0, 0), pipeline_mode=pl.Buffered(1))],
        out_specs=pl.BlockSpec((tq, ATTN_WIDTH), lambda i: (i, 0)),
        out_shape=jax.ShapeDtypeStruct((s, ATTN_WIDTH), F32),
        scratch_shapes=[pltpu.VMEM((nb + nb % 2, tq, tq), F32),
                        pltpu.VMEM((nb + nb % 2, tq, tq), BF16),
                        pltpu.VMEM((ATTN_HEADS, _N_KINDS, tq, tq), BF16),
                        pltpu.VMEM((3, tq, tq), F32),
                        pltpu.VMEM((2, ATTN_HEADS, tq, tq), BF16),
                        pltpu.VMEM((IDX_HEADS, tq, LANES), BF16),
                        pltpu.VMEM((ATTN_HEADS, tq, LANES), BF16),
                        pltpu.VMEM((ATTN_HEADS, 1, tq), F32),
                        pltpu.VMEM((ATTN_HEADS, _VT_ROWS, tq), F32),
                        pltpu.VMEM((1, tq), F32),
                        pltpu.VMEM((nb + nb % 2, 1, tq), F32)],
        compiler_params=_cparams("arbitrary"),
        name="dsa_attention",
    )(rel_bias, qi, wi, q, ki2, k, vt)


_HGRN_CHUNK = 128
_HGRN_BASE = SUBLANES


def _hgrn_kernel(h4_ref, lb_ref, ng_ref, o_ref, st_sc, *, layer, rows):
    c_len = _HGRN_CHUNK

    @pl.when(pl.program_id(0) == 0)
    def _():
        st_sc[...] = jnp.zeros_like(st_sc)

    lb_all = lb_ref[...]
    e = jnp.exp(lb_all - jnp.max(lb_all, axis=0, keepdims=True))
    sm = e / jnp.sum(e, axis=0, keepdims=True)
    lbs = jnp.sum(sm[0:layer + 1, :], axis=0, keepdims=True) - sm[0:1, :]

    r1 = lax.broadcasted_iota(I32, (c_len, 1), 0)
    rr = lax.broadcasted_iota(I32, (c_len, c_len), 0)
    cc = lax.broadcasted_iota(I32, (c_len, c_len), 1)

    def block_ref(a, blk, r):
        x = a.reshape(c_len // blk, blk, a.shape[-1])[:, r:r + 1, :]
        return jnp.broadcast_to(x, (c_len // blk, blk, a.shape[-1])).reshape(c_len, a.shape[-1])

    w = HGRN_WIDTH
    for c in range(rows // c_len):
        rs = slice(c * c_len, (c + 1) * c_len)
        hq = h4_ref[rs, 0:w]
        z = h4_ref[rs, w:2 * w]
        val = h4_ref[rs, 2 * w:3 * w]
        hg = h4_ref[rs, 3 * w:4 * w]
        key = (1.0 - lbs) * jax.nn.sigmoid(-z)
        g = jnp.log1p(-jnp.minimum(key, KEY_MAX))
        a_cum = g
        sh = 1
        while sh < c_len:
            a_cum = a_cum + jnp.where(r1 >= sh, pltpu.roll(a_cum, sh, 0), 0.0)
            sh *= 2
        qf = _silu(hq)
        outs = []
        for h in range(HGRN_HEADS):
            hs = slice(h * HGRN_DIM, (h + 1) * HGRN_DIM)
            ah, qh, kh, vh = a_cum[:, hs], qf[:, hs], key[:, hs], val[:, hs]
            vh16 = vh.astype(BF16)
            st = st_sc[h]
            o = _dot_nt((qh * jnp.exp(ah)).astype(BF16), st.astype(BF16))
            ref = block_ref(ah, _HGRN_BASE, _HGRN_BASE // 2 - 1)
            sc = _dot_nt((qh * jnp.exp(ah - ref)).astype(BF16), (kh * jnp.exp(ref - ah)).astype(BF16))
            same = (rr >> _log2(_HGRN_BASE)) == (cc >> _log2(_HGRN_BASE))
            scores = jnp.where(jnp.logical_and(same, cc <= rr), sc, 0.0)
            half = _HGRN_BASE
            while half < c_len:
                blk = 2 * half
                ref = block_ref(ah, blk, half - 1)
                second = (r1 & (blk - 1)) >= half
                ql = jnp.where(second, qh * jnp.exp(jnp.minimum(ah - ref, 0.0)), 0.0)
                kl = jnp.where(second, 0.0, kh * jnp.exp(jnp.minimum(ref - ah, 0.0)))
                sc = _dot_nt(ql.astype(BF16), kl.astype(BF16))
                scores = scores + jnp.where((rr >> _log2(blk)) == (cc >> _log2(blk)), sc, 0.0)
                half = blk
            o = o + _dot(scores.astype(BF16), vh16)
            a_last = ah[c_len - 1:c_len, :]
            kd = (kh * jnp.exp(a_last - ah)).astype(BF16)
            st_sc[h] = st * jnp.exp(a_last) + _dot_tn(vh16, kd)
            o = o * lax.rsqrt(jnp.mean(o * o, axis=-1, keepdims=True) + RMS_EPS)
            outs.append(o)
        o_ref[rs, :] = jnp.concatenate(outs, axis=1) * ng_ref[...] * _silu(hg)


def _hgrn_call(h4, lb, ng, *, layer, rows):
    s = h4.shape[0]
    kern = functools.partial(_hgrn_kernel, layer=layer, rows=rows)
    return pl.pallas_call(
        kern,
        grid=(s // rows,),
        in_specs=[pl.BlockSpec((rows, 4 * HGRN_WIDTH), lambda i: (i, 0)),
                  pl.BlockSpec(lb.shape, lambda i: (0, 0)),
                  pl.BlockSpec((1, HGRN_WIDTH), lambda i: (0, 0))],
        out_specs=pl.BlockSpec((rows, HGRN_WIDTH), lambda i: (i, 0)),
        out_shape=jax.ShapeDtypeStruct((s, HGRN_WIDTH), F32),
        scratch_shapes=[pltpu.VMEM((HGRN_HEADS, HGRN_DIM, HGRN_DIM), F32)],
        compiler_params=_cparams("arbitrary"),
        name="hgrn2",
    )(h4, lb, ng)


_POOL_HALO = 16


def _merge_kernel(x_ref, mod_ref, a_ref, ah_ref, yb_ref, yc_ref, g_ref, pw_ref, ps_ref, wb_ref, wo_ref,
                  lng_ref, lnb_ref, o_ref, *, tm, alpha):
    i = pl.program_id(0)
    d = x_ref.shape[1]
    _, _, gate = _mod_rows(mod_ref, 1)
    a = a_ref[...]
    halo = jnp.where(i == 0, 0.0, ah_ref[...])
    ext = jnp.concatenate([halo, a], axis=0)
    lane = lax.broadcasted_iota(I32, (tm, POOL_WIDTH), 1)
    grp = lane >> _log2(POOL_GROUP)
    pos1 = (i * tm + lax.broadcasted_iota(I32, (tm, POOL_WIDTH), 0) + 1).astype(F32)
    win = jnp.zeros((tm, POOL_WIDTH), F32)
    cnt = jnp.zeros((tm, POOL_WIDTH), F32)
    ssum, have = ext, 1
    for gi, wd in enumerate(POOL_WINDOWS):
        while have < wd:
            ssum = ssum + pltpu.roll(ssum, have, 0)
            have *= 2
        assert have == wd
        win = jnp.where(grp == gi, ssum[_POOL_HALO:, :], win)
        cnt = jnp.where(grp == gi, jnp.minimum(pos1, float(wd)), cnt)
    dlt = (win / cnt - a).astype(BF16)
    ya = _dot(dlt, pw_ref[...]) * ps_ref[...]
    za = _dot(ya.astype(BF16), wb_ref[0:POOL_WIDTH, :])
    zb = _dot(yb_ref[...].astype(BF16), wb_ref[POOL_WIDTH:POOL_WIDTH + ATTN_WIDTH, :])
    zc = _dot(yc_ref[...].astype(BF16), wb_ref[POOL_WIDTH + ATTN_WIDTH:, :])
    merged = (jax.nn.sigmoid(g_ref[:, 0:d]) * za + jax.nn.sigmoid(g_ref[:, d:2 * d]) * zb
              + jax.nn.sigmoid(g_ref[:, 2 * d:3 * d]) * zc)
    y = _dot(merged.astype(BF16), wo_ref[...])
    z = alpha * x_ref[...] + (1.0 + gate) * y
    o_ref[...] = _layer_norm(z, lng_ref[1:2, :], lnb_ref[1:2, :])


def _merge_call(x, mod, a, yb, yc, gates, pw_bd, ps, wb, wo, ln_g, ln_b, *, tm, alpha):
    s, d = x.shape
    kern = functools.partial(_merge_kernel, tm=tm, alpha=alpha)
    full = lambda arr: pl.BlockSpec(arr.shape, lambda i: (0, 0))
    hb = tm // _POOL_HALO
    return pl.pallas_call(
        kern,
        grid=(s // tm,),
        in_specs=[pl.BlockSpec((tm, d), lambda i: (i, 0)),
                  full(mod),
                  pl.BlockSpec((tm, POOL_WIDTH), lambda i: (i, 0)),
                  pl.BlockSpec((_POOL_HALO, POOL_WIDTH), lambda i: (jnp.maximum(i * hb - 1, 0), 0)),
                  pl.BlockSpec((tm, ATTN_WIDTH), lambda i: (i, 0)),
                  pl.BlockSpec((tm, HGRN_WIDTH), lambda i: (i, 0)),
                  pl.BlockSpec((tm, 3 * d), lambda i: (i, 0)),
                  full(pw_bd), full(ps), full(wb), full(wo), full(ln_g), full(ln_b)],
        out_specs=pl.BlockSpec((tm, d), lambda i: (i, 0)),
        out_shape=jax.ShapeDtypeStruct((s, d), F32),
        compiler_params=_cparams("arbitrary"),
        name="pool_merge",
    )(x, mod, a, a, yb, yc, gates, pw_bd, ps, wb, wo, ln_g, ln_b)


def _pool_blockdiag(pool_w):
    n = pool_w.shape[0]
    rows = [jnp.concatenate([pool_w[g] if g == j else jnp.zeros_like(pool_w[g]) for j in range(n)], axis=1)
            for g in range(n)]
    return jnp.concatenate(rows, axis=0)


def kernel(x, c, w_ada, b_ada, ln_g, ln_b, ffn_w_in, ffn_w_out, mix_w_in, pool_w, pool_scale, rel_bias,
           hgrn_lb, hgrn_norm_g, w_branch, w_out):
    batch, s, d = x.shape
    assert batch == 1, "kernels are written for a single sequence"
    depth = w_ada.shape[0]
    alpha = (2 * depth) ** 0.25
    tm_ffn = min(512, s)
    tf = 256
    tm_proj = min(256, s)
    tq = min(256, s)
    tm_merge = min(256, s)
    hgrn_rows = min(256, s)

    mods = _ada_call(c, w_ada, b_ada)
    xs = x[0]
    for l in range(depth):
        mod = mods[l].reshape(3 * N_SUB, d)
        xs = _ffn_call(xs, mod, ffn_w_in[l, 0].astype(BF16), ffn_w_out[l, 0].astype(BF16), ln_g[l], ln_b[l],
                       sub=0, alpha=alpha, tm=tm_ffn, tf=tf)
        gates, a, h4, qi, q, k, v, ki2, wi = _proj_call(xs, mod, _proj_weight(mix_w_in[l]), tm=tm_proj)
        yb = _dsa_call(rel_bias, qi, wi, q, ki2, k, v, tq=tq)
        yc = _hgrn_call(h4, hgrn_lb, hgrn_norm_g[l].reshape(1, -1), layer=l, rows=hgrn_rows)
        xs = _merge_call(xs, mod, a, yb, yc, gates, _pool_blockdiag(pool_w[l]).astype(BF16),
                         pool_scale[l].reshape(1, -1), w_branch[l].astype(BF16), w_out[l].astype(BF16),
                         ln_g[l], ln_b[l], tm=tm_merge, alpha=alpha)
        xs = _ffn_call(xs, mod, ffn_w_in[l, 1].astype(BF16), ffn_w_out[l, 1].astype(BF16), ln_g[l], ln_b[l],
                       sub=2, alpha=alpha, tm=tm_ffn, tf=tf)
    return xs[None]
```

```python
import functools
import math

import jax
import jax.numpy as jnp
from jax import lax
from jax.experimental import pallas as pl
from jax.experimental.pallas import tpu as pltpu

F32 = jnp.float32
BF16 = jnp.bfloat16
I32 = jnp.int32
HIGHEST = lax.Precision.HIGHEST

POOL_WINDOWS = (2, 4, 8, 16)
POOL_GROUP = 64
POOL_WIDTH = POOL_GROUP * len(POOL_WINDOWS)
ATTN_HEADS = 4
ATTN_HEAD_DIM = 64
ATTN_WIDTH = ATTN_HEADS * ATTN_HEAD_DIM
IDX_HEADS = 8
IDX_DIM = 64
TOPK_MAX = 256
HGRN_HEADS = 4
HGRN_DIM = 128
HGRN_WIDTH = HGRN_HEADS * HGRN_DIM
KEY_MAX = 1.0 - 1e-6
REL_BUCKETS = 32
REL_MAX_DIST = 128
N_SUB = 3
LN_EPS = 1e-5
RMS_EPS = 1e-6

LANES = 128
SUBLANES = 8
VMEM_LIMIT_BYTES = 58 * 1024 * 1024

NEG_INF = float("-inf")
POS_INF = float("inf")


def _cparams(*sem):
    return pltpu.CompilerParams(dimension_semantics=sem, vmem_limit_bytes=VMEM_LIMIT_BYTES)


def _dot(a, b):
    return jnp.dot(a, b, preferred_element_type=F32)


def _dot_nt(a, b):
    return lax.dot_general(a, b, (((1,), (1,)), ((), ())), preferred_element_type=F32)


def _dot_tn(a, b):
    return lax.dot_general(a, b, (((0,), (0,)), ((), ())), preferred_element_type=F32)


def _log2(n):
    assert n & (n - 1) == 0
    return n.bit_length() - 1


def _silu(x):
    return x * jax.nn.sigmoid(x)


def _mod_rows(mod_ref, j):
    return mod_ref[3 * j:3 * j + 1, :], mod_ref[3 * j + 1:3 * j + 2, :], mod_ref[3 * j + 2:3 * j + 3, :]


def _layer_norm(z, g, b):
    mu = jnp.mean(z, axis=-1, keepdims=True)
    zc = z - mu
    var = jnp.mean(zc * zc, axis=-1, keepdims=True)
    return zc * lax.rsqrt(var + LN_EPS) * g + b


def _ada_kernel(c_ref, w_ref, b_ref, o_ref):
    c = c_ref[...]
    r = jnp.dot(_silu(c), w_ref[0], precision=HIGHEST, preferred_element_type=F32)
    o_ref[0] = r[0:1, :] + b_ref[0]


def _ada_call(c, w_ada, b_ada):
    depth, d, n = w_ada.shape
    tn = n // 8
    c8 = jnp.broadcast_to(c, (SUBLANES, d))
    return pl.pallas_call(
        _ada_kernel,
        grid=(depth, n // tn),
        in_specs=[pl.BlockSpec((SUBLANES, d), lambda l, j: (0, 0)),
                  pl.BlockSpec((1, d, tn), lambda l, j: (l, 0, j)),
                  pl.BlockSpec((1, 1, tn), lambda l, j: (l, 0, j))],
        out_specs=pl.BlockSpec((1, 1, tn), lambda l, j: (l, 0, j)),
        out_shape=jax.ShapeDtypeStruct((depth, 1, n), F32),
        compiler_params=_cparams("arbitrary", "arbitrary"),
        name="ada_mod",
    )(c8, w_ada, b_ada.reshape(depth, 1, n))


def _ffn_kernel(x_ref, mod_ref, wi_ref, wo_ref, lng_ref, lnb_ref, o_ref, *, sub, tf, alpha):
    shift, scale, gate = _mod_rows(mod_ref, sub)
    ff = wo_ref.shape[0]
    x = x_ref[...]
    h = (x * (1.0 + scale) + shift).astype(BF16)
    y = jnp.zeros(x.shape, F32)
    for c0 in range(0, ff, tf):
        g = _dot(h, wi_ref[:, c0:c0 + tf])
        u = _dot(h, wi_ref[:, ff + c0:ff + c0 + tf])
        y = y + _dot((_silu(g) * u).astype(BF16), wo_ref[c0:c0 + tf, :])
    z = alpha * x + (0.5 * (1.0 + gate)) * y
    o_ref[...] = _layer_norm(z, lng_ref[sub:sub + 1, :], lnb_ref[sub:sub + 1, :])


def _ffn_call(x, mod, w_in, w_out, ln_g, ln_b, *, sub, alpha, tm, tf):
    s, d = x.shape
    ff = w_out.shape[0]
    assert ff % tf == 0
    kern = functools.partial(_ffn_kernel, sub=sub, tf=tf, alpha=alpha)
    resident = lambda arr: pl.BlockSpec(arr.shape, lambda i: (0, 0), pipeline_mode=pl.Buffered(1))
    return pl.pallas_call(
        kern,
        grid=(s // tm,),
        in_specs=[pl.BlockSpec((tm, d), lambda i: (i, 0)),
                  pl.BlockSpec((3 * N_SUB, d), lambda i: (0, 0)),
                  resident(w_in), resident(w_out),
                  pl.BlockSpec((N_SUB, d), lambda i: (0, 0)),
                  pl.BlockSpec((N_SUB, d), lambda i: (0, 0))],
        out_specs=pl.BlockSpec((tm, d), lambda i: (i, 0)),
        out_shape=jax.ShapeDtypeStruct((s, d), F32),
        compiler_params=_cparams("arbitrary"),
        name="ffn",
    )(x, mod, w_in, w_out, ln_g, ln_b)


_PROJ_OUTS = (("gates", 3 * 1024, BF16), ("a", POOL_WIDTH, F32), ("h4", 4 * HGRN_WIDTH, F32),
              ("qi", IDX_HEADS * IDX_DIM, BF16), ("q", ATTN_WIDTH, BF16), ("k", ATTN_WIDTH, BF16),
              ("v", ATTN_WIDTH, BF16), ("ki2", 2 * IDX_DIM, BF16), ("wi", LANES, F32))
_PROJ_CHUNK = 1024


def _proj_kernel(x_ref, mod_ref, w_ref, *o_refs):
    shift, scale, _ = _mod_rows(mod_ref, 1)
    h = (x_ref[...] * (1.0 + scale) + shift).astype(BF16)
    c0 = 0
    for (_, width, dt), o_ref in zip(_PROJ_OUTS, o_refs):
        for cc in range(0, width, _PROJ_CHUNK):
            cw = min(_PROJ_CHUNK, width - cc)
            o_ref[:, cc:cc + cw] = _dot(h, w_ref[:, c0 + cc:c0 + cc + cw]).astype(dt)
        c0 += width


def _proj_weight(w):
    d = w.shape[0]
    widths = (POOL_WIDTH, ATTN_WIDTH, ATTN_WIDTH, ATTN_WIDTH, IDX_HEADS * IDX_DIM, IDX_DIM, IDX_HEADS,
              HGRN_WIDTH, HGRN_WIDTH, HGRN_WIDTH, HGRN_WIDTH, 3 * d)
    parts, c0 = [], 0
    for wd in widths:
        parts.append(w[:, c0:c0 + wd])
        c0 += wd
    a, q, k, v, qi, ki, wi, hq, hf, hi, hg, gates = parts
    wi_pad = jnp.pad(wi, ((0, 0), (0, LANES - IDX_HEADS)))
    q = q * (ATTN_HEAD_DIM ** -0.5 * _LOG2E)
    return jnp.concatenate([gates, a, hq, hf, hi, hg, qi, q, k, v, ki, ki, wi_pad], axis=1).astype(BF16)


def _proj_call(x, mod, w, *, tm):
    s, d = x.shape
    n = w.shape[1]
    assert n == sum(wd for _, wd, _ in _PROJ_OUTS)
    return pl.pallas_call(
        _proj_kernel,
        grid=(s // tm,),
        in_specs=[pl.BlockSpec((tm, d), lambda i: (i, 0)),
                  pl.BlockSpec((3 * N_SUB, d), lambda i: (0, 0)),
                  pl.BlockSpec((d, n), lambda i: (0, 0))],
        out_specs=[pl.BlockSpec((tm, wd), lambda i: (i, 0)) for _, wd, _ in _PROJ_OUTS],
        out_shape=[jax.ShapeDtypeStruct((s, wd), dt) for _, wd, dt in _PROJ_OUTS],
        compiler_params=_cparams("arbitrary"),
        name="mix_proj",
    )(x, mod, w)


_BISECT_MAX_ITERS = 320
_F32_TINY = float(jnp.finfo(jnp.float32).tiny)
_COARSE_PASSES = 9
_COARSE_MIN_ABS, _COARSE_MAX_ABS = 1e-30, 1e30
_KIND_FAR, _KIND_PAD, _N_KINDS = 2, 3, 4
_VT_ROWS = LANES + 16
_SMASK_FULL, _SMASK_DIAG, _SMASK_PAD = 0, 1, 2
_LOG2E = math.log2(math.e)


def _t5_bucket(dist):
    max_exact = REL_BUCKETS // 2
    d32 = jnp.maximum(dist, 1).astype(F32)
    large = max_exact + (jnp.log(d32 / max_exact) / math.log(REL_MAX_DIST / max_exact)
                         * (REL_BUCKETS - max_exact)).astype(I32)
    large = jnp.minimum(large, REL_BUCKETS - 1)
    return jnp.where(dist < max_exact, dist, large)


def _dsa_kernel(relb_ref, qi_ref, wi_ref, q_ref, ki2_ref, k_ref, vt_ref, o_ref,
                s_sc, sb_sc, bias_sc, smask_sc, lg_sc, qim_sc, qm_sc, m_sc, acc_sc, tie_sc, cnt_sc,
                *, tq, topk, idx_scale):
    tk = tq
    i = pl.program_id(0)
    krow = lax.broadcasted_iota(I32, (tk, tq), 0)
    qcol = lax.broadcasted_iota(I32, (tk, tq), 1)
    causal = krow <= qcol
    lane = lax.broadcasted_iota(I32, (tq, LANES), 1)
    lo_half = lane < ATTN_HEAD_DIM

    @pl.when(i == 0)
    def _():
        for which in range(2):
            bucket = _t5_bucket(jnp.maximum(qcol - krow + which * tq, 0))
            for h in range(ATTN_HEADS):
                b = jnp.zeros((tk, tq), F32)
                for bk in range(REL_BUCKETS):
                    b = jnp.where(bucket == bk, relb_ref[bk, h] * _LOG2E, b)
                bias_sc[h, which] = (jnp.where(causal, b, NEG_INF) if which == 0 else b).astype(BF16)
        for h in range(ATTN_HEADS):
            bias_sc[h, _KIND_FAR] = jnp.full((tk, tq), relb_ref[REL_BUCKETS - 1, h] * _LOG2E, F32).astype(BF16)
            bias_sc[h, _KIND_PAD] = jnp.full((tk, tq), NEG_INF, BF16)
        smask_sc[_SMASK_FULL] = jnp.zeros((tk, tq), F32)
        smask_sc[_SMASK_DIAG] = jnp.where(causal, 0.0, NEG_INF)
        smask_sc[_SMASK_PAD] = jnp.full((tk, tq), NEG_INF, F32)

    qi = qi_ref[...]
    for h in range(IDX_HEADS):
        pair = qi[:, LANES * (h // 2):LANES * (h // 2 + 1)]
        qim_sc[h] = jnp.where(lo_half if h % 2 == 0 else ~lo_half, pair, jnp.zeros_like(pair))
    qs = q_ref[...]
    for h in range(ATTN_HEADS):
        pair = qs[:, LANES * (h // 2):LANES * (h // 2 + 1)]
        qm_sc[h] = jnp.where(lo_half if h % 2 == 0 else ~lo_half, pair, jnp.zeros_like(pair))

    wt = jnp.transpose(wi_ref[...])

    def key_rows(jb):
        return pl.ds(pl.multiple_of(jb * tk, tk), tk)

    def fold(x):
        return x.reshape(tk // SUBLANES, SUBLANES, tq)

    def index_scores(jb):
        kb = ki2_ref[key_rows(jb), :]
        acc = jnp.zeros((tk, tq), F32)
        for h in range(IDX_HEADS):
            acc = acc + wt[h:h + 1, :] * jnp.maximum(_dot_nt(kb, qim_sc[h]), 0.0)
        return acc * idx_scale

    npair = (i + 2) // 2
    last_kv = k_ref.shape[0] // tk - 1

    def p1_pair(j, carry):
        rmin, rmax = carry
        for jb in (2 * j, 2 * j + 1):
            raw = index_scores(jnp.minimum(jb, last_kv))
            kind = jnp.where(jb > i, _SMASK_PAD, jnp.where(jb == i, _SMASK_DIAG, _SMASK_FULL))
            sc = raw + smask_sc[kind]
            s_sc[jb] = sc
            sb_sc[jb] = sc.astype(BF16)
            rmin = jnp.minimum(rmin, jnp.min(fold(raw), axis=0))
            rmax = jnp.maximum(rmax, jnp.max(fold(raw), axis=0))
        return rmin, rmax

    rmin, rmax = lax.fori_loop(0, npair, p1_pair, (jnp.full((SUBLANES, tq), POS_INF, F32),
                                                   jnp.full((SUBLANES, tq), NEG_INF, F32)))
    rowmin = jnp.min(rmin, axis=0, keepdims=True)
    rowmax = jnp.max(rmax, axis=0, keepdims=True)

    kf = float(topk)
    nfin = (i * tq + lax.broadcasted_iota(I32, (1, tq), 1) + 1).astype(F32)
    small = nfin <= kf

    def count_ge(x):
        def body(j, cnt):
            for jb in (2 * j, 2 * j + 1):
                cnt = cnt + jnp.sum(fold(jnp.where(s_sc[jb] >= x, 1.0, 0.0)), axis=0)
            return cnt
        cnt = lax.fori_loop(0, npair, body, jnp.zeros((SUBLANES, tq), F32))
        return jnp.sum(cnt, axis=0, keepdims=True)

    def count_ge16(xb):
        one, zero = jnp.ones((), BF16), jnp.zeros((), BF16)
        rows16 = 2 * SUBLANES

        def body(j, cnt):
            parts = []
            for jb in (2 * j, 2 * j + 1):
                c = jnp.where(sb_sc[jb] >= xb, one, zero)
                parts += [c[r0:r0 + rows16, :] for r0 in range(0, tk, rows16)]
            while len(parts) > 1:
                parts = [a + b for a, b in zip(parts[0::2], parts[1::2])]
            return cnt + parts[0].astype(F32)
        cnt = lax.fori_loop(0, npair, body, jnp.zeros((rows16, tq), F32))
        return jnp.sum(cnt, axis=0, keepdims=True)

    def coarse_probe(mid):
        xb = mid.astype(BF16)
        xbits = lax.bitcast_convert_type(xb.astype(F32), I32)
        neg = xbits < 0
        mu = jnp.where(neg, xbits + 0x8000, xbits - 0x8000)
        odd = (xbits & 0x10000) != 0
        tbits = jnp.where(odd, jnp.where(neg, mu - 1, mu + 1), mu)
        t = lax.bitcast_convert_type(tbits, F32)
        ok = jnp.logical_and(jnp.abs(xb.astype(F32)) > _COARSE_MIN_ABS, jnp.abs(xb.astype(F32)) < _COARSE_MAX_ABS)
        return xb, t, ok

    def narrow(st, probe, c, active):
        lo, hi, clo, chi, done = st
        up = jnp.logical_and(active, c >= kf)
        dn = jnp.logical_and(active, c < kf)
        return (jnp.where(up, probe, lo), jnp.where(dn, probe, hi), jnp.where(up, c, clo),
                jnp.where(dn, c, chi), jnp.where(jnp.logical_and(active, c == kf), 1.0, done))

    def coarse_body(_, st):
        lo, hi, _, _, done = st
        xb, t, ok = coarse_probe(0.5 * lo + 0.5 * hi)
        ok = jnp.logical_and(ok, jnp.logical_and(t > lo, t < hi))
        return narrow(st, t, count_ge16(xb), jnp.logical_and(done < 0.5, ok))

    def fine_cond(st):
        it, (_, _, _, _, done) = st
        return jnp.logical_and(it < _BISECT_MAX_ITERS, jnp.min(done) < 0.5)

    def fine_body(st):
        it, inner = st
        lo, hi, _, _, done = inner
        mid = 0.5 * lo + 0.5 * hi
        mid = jnp.where(jnp.logical_and(lo == 0.0, hi > _F32_TINY), _F32_TINY, mid)
        mid = jnp.where(jnp.logical_and(lo < 0.0, hi > 0.0), 0.0, mid)
        stuck = jnp.logical_or(jnp.logical_or(mid <= lo, mid >= hi),
                               jnp.logical_and(lo == 0.0, hi <= _F32_TINY))
        lo, hi, clo, chi, done = narrow(inner, mid, count_ge(mid),
                                        jnp.logical_and(done < 0.5, jnp.logical_not(stuck)))
        return it + 1, (lo, hi, clo, chi, jnp.where(stuck, 1.0, done))

    mbits = lax.bitcast_convert_type(rowmax, I32)
    above_max = jnp.where(rowmax == 0.0, _F32_TINY,
                          lax.bitcast_convert_type(jnp.where(rowmax > 0.0, mbits + 1, mbits - 1), F32))
    st0 = (jnp.where(small, NEG_INF, rowmin), above_max, nfin, jnp.zeros((1, tq), F32),
           jnp.where(small, 1.0, 0.0))
    st1 = lax.fori_loop(0, _COARSE_PASSES, coarse_body, st0)
    _, (lo, hi, clo, chi, _) = lax.while_loop(fine_cond, fine_body, (jnp.int32(0), st1))
    budget = kf - chi
    need_ties = jnp.max((clo - chi) - budget) > 0.5

    pairs = [slice(LANES * (h // 2), LANES * (h // 2 + 1)) for h in range(ATTN_HEADS)]

    def issue_qk(jb, slot):
        kb = k_ref[key_rows(jnp.minimum(jb, last_kv)), :]
        for h in range(ATTN_HEADS):
            lg_sc[slot, h] = _dot_nt(kb[:, pairs[h]], qm_sc[h]).astype(BF16)

    def attend(jb, slot, thr, cum=None):
        s = s_sc[jb]
        if cum is None:
            sel = s >= thr
        else:
            cand = jnp.logical_and(s >= lo, s < hi)
            before = (qcol < krow).astype(BF16)
            rank = cum + _dot(before, jnp.where(cand, 1.0, 0.0).astype(BF16))
            sel = jnp.logical_or(s >= hi, jnp.logical_and(cand, rank < budget))
        kind = jnp.where(jb > i, _KIND_PAD, jnp.minimum(i - jb, _KIND_FAR))
        selm = jnp.where(sel, 0.0, NEG_INF).astype(BF16)
        probs, alphas = [], []
        for h in range(ATTN_HEADS):
            lg = lg_sc[slot, h] + bias_sc[h, kind] + selm
            m_old = m_sc[h]
            m_new = jnp.maximum(m_old, jnp.max(lg, axis=0, keepdims=True).astype(F32))
            m_safe = jnp.where(m_new == NEG_INF, 0.0, m_new)
            alphas.append(jnp.exp2(m_old - m_safe))
            probs.append(jnp.exp2(lg - m_safe.astype(BF16)))
            m_sc[h] = m_new
        vb = vt_ref[jnp.minimum(jb, last_kv)]
        for h in range(ATTN_HEADS):
            vrows = slice(_VT_ROWS * (h // 2), _VT_ROWS * (h // 2 + 1))
            acc_sc[h] = alphas[h] * acc_sc[h] + _dot(vb[vrows, :], probs[h])

    def run(with_ties):
        m_sc[...] = jnp.full(m_sc.shape, NEG_INF, F32)
        acc_sc[...] = jnp.zeros_like(acc_sc)
        tie_sc[...] = jnp.zeros_like(tie_sc)
        if with_ties:
            def cand_body(j, carry):
                for jb in (2 * j, 2 * j + 1):
                    s = s_sc[jb]
                    cand = jnp.logical_and(s >= lo, s < hi)
                    cnt_sc[jb] = jnp.sum(jnp.where(cand, 1.0, 0.0), axis=0, keepdims=True)
                return carry
            lax.fori_loop(0, npair, cand_body, 0)
        issue_qk(0, 0)

        def pair(j, thr0, thr1, cum0=None, cum1=None):
            issue_qk(2 * j + 1, 1)
            attend(2 * j, 0, thr0, cum0)
            issue_qk(2 * j + 2, 0)
            attend(2 * j + 1, 1, thr1, cum1)

        def pair_body(j, carry):
            if not with_ties:
                pair(j, lo, lo)
                return carry
            cum0 = tie_sc[...]
            cum1 = cum0 + cnt_sc[2 * j]
            cum2 = cum1 + cnt_sc[2 * j + 1]
            tie_sc[...] = cum2
            all0, all1 = cum1 <= budget, cum2 <= budget
            runs_out = jnp.logical_or(jnp.logical_and(jnp.logical_not(all0), cum0 < budget),
                                      jnp.logical_and(jnp.logical_not(all1), cum1 < budget))
            ranked = jnp.max(jnp.where(runs_out, 1.0, 0.0)) > 0.5

            @pl.when(ranked)
            def _():
                pair(j, None, None, cum0, cum1)

            @pl.when(jnp.logical_not(ranked))
            def _():
                pair(j, jnp.where(all0, lo, hi), jnp.where(all1, lo, hi))
            return carry
        lax.fori_loop(0, npair, pair_body, 0)

    @pl.when(need_ties)
    def _():
        run(True)

    @pl.when(jnp.logical_not(need_ties))
    def _():
        run(False)

    first_head = lax.broadcasted_iota(I32, (LANES, tq), 0) < ATTN_HEAD_DIM
    normed = [acc_sc[h, 0:LANES, :] / acc_sc[h, LANES:LANES + 1, :] for h in range(ATTN_HEADS)]
    for p in range(ATTN_HEADS // 2):
        pair_t = jnp.where(first_head, normed[2 * p], normed[2 * p + 1])
        o_ref[:, LANES * p:LANES * (p + 1)] = jnp.transpose(pair_t)


def _dsa_call(rel_bias, qi, wi, q, ki2, k, v, *, tq):
    s = q.shape[0]
    assert tq > REL_MAX_DIST and s % tq == 0
    nb = s // tq
    topk = min(TOPK_MAX, s // 4)
    kern = functools.partial(_dsa_kernel, tq=tq, topk=topk,
                             idx_scale=(IDX_DIM ** -0.5) * (IDX_HEADS ** -0.5))
    whole = lambda width: pl.BlockSpec((s, width), lambda i: (0, 0), pipeline_mode=pl.Buffered(1))
    n_pairs = ATTN_WIDTH // LANES
    vt = jnp.transpose(v.reshape(nb, tq, n_pairs, LANES), (0, 2, 3, 1))
    vt = jnp.concatenate([vt, jnp.ones((nb, n_pairs, _VT_ROWS - LANES, tq), BF16)], axis=2)
    vt = vt.reshape(nb, n_pairs * _VT_ROWS, tq)
    return pl.pallas_call(
        kern,
        grid=(nb,),
        in_specs=[pl.BlockSpec(memory_space=pltpu.SMEM),
                  pl.BlockSpec((tq, IDX_HEADS * IDX_DIM), lambda i: (i, 0)),
                  pl.BlockSpec((tq, LANES), lambda i: (i, 0)),
                  pl.BlockSpec((tq, ATTN_WIDTH), lambda i: (i, 0)),
                  whole(2 * IDX_DIM), whole(ATTN_WIDTH),
                  pl.BlockSpec(vt.shape, lambda i: (0, 0, 0), pipeline_mode=pl.Buffered(1))],
        out_specs=pl.BlockSpec((tq, ATTN_WIDTH), lambda i: (i, 0)),
        out_shape=jax.ShapeDtypeStruct((s, ATTN_WIDTH), F32),
        scratch_shapes=[pltpu.VMEM((nb + nb % 2, tq, tq), F32),
                        pltpu.VMEM((nb + nb % 2, tq, tq), BF16),
                        pltpu.VMEM((ATTN_HEADS, _N_KINDS, tq, tq), BF16),
                        pltpu.VMEM((3, tq, tq), F32),
                        pltpu.VMEM((2, ATTN_HEADS, tq, tq), BF16),
                        pltpu.VMEM((IDX_HEADS, tq, LANES), BF16),
                        pltpu.VMEM((ATTN_HEADS, tq, LANES), BF16),
                        pltpu.VMEM((ATTN_HEADS, 1, tq), F32),
                        pltpu.VMEM((ATTN_HEADS, _VT_ROWS, tq), F32),
                        pltpu.VMEM((1, tq), F32),
                        pltpu.VMEM((nb + nb % 2, 1, tq), F32)],
        compiler_params=_cparams("arbitrary"),
        name="dsa_attention",
    )(rel_bias, qi, wi, q, ki2, k, vt)


_HGRN_CHUNK = 128
_HGRN_BASE = SUBLANES


def _hgrn_kernel(h4_ref, lb_ref, ng_ref, o_ref, st_sc, *, layer, rows):
    c_len = _HGRN_CHUNK

    @pl.when(pl.program_id(0) == 0)
    def _():
        st_sc[...] = jnp.zeros_like(st_sc)

    lb_all = lb_ref[...]
    e = jnp.exp(lb_all - jnp.max(lb_all, axis=0, keepdims=True))
    sm = e / jnp.sum(e, axis=0, keepdims=True)
    lbs = jnp.sum(sm[0:layer + 1, :], axis=0, keepdims=True) - sm[0:1, :]

    r1 = lax.broadcasted_iota(I32, (c_len, 1), 0)
    rr = lax.broadcasted_iota(I32, (c_len, c_len), 0)
    cc = lax.broadcasted_iota(I32, (c_len, c_len), 1)

    def block_ref(a, blk, r):
        x = a.reshape(c_len // blk, blk, a.shape[-1])[:, r:r + 1, :]
        return jnp.broadcast_to(x, (c_len // blk, blk, a.shape[-1])).reshape(c_len, a.shape[-1])

    w = HGRN_WIDTH
    for c in range(rows // c_len):
        rs = slice(c * c_len, (c + 1) * c_len)
        hq = h4_ref[rs, 0:w]
        z = h4_ref[rs, w:2 * w]
        val = h4_ref[rs, 2 * w:3 * w]
        hg = h4_ref[rs, 3 * w:4 * w]
        key = (1.0 - lbs) * jax.nn.sigmoid(-z)
        g = jnp.log1p(-jnp.minimum(key, KEY_MAX))
        a_cum = g
        sh = 1
        while sh < c_len:
            a_cum = a_cum + jnp.where(r1 >= sh, pltpu.roll(a_cum, sh, 0), 0.0)
            sh *= 2
        qf = _silu(hq)
        outs = []
        for h in range(HGRN_HEADS):
            hs = slice(h * HGRN_DIM, (h + 1) * HGRN_DIM)
            ah, qh, kh, vh = a_cum[:, hs], qf[:, hs], key[:, hs], val[:, hs]
            vh16 = vh.astype(BF16)
            st = st_sc[h]
            o = _dot_nt((qh * jnp.exp(ah)).astype(BF16), st.astype(BF16))
            ref = block_ref(ah, _HGRN_BASE, _HGRN_BASE // 2 - 1)
            sc = _dot_nt((qh * jnp.exp(ah - ref)).astype(BF16), (kh * jnp.exp(ref - ah)).astype(BF16))
            same = (rr >> _log2(_HGRN_BASE)) == (cc >> _log2(_HGRN_BASE))
            scores = jnp.where(jnp.logical_and(same, cc <= rr), sc, 0.0)
            half = _HGRN_BASE
            while half < c_len:
                blk = 2 * half
                ref = block_ref(ah, blk, half - 1)
                second = (r1 & (blk - 1)) >= half
                ql = jnp.where(second, qh * jnp.exp(jnp.minimum(ah - ref, 0.0)), 0.0)
                kl = jnp.where(second, 0.0, kh * jnp.exp(jnp.minimum(ref - ah, 0.0)))
                sc = _dot_nt(ql.astype(BF16), kl.astype(BF16))
                scores = scores + jnp.where((rr >> _log2(blk)) == (cc >> _log2(blk)), sc, 0.0)
                half = blk
            o = o + _dot(scores.astype(BF16), vh16)
            a_last = ah[c_len - 1:c_len, :]
            kd = (kh * jnp.exp(a_last - ah)).astype(BF16)
            st_sc[h] = st * jnp.exp(a_last) + _dot_tn(vh16, kd)
            o = o * lax.rsqrt(jnp.mean(o * o, axis=-1, keepdims=True) + RMS_EPS)
            outs.append(o)
        o_ref[rs, :] = jnp.concatenate(outs, axis=1) * ng_ref[...] * _silu(hg)


def _hgrn_call(h4, lb, ng, *, layer, rows):
    s = h4.shape[0]
    kern = functools.partial(_hgrn_kernel, layer=layer, rows=rows)
    return pl.pallas_call(
        kern,
        grid=(s // rows,),
        in_specs=[pl.BlockSpec((rows, 4 * HGRN_WIDTH), lambda i: (i, 0)),
                  pl.BlockSpec(lb.shape, lambda i: (0, 0)),
                  pl.BlockSpec((1, HGRN_WIDTH), lambda i: (0, 0))],
        out_specs=pl.BlockSpec((rows, HGRN_WIDTH), lambda i: (i, 0)),
        out_shape=jax.ShapeDtypeStruct((s, HGRN_WIDTH), F32),
        scratch_shapes=[pltpu.VMEM((HGRN_HEADS, HGRN_DIM, HGRN_DIM), F32)],
        compiler_params=_cparams("arbitrary"),
        name="hgrn2",
    )(h4, lb, ng)


_POOL_HALO = 16


def _merge_kernel(x_ref, mod_ref, a_ref, ah_ref, yb_ref, yc_ref, g_ref, pw_ref, ps_ref, wb_ref, wo_ref,
                  lng_ref, lnb_ref, o_ref, *, tm, alpha):
    i = pl.program_id(0)
    d = x_ref.shape[1]
    _, _, gate = _mod_rows(mod_ref, 1)
    a = a_ref[...]
    halo = jnp.where(i == 0, 0.0, ah_ref[...])
    ext = jnp.concatenate([halo, a], axis=0)
    lane = lax.broadcasted_iota(I32, (tm, POOL_WIDTH), 1)
    grp = lane >> _log2(POOL_GROUP)
    pos1 = (i * tm + lax.broadcasted_iota(I32, (tm, POOL_WIDTH), 0) + 1).astype(F32)
    win = jnp.zeros((tm, POOL_WIDTH), F32)
    cnt = jnp.zeros((tm, POOL_WIDTH), F32)
    ssum, have = ext, 1
    for gi, wd in enumerate(POOL_WINDOWS):
        while have < wd:
            ssum = ssum + pltpu.roll(ssum, have, 0)
            have *= 2
        assert have == wd
        win = jnp.where(grp == gi, ssum[_POOL_HALO:, :], win)
        cnt = jnp.where(grp == gi, jnp.minimum(pos1, float(wd)), cnt)
    dlt = (win / cnt - a).astype(BF16)
    ya = _dot(dlt, pw_ref[...]) * ps_ref[...]
    za = _dot(ya.astype(BF16), wb_ref[0:POOL_WIDTH, :])
    zb = _dot(yb_ref[...].astype(BF16), wb_ref[POOL_WIDTH:POOL_WIDTH + ATTN_WIDTH, :])
    zc = _dot(yc_ref[...].astype(BF16), wb_ref[POOL_WIDTH + ATTN_WIDTH:, :])
    g0, g1, g2 = (jax.nn.sigmoid(g_ref[:, j * d:(j + 1) * d].astype(F32)) for j in range(3))
    merged = g0 * za + g1 * zb + g2 * zc
    y = _dot(merged.astype(BF16), wo_ref[...])
    z = alpha * x_ref[...] + (1.0 + gate) * y
    o_ref[...] = _layer_norm(z, lng_ref[1:2, :], lnb_ref[1:2, :])


def _merge_call(x, mod, a, yb, yc, gates, pw_bd, ps, wb, wo, ln_g, ln_b, *, tm, alpha):
    s, d = x.shape
    kern = functools.partial(_merge_kernel, tm=tm, alpha=alpha)
    full = lambda arr: pl.BlockSpec(arr.shape, lambda i: (0, 0))
    hb = tm // _POOL_HALO
    return pl.pallas_call(
        kern,
        grid=(s // tm,),
        in_specs=[pl.BlockSpec((tm, d), lambda i: (i, 0)),
                  full(mod),
                  pl.BlockSpec((tm, POOL_WIDTH), lambda i: (i, 0)),
                  pl.BlockSpec((_POOL_HALO, POOL_WIDTH), lambda i: (jnp.maximum(i * hb - 1, 0), 0)),
                  pl.BlockSpec((tm, ATTN_WIDTH), lambda i: (i, 0)),
                  pl.BlockSpec((tm, HGRN_WIDTH), lambda i: (i, 0)),
                  pl.BlockSpec((tm, 3 * d), lambda i: (i, 0)),
                  full(pw_bd), full(ps), full(wb), full(wo), full(ln_g), full(ln_b)],
        out_specs=pl.BlockSpec((tm, d), lambda i: (i, 0)),
        out_shape=jax.ShapeDtypeStruct((s, d), F32),
        compiler_params=_cparams("arbitrary"),
        name="pool_merge",
    )(x, mod, a, a, yb, yc, gates, pw_bd, ps, wb, wo, ln_g, ln_b)


def _pool_blockdiag(pool_w):
    n = pool_w.shape[0]
    rows = [jnp.concatenate([pool_w[g] if g == j else jnp.zeros_like(pool_w[g]) for j in range(n)], axis=1)
            for g in range(n)]
    return jnp.concatenate(rows, axis=0)


def kernel(x, c, w_ada, b_ada, ln_g, ln_b, ffn_w_in, ffn_w_out, mix_w_in, pool_w, pool_scale, rel_bias,
           hgrn_lb, hgrn_norm_g, w_branch, w_out):
    batch, s, d = x.shape
    assert batch == 1, "kernels are written for a single sequence"
    depth = w_ada.shape[0]
    alpha = (2 * depth) ** 0.25
    tm_ffn = min(512, s)
    tf = 256
    tm_proj = min(256, s)
    tq = min(256, s)
    tm_merge = min(256, s)
    hgrn_rows = min(256, s)

    mods = _ada_call(c, w_ada, b_ada)
    xs = x[0]
    for l in range(depth):
        mod = mods[l].reshape(3 * N_SUB, d)
        xs = _ffn_call(xs, mod, ffn_w_in[l, 0].astype(BF16), ffn_w_out[l, 0].astype(BF16), ln_g[l], ln_b[l],
                       sub=0, alpha=alpha, tm=tm_ffn, tf=tf)
        gates, a, h4, qi, q, k, v, ki2, wi = _proj_call(xs, mod, _proj_weight(mix_w_in[l]), tm=tm_proj)
        yb = _dsa_call(rel_bias, qi, wi, q, ki2, k, v, tq=tq)
        yc = _hgrn_call(h4, hgrn_lb, hgrn_norm_g[l].reshape(1, -1), layer=l, rows=hgrn_rows)
        xs = _merge_call(xs, mod, a, yb, yc, gates, _pool_blockdiag(pool_w[l]).astype(BF16),
                         pool_scale[l].reshape(1, -1), w_branch[l].astype(BF16), w_out[l].astype(BF16),
                         ln_g[l], ln_b[l], tm=tm_merge, alpha=alpha)
        xs = _ffn_call(xs, mod, ffn_w_in[l, 1].astype(BF16), ffn_w_out[l, 1].astype(BF16), ln_g[l], ln_b[l],
                       sub=2, alpha=alpha, tm=tm_ffn, tf=tf)
    return xs[None]
```

```python
import functools
import math

import jax
import jax.numpy as jnp
from jax import lax
from jax.experimental import pallas as pl
from jax.experimental.pallas import tpu as pltpu

F32 = jnp.float32
BF16 = jnp.bfloat16
I32 = jnp.int32
HIGHEST = lax.Precision.HIGHEST

POOL_WINDOWS = (2, 4, 8, 16)
POOL_GROUP = 64
POOL_WIDTH = POOL_GROUP * len(POOL_WINDOWS)
ATTN_HEADS = 4
ATTN_HEAD_DIM = 64
ATTN_WIDTH = ATTN_HEADS * ATTN_HEAD_DIM
IDX_HEADS = 8
IDX_DIM = 64
TOPK_MAX = 256
HGRN_HEADS = 4
HGRN_DIM = 128
HGRN_WIDTH = HGRN_HEADS * HGRN_DIM
KEY_MAX = 1.0 - 1e-6
REL_BUCKETS = 32
REL_MAX_DIST = 128
N_SUB = 3
LN_EPS = 1e-5
RMS_EPS = 1e-6

LANES = 128
SUBLANES = 8
VMEM_LIMIT_BYTES = 58 * 1024 * 1024

NEG_INF = float("-inf")
POS_INF = float("inf")


def _cparams(*sem):
    return pltpu.CompilerParams(dimension_semantics=sem, vmem_limit_bytes=VMEM_LIMIT_BYTES)


def _dot(a, b):
    return jnp.dot(a, b, preferred_element_type=F32)


def _dot_nt(a, b):
    return lax.dot_general(a, b, (((1,), (1,)), ((), ())), preferred_element_type=F32)


def _dot_tn(a, b):
    return lax.dot_general(a, b, (((0,), (0,)), ((), ())), preferred_element_type=F32)


def _log2(n):
    assert n & (n - 1) == 0
    return n.bit_length() - 1


def _silu(x):
    return x * jax.nn.sigmoid(x)


def _mod_rows(mod_ref, j):
    return mod_ref[3 * j:3 * j + 1, :], mod_ref[3 * j + 1:3 * j + 2, :], mod_ref[3 * j + 2:3 * j + 3, :]


def _layer_norm(z, g, b):
    mu = jnp.mean(z, axis=-1, keepdims=True)
    zc = z - mu
    var = jnp.mean(zc * zc, axis=-1, keepdims=True)
    return zc * lax.rsqrt(var + LN_EPS) * g + b


def _ada_kernel(c_ref, w_ref, b_ref, o_ref):
    c = c_ref[...]
    o_ref[0] = jnp.sum(_silu(c) * w_ref[0], axis=0, keepdims=True) + b_ref[0]


def _ada_call(c, w_ada, b_ada):
    depth, d, n = w_ada.shape
    tn = n // 8
    return pl.pallas_call(
        _ada_kernel,
        grid=(depth, n // tn),
        in_specs=[pl.BlockSpec((d, 1), lambda l, j: (0, 0)),
                  pl.BlockSpec((1, d, tn), lambda l, j: (l, 0, j)),
                  pl.BlockSpec((1, 1, tn), lambda l, j: (l, 0, j))],
        out_specs=pl.BlockSpec((1, 1, tn), lambda l, j: (l, 0, j)),
        out_shape=jax.ShapeDtypeStruct((depth, 1, n), F32),
        compiler_params=_cparams("arbitrary", "arbitrary"),
        name="ada_mod",
    )(c.reshape(d, 1), w_ada, b_ada.reshape(depth, 1, n))


def _ffn_kernel(x_ref, mod_ref, wi_ref, wo_ref, lng_ref, lnb_ref, o_ref, *, sub, tf, alpha):
    shift, scale, gate = _mod_rows(mod_ref, sub)
    ff = wo_ref.shape[0]
    x = x_ref[...]
    h = (x * (1.0 + scale) + shift).astype(BF16)
    y = jnp.zeros(x.shape, F32)
    for c0 in range(0, ff, tf):
        g = _dot(h, wi_ref[:, c0:c0 + tf])
        u = _dot(h, wi_ref[:, ff + c0:ff + c0 + tf])
        y = y + _dot((_silu(g) * u).astype(BF16), wo_ref[c0:c0 + tf, :])
    z = alpha * x + (0.5 * (1.0 + gate)) * y
    o_ref[...] = _layer_norm(z, lng_ref[sub:sub + 1, :], lnb_ref[sub:sub + 1, :])


def _ffn_call(x, mod, w_in, w_out, ln_g, ln_b, *, sub, alpha, tm, tf):
    s, d = x.shape
    ff = w_out.shape[0]
    assert ff % tf == 0
    kern = functools.partial(_ffn_kernel, sub=sub, tf=tf, alpha=alpha)
    resident = lambda arr: pl.BlockSpec(arr.shape, lambda i: (0, 0), pipeline_mode=pl.Buffered(1))
    return pl.pallas_call(
        kern,
        grid=(s // tm,),
        in_specs=[pl.BlockSpec((tm, d), lambda i: (i, 0)),
                  pl.BlockSpec((3 * N_SUB, d), lambda i: (0, 0)),
                  resident(w_in), resident(w_out),
                  pl.BlockSpec((N_SUB, d), lambda i: (0, 0)),
                  pl.BlockSpec((N_SUB, d), lambda i: (0, 0))],
        out_specs=pl.BlockSpec((tm, d), lambda i: (i, 0)),
        out_shape=jax.ShapeDtypeStruct((s, d), F32),
        compiler_params=_cparams("arbitrary"),
        name="ffn",
    )(x, mod, w_in, w_out, ln_g, ln_b)


_PROJ_OUTS = (("gates", 3 * 1024, BF16), ("a", POOL_WIDTH, F32), ("h4", 4 * HGRN_WIDTH, F32),
              ("qi", IDX_HEADS * IDX_DIM, BF16), ("q", ATTN_WIDTH, BF16), ("k", ATTN_WIDTH, BF16),
              ("v", ATTN_WIDTH, BF16), ("ki2", 2 * IDX_DIM, BF16), ("wi", LANES, F32))
_PROJ_CHUNK = 1024


def _proj_kernel(x_ref, mod_ref, w_ref, *o_refs):
    shift, scale, _ = _mod_rows(mod_ref, 1)
    h = (x_ref[...] * (1.0 + scale) + shift).astype(BF16)
    c0 = 0
    for (_, width, dt), o_ref in zip(_PROJ_OUTS, o_refs):
        for cc in range(0, width, _PROJ_CHUNK):
            cw = min(_PROJ_CHUNK, width - cc)
            o_ref[:, cc:cc + cw] = _dot(h, w_ref[:, c0 + cc:c0 + cc + cw]).astype(dt)
        c0 += width


def _proj_weight(w):
    d = w.shape[0]
    widths = (POOL_WIDTH, ATTN_WIDTH, ATTN_WIDTH, ATTN_WIDTH, IDX_HEADS * IDX_DIM, IDX_DIM, IDX_HEADS,
              HGRN_WIDTH, HGRN_WIDTH, HGRN_WIDTH, HGRN_WIDTH, 3 * d)
    parts, c0 = [], 0
    for wd in widths:
        parts.append(w[:, c0:c0 + wd])
        c0 += wd
    a, q, k, v, qi, ki, wi, hq, hf, hi, hg, gates = parts
    wi_pad = jnp.pad(wi, ((0, 0), (0, LANES - IDX_HEADS)))
    q = q * (ATTN_HEAD_DIM ** -0.5 * _LOG2E)
    return jnp.concatenate([gates, a, hq, hf, hi, hg, qi, q, k, v, ki, ki, wi_pad], axis=1).astype(BF16)


def _proj_call(x, mod, w, *, tm):
    s, d = x.shape
    n = w.shape[1]
    assert n == sum(wd for _, wd, _ in _PROJ_OUTS)
    return pl.pallas_call(
        _proj_kernel,
        grid=(s // tm,),
        in_specs=[pl.BlockSpec((tm, d), lambda i: (i, 0)),
                  pl.BlockSpec((3 * N_SUB, d), lambda i: (0, 0)),
                  pl.BlockSpec((d, n), lambda i: (0, 0))],
        out_specs=[pl.BlockSpec((tm, wd), lambda i: (i, 0)) for _, wd, _ in _PROJ_OUTS],
        out_shape=[jax.ShapeDtypeStruct((s, wd), dt) for _, wd, dt in _PROJ_OUTS],
        compiler_params=_cparams("arbitrary"),
        name="mix_proj",
    )(x, mod, w)


_BISECT_MAX_ITERS = 320
_F32_TINY = float(jnp.finfo(jnp.float32).tiny)
_COARSE_PASSES = 9
_COARSE_MIN_ABS, _COARSE_MAX_ABS = 1e-30, 1e30
_KIND_FAR, _KIND_PAD, _N_KINDS = 2, 3, 4
_VT_ROWS = LANES + 16
_SMASK_FULL, _SMASK_DIAG, _SMASK_PAD = 0, 1, 2
_LOG2E = math.log2(math.e)


def _t5_bucket(dist):
    max_exact = REL_BUCKETS // 2
    d32 = jnp.maximum(dist, 1).astype(F32)
    large = max_exact + (jnp.log(d32 / max_exact) / math.log(REL_MAX_DIST / max_exact)
                         * (REL_BUCKETS - max_exact)).astype(I32)
    large = jnp.minimum(large, REL_BUCKETS - 1)
    return jnp.where(dist < max_exact, dist, large)


def _dsa_kernel(relb_ref, qi_ref, wi_ref, q_ref, ki2_ref, k_ref, vt_ref, o_ref,
                s_sc, sb_sc, bias_sc, smask_sc, lg_sc, qim_sc, qm_sc, m_sc, acc_sc, tie_sc, cnt_sc,
                *, tq, topk, idx_scale):
    tk = tq
    i = pl.program_id(0)
    krow = lax.broadcasted_iota(I32, (tk, tq), 0)
    qcol = lax.broadcasted_iota(I32, (tk, tq), 1)
    causal = krow <= qcol
    lane = lax.broadcasted_iota(I32, (tq, LANES), 1)
    lo_half = lane < ATTN_HEAD_DIM

    @pl.when(i == 0)
    def _():
        for which in range(2):
            bucket = _t5_bucket(jnp.maximum(qcol - krow + which * tq, 0))
            for h in range(ATTN_HEADS):
                b = jnp.zeros((tk, tq), F32)
                for bk in range(REL_BUCKETS):
                    b = jnp.where(bucket == bk, relb_ref[bk, h] * _LOG2E, b)
                bias_sc[h, which] = (jnp.where(causal, b, NEG_INF) if which == 0 else b).astype(BF16)
        for h in range(ATTN_HEADS):
            bias_sc[h, _KIND_FAR] = jnp.full((tk, tq), relb_ref[REL_BUCKETS - 1, h] * _LOG2E, F32).astype(BF16)
            bias_sc[h, _KIND_PAD] = jnp.full((tk, tq), NEG_INF, BF16)
        smask_sc[_SMASK_FULL] = jnp.zeros((tk, tq), F32)
        smask_sc[_SMASK_DIAG] = jnp.where(causal, 0.0, NEG_INF)
        smask_sc[_SMASK_PAD] = jnp.full((tk, tq), NEG_INF, F32)

    qi = qi_ref[...]
    for h in range(IDX_HEADS):
        pair = qi[:, LANES * (h // 2):LANES * (h // 2 + 1)]
        qim_sc[h] = jnp.where(lo_half if h % 2 == 0 else ~lo_half, pair, jnp.zeros_like(pair))
    qs = q_ref[...]
    for h in range(ATTN_HEADS):
        pair = qs[:, LANES * (h // 2):LANES * (h // 2 + 1)]
        qm_sc[h] = jnp.where(lo_half if h % 2 == 0 else ~lo_half, pair, jnp.zeros_like(pair))

    wt = jnp.transpose(wi_ref[...])

    def key_rows(jb):
        return pl.ds(pl.multiple_of(jb * tk, tk), tk)

    def fold(x):
        return x.reshape(tk // SUBLANES, SUBLANES, tq)

    def index_scores(jb):
        kb = ki2_ref[key_rows(jb), :]
        acc = jnp.zeros((tk, tq), F32)
        for h in range(IDX_HEADS):
            acc = acc + wt[h:h + 1, :] * jnp.maximum(_dot_nt(kb, qim_sc[h]), 0.0)
        return acc * idx_scale

    npair = (i + 2) // 2
    last_kv = k_ref.shape[0] // tk - 1

    def p1_pair(j, carry):
        rmin, rmax = carry
        for jb in (2 * j, 2 * j + 1):
            raw = index_scores(jnp.minimum(jb, last_kv))
            kind = jnp.where(jb > i, _SMASK_PAD, jnp.where(jb == i, _SMASK_DIAG, _SMASK_FULL))
            sc = raw + smask_sc[kind]
            s_sc[jb] = sc
            sb_sc[jb] = sc.astype(BF16)
            rmin = jnp.minimum(rmin, jnp.min(fold(raw), axis=0))
            rmax = jnp.maximum(rmax, jnp.max(fold(raw), axis=0))
        return rmin, rmax

    rmin, rmax = lax.fori_loop(0, npair, p1_pair, (jnp.full((SUBLANES, tq), POS_INF, F32),
                                                   jnp.full((SUBLANES, tq), NEG_INF, F32)))
    rowmin = jnp.min(rmin, axis=0, keepdims=True)
    rowmax = jnp.max(rmax, axis=0, keepdims=True)

    kf = float(topk)
    nfin = (i * tq + lax.broadcasted_iota(I32, (1, tq), 1) + 1).astype(F32)
    small = nfin <= kf

    def walk_blocks(blocks_fn, init):
        nquad = npair // 2
        carry = lax.fori_loop(0, nquad, lambda q, c: blocks_fn([4 * q + r for r in range(4)], c), init)
        return lax.fori_loop(2 * nquad, npair, lambda j, c: blocks_fn([2 * j, 2 * j + 1], c), carry)

    def count_ge(x):
        def blocks_fn(jbs, cnt):
            for jb in jbs:
                cnt = cnt + jnp.sum(fold(jnp.where(s_sc[jb] >= x, 1.0, 0.0)), axis=0)
            return cnt
        return jnp.sum(walk_blocks(blocks_fn, jnp.zeros((SUBLANES, tq), F32)), axis=0, keepdims=True)

    def count_ge16(xb):
        one, zero = jnp.ones((), BF16), jnp.zeros((), BF16)
        rows16 = 2 * SUBLANES

        def blocks_fn(jbs, cnt):
            parts = []
            for jb in jbs:
                c = jnp.where(sb_sc[jb] >= xb, one, zero)
                parts += [c[r0:r0 + rows16, :] for r0 in range(0, tk, rows16)]
            while len(parts) > 1:
                parts = [a + b for a, b in zip(parts[0::2], parts[1::2])]
            return cnt + parts[0].astype(F32)
        return jnp.sum(walk_blocks(blocks_fn, jnp.zeros((rows16, tq), F32)), axis=0, keepdims=True)

    def coarse_probe(mid):
        xb = mid.astype(BF16)
        xbits = lax.bitcast_convert_type(xb.astype(F32), I32)
        neg = xbits < 0
        mu = jnp.where(neg, xbits + 0x8000, xbits - 0x8000)
        odd = (xbits & 0x10000) != 0
        tbits = jnp.where(odd, jnp.where(neg, mu - 1, mu + 1), mu)
        t = lax.bitcast_convert_type(tbits, F32)
        ok = jnp.logical_and(jnp.abs(xb.astype(F32)) > _COARSE_MIN_ABS, jnp.abs(xb.astype(F32)) < _COARSE_MAX_ABS)
        return xb, t, ok

    def narrow(st, probe, c, active):
        lo, hi, clo, chi, done = st
        up = jnp.logical_and(active, c >= kf)
        dn = jnp.logical_and(active, c < kf)
        return (jnp.where(up, probe, lo), jnp.where(dn, probe, hi), jnp.where(up, c, clo),
                jnp.where(dn, c, chi), jnp.where(jnp.logical_and(active, c == kf), 1.0, done))

    def coarse_body(_, st):
        lo, hi, _, _, done = st
        xb, t, ok = coarse_probe(0.5 * lo + 0.5 * hi)
        ok = jnp.logical_and(ok, jnp.logical_and(t > lo, t < hi))
        return narrow(st, t, count_ge16(xb), jnp.logical_and(done < 0.5, ok))

    def fine_cond(st):
        it, (_, _, _, _, done) = st
        return jnp.logical_and(it < _BISECT_MAX_ITERS, jnp.min(done) < 0.5)

    def fine_body(st):
        it, inner = st
        lo, hi, _, _, done = inner
        mid = 0.5 * lo + 0.5 * hi
        mid = jnp.where(jnp.logical_and(lo == 0.0, hi > _F32_TINY), _F32_TINY, mid)
        mid = jnp.where(jnp.logical_and(lo < 0.0, hi > 0.0), 0.0, mid)
        stuck = jnp.logical_or(jnp.logical_or(mid <= lo, mid >= hi),
                               jnp.logical_and(lo == 0.0, hi <= _F32_TINY))
        lo, hi, clo, chi, done = narrow(inner, mid, count_ge(mid),
                                        jnp.logical_and(done < 0.5, jnp.logical_not(stuck)))
        return it + 1, (lo, hi, clo, chi, jnp.where(stuck, 1.0, done))

    mbits = lax.bitcast_convert_type(rowmax, I32)
    above_max = jnp.where(rowmax == 0.0, _F32_TINY,
                          lax.bitcast_convert_type(jnp.where(rowmax > 0.0, mbits + 1, mbits - 1), F32))
    st0 = (jnp.where(small, NEG_INF, rowmin), above_max, nfin, jnp.zeros((1, tq), F32),
           jnp.where(small, 1.0, 0.0))
    st1 = lax.fori_loop(0, _COARSE_PASSES, coarse_body, st0)
    _, (lo, hi, clo, chi, _) = lax.while_loop(fine_cond, fine_body, (jnp.int32(0), st1))
    budget = kf - chi
    need_ties = jnp.max((clo - chi) - budget) > 0.5

    pairs = [slice(LANES * (h // 2), LANES * (h // 2 + 1)) for h in range(ATTN_HEADS)]

    def issue_qk(jb, slot):
        kb = k_ref[key_rows(jnp.minimum(jb, last_kv)), :]
        for h in range(ATTN_HEADS):
            lg_sc[slot, h] = _dot_nt(kb[:, pairs[h]], qm_sc[h]).astype(BF16)

    def attend(jb, slot, thr, cum=None):
        s = s_sc[jb]
        if cum is None:
            sel = s >= thr
        else:
            cand = jnp.logical_and(s >= lo, s < hi)
            before = (qcol < krow).astype(BF16)
            rank = cum + _dot(before, jnp.where(cand, 1.0, 0.0).astype(BF16))
            sel = jnp.logical_or(s >= hi, jnp.logical_and(cand, rank < budget))
        kind = jnp.where(jb > i, _KIND_PAD, jnp.minimum(i - jb, _KIND_FAR))
        selm = jnp.where(sel, 0.0, NEG_INF).astype(BF16)
        probs, alphas = [], []
        for h in range(ATTN_HEADS):
            lg = lg_sc[slot, h] + bias_sc[h, kind] + selm
            m_old = m_sc[h]
            m_new = jnp.maximum(m_old, jnp.max(lg, axis=0, keepdims=True).astype(F32))
            m_safe = jnp.where(m_new == NEG_INF, 0.0, m_new)
            alphas.append(jnp.exp2(m_old - m_safe))
            probs.append(jnp.exp2(lg - m_safe.astype(BF16)))
            m_sc[h] = m_new
        vb = vt_ref[jnp.minimum(jb, last_kv)]
        for h in range(ATTN_HEADS):
            vrows = slice(_VT_ROWS * (h // 2), _VT_ROWS * (h // 2 + 1))
            acc_sc[h] = alphas[h] * acc_sc[h] + _dot(vb[vrows, :], probs[h])

    def run(with_ties):
        m_sc[...] = jnp.full(m_sc.shape, NEG_INF, F32)
        acc_sc[...] = jnp.zeros_like(acc_sc)
        tie_sc[...] = jnp.zeros_like(tie_sc)
        if with_ties:
            def cand_body(j, carry):
                for jb in (2 * j, 2 * j + 1):
                    s = s_sc[jb]
                    cand = jnp.logical_and(s >= lo, s < hi)
                    cnt_sc[jb] = jnp.sum(jnp.where(cand, 1.0, 0.0), axis=0, keepdims=True)
                return carry
            lax.fori_loop(0, npair, cand_body, 0)
        issue_qk(0, 0)

        def pair(j, thr0, thr1, cum0=None, cum1=None):
            issue_qk(2 * j + 1, 1)
            attend(2 * j, 0, thr0, cum0)
            issue_qk(2 * j + 2, 0)
            attend(2 * j + 1, 1, thr1, cum1)

        def pair_body(j, carry):
            if not with_ties:
                pair(j, lo, lo)
                return carry
            cum0 = tie_sc[...]
            cum1 = cum0 + cnt_sc[2 * j]
            cum2 = cum1 + cnt_sc[2 * j + 1]
            tie_sc[...] = cum2
            all0, all1 = cum1 <= budget, cum2 <= budget
            runs_out = jnp.logical_or(jnp.logical_and(jnp.logical_not(all0), cum0 < budget),
                                      jnp.logical_and(jnp.logical_not(all1), cum1 < budget))
            ranked = jnp.max(jnp.where(runs_out, 1.0, 0.0)) > 0.5

            @pl.when(ranked)
            def _():
                pair(j, None, None, cum0, cum1)

            @pl.when(jnp.logical_not(ranked))
            def _():
                pair(j, jnp.where(all0, lo, hi), jnp.where(all1, lo, hi))
            return carry
        lax.fori_loop(0, npair, pair_body, 0)

    @pl.when(need_ties)
    def _():
        run(True)

    @pl.when(jnp.logical_not(need_ties))
    def _():
        run(False)

    first_head = lax.broadcasted_iota(I32, (LANES, tq), 0) < ATTN_HEAD_DIM
    normed = [acc_sc[h, 0:LANES, :] / acc_sc[h, LANES:LANES + 1, :] for h in range(ATTN_HEADS)]
    for p in range(ATTN_HEADS // 2):
        pair_t = jnp.where(first_head, normed[2 * p], normed[2 * p + 1])
        o_ref[:, LANES * p:LANES * (p + 1)] = jnp.transpose(pair_t)


def _dsa_call(rel_bias, qi, wi, q, ki2, k, v, *, tq):
    s = q.shape[0]
    assert tq > REL_MAX_DIST and s % tq == 0
    nb = s // tq
    topk = min(TOPK_MAX, s // 4)
    kern = functools.partial(_dsa_kernel, tq=tq, topk=topk,
                             idx_scale=(IDX_DIM ** -0.5) * (IDX_HEADS ** -0.5))
    whole = lambda width: pl.BlockSpec((s, width), lambda i: (0, 0), pipeline_mode=pl.Buffered(1))
    n_pairs = ATTN_WIDTH // LANES
    vt = jnp.transpose(v.reshape(nb, tq, n_pairs, LANES), (0, 2, 3, 1))
    vt = jnp.concatenate([vt, jnp.ones((nb, n_pairs, _VT_ROWS - LANES, tq), BF16)], axis=2)
    vt = vt.reshape(nb, n_pairs * _VT_ROWS, tq)
    return pl.pallas_call(
        kern,
        grid=(nb,),
        in_specs=[pl.BlockSpec(memory_space=pltpu.SMEM),
                  pl.BlockSpec((tq, IDX_HEADS * IDX_DIM), lambda i: (i, 0)),
                  pl.BlockSpec((tq, LANES), lambda i: (i, 0)),
                  pl.BlockSpec((tq, ATTN_WIDTH), lambda i: (i, 0)),
                  whole(2 * IDX_DIM), whole(ATTN_WIDTH),
                  pl.BlockSpec(vt.shape, lambda i: (0, 0, 0), pipeline_mode=pl.Buffered(1))],
        out_specs=pl.BlockSpec((tq, ATTN_WIDTH), lambda i: (i, 0)),
        out_shape=jax.ShapeDtypeStruct((s, ATTN_WIDTH), F32),
        scratch_shapes=[pltpu.VMEM((nb + nb % 2, tq, tq), F32),
                        pltpu.VMEM((nb + nb % 2, tq, tq), BF16),
                        pltpu.VMEM((ATTN_HEADS, _N_KINDS, tq, tq), BF16),
                        pltpu.VMEM((3, tq, tq), F32),
                        pltpu.VMEM((2, ATTN_HEADS, tq, tq), BF16),
                        pltpu.VMEM((IDX_HEADS, tq, LANES), BF16),
                        pltpu.VMEM((ATTN_HEADS, tq, LANES), BF16),
                        pltpu.VMEM((ATTN_HEADS, 1, tq), F32),
                        pltpu.VMEM((ATTN_HEADS, _VT_ROWS, tq), F32),
                        pltpu.VMEM((1, tq), F32),
                        pltpu.VMEM((nb + nb % 2, 1, tq), F32)],
        compiler_params=_cparams("arbitrary"),
        name="dsa_attention",
    )(rel_bias, qi, wi, q, ki2, k, vt)


_HGRN_CHUNK = 128
_HGRN_BASE = SUBLANES


def _hgrn_kernel(h4_ref, lb_ref, ng_ref, o_ref, st_sc, *, layer, rows):
    c_len = _HGRN_CHUNK

    @pl.when(pl.program_id(0) == 0)
    def _():
        st_sc[...] = jnp.zeros_like(st_sc)

    lb_all = lb_ref[...]
    e = jnp.exp(lb_all - jnp.max(lb_all, axis=0, keepdims=True))
    sm = e / jnp.sum(e, axis=0, keepdims=True)
    lbs = jnp.sum(sm[0:layer + 1, :], axis=0, keepdims=True) - sm[0:1, :]

    r1 = lax.broadcasted_iota(I32, (c_len, 1), 0)
    rr = lax.broadcasted_iota(I32, (c_len, c_len), 0)
    cc = lax.broadcasted_iota(I32, (c_len, c_len), 1)

    def block_ref(a, blk, r):
        x = a.reshape(c_len // blk, blk, a.shape[-1])[:, r:r + 1, :]
        return jnp.broadcast_to(x, (c_len // blk, blk, a.shape[-1])).reshape(c_len, a.shape[-1])

    w = HGRN_WIDTH
    for c in range(rows // c_len):
        rs = slice(c * c_len, (c + 1) * c_len)
        hq = h4_ref[rs, 0:w]
        z = h4_ref[rs, w:2 * w]
        val = h4_ref[rs, 2 * w:3 * w]
        hg = h4_ref[rs, 3 * w:4 * w]
        key = (1.0 - lbs) * jax.nn.sigmoid(-z)
        g = jnp.log1p(-jnp.minimum(key, KEY_MAX))
        a_cum = g
        sh = 1
        while sh < c_len:
            a_cum = a_cum + jnp.where(r1 >= sh, pltpu.roll(a_cum, sh, 0), 0.0)
            sh *= 2
        qf = _silu(hq)
        outs = []
        for h in range(HGRN_HEADS):
            hs = slice(h * HGRN_DIM, (h + 1) * HGRN_DIM)
            ah, qh, kh, vh = a_cum[:, hs], qf[:, hs], key[:, hs], val[:, hs]
            vh16 = vh.astype(BF16)
            st = st_sc[h]
            o = _dot_nt((qh * jnp.exp(ah)).astype(BF16), st.astype(BF16))
            ref = block_ref(ah, _HGRN_BASE, _HGRN_BASE // 2 - 1)
            sc = _dot_nt((qh * jnp.exp(ah - ref)).astype(BF16), (kh * jnp.exp(ref - ah)).astype(BF16))
            same = (rr >> _log2(_HGRN_BASE)) == (cc >> _log2(_HGRN_BASE))
            scores = jnp.where(jnp.logical_and(same, cc <= rr), sc, 0.0)
            half = _HGRN_BASE
            while half < c_len:
                blk = 2 * half
                ref = block_ref(ah, blk, half - 1)
                second = (r1 & (blk - 1)) >= half
                ql = jnp.where(second, qh * jnp.exp(jnp.minimum(ah - ref, 0.0)), 0.0)
                kl = jnp.where(second, 0.0, kh * jnp.exp(jnp.minimum(ref - ah, 0.0)))
                sc = _dot_nt(ql.astype(BF16), kl.astype(BF16))
                scores = scores + jnp.where((rr >> _log2(blk)) == (cc >> _log2(blk)), sc, 0.0)
                half = blk
            o = o + _dot(scores.astype(BF16), vh16)
            a_last = ah[c_len - 1:c_len, :]
            kd = (kh * jnp.exp(a_last - ah)).astype(BF16)
            st_sc[h] = st * jnp.exp(a_last) + _dot_tn(vh16, kd)
            o = o * lax.rsqrt(jnp.mean(o * o, axis=-1, keepdims=True) + RMS_EPS)
            outs.append(o)
        o_ref[rs, :] = jnp.concatenate(outs, axis=1) * ng_ref[...] * _silu(hg)


def _hgrn_call(h4, lb, ng, *, layer, rows):
    s = h4.shape[0]
    kern = functools.partial(_hgrn_kernel, layer=layer, rows=rows)
    return pl.pallas_call(
        kern,
        grid=(s // rows,),
        in_specs=[pl.BlockSpec((rows, 4 * HGRN_WIDTH), lambda i: (i, 0)),
                  pl.BlockSpec(lb.shape, lambda i: (0, 0)),
                  pl.BlockSpec((1, HGRN_WIDTH), lambda i: (0, 0))],
        out_specs=pl.BlockSpec((rows, HGRN_WIDTH), lambda i: (i, 0)),
        out_shape=jax.ShapeDtypeStruct((s, HGRN_WIDTH), F32),
        scratch_shapes=[pltpu.VMEM((HGRN_HEADS, HGRN_DIM, HGRN_DIM), F32)],
        compiler_params=_cparams("arbitrary"),
        name="hgrn2",
    )(h4, lb, ng)


_POOL_HALO = 16


def _merge_kernel(x_ref, mod_ref, a_ref, ah_ref, yb_ref, yc_ref, g_ref, pw_ref, ps_ref, wb_ref, wo_ref,
                  lng_ref, lnb_ref, o_ref, *, tm, alpha):
    i = pl.program_id(0)
    d = x_ref.shape[1]
    _, _, gate = _mod_rows(mod_ref, 1)
    a = a_ref[...]
    halo = jnp.where(i == 0, 0.0, ah_ref[...])
    ext = jnp.concatenate([halo, a], axis=0)
    lane = lax.broadcasted_iota(I32, (tm, POOL_WIDTH), 1)
    grp = lane >> _log2(POOL_GROUP)
    pos1 = (i * tm + lax.broadcasted_iota(I32, (tm, POOL_WIDTH), 0) + 1).astype(F32)
    win = jnp.zeros((tm, POOL_WIDTH), F32)
    cnt = jnp.zeros((tm, POOL_WIDTH), F32)
    ssum, have = ext, 1
    for gi, wd in enumerate(POOL_WINDOWS):
        while have < wd:
            ssum = ssum + pltpu.roll(ssum, have, 0)
            have *= 2
        assert have == wd
        win = jnp.where(grp == gi, ssum[_POOL_HALO:, :], win)
        cnt = jnp.where(grp == gi, jnp.minimum(pos1, float(wd)), cnt)
    dlt = (win / cnt - a).astype(BF16)
    ya = _dot(dlt, pw_ref[...]) * ps_ref[...]
    za = _dot(ya.astype(BF16), wb_ref[0:POOL_WIDTH, :])
    zb = _dot(yb_ref[...].astype(BF16), wb_ref[POOL_WIDTH:POOL_WIDTH + ATTN_WIDTH, :])
    zc = _dot(yc_ref[...].astype(BF16), wb_ref[POOL_WIDTH + ATTN_WIDTH:, :])
    g0, g1, g2 = (jax.nn.sigmoid(g_ref[:, j * d:(j + 1) * d].astype(F32)) for j in range(3))
    merged = g0 * za + g1 * zb + g2 * zc
    y = _dot(merged.astype(BF16), wo_ref[...])
    z = alpha * x_ref[...] + (1.0 + gate) * y
    o_ref[...] = _layer_norm(z, lng_ref[1:2, :], lnb_ref[1:2, :])


def _merge_call(x, mod, a, yb, yc, gates, pw_bd, ps, wb, wo, ln_g, ln_b, *, tm, alpha):
    s, d = x.shape
    kern = functools.partial(_merge_kernel, tm=tm, alpha=alpha)
    full = lambda arr: pl.BlockSpec(arr.shape, lambda i: (0, 0))
    hb = tm // _POOL_HALO
    return pl.pallas_call(
        kern,
        grid=(s // tm,),
        in_specs=[pl.BlockSpec((tm, d), lambda i: (i, 0)),
                  full(mod),
                  pl.BlockSpec((tm, POOL_WIDTH), lambda i: (i, 0)),
                  pl.BlockSpec((_POOL_HALO, POOL_WIDTH), lambda i: (jnp.maximum(i * hb - 1, 0), 0)),
                  pl.BlockSpec((tm, ATTN_WIDTH), lambda i: (i, 0)),
                  pl.BlockSpec((tm, HGRN_WIDTH), lambda i: (i, 0)),
                  pl.BlockSpec((tm, 3 * d), lambda i: (i, 0)),
                  full(pw_bd), full(ps), full(wb), full(wo), full(ln_g), full(ln_b)],
        out_specs=pl.BlockSpec((tm, d), lambda i: (i, 0)),
        out_shape=jax.ShapeDtypeStruct((s, d), F32),
        compiler_params=_cparams("arbitrary"),
        name="pool_merge",
    )(x, mod, a, a, yb, yc, gates, pw_bd, ps, wb, wo, ln_g, ln_b)


def _pool_blockdiag(pool_w):
    n = pool_w.shape[0]
    rows = [jnp.concatenate([pool_w[g] if g == j else jnp.zeros_like(pool_w[g]) for j in range(n)], axis=1)
            for g in range(n)]
    return jnp.concatenate(rows, axis=0)


def kernel(x, c, w_ada, b_ada, ln_g, ln_b, ffn_w_in, ffn_w_out, mix_w_in, pool_w, pool_scale, rel_bias,
           hgrn_lb, hgrn_norm_g, w_branch, w_out):
    batch, s, d = x.shape
    assert batch == 1, "kernels are written for a single sequence"
    depth = w_ada.shape[0]
    alpha = (2 * depth) ** 0.25
    tm_ffn = min(512, s)
    tf = 256
    tm_proj = min(256, s)
    tq = min(256, s)
    tm_merge = min(256, s)
    hgrn_rows = min(256, s)

    mods = _ada_call(c, w_ada, b_ada)
    xs = x[0]
    for l in range(depth):
        mod = mods[l].reshape(3 * N_SUB, d)
        xs = _ffn_call(xs, mod, ffn_w_in[l, 0].astype(BF16), ffn_w_out[l, 0].astype(BF16), ln_g[l], ln_b[l],
                       sub=0, alpha=alpha, tm=tm_ffn, tf=tf)
        gates, a, h4, qi, q, k, v, ki2, wi = _proj_call(xs, mod, _proj_weight(mix_w_in[l]), tm=tm_proj)
        yb = _dsa_call(rel_bias, qi, wi, q, ki2, k, v, tq=tq)
        yc = _hgrn_call(h4, hgrn_lb, hgrn_norm_g[l].reshape(1, -1), layer=l, rows=hgrn_rows)
        xs = _merge_call(xs, mod, a, yb, yc, gates, _pool_blockdiag(pool_w[l]).astype(BF16),
                         pool_scale[l].reshape(1, -1), w_branch[l].astype(BF16), w_out[l].astype(BF16),
                         ln_g[l], ln_b[l], tm=tm_merge, alpha=alpha)
        xs = _ffn_call(xs, mod, ffn_w_in[l, 1].astype(BF16), ffn_w_out[l, 1].astype(BF16), ln_g[l], ln_b[l],
                       sub=2, alpha=alpha, tm=tm_ffn, tf=tf)
    return xs[None]
```

```python
import functools
import math

import jax
import jax.numpy as jnp
from jax import lax
from jax.experimental import pallas as pl
from jax.experimental.pallas import tpu as pltpu

F32 = jnp.float32
BF16 = jnp.bfloat16
I32 = jnp.int32
HIGHEST = lax.Precision.HIGHEST

POOL_WINDOWS = (2, 4, 8, 16)
POOL_GROUP = 64
POOL_WIDTH = POOL_GROUP * len(POOL_WINDOWS)
ATTN_HEADS = 4
ATTN_HEAD_DIM = 64
ATTN_WIDTH = ATTN_HEADS * ATTN_HEAD_DIM
IDX_HEADS = 8
IDX_DIM = 64
TOPK_MAX = 256
HGRN_HEADS = 4
HGRN_DIM = 128
HGRN_WIDTH = HGRN_HEADS * HGRN_DIM
KEY_MAX = 1.0 - 1e-6
REL_BUCKETS = 32
REL_MAX_DIST = 128
N_SUB = 3
LN_EPS = 1e-5
RMS_EPS = 1e-6

LANES = 128
SUBLANES = 8
VMEM_LIMIT_BYTES = 58 * 1024 * 1024

NEG_INF = float("-inf")
POS_INF = float("inf")


def _cparams(*sem):
    return pltpu.CompilerParams(dimension_semantics=sem, vmem_limit_bytes=VMEM_LIMIT_BYTES)


def _dot(a, b):
    return jnp.dot(a, b, preferred_element_type=F32)


def _dot_nt(a, b):
    return lax.dot_general(a, b, (((1,), (1,)), ((), ())), preferred_element_type=F32)


def _dot_tn(a, b):
    return lax.dot_general(a, b, (((0,), (0,)), ((), ())), preferred_element_type=F32)


def _log2(n):
    assert n & (n - 1) == 0
    return n.bit_length() - 1


def _silu(x):
    return x * jax.nn.sigmoid(x)


def _mod_rows(mod_ref, j):
    return mod_ref[3 * j:3 * j + 1, :], mod_ref[3 * j + 1:3 * j + 2, :], mod_ref[3 * j + 2:3 * j + 3, :]


def _layer_norm(z, g, b):
    mu = jnp.mean(z, axis=-1, keepdims=True)
    zc = z - mu
    var = jnp.mean(zc * zc, axis=-1, keepdims=True)
    return zc * lax.rsqrt(var + LN_EPS) * g + b


def _ada_kernel(c_ref, w_ref, b_ref, o_ref):
    @pl.when(pl.program_id(1) == 0)
    def _():
        o_ref[0] = b_ref[0]

    c = c_ref[...]
    o_ref[0] += jnp.sum(_silu(c) * w_ref[0], axis=0, keepdims=True)


def _ada_call(c, w_ada, b_ada):
    depth, d, n = w_ada.shape
    rows = LANES
    return pl.pallas_call(
        _ada_kernel,
        grid=(depth, d // rows),
        in_specs=[pl.BlockSpec((rows, 1), lambda l, k: (k, 0)),
                  pl.BlockSpec((1, rows, n), lambda l, k: (l, k, 0)),
                  pl.BlockSpec((1, 1, n), lambda l, k: (l, 0, 0))],
        out_specs=pl.BlockSpec((1, 1, n), lambda l, k: (l, 0, 0)),
        out_shape=jax.ShapeDtypeStruct((depth, 1, n), F32),
        compiler_params=_cparams("arbitrary", "arbitrary"),
        name="ada_mod",
    )(c.reshape(d, 1), w_ada, b_ada.reshape(depth, 1, n))


def _ffn_kernel(x_ref, mod_ref, wi_ref, wo_ref, lng_ref, lnb_ref, o_ref, *, sub, tf, alpha):
    shift, scale, gate = _mod_rows(mod_ref, sub)
    ff = wo_ref.shape[0]
    x = x_ref[...]
    h = (x * (1.0 + scale) + shift).astype(BF16)
    y = jnp.zeros(x.shape, F32)
    for c0 in range(0, ff, tf):
        g = _dot(h, wi_ref[:, c0:c0 + tf])
        u = _dot(h, wi_ref[:, ff + c0:ff + c0 + tf])
        y = y + _dot((_silu(g) * u).astype(BF16), wo_ref[c0:c0 + tf, :])
    z = alpha * x + (0.5 * (1.0 + gate)) * y
    o_ref[...] = _layer_norm(z, lng_ref[sub:sub + 1, :], lnb_ref[sub:sub + 1, :])


def _ffn_call(x, mod, w_in, w_out, ln_g, ln_b, *, sub, alpha, tm, tf):
    s, d = x.shape
    ff = w_out.shape[0]
    assert ff % tf == 0
    kern = functools.partial(_ffn_kernel, sub=sub, tf=tf, alpha=alpha)
    resident = lambda arr: pl.BlockSpec(arr.shape, lambda i: (0, 0), pipeline_mode=pl.Buffered(1))
    return pl.pallas_call(
        kern,
        grid=(s // tm,),
        in_specs=[pl.BlockSpec((tm, d), lambda i: (i, 0)),
                  pl.BlockSpec((3 * N_SUB, d), lambda i: (0, 0)),
                  resident(w_in), resident(w_out),
                  pl.BlockSpec((N_SUB, d), lambda i: (0, 0)),
                  pl.BlockSpec((N_SUB, d), lambda i: (0, 0))],
        out_specs=pl.BlockSpec((tm, d), lambda i: (i, 0)),
        out_shape=jax.ShapeDtypeStruct((s, d), F32),
        compiler_params=_cparams("arbitrary"),
        name="ffn",
    )(x, mod, w_in, w_out, ln_g, ln_b)


_PROJ_OUTS = (("gates", 3 * 1024, BF16), ("a", POOL_WIDTH, F32), ("h4", 4 * HGRN_WIDTH, F32),
              ("qi", IDX_HEADS * IDX_DIM, BF16), ("q", ATTN_WIDTH, BF16), ("k", ATTN_WIDTH, BF16),
              ("v", ATTN_WIDTH, BF16), ("ki2", 2 * IDX_DIM, BF16), ("wi", LANES, F32))
_PROJ_CHUNK = 1024


def _proj_kernel(x_ref, mod_ref, w_ref, *o_refs):
    shift, scale, _ = _mod_rows(mod_ref, 1)
    h = (x_ref[...] * (1.0 + scale) + shift).astype(BF16)
    c0 = 0
    for (_, width, dt), o_ref in zip(_PROJ_OUTS, o_refs):
        for cc in range(0, width, _PROJ_CHUNK):
            cw = min(_PROJ_CHUNK, width - cc)
            o_ref[:, cc:cc + cw] = _dot(h, w_ref[:, c0 + cc:c0 + cc + cw]).astype(dt)
        c0 += width


def _proj_weight(w):
    d = w.shape[0]
    widths = (POOL_WIDTH, ATTN_WIDTH, ATTN_WIDTH, ATTN_WIDTH, IDX_HEADS * IDX_DIM, IDX_DIM, IDX_HEADS,
              HGRN_WIDTH, HGRN_WIDTH, HGRN_WIDTH, HGRN_WIDTH, 3 * d)
    parts, c0 = [], 0
    for wd in widths:
        parts.append(w[:, c0:c0 + wd])
        c0 += wd
    a, q, k, v, qi, ki, wi, hq, hf, hi, hg, gates = parts
    wi_pad = jnp.pad(wi, ((0, 0), (0, LANES - IDX_HEADS)))
    q = q * (ATTN_HEAD_DIM ** -0.5 * _LOG2E)
    return jnp.concatenate([gates, a, hq, hf, hi, hg, qi, q, k, v, ki, ki, wi_pad], axis=1).astype(BF16)


def _proj_call(x, mod, w, *, tm):
    s, d = x.shape
    n = w.shape[1]
    assert n == sum(wd for _, wd, _ in _PROJ_OUTS)
    return pl.pallas_call(
        _proj_kernel,
        grid=(s // tm,),
        in_specs=[pl.BlockSpec((tm, d), lambda i: (i, 0)),
                  pl.BlockSpec((3 * N_SUB, d), lambda i: (0, 0)),
                  pl.BlockSpec((d, n), lambda i: (0, 0))],
        out_specs=[pl.BlockSpec((tm, wd), lambda i: (i, 0)) for _, wd, _ in _PROJ_OUTS],
        out_shape=[jax.ShapeDtypeStruct((s, wd), dt) for _, wd, dt in _PROJ_OUTS],
        compiler_params=_cparams("arbitrary"),
        name="mix_proj",
    )(x, mod, w)


_BISECT_MAX_ITERS = 320
_F32_TINY = float(jnp.finfo(jnp.float32).tiny)
_COARSE_PASSES = 9
_COARSE_MIN_ABS, _COARSE_MAX_ABS = 1e-30, 1e30
_KIND_FAR, _KIND_PAD, _N_KINDS = 2, 3, 4
_VT_ROWS = LANES + 16
_SMASK_FULL, _SMASK_DIAG, _SMASK_PAD = 0, 1, 2
_LOG2E = math.log2(math.e)


def _t5_bucket(dist):
    max_exact = REL_BUCKETS // 2
    d32 = jnp.maximum(dist, 1).astype(F32)
    large = max_exact + (jnp.log(d32 / max_exact) / math.log(REL_MAX_DIST / max_exact)
                         * (REL_BUCKETS - max_exact)).astype(I32)
    large = jnp.minimum(large, REL_BUCKETS - 1)
    return jnp.where(dist < max_exact, dist, large)


def _dsa_kernel(relb_ref, qi_ref, wi_ref, q_ref, ki2_ref, k_ref, vt_ref, o_ref,
                s_sc, sb_sc, bias_sc, smask_sc, lg_sc, qim_sc, qm_sc, m_sc, acc_sc, tie_sc, cnt_sc,
                *, tq, topk, idx_scale):
    tk = tq
    i = pl.program_id(0)
    krow = lax.broadcasted_iota(I32, (tk, tq), 0)
    qcol = lax.broadcasted_iota(I32, (tk, tq), 1)
    causal = krow <= qcol
    lane = lax.broadcasted_iota(I32, (tq, LANES), 1)
    lo_half = lane < ATTN_HEAD_DIM

    @pl.when(i == 0)
    def _():
        for which in range(2):
            bucket = _t5_bucket(jnp.maximum(qcol - krow + which * tq, 0))
            for h in range(ATTN_HEADS):
                b = jnp.zeros((tk, tq), F32)
                for bk in range(REL_BUCKETS):
                    b = jnp.where(bucket == bk, relb_ref[bk, h] * _LOG2E, b)
                bias_sc[h, which] = (jnp.where(causal, b, NEG_INF) if which == 0 else b).astype(BF16)
        for h in range(ATTN_HEADS):
            bias_sc[h, _KIND_FAR] = jnp.full((tk, tq), relb_ref[REL_BUCKETS - 1, h] * _LOG2E, F32).astype(BF16)
            bias_sc[h, _KIND_PAD] = jnp.full((tk, tq), NEG_INF, BF16)
        smask_sc[_SMASK_FULL] = jnp.zeros((tk, tq), F32)
        smask_sc[_SMASK_DIAG] = jnp.where(causal, 0.0, NEG_INF)
        smask_sc[_SMASK_PAD] = jnp.full((tk, tq), NEG_INF, F32)

    def head_operands(src_ref, dst_sc, n_heads):
        src = src_ref[...].astype(F32)
        for h in range(n_heads):
            pair = src[:, LANES * (h // 2):LANES * (h // 2 + 1)]
            pair = jnp.where(lo_half if h % 2 == 0 else ~lo_half, pair, 0.0)
            dst_sc[h] = jnp.transpose(pair).astype(BF16)

    head_operands(qi_ref, qim_sc, IDX_HEADS)
    head_operands(q_ref, qm_sc, ATTN_HEADS)

    wt = jnp.transpose(wi_ref[...])

    def key_rows(jb):
        return pl.ds(pl.multiple_of(jb * tk, tk), tk)

    def fold(x):
        return x.reshape(tk // SUBLANES, SUBLANES, tq)

    def index_scores(jb):
        kb = ki2_ref[key_rows(jb), :]
        acc = jnp.zeros((tk, tq), F32)
        for h in range(IDX_HEADS):
            acc = acc + wt[h:h + 1, :] * jnp.maximum(_dot(kb, qim_sc[h]), 0.0)
        return acc * idx_scale

    npair = (i + 2) // 2
    last_kv = k_ref.shape[0] // tk - 1

    def p1_pair(j, carry):
        rmin, rmax = carry
        for jb in (2 * j, 2 * j + 1):
            raw = index_scores(jnp.minimum(jb, last_kv))
            kind = jnp.where(jb > i, _SMASK_PAD, jnp.where(jb == i, _SMASK_DIAG, _SMASK_FULL))
            sc = raw + smask_sc[kind]
            s_sc[jb] = sc
            sb_sc[jb] = sc.astype(BF16)
            rmin = jnp.minimum(rmin, jnp.min(fold(raw), axis=0))
            rmax = jnp.maximum(rmax, jnp.max(fold(raw), axis=0))
        return rmin, rmax

    rmin, rmax = lax.fori_loop(0, npair, p1_pair, (jnp.full((SUBLANES, tq), POS_INF, F32),
                                                   jnp.full((SUBLANES, tq), NEG_INF, F32)))
    rowmin = jnp.min(rmin, axis=0, keepdims=True)
    rowmax = jnp.max(rmax, axis=0, keepdims=True)

    kf = float(topk)
    nfin = (i * tq + lax.broadcasted_iota(I32, (1, tq), 1) + 1).astype(F32)
    small = nfin <= kf

    def walk_blocks(blocks_fn, init):
        return lax.fori_loop(0, npair, lambda j, c: blocks_fn([2 * j, 2 * j + 1], c), init)

    def count_ge(x):
        def blocks_fn(jbs, cnt):
            for jb in jbs:
                cnt = cnt + jnp.sum(fold(jnp.where(s_sc[jb] >= x, 1.0, 0.0)), axis=0)
            return cnt
        return jnp.sum(walk_blocks(blocks_fn, jnp.zeros((SUBLANES, tq), F32)), axis=0, keepdims=True)

    def count_ge16(xb):
        one, zero = jnp.ones((), BF16), jnp.zeros((), BF16)
        rows16 = 2 * SUBLANES

        def blocks_fn(jbs, cnt):
            parts = []
            for jb in jbs:
                c = jnp.where(sb_sc[jb] >= xb, one, zero)
                parts += [c[r0:r0 + rows16, :] for r0 in range(0, tk, rows16)]
            while len(parts) > 1:
                parts = [a + b for a, b in zip(parts[0::2], parts[1::2])]
            return cnt + parts[0].astype(F32)
        return jnp.sum(walk_blocks(blocks_fn, jnp.zeros((rows16, tq), F32)), axis=0, keepdims=True)

    def coarse_probe(mid):
        xb = mid.astype(BF16)
        xbits = lax.bitcast_convert_type(xb.astype(F32), I32)
        neg = xbits < 0
        mu = jnp.where(neg, xbits + 0x8000, xbits - 0x8000)
        odd = (xbits & 0x10000) != 0
        tbits = jnp.where(odd, jnp.where(neg, mu - 1, mu + 1), mu)
        t = lax.bitcast_convert_type(tbits, F32)
        ok = jnp.logical_and(jnp.abs(xb.astype(F32)) > _COARSE_MIN_ABS, jnp.abs(xb.astype(F32)) < _COARSE_MAX_ABS)
        return xb, t, ok

    def narrow(st, probe, c, active):
        lo, hi, clo, chi, done = st
        up = jnp.logical_and(active, c >= kf)
        dn = jnp.logical_and(active, c < kf)
        return (jnp.where(up, probe, lo), jnp.where(dn, probe, hi), jnp.where(up, c, clo),
                jnp.where(dn, c, chi), jnp.where(jnp.logical_and(active, c == kf), 1.0, done))

    def coarse_body(_, st):
        lo, hi, _, _, done = st
        xb, t, ok = coarse_probe(0.5 * lo + 0.5 * hi)
        ok = jnp.logical_and(ok, jnp.logical_and(t > lo, t < hi))
        return narrow(st, t, count_ge16(xb), jnp.logical_and(done < 0.5, ok))

    def fine_cond(st):
        it, (_, _, _, _, done) = st
        return jnp.logical_and(it < _BISECT_MAX_ITERS, jnp.min(done) < 0.5)

    def fine_body(st):
        it, inner = st
        lo, hi, _, _, done = inner
        mid = 0.5 * lo + 0.5 * hi
        mid = jnp.where(jnp.logical_and(lo == 0.0, hi > _F32_TINY), _F32_TINY, mid)
        mid = jnp.where(jnp.logical_and(lo < 0.0, hi > 0.0), 0.0, mid)
        stuck = jnp.logical_or(jnp.logical_or(mid <= lo, mid >= hi),
                               jnp.logical_and(lo == 0.0, hi <= _F32_TINY))
        lo, hi, clo, chi, done = narrow(inner, mid, count_ge(mid),
                                        jnp.logical_and(done < 0.5, jnp.logical_not(stuck)))
        return it + 1, (lo, hi, clo, chi, jnp.where(stuck, 1.0, done))

    mbits = lax.bitcast_convert_type(rowmax, I32)
    above_max = jnp.where(rowmax == 0.0, _F32_TINY,
                          lax.bitcast_convert_type(jnp.where(rowmax > 0.0, mbits + 1, mbits - 1), F32))
    st0 = (jnp.where(small, NEG_INF, rowmin), above_max, nfin, jnp.zeros((1, tq), F32),
           jnp.where(small, 1.0, 0.0))
    st1 = lax.fori_loop(0, _COARSE_PASSES, coarse_body, st0)
    _, (lo, hi, clo, chi, _) = lax.while_loop(fine_cond, fine_body, (jnp.int32(0), st1))
    budget = kf - chi
    need_ties = jnp.max((clo - chi) - budget) > 0.5

    pairs = [slice(LANES * (h // 2), LANES * (h // 2 + 1)) for h in range(ATTN_HEADS)]

    def issue_qk(jb, slot):
        kb = k_ref[key_rows(jnp.minimum(jb, last_kv)), :]
        for h in range(ATTN_HEADS):
            lg_sc[slot, h] = _dot(kb[:, pairs[h]], qm_sc[h]).astype(BF16)

    def attend(jb, slot, thr, cum=None):
        s = s_sc[jb]
        if cum is None:
            sel = s >= thr
        else:
            cand = jnp.logical_and(s >= lo, s < hi)
            before = (qcol < krow).astype(BF16)
            rank = cum + _dot(before, jnp.where(cand, 1.0, 0.0).astype(BF16))
            sel = jnp.logical_or(s >= hi, jnp.logical_and(cand, rank < budget))
        kind = jnp.where(jb > i, _KIND_PAD, jnp.minimum(i - jb, _KIND_FAR))
        selm = jnp.where(sel, 0.0, NEG_INF).astype(BF16)
        probs, alphas = [], []
        for h in range(ATTN_HEADS):
            lg = lg_sc[slot, h] + bias_sc[h, kind] + selm
            m_old = m_sc[h]
            m_new = jnp.maximum(m_old, jnp.max(lg, axis=0, keepdims=True).astype(F32))
            m_safe = jnp.where(m_new == NEG_INF, 0.0, m_new)
            alphas.append(jnp.exp2(m_old - m_safe))
            probs.append(jnp.exp2(lg - m_safe.astype(BF16)))
            m_sc[h] = m_new
        vb = vt_ref[jnp.minimum(jb, last_kv)]
        for h in range(ATTN_HEADS):
            vrows = slice(_VT_ROWS * (h // 2), _VT_ROWS * (h // 2 + 1))
            acc_sc[h] = alphas[h] * acc_sc[h] + _dot(vb[vrows, :], probs[h])

    def run(with_ties):
        m_sc[...] = jnp.full(m_sc.shape, NEG_INF, F32)
        acc_sc[...] = jnp.zeros_like(acc_sc)
        tie_sc[...] = jnp.zeros_like(tie_sc)
        if with_ties:
            def cand_body(j, carry):
                for jb in (2 * j, 2 * j + 1):
                    s = s_sc[jb]
                    cand = jnp.logical_and(s >= lo, s < hi)
                    cnt_sc[jb] = jnp.sum(jnp.where(cand, 1.0, 0.0), axis=0, keepdims=True)
                return carry
            lax.fori_loop(0, npair, cand_body, 0)
        issue_qk(0, 0)

        def pair(j, thr0, thr1, cum0=None, cum1=None):
            issue_qk(2 * j + 1, 1)
            attend(2 * j, 0, thr0, cum0)
            issue_qk(2 * j + 2, 0)
            attend(2 * j + 1, 1, thr1, cum1)

        def pair_body(j, carry):
            if not with_ties:
                pair(j, lo, lo)
                return carry
            cum0 = tie_sc[...]
            cum1 = cum0 + cnt_sc[2 * j]
            cum2 = cum1 + cnt_sc[2 * j + 1]
            tie_sc[...] = cum2
            all0, all1 = cum1 <= budget, cum2 <= budget
            runs_out = jnp.logical_or(jnp.logical_and(jnp.logical_not(all0), cum0 < budget),
                                      jnp.logical_and(jnp.logical_not(all1), cum1 < budget))
            ranked = jnp.max(jnp.where(runs_out, 1.0, 0.0)) > 0.5

            @pl.when(ranked)
            def _():
                pair(j, None, None, cum0, cum1)

            @pl.when(jnp.logical_not(ranked))
            def _():
                pair(j, jnp.where(all0, lo, hi), jnp.where(all1, lo, hi))
            return carry
        lax.fori_loop(0, npair, pair_body, 0)

    @pl.when(need_ties)
    def _():
        run(True)

    @pl.when(jnp.logical_not(need_ties))
    def _():
        run(False)

    first_head = lax.broadcasted_iota(I32, (LANES, tq), 0) < ATTN_HEAD_DIM
    normed = [acc_sc[h, 0:LANES, :] / acc_sc[h, LANES:LANES + 1, :] for h in range(ATTN_HEADS)]
    for p in range(ATTN_HEADS // 2):
        pair_t = jnp.where(first_head, normed[2 * p], normed[2 * p + 1])
        o_ref[:, LANES * p:LANES * (p + 1)] = jnp.transpose(pair_t)


def _dsa_call(rel_bias, qi, wi, q, ki2, k, v, *, tq):
    s = q.shape[0]
    assert tq > REL_MAX_DIST and s % tq == 0
    nb = s // tq
    topk = min(TOPK_MAX, s // 4)
    kern = functools.partial(_dsa_kernel, tq=tq, topk=topk,
                             idx_scale=(IDX_DIM ** -0.5) * (IDX_HEADS ** -0.5))
    whole = lambda width: pl.BlockSpec((s, width), lambda i: (0, 0), pipeline_mode=pl.Buffered(1))
    n_pairs = ATTN_WIDTH // LANES
    vt = jnp.transpose(v.reshape(nb, tq, n_pairs, LANES), (0, 2, 3, 1))
    vt = jnp.concatenate([vt, jnp.ones((nb, n_pairs, _VT_ROWS - LANES, tq), BF16)], axis=2)
    vt = vt.reshape(nb, n_pairs * _VT_ROWS, tq)
    return pl.pallas_call(
        kern,
        grid=(nb,),
        in_specs=[pl.BlockSpec(memory_space=pltpu.SMEM),
                  pl.BlockSpec((tq, IDX_HEADS * IDX_DIM), lambda i: (i, 0)),
                  pl.BlockSpec((tq, LANES), lambda i: (i, 0)),
                  pl.BlockSpec((tq, ATTN_WIDTH), lambda i: (i, 0)),
                  whole(2 * IDX_DIM), whole(ATTN_WIDTH),
                  pl.BlockSpec(vt.shape, lambda i: (0, 0, 0), pipeline_mode=pl.Buffered(1))],
        out_specs=pl.BlockSpec((tq, ATTN_WIDTH), lambda i: (i, 0)),
        out_shape=jax.ShapeDtypeStruct((s, ATTN_WIDTH), F32),
        scratch_shapes=[pltpu.VMEM((nb + nb % 2, tq, tq), F32),
                        pltpu.VMEM((nb + nb % 2, tq, tq), BF16),
                        pltpu.VMEM((ATTN_HEADS, _N_KINDS, tq, tq), BF16),
                        pltpu.VMEM((3, tq, tq), F32),
                        pltpu.VMEM((2, ATTN_HEADS, tq, tq), BF16),
                        pltpu.VMEM((IDX_HEADS, LANES, tq), BF16),
                        pltpu.VMEM((ATTN_HEADS, LANES, tq), BF16),
                        pltpu.VMEM((ATTN_HEADS, 1, tq), F32),
                        pltpu.VMEM((ATTN_HEADS, _VT_ROWS, tq), F32),
                        pltpu.VMEM((1, tq), F32),
                        pltpu.VMEM((nb + nb % 2, 1, tq), F32)],
        compiler_params=_cparams("arbitrary"),
        name="dsa_attention",
    )(rel_bias, qi, wi, q, ki2, k, vt)


_HGRN_CHUNK = 128
_HGRN_BASE = SUBLANES


def _hgrn_kernel(h4_ref, lb_ref, ng_ref, o_ref, st_sc, *, layer, rows):
    c_len = _HGRN_CHUNK

    @pl.when(pl.program_id(0) == 0)
    def _():
        st_sc[...] = jnp.zeros_like(st_sc)

    lb_all = lb_ref[...]
    e = jnp.exp(lb_all - jnp.max(lb_all, axis=0, keepdims=True))
    sm = e / jnp.sum(e, axis=0, keepdims=True)
    lbs = jnp.sum(sm[0:layer + 1, :], axis=0, keepdims=True) - sm[0:1, :]

    r1 = lax.broadcasted_iota(I32, (c_len, 1), 0)
    rr = lax.broadcasted_iota(I32, (c_len, c_len), 0)
    cc = lax.broadcasted_iota(I32, (c_len, c_len), 1)

    def block_ref(a, blk, r):
        x = a.reshape(c_len // blk, blk, a.shape[-1])[:, r:r + 1, :]
        return jnp.broadcast_to(x, (c_len // blk, blk, a.shape[-1])).reshape(c_len, a.shape[-1])

    w = HGRN_WIDTH
    for c in range(rows // c_len):
        rs = slice(c * c_len, (c + 1) * c_len)
        hq = h4_ref[rs, 0:w]
        z = h4_ref[rs, w:2 * w]
        val = h4_ref[rs, 2 * w:3 * w]
        hg = h4_ref[rs, 3 * w:4 * w]
        key = (1.0 - lbs) * jax.nn.sigmoid(-z)
        g = jnp.log1p(-jnp.minimum(key, KEY_MAX))
        a_cum = g
        sh = 1
        while sh < c_len:
            a_cum = a_cum + jnp.where(r1 >= sh, pltpu.roll(a_cum, sh, 0), 0.0)
            sh *= 2
        qf = _silu(hq)
        outs = []
        for h in range(HGRN_HEADS):
            hs = slice(h * HGRN_DIM, (h + 1) * HGRN_DIM)
            ah, qh, kh, vh = a_cum[:, hs], qf[:, hs], key[:, hs], val[:, hs]
            vh16 = vh.astype(BF16)
            st = st_sc[h]
            o = _dot_nt((qh * jnp.exp(ah)).astype(BF16), st.astype(BF16))
            ref = block_ref(ah, _HGRN_BASE, _HGRN_BASE // 2 - 1)
            sc = _dot_nt((qh * jnp.exp(ah - ref)).astype(BF16), (kh * jnp.exp(ref - ah)).astype(BF16))
            same = (rr >> _log2(_HGRN_BASE)) == (cc >> _log2(_HGRN_BASE))
            scores = jnp.where(jnp.logical_and(same, cc <= rr), sc, 0.0)
            half = _HGRN_BASE
            while half < c_len:
                blk = 2 * half
                ref = block_ref(ah, blk, half - 1)
                second = (r1 & (blk - 1)) >= half
                ql = jnp.where(second, qh * jnp.exp(jnp.minimum(ah - ref, 0.0)), 0.0)
                kl = jnp.where(second, 0.0, kh * jnp.exp(jnp.minimum(ref - ah, 0.0)))
                sc = _dot_nt(ql.astype(BF16), kl.astype(BF16))
                scores = scores + jnp.where((rr >> _log2(blk)) == (cc >> _log2(blk)), sc, 0.0)
                half = blk
            o = o + _dot(scores.astype(BF16), vh16)
            a_last = ah[c_len - 1:c_len, :]
            kd = (kh * jnp.exp(a_last - ah)).astype(BF16)
            st_sc[h] = st * jnp.exp(a_last) + _dot_tn(vh16, kd)
            o = o * lax.rsqrt(jnp.mean(o * o, axis=-1, keepdims=True) + RMS_EPS)
            outs.append(o)
        o_ref[rs, :] = jnp.concatenate(outs, axis=1) * ng_ref[...] * _silu(hg)


def _hgrn_call(h4, lb, ng, *, layer, rows):
    s = h4.shape[0]
    kern = functools.partial(_hgrn_kernel, layer=layer, rows=rows)
    return pl.pallas_call(
        kern,
        grid=(s // rows,),
        in_specs=[pl.BlockSpec((rows, 4 * HGRN_WIDTH), lambda i: (i, 0)),
                  pl.BlockSpec(lb.shape, lambda i: (0, 0)),
                  pl.BlockSpec((1, HGRN_WIDTH), lambda i: (0, 0))],
        out_specs=pl.BlockSpec((rows, HGRN_WIDTH), lambda i: (i, 0)),
        out_shape=jax.ShapeDtypeStruct((s, HGRN_WIDTH), F32),
        scratch_shapes=[pltpu.VMEM((HGRN_HEADS, HGRN_DIM, HGRN_DIM), F32)],
        compiler_params=_cparams("arbitrary"),
        name="hgrn2",
    )(h4, lb, ng)


_POOL_HALO = 16


def _merge_kernel(x_ref, mod_ref, a_ref, ah_ref, yb_ref, yc_ref, g_ref, pw_ref, ps_ref, wb_ref, wo_ref,
                  lng_ref, lnb_ref, o_ref, *, tm, alpha):
    i = pl.program_id(0)
    d = x_ref.shape[1]
    _, _, gate = _mod_rows(mod_ref, 1)
    a = a_ref[...]
    halo = jnp.where(i == 0, 0.0, ah_ref[...])
    ext = jnp.concatenate([halo, a], axis=0)
    lane = lax.broadcasted_iota(I32, (tm, POOL_WIDTH), 1)
    grp = lane >> _log2(POOL_GROUP)
    pos1 = (i * tm + lax.broadcasted_iota(I32, (tm, POOL_WIDTH), 0) + 1).astype(F32)
    win = jnp.zeros((tm, POOL_WIDTH), F32)
    cnt = jnp.zeros((tm, POOL_WIDTH), F32)
    ssum, have = ext, 1
    for gi, wd in enumerate(POOL_WINDOWS):
        while have < wd:
            ssum = ssum + pltpu.roll(ssum, have, 0)
            have *= 2
        assert have == wd
        win = jnp.where(grp == gi, ssum[_POOL_HALO:, :], win)
        cnt = jnp.where(grp == gi, jnp.minimum(pos1, float(wd)), cnt)
    dlt = (win / cnt - a).astype(BF16)
    ya = _dot(dlt, pw_ref[...]) * ps_ref[...]
    za = _dot(ya.astype(BF16), wb_ref[0:POOL_WIDTH, :])
    zb = _dot(yb_ref[...].astype(BF16), wb_ref[POOL_WIDTH:POOL_WIDTH + ATTN_WIDTH, :])
    zc = _dot(yc_ref[...].astype(BF16), wb_ref[POOL_WIDTH + ATTN_WIDTH:, :])
    g0, g1, g2 = (jax.nn.sigmoid(g_ref[:, j * d:(j + 1) * d].astype(F32)) for j in range(3))
    merged = g0 * za + g1 * zb + g2 * zc
    y = _dot(merged.astype(BF16), wo_ref[...])
    z = alpha * x_ref[...] + (1.0 + gate) * y
    o_ref[...] = _layer_norm(z, lng_ref[1:2, :], lnb_ref[1:2, :])


def _merge_call(x, mod, a, yb, yc, gates, pw_bd, ps, wb, wo, ln_g, ln_b, *, tm, alpha):
    s, d = x.shape
    kern = functools.partial(_merge_kernel, tm=tm, alpha=alpha)
    full = lambda arr: pl.BlockSpec(arr.shape, lambda i: (0, 0))
    hb = tm // _POOL_HALO
    return pl.pallas_call(
        kern,
        grid=(s // tm,),
        in_specs=[pl.BlockSpec((tm, d), lambda i: (i, 0)),
                  full(mod),
                  pl.BlockSpec((tm, POOL_WIDTH), lambda i: (i, 0)),
                  pl.BlockSpec((_POOL_HALO, POOL_WIDTH), lambda i: (jnp.maximum(i * hb - 1, 0), 0)),
                  pl.BlockSpec((tm, ATTN_WIDTH), lambda i: (i, 0)),
                  pl.BlockSpec((tm, HGRN_WIDTH), lambda i: (i, 0)),
                  pl.BlockSpec((tm, 3 * d), lambda i: (i, 0)),
                  full(pw_bd), full(ps), full(wb), full(wo), full(ln_g), full(ln_b)],
        out_specs=pl.BlockSpec((tm, d), lambda i: (i, 0)),
        out_shape=jax.ShapeDtypeStruct((s, d), F32),
        compiler_params=_cparams("arbitrary"),
        name="pool_merge",
    )(x, mod, a, a, yb, yc, gates, pw_bd, ps, wb, wo, ln_g, ln_b)


def _pool_blockdiag(pool_w):
    n = pool_w.shape[0]
    rows = [jnp.concatenate([pool_w[g] if g == j else jnp.zeros_like(pool_w[g]) for j in range(n)], axis=1)
            for g in range(n)]
    return jnp.concatenate(rows, axis=0)


def kernel(x, c, w_ada, b_ada, ln_g, ln_b, ffn_w_in, ffn_w_out, mix_w_in, pool_w, pool_scale, rel_bias,
           hgrn_lb, hgrn_norm_g, w_branch, w_out):
    batch, s, d = x.shape
    assert batch == 1, "kernels are written for a single sequence"
    depth = w_ada.shape[0]
    alpha = (2 * depth) ** 0.25
    tm_ffn = min(512, s)
    tf = 256
    tm_proj = min(256, s)
    tq = min(256, s)
    tm_merge = min(256, s)
    hgrn_rows = min(256, s)

    mods = _ada_call(c, w_ada, b_ada)
    xs = x[0]
    for l in range(depth):
        mod = mods[l].reshape(3 * N_SUB, d)
        xs = _ffn_call(xs, mod, ffn_w_in[l, 0].astype(BF16), ffn_w_out[l, 0].astype(BF16), ln_g[l], ln_b[l],
                       sub=0, alpha=alpha, tm=tm_ffn, tf=tf)
        gates, a, h4, qi, q, k, v, ki2, wi = _proj_call(xs, mod, _proj_weight(mix_w_in[l]), tm=tm_proj)
        yb = _dsa_call(rel_bias, qi, wi, q, ki2, k, v, tq=tq)
        yc = _hgrn_call(h4, hgrn_lb, hgrn_norm_g[l].reshape(1, -1), layer=l, rows=hgrn_rows)
        xs = _merge_call(xs, mod, a, yb, yc, gates, _pool_blockdiag(pool_w[l]).astype(BF16),
                         pool_scale[l].reshape(1, -1), w_branch[l].astype(BF16), w_out[l].astype(BF16),
                         ln_g[l], ln_b[l], tm=tm_merge, alpha=alpha)
        xs = _ffn_call(xs, mod, ffn_w_in[l, 1].astype(BF16), ffn_w_out[l, 1].astype(BF16), ln_g[l], ln_b[l],
                       sub=2, alpha=alpha, tm=tm_ffn, tf=tf)
    return xs[None]
```

```python
import functools
import math

import jax
import jax.numpy as jnp
from jax import lax
from jax.experimental import pallas as pl
from jax.experimental.pallas import tpu as pltpu

F32 = jnp.float32
BF16 = jnp.bfloat16
I32 = jnp.int32
HIGHEST = lax.Precision.HIGHEST

POOL_WINDOWS = (2, 4, 8, 16)
POOL_GROUP = 64
POOL_WIDTH = POOL_GROUP * len(POOL_WINDOWS)
ATTN_HEADS = 4
ATTN_HEAD_DIM = 64
ATTN_WIDTH = ATTN_HEADS * ATTN_HEAD_DIM
IDX_HEADS = 8
IDX_DIM = 64
TOPK_MAX = 256
HGRN_HEADS = 4
HGRN_DIM = 128
HGRN_WIDTH = HGRN_HEADS * HGRN_DIM
KEY_MAX = 1.0 - 1e-6
REL_BUCKETS = 32
REL_MAX_DIST = 128
N_SUB = 3
LN_EPS = 1e-5
RMS_EPS = 1e-6

LANES = 128
SUBLANES = 8
VMEM_LIMIT_BYTES = 58 * 1024 * 1024

NEG_INF = float("-inf")
POS_INF = float("inf")


def _cparams(*sem):
    return pltpu.CompilerParams(dimension_semantics=sem, vmem_limit_bytes=VMEM_LIMIT_BYTES)


def _dot(a, b):
    return jnp.dot(a, b, preferred_element_type=F32)


def _dot_nt(a, b):
    return lax.dot_general(a, b, (((1,), (1,)), ((), ())), preferred_element_type=F32)


def _dot_tn(a, b):
    return lax.dot_general(a, b, (((0,), (0,)), ((), ())), preferred_element_type=F32)


def _log2(n):
    assert n & (n - 1) == 0
    return n.bit_length() - 1


def _silu(x):
    return x * jax.nn.sigmoid(x)


def _mod_rows(mod_ref, j):
    return mod_ref[3 * j:3 * j + 1, :], mod_ref[3 * j + 1:3 * j + 2, :], mod_ref[3 * j + 2:3 * j + 3, :]


def _layer_norm(z, g, b):
    mu = jnp.mean(z, axis=-1, keepdims=True)
    zc = z - mu
    var = jnp.mean(zc * zc, axis=-1, keepdims=True)
    return zc * lax.rsqrt(var + LN_EPS) * g + b


def _ada_kernel(c_ref, w_ref, b_ref, o_ref):
    @pl.when(pl.program_id(1) == 0)
    def _():
        o_ref[0] = b_ref[0]

    c = c_ref[...]
    o_ref[0] += jnp.sum(_silu(c) * w_ref[0], axis=0, keepdims=True)


def _ada_call(c, w_ada, b_ada):
    depth, d, n = w_ada.shape
    rows = LANES
    return pl.pallas_call(
        _ada_kernel,
        grid=(depth, d // rows),
        in_specs=[pl.BlockSpec((rows, 1), lambda l, k: (k, 0)),
                  pl.BlockSpec((1, rows, n), lambda l, k: (l, k, 0)),
                  pl.BlockSpec((1, 1, n), lambda l, k: (l, 0, 0))],
        out_specs=pl.BlockSpec((1, 1, n), lambda l, k: (l, 0, 0)),
        out_shape=jax.ShapeDtypeStruct((depth, 1, n), F32),
        compiler_params=_cparams("arbitrary", "arbitrary"),
        name="ada_mod",
    )(c.reshape(d, 1), w_ada, b_ada.reshape(depth, 1, n))


def _ffn_kernel(x_ref, mod_ref, wi_ref, wo_ref, lng_ref, lnb_ref, o_ref, *, sub, tf, alpha):
    shift, scale, gate = _mod_rows(mod_ref, sub)
    ff = wo_ref.shape[0]
    x = x_ref[...]
    h = (x * (1.0 + scale) + shift).astype(BF16)
    y = jnp.zeros(x.shape, F32)
    for c0 in range(0, ff, tf):
        g = _dot(h, wi_ref[:, c0:c0 + tf])
        u = _dot(h, wi_ref[:, ff + c0:ff + c0 + tf])
        y = y + _dot((_silu(g) * u).astype(BF16), wo_ref[c0:c0 + tf, :])
    z = alpha * x + (0.5 * (1.0 + gate)) * y
    o_ref[...] = _layer_norm(z, lng_ref[sub:sub + 1, :], lnb_ref[sub:sub + 1, :])


def _ffn_call(x, mod, w_in, w_out, ln_g, ln_b, *, sub, alpha, tm, tf):
    s, d = x.shape
    ff = w_out.shape[0]
    assert ff % tf == 0
    kern = functools.partial(_ffn_kernel, sub=sub, tf=tf, alpha=alpha)
    resident = lambda arr: pl.BlockSpec(arr.shape, lambda i: (0, 0), pipeline_mode=pl.Buffered(1))
    return pl.pallas_call(
        kern,
        grid=(s // tm,),
        in_specs=[pl.BlockSpec((tm, d), lambda i: (i, 0)),
                  pl.BlockSpec((3 * N_SUB, d), lambda i: (0, 0)),
                  resident(w_in), resident(w_out),
                  pl.BlockSpec((N_SUB, d), lambda i: (0, 0)),
                  pl.BlockSpec((N_SUB, d), lambda i: (0, 0))],
        out_specs=pl.BlockSpec((tm, d), lambda i: (i, 0)),
        out_shape=jax.ShapeDtypeStruct((s, d), F32),
        compiler_params=_cparams("arbitrary"),
        name="ffn",
    )(x, mod, w_in, w_out, ln_g, ln_b)


_PROJ_OUTS = (("gates", 3 * 1024, BF16), ("a", POOL_WIDTH, F32), ("h4", 4 * HGRN_WIDTH, F32),
              ("qi", IDX_HEADS * IDX_DIM, BF16), ("q", ATTN_WIDTH, BF16), ("k", ATTN_WIDTH, BF16),
              ("v", ATTN_WIDTH, BF16), ("ki2", 2 * IDX_DIM, BF16), ("wi", LANES, F32))
_PROJ_CHUNK = 1024


def _proj_kernel(x_ref, mod_ref, w_ref, *o_refs):
    shift, scale, _ = _mod_rows(mod_ref, 1)
    h = (x_ref[...] * (1.0 + scale) + shift).astype(BF16)
    c0 = 0
    for (_, width, dt), o_ref in zip(_PROJ_OUTS, o_refs):
        for cc in range(0, width, _PROJ_CHUNK):
            cw = min(_PROJ_CHUNK, width - cc)
            o_ref[:, cc:cc + cw] = _dot(h, w_ref[:, c0 + cc:c0 + cc + cw]).astype(dt)
        c0 += width


def _proj_weight(w):
    d = w.shape[0]
    widths = (POOL_WIDTH, ATTN_WIDTH, ATTN_WIDTH, ATTN_WIDTH, IDX_HEADS * IDX_DIM, IDX_DIM, IDX_HEADS,
              HGRN_WIDTH, HGRN_WIDTH, HGRN_WIDTH, HGRN_WIDTH, 3 * d)
    parts, c0 = [], 0
    for wd in widths:
        parts.append(w[:, c0:c0 + wd])
        c0 += wd
    a, q, k, v, qi, ki, wi, hq, hf, hi, hg, gates = parts
    wi_pad = jnp.pad(wi, ((0, 0), (0, LANES - IDX_HEADS)))
    q = q * (ATTN_HEAD_DIM ** -0.5 * _LOG2E)
    return jnp.concatenate([gates, a, hq, hf, hi, hg, qi, q, k, v, ki, ki, wi_pad], axis=1).astype(BF16)


def _proj_call(x, mod, w, *, tm):
    s, d = x.shape
    n = w.shape[1]
    assert n == sum(wd for _, wd, _ in _PROJ_OUTS)
    return pl.pallas_call(
        _proj_kernel,
        grid=(s // tm,),
        in_specs=[pl.BlockSpec((tm, d), lambda i: (i, 0)),
                  pl.BlockSpec((3 * N_SUB, d), lambda i: (0, 0)),
                  pl.BlockSpec((d, n), lambda i: (0, 0))],
        out_specs=[pl.BlockSpec((tm, wd), lambda i: (i, 0)) for _, wd, _ in _PROJ_OUTS],
        out_shape=[jax.ShapeDtypeStruct((s, wd), dt) for _, wd, dt in _PROJ_OUTS],
        compiler_params=_cparams("arbitrary"),
        name="mix_proj",
    )(x, mod, w)


_BISECT_MAX_ITERS = 320
_F32_TINY = float(jnp.finfo(jnp.float32).tiny)
_COARSE_PASSES = 9
_COARSE_MIN_ABS, _COARSE_MAX_ABS = 1e-30, 1e30
_KIND_FAR, _KIND_PAD, _N_KINDS = 2, 3, 4
_VT_ROWS = LANES + 16
_SMASK_FULL, _SMASK_DIAG, _SMASK_PAD = 0, 1, 2
_LOG2E = math.log2(math.e)


def _t5_bucket(dist):
    max_exact = REL_BUCKETS // 2
    d32 = jnp.maximum(dist, 1).astype(F32)
    large = max_exact + (jnp.log(d32 / max_exact) / math.log(REL_MAX_DIST / max_exact)
                         * (REL_BUCKETS - max_exact)).astype(I32)
    large = jnp.minimum(large, REL_BUCKETS - 1)
    return jnp.where(dist < max_exact, dist, large)


def _dsa_kernel(relb_ref, qi_ref, wi_ref, q_ref, ki2_ref, k_ref, vt_ref, o_ref,
                s_sc, sb_sc, bias_sc, smask_sc, lg_sc, qim_sc, qm_sc, m_sc, acc_sc,
                *, tq, topk, idx_scale):
    tk = tq
    i = pl.program_id(0)
    krow = lax.broadcasted_iota(I32, (tk, tq), 0)
    qcol = lax.broadcasted_iota(I32, (tk, tq), 1)
    causal = krow <= qcol
    lane = lax.broadcasted_iota(I32, (tq, LANES), 1)
    lo_half = lane < ATTN_HEAD_DIM

    @pl.when(i == 0)
    def _():
        for which in range(2):
            bucket = _t5_bucket(jnp.maximum(qcol - krow + which * tq, 0))
            for h in range(ATTN_HEADS):
                b = jnp.zeros((tk, tq), F32)
                for bk in range(REL_BUCKETS):
                    b = jnp.where(bucket == bk, relb_ref[bk, h] * _LOG2E, b)
                bias_sc[h, which] = (jnp.where(causal, b, NEG_INF) if which == 0 else b).astype(BF16)
        for h in range(ATTN_HEADS):
            bias_sc[h, _KIND_FAR] = jnp.full((tk, tq), relb_ref[REL_BUCKETS - 1, h] * _LOG2E, F32).astype(BF16)
            bias_sc[h, _KIND_PAD] = jnp.full((tk, tq), NEG_INF, BF16)
        smask_sc[_SMASK_FULL] = jnp.zeros((tk, tq), F32)
        smask_sc[_SMASK_DIAG] = jnp.where(causal, 0.0, NEG_INF)
        smask_sc[_SMASK_PAD] = jnp.full((tk, tq), NEG_INF, F32)

    def head_operands(src_ref, dst_sc, n_heads):
        src = src_ref[...].astype(F32)
        for h in range(n_heads):
            pair = src[:, LANES * (h // 2):LANES * (h // 2 + 1)]
            pair = jnp.where(lo_half if h % 2 == 0 else ~lo_half, pair, 0.0)
            dst_sc[h] = jnp.transpose(pair).astype(BF16)

    head_operands(qi_ref, qim_sc, IDX_HEADS)
    head_operands(q_ref, qm_sc, ATTN_HEADS)

    wt = jnp.transpose(wi_ref[...])

    def key_rows(jb):
        return pl.ds(pl.multiple_of(jb * tk, tk), tk)

    def fold(x):
        return x.reshape(tk // SUBLANES, SUBLANES, tq)

    def index_scores(jb):
        kb = ki2_ref[key_rows(jb), :]
        acc = jnp.zeros((tk, tq), F32)
        for h in range(IDX_HEADS):
            acc = acc + wt[h:h + 1, :] * jnp.maximum(_dot(kb, qim_sc[h]), 0.0)
        return acc * idx_scale

    npair = (i + 2) // 2
    last_kv = k_ref.shape[0] // tk - 1

    def p1_pair(j, carry):
        rmin, rmax = carry
        for jb in (2 * j, 2 * j + 1):
            raw = index_scores(jnp.minimum(jb, last_kv))
            kind = jnp.where(jb > i, _SMASK_PAD, jnp.where(jb == i, _SMASK_DIAG, _SMASK_FULL))
            sc = raw + smask_sc[kind]
            s_sc[jb] = sc
            sb_sc[jb] = sc.astype(BF16)
            rmin = jnp.minimum(rmin, jnp.min(fold(raw), axis=0))
            rmax = jnp.maximum(rmax, jnp.max(fold(raw), axis=0))
        return rmin, rmax

    rmin, rmax = lax.fori_loop(0, npair, p1_pair, (jnp.full((SUBLANES, tq), POS_INF, F32),
                                                   jnp.full((SUBLANES, tq), NEG_INF, F32)))
    rowmin = jnp.min(rmin, axis=0, keepdims=True)
    rowmax = jnp.max(rmax, axis=0, keepdims=True)

    kf = float(topk)
    nfin = (i * tq + lax.broadcasted_iota(I32, (1, tq), 1) + 1).astype(F32)
    small = nfin <= kf

    def walk_blocks(blocks_fn, init):
        return lax.fori_loop(0, npair, lambda j, c: blocks_fn([2 * j, 2 * j + 1], c), init)

    def count_ge(x):
        def blocks_fn(jbs, cnt):
            for jb in jbs:
                cnt = cnt + jnp.sum(fold(jnp.where(s_sc[jb] >= x, 1.0, 0.0)), axis=0)
            return cnt
        return jnp.sum(walk_blocks(blocks_fn, jnp.zeros((SUBLANES, tq), F32)), axis=0, keepdims=True)

    def count_ge16(xb):
        one, zero = jnp.ones((), BF16), jnp.zeros((), BF16)
        rows16 = 2 * SUBLANES

        def blocks_fn(jbs, cnt):
            parts = []
            for jb in jbs:
                c = jnp.where(sb_sc[jb] >= xb, one, zero)
                parts += [c[r0:r0 + rows16, :] for r0 in range(0, tk, rows16)]
            while len(parts) > 1:
                parts = [a + b for a, b in zip(parts[0::2], parts[1::2])]
            return cnt + parts[0].astype(F32)
        return jnp.sum(walk_blocks(blocks_fn, jnp.zeros((rows16, tq), F32)), axis=0, keepdims=True)

    def coarse_probe(mid):
        xb = mid.astype(BF16)
        xbits = lax.bitcast_convert_type(xb.astype(F32), I32)
        neg = xbits < 0
        mu = jnp.where(neg, xbits + 0x8000, xbits - 0x8000)
        odd = (xbits & 0x10000) != 0
        tbits = jnp.where(odd, jnp.where(neg, mu - 1, mu + 1), mu)
        t = lax.bitcast_convert_type(tbits, F32)
        ok = jnp.logical_and(jnp.abs(xb.astype(F32)) > _COARSE_MIN_ABS, jnp.abs(xb.astype(F32)) < _COARSE_MAX_ABS)
        return xb, t, ok

    def narrow(st, probe, c, active):
        lo, hi, clo, chi, done = st
        up = jnp.logical_and(active, c >= kf)
        dn = jnp.logical_and(active, c < kf)
        return (jnp.where(up, probe, lo), jnp.where(dn, probe, hi), jnp.where(up, c, clo),
                jnp.where(dn, c, chi), jnp.where(jnp.logical_and(active, c == kf), 1.0, done))

    def coarse_body(_, st):
        lo, hi, _, _, done = st
        xb, t, ok = coarse_probe(0.5 * lo + 0.5 * hi)
        ok = jnp.logical_and(ok, jnp.logical_and(t > lo, t < hi))
        return narrow(st, t, count_ge16(xb), jnp.logical_and(done < 0.5, ok))

    def fine_cond(st):
        it, (_, _, _, _, done) = st
        return jnp.logical_and(it < _BISECT_MAX_ITERS, jnp.min(done) < 0.5)

    def fine_body(st):
        it, inner = st
        lo, hi, _, _, done = inner
        mid = 0.5 * lo + 0.5 * hi
        mid = jnp.where(jnp.logical_and(lo == 0.0, hi > _F32_TINY), _F32_TINY, mid)
        mid = jnp.where(jnp.logical_and(lo < 0.0, hi > 0.0), 0.0, mid)
        stuck = jnp.logical_or(jnp.logical_or(mid <= lo, mid >= hi),
                               jnp.logical_and(lo == 0.0, hi <= _F32_TINY))
        lo, hi, clo, chi, done = narrow(inner, mid, count_ge(mid),
                                        jnp.logical_and(done < 0.5, jnp.logical_not(stuck)))
        return it + 1, (lo, hi, clo, chi, jnp.where(stuck, 1.0, done))

    mbits = lax.bitcast_convert_type(rowmax, I32)
    above_max = jnp.where(rowmax == 0.0, _F32_TINY,
                          lax.bitcast_convert_type(jnp.where(rowmax > 0.0, mbits + 1, mbits - 1), F32))
    st0 = (jnp.where(small, NEG_INF, rowmin), above_max, nfin, jnp.zeros((1, tq), F32),
           jnp.where(small, 1.0, 0.0))
    st1 = lax.fori_loop(0, _COARSE_PASSES, coarse_body, st0)
    _, (lo, hi, clo, chi, _) = lax.while_loop(fine_cond, fine_body, (jnp.int32(0), st1))
    budget = kf - chi
    need_ties = jnp.max((clo - chi) - budget) > 0.5

    pairs = [slice(LANES * (h // 2), LANES * (h // 2 + 1)) for h in range(ATTN_HEADS)]

    def issue_qk(jb, slot):
        kb = k_ref[key_rows(jnp.minimum(jb, last_kv)), :]
        for h in range(ATTN_HEADS):
            lg_sc[slot, h] = _dot(kb[:, pairs[h]], qm_sc[h]).astype(BF16)

    def attend(jb, slot, thr, cum=None):
        s = s_sc[jb]
        if cum is None:
            sel = s >= thr
        else:
            cand = jnp.logical_and(s >= lo, s < hi)
            before = (qcol < krow).astype(BF16)
            rank = cum + _dot(before, jnp.where(cand, 1.0, 0.0).astype(BF16))
            sel = jnp.logical_or(s >= hi, jnp.logical_and(cand, rank < budget))
        kind = jnp.where(jb > i, _KIND_PAD, jnp.minimum(i - jb, _KIND_FAR))
        selm = jnp.where(sel, 0.0, NEG_INF).astype(BF16)
        probs, alphas = [], []
        for h in range(ATTN_HEADS):
            lg = lg_sc[slot, h] + bias_sc[h, kind] + selm
            m_old = m_sc[h]
            m_new = jnp.maximum(m_old, jnp.max(lg, axis=0, keepdims=True).astype(F32))
            m_safe = jnp.where(m_new == NEG_INF, 0.0, m_new)
            alphas.append(jnp.exp2(m_old - m_safe))
            probs.append(jnp.exp2(lg - m_safe.astype(BF16)))
            m_sc[h] = m_new
        vb = vt_ref[jnp.minimum(jb, last_kv)]
        for h in range(ATTN_HEADS):
            vrows = slice(_VT_ROWS * (h // 2), _VT_ROWS * (h // 2 + 1))
            acc_sc[h] = alphas[h] * acc_sc[h] + _dot(vb[vrows, :], probs[h])

    def run(with_ties):
        m_sc[...] = jnp.full(m_sc.shape, NEG_INF, F32)
        acc_sc[...] = jnp.zeros_like(acc_sc)
        issue_qk(0, 0)

        def pair(j, thr0, thr1, cum0=None, cum1=None):
            issue_qk(2 * j + 1, 1)
            attend(2 * j, 0, thr0, cum0)
            issue_qk(2 * j + 2, 0)
            attend(2 * j + 1, 1, thr1, cum1)

        if not with_ties:
            def pair_body(j, carry):
                pair(j, lo, lo)
                return carry
            lax.fori_loop(0, npair, pair_body, 0)
            return

        last_s = s_sc.shape[0] - 1

        def cand_count(jb):
            s = s_sc[jnp.minimum(jb, last_s)]
            cand = jnp.logical_and(s >= lo, s < hi)
            return jnp.sum(jnp.where(cand, 1.0, 0.0), axis=0, keepdims=True)

        def needs_ranking(cum0, ca, cb):
            cum1 = cum0 + ca
            runs_out = jnp.logical_or(jnp.logical_and(cum1 > budget, cum0 < budget),
                                      jnp.logical_and(cum1 + cb > budget, cum1 < budget))
            return (jnp.max(jnp.where(runs_out, 1.0, 0.0)) > 0.5).astype(I32)

        def tie_pair_body(j, carry):
            ranked, cum0, ca, cb = carry
            cum1 = cum0 + ca
            cum2 = cum1 + cb

            def variant(with_rank):
                def fn():
                    na, nb = cand_count(2 * j + 2), cand_count(2 * j + 3)
                    nxt = needs_ranking(cum2, na, nb)
                    if with_rank:
                        pair(j, None, None, cum0, cum1)
                    else:
                        pair(j, jnp.where(cum1 <= budget, lo, hi), jnp.where(cum2 <= budget, lo, hi))
                    return nxt, cum2, na, nb
                return fn
            return lax.cond(ranked == 1, variant(True), variant(False))

        c0, c1 = cand_count(0), cand_count(1)
        zero = jnp.zeros((1, tq), F32)
        lax.fori_loop(0, npair, tie_pair_body, (needs_ranking(zero, c0, c1), zero, c0, c1))

    @pl.when(need_ties)
    def _():
        run(True)

    @pl.when(jnp.logical_not(need_ties))
    def _():
        run(False)

    first_head = lax.broadcasted_iota(I32, (LANES, tq), 0) < ATTN_HEAD_DIM
    normed = [acc_sc[h, 0:LANES, :] / acc_sc[h, LANES:LANES + 1, :] for h in range(ATTN_HEADS)]
    for p in range(ATTN_HEADS // 2):
        pair_t = jnp.where(first_head, normed[2 * p], normed[2 * p + 1])
        o_ref[:, LANES * p:LANES * (p + 1)] = jnp.transpose(pair_t)


def _dsa_call(rel_bias, qi, wi, q, ki2, k, v, *, tq):
    s = q.shape[0]
    assert tq > REL_MAX_DIST and s % tq == 0
    nb = s // tq
    topk = min(TOPK_MAX, s // 4)
    kern = functools.partial(_dsa_kernel, tq=tq, topk=topk,
                             idx_scale=(IDX_DIM ** -0.5) * (IDX_HEADS ** -0.5))
    whole = lambda width: pl.BlockSpec((s, width), lambda i: (0, 0), pipeline_mode=pl.Buffered(1))
    n_pairs = ATTN_WIDTH // LANES
    vt = jnp.transpose(v.reshape(nb, tq, n_pairs, LANES), (0, 2, 3, 1))
    vt = jnp.concatenate([vt, jnp.ones((nb, n_pairs, _VT_ROWS - LANES, tq), BF16)], axis=2)
    vt = vt.reshape(nb, n_pairs * _VT_ROWS, tq)
    return pl.pallas_call(
        kern,
        grid=(nb,),
        in_specs=[pl.BlockSpec(memory_space=pltpu.SMEM),
                  pl.BlockSpec((tq, IDX_HEADS * IDX_DIM), lambda i: (i, 0)),
                  pl.BlockSpec((tq, LANES), lambda i: (i, 0)),
                  pl.BlockSpec((tq, ATTN_WIDTH), lambda i: (i, 0)),
                  whole(2 * IDX_DIM), whole(ATTN_WIDTH),
                  pl.BlockSpec(vt.shape, lambda i: (0, 0, 0), pipeline_mode=pl.Buffered(1))],
        out_specs=pl.BlockSpec((tq, ATTN_WIDTH), lambda i: (i, 0)),
        out_shape=jax.ShapeDtypeStruct((s, ATTN_WIDTH), F32),
        scratch_shapes=[pltpu.VMEM((nb + nb % 2, tq, tq), F32),
                        pltpu.VMEM((nb + nb % 2, tq, tq), BF16),
                        pltpu.VMEM((ATTN_HEADS, _N_KINDS, tq, tq), BF16),
                        pltpu.VMEM((3, tq, tq), F32),
                        pltpu.VMEM((2, ATTN_HEADS, tq, tq), BF16),
                        pltpu.VMEM((IDX_HEADS, LANES, tq), BF16),
                        pltpu.VMEM((ATTN_HEADS, LANES, tq), BF16),
                        pltpu.VMEM((ATTN_HEADS, 1, tq), F32),
                        pltpu.VMEM((ATTN_HEADS, _VT_ROWS, tq), F32)],
        compiler_params=_cparams("arbitrary"),
        name="dsa_attention",
    )(rel_bias, qi, wi, q, ki2, k, vt)


_HGRN_CHUNK = 128
_HGRN_BASE = SUBLANES


def _hgrn_kernel(h4_ref, lb_ref, ng_ref, o_ref, st_sc, *, layer, rows):
    c_len = _HGRN_CHUNK

    @pl.when(pl.program_id(0) == 0)
    def _():
        st_sc[...] = jnp.zeros_like(st_sc)

    lb_all = lb_ref[...]
    e = jnp.exp(lb_all - jnp.max(lb_all, axis=0, keepdims=True))
    sm = e / jnp.sum(e, axis=0, keepdims=True)
    lbs = jnp.sum(sm[0:layer + 1, :], axis=0, keepdims=True) - sm[0:1, :]

    r1 = lax.broadcasted_iota(I32, (c_len, 1), 0)
    rr = lax.broadcasted_iota(I32, (c_len, c_len), 0)
    cc = lax.broadcasted_iota(I32, (c_len, c_len), 1)

    def block_ref(a, blk, r):
        x = a.reshape(c_len // blk, blk, a.shape[-1])[:, r:r + 1, :]
        return jnp.broadcast_to(x, (c_len // blk, blk, a.shape[-1])).reshape(c_len, a.shape[-1])

    w = HGRN_WIDTH
    for c in range(rows // c_len):
        rs = slice(c * c_len, (c + 1) * c_len)
        hq = h4_ref[rs, 0:w]
        z = h4_ref[rs, w:2 * w]
        val = h4_ref[rs, 2 * w:3 * w]
        hg = h4_ref[rs, 3 * w:4 * w]
        key = (1.0 - lbs) * jax.nn.sigmoid(-z)
        g = jnp.log1p(-jnp.minimum(key, KEY_MAX))
        a_cum = g
        sh = 1
        while sh < c_len:
            a_cum = a_cum + jnp.where(r1 >= sh, pltpu.roll(a_cum, sh, 0), 0.0)
            sh *= 2
        qf = _silu(hq)
        outs = []
        for h in range(HGRN_HEADS):
            hs = slice(h * HGRN_DIM, (h + 1) * HGRN_DIM)
            ah, qh, kh, vh = a_cum[:, hs], qf[:, hs], key[:, hs], val[:, hs]
            vh16 = vh.astype(BF16)
            st = st_sc[h]
            o = _dot_nt((qh * jnp.exp(ah)).astype(BF16), st.astype(BF16))
            ref = block_ref(ah, _HGRN_BASE, _HGRN_BASE // 2 - 1)
            sc = _dot_nt((qh * jnp.exp(ah - ref)).astype(BF16), (kh * jnp.exp(ref - ah)).astype(BF16))
            same = (rr >> _log2(_HGRN_BASE)) == (cc >> _log2(_HGRN_BASE))
            scores = jnp.where(jnp.logical_and(same, cc <= rr), sc, 0.0)
            half = _HGRN_BASE
            while half < c_len:
                blk = 2 * half
                ref = block_ref(ah, blk, half - 1)
                second = (r1 & (blk - 1)) >= half
                ql = jnp.where(second, qh * jnp.exp(jnp.minimum(ah - ref, 0.0)), 0.0)
                kl = jnp.where(second, 0.0, kh * jnp.exp(jnp.minimum(ref - ah, 0.0)))
                sc = _dot_nt(ql.astype(BF16), kl.astype(BF16))
                scores = scores + jnp.where((rr >> _log2(blk)) == (cc >> _log2(blk)), sc, 0.0)
                half = blk
            o = o + _dot(scores.astype(BF16), vh16)
            a_last = ah[c_len - 1:c_len, :]
            kd = (kh * jnp.exp(a_last - ah)).astype(BF16)
            st_sc[h] = st * jnp.exp(a_last) + _dot_tn(vh16, kd)
            o = o * lax.rsqrt(jnp.mean(o * o, axis=-1, keepdims=True) + RMS_EPS)
            outs.append(o)
        o_ref[rs, :] = jnp.concatenate(outs, axis=1) * ng_ref[...] * _silu(hg)


def _hgrn_call(h4, lb, ng, *, layer, rows):
    s = h4.shape[0]
    kern = functools.partial(_hgrn_kernel, layer=layer, rows=rows)
    return pl.pallas_call(
        kern,
        grid=(s // rows,),
        in_specs=[pl.BlockSpec((rows, 4 * HGRN_WIDTH), lambda i: (i, 0)),
                  pl.BlockSpec(lb.shape, lambda i: (0, 0)),
                  pl.BlockSpec((1, HGRN_WIDTH), lambda i: (0, 0))],
        out_specs=pl.BlockSpec((rows, HGRN_WIDTH), lambda i: (i, 0)),
        out_shape=jax.ShapeDtypeStruct((s, HGRN_WIDTH), F32),
        scratch_shapes=[pltpu.VMEM((HGRN_HEADS, HGRN_DIM, HGRN_DIM), F32)],
        compiler_params=_cparams("arbitrary"),
        name="hgrn2",
    )(h4, lb, ng)


_POOL_HALO = 16


def _merge_kernel(x_ref, mod_ref, a_ref, ah_ref, yb_ref, yc_ref, g_ref, pw_ref, ps_ref, wb_ref, wo_ref,
                  lng_ref, lnb_ref, o_ref, *, tm, alpha):
    i = pl.program_id(0)
    d = x_ref.shape[1]
    _, _, gate = _mod_rows(mod_ref, 1)
    a = a_ref[...]
    halo = jnp.where(i == 0, 0.0, ah_ref[...])
    ext = jnp.concatenate([halo, a], axis=0)
    lane = lax.broadcasted_iota(I32, (tm, POOL_WIDTH), 1)
    grp = lane >> _log2(POOL_GROUP)
    pos1 = (i * tm + lax.broadcasted_iota(I32, (tm, POOL_WIDTH), 0) + 1).astype(F32)
    win = jnp.zeros((tm, POOL_WIDTH), F32)
    cnt = jnp.zeros((tm, POOL_WIDTH), F32)
    ssum, have = ext, 1
    for gi, wd in enumerate(POOL_WINDOWS):
        while have < wd:
            ssum = ssum + pltpu.roll(ssum, have, 0)
            have *= 2
        assert have == wd
        win = jnp.where(grp == gi, ssum[_POOL_HALO:, :], win)
        cnt = jnp.where(grp == gi, jnp.minimum(pos1, float(wd)), cnt)
    dlt = (win / cnt - a).astype(BF16)
    ya = _dot(dlt, pw_ref[...]) * ps_ref[...]
    za = _dot(ya.astype(BF16), wb_ref[0:POOL_WIDTH, :])
    zb = _dot(yb_ref[...].astype(BF16), wb_ref[POOL_WIDTH:POOL_WIDTH + ATTN_WIDTH, :])
    zc = _dot(yc_ref[...].astype(BF16), wb_ref[POOL_WIDTH + ATTN_WIDTH:, :])
    g0, g1, g2 = (jax.nn.sigmoid(g_ref[:, j * d:(j + 1) * d].astype(F32)) for j in range(3))
    merged = g0 * za + g1 * zb + g2 * zc
    y = _dot(merged.astype(BF16), wo_ref[...])
    z = alpha * x_ref[...] + (1.0 + gate) * y
    o_ref[...] = _layer_norm(z, lng_ref[1:2, :], lnb_ref[1:2, :])


def _merge_call(x, mod, a, yb, yc, gates, pw_bd, ps, wb, wo, ln_g, ln_b, *, tm, alpha):
    s, d = x.shape
    kern = functools.partial(_merge_kernel, tm=tm, alpha=alpha)
    full = lambda arr: pl.BlockSpec(arr.shape, lambda i: (0, 0))
    hb = tm // _POOL_HALO
    return pl.pallas_call(
        kern,
        grid=(s // tm,),
        in_specs=[pl.BlockSpec((tm, d), lambda i: (i, 0)),
                  full(mod),
                  pl.BlockSpec((tm, POOL_WIDTH), lambda i: (i, 0)),
                  pl.BlockSpec((_POOL_HALO, POOL_WIDTH), lambda i: (jnp.maximum(i * hb - 1, 0), 0)),
                  pl.BlockSpec((tm, ATTN_WIDTH), lambda i: (i, 0)),
                  pl.BlockSpec((tm, HGRN_WIDTH), lambda i: (i, 0)),
                  pl.BlockSpec((tm, 3 * d), lambda i: (i, 0)),
                  full(pw_bd), full(ps), full(wb), full(wo), full(ln_g), full(ln_b)],
        out_specs=pl.BlockSpec((tm, d), lambda i: (i, 0)),
        out_shape=jax.ShapeDtypeStruct((s, d), F32),
        compiler_params=_cparams("arbitrary"),
        name="pool_merge",
    )(x, mod, a, a, yb, yc, gates, pw_bd, ps, wb, wo, ln_g, ln_b)


def _pool_blockdiag(pool_w):
    n = pool_w.shape[0]
    rows = [jnp.concatenate([pool_w[g] if g == j else jnp.zeros_like(pool_w[g]) for j in range(n)], axis=1)
            for g in range(n)]
    return jnp.concatenate(rows, axis=0)


def kernel(x, c, w_ada, b_ada, ln_g, ln_b, ffn_w_in, ffn_w_out, mix_w_in, pool_w, pool_scale, rel_bias,
           hgrn_lb, hgrn_norm_g, w_branch, w_out):
    batch, s, d = x.shape
    assert batch == 1, "kernels are written for a single sequence"
    depth = w_ada.shape[0]
    alpha = (2 * depth) ** 0.25
    tm_ffn = min(512, s)
    tf = 256
    tm_proj = min(256, s)
    tq = min(256, s)
    tm_merge = min(256, s)
    hgrn_rows = min(256, s)

    mods = _ada_call(c, w_ada, b_ada)
    xs = x[0]
    for l in range(depth):
        mod = mods[l].reshape(3 * N_SUB, d)
        xs = _ffn_call(xs, mod, ffn_w_in[l, 0].astype(BF16), ffn_w_out[l, 0].astype(BF16), ln_g[l], ln_b[l],
                       sub=0, alpha=alpha, tm=tm_ffn, tf=tf)
        gates, a, h4, qi, q, k, v, ki2, wi = _proj_call(xs, mod, _proj_weight(mix_w_in[l]), tm=tm_proj)
        yb = _dsa_call(rel_bias, qi, wi, q, ki2, k, v, tq=tq)
        yc = _hgrn_call(h4, hgrn_lb, hgrn_norm_g[l].reshape(1, -1), layer=l, rows=hgrn_rows)
        xs = _merge_call(xs, mod, a, yb, yc, gates, _pool_blockdiag(pool_w[l]).astype(BF16),
                         pool_scale[l].reshape(1, -1), w_branch[l].astype(BF16), w_out[l].astype(BF16),
                         ln_g[l], ln_b[l], tm=tm_merge, alpha=alpha)
        xs = _ffn_call(xs, mod, ffn_w_in[l, 1].astype(BF16), ffn_w_out[l, 1].astype(BF16), ln_g[l], ln_b[l],
                       sub=2, alpha=alpha, tm=tm_ffn, tf=tf)
    return xs[None]
```

```python
import functools
import math

import jax
import jax.numpy as jnp
from jax import lax
from jax.experimental import pallas as pl
from jax.experimental.pallas import tpu as pltpu

F32 = jnp.float32
BF16 = jnp.bfloat16
I32 = jnp.int32

POOL_WINDOWS = (2, 4, 8, 16)
POOL_GROUP = 64
POOL_WIDTH = POOL_GROUP * len(POOL_WINDOWS)
ATTN_HEADS = 4
ATTN_HEAD_DIM = 64
ATTN_WIDTH = ATTN_HEADS * ATTN_HEAD_DIM
IDX_HEADS = 8
IDX_DIM = 64
TOPK_MAX = 256
HGRN_HEADS = 4
HGRN_DIM = 128
HGRN_WIDTH = HGRN_HEADS * HGRN_DIM
KEY_MAX = 1.0 - 1e-6
REL_BUCKETS = 32
REL_MAX_DIST = 128
N_SUB = 3
LN_EPS = 1e-5
RMS_EPS = 1e-6

LANES = 128
SUBLANES = 8
VMEM_LIMIT_BYTES = 58 * 1024 * 1024

NEG_INF = float("-inf")
POS_INF = float("inf")


def _cparams(*sem):
    return pltpu.CompilerParams(dimension_semantics=sem, vmem_limit_bytes=VMEM_LIMIT_BYTES)


def _dot(a, b):
    return jnp.dot(a, b, preferred_element_type=F32)


def _dot_nt(a, b):
    return lax.dot_general(a, b, (((1,), (1,)), ((), ())), preferred_element_type=F32)


def _dot_tn(a, b):
    return lax.dot_general(a, b, (((0,), (0,)), ((), ())), preferred_element_type=F32)


def _log2(n):
    assert n & (n - 1) == 0
    return n.bit_length() - 1


def _silu(x):
    return x * jax.nn.sigmoid(x)


def _mod_rows(mod_ref, j):
    return mod_ref[3 * j:3 * j + 1, :], mod_ref[3 * j + 1:3 * j + 2, :], mod_ref[3 * j + 2:3 * j + 3, :]


def _layer_norm(z, g, b):
    mu = jnp.mean(z, axis=-1, keepdims=True)
    zc = z - mu
    var = jnp.mean(zc * zc, axis=-1, keepdims=True)
    return zc * lax.rsqrt(var + LN_EPS) * g + b


def _ada_kernel(c_ref, w_ref, b_ref, o_ref):
    @pl.when(pl.program_id(1) == 0)
    def _():
        o_ref[0] = b_ref[0]

    c = c_ref[...]
    o_ref[0] += jnp.sum(_silu(c) * w_ref[0], axis=0, keepdims=True)


def _ada_call(c, w_ada, b_ada):
    depth, d, n = w_ada.shape
    rows = LANES
    return pl.pallas_call(
        _ada_kernel,
        grid=(depth, d // rows),
        in_specs=[pl.BlockSpec((rows, 1), lambda l, k: (k, 0)),
                  pl.BlockSpec((1, rows, n), lambda l, k: (l, k, 0)),
                  pl.BlockSpec((1, 1, n), lambda l, k: (l, 0, 0))],
        out_specs=pl.BlockSpec((1, 1, n), lambda l, k: (l, 0, 0)),
        out_shape=jax.ShapeDtypeStruct((depth, 1, n), F32),
        compiler_params=_cparams("arbitrary", "arbitrary"),
        name="ada_mod",
    )(c.reshape(d, 1), w_ada, b_ada.reshape(depth, 1, n))


def _ffn_rows(x, mod_ref, wi_ref, wo_ref, lng_ref, lnb_ref, *, sub, tf, alpha):
    shift, scale, gate = _mod_rows(mod_ref, sub)
    ff = wo_ref.shape[0]
    h = (x * (1.0 + scale) + shift).astype(BF16)
    y = jnp.zeros(x.shape, F32)
    for c0 in range(0, ff, tf):
        g = _dot(h, wi_ref[:, c0:c0 + tf])
        u = _dot(h, wi_ref[:, ff + c0:ff + c0 + tf])
        y = y + _dot((_silu(g) * u).astype(BF16), wo_ref[c0:c0 + tf, :])
    z = alpha * x + (0.5 * (1.0 + gate)) * y
    return _layer_norm(z, lng_ref[sub:sub + 1, :], lnb_ref[sub:sub + 1, :])


def _ffn_kernel(x_ref, mod_ref, wi_ref, wo_ref, lng_ref, lnb_ref, o_ref, *, sub, tf, alpha):
    o_ref[...] = _ffn_rows(x_ref[...], mod_ref, wi_ref, wo_ref, lng_ref, lnb_ref, sub=sub, tf=tf, alpha=alpha)


def _ffn_call(x, mod, w_in, w_out, ln_g, ln_b, *, sub, alpha, tm, tf):
    s, d = x.shape
    ff = w_out.shape[0]
    assert ff % tf == 0
    kern = functools.partial(_ffn_kernel, sub=sub, tf=tf, alpha=alpha)
    resident = lambda arr: pl.BlockSpec(arr.shape, lambda i: (0, 0), pipeline_mode=pl.Buffered(1))
    return pl.pallas_call(
        kern,
        grid=(s // tm,),
        in_specs=[pl.BlockSpec((tm, d), lambda i: (i, 0)),
                  pl.BlockSpec((3 * N_SUB, d), lambda i: (0, 0)),
                  resident(w_in), resident(w_out),
                  pl.BlockSpec((N_SUB, d), lambda i: (0, 0)),
                  pl.BlockSpec((N_SUB, d), lambda i: (0, 0))],
        out_specs=pl.BlockSpec((tm, d), lambda i: (i, 0)),
        out_shape=jax.ShapeDtypeStruct((s, d), F32),
        compiler_params=_cparams("arbitrary"),
        name="ffn",
    )(x, mod, w_in, w_out, ln_g, ln_b)


_PROJ_OUTS = (("gates", 3 * 1024, BF16), ("a", POOL_WIDTH, F32), ("h4", 4 * HGRN_WIDTH, F32),
              ("qi", IDX_HEADS * IDX_DIM, BF16), ("q", ATTN_WIDTH, BF16), ("k", ATTN_WIDTH, BF16),
              ("v", ATTN_WIDTH, BF16), ("ki2", 2 * IDX_DIM, BF16), ("wi", LANES, F32))
_PROJ_CHUNK = 1024


def _proj_kernel(x_ref, mod_ref, w_ref, *o_refs):
    shift, scale, _ = _mod_rows(mod_ref, 1)
    h = (x_ref[...] * (1.0 + scale) + shift).astype(BF16)
    c0 = 0
    for (_, width, dt), o_ref in zip(_PROJ_OUTS, o_refs):
        for cc in range(0, width, _PROJ_CHUNK):
            cw = min(_PROJ_CHUNK, width - cc)
            o_ref[:, cc:cc + cw] = _dot(h, w_ref[:, c0 + cc:c0 + cc + cw]).astype(dt)
        c0 += width


def _proj_weight(w):
    d = w.shape[0]
    widths = (POOL_WIDTH, ATTN_WIDTH, ATTN_WIDTH, ATTN_WIDTH, IDX_HEADS * IDX_DIM, IDX_DIM, IDX_HEADS,
              HGRN_WIDTH, HGRN_WIDTH, HGRN_WIDTH, HGRN_WIDTH, 3 * d)
    parts, c0 = [], 0
    for wd in widths:
        parts.append(w[:, c0:c0 + wd])
        c0 += wd
    a, q, k, v, qi, ki, wi, hq, hf, hi, hg, gates = parts
    wi_pad = jnp.pad(wi, ((0, 0), (0, LANES - IDX_HEADS)))
    q = q * (ATTN_HEAD_DIM ** -0.5 * _LOG2E)
    return jnp.concatenate([gates, a, hq, hf, hi, hg, qi, q, k, v, ki, ki, wi_pad], axis=1).astype(BF16)


def _proj_call(x, mod, w, *, tm):
    s, d = x.shape
    n = w.shape[1]
    assert n == sum(wd for _, wd, _ in _PROJ_OUTS)
    return pl.pallas_call(
        _proj_kernel,
        grid=(s // tm,),
        in_specs=[pl.BlockSpec((tm, d), lambda i: (i, 0)),
                  pl.BlockSpec((3 * N_SUB, d), lambda i: (0, 0)),
                  pl.BlockSpec((d, n), lambda i: (0, 0))],
        out_specs=[pl.BlockSpec((tm, wd), lambda i: (i, 0)) for _, wd, _ in _PROJ_OUTS],
        out_shape=[jax.ShapeDtypeStruct((s, wd), dt) for _, wd, dt in _PROJ_OUTS],
        compiler_params=_cparams("arbitrary"),
        name="mix_proj",
    )(x, mod, w)


_BISECT_MAX_ITERS = 320
_F32_TINY = float(jnp.finfo(jnp.float32).tiny)
_COARSE_PASSES = 9
_COARSE_MIN_ABS, _COARSE_MAX_ABS = 1e-30, 1e30
_KIND_FAR, _KIND_PAD, _N_KINDS = 2, 3, 4
_VT_ROWS = LANES + 16
_SMASK_FULL, _SMASK_DIAG, _SMASK_PAD = 0, 1, 2
_LOG2E = math.log2(math.e)


def _t5_bucket(dist):
    max_exact = REL_BUCKETS // 2
    d32 = jnp.maximum(dist, 1).astype(F32)
    large = max_exact + (jnp.log(d32 / max_exact) / math.log(REL_MAX_DIST / max_exact)
                         * (REL_BUCKETS - max_exact)).astype(I32)
    large = jnp.minimum(large, REL_BUCKETS - 1)
    return jnp.where(dist < max_exact, dist, large)


def _dsa_kernel(relb_ref, qi_ref, wi_ref, q_ref, ki2_ref, k_ref, vt_ref, o_ref,
                s_sc, sb_sc, bias_sc, smask_sc, lg_sc, qim_sc, qm_sc, m_sc, acc_sc,
                *, tq, topk, idx_scale):
    tk = tq
    i = pl.program_id(0)
    krow = lax.broadcasted_iota(I32, (tk, tq), 0)
    qcol = lax.broadcasted_iota(I32, (tk, tq), 1)
    causal = krow <= qcol
    lane = lax.broadcasted_iota(I32, (tq, LANES), 1)
    lo_half = lane < ATTN_HEAD_DIM

    @pl.when(i == 0)
    def _():
        for which in range(2):
            bucket = _t5_bucket(jnp.maximum(qcol - krow + which * tq, 0))
            for h in range(ATTN_HEADS):
                b = jnp.zeros((tk, tq), F32)
                for bk in range(REL_BUCKETS):
                    b = jnp.where(bucket == bk, relb_ref[bk, h] * _LOG2E, b)
                bias_sc[h, which] = (jnp.where(causal, b, NEG_INF) if which == 0 else b).astype(BF16)
        for h in range(ATTN_HEADS):
            bias_sc[h, _KIND_FAR] = jnp.full((tk, tq), relb_ref[REL_BUCKETS - 1, h] * _LOG2E, F32).astype(BF16)
            bias_sc[h, _KIND_PAD] = jnp.full((tk, tq), NEG_INF, BF16)
        smask_sc[_SMASK_FULL] = jnp.zeros((tk, tq), F32)
        smask_sc[_SMASK_DIAG] = jnp.where(causal, 0.0, NEG_INF)
        smask_sc[_SMASK_PAD] = jnp.full((tk, tq), NEG_INF, F32)

    def head_operands(src_ref, dst_sc, n_heads):
        src = src_ref[...].astype(F32)
        for h in range(n_heads):
            pair = src[:, LANES * (h // 2):LANES * (h // 2 + 1)]
            pair = jnp.where(lo_half if h % 2 == 0 else ~lo_half, pair, 0.0)
            dst_sc[h] = jnp.transpose(pair).astype(BF16)

    head_operands(qi_ref, qim_sc, IDX_HEADS)
    head_operands(q_ref, qm_sc, ATTN_HEADS)

    wt = jnp.transpose(wi_ref[...])

    def key_rows(jb):
        return pl.ds(pl.multiple_of(jb * tk, tk), tk)

    def fold(x):
        return x.reshape(tk // SUBLANES, SUBLANES, tq)

    def index_scores(jb):
        kb = ki2_ref[key_rows(jb), :]
        acc = jnp.zeros((tk, tq), F32)
        for h in range(IDX_HEADS):
            acc = acc + wt[h:h + 1, :] * jnp.maximum(_dot(kb, qim_sc[h]), 0.0)
        return acc * idx_scale

    npair = (i + 2) // 2
    last_kv = k_ref.shape[0] // tk - 1

    def p1_pair(j, carry):
        rmin, rmax = carry
        for jb in (2 * j, 2 * j + 1):
            raw = index_scores(jnp.minimum(jb, last_kv))
            kind = jnp.where(jb > i, _SMASK_PAD, jnp.where(jb == i, _SMASK_DIAG, _SMASK_FULL))
            sc = raw + smask_sc[kind]
            s_sc[jb] = sc
            sb_sc[jb] = sc.astype(BF16)
            rmin = jnp.minimum(rmin, jnp.min(fold(raw), axis=0))
            rmax = jnp.maximum(rmax, jnp.max(fold(raw), axis=0))
        return rmin, rmax

    rmin, rmax = lax.fori_loop(0, npair, p1_pair, (jnp.full((SUBLANES, tq), POS_INF, F32),
                                                   jnp.full((SUBLANES, tq), NEG_INF, F32)))
    rowmin = jnp.min(rmin, axis=0, keepdims=True)
    rowmax = jnp.max(rmax, axis=0, keepdims=True)

    kf = float(topk)
    nfin = (i * tq + lax.broadcasted_iota(I32, (1, tq), 1) + 1).astype(F32)
    small = nfin <= kf

    def walk_blocks(blocks_fn, init):
        return lax.fori_loop(0, npair, lambda j, c: blocks_fn([2 * j, 2 * j + 1], c), init)

    def count_ge(x):
        def blocks_fn(jbs, cnt):
            for jb in jbs:
                cnt = cnt + jnp.sum(fold(jnp.where(s_sc[jb] >= x, 1.0, 0.0)), axis=0)
            return cnt
        return jnp.sum(walk_blocks(blocks_fn, jnp.zeros((SUBLANES, tq), F32)), axis=0, keepdims=True)

    def count_ge16(xb):
        one, zero = jnp.ones((), BF16), jnp.zeros((), BF16)
        rows16 = 2 * SUBLANES

        def blocks_fn(jbs, cnt):
            parts = []
            for jb in jbs:
                c = jnp.where(sb_sc[jb] >= xb, one, zero)
                parts += [c[r0:r0 + rows16, :] for r0 in range(0, tk, rows16)]
            while len(parts) > 1:
                parts = [a + b for a, b in zip(parts[0::2], parts[1::2])]
            return cnt + parts[0].astype(F32)
        return jnp.sum(walk_blocks(blocks_fn, jnp.zeros((rows16, tq), F32)), axis=0, keepdims=True)

    def coarse_probe(mid):
        xb = mid.astype(BF16)
        xbits = lax.bitcast_convert_type(xb.astype(F32), I32)
        neg = xbits < 0
        mu = jnp.where(neg, xbits + 0x8000, xbits - 0x8000)
        odd = (xbits & 0x10000) != 0
        tbits = jnp.where(odd, jnp.where(neg, mu - 1, mu + 1), mu)
        t = lax.bitcast_convert_type(tbits, F32)
        ok = jnp.logical_and(jnp.abs(xb.astype(F32)) > _COARSE_MIN_ABS, jnp.abs(xb.astype(F32)) < _COARSE_MAX_ABS)
        return xb, t, ok

    def narrow(st, probe, c, active):
        lo, hi, clo, chi, done = st
        up = jnp.logical_and(active, c >= kf)
        dn = jnp.logical_and(active, c < kf)
        return (jnp.where(up, probe, lo), jnp.where(dn, probe, hi), jnp.where(up, c, clo),
                jnp.where(dn, c, chi), jnp.where(jnp.logical_and(active, c == kf), 1.0, done))

    def coarse_body(_, st):
        lo, hi, _, _, done = st
        xb, t, ok = coarse_probe(0.5 * lo + 0.5 * hi)
        ok = jnp.logical_and(ok, jnp.logical_and(t > lo, t < hi))
        return narrow(st, t, count_ge16(xb), jnp.logical_and(done < 0.5, ok))

    def fine_cond(st):
        it, (_, _, _, _, done) = st
        return jnp.logical_and(it < _BISECT_MAX_ITERS, jnp.min(done) < 0.5)

    def fine_body(st):
        it, inner = st
        lo, hi, _, _, done = inner
        mid = 0.5 * lo + 0.5 * hi
        mid = jnp.where(jnp.logical_and(lo == 0.0, hi > _F32_TINY), _F32_TINY, mid)
        mid = jnp.where(jnp.logical_and(lo < 0.0, hi > 0.0), 0.0, mid)
        stuck = jnp.logical_or(jnp.logical_or(mid <= lo, mid >= hi),
                               jnp.logical_and(lo == 0.0, hi <= _F32_TINY))
        lo, hi, clo, chi, done = narrow(inner, mid, count_ge(mid),
                                        jnp.logical_and(done < 0.5, jnp.logical_not(stuck)))
        return it + 1, (lo, hi, clo, chi, jnp.where(stuck, 1.0, done))

    mbits = lax.bitcast_convert_type(rowmax, I32)
    above_max = jnp.where(rowmax == 0.0, _F32_TINY,
                          lax.bitcast_convert_type(jnp.where(rowmax > 0.0, mbits + 1, mbits - 1), F32))
    st0 = (jnp.where(small, NEG_INF, rowmin), above_max, nfin, jnp.zeros((1, tq), F32),
           jnp.where(small, 1.0, 0.0))
    st1 = lax.fori_loop(0, _COARSE_PASSES, coarse_body, st0)
    _, (lo, hi, clo, chi, _) = lax.while_loop(fine_cond, fine_body, (jnp.int32(0), st1))
    budget = kf - chi
    need_ties = jnp.max((clo - chi) - budget) > 0.5

    pairs = [slice(LANES * (h // 2), LANES * (h // 2 + 1)) for h in range(ATTN_HEADS)]

    def issue_qk(jb, slot):
        kb = k_ref[key_rows(jnp.minimum(jb, last_kv)), :]
        for h in range(ATTN_HEADS):
            lg_sc[slot, h] = _dot(kb[:, pairs[h]], qm_sc[h]).astype(BF16)

    def attend(jb, slot, thr, cum=None):
        s = s_sc[jb]
        if cum is None:
            sel = s >= thr
        else:
            cand = jnp.logical_and(s >= lo, s < hi)
            before = (qcol < krow).astype(BF16)
            rank = cum + _dot(before, jnp.where(cand, 1.0, 0.0).astype(BF16))
            sel = jnp.logical_or(s >= hi, jnp.logical_and(cand, rank < budget))
        kind = jnp.where(jb > i, _KIND_PAD, jnp.minimum(i - jb, _KIND_FAR))
        selm = jnp.where(sel, 0.0, NEG_INF).astype(BF16)
        probs, alphas = [], []
        for h in range(ATTN_HEADS):
            lg = lg_sc[slot, h] + bias_sc[h, kind] + selm
            m_old = m_sc[h]
            m_new = jnp.maximum(m_old, jnp.max(lg, axis=0, keepdims=True).astype(F32))
            m_safe = jnp.where(m_new == NEG_INF, 0.0, m_new)
            alphas.append(jnp.exp2(m_old - m_safe))
            probs.append(jnp.exp2(lg - m_safe.astype(BF16)))
            m_sc[h] = m_new
        vb = vt_ref[jnp.minimum(jb, last_kv)]
        for h in range(ATTN_HEADS):
            vrows = slice(_VT_ROWS * (h // 2), _VT_ROWS * (h // 2 + 1))
            acc_sc[h] = alphas[h] * acc_sc[h] + _dot(vb[vrows, :], probs[h])

    def run(with_ties):
        m_sc[...] = jnp.full(m_sc.shape, NEG_INF, F32)
        acc_sc[...] = jnp.zeros_like(acc_sc)
        issue_qk(0, 0)

        def pair(j, thr0, thr1, cum0=None, cum1=None):
            issue_qk(2 * j + 1, 1)
            attend(2 * j, 0, thr0, cum0)
            issue_qk(2 * j + 2, 0)
            attend(2 * j + 1, 1, thr1, cum1)

        if not with_ties:
            def pair_body(j, carry):
                pair(j, lo, lo)
                return carry
            lax.fori_loop(0, npair, pair_body, 0)
            return

        last_s = s_sc.shape[0] - 1

        def cand_count(jb):
            s = s_sc[jnp.minimum(jb, last_s)]
            cand = jnp.logical_and(s >= lo, s < hi)
            return jnp.sum(jnp.where(cand, 1.0, 0.0), axis=0, keepdims=True)

        def needs_ranking(cum0, ca, cb):
            cum1 = cum0 + ca
            runs_out = jnp.logical_or(jnp.logical_and(cum1 > budget, cum0 < budget),
                                      jnp.logical_and(cum1 + cb > budget, cum1 < budget))
            return (jnp.max(jnp.where(runs_out, 1.0, 0.0)) > 0.5).astype(I32)

        def tie_pair_body(j, carry):
            ranked, cum0, ca, cb = carry
            cum1 = cum0 + ca
            cum2 = cum1 + cb

            def variant(with_rank):
                def fn():
                    na, nb = cand_count(2 * j + 2), cand_count(2 * j + 3)
                    nxt = needs_ranking(cum2, na, nb)
                    if with_rank:
                        pair(j, None, None, cum0, cum1)
                    else:
                        pair(j, jnp.where(cum1 <= budget, lo, hi), jnp.where(cum2 <= budget, lo, hi))
                    return nxt, cum2, na, nb
                return fn
            return lax.cond(ranked == 1, variant(True), variant(False))

        c0, c1 = cand_count(0), cand_count(1)
        zero = jnp.zeros((1, tq), F32)
        lax.fori_loop(0, npair, tie_pair_body, (needs_ranking(zero, c0, c1), zero, c0, c1))

    @pl.when(need_ties)
    def _():
        run(True)

    @pl.when(jnp.logical_not(need_ties))
    def _():
        run(False)

    first_head = lax.broadcasted_iota(I32, (LANES, tq), 0) < ATTN_HEAD_DIM
    normed = [acc_sc[h, 0:LANES, :] / acc_sc[h, LANES:LANES + 1, :] for h in range(ATTN_HEADS)]
    for p in range(ATTN_HEADS // 2):
        pair_t = jnp.where(first_head, normed[2 * p], normed[2 * p + 1])
        o_ref[:, LANES * p:LANES * (p + 1)] = jnp.transpose(pair_t)


def _dsa_call(rel_bias, qi, wi, q, ki2, k, v, *, tq):
    s = q.shape[0]
    assert tq > REL_MAX_DIST and s % tq == 0
    nb = s // tq
    topk = min(TOPK_MAX, s // 4)
    kern = functools.partial(_dsa_kernel, tq=tq, topk=topk,
                             idx_scale=(IDX_DIM ** -0.5) * (IDX_HEADS ** -0.5))
    whole = lambda width: pl.BlockSpec((s, width), lambda i: (0, 0), pipeline_mode=pl.Buffered(1))
    n_pairs = ATTN_WIDTH // LANES
    vt = jnp.transpose(v.reshape(nb, tq, n_pairs, LANES), (0, 2, 3, 1))
    vt = jnp.concatenate([vt, jnp.ones((nb, n_pairs, _VT_ROWS - LANES, tq), BF16)], axis=2)
    vt = vt.reshape(nb, n_pairs * _VT_ROWS, tq)
    return pl.pallas_call(
        kern,
        grid=(nb,),
        in_specs=[pl.BlockSpec(memory_space=pltpu.SMEM),
                  pl.BlockSpec((tq, IDX_HEADS * IDX_DIM), lambda i: (i, 0)),
                  pl.BlockSpec((tq, LANES), lambda i: (i, 0)),
                  pl.BlockSpec((tq, ATTN_WIDTH), lambda i: (i, 0)),
                  whole(2 * IDX_DIM), whole(ATTN_WIDTH),
                  pl.BlockSpec(vt.shape, lambda i: (0, 0, 0), pipeline_mode=pl.Buffered(1))],
        out_specs=pl.BlockSpec((tq, ATTN_WIDTH), lambda i: (i, 0)),
        out_shape=jax.ShapeDtypeStruct((s, ATTN_WIDTH), F32),
        scratch_shapes=[pltpu.VMEM((nb + nb % 2, tq, tq), F32),
                        pltpu.VMEM((nb + nb % 2, tq, tq), BF16),
                        pltpu.VMEM((ATTN_HEADS, _N_KINDS, tq, tq), BF16),
                        pltpu.VMEM((3, tq, tq), F32),
                        pltpu.VMEM((2, ATTN_HEADS, tq, tq), BF16),
                        pltpu.VMEM((IDX_HEADS, LANES, tq), BF16),
                        pltpu.VMEM((ATTN_HEADS, LANES, tq), BF16),
                        pltpu.VMEM((ATTN_HEADS, 1, tq), F32),
                        pltpu.VMEM((ATTN_HEADS, _VT_ROWS, tq), F32)],
        compiler_params=_cparams("arbitrary"),
        name="dsa_attention",
    )(rel_bias, qi, wi, q, ki2, k, vt)


_HGRN_CHUNK = 128
_HGRN_BASE = SUBLANES


def _hgrn_kernel(h4_ref, lb_ref, ng_ref, o_ref, st_sc, *, layer, rows):
    c_len = _HGRN_CHUNK

    @pl.when(pl.program_id(0) == 0)
    def _():
        st_sc[...] = jnp.zeros_like(st_sc)

    lb_all = lb_ref[...]
    e = jnp.exp(lb_all - jnp.max(lb_all, axis=0, keepdims=True))
    sm = e / jnp.sum(e, axis=0, keepdims=True)
    lbs = jnp.sum(sm[0:layer + 1, :], axis=0, keepdims=True) - sm[0:1, :]

    r1 = lax.broadcasted_iota(I32, (c_len, 1), 0)
    rr = lax.broadcasted_iota(I32, (c_len, c_len), 0)
    cc = lax.broadcasted_iota(I32, (c_len, c_len), 1)

    def block_ref(a, blk, r):
        x = a.reshape(c_len // blk, blk, a.shape[-1])[:, r:r + 1, :]
        return jnp.broadcast_to(x, (c_len // blk, blk, a.shape[-1])).reshape(c_len, a.shape[-1])

    w = HGRN_WIDTH
    for c in range(rows // c_len):
        rs = slice(c * c_len, (c + 1) * c_len)
        hq = h4_ref[rs, 0:w]
        z = h4_ref[rs, w:2 * w]
        val = h4_ref[rs, 2 * w:3 * w]
        hg = h4_ref[rs, 3 * w:4 * w]
        key = (1.0 - lbs) * jax.nn.sigmoid(-z)
        g = jnp.log1p(-jnp.minimum(key, KEY_MAX))
        a_cum = g
        sh = 1
        while sh < c_len:
            a_cum = a_cum + jnp.where(r1 >= sh, pltpu.roll(a_cum, sh, 0), 0.0)
            sh *= 2
        qf = _silu(hq)
        outs = []
        for h in range(HGRN_HEADS):
            hs = slice(h * HGRN_DIM, (h + 1) * HGRN_DIM)
            ah, qh, kh, vh = a_cum[:, hs], qf[:, hs], key[:, hs], val[:, hs]
            vh16 = vh.astype(BF16)
            st = st_sc[h]
            o = _dot_nt((qh * jnp.exp(ah)).astype(BF16), st.astype(BF16))
            ref = block_ref(ah, _HGRN_BASE, _HGRN_BASE // 2 - 1)
            sc = _dot_nt((qh * jnp.exp(ah - ref)).astype(BF16), (kh * jnp.exp(ref - ah)).astype(BF16))
            same = (rr >> _log2(_HGRN_BASE)) == (cc >> _log2(_HGRN_BASE))
            scores = jnp.where(jnp.logical_and(same, cc <= rr), sc, 0.0)
            half = _HGRN_BASE
            while half < c_len:
                blk = 2 * half
                ref = block_ref(ah, blk, half - 1)
                second = (r1 & (blk - 1)) >= half
                ql = jnp.where(second, qh * jnp.exp(jnp.minimum(ah - ref, 0.0)), 0.0)
                kl = jnp.where(second, 0.0, kh * jnp.exp(jnp.minimum(ref - ah, 0.0)))
                sc = _dot_nt(ql.astype(BF16), kl.astype(BF16))
                scores = scores + jnp.where((rr >> _log2(blk)) == (cc >> _log2(blk)), sc, 0.0)
                half = blk
            o = o + _dot(scores.astype(BF16), vh16)
            a_last = ah[c_len - 1:c_len, :]
            kd = (kh * jnp.exp(a_last - ah)).astype(BF16)
            st_sc[h] = st * jnp.exp(a_last) + _dot_tn(vh16, kd)
            o = o * lax.rsqrt(jnp.mean(o * o, axis=-1, keepdims=True) + RMS_EPS)
            outs.append(o)
        o_ref[rs, :] = jnp.concatenate(outs, axis=1) * ng_ref[...] * _silu(hg)


def _hgrn_call(h4, lb, ng, *, layer, rows):
    s = h4.shape[0]
    kern = functools.partial(_hgrn_kernel, layer=layer, rows=rows)
    return pl.pallas_call(
        kern,
        grid=(s // rows,),
        in_specs=[pl.BlockSpec((rows, 4 * HGRN_WIDTH), lambda i: (i, 0)),
                  pl.BlockSpec(lb.shape, lambda i: (0, 0)),
                  pl.BlockSpec((1, HGRN_WIDTH), lambda i: (0, 0))],
        out_specs=pl.BlockSpec((rows, HGRN_WIDTH), lambda i: (i, 0)),
        out_shape=jax.ShapeDtypeStruct((s, HGRN_WIDTH), F32),
        scratch_shapes=[pltpu.VMEM((HGRN_HEADS, HGRN_DIM, HGRN_DIM), F32)],
        compiler_params=_cparams("arbitrary"),
        name="hgrn2",
    )(h4, lb, ng)


_POOL_HALO = 16


def _merge_kernel(x_ref, mod_ref, a_ref, ah_ref, yb_ref, yc_ref, g_ref, pw_ref, ps_ref, wb_ref, wo_ref,
                  lng_ref, lnb_ref, fwi_ref, fwo_ref, o_ref, *, tm, tf, alpha):
    i = pl.program_id(0)
    d = x_ref.shape[1]
    _, _, gate = _mod_rows(mod_ref, 1)
    a = a_ref[...]
    halo = jnp.where(i == 0, 0.0, ah_ref[...])
    ext = jnp.concatenate([halo, a], axis=0)
    lane = lax.broadcasted_iota(I32, (tm, POOL_WIDTH), 1)
    grp = lane >> _log2(POOL_GROUP)
    pos1 = (i * tm + lax.broadcasted_iota(I32, (tm, POOL_WIDTH), 0) + 1).astype(F32)
    win = jnp.zeros((tm, POOL_WIDTH), F32)
    cnt = jnp.zeros((tm, POOL_WIDTH), F32)
    ssum, have = ext, 1
    for gi, wd in enumerate(POOL_WINDOWS):
        while have < wd:
            ssum = ssum + pltpu.roll(ssum, have, 0)
            have *= 2
        assert have == wd
        win = jnp.where(grp == gi, ssum[_POOL_HALO:, :], win)
        cnt = jnp.where(grp == gi, jnp.minimum(pos1, float(wd)), cnt)
    dlt = (win / cnt - a).astype(BF16)
    ya = _dot(dlt, pw_ref[...]) * ps_ref[...]
    za = _dot(ya.astype(BF16), wb_ref[0:POOL_WIDTH, :])
    zb = _dot(yb_ref[...].astype(BF16), wb_ref[POOL_WIDTH:POOL_WIDTH + ATTN_WIDTH, :])
    zc = _dot(yc_ref[...].astype(BF16), wb_ref[POOL_WIDTH + ATTN_WIDTH:, :])
    g0, g1, g2 = (jax.nn.sigmoid(g_ref[:, j * d:(j + 1) * d].astype(F32)) for j in range(3))
    merged = g0 * za + g1 * zb + g2 * zc
    y = _dot(merged.astype(BF16), wo_ref[...])
    z = alpha * x_ref[...] + (1.0 + gate) * y
    xm = _layer_norm(z, lng_ref[1:2, :], lnb_ref[1:2, :])
    o_ref[...] = _ffn_rows(xm, mod_ref, fwi_ref, fwo_ref, lng_ref, lnb_ref, sub=2, tf=tf, alpha=alpha)


def _merge_call(x, mod, a, yb, yc, gates, pw_bd, ps, wb, wo, ln_g, ln_b, ffn_wi, ffn_wo, *, tm, tf, alpha):
    s, d = x.shape
    kern = functools.partial(_merge_kernel, tm=tm, tf=tf, alpha=alpha)
    full = lambda arr: pl.BlockSpec(arr.shape, lambda i: (0, 0), pipeline_mode=pl.Buffered(1))
    hb = tm // _POOL_HALO
    return pl.pallas_call(
        kern,
        grid=(s // tm,),
        in_specs=[pl.BlockSpec((tm, d), lambda i: (i, 0)),
                  full(mod),
                  pl.BlockSpec((tm, POOL_WIDTH), lambda i: (i, 0)),
                  pl.BlockSpec((_POOL_HALO, POOL_WIDTH), lambda i: (jnp.maximum(i * hb - 1, 0), 0)),
                  pl.BlockSpec((tm, ATTN_WIDTH), lambda i: (i, 0)),
                  pl.BlockSpec((tm, HGRN_WIDTH), lambda i: (i, 0)),
                  pl.BlockSpec((tm, 3 * d), lambda i: (i, 0)),
                  full(pw_bd), full(ps), full(wb), full(wo), full(ln_g), full(ln_b), full(ffn_wi), full(ffn_wo)],
        out_specs=pl.BlockSpec((tm, d), lambda i: (i, 0)),
        out_shape=jax.ShapeDtypeStruct((s, d), F32),
        compiler_params=_cparams("arbitrary"),
        name="merge_ffn",
    )(x, mod, a, a, yb, yc, gates, pw_bd, ps, wb, wo, ln_g, ln_b, ffn_wi, ffn_wo)


def _pool_blockdiag(pool_w):
    n = pool_w.shape[0]
    rows = [jnp.concatenate([pool_w[g] if g == j else jnp.zeros_like(pool_w[g]) for j in range(n)], axis=1)
            for g in range(n)]
    return jnp.concatenate(rows, axis=0)


def kernel(x, c, w_ada, b_ada, ln_g, ln_b, ffn_w_in, ffn_w_out, mix_w_in, pool_w, pool_scale, rel_bias,
           hgrn_lb, hgrn_norm_g, w_branch, w_out):
    batch, s, d = x.shape
    assert batch == 1, "kernels are written for a single sequence"
    depth = w_ada.shape[0]
    alpha = (2 * depth) ** 0.25
    tm_ffn = min(512, s)
    tf = 256
    tm_proj = min(256, s)
    tq = min(256, s)
    tm_merge = min(512, s)
    hgrn_rows = min(256, s)

    mods = _ada_call(c, w_ada, b_ada)
    xs = x.reshape(s, d)
    for l in range(depth):
        mod = mods[l].reshape(3 * N_SUB, d)
        xs = _ffn_call(xs, mod, ffn_w_in[l, 0].astype(BF16), ffn_w_out[l, 0].astype(BF16), ln_g[l], ln_b[l],
                       sub=0, alpha=alpha, tm=tm_ffn, tf=tf)
        gates, a, h4, qi, q, k, v, ki2, wi = _proj_call(xs, mod, _proj_weight(mix_w_in[l]), tm=tm_proj)
        yb = _dsa_call(rel_bias, qi, wi, q, ki2, k, v, tq=tq)
        yc = _hgrn_call(h4, hgrn_lb, hgrn_norm_g[l].reshape(1, -1), layer=l, rows=hgrn_rows)
        xs = _merge_call(xs, mod, a, yb, yc, gates, _pool_blockdiag(pool_w[l]).astype(BF16),
                         pool_scale[l].reshape(1, -1), w_branch[l].astype(BF16), w_out[l].astype(BF16),
                         ln_g[l], ln_b[l], ffn_w_in[l, 1].astype(BF16), ffn_w_out[l, 1].astype(BF16),
                         tm=tm_merge, tf=tf, alpha=alpha)
    return xs.reshape(batch, s, d)
```

```python
import functools
import math

import jax
import jax.numpy as jnp
from jax import lax
from jax.experimental import pallas as pl
from jax.experimental.pallas import tpu as pltpu

F32 = jnp.float32
BF16 = jnp.bfloat16
I32 = jnp.int32

POOL_WINDOWS = (2, 4, 8, 16)
POOL_GROUP = 64
POOL_WIDTH = POOL_GROUP * len(POOL_WINDOWS)
ATTN_HEADS = 4
ATTN_HEAD_DIM = 64
ATTN_WIDTH = ATTN_HEADS * ATTN_HEAD_DIM
IDX_HEADS = 8
IDX_DIM = 64
TOPK_MAX = 256
HGRN_HEADS = 4
HGRN_DIM = 128
HGRN_WIDTH = HGRN_HEADS * HGRN_DIM
KEY_MAX = 1.0 - 1e-6
REL_BUCKETS = 32
REL_MAX_DIST = 128
N_SUB = 3
LN_EPS = 1e-5
RMS_EPS = 1e-6

LANES = 128
SUBLANES = 8
VMEM_LIMIT_BYTES = 58 * 1024 * 1024

NEG_INF = float("-inf")
POS_INF = float("inf")


def _cparams(*sem):
    return pltpu.CompilerParams(dimension_semantics=sem, vmem_limit_bytes=VMEM_LIMIT_BYTES)


def _dot(a, b):
    return jnp.dot(a, b, preferred_element_type=F32)


def _dot_nt(a, b):
    return lax.dot_general(a, b, (((1,), (1,)), ((), ())), preferred_element_type=F32)


def _dot_tn(a, b):
    return lax.dot_general(a, b, (((0,), (0,)), ((), ())), preferred_element_type=F32)


def _log2(n):
    assert n & (n - 1) == 0
    return n.bit_length() - 1


def _silu(x):
    return x * jax.nn.sigmoid(x)


def _mod_rows(mod_ref, j):
    return mod_ref[3 * j:3 * j + 1, :], mod_ref[3 * j + 1:3 * j + 2, :], mod_ref[3 * j + 2:3 * j + 3, :]


def _layer_norm(z, g, b):
    mu = jnp.mean(z, axis=-1, keepdims=True)
    zc = z - mu
    var = jnp.mean(zc * zc, axis=-1, keepdims=True)
    return zc * lax.rsqrt(var + LN_EPS) * g + b


def _ada_kernel(c_ref, w_ref, b_ref, o_ref):
    @pl.when(pl.program_id(1) == 0)
    def _():
        o_ref[0] = b_ref[0]

    c = c_ref[...]
    o_ref[0] += jnp.sum(_silu(c) * w_ref[0], axis=0, keepdims=True)


def _ada_call(c, w_ada, b_ada):
    depth, d, n = w_ada.shape
    rows = LANES
    return pl.pallas_call(
        _ada_kernel,
        grid=(depth, d // rows),
        in_specs=[pl.BlockSpec((rows, 1), lambda l, k: (k, 0)),
                  pl.BlockSpec((1, rows, n), lambda l, k: (l, k, 0)),
                  pl.BlockSpec((1, 1, n), lambda l, k: (l, 0, 0))],
        out_specs=pl.BlockSpec((1, 1, n), lambda l, k: (l, 0, 0)),
        out_shape=jax.ShapeDtypeStruct((depth, 1, n), F32),
        compiler_params=_cparams("arbitrary", "arbitrary"),
        name="ada_mod",
    )(c.reshape(d, 1), w_ada, b_ada.reshape(depth, 1, n))


def _ffn_rows(x, mod_ref, wi_ref, wo_ref, lng_ref, lnb_ref, *, sub, tf, alpha):
    shift, scale, gate = _mod_rows(mod_ref, sub)
    ff = wo_ref.shape[0]
    h = (x * (1.0 + scale) + shift).astype(BF16)
    y = jnp.zeros(x.shape, F32)
    for c0 in range(0, ff, tf):
        g = _dot(h, wi_ref[:, c0:c0 + tf])
        u = _dot(h, wi_ref[:, ff + c0:ff + c0 + tf])
        y = y + _dot((_silu(g) * u).astype(BF16), wo_ref[c0:c0 + tf, :])
    z = alpha * x + (0.5 * (1.0 + gate)) * y
    return _layer_norm(z, lng_ref[sub:sub + 1, :], lnb_ref[sub:sub + 1, :])


def _ffn_kernel(x_ref, mod_ref, wi_ref, wo_ref, lng_ref, lnb_ref, o_ref, *, sub, tf, alpha):
    o_ref[...] = _ffn_rows(x_ref[...], mod_ref, wi_ref, wo_ref, lng_ref, lnb_ref, sub=sub, tf=tf, alpha=alpha)


def _ffn_call(x, mod, w_in, w_out, ln_g, ln_b, *, layer, which, sub, alpha, tm, tf):
    s, d = x.shape
    assert w_out.shape[2] % tf == 0
    kern = functools.partial(_ffn_kernel, sub=sub, tf=tf, alpha=alpha)
    return pl.pallas_call(
        kern,
        grid=(s // tm,),
        in_specs=[pl.BlockSpec((tm, d), lambda i: (i, 0)),
                  pl.BlockSpec((3 * N_SUB, d), lambda i: (0, 0)),
                  _slab(w_in, layer, which), _slab(w_out, layer, which),
                  _slab(ln_g, layer), _slab(ln_b, layer)],
        out_specs=pl.BlockSpec((tm, d), lambda i: (i, 0)),
        out_shape=jax.ShapeDtypeStruct((s, d), F32),
        compiler_params=_cparams("arbitrary"),
        name="ffn",
    )(x, mod, w_in, w_out, ln_g, ln_b)


_PROJ_OUTS = (("gates", 3 * 1024, BF16), ("a", POOL_WIDTH, F32), ("h4", 4 * HGRN_WIDTH, F32),
              ("qi", IDX_HEADS * IDX_DIM, BF16), ("q", ATTN_WIDTH, BF16), ("k", ATTN_WIDTH, BF16),
              ("v", ATTN_WIDTH, BF16), ("ki2", 2 * IDX_DIM, BF16), ("wi", LANES, F32))
_PROJ_CHUNK = 1024


def _proj_kernel(x_ref, mod_ref, w_ref, *o_refs):
    shift, scale, _ = _mod_rows(mod_ref, 1)
    h = (x_ref[...] * (1.0 + scale) + shift).astype(BF16)
    c0 = 0
    for (name, width, dt), o_ref in zip(_PROJ_OUTS, o_refs):
        if name == "v":
            vt = lax.dot_general(w_ref[:, c0:c0 + width], h, (((0,), (1,)), ((), ())),
                                 preferred_element_type=F32)
            ones = jnp.ones((_VT_ROWS - LANES, vt.shape[1]), F32)
            rows = []
            for p0 in range(0, width, LANES):
                rows += [vt[p0:p0 + LANES, :], ones]
            o_ref[0] = jnp.concatenate(rows, axis=0).astype(dt)
        else:
            for cc in range(0, width, _PROJ_CHUNK):
                cw = min(_PROJ_CHUNK, width - cc)
                o_ref[:, cc:cc + cw] = _dot(h, w_ref[:, c0 + cc:c0 + cc + cw]).astype(dt)
        c0 += width


def _proj_weight(w):
    d = w.shape[-2]
    widths = (POOL_WIDTH, ATTN_WIDTH, ATTN_WIDTH, ATTN_WIDTH, IDX_HEADS * IDX_DIM, IDX_DIM, IDX_HEADS,
              HGRN_WIDTH, HGRN_WIDTH, HGRN_WIDTH, HGRN_WIDTH, 3 * d)
    parts, c0 = [], 0
    for wd in widths:
        parts.append(w[..., c0:c0 + wd])
        c0 += wd
    a, q, k, v, qi, ki, wi, hq, hf, hi, hg, gates = parts
    wi_pad = jnp.pad(wi, [(0, 0)] * (w.ndim - 1) + [(0, LANES - IDX_HEADS)])
    q = q * (ATTN_HEAD_DIM ** -0.5 * _LOG2E)
    return jnp.concatenate([gates, a, hq, hf, hi, hg, qi, q, k, v, ki, ki, wi_pad], axis=-1).astype(BF16)


def _slab(arr, *lead):
    block = (None,) * len(lead) + tuple(arr.shape[len(lead):])
    index = tuple(lead) + (0,) * (arr.ndim - len(lead))
    return pl.BlockSpec(block, lambda i: index, pipeline_mode=pl.Buffered(1))


def _proj_call(x, mod, w_all, *, layer, tm):
    s, d = x.shape
    n = w_all.shape[-1]
    assert n == sum(wd for _, wd, _ in _PROJ_OUTS)
    n_vt = (ATTN_WIDTH // LANES) * _VT_ROWS
    out_specs = [pl.BlockSpec((1, n_vt, tm), lambda i: (i, 0, 0)) if name == "v"
                 else pl.BlockSpec((tm, wd), lambda i: (i, 0)) for name, wd, _ in _PROJ_OUTS]
    out_shape = [jax.ShapeDtypeStruct((s // tm, n_vt, tm) if name == "v" else (s, wd), dt)
                 for name, wd, dt in _PROJ_OUTS]
    return pl.pallas_call(
        _proj_kernel,
        grid=(s // tm,),
        in_specs=[pl.BlockSpec((tm, d), lambda i: (i, 0)),
                  pl.BlockSpec((3 * N_SUB, d), lambda i: (0, 0)),
                  _slab(w_all, layer)],
        out_specs=out_specs,
        out_shape=out_shape,
        compiler_params=_cparams("arbitrary"),
        name="mix_proj",
    )(x, mod, w_all)


_BISECT_MAX_ITERS = 320
_F32_TINY = float(jnp.finfo(jnp.float32).tiny)
_COARSE_PASSES = 9
_COARSE_MIN_ABS, _COARSE_MAX_ABS = 1e-30, 1e30
_KIND_FAR, _KIND_PAD, _N_KINDS = 2, 3, 4
_VT_ROWS = LANES + 16
_SMASK_FULL, _SMASK_DIAG, _SMASK_PAD = 0, 1, 2
_LOG2E = math.log2(math.e)


def _t5_bucket(dist):
    max_exact = REL_BUCKETS // 2
    d32 = jnp.maximum(dist, 1).astype(F32)
    large = max_exact + (jnp.log(d32 / max_exact) / math.log(REL_MAX_DIST / max_exact)
                         * (REL_BUCKETS - max_exact)).astype(I32)
    large = jnp.minimum(large, REL_BUCKETS - 1)
    return jnp.where(dist < max_exact, dist, large)


def _dsa_kernel(relb_ref, qi_ref, wi_ref, q_ref, ki2_ref, k_ref, vt_ref, o_ref,
                s_sc, sb_sc, bias_sc, smask_sc, lg_sc, qim_sc, qm_sc, m_sc, acc_sc,
                *, tq, topk, idx_scale):
    tk = tq
    i = pl.program_id(0)
    krow = lax.broadcasted_iota(I32, (tk, tq), 0)
    qcol = lax.broadcasted_iota(I32, (tk, tq), 1)
    causal = krow <= qcol
    lane = lax.broadcasted_iota(I32, (tq, LANES), 1)
    lo_half = lane < ATTN_HEAD_DIM

    @pl.when(i == 0)
    def _():
        for which in range(2):
            bucket = _t5_bucket(jnp.maximum(qcol - krow + which * tq, 0))
            for h in range(ATTN_HEADS):
                b = jnp.zeros((tk, tq), F32)
                for bk in range(REL_BUCKETS):
                    b = jnp.where(bucket == bk, relb_ref[bk, h] * _LOG2E, b)
                bias_sc[h, which] = (jnp.where(causal, b, NEG_INF) if which == 0 else b).astype(BF16)
        for h in range(ATTN_HEADS):
            bias_sc[h, _KIND_FAR] = jnp.full((tk, tq), relb_ref[REL_BUCKETS - 1, h] * _LOG2E, F32).astype(BF16)
            bias_sc[h, _KIND_PAD] = jnp.full((tk, tq), NEG_INF, BF16)
        smask_sc[_SMASK_FULL] = jnp.zeros((tk, tq), F32)
        smask_sc[_SMASK_DIAG] = jnp.where(causal, 0.0, NEG_INF)
        smask_sc[_SMASK_PAD] = jnp.full((tk, tq), NEG_INF, F32)

    def head_operands(src_ref, dst_sc, n_heads):
        src = src_ref[...].astype(F32)
        for h in range(n_heads):
            pair = src[:, LANES * (h // 2):LANES * (h // 2 + 1)]
            pair = jnp.where(lo_half if h % 2 == 0 else ~lo_half, pair, 0.0)
            dst_sc[h] = jnp.transpose(pair).astype(BF16)

    head_operands(qi_ref, qim_sc, IDX_HEADS)
    head_operands(q_ref, qm_sc, ATTN_HEADS)

    wt = jnp.transpose(wi_ref[...])

    def key_rows(jb):
        return pl.ds(pl.multiple_of(jb * tk, tk), tk)

    def fold(x):
        return x.reshape(tk // SUBLANES, SUBLANES, tq)

    def index_scores(jb):
        kb = ki2_ref[key_rows(jb), :]
        acc = jnp.zeros((tk, tq), F32)
        for h in range(IDX_HEADS):
            acc = acc + wt[h:h + 1, :] * jnp.maximum(_dot(kb, qim_sc[h]), 0.0)
        return acc * idx_scale

    npair = (i + 2) // 2
    last_kv = k_ref.shape[0] // tk - 1

    def p1_pair(j, carry):
        rmin, rmax = carry
        for jb in (2 * j, 2 * j + 1):
            raw = index_scores(jnp.minimum(jb, last_kv))
            kind = jnp.where(jb > i, _SMASK_PAD, jnp.where(jb == i, _SMASK_DIAG, _SMASK_FULL))
            sc = raw + smask_sc[kind]
            s_sc[jb] = sc
            sb_sc[jb] = sc.astype(BF16)
            rmin = jnp.minimum(rmin, jnp.min(fold(raw), axis=0))
            rmax = jnp.maximum(rmax, jnp.max(fold(raw), axis=0))
        return rmin, rmax

    rmin, rmax = lax.fori_loop(0, npair, p1_pair, (jnp.full((SUBLANES, tq), POS_INF, F32),
                                                   jnp.full((SUBLANES, tq), NEG_INF, F32)))
    rowmin = jnp.min(rmin, axis=0, keepdims=True)
    rowmax = jnp.max(rmax, axis=0, keepdims=True)

    kf = float(topk)
    nfin = (i * tq + lax.broadcasted_iota(I32, (1, tq), 1) + 1).astype(F32)
    small = nfin <= kf

    def walk_blocks(blocks_fn, init):
        return lax.fori_loop(0, npair, lambda j, c: blocks_fn([2 * j, 2 * j + 1], c), init)

    def count_ge(x):
        def blocks_fn(jbs, cnt):
            for jb in jbs:
                cnt = cnt + jnp.sum(fold(jnp.where(s_sc[jb] >= x, 1.0, 0.0)), axis=0)
            return cnt
        return jnp.sum(walk_blocks(blocks_fn, jnp.zeros((SUBLANES, tq), F32)), axis=0, keepdims=True)

    def count_ge16(xb):
        one, zero = jnp.ones((), BF16), jnp.zeros((), BF16)
        rows16 = 2 * SUBLANES

        def blocks_fn(jbs, cnt):
            parts = []
            for jb in jbs:
                c = jnp.where(sb_sc[jb] >= xb, one, zero)
                parts += [c[r0:r0 + rows16, :] for r0 in range(0, tk, rows16)]
            while len(parts) > 1:
                parts = [a + b for a, b in zip(parts[0::2], parts[1::2])]
            return cnt + parts[0].astype(F32)
        return jnp.sum(walk_blocks(blocks_fn, jnp.zeros((rows16, tq), F32)), axis=0, keepdims=True)

    def coarse_probe(mid):
        xb = mid.astype(BF16)
        xbits = lax.bitcast_convert_type(xb.astype(F32), I32)
        neg = xbits < 0
        mu = jnp.where(neg, xbits + 0x8000, xbits - 0x8000)
        odd = (xbits & 0x10000) != 0
        tbits = jnp.where(odd, jnp.where(neg, mu - 1, mu + 1), mu)
        t = lax.bitcast_convert_type(tbits, F32)
        ok = jnp.logical_and(jnp.abs(xb.astype(F32)) > _COARSE_MIN_ABS, jnp.abs(xb.astype(F32)) < _COARSE_MAX_ABS)
        return xb, t, ok

    def narrow(st, probe, c, active):
        lo, hi, clo, chi, done = st
        up = jnp.logical_and(active, c >= kf)
        dn = jnp.logical_and(active, c < kf)
        return (jnp.where(up, probe, lo), jnp.where(dn, probe, hi), jnp.where(up, c, clo),
                jnp.where(dn, c, chi), jnp.where(jnp.logical_and(active, c == kf), 1.0, done))

    def coarse_body(_, st):
        lo, hi, _, _, done = st
        xb, t, ok = coarse_probe(0.5 * lo + 0.5 * hi)
        ok = jnp.logical_and(ok, jnp.logical_and(t > lo, t < hi))
        return narrow(st, t, count_ge16(xb), jnp.logical_and(done < 0.5, ok))

    def fine_cond(st):
        it, (_, _, _, _, done) = st
        return jnp.logical_and(it < _BISECT_MAX_ITERS, jnp.min(done) < 0.5)

    def fine_body(st):
        it, inner = st
        lo, hi, _, _, done = inner
        mid = 0.5 * lo + 0.5 * hi
        mid = jnp.where(jnp.logical_and(lo == 0.0, hi > _F32_TINY), _F32_TINY, mid)
        mid = jnp.where(jnp.logical_and(lo < 0.0, hi > 0.0), 0.0, mid)
        stuck = jnp.logical_or(jnp.logical_or(mid <= lo, mid >= hi),
                               jnp.logical_and(lo == 0.0, hi <= _F32_TINY))
        lo, hi, clo, chi, done = narrow(inner, mid, count_ge(mid),
                                        jnp.logical_and(done < 0.5, jnp.logical_not(stuck)))
        return it + 1, (lo, hi, clo, chi, jnp.where(stuck, 1.0, done))

    mbits = lax.bitcast_convert_type(rowmax, I32)
    above_max = jnp.where(rowmax == 0.0, _F32_TINY,
                          lax.bitcast_convert_type(jnp.where(rowmax > 0.0, mbits + 1, mbits - 1), F32))
    st0 = (jnp.where(small, NEG_INF, rowmin), above_max, nfin, jnp.zeros((1, tq), F32),
           jnp.where(small, 1.0, 0.0))
    st1 = lax.fori_loop(0, _COARSE_PASSES, coarse_body, st0)
    _, (lo, hi, clo, chi, _) = lax.while_loop(fine_cond, fine_body, (jnp.int32(0), st1))
    budget = kf - chi
    need_ties = jnp.max((clo - chi) - budget) > 0.5

    pairs = [slice(LANES * (h // 2), LANES * (h // 2 + 1)) for h in range(ATTN_HEADS)]

    def issue_qk(jb, slot):
        kb = k_ref[key_rows(jnp.minimum(jb, last_kv)), :]
        for h in range(ATTN_HEADS):
            lg_sc[slot, h] = _dot(kb[:, pairs[h]], qm_sc[h]).astype(BF16)

    def attend(jb, slot, thr, cum=None):
        s = s_sc[jb]
        if cum is None:
            sel = s >= thr
        else:
            cand = jnp.logical_and(s >= lo, s < hi)
            before = (qcol < krow).astype(BF16)
            rank = cum + _dot(before, jnp.where(cand, 1.0, 0.0).astype(BF16))
            sel = jnp.logical_or(s >= hi, jnp.logical_and(cand, rank < budget))
        kind = jnp.where(jb > i, _KIND_PAD, jnp.minimum(i - jb, _KIND_FAR))
        selm = jnp.where(sel, 0.0, NEG_INF).astype(BF16)
        probs, alphas = [], []
        for h in range(ATTN_HEADS):
            lg = lg_sc[slot, h] + bias_sc[h, kind] + selm
            m_old = m_sc[h]
            m_new = jnp.maximum(m_old, jnp.max(lg, axis=0, keepdims=True).astype(F32))
            m_safe = jnp.where(m_new == NEG_INF, 0.0, m_new)
            alphas.append(jnp.exp2(m_old - m_safe))
            probs.append(jnp.exp2(lg - m_safe.astype(BF16)))
            m_sc[h] = m_new
        vb = vt_ref[jnp.minimum(jb, last_kv)]
        for h in range(ATTN_HEADS):
            vrows = slice(_VT_ROWS * (h // 2), _VT_ROWS * (h // 2 + 1))
            acc_sc[h] = alphas[h] * acc_sc[h] + _dot(vb[vrows, :], probs[h])

    def run(with_ties):
        m_sc[...] = jnp.full(m_sc.shape, NEG_INF, F32)
        acc_sc[...] = jnp.zeros_like(acc_sc)
        issue_qk(0, 0)

        def pair(j, thr0, thr1, cum0=None, cum1=None):
            issue_qk(2 * j + 1, 1)
            attend(2 * j, 0, thr0, cum0)
            issue_qk(2 * j + 2, 0)
            attend(2 * j + 1, 1, thr1, cum1)

        if not with_ties:
            def pair_body(j, carry):
                pair(j, lo, lo)
                return carry
            lax.fori_loop(0, npair, pair_body, 0)
            return

        last_s = s_sc.shape[0] - 1

        def cand_count(jb):
            s = s_sc[jnp.minimum(jb, last_s)]
            cand = jnp.logical_and(s >= lo, s < hi)
            return jnp.sum(jnp.where(cand, 1.0, 0.0), axis=0, keepdims=True)

        def needs_ranking(cum0, ca, cb):
            cum1 = cum0 + ca
            runs_out = jnp.logical_or(jnp.logical_and(cum1 > budget, cum0 < budget),
                                      jnp.logical_and(cum1 + cb > budget, cum1 < budget))
            return (jnp.max(jnp.where(runs_out, 1.0, 0.0)) > 0.5).astype(I32)

        def tie_pair_body(j, carry):
            ranked, cum0, ca, cb = carry
            cum1 = cum0 + ca
            cum2 = cum1 + cb

            def variant(with_rank):
                def fn():
                    na, nb = cand_count(2 * j + 2), cand_count(2 * j + 3)
                    nxt = needs_ranking(cum2, na, nb)
                    if with_rank:
                        pair(j, None, None, cum0, cum1)
                    else:
                        pair(j, jnp.where(cum1 <= budget, lo, hi), jnp.where(cum2 <= budget, lo, hi))
                    return nxt, cum2, na, nb
                return fn
            return lax.cond(ranked == 1, variant(True), variant(False))

        c0, c1 = cand_count(0), cand_count(1)
        zero = jnp.zeros((1, tq), F32)
        lax.fori_loop(0, npair, tie_pair_body, (needs_ranking(zero, c0, c1), zero, c0, c1))

    @pl.when(need_ties)
    def _():
        run(True)

    @pl.when(jnp.logical_not(need_ties))
    def _():
        run(False)

    first_head = lax.broadcasted_iota(I32, (LANES, tq), 0) < ATTN_HEAD_DIM
    normed = [acc_sc[h, 0:LANES, :] / acc_sc[h, LANES:LANES + 1, :] for h in range(ATTN_HEADS)]
    for p in range(ATTN_HEADS // 2):
        pair_t = jnp.where(first_head, normed[2 * p], normed[2 * p + 1])
        o_ref[:, LANES * p:LANES * (p + 1)] = jnp.transpose(pair_t)


def _dsa_call(rel_bias, qi, wi, q, ki2, k, vt, *, tq):
    s = q.shape[0]
    assert tq > REL_MAX_DIST and s % tq == 0
    nb = s // tq
    topk = min(TOPK_MAX, s // 4)
    kern = functools.partial(_dsa_kernel, tq=tq, topk=topk,
                             idx_scale=(IDX_DIM ** -0.5) * (IDX_HEADS ** -0.5))
    whole = lambda width: pl.BlockSpec((s, width), lambda i: (0, 0), pipeline_mode=pl.Buffered(1))
    assert vt.shape == (nb, (ATTN_WIDTH // LANES) * _VT_ROWS, tq)
    return pl.pallas_call(
        kern,
        grid=(nb,),
        in_specs=[pl.BlockSpec(memory_space=pltpu.SMEM),
                  pl.BlockSpec((tq, IDX_HEADS * IDX_DIM), lambda i: (i, 0)),
                  pl.BlockSpec((tq, LANES), lambda i: (i, 0)),
                  pl.BlockSpec((tq, ATTN_WIDTH), lambda i: (i, 0)),
                  whole(2 * IDX_DIM), whole(ATTN_WIDTH),
                  pl.BlockSpec(vt.shape, lambda i: (0, 0, 0), pipeline_mode=pl.Buffered(1))],
        out_specs=pl.BlockSpec((tq, ATTN_WIDTH), lambda i: (i, 0)),
        out_shape=jax.ShapeDtypeStruct((s, ATTN_WIDTH), F32),
        scratch_shapes=[pltpu.VMEM((nb + nb % 2, tq, tq), F32),
                        pltpu.VMEM((nb + nb % 2, tq, tq), BF16),
                        pltpu.VMEM((ATTN_HEADS, _N_KINDS, tq, tq), BF16),
                        pltpu.VMEM((3, tq, tq), F32),
                        pltpu.VMEM((2, ATTN_HEADS, tq, tq), BF16),
                        pltpu.VMEM((IDX_HEADS, LANES, tq), BF16),
                        pltpu.VMEM((ATTN_HEADS, LANES, tq), BF16),
                        pltpu.VMEM((ATTN_HEADS, 1, tq), F32),
                        pltpu.VMEM((ATTN_HEADS, _VT_ROWS, tq), F32)],
        compiler_params=_cparams("arbitrary"),
        name="dsa_attention",
    )(rel_bias, qi, wi, q, ki2, k, vt)


_HGRN_CHUNK = 128
_HGRN_BASE = SUBLANES


def _hgrn_kernel(h4_ref, lb_ref, ng_ref, o_ref, st_sc, *, layer, rows):
    c_len = _HGRN_CHUNK

    @pl.when(pl.program_id(0) == 0)
    def _():
        st_sc[...] = jnp.zeros_like(st_sc)

    lb_all = lb_ref[...]
    e = jnp.exp(lb_all - jnp.max(lb_all, axis=0, keepdims=True))
    sm = e / jnp.sum(e, axis=0, keepdims=True)
    lbs = jnp.sum(sm[0:layer + 1, :], axis=0, keepdims=True) - sm[0:1, :]

    r1 = lax.broadcasted_iota(I32, (c_len, 1), 0)
    rr = lax.broadcasted_iota(I32, (c_len, c_len), 0)
    cc = lax.broadcasted_iota(I32, (c_len, c_len), 1)

    def block_ref(a, blk, r):
        x = a.reshape(c_len // blk, blk, a.shape[-1])[:, r:r + 1, :]
        return jnp.broadcast_to(x, (c_len // blk, blk, a.shape[-1])).reshape(c_len, a.shape[-1])

    w = HGRN_WIDTH
    for c in range(rows // c_len):
        rs = slice(c * c_len, (c + 1) * c_len)
        hq = h4_ref[rs, 0:w]
        z = h4_ref[rs, w:2 * w]
        val = h4_ref[rs, 2 * w:3 * w]
        hg = h4_ref[rs, 3 * w:4 * w]
        key = (1.0 - lbs) * jax.nn.sigmoid(-z)
        g = jnp.log1p(-jnp.minimum(key, KEY_MAX))
        a_cum = g
        sh = 1
        while sh < c_len:
            a_cum = a_cum + jnp.where(r1 >= sh, pltpu.roll(a_cum, sh, 0), 0.0)
            sh *= 2
        qf = _silu(hq)
        outs = []
        for h in range(HGRN_HEADS):
            hs = slice(h * HGRN_DIM, (h + 1) * HGRN_DIM)
            ah, qh, kh, vh = a_cum[:, hs], qf[:, hs], key[:, hs], val[:, hs]
            vh16 = vh.astype(BF16)
            st = st_sc[h]
            o = _dot_nt((qh * jnp.exp(ah)).astype(BF16), st.astype(BF16))
            ref = block_ref(ah, _HGRN_BASE, _HGRN_BASE // 2 - 1)
            sc = _dot_nt((qh * jnp.exp(ah - ref)).astype(BF16), (kh * jnp.exp(ref - ah)).astype(BF16))
            same = (rr >> _log2(_HGRN_BASE)) == (cc >> _log2(_HGRN_BASE))
            scores = jnp.where(jnp.logical_and(same, cc <= rr), sc, 0.0)
            half = _HGRN_BASE
            while half < c_len:
                blk = 2 * half
                ref = block_ref(ah, blk, half - 1)
                second = (r1 & (blk - 1)) >= half
                ql = jnp.where(second, qh * jnp.exp(jnp.minimum(ah - ref, 0.0)), 0.0)
                kl = jnp.where(second, 0.0, kh * jnp.exp(jnp.minimum(ref - ah, 0.0)))
                sc = _dot_nt(ql.astype(BF16), kl.astype(BF16))
                scores = scores + jnp.where((rr >> _log2(blk)) == (cc >> _log2(blk)), sc, 0.0)
                half = blk
            o = o + _dot(scores.astype(BF16), vh16)
            a_last = ah[c_len - 1:c_len, :]
            kd = (kh * jnp.exp(a_last - ah)).astype(BF16)
            st_sc[h] = st * jnp.exp(a_last) + _dot_tn(vh16, kd)
            o = o * lax.rsqrt(jnp.mean(o * o, axis=-1, keepdims=True) + RMS_EPS)
            outs.append(o)
        o_ref[rs, :] = jnp.concatenate(outs, axis=1) * ng_ref[...] * _silu(hg)


def _hgrn_call(h4, lb, ng, *, layer, rows):
    s = h4.shape[0]
    kern = functools.partial(_hgrn_kernel, layer=layer, rows=rows)
    return pl.pallas_call(
        kern,
        grid=(s // rows,),
        in_specs=[pl.BlockSpec((rows, 4 * HGRN_WIDTH), lambda i: (i, 0)),
                  pl.BlockSpec(lb.shape, lambda i: (0, 0)),
                  _slab(ng, layer)],
        out_specs=pl.BlockSpec((rows, HGRN_WIDTH), lambda i: (i, 0)),
        out_shape=jax.ShapeDtypeStruct((s, HGRN_WIDTH), F32),
        scratch_shapes=[pltpu.VMEM((HGRN_HEADS, HGRN_DIM, HGRN_DIM), F32)],
        compiler_params=_cparams("arbitrary"),
        name="hgrn2",
    )(h4, lb, ng)


_POOL_HALO = 16


def _merge_kernel(x_ref, mod_ref, a_ref, ah_ref, yb_ref, yc_ref, g_ref, pw_ref, ps_ref, wb_ref, wo_ref,
                  lng_ref, lnb_ref, fwi_ref, fwo_ref, o_ref, *, tm, tf, alpha):
    i = pl.program_id(0)
    d = x_ref.shape[1]
    _, _, gate = _mod_rows(mod_ref, 1)
    a = a_ref[...]
    halo = jnp.where(i == 0, 0.0, ah_ref[...])
    ext = jnp.concatenate([halo, a], axis=0)
    lane = lax.broadcasted_iota(I32, (tm, POOL_WIDTH), 1)
    grp = lane >> _log2(POOL_GROUP)
    pos1 = (i * tm + lax.broadcasted_iota(I32, (tm, POOL_WIDTH), 0) + 1).astype(F32)
    win = jnp.zeros((tm, POOL_WIDTH), F32)
    cnt = jnp.zeros((tm, POOL_WIDTH), F32)
    ssum, have = ext, 1
    for gi, wd in enumerate(POOL_WINDOWS):
        while have < wd:
            ssum = ssum + pltpu.roll(ssum, have, 0)
            have *= 2
        assert have == wd
        win = jnp.where(grp == gi, ssum[_POOL_HALO:, :], win)
        cnt = jnp.where(grp == gi, jnp.minimum(pos1, float(wd)), cnt)
    dlt = (win / cnt - a).astype(BF16)
    ya = _dot(dlt, pw_ref[...]) * ps_ref[...]
    za = _dot(ya.astype(BF16), wb_ref[0:POOL_WIDTH, :])
    zb = _dot(yb_ref[...].astype(BF16), wb_ref[POOL_WIDTH:POOL_WIDTH + ATTN_WIDTH, :])
    zc = _dot(yc_ref[...].astype(BF16), wb_ref[POOL_WIDTH + ATTN_WIDTH:, :])
    g0, g1, g2 = (jax.nn.sigmoid(g_ref[:, j * d:(j + 1) * d].astype(F32)) for j in range(3))
    merged = g0 * za + g1 * zb + g2 * zc
    y = _dot(merged.astype(BF16), wo_ref[...])
    z = alpha * x_ref[...] + (1.0 + gate) * y
    xm = _layer_norm(z, lng_ref[1:2, :], lnb_ref[1:2, :])
    o_ref[...] = _ffn_rows(xm, mod_ref, fwi_ref, fwo_ref, lng_ref, lnb_ref, sub=2, tf=tf, alpha=alpha)


def _merge_call(x, mod, a, yb, yc, gates, pw_bd, ps, wb, wo, ln_g, ln_b, ffn_wi, ffn_wo, *, layer, tm, tf, alpha):
    s, d = x.shape
    kern = functools.partial(_merge_kernel, tm=tm, tf=tf, alpha=alpha)
    full = lambda arr: _slab(arr, layer)
    hb = tm // _POOL_HALO
    return pl.pallas_call(
        kern,
        grid=(s // tm,),
        in_specs=[pl.BlockSpec((tm, d), lambda i: (i, 0)),
                  pl.BlockSpec((3 * N_SUB, d), lambda i: (0, 0)),
                  pl.BlockSpec((tm, POOL_WIDTH), lambda i: (i, 0)),
                  pl.BlockSpec((_POOL_HALO, POOL_WIDTH), lambda i: (jnp.maximum(i * hb - 1, 0), 0)),
                  pl.BlockSpec((tm, ATTN_WIDTH), lambda i: (i, 0)),
                  pl.BlockSpec((tm, HGRN_WIDTH), lambda i: (i, 0)),
                  pl.BlockSpec((tm, 3 * d), lambda i: (i, 0)),
                  full(pw_bd), full(ps), full(wb), full(wo), full(ln_g), full(ln_b),
                  _slab(ffn_wi, layer, 1), _slab(ffn_wo, layer, 1)],
        out_specs=pl.BlockSpec((tm, d), lambda i: (i, 0)),
        out_shape=jax.ShapeDtypeStruct((s, d), F32),
        compiler_params=_cparams("arbitrary"),
        name="merge_ffn",
    )(x, mod, a, a, yb, yc, gates, pw_bd, ps, wb, wo, ln_g, ln_b, ffn_wi, ffn_wo)


def _pool_blockdiag(pool_w):
    depth, n, g, _ = pool_w.shape
    eye = jnp.eye(n, dtype=pool_w.dtype)
    return jnp.einsum("lgij,gh->lgihj", pool_w, eye).reshape(depth, n * g, n * g)


def kernel(x, c, w_ada, b_ada, ln_g, ln_b, ffn_w_in, ffn_w_out, mix_w_in, pool_w, pool_scale, rel_bias,
           hgrn_lb, hgrn_norm_g, w_branch, w_out):
    batch, s, d = x.shape
    assert batch == 1, "kernels are written for a single sequence"
    depth = w_ada.shape[0]
    alpha = (2 * depth) ** 0.25
    tm_ffn = min(512, s)
    tf = 256
    tq = min(256, s)
    tm_proj = tq
    tm_merge = min(512, s)
    hgrn_rows = min(256, s)

    ffn_wi16, ffn_wo16 = ffn_w_in.astype(BF16), ffn_w_out.astype(BF16)
    mix_w16 = _proj_weight(mix_w_in)
    pool_bd16 = _pool_blockdiag(pool_w).astype(BF16)
    wb16, wo16 = w_branch.astype(BF16), w_out.astype(BF16)
    pool_scale3 = pool_scale.reshape(depth, 1, -1)
    norm_g3 = hgrn_norm_g.reshape(depth, 1, -1)

    mods = _ada_call(c, w_ada, b_ada)
    xs = x.reshape(s, d)
    for l in range(depth):
        mod = mods[l].reshape(3 * N_SUB, d)
        xs = _ffn_call(xs, mod, ffn_wi16, ffn_wo16, ln_g, ln_b, layer=l, which=0, sub=0, alpha=alpha,
                       tm=tm_ffn, tf=tf)
        gates, a, h4, qi, q, k, vt, ki2, wi = _proj_call(xs, mod, mix_w16, layer=l, tm=tm_proj)
        yb = _dsa_call(rel_bias, qi, wi, q, ki2, k, vt, tq=tq)
        yc = _hgrn_call(h4, hgrn_lb, norm_g3, layer=l, rows=hgrn_rows)
        xs = _merge_call(xs, mod, a, yb, yc, gates, pool_bd16, pool_scale3, wb16, wo16, ln_g, ln_b,
                         ffn_wi16, ffn_wo16, layer=l, tm=tm_merge, tf=tf, alpha=alpha)
    return xs.reshape(batch, s, d)
```

```python
import functools
import math

import jax
import jax.numpy as jnp
from jax import lax
from jax.experimental import pallas as pl
from jax.experimental.pallas import tpu as pltpu

F32 = jnp.float32
BF16 = jnp.bfloat16
I32 = jnp.int32

POOL_WINDOWS = (2, 4, 8, 16)
POOL_GROUP = 64
POOL_WIDTH = POOL_GROUP * len(POOL_WINDOWS)
ATTN_HEADS = 4
ATTN_HEAD_DIM = 64
ATTN_WIDTH = ATTN_HEADS * ATTN_HEAD_DIM
IDX_HEADS = 8
IDX_DIM = 64
TOPK_MAX = 256
HGRN_HEADS = 4
HGRN_DIM = 128
HGRN_WIDTH = HGRN_HEADS * HGRN_DIM
KEY_MAX = 1.0 - 1e-6
REL_BUCKETS = 32
REL_MAX_DIST = 128
N_SUB = 3
LN_EPS = 1e-5
RMS_EPS = 1e-6

LANES = 128
SUBLANES = 8
VMEM_LIMIT_BYTES = 58 * 1024 * 1024

NEG_INF = float("-inf")
POS_INF = float("inf")


def _cparams(*sem):
    return pltpu.CompilerParams(dimension_semantics=sem, vmem_limit_bytes=VMEM_LIMIT_BYTES)


def _dot(a, b):
    return jnp.dot(a, b, preferred_element_type=F32)


def _dot_nt(a, b):
    return lax.dot_general(a, b, (((1,), (1,)), ((), ())), preferred_element_type=F32)


def _dot_tn(a, b):
    return lax.dot_general(a, b, (((0,), (0,)), ((), ())), preferred_element_type=F32)


def _log2(n):
    assert n & (n - 1) == 0
    return n.bit_length() - 1


def _silu(x):
    return x * jax.nn.sigmoid(x)


def _mod_rows(mod_ref, j):
    return mod_ref[3 * j:3 * j + 1, :], mod_ref[3 * j + 1:3 * j + 2, :], mod_ref[3 * j + 2:3 * j + 3, :]


def _layer_norm(z, g, b):
    mu = jnp.mean(z, axis=-1, keepdims=True)
    zc = z - mu
    var = jnp.mean(zc * zc, axis=-1, keepdims=True)
    return zc * lax.rsqrt(var + LN_EPS) * g + b


def _ada_kernel(c_ref, w_ref, b_ref, o_ref):
    @pl.when(pl.program_id(1) == 0)
    def _():
        o_ref[0] = b_ref[0]

    c = c_ref[...]
    o_ref[0] += jnp.sum(_silu(c) * w_ref[0], axis=0, keepdims=True)


def _ada_call(c, w_ada, b_ada):
    depth, d, n = w_ada.shape
    rows = LANES
    return pl.pallas_call(
        _ada_kernel,
        grid=(depth, d // rows),
        in_specs=[pl.BlockSpec((rows, 1), lambda l, k: (k, 0)),
                  pl.BlockSpec((1, rows, n), lambda l, k: (l, k, 0)),
                  pl.BlockSpec((1, 1, n), lambda l, k: (l, 0, 0))],
        out_specs=pl.BlockSpec((1, 1, n), lambda l, k: (l, 0, 0)),
        out_shape=jax.ShapeDtypeStruct((depth, 1, n), F32),
        compiler_params=_cparams("arbitrary", "arbitrary"),
        name="ada_mod",
    )(c.reshape(d, 1), w_ada, b_ada.reshape(depth, 1, n))


def _ffn_rows(x, mod_ref, wi_ref, wo_ref, lng_ref, lnb_ref, *, sub, tf, alpha):
    shift, scale, gate = _mod_rows(mod_ref, sub)
    ff = wo_ref.shape[0]
    h = (x * (1.0 + scale) + shift).astype(BF16)
    y = jnp.zeros(x.shape, F32)
    for c0 in range(0, ff, tf):
        g = _dot(h, wi_ref[:, c0:c0 + tf])
        u = _dot(h, wi_ref[:, ff + c0:ff + c0 + tf])
        y = y + _dot((_silu(g) * u).astype(BF16), wo_ref[c0:c0 + tf, :])
    z = alpha * x + (0.5 * (1.0 + gate)) * y
    return _layer_norm(z, lng_ref[sub:sub + 1, :], lnb_ref[sub:sub + 1, :])


def _ffn_kernel(x_ref, mod_ref, wi_ref, wo_ref, lng_ref, lnb_ref, o_ref, *, sub, tf, alpha):
    o_ref[...] = _ffn_rows(x_ref[...], mod_ref, wi_ref, wo_ref, lng_ref, lnb_ref, sub=sub, tf=tf, alpha=alpha)


def _ffn_call(x, mod, w_in, w_out, ln_g, ln_b, *, layer, which, sub, alpha, tm, tf):
    s, d = x.shape
    assert w_out.shape[2] % tf == 0
    kern = functools.partial(_ffn_kernel, sub=sub, tf=tf, alpha=alpha)
    return pl.pallas_call(
        kern,
        grid=(s // tm,),
        in_specs=[pl.BlockSpec((tm, d), lambda i: (i, 0)),
                  pl.BlockSpec((3 * N_SUB, d), lambda i: (0, 0)),
                  _slab(w_in, layer, which), _slab(w_out, layer, which),
                  _slab(ln_g, layer), _slab(ln_b, layer)],
        out_specs=pl.BlockSpec((tm, d), lambda i: (i, 0)),
        out_shape=jax.ShapeDtypeStruct((s, d), F32),
        compiler_params=_cparams("arbitrary"),
        name="ffn",
    )(x, mod, w_in, w_out, ln_g, ln_b)


_PROJ_OUTS = (("gates", 3 * 1024, BF16), ("a", POOL_WIDTH, F32), ("h4", 4 * HGRN_WIDTH, F32),
              ("qi", IDX_HEADS * IDX_DIM, BF16), ("q", ATTN_WIDTH, BF16), ("k", ATTN_WIDTH, BF16),
              ("v", ATTN_WIDTH, BF16), ("ki2", 2 * IDX_DIM, BF16), ("wi", LANES, F32))
_PROJ_CHUNK = 1024


def _proj_kernel(x_ref, mod_ref, w_ref, *o_refs):
    shift, scale, _ = _mod_rows(mod_ref, 1)
    h = (x_ref[...] * (1.0 + scale) + shift).astype(BF16)
    c0 = 0
    for (name, width, dt), o_ref in zip(_PROJ_OUTS, o_refs):
        if name == "v":
            vt = lax.dot_general(w_ref[:, c0:c0 + width], h, (((0,), (1,)), ((), ())),
                                 preferred_element_type=F32)
            ones = jnp.ones((_VT_ROWS - LANES, vt.shape[1]), F32)
            rows = []
            for p0 in range(0, width, LANES):
                rows += [vt[p0:p0 + LANES, :], ones]
            o_ref[0] = jnp.concatenate(rows, axis=0).astype(dt)
        else:
            for cc in range(0, width, _PROJ_CHUNK):
                cw = min(_PROJ_CHUNK, width - cc)
                o_ref[:, cc:cc + cw] = _dot(h, w_ref[:, c0 + cc:c0 + cc + cw]).astype(dt)
        c0 += width


def _proj_weight_kernel(w_ref, o_ref):
    out_width = {name: wd for name, wd, _ in _PROJ_OUTS}
    src = {}
    c0 = 0
    for name, wd in (("a", POOL_WIDTH), ("q", ATTN_WIDTH), ("k", ATTN_WIDTH), ("v", ATTN_WIDTH),
                     ("qi", IDX_HEADS * IDX_DIM), ("ki", IDX_DIM), ("wi", IDX_HEADS),
                     ("h4", 4 * HGRN_WIDTH), ("gates", out_width["gates"])):
        src[name] = (c0, wd)
        c0 += wd
    assert c0 == w_ref.shape[-1]
    q_scale = ATTN_HEAD_DIM ** -0.5 * _LOG2E
    dst = 0
    for name, width, _ in _PROJ_OUTS:
        if name == "ki2":
            pieces = [("ki", 1.0), ("ki", 1.0)]
        elif name == "wi":
            o_ref[:, dst:dst + width] = jnp.zeros((o_ref.shape[0], width), BF16)
            pieces = [("wi", 1.0)]
        else:
            pieces = [(name, q_scale if name == "q" else 1.0)]
        off = dst
        for pname, scale in pieces:
            s0, wd = src[pname]
            o_ref[:, off:off + wd] = (w_ref[:, s0:s0 + wd] * scale).astype(BF16)
            off += wd
        dst += width


def _proj_weight(w):
    depth, d, n_in = w.shape
    n_out = sum(wd for _, wd, _ in _PROJ_OUTS)
    rows = 256
    return pl.pallas_call(
        _proj_weight_kernel,
        grid=(depth, d // rows),
        in_specs=[pl.BlockSpec((None, rows, n_in), lambda l, r: (l, r, 0))],
        out_specs=pl.BlockSpec((None, rows, n_out), lambda l, r: (l, r, 0)),
        out_shape=jax.ShapeDtypeStruct((depth, d, n_out), BF16),
        compiler_params=_cparams("arbitrary", "arbitrary"),
        name="proj_weight",
    )(w)


def _slab(arr, *lead):
    block = (None,) * len(lead) + tuple(arr.shape[len(lead):])
    index = tuple(lead) + (0,) * (arr.ndim - len(lead))
    return pl.BlockSpec(block, lambda i: index, pipeline_mode=pl.Buffered(1))


def _proj_call(x, mod, w_all, *, layer, tm):
    s, d = x.shape
    n = w_all.shape[-1]
    assert n == sum(wd for _, wd, _ in _PROJ_OUTS)
    n_vt = (ATTN_WIDTH // LANES) * _VT_ROWS
    out_specs = [pl.BlockSpec((1, n_vt, tm), lambda i: (i, 0, 0)) if name == "v"
                 else pl.BlockSpec((tm, wd), lambda i: (i, 0)) for name, wd, _ in _PROJ_OUTS]
    out_shape = [jax.ShapeDtypeStruct((s // tm, n_vt, tm) if name == "v" else (s, wd), dt)
                 for name, wd, dt in _PROJ_OUTS]
    return pl.pallas_call(
        _proj_kernel,
        grid=(s // tm,),
        in_specs=[pl.BlockSpec((tm, d), lambda i: (i, 0)),
                  pl.BlockSpec((3 * N_SUB, d), lambda i: (0, 0)),
                  _slab(w_all, layer)],
        out_specs=out_specs,
        out_shape=out_shape,
        compiler_params=_cparams("arbitrary"),
        name="mix_proj",
    )(x, mod, w_all)


_BISECT_MAX_ITERS = 320
_F32_TINY = float(jnp.finfo(jnp.float32).tiny)
_COARSE_PASSES = 9
_COARSE_MIN_ABS, _COARSE_MAX_ABS = 1e-30, 1e30
_KIND_FAR, _KIND_PAD, _N_KINDS = 2, 3, 4
_VT_ROWS = LANES + 16
_SMASK_FULL, _SMASK_DIAG, _SMASK_PAD = 0, 1, 2
_LOG2E = math.log2(math.e)


def _t5_bucket(dist):
    max_exact = REL_BUCKETS // 2
    d32 = jnp.maximum(dist, 1).astype(F32)
    large = max_exact + (jnp.log(d32 / max_exact) / math.log(REL_MAX_DIST / max_exact)
                         * (REL_BUCKETS - max_exact)).astype(I32)
    large = jnp.minimum(large, REL_BUCKETS - 1)
    return jnp.where(dist < max_exact, dist, large)


def _dsa_kernel(relb_ref, qi_ref, wi_ref, q_ref, ki2_ref, k_ref, vt_ref, o_ref,
                s_sc, sb_sc, bias_sc, smask_sc, lg_sc, qim_sc, qm_sc, m_sc, acc_sc,
                *, tq, topk, idx_scale):
    tk = tq
    i = pl.program_id(0)
    krow = lax.broadcasted_iota(I32, (tk, tq), 0)
    qcol = lax.broadcasted_iota(I32, (tk, tq), 1)
    causal = krow <= qcol
    lane = lax.broadcasted_iota(I32, (tq, LANES), 1)
    lo_half = lane < ATTN_HEAD_DIM

    @pl.when(i == 0)
    def _():
        for which in range(2):
            bucket = _t5_bucket(jnp.maximum(qcol - krow + which * tq, 0))
            for h in range(ATTN_HEADS):
                b = jnp.zeros((tk, tq), F32)
                for bk in range(REL_BUCKETS):
                    b = jnp.where(bucket == bk, relb_ref[bk, h] * _LOG2E, b)
                bias_sc[h, which] = (jnp.where(causal, b, NEG_INF) if which == 0 else b).astype(BF16)
        for h in range(ATTN_HEADS):
            bias_sc[h, _KIND_FAR] = jnp.full((tk, tq), relb_ref[REL_BUCKETS - 1, h] * _LOG2E, F32).astype(BF16)
            bias_sc[h, _KIND_PAD] = jnp.full((tk, tq), NEG_INF, BF16)
        smask_sc[_SMASK_FULL] = jnp.zeros((tk, tq), F32)
        smask_sc[_SMASK_DIAG] = jnp.where(causal, 0.0, NEG_INF)
        smask_sc[_SMASK_PAD] = jnp.full((tk, tq), NEG_INF, F32)

    def head_operands(src_ref, dst_sc, n_heads):
        src = src_ref[...].astype(F32)
        for h in range(n_heads):
            pair = src[:, LANES * (h // 2):LANES * (h // 2 + 1)]
            pair = jnp.where(lo_half if h % 2 == 0 else ~lo_half, pair, 0.0)
            dst_sc[h] = jnp.transpose(pair).astype(BF16)

    head_operands(qi_ref, qim_sc, IDX_HEADS)
    head_operands(q_ref, qm_sc, ATTN_HEADS)

    wt = jnp.transpose(wi_ref[...])

    def key_rows(jb):
        return pl.ds(pl.multiple_of(jb * tk, tk), tk)

    def fold(x):
        return x.reshape(tk // SUBLANES, SUBLANES, tq)

    def index_scores(jb):
        kb = ki2_ref[key_rows(jb), :]
        acc = jnp.zeros((tk, tq), F32)
        for h in range(IDX_HEADS):
            acc = acc + wt[h:h + 1, :] * jnp.maximum(_dot(kb, qim_sc[h]), 0.0)
        return acc * idx_scale

    npair = (i + 2) // 2
    last_kv = k_ref.shape[0] // tk - 1

    def p1_pair(j, carry):
        rmin, rmax = carry
        for jb in (2 * j, 2 * j + 1):
            raw = index_scores(jnp.minimum(jb, last_kv))
            kind = jnp.where(jb > i, _SMASK_PAD, jnp.where(jb == i, _SMASK_DIAG, _SMASK_FULL))
            sc = raw + smask_sc[kind]
            s_sc[jb] = sc
            sb_sc[jb] = sc.astype(BF16)
            rmin = jnp.minimum(rmin, jnp.min(fold(raw), axis=0))
            rmax = jnp.maximum(rmax, jnp.max(fold(raw), axis=0))
        return rmin, rmax

    rmin, rmax = lax.fori_loop(0, npair, p1_pair, (jnp.full((SUBLANES, tq), POS_INF, F32),
                                                   jnp.full((SUBLANES, tq), NEG_INF, F32)))
    rowmin = jnp.min(rmin, axis=0, keepdims=True)
    rowmax = jnp.max(rmax, axis=0, keepdims=True)

    kf = float(topk)
    nfin = (i * tq + lax.broadcasted_iota(I32, (1, tq), 1) + 1).astype(F32)
    small = nfin <= kf

    def walk_blocks(blocks_fn, init):
        return lax.fori_loop(0, npair, lambda j, c: blocks_fn([2 * j, 2 * j + 1], c), init)

    def count_ge(x):
        def blocks_fn(jbs, cnt):
            for jb in jbs:
                cnt = cnt + jnp.sum(fold(jnp.where(s_sc[jb] >= x, 1.0, 0.0)), axis=0)
            return cnt
        return jnp.sum(walk_blocks(blocks_fn, jnp.zeros((SUBLANES, tq), F32)), axis=0, keepdims=True)

    def count_ge16(xb):
        one, zero = jnp.ones((), BF16), jnp.zeros((), BF16)
        rows16 = 2 * SUBLANES

        def blocks_fn(jbs, cnt):
            parts = []
            for jb in jbs:
                c = jnp.where(sb_sc[jb] >= xb, one, zero)
                parts += [c[r0:r0 + rows16, :] for r0 in range(0, tk, rows16)]
            while len(parts) > 1:
                parts = [a + b for a, b in zip(parts[0::2], parts[1::2])]
            return cnt + parts[0].astype(F32)
        return jnp.sum(walk_blocks(blocks_fn, jnp.zeros((rows16, tq), F32)), axis=0, keepdims=True)

    def coarse_probe(mid):
        xb = mid.astype(BF16)
        xbits = lax.bitcast_convert_type(xb.astype(F32), I32)
        neg = xbits < 0
        mu = jnp.where(neg, xbits + 0x8000, xbits - 0x8000)
        odd = (xbits & 0x10000) != 0
        tbits = jnp.where(odd, jnp.where(neg, mu - 1, mu + 1), mu)
        t = lax.bitcast_convert_type(tbits, F32)
        ok = jnp.logical_and(jnp.abs(xb.astype(F32)) > _COARSE_MIN_ABS, jnp.abs(xb.astype(F32)) < _COARSE_MAX_ABS)
        return xb, t, ok

    def narrow(st, probe, c, active):
        lo, hi, clo, chi, done = st
        up = jnp.logical_and(active, c >= kf)
        dn = jnp.logical_and(active, c < kf)
        return (jnp.where(up, probe, lo), jnp.where(dn, probe, hi), jnp.where(up, c, clo),
                jnp.where(dn, c, chi), jnp.where(jnp.logical_and(active, c == kf), 1.0, done))

    def coarse_body(_, st):
        lo, hi, _, _, done = st
        xb, t, ok = coarse_probe(0.5 * lo + 0.5 * hi)
        ok = jnp.logical_and(ok, jnp.logical_and(t > lo, t < hi))
        return narrow(st, t, count_ge16(xb), jnp.logical_and(done < 0.5, ok))

    def fine_cond(st):
        it, (_, _, _, _, done) = st
        return jnp.logical_and(it < _BISECT_MAX_ITERS, jnp.min(done) < 0.5)

    def fine_body(st):
        it, inner = st
        lo, hi, _, _, done = inner
        mid = 0.5 * lo + 0.5 * hi
        mid = jnp.where(jnp.logical_and(lo == 0.0, hi > _F32_TINY), _F32_TINY, mid)
        mid = jnp.where(jnp.logical_and(lo < 0.0, hi > 0.0), 0.0, mid)
        stuck = jnp.logical_or(jnp.logical_or(mid <= lo, mid >= hi),
                               jnp.logical_and(lo == 0.0, hi <= _F32_TINY))
        lo, hi, clo, chi, done = narrow(inner, mid, count_ge(mid),
                                        jnp.logical_and(done < 0.5, jnp.logical_not(stuck)))
        return it + 1, (lo, hi, clo, chi, jnp.where(stuck, 1.0, done))

    mbits = lax.bitcast_convert_type(rowmax, I32)
    above_max = jnp.where(rowmax == 0.0, _F32_TINY,
                          lax.bitcast_convert_type(jnp.where(rowmax > 0.0, mbits + 1, mbits - 1), F32))
    st0 = (jnp.where(small, NEG_INF, rowmin), above_max, nfin, jnp.zeros((1, tq), F32),
           jnp.where(small, 1.0, 0.0))
    st1 = lax.fori_loop(0, _COARSE_PASSES, coarse_body, st0)
    _, (lo, hi, clo, chi, _) = lax.while_loop(fine_cond, fine_body, (jnp.int32(0), st1))
    budget = kf - chi
    need_ties = jnp.max((clo - chi) - budget) > 0.5

    pairs = [slice(LANES * (h // 2), LANES * (h // 2 + 1)) for h in range(ATTN_HEADS)]

    def issue_qk(jb, slot):
        kb = k_ref[key_rows(jnp.minimum(jb, last_kv)), :]
        for h in range(ATTN_HEADS):
            lg_sc[slot, h] = _dot(kb[:, pairs[h]], qm_sc[h]).astype(BF16)

    def attend(jb, slot, thr, cum=None):
        s = s_sc[jb]
        if cum is None:
            sel = s >= thr
        else:
            cand = jnp.logical_and(s >= lo, s < hi)
            before = (qcol < krow).astype(BF16)
            rank = cum + _dot(before, jnp.where(cand, 1.0, 0.0).astype(BF16))
            sel = jnp.logical_or(s >= hi, jnp.logical_and(cand, rank < budget))
        kind = jnp.where(jb > i, _KIND_PAD, jnp.minimum(i - jb, _KIND_FAR))
        selm = jnp.where(sel, 0.0, NEG_INF).astype(BF16)
        probs, alphas = [], []
        for h in range(ATTN_HEADS):
            lg = lg_sc[slot, h] + bias_sc[h, kind] + selm
            m_old = m_sc[h]
            m_new = jnp.maximum(m_old, jnp.max(lg, axis=0, keepdims=True).astype(F32))
            m_safe = jnp.where(m_new == NEG_INF, 0.0, m_new)
            alphas.append(jnp.exp2(m_old - m_safe))
            probs.append(jnp.exp2(lg - m_safe.astype(BF16)))
            m_sc[h] = m_new
        vb = vt_ref[jnp.minimum(jb, last_kv)]
        for h in range(ATTN_HEADS):
            vrows = slice(_VT_ROWS * (h // 2), _VT_ROWS * (h // 2 + 1))
            acc_sc[h] = alphas[h] * acc_sc[h] + _dot(vb[vrows, :], probs[h])

    def run(with_ties):
        m_sc[...] = jnp.full(m_sc.shape, NEG_INF, F32)
        acc_sc[...] = jnp.zeros_like(acc_sc)
        issue_qk(0, 0)

        def pair(j, thr0, thr1, cum0=None, cum1=None):
            issue_qk(2 * j + 1, 1)
            attend(2 * j, 0, thr0, cum0)
            issue_qk(2 * j + 2, 0)
            attend(2 * j + 1, 1, thr1, cum1)

        if not with_ties:
            def pair_body(j, carry):
                pair(j, lo, lo)
                return carry
            lax.fori_loop(0, npair, pair_body, 0)
            return

        last_s = s_sc.shape[0] - 1

        def cand_count(jb):
            s = s_sc[jnp.minimum(jb, last_s)]
            cand = jnp.logical_and(s >= lo, s < hi)
            return jnp.sum(jnp.where(cand, 1.0, 0.0), axis=0, keepdims=True)

        def needs_ranking(cum0, ca, cb):
            cum1 = cum0 + ca
            runs_out = jnp.logical_or(jnp.logical_and(cum1 > budget, cum0 < budget),
                                      jnp.logical_and(cum1 + cb > budget, cum1 < budget))
            return (jnp.max(jnp.where(runs_out, 1.0, 0.0)) > 0.5).astype(I32)

        def tie_pair_body(j, carry):
            ranked, cum0, ca, cb = carry
            cum1 = cum0 + ca
            cum2 = cum1 + cb

            def variant(with_rank):
                def fn():
                    na, nb = cand_count(2 * j + 2), cand_count(2 * j + 3)
                    nxt = needs_ranking(cum2, na, nb)
                    if with_rank:
                        pair(j, None, None, cum0, cum1)
                    else:
                        pair(j, jnp.where(cum1 <= budget, lo, hi), jnp.where(cum2 <= budget, lo, hi))
                    return nxt, cum2, na, nb
                return fn
            return lax.cond(ranked == 1, variant(True), variant(False))

        c0, c1 = cand_count(0), cand_count(1)
        zero = jnp.zeros((1, tq), F32)
        lax.fori_loop(0, npair, tie_pair_body, (needs_ranking(zero, c0, c1), zero, c0, c1))

    @pl.when(need_ties)
    def _():
        run(True)

    @pl.when(jnp.logical_not(need_ties))
    def _():
        run(False)

    first_head = lax.broadcasted_iota(I32, (LANES, tq), 0) < ATTN_HEAD_DIM
    normed = [acc_sc[h, 0:LANES, :] / acc_sc[h, LANES:LANES + 1, :] for h in range(ATTN_HEADS)]
    for p in range(ATTN_HEADS // 2):
        pair_t = jnp.where(first_head, normed[2 * p], normed[2 * p + 1])
        o_ref[:, LANES * p:LANES * (p + 1)] = jnp.transpose(pair_t)


def _dsa_call(rel_bias, qi, wi, q, ki2, k, vt, *, tq):
    s = q.shape[0]
    assert tq > REL_MAX_DIST and s % tq == 0
    nb = s // tq
    topk = min(TOPK_MAX, s // 4)
    kern = functools.partial(_dsa_kernel, tq=tq, topk=topk,
                             idx_scale=(IDX_DIM ** -0.5) * (IDX_HEADS ** -0.5))
    whole = lambda width: pl.BlockSpec((s, width), lambda i: (0, 0), pipeline_mode=pl.Buffered(1))
    assert vt.shape == (nb, (ATTN_WIDTH // LANES) * _VT_ROWS, tq)
    return pl.pallas_call(
        kern,
        grid=(nb,),
        in_specs=[pl.BlockSpec(memory_space=pltpu.SMEM),
                  pl.BlockSpec((tq, IDX_HEADS * IDX_DIM), lambda i: (i, 0)),
                  pl.BlockSpec((tq, LANES), lambda i: (i, 0)),
                  pl.BlockSpec((tq, ATTN_WIDTH), lambda i: (i, 0)),
                  whole(2 * IDX_DIM), whole(ATTN_WIDTH),
                  pl.BlockSpec(vt.shape, lambda i: (0, 0, 0), pipeline_mode=pl.Buffered(1))],
        out_specs=pl.BlockSpec((tq, ATTN_WIDTH), lambda i: (i, 0)),
        out_shape=jax.ShapeDtypeStruct((s, ATTN_WIDTH), F32),
        scratch_shapes=[pltpu.VMEM((nb + nb % 2, tq, tq), F32),
                        pltpu.VMEM((nb + nb % 2, tq, tq), BF16),
                        pltpu.VMEM((ATTN_HEADS, _N_KINDS, tq, tq), BF16),
                        pltpu.VMEM((3, tq, tq), F32),
                        pltpu.VMEM((2, ATTN_HEADS, tq, tq), BF16),
                        pltpu.VMEM((IDX_HEADS, LANES, tq), BF16),
                        pltpu.VMEM((ATTN_HEADS, LANES, tq), BF16),
                        pltpu.VMEM((ATTN_HEADS, 1, tq), F32),
                        pltpu.VMEM((ATTN_HEADS, _VT_ROWS, tq), F32)],
        compiler_params=_cparams("arbitrary"),
        name="dsa_attention",
    )(rel_bias, qi, wi, q, ki2, k, vt)


_HGRN_CHUNK = 128
_HGRN_BASE = SUBLANES


def _hgrn_kernel(h4_ref, lb_ref, ng_ref, o_ref, st_sc, *, layer, rows):
    c_len = _HGRN_CHUNK

    @pl.when(pl.program_id(0) == 0)
    def _():
        st_sc[...] = jnp.zeros_like(st_sc)

    lb_all = lb_ref[...]
    e = jnp.exp(lb_all - jnp.max(lb_all, axis=0, keepdims=True))
    sm = e / jnp.sum(e, axis=0, keepdims=True)
    lbs = jnp.sum(sm[0:layer + 1, :], axis=0, keepdims=True) - sm[0:1, :]

    r1 = lax.broadcasted_iota(I32, (c_len, 1), 0)
    rr = lax.broadcasted_iota(I32, (c_len, c_len), 0)
    cc = lax.broadcasted_iota(I32, (c_len, c_len), 1)

    def block_ref(a, blk, r):
        x = a.reshape(c_len // blk, blk, a.shape[-1])[:, r:r + 1, :]
        return jnp.broadcast_to(x, (c_len // blk, blk, a.shape[-1])).reshape(c_len, a.shape[-1])

    w = HGRN_WIDTH
    for c in range(rows // c_len):
        rs = slice(c * c_len, (c + 1) * c_len)
        hq = h4_ref[rs, 0:w]
        z = h4_ref[rs, w:2 * w]
        val = h4_ref[rs, 2 * w:3 * w]
        hg = h4_ref[rs, 3 * w:4 * w]
        key = (1.0 - lbs) * jax.nn.sigmoid(-z)
        g = jnp.log1p(-jnp.minimum(key, KEY_MAX))
        a_cum = g
        sh = 1
        while sh < c_len:
            a_cum = a_cum + jnp.where(r1 >= sh, pltpu.roll(a_cum, sh, 0), 0.0)
            sh *= 2
        qf = _silu(hq)
        outs = []
        for h in range(HGRN_HEADS):
            hs = slice(h * HGRN_DIM, (h + 1) * HGRN_DIM)
            ah, qh, kh, vh = a_cum[:, hs], qf[:, hs], key[:, hs], val[:, hs]
            vh16 = vh.astype(BF16)
            st = st_sc[h]
            o = _dot_nt((qh * jnp.exp(ah)).astype(BF16), st.astype(BF16))
            ref = block_ref(ah, _HGRN_BASE, _HGRN_BASE // 2 - 1)
            sc = _dot_nt((qh * jnp.exp(ah - ref)).astype(BF16), (kh * jnp.exp(ref - ah)).astype(BF16))
            same = (rr >> _log2(_HGRN_BASE)) == (cc >> _log2(_HGRN_BASE))
            scores = jnp.where(jnp.logical_and(same, cc <= rr), sc, 0.0)
            half = _HGRN_BASE
            while half < c_len:
                blk = 2 * half
                ref = block_ref(ah, blk, half - 1)
                second = (r1 & (blk - 1)) >= half
                ql = jnp.where(second, qh * jnp.exp(jnp.minimum(ah - ref, 0.0)), 0.0)
                kl = jnp.where(second, 0.0, kh * jnp.exp(jnp.minimum(ref - ah, 0.0)))
                sc = _dot_nt(ql.astype(BF16), kl.astype(BF16))
                scores = scores + jnp.where((rr >> _log2(blk)) == (cc >> _log2(blk)), sc, 0.0)
                half = blk
            o = o + _dot(scores.astype(BF16), vh16)
            a_last = ah[c_len - 1:c_len, :]
            kd = (kh * jnp.exp(a_last - ah)).astype(BF16)
            st_sc[h] = st * jnp.exp(a_last) + _dot_tn(vh16, kd)
            o = o * lax.rsqrt(jnp.mean(o * o, axis=-1, keepdims=True) + RMS_EPS)
            outs.append(o)
        o_ref[rs, :] = jnp.concatenate(outs, axis=1) * ng_ref[...] * _silu(hg)


def _hgrn_call(h4, lb, ng, *, layer, rows):
    s = h4.shape[0]
    kern = functools.partial(_hgrn_kernel, layer=layer, rows=rows)
    return pl.pallas_call(
        kern,
        grid=(s // rows,),
        in_specs=[pl.BlockSpec((rows, 4 * HGRN_WIDTH), lambda i: (i, 0)),
                  pl.BlockSpec(lb.shape, lambda i: (0, 0)),
                  _slab(ng, layer)],
        out_specs=pl.BlockSpec((rows, HGRN_WIDTH), lambda i: (i, 0)),
        out_shape=jax.ShapeDtypeStruct((s, HGRN_WIDTH), F32),
        scratch_shapes=[pltpu.VMEM((HGRN_HEADS, HGRN_DIM, HGRN_DIM), F32)],
        compiler_params=_cparams("arbitrary"),
        name="hgrn2",
    )(h4, lb, ng)


_POOL_HALO = 16


def _merge_kernel(x_ref, mod_ref, a_ref, ah_ref, yb_ref, yc_ref, g_ref, pw_ref, ps_ref, wb_ref, wo_ref,
                  lng_ref, lnb_ref, fwi_ref, fwo_ref, o_ref, *, tm, tf, alpha):
    i = pl.program_id(0)
    d = x_ref.shape[1]
    _, _, gate = _mod_rows(mod_ref, 1)
    a = a_ref[...]
    halo = jnp.where(i == 0, 0.0, ah_ref[...])
    ext = jnp.concatenate([halo, a], axis=0)
    lane = lax.broadcasted_iota(I32, (tm, POOL_WIDTH), 1)
    grp = lane >> _log2(POOL_GROUP)
    pos1 = (i * tm + lax.broadcasted_iota(I32, (tm, POOL_WIDTH), 0) + 1).astype(F32)
    win = jnp.zeros((tm, POOL_WIDTH), F32)
    cnt = jnp.zeros((tm, POOL_WIDTH), F32)
    ssum, have = ext, 1
    for gi, wd in enumerate(POOL_WINDOWS):
        while have < wd:
            ssum = ssum + pltpu.roll(ssum, have, 0)
            have *= 2
        assert have == wd
        win = jnp.where(grp == gi, ssum[_POOL_HALO:, :], win)
        cnt = jnp.where(grp == gi, jnp.minimum(pos1, float(wd)), cnt)
    dlt = (win / cnt - a).astype(BF16)
    ya = _dot(dlt, pw_ref[...]) * ps_ref[...]
    za = _dot(ya.astype(BF16), wb_ref[0:POOL_WIDTH, :])
    zb = _dot(yb_ref[...].astype(BF16), wb_ref[POOL_WIDTH:POOL_WIDTH + ATTN_WIDTH, :])
    zc = _dot(yc_ref[...].astype(BF16), wb_ref[POOL_WIDTH + ATTN_WIDTH:, :])
    g0, g1, g2 = (jax.nn.sigmoid(g_ref[:, j * d:(j + 1) * d].astype(F32)) for j in range(3))
    merged = g0 * za + g1 * zb + g2 * zc
    y = _dot(merged.astype(BF16), wo_ref[...])
    z = alpha * x_ref[...] + (1.0 + gate) * y
    xm = _layer_norm(z, lng_ref[1:2, :], lnb_ref[1:2, :])
    o_ref[...] = _ffn_rows(xm, mod_ref, fwi_ref, fwo_ref, lng_ref, lnb_ref, sub=2, tf=tf, alpha=alpha)


def _merge_call(x, mod, a, yb, yc, gates, pw_bd, ps, wb, wo, ln_g, ln_b, ffn_wi, ffn_wo, *, layer, tm, tf, alpha):
    s, d = x.shape
    kern = functools.partial(_merge_kernel, tm=tm, tf=tf, alpha=alpha)
    full = lambda arr: _slab(arr, layer)
    hb = tm // _POOL_HALO
    return pl.pallas_call(
        kern,
        grid=(s // tm,),
        in_specs=[pl.BlockSpec((tm, d), lambda i: (i, 0)),
                  pl.BlockSpec((3 * N_SUB, d), lambda i: (0, 0)),
                  pl.BlockSpec((tm, POOL_WIDTH), lambda i: (i, 0)),
                  pl.BlockSpec((_POOL_HALO, POOL_WIDTH), lambda i: (jnp.maximum(i * hb - 1, 0), 0)),
                  pl.BlockSpec((tm, ATTN_WIDTH), lambda i: (i, 0)),
                  pl.BlockSpec((tm, HGRN_WIDTH), lambda i: (i, 0)),
                  pl.BlockSpec((tm, 3 * d), lambda i: (i, 0)),
                  full(pw_bd), full(ps), full(wb), full(wo), full(ln_g), full(ln_b),
                  _slab(ffn_wi, layer, 1), _slab(ffn_wo, layer, 1)],
        out_specs=pl.BlockSpec((tm, d), lambda i: (i, 0)),
        out_shape=jax.ShapeDtypeStruct((s, d), F32),
        compiler_params=_cparams("arbitrary"),
        name="merge_ffn",
    )(x, mod, a, a, yb, yc, gates, pw_bd, ps, wb, wo, ln_g, ln_b, ffn_wi, ffn_wo)


def _pool_blockdiag(pool_w):
    depth, n, g, _ = pool_w.shape
    eye = jnp.eye(n, dtype=pool_w.dtype)
    return jnp.einsum("lgij,gh->lgihj", pool_w, eye).reshape(depth, n * g, n * g)


def kernel(x, c, w_ada, b_ada, ln_g, ln_b, ffn_w_in, ffn_w_out, mix_w_in, pool_w, pool_scale, rel_bias,
           hgrn_lb, hgrn_norm_g, w_branch, w_out):
    batch, s, d = x.shape
    assert batch == 1, "kernels are written for a single sequence"
    depth = w_ada.shape[0]
    alpha = (2 * depth) ** 0.25
    tm_ffn = min(512, s)
    tf = 256
    tq = min(256, s)
    tm_proj = tq
    tm_merge = min(512, s)
    hgrn_rows = min(256, s)

    ffn_wi16, ffn_wo16 = ffn_w_in.astype(BF16), ffn_w_out.astype(BF16)
    mix_w16 = _proj_weight(mix_w_in)
    pool_bd16 = _pool_blockdiag(pool_w).astype(BF16)
    wb16, wo16 = w_branch.astype(BF16), w_out.astype(BF16)
    pool_scale3 = pool_scale.reshape(depth, 1, -1)
    norm_g3 = hgrn_norm_g.reshape(depth, 1, -1)

    mods = _ada_call(c, w_ada, b_ada)
    xs = x.reshape(s, d)
    for l in range(depth):
        mod = mods[l].reshape(3 * N_SUB, d)
        xs = _ffn_call(xs, mod, ffn_wi16, ffn_wo16, ln_g, ln_b, layer=l, which=0, sub=0, alpha=alpha,
                       tm=tm_ffn, tf=tf)
        gates, a, h4, qi, q, k, vt, ki2, wi = _proj_call(xs, mod, mix_w16, layer=l, tm=tm_proj)
        yb = _dsa_call(rel_bias, qi, wi, q, ki2, k, vt, tq=tq)
        yc = _hgrn_call(h4, hgrn_lb, norm_g3, layer=l, rows=hgrn_rows)
        xs = _merge_call(xs, mod, a, yb, yc, gates, pool_bd16, pool_scale3, wb16, wo16, ln_g, ln_b,
                         ffn_wi16, ffn_wo16, layer=l, tm=tm_merge, tf=tf, alpha=alpha)
    return xs.reshape(batch, s, d)
```

```python
import functools
import math

import jax
import jax.numpy as jnp
from jax import lax
from jax.experimental import pallas as pl
from jax.experimental.pallas import tpu as pltpu

F32 = jnp.float32
BF16 = jnp.bfloat16
I32 = jnp.int32

POOL_WINDOWS = (2, 4, 8, 16)
POOL_GROUP = 64
POOL_WIDTH = POOL_GROUP * len(POOL_WINDOWS)
ATTN_HEADS = 4
ATTN_HEAD_DIM = 64
ATTN_WIDTH = ATTN_HEADS * ATTN_HEAD_DIM
IDX_HEADS = 8
IDX_DIM = 64
TOPK_MAX = 256
HGRN_HEADS = 4
HGRN_DIM = 128
HGRN_WIDTH = HGRN_HEADS * HGRN_DIM
KEY_MAX = 1.0 - 1e-6
REL_BUCKETS = 32
REL_MAX_DIST = 128
N_SUB = 3
LN_EPS = 1e-5
RMS_EPS = 1e-6

LANES = 128
SUBLANES = 8
VMEM_LIMIT_BYTES = 58 * 1024 * 1024

NEG_INF = float("-inf")
POS_INF = float("inf")


def _cparams(*sem):
    return pltpu.CompilerParams(dimension_semantics=sem, vmem_limit_bytes=VMEM_LIMIT_BYTES)


def _dot(a, b):
    return jnp.dot(a, b, preferred_element_type=F32)


def _dot_nt(a, b):
    return lax.dot_general(a, b, (((1,), (1,)), ((), ())), preferred_element_type=F32)


def _dot_tn(a, b):
    return lax.dot_general(a, b, (((0,), (0,)), ((), ())), preferred_element_type=F32)


def _log2(n):
    assert n & (n - 1) == 0
    return n.bit_length() - 1


def _silu(x):
    return x * jax.nn.sigmoid(x)


def _mod_rows(mod_ref, j):
    return mod_ref[3 * j:3 * j + 1, :], mod_ref[3 * j + 1:3 * j + 2, :], mod_ref[3 * j + 2:3 * j + 3, :]


def _layer_norm(z, g, b):
    mu = jnp.mean(z, axis=-1, keepdims=True)
    zc = z - mu
    var = jnp.mean(zc * zc, axis=-1, keepdims=True)
    return zc * lax.rsqrt(var + LN_EPS) * g + b


def _ada_kernel(c_ref, w_ref, b_ref, o_ref):
    @pl.when(pl.program_id(1) == 0)
    def _():
        o_ref[0] = b_ref[0]

    c = c_ref[...]
    o_ref[0] += jnp.sum(_silu(c) * w_ref[0], axis=0, keepdims=True)


def _ada_call(c, w_ada, b_ada):
    depth, d, n = w_ada.shape
    rows = LANES
    return pl.pallas_call(
        _ada_kernel,
        grid=(depth, d // rows),
        in_specs=[pl.BlockSpec((rows, 1), lambda l, k: (k, 0)),
                  pl.BlockSpec((1, rows, n), lambda l, k: (l, k, 0)),
                  pl.BlockSpec((1, 1, n), lambda l, k: (l, 0, 0))],
        out_specs=pl.BlockSpec((1, 1, n), lambda l, k: (l, 0, 0)),
        out_shape=jax.ShapeDtypeStruct((depth, 1, n), F32),
        compiler_params=_cparams("arbitrary", "arbitrary"),
        name="ada_mod",
    )(c.reshape(d, 1), w_ada, b_ada.reshape(depth, 1, n))


def _ffn_rows(x, mod_ref, wi_ref, wo_ref, lng_ref, lnb_ref, *, sub, tf, alpha):
    shift, scale, gate = _mod_rows(mod_ref, sub)
    ff = wo_ref.shape[0]
    h = (x * (1.0 + scale) + shift).astype(BF16)
    y = jnp.zeros(x.shape, F32)
    for c0 in range(0, ff, tf):
        g = _dot(h, wi_ref[:, c0:c0 + tf])
        u = _dot(h, wi_ref[:, ff + c0:ff + c0 + tf])
        y = y + _dot((_silu(g) * u).astype(BF16), wo_ref[c0:c0 + tf, :])
    z = alpha * x + (0.5 * (1.0 + gate)) * y
    return _layer_norm(z, lng_ref[sub:sub + 1, :], lnb_ref[sub:sub + 1, :])


def _ffn_kernel(x_ref, mod_ref, wi_ref, wo_ref, lng_ref, lnb_ref, o_ref, *, sub, tf, alpha):
    o_ref[...] = _ffn_rows(x_ref[...], mod_ref, wi_ref, wo_ref, lng_ref, lnb_ref, sub=sub, tf=tf, alpha=alpha)


def _ffn_call(x, mod, w_in, w_out, ln_g, ln_b, *, layer, which, sub, alpha, tm, tf):
    s, d = x.shape
    assert w_out.shape[2] % tf == 0
    kern = functools.partial(_ffn_kernel, sub=sub, tf=tf, alpha=alpha)
    return pl.pallas_call(
        kern,
        grid=(s // tm,),
        in_specs=[pl.BlockSpec((tm, d), lambda i: (i, 0)),
                  pl.BlockSpec((3 * N_SUB, d), lambda i: (0, 0)),
                  _slab(w_in, layer, which), _slab(w_out, layer, which),
                  _slab(ln_g, layer), _slab(ln_b, layer)],
        out_specs=pl.BlockSpec((tm, d), lambda i: (i, 0)),
        out_shape=jax.ShapeDtypeStruct((s, d), F32),
        compiler_params=_cparams("arbitrary"),
        name="ffn",
    )(x, mod, w_in, w_out, ln_g, ln_b)


_PROJ_OUTS = (("gates", 3 * 1024, BF16), ("a", POOL_WIDTH, F32), ("h4", 4 * HGRN_WIDTH, F32),
              ("qi", IDX_HEADS * IDX_DIM, BF16), ("q", ATTN_WIDTH, BF16), ("k", ATTN_WIDTH, BF16),
              ("v", ATTN_WIDTH, BF16), ("ki2", 2 * IDX_DIM, BF16), ("wi", LANES, F32))
_PROJ_CHUNK = 1024


def _proj_kernel(x_ref, mod_ref, w_ref, *o_refs):
    shift, scale, _ = _mod_rows(mod_ref, 1)
    h = (x_ref[...] * (1.0 + scale) + shift).astype(BF16)
    c0 = 0
    for (name, width, dt), o_ref in zip(_PROJ_OUTS, o_refs):
        if name == "v":
            vt = lax.dot_general(w_ref[:, c0:c0 + width], h, (((0,), (1,)), ((), ())),
                                 preferred_element_type=F32)
            ones = jnp.ones((_VT_ROWS - ATTN_HEAD_DIM, vt.shape[1]), F32)
            rows = []
            for p0 in range(0, width, ATTN_HEAD_DIM):
                rows += [vt[p0:p0 + ATTN_HEAD_DIM, :], ones]
            o_ref[0] = jnp.concatenate(rows, axis=0).astype(dt)
        else:
            for cc in range(0, width, _PROJ_CHUNK):
                cw = min(_PROJ_CHUNK, width - cc)
                o_ref[:, cc:cc + cw] = _dot(h, w_ref[:, c0 + cc:c0 + cc + cw]).astype(dt)
        c0 += width


def _proj_weight_kernel(w_ref, o_ref):
    out_width = {name: wd for name, wd, _ in _PROJ_OUTS}
    src = {}
    c0 = 0
    for name, wd in (("a", POOL_WIDTH), ("q", ATTN_WIDTH), ("k", ATTN_WIDTH), ("v", ATTN_WIDTH),
                     ("qi", IDX_HEADS * IDX_DIM), ("ki", IDX_DIM), ("wi", IDX_HEADS),
                     ("h4", 4 * HGRN_WIDTH), ("gates", out_width["gates"])):
        src[name] = (c0, wd)
        c0 += wd
    assert c0 == w_ref.shape[-1]
    q_scale = ATTN_HEAD_DIM ** -0.5 * _LOG2E
    dst = 0
    for name, width, _ in _PROJ_OUTS:
        if name == "ki2":
            pieces = [("ki", 1.0), ("ki", 1.0)]
        elif name == "wi":
            o_ref[:, dst:dst + width] = jnp.zeros((o_ref.shape[0], width), BF16)
            pieces = [("wi", 1.0)]
        else:
            pieces = [(name, q_scale if name == "q" else 1.0)]
        off = dst
        for pname, scale in pieces:
            s0, wd = src[pname]
            o_ref[:, off:off + wd] = (w_ref[:, s0:s0 + wd] * scale).astype(BF16)
            off += wd
        dst += width


def _proj_weight(w):
    depth, d, n_in = w.shape
    n_out = sum(wd for _, wd, _ in _PROJ_OUTS)
    rows = 256
    return pl.pallas_call(
        _proj_weight_kernel,
        grid=(depth, d // rows),
        in_specs=[pl.BlockSpec((None, rows, n_in), lambda l, r: (l, r, 0))],
        out_specs=pl.BlockSpec((None, rows, n_out), lambda l, r: (l, r, 0)),
        out_shape=jax.ShapeDtypeStruct((depth, d, n_out), BF16),
        compiler_params=_cparams("arbitrary", "arbitrary"),
        name="proj_weight",
    )(w)


def _slab(arr, *lead):
    block = (None,) * len(lead) + tuple(arr.shape[len(lead):])
    index = tuple(lead) + (0,) * (arr.ndim - len(lead))
    return pl.BlockSpec(block, lambda i: index, pipeline_mode=pl.Buffered(1))


def _proj_call(x, mod, w_all, *, layer, tm):
    s, d = x.shape
    n = w_all.shape[-1]
    assert n == sum(wd for _, wd, _ in _PROJ_OUTS)
    n_vt = ATTN_HEADS * _VT_ROWS
    out_specs = [pl.BlockSpec((1, n_vt, tm), lambda i: (i, 0, 0)) if name == "v"
                 else pl.BlockSpec((tm, wd), lambda i: (i, 0)) for name, wd, _ in _PROJ_OUTS]
    out_shape = [jax.ShapeDtypeStruct((s // tm, n_vt, tm) if name == "v" else (s, wd), dt)
                 for name, wd, dt in _PROJ_OUTS]
    return pl.pallas_call(
        _proj_kernel,
        grid=(s // tm,),
        in_specs=[pl.BlockSpec((tm, d), lambda i: (i, 0)),
                  pl.BlockSpec((3 * N_SUB, d), lambda i: (0, 0)),
                  _slab(w_all, layer)],
        out_specs=out_specs,
        out_shape=out_shape,
        compiler_params=_cparams("arbitrary"),
        name="mix_proj",
    )(x, mod, w_all)


_BISECT_MAX_ITERS = 320
_F32_TINY = float(jnp.finfo(jnp.float32).tiny)
_COARSE_PASSES = 9
_COARSE_MIN_ABS, _COARSE_MAX_ABS = 1e-30, 1e30
_KIND_FAR, _KIND_PAD, _N_KINDS = 2, 3, 4
_VT_ROWS = ATTN_HEAD_DIM + 16
_SMASK_FULL, _SMASK_DIAG, _SMASK_PAD = 0, 1, 2
_LOG2E = math.log2(math.e)


def _t5_bucket(dist):
    max_exact = REL_BUCKETS // 2
    d32 = jnp.maximum(dist, 1).astype(F32)
    large = max_exact + (jnp.log(d32 / max_exact) / math.log(REL_MAX_DIST / max_exact)
                         * (REL_BUCKETS - max_exact)).astype(I32)
    large = jnp.minimum(large, REL_BUCKETS - 1)
    return jnp.where(dist < max_exact, dist, large)


def _dsa_kernel(relb_ref, qi_ref, wi_ref, q_ref, ki2_ref, k_ref, vt_ref, o_ref,
                s_sc, sb_sc, bias_sc, smask_sc, lg_sc, qim_sc, qm_sc, m_sc, acc_sc,
                *, tq, topk, idx_scale):
    tk = tq
    i = pl.program_id(0)
    krow = lax.broadcasted_iota(I32, (tk, tq), 0)
    qcol = lax.broadcasted_iota(I32, (tk, tq), 1)
    causal = krow <= qcol
    lane = lax.broadcasted_iota(I32, (tq, LANES), 1)
    lo_half = lane < ATTN_HEAD_DIM

    @pl.when(i == 0)
    def _():
        for which in range(2):
            bucket = _t5_bucket(jnp.maximum(qcol - krow + which * tq, 0))
            for h in range(ATTN_HEADS):
                b = jnp.zeros((tk, tq), F32)
                for bk in range(REL_BUCKETS):
                    b = jnp.where(bucket == bk, relb_ref[bk, h] * _LOG2E, b)
                bias_sc[h, which] = (jnp.where(causal, b, NEG_INF) if which == 0 else b).astype(BF16)
        for h in range(ATTN_HEADS):
            bias_sc[h, _KIND_FAR] = jnp.full((tk, tq), relb_ref[REL_BUCKETS - 1, h] * _LOG2E, F32).astype(BF16)
            bias_sc[h, _KIND_PAD] = jnp.full((tk, tq), NEG_INF, BF16)
        smask_sc[_SMASK_FULL] = jnp.zeros((tk, tq), F32)
        smask_sc[_SMASK_DIAG] = jnp.where(causal, 0.0, NEG_INF)
        smask_sc[_SMASK_PAD] = jnp.full((tk, tq), NEG_INF, F32)

    def head_operands(src_ref, dst_sc, n_heads):
        src = src_ref[...].astype(F32)
        for h in range(n_heads):
            pair = src[:, LANES * (h // 2):LANES * (h // 2 + 1)]
            pair = jnp.where(lo_half if h % 2 == 0 else ~lo_half, pair, 0.0)
            dst_sc[h] = jnp.transpose(pair).astype(BF16)

    head_operands(qi_ref, qim_sc, IDX_HEADS)
    head_operands(q_ref, qm_sc, ATTN_HEADS)

    wt = jnp.transpose(wi_ref[...])

    def key_rows(jb):
        return pl.ds(pl.multiple_of(jb * tk, tk), tk)

    def fold(x):
        return x.reshape(tk // SUBLANES, SUBLANES, tq)

    def index_scores(jb):
        kb = ki2_ref[key_rows(jb), :]
        acc = jnp.zeros((tk, tq), F32)
        for h in range(IDX_HEADS):
            acc = acc + wt[h:h + 1, :] * jnp.maximum(_dot(kb, qim_sc[h]), 0.0)
        return acc * idx_scale

    npair = (i + 2) // 2
    last_kv = k_ref.shape[0] // tk - 1

    def p1_pair(j, carry):
        rmin, rmax = carry
        for jb in (2 * j, 2 * j + 1):
            raw = index_scores(jnp.minimum(jb, last_kv))
            kind = jnp.where(jb > i, _SMASK_PAD, jnp.where(jb == i, _SMASK_DIAG, _SMASK_FULL))
            sc = raw + smask_sc[kind]
            s_sc[jb] = sc
            sb_sc[jb] = sc.astype(BF16)
            rmin = jnp.minimum(rmin, jnp.min(fold(raw), axis=0))
            rmax = jnp.maximum(rmax, jnp.max(fold(raw), axis=0))
        return rmin, rmax

    rmin, rmax = lax.fori_loop(0, npair, p1_pair, (jnp.full((SUBLANES, tq), POS_INF, F32),
                                                   jnp.full((SUBLANES, tq), NEG_INF, F32)))
    rowmin = jnp.min(rmin, axis=0, keepdims=True)
    rowmax = jnp.max(rmax, axis=0, keepdims=True)

    kf = float(topk)
    nfin = (i * tq + lax.broadcasted_iota(I32, (1, tq), 1) + 1).astype(F32)
    small = nfin <= kf

    def walk_blocks(blocks_fn, init):
        return lax.fori_loop(0, npair, lambda j, c: blocks_fn([2 * j, 2 * j + 1], c), init)

    def count_ge(x):
        def blocks_fn(jbs, cnt):
            for jb in jbs:
                cnt = cnt + jnp.sum(fold(jnp.where(s_sc[jb] >= x, 1.0, 0.0)), axis=0)
            return cnt
        return jnp.sum(walk_blocks(blocks_fn, jnp.zeros((SUBLANES, tq), F32)), axis=0, keepdims=True)

    def count_ge16(xb):
        one, zero = jnp.ones((), BF16), jnp.zeros((), BF16)
        rows16 = 2 * SUBLANES

        def blocks_fn(jbs, cnt):
            parts = []
            for jb in jbs:
                c = jnp.where(sb_sc[jb] >= xb, one, zero)
                parts += [c[r0:r0 + rows16, :] for r0 in range(0, tk, rows16)]
            while len(parts) > 1:
                parts = [a + b for a, b in zip(parts[0::2], parts[1::2])]
            return cnt + parts[0].astype(F32)
        return jnp.sum(walk_blocks(blocks_fn, jnp.zeros((rows16, tq), F32)), axis=0, keepdims=True)

    def coarse_probe(mid):
        xb = mid.astype(BF16)
        xbits = lax.bitcast_convert_type(xb.astype(F32), I32)
        neg = xbits < 0
        mu = jnp.where(neg, xbits + 0x8000, xbits - 0x8000)
        odd = (xbits & 0x10000) != 0
        tbits = jnp.where(odd, jnp.where(neg, mu - 1, mu + 1), mu)
        t = lax.bitcast_convert_type(tbits, F32)
        ok = jnp.logical_and(jnp.abs(xb.astype(F32)) > _COARSE_MIN_ABS, jnp.abs(xb.astype(F32)) < _COARSE_MAX_ABS)
        return xb, t, ok

    def narrow(st, probe, c, active):
        lo, hi, clo, chi, done = st
        up = jnp.logical_and(active, c >= kf)
        dn = jnp.logical_and(active, c < kf)
        return (jnp.where(up, probe, lo), jnp.where(dn, probe, hi), jnp.where(up, c, clo),
                jnp.where(dn, c, chi), jnp.where(jnp.logical_and(active, c == kf), 1.0, done))

    def coarse_body(_, st):
        lo, hi, _, _, done = st
        xb, t, ok = coarse_probe(0.5 * lo + 0.5 * hi)
        ok = jnp.logical_and(ok, jnp.logical_and(t > lo, t < hi))
        return narrow(st, t, count_ge16(xb), jnp.logical_and(done < 0.5, ok))

    def fine_cond(st):
        it, (_, _, _, _, done) = st
        return jnp.logical_and(it < _BISECT_MAX_ITERS, jnp.min(done) < 0.5)

    def fine_body(st):
        it, inner = st
        lo, hi, _, _, done = inner
        mid = 0.5 * lo + 0.5 * hi
        mid = jnp.where(jnp.logical_and(lo == 0.0, hi > _F32_TINY), _F32_TINY, mid)
        mid = jnp.where(jnp.logical_and(lo < 0.0, hi > 0.0), 0.0, mid)
        stuck = jnp.logical_or(jnp.logical_or(mid <= lo, mid >= hi),
                               jnp.logical_and(lo == 0.0, hi <= _F32_TINY))
        lo, hi, clo, chi, done = narrow(inner, mid, count_ge(mid),
                                        jnp.logical_and(done < 0.5, jnp.logical_not(stuck)))
        return it + 1, (lo, hi, clo, chi, jnp.where(stuck, 1.0, done))

    mbits = lax.bitcast_convert_type(rowmax, I32)
    above_max = jnp.where(rowmax == 0.0, _F32_TINY,
                          lax.bitcast_convert_type(jnp.where(rowmax > 0.0, mbits + 1, mbits - 1), F32))
    st0 = (jnp.where(small, NEG_INF, rowmin), above_max, nfin, jnp.zeros((1, tq), F32),
           jnp.where(small, 1.0, 0.0))
    st1 = lax.fori_loop(0, _COARSE_PASSES, coarse_body, st0)
    _, (lo, hi, clo, chi, _) = lax.while_loop(fine_cond, fine_body, (jnp.int32(0), st1))
    budget = kf - chi
    need_ties = jnp.max((clo - chi) - budget) > 0.5

    pairs = [slice(LANES * (h // 2), LANES * (h // 2 + 1)) for h in range(ATTN_HEADS)]

    def issue_qk(jb, slot):
        kb = k_ref[key_rows(jnp.minimum(jb, last_kv)), :]
        for h in range(ATTN_HEADS):
            lg_sc[slot, h] = _dot(kb[:, pairs[h]], qm_sc[h]).astype(BF16)

    def attend(jb, slot, thr, cum=None):
        s = s_sc[jb]
        if cum is None:
            sel = s >= thr
        else:
            cand = jnp.logical_and(s >= lo, s < hi)
            before = (qcol < krow).astype(BF16)
            rank = cum + _dot(before, jnp.where(cand, 1.0, 0.0).astype(BF16))
            sel = jnp.logical_or(s >= hi, jnp.logical_and(cand, rank < budget))
        kind = jnp.where(jb > i, _KIND_PAD, jnp.minimum(i - jb, _KIND_FAR))
        selm = jnp.where(sel, 0.0, NEG_INF).astype(BF16)
        probs, alphas = [], []
        for h in range(ATTN_HEADS):
            lg = lg_sc[slot, h] + bias_sc[h, kind] + selm
            m_old = m_sc[h]
            m_new = jnp.maximum(m_old, jnp.max(lg, axis=0, keepdims=True).astype(F32))
            m_safe = jnp.where(m_new == NEG_INF, 0.0, m_new)
            alphas.append(jnp.exp2(m_old - m_safe))
            probs.append(jnp.exp2(lg - m_safe.astype(BF16)))
            m_sc[h] = m_new
        vb = vt_ref[jnp.minimum(jb, last_kv)]
        for h in range(ATTN_HEADS):
            vrows = slice(_VT_ROWS * h, _VT_ROWS * (h + 1))
            acc_sc[h] = alphas[h] * acc_sc[h] + _dot(vb[vrows, :], probs[h])

    def run(with_ties):
        m_sc[...] = jnp.full(m_sc.shape, NEG_INF, F32)
        acc_sc[...] = jnp.zeros_like(acc_sc)
        issue_qk(0, 0)

        def pair(j, thr0, thr1, cum0=None, cum1=None):
            issue_qk(2 * j + 1, 1)
            attend(2 * j, 0, thr0, cum0)
            issue_qk(2 * j + 2, 0)
            attend(2 * j + 1, 1, thr1, cum1)

        if not with_ties:
            def pair_body(j, carry):
                pair(j, lo, lo)
                return carry
            lax.fori_loop(0, npair, pair_body, 0)
            return

        last_s = s_sc.shape[0] - 1

        def cand_count(jb):
            s = s_sc[jnp.minimum(jb, last_s)]
            cand = jnp.logical_and(s >= lo, s < hi)
            return jnp.sum(jnp.where(cand, 1.0, 0.0), axis=0, keepdims=True)

        def needs_ranking(cum0, ca, cb):
            cum1 = cum0 + ca
            runs_out = jnp.logical_or(jnp.logical_and(cum1 > budget, cum0 < budget),
                                      jnp.logical_and(cum1 + cb > budget, cum1 < budget))
            return (jnp.max(jnp.where(runs_out, 1.0, 0.0)) > 0.5).astype(I32)

        def tie_pair_body(j, carry):
            ranked, cum0, ca, cb = carry
            cum1 = cum0 + ca
            cum2 = cum1 + cb

            def variant(with_rank):
                def fn():
                    na, nb = cand_count(2 * j + 2), cand_count(2 * j + 3)
                    nxt = needs_ranking(cum2, na, nb)
                    if with_rank:
                        pair(j, None, None, cum0, cum1)
                    else:
                        pair(j, jnp.where(cum1 <= budget, lo, hi), jnp.where(cum2 <= budget, lo, hi))
                    return nxt, cum2, na, nb
                return fn
            return lax.cond(ranked == 1, variant(True), variant(False))

        c0, c1 = cand_count(0), cand_count(1)
        zero = jnp.zeros((1, tq), F32)
        lax.fori_loop(0, npair, tie_pair_body, (needs_ranking(zero, c0, c1), zero, c0, c1))

    @pl.when(need_ties)
    def _():
        run(True)

    @pl.when(jnp.logical_not(need_ties))
    def _():
        run(False)

    hd = ATTN_HEAD_DIM
    normed = [acc_sc[h, 0:hd, :] / acc_sc[h, hd:hd + 1, :] for h in range(ATTN_HEADS)]
    for p in range(ATTN_HEADS // 2):
        pair_t = jnp.concatenate([normed[2 * p], normed[2 * p + 1]], axis=0)
        o_ref[:, LANES * p:LANES * (p + 1)] = jnp.transpose(pair_t)


def _dsa_call(rel_bias, qi, wi, q, ki2, k, vt, *, tq):
    s = q.shape[0]
    assert tq > REL_MAX_DIST and s % tq == 0
    nb = s // tq
    topk = min(TOPK_MAX, s // 4)
    kern = functools.partial(_dsa_kernel, tq=tq, topk=topk,
                             idx_scale=(IDX_DIM ** -0.5) * (IDX_HEADS ** -0.5))
    whole = lambda width: pl.BlockSpec((s, width), lambda i: (0, 0), pipeline_mode=pl.Buffered(1))
    assert vt.shape == (nb, ATTN_HEADS * _VT_ROWS, tq)
    return pl.pallas_call(
        kern,
        grid=(nb,),
        in_specs=[pl.BlockSpec(memory_space=pltpu.SMEM),
                  pl.BlockSpec((tq, IDX_HEADS * IDX_DIM), lambda i: (i, 0)),
                  pl.BlockSpec((tq, LANES), lambda i: (i, 0)),
                  pl.BlockSpec((tq, ATTN_WIDTH), lambda i: (i, 0)),
                  whole(2 * IDX_DIM), whole(ATTN_WIDTH),
                  pl.BlockSpec(vt.shape, lambda i: (0, 0, 0), pipeline_mode=pl.Buffered(1))],
        out_specs=pl.BlockSpec((tq, ATTN_WIDTH), lambda i: (i, 0)),
        out_shape=jax.ShapeDtypeStruct((s, ATTN_WIDTH), F32),
        scratch_shapes=[pltpu.VMEM((nb + nb % 2, tq, tq), F32),
                        pltpu.VMEM((nb + nb % 2, tq, tq), BF16),
                        pltpu.VMEM((ATTN_HEADS, _N_KINDS, tq, tq), BF16),
                        pltpu.VMEM((3, tq, tq), F32),
                        pltpu.VMEM((2, ATTN_HEADS, tq, tq), BF16),
                        pltpu.VMEM((IDX_HEADS, LANES, tq), BF16),
                        pltpu.VMEM((ATTN_HEADS, LANES, tq), BF16),
                        pltpu.VMEM((ATTN_HEADS, 1, tq), F32),
                        pltpu.VMEM((ATTN_HEADS, _VT_ROWS, tq), F32)],
        compiler_params=_cparams("arbitrary"),
        name="dsa_attention",
    )(rel_bias, qi, wi, q, ki2, k, vt)


_HGRN_CHUNK = 128
_HGRN_BASE = SUBLANES


def _hgrn_kernel(h4_ref, lb_ref, ng_ref, o_ref, st_sc, *, layer, rows):
    c_len = _HGRN_CHUNK

    @pl.when(pl.program_id(0) == 0)
    def _():
        st_sc[...] = jnp.zeros_like(st_sc)

    lb_all = lb_ref[...]
    e = jnp.exp(lb_all - jnp.max(lb_all, axis=0, keepdims=True))
    sm = e / jnp.sum(e, axis=0, keepdims=True)
    lbs = jnp.sum(sm[0:layer + 1, :], axis=0, keepdims=True) - sm[0:1, :]

    r1 = lax.broadcasted_iota(I32, (c_len, 1), 0)
    rr = lax.broadcasted_iota(I32, (c_len, c_len), 0)
    cc = lax.broadcasted_iota(I32, (c_len, c_len), 1)

    def block_ref(a, blk, r):
        x = a.reshape(c_len // blk, blk, a.shape[-1])[:, r:r + 1, :]
        return jnp.broadcast_to(x, (c_len // blk, blk, a.shape[-1])).reshape(c_len, a.shape[-1])

    w = HGRN_WIDTH
    for c in range(rows // c_len):
        rs = slice(c * c_len, (c + 1) * c_len)
        hq = h4_ref[rs, 0:w]
        z = h4_ref[rs, w:2 * w]
        val = h4_ref[rs, 2 * w:3 * w]
        hg = h4_ref[rs, 3 * w:4 * w]
        key = (1.0 - lbs) * jax.nn.sigmoid(-z)
        g = jnp.log1p(-jnp.minimum(key, KEY_MAX))
        a_cum = g
        sh = 1
        while sh < c_len:
            a_cum = a_cum + jnp.where(r1 >= sh, pltpu.roll(a_cum, sh, 0), 0.0)
            sh *= 2
        qf = _silu(hq)
        outs = []
        for h in range(HGRN_HEADS):
            hs = slice(h * HGRN_DIM, (h + 1) * HGRN_DIM)
            ah, qh, kh, vh = a_cum[:, hs], qf[:, hs], key[:, hs], val[:, hs]
            vh16 = vh.astype(BF16)
            st = st_sc[h]
            o = _dot_nt((qh * jnp.exp(ah)).astype(BF16), st.astype(BF16))
            ref = block_ref(ah, _HGRN_BASE, _HGRN_BASE // 2 - 1)
            sc = _dot_nt((qh * jnp.exp(ah - ref)).astype(BF16), (kh * jnp.exp(ref - ah)).astype(BF16))
            same = (rr >> _log2(_HGRN_BASE)) == (cc >> _log2(_HGRN_BASE))
            scores = jnp.where(jnp.logical_and(same, cc <= rr), sc, 0.0)
            half = _HGRN_BASE
            while half < c_len:
                blk = 2 * half
                ref = block_ref(ah, blk, half - 1)
                second = (r1 & (blk - 1)) >= half
                ql = jnp.where(second, qh * jnp.exp(jnp.minimum(ah - ref, 0.0)), 0.0)
                kl = jnp.where(second, 0.0, kh * jnp.exp(jnp.minimum(ref - ah, 0.0)))
                sc = _dot_nt(ql.astype(BF16), kl.astype(BF16))
                scores = scores + jnp.where((rr >> _log2(blk)) == (cc >> _log2(blk)), sc, 0.0)
                half = blk
            o = o + _dot(scores.astype(BF16), vh16)
            a_last = ah[c_len - 1:c_len, :]
            kd = (kh * jnp.exp(a_last - ah)).astype(BF16)
            st_sc[h] = st * jnp.exp(a_last) + _dot_tn(vh16, kd)
            o = o * lax.rsqrt(jnp.mean(o * o, axis=-1, keepdims=True) + RMS_EPS)
            outs.append(o)
        o_ref[rs, :] = jnp.concatenate(outs, axis=1) * ng_ref[...] * _silu(hg)


def _hgrn_call(h4, lb, ng, *, layer, rows):
    s = h4.shape[0]
    kern = functools.partial(_hgrn_kernel, layer=layer, rows=rows)
    return pl.pallas_call(
        kern,
        grid=(s // rows,),
        in_specs=[pl.BlockSpec((rows, 4 * HGRN_WIDTH), lambda i: (i, 0)),
                  pl.BlockSpec(lb.shape, lambda i: (0, 0)),
                  _slab(ng, layer)],
        out_specs=pl.BlockSpec((rows, HGRN_WIDTH), lambda i: (i, 0)),
        out_shape=jax.ShapeDtypeStruct((s, HGRN_WIDTH), F32),
        scratch_shapes=[pltpu.VMEM((HGRN_HEADS, HGRN_DIM, HGRN_DIM), F32)],
        compiler_params=_cparams("arbitrary"),
        name="hgrn2",
    )(h4, lb, ng)


_POOL_HALO = 16


def _merge_kernel(x_ref, mod_ref, a_ref, ah_ref, yb_ref, yc_ref, g_ref, pw_ref, ps_ref, wb_ref, wo_ref,
                  lng_ref, lnb_ref, fwi_ref, fwo_ref, o_ref, *, tm, tf, alpha):
    i = pl.program_id(0)
    d = x_ref.shape[1]
    _, _, gate = _mod_rows(mod_ref, 1)
    a = a_ref[...]
    halo = jnp.where(i == 0, 0.0, ah_ref[...])
    ext = jnp.concatenate([halo, a], axis=0)
    lane = lax.broadcasted_iota(I32, (tm, POOL_WIDTH), 1)
    grp = lane >> _log2(POOL_GROUP)
    pos1 = (i * tm + lax.broadcasted_iota(I32, (tm, POOL_WIDTH), 0) + 1).astype(F32)
    win = jnp.zeros((tm, POOL_WIDTH), F32)
    cnt = jnp.zeros((tm, POOL_WIDTH), F32)
    ssum, have = ext, 1
    for gi, wd in enumerate(POOL_WINDOWS):
        while have < wd:
            ssum = ssum + pltpu.roll(ssum, have, 0)
            have *= 2
        assert have == wd
        win = jnp.where(grp == gi, ssum[_POOL_HALO:, :], win)
        cnt = jnp.where(grp == gi, jnp.minimum(pos1, float(wd)), cnt)
    dlt = (win / cnt - a).astype(BF16)
    ya = _dot(dlt, pw_ref[...]) * ps_ref[...]
    za = _dot(ya.astype(BF16), wb_ref[0:POOL_WIDTH, :])
    zb = _dot(yb_ref[...].astype(BF16), wb_ref[POOL_WIDTH:POOL_WIDTH + ATTN_WIDTH, :])
    zc = _dot(yc_ref[...].astype(BF16), wb_ref[POOL_WIDTH + ATTN_WIDTH:, :])
    g0, g1, g2 = (jax.nn.sigmoid(g_ref[:, j * d:(j + 1) * d].astype(F32)) for j in range(3))
    merged = g0 * za + g1 * zb + g2 * zc
    y = _dot(merged.astype(BF16), wo_ref[...])
    z = alpha * x_ref[...] + (1.0 + gate) * y
    xm = _layer_norm(z, lng_ref[1:2, :], lnb_ref[1:2, :])
    o_ref[...] = _ffn_rows(xm, mod_ref, fwi_ref, fwo_ref, lng_ref, lnb_ref, sub=2, tf=tf, alpha=alpha)


def _merge_call(x, mod, a, yb, yc, gates, pw_bd, ps, wb, wo, ln_g, ln_b, ffn_wi, ffn_wo, *, layer, tm, tf, alpha):
    s, d = x.shape
    kern = functools.partial(_merge_kernel, tm=tm, tf=tf, alpha=alpha)
    full = lambda arr: _slab(arr, layer)
    hb = tm // _POOL_HALO
    return pl.pallas_call(
        kern,
        grid=(s // tm,),
        in_specs=[pl.BlockSpec((tm, d), lambda i: (i, 0)),
                  pl.BlockSpec((3 * N_SUB, d), lambda i: (0, 0)),
                  pl.BlockSpec((tm, POOL_WIDTH), lambda i: (i, 0)),
                  pl.BlockSpec((_POOL_HALO, POOL_WIDTH), lambda i: (jnp.maximum(i * hb - 1, 0), 0)),
                  pl.BlockSpec((tm, ATTN_WIDTH), lambda i: (i, 0)),
                  pl.BlockSpec((tm, HGRN_WIDTH), lambda i: (i, 0)),
                  pl.BlockSpec((tm, 3 * d), lambda i: (i, 0)),
                  full(pw_bd), full(ps), full(wb), full(wo), full(ln_g), full(ln_b),
                  _slab(ffn_wi, layer, 1), _slab(ffn_wo, layer, 1)],
        out_specs=pl.BlockSpec((tm, d), lambda i: (i, 0)),
        out_shape=jax.ShapeDtypeStruct((s, d), F32),
        compiler_params=_cparams("arbitrary"),
        name="merge_ffn",
    )(x, mod, a, a, yb, yc, gates, pw_bd, ps, wb, wo, ln_g, ln_b, ffn_wi, ffn_wo)


def _pool_blockdiag(pool_w):
    depth, n, g, _ = pool_w.shape
    eye = jnp.eye(n, dtype=pool_w.dtype)
    return jnp.einsum("lgij,gh->lgihj", pool_w, eye).reshape(depth, n * g, n * g)


def kernel(x, c, w_ada, b_ada, ln_g, ln_b, ffn_w_in, ffn_w_out, mix_w_in, pool_w, pool_scale, rel_bias,
           hgrn_lb, hgrn_norm_g, w_branch, w_out):
    batch, s, d = x.shape
    assert batch == 1, "kernels are written for a single sequence"
    depth = w_ada.shape[0]
    alpha = (2 * depth) ** 0.25
    tm_ffn = min(512, s)
    tf = 256
    tq = min(256, s)
    tm_proj = tq
    tm_merge = min(512, s)
    hgrn_rows = min(256, s)

    ffn_wi16, ffn_wo16 = ffn_w_in.astype(BF16), ffn_w_out.astype(BF16)
    mix_w16 = _proj_weight(mix_w_in)
    pool_bd16 = _pool_blockdiag(pool_w).astype(BF16)
    wb16, wo16 = w_branch.astype(BF16), w_out.astype(BF16)
    pool_scale3 = pool_scale.reshape(depth, 1, -1)
    norm_g3 = hgrn_norm_g.reshape(depth, 1, -1)

    mods = _ada_call(c, w_ada, b_ada)
    xs = x.reshape(s, d)
    for l in range(depth):
        mod = mods[l].reshape(3 * N_SUB, d)
        xs = _ffn_call(xs, mod, ffn_wi16, ffn_wo16, ln_g, ln_b, layer=l, which=0, sub=0, alpha=alpha,
                       tm=tm_ffn, tf=tf)
        gates, a, h4, qi, q, k, vt, ki2, wi = _proj_call(xs, mod, mix_w16, layer=l, tm=tm_proj)
        yb = _dsa_call(rel_bias, qi, wi, q, ki2, k, vt, tq=tq)
        yc = _hgrn_call(h4, hgrn_lb, norm_g3, layer=l, rows=hgrn_rows)
        xs = _merge_call(xs, mod, a, yb, yc, gates, pool_bd16, pool_scale3, wb16, wo16, ln_g, ln_b,
                         ffn_wi16, ffn_wo16, layer=l, tm=tm_merge, tf=tf, alpha=alpha)
    return xs.reshape(batch, s, d)
```

```python
import functools
import math

import jax
import jax.numpy as jnp
from jax import lax
from jax.experimental import pallas as pl
from jax.experimental.pallas import tpu as pltpu

F32 = jnp.float32
BF16 = jnp.bfloat16
I32 = jnp.int32

POOL_WINDOWS = (2, 4, 8, 16)
POOL_GROUP = 64
POOL_WIDTH = POOL_GROUP * len(POOL_WINDOWS)
ATTN_HEADS = 4
ATTN_HEAD_DIM = 64
ATTN_WIDTH = ATTN_HEADS * ATTN_HEAD_DIM
IDX_HEADS = 8
IDX_DIM = 64
TOPK_MAX = 256
HGRN_HEADS = 4
HGRN_DIM = 128
HGRN_WIDTH = HGRN_HEADS * HGRN_DIM
KEY_MAX = 1.0 - 1e-6
REL_BUCKETS = 32
REL_MAX_DIST = 128
N_SUB = 3
LN_EPS = 1e-5
RMS_EPS = 1e-6

LANES = 128
SUBLANES = 8
VMEM_LIMIT_BYTES = 58 * 1024 * 1024

NEG_INF = float("-inf")
POS_INF = float("inf")


def _cparams(*sem):
    return pltpu.CompilerParams(dimension_semantics=sem, vmem_limit_bytes=VMEM_LIMIT_BYTES)


def _dot(a, b):
    return jnp.dot(a, b, preferred_element_type=F32)


def _dot_nt(a, b):
    return lax.dot_general(a, b, (((1,), (1,)), ((), ())), preferred_element_type=F32)


def _dot_tn(a, b):
    return lax.dot_general(a, b, (((0,), (0,)), ((), ())), preferred_element_type=F32)


def _log2(n):
    assert n & (n - 1) == 0
    return n.bit_length() - 1


def _silu(x):
    return x * jax.nn.sigmoid(x)


def _mod_rows(mod_ref, j):
    return mod_ref[3 * j:3 * j + 1, :], mod_ref[3 * j + 1:3 * j + 2, :], mod_ref[3 * j + 2:3 * j + 3, :]


def _layer_norm(z, g, b):
    mu = jnp.mean(z, axis=-1, keepdims=True)
    zc = z - mu
    var = jnp.mean(zc * zc, axis=-1, keepdims=True)
    return zc * lax.rsqrt(var + LN_EPS) * g + b


def _ada_kernel(c_ref, w_ref, b_ref, o_ref):
    @pl.when(pl.program_id(1) == 0)
    def _():
        o_ref[0] = b_ref[0]

    c = c_ref[...]
    o_ref[0] += jnp.sum(_silu(c) * w_ref[0], axis=0, keepdims=True)


def _ada_call(c, w_ada, b_ada):
    depth, d, n = w_ada.shape
    rows = LANES
    return pl.pallas_call(
        _ada_kernel,
        grid=(depth, d // rows),
        in_specs=[pl.BlockSpec((rows, 1), lambda l, k: (k, 0)),
                  pl.BlockSpec((1, rows, n), lambda l, k: (l, k, 0)),
                  pl.BlockSpec((1, 1, n), lambda l, k: (l, 0, 0))],
        out_specs=pl.BlockSpec((1, 1, n), lambda l, k: (l, 0, 0)),
        out_shape=jax.ShapeDtypeStruct((depth, 1, n), F32),
        compiler_params=_cparams("arbitrary", "arbitrary"),
        name="ada_mod",
    )(c.reshape(d, 1), w_ada, b_ada.reshape(depth, 1, n))


def _ffn_rows(x, mod_ref, wi_ref, wo_ref, lng_ref, lnb_ref, *, sub, tf, alpha):
    shift, scale, gate = _mod_rows(mod_ref, sub)
    ff = wo_ref.shape[0]
    h = (x * (1.0 + scale) + shift).astype(BF16)
    y = jnp.zeros(x.shape, F32)
    for c0 in range(0, ff, tf):
        g = _dot(h, wi_ref[:, c0:c0 + tf])
        u = _dot(h, wi_ref[:, ff + c0:ff + c0 + tf])
        y = y + _dot((_silu(g) * u).astype(BF16), wo_ref[c0:c0 + tf, :])
    z = alpha * x + (0.5 * (1.0 + gate)) * y
    return _layer_norm(z, lng_ref[sub:sub + 1, :], lnb_ref[sub:sub + 1, :])


def _ffn_kernel(x_ref, mod_ref, wi_ref, wo_ref, lng_ref, lnb_ref, o_ref, *, sub, tf, alpha):
    o_ref[...] = _ffn_rows(x_ref[...], mod_ref, wi_ref, wo_ref, lng_ref, lnb_ref, sub=sub, tf=tf, alpha=alpha)


def _ffn_call(x, mod, w_in, w_out, ln_g, ln_b, *, layer, which, sub, alpha, tm, tf):
    s, d = x.shape
    assert w_out.shape[2] % tf == 0
    kern = functools.partial(_ffn_kernel, sub=sub, tf=tf, alpha=alpha)
    return pl.pallas_call(
        kern,
        grid=(s // tm,),
        in_specs=[pl.BlockSpec((tm, d), lambda i: (i, 0)),
                  pl.BlockSpec((3 * N_SUB, d), lambda i: (0, 0)),
                  _slab(w_in, layer, which), _slab(w_out, layer, which),
                  _slab(ln_g, layer), _slab(ln_b, layer)],
        out_specs=pl.BlockSpec((tm, d), lambda i: (i, 0)),
        out_shape=jax.ShapeDtypeStruct((s, d), F32),
        compiler_params=_cparams("arbitrary"),
        name="ffn",
    )(x, mod, w_in, w_out, ln_g, ln_b)


_PROJ_OUTS = (("gates", 3 * 1024, BF16), ("a", POOL_WIDTH, F32), ("h4", 4 * HGRN_WIDTH, F32),
              ("qi", IDX_HEADS * IDX_DIM, BF16), ("q", ATTN_WIDTH, BF16), ("k", ATTN_WIDTH, BF16),
              ("v", ATTN_WIDTH, BF16), ("ki2", 2 * IDX_DIM, BF16), ("wi", LANES, F32))
_PROJ_CHUNK = 1024


def _proj_kernel(x_ref, mod_ref, w_ref, *o_refs):
    shift, scale, _ = _mod_rows(mod_ref, 1)
    h = (x_ref[...] * (1.0 + scale) + shift).astype(BF16)
    c0 = 0
    for (name, width, dt), o_ref in zip(_PROJ_OUTS, o_refs):
        if name == "v":
            vt = lax.dot_general(w_ref[:, c0:c0 + width], h, (((0,), (1,)), ((), ())),
                                 preferred_element_type=F32)
            ones = jnp.ones((_VT_ROWS - ATTN_HEAD_DIM, vt.shape[1]), F32)
            rows = []
            for p0 in range(0, width, ATTN_HEAD_DIM):
                rows += [vt[p0:p0 + ATTN_HEAD_DIM, :], ones]
            o_ref[0] = jnp.concatenate(rows, axis=0).astype(dt)
        else:
            for cc in range(0, width, _PROJ_CHUNK):
                cw = min(_PROJ_CHUNK, width - cc)
                o_ref[:, cc:cc + cw] = _dot(h, w_ref[:, c0 + cc:c0 + cc + cw]).astype(dt)
        c0 += width


def _proj_weight_kernel(w_ref, o_ref):
    out_width = {name: wd for name, wd, _ in _PROJ_OUTS}
    src = {}
    c0 = 0
    for name, wd in (("a", POOL_WIDTH), ("q", ATTN_WIDTH), ("k", ATTN_WIDTH), ("v", ATTN_WIDTH),
                     ("qi", IDX_HEADS * IDX_DIM), ("ki", IDX_DIM), ("wi", IDX_HEADS),
                     ("h4", 4 * HGRN_WIDTH), ("gates", out_width["gates"])):
        src[name] = (c0, wd)
        c0 += wd
    assert c0 == w_ref.shape[-1]
    q_scale = ATTN_HEAD_DIM ** -0.5 * _LOG2E
    dst = 0
    for name, width, _ in _PROJ_OUTS:
        if name == "ki2":
            pieces = [("ki", 1.0), ("ki", 1.0)]
        elif name == "wi":
            o_ref[:, dst:dst + width] = jnp.zeros((o_ref.shape[0], width), BF16)
            pieces = [("wi", 1.0)]
        else:
            pieces = [(name, q_scale if name == "q" else 1.0)]
        off = dst
        for pname, scale in pieces:
            s0, wd = src[pname]
            o_ref[:, off:off + wd] = (w_ref[:, s0:s0 + wd] * scale).astype(BF16)
            off += wd
        dst += width


def _proj_weight(w):
    depth, d, n_in = w.shape
    n_out = sum(wd for _, wd, _ in _PROJ_OUTS)
    rows = 256
    return pl.pallas_call(
        _proj_weight_kernel,
        grid=(depth, d // rows),
        in_specs=[pl.BlockSpec((None, rows, n_in), lambda l, r: (l, r, 0))],
        out_specs=pl.BlockSpec((None, rows, n_out), lambda l, r: (l, r, 0)),
        out_shape=jax.ShapeDtypeStruct((depth, d, n_out), BF16),
        compiler_params=_cparams("arbitrary", "arbitrary"),
        name="proj_weight",
    )(w)


def _slab(arr, *lead):
    block = (None,) * len(lead) + tuple(arr.shape[len(lead):])
    index = tuple(lead) + (0,) * (arr.ndim - len(lead))
    return pl.BlockSpec(block, lambda i: index, pipeline_mode=pl.Buffered(1))


def _proj_call(x, mod, w_all, *, layer, tm):
    s, d = x.shape
    n = w_all.shape[-1]
    assert n == sum(wd for _, wd, _ in _PROJ_OUTS)
    n_vt = ATTN_HEADS * _VT_ROWS
    out_specs = [pl.BlockSpec((1, n_vt, tm), lambda i: (i, 0, 0)) if name == "v"
                 else pl.BlockSpec((tm, wd), lambda i: (i, 0)) for name, wd, _ in _PROJ_OUTS]
    out_shape = [jax.ShapeDtypeStruct((s // tm, n_vt, tm) if name == "v" else (s, wd), dt)
                 for name, wd, dt in _PROJ_OUTS]
    return pl.pallas_call(
        _proj_kernel,
        grid=(s // tm,),
        in_specs=[pl.BlockSpec((tm, d), lambda i: (i, 0)),
                  pl.BlockSpec((3 * N_SUB, d), lambda i: (0, 0)),
                  _slab(w_all, layer)],
        out_specs=out_specs,
        out_shape=out_shape,
        compiler_params=_cparams("arbitrary"),
        name="mix_proj",
    )(x, mod, w_all)


_BISECT_MAX_ITERS = 320
_F32_TINY = float(jnp.finfo(jnp.float32).tiny)
_COARSE_PASSES = 9
_COARSE_MIN_ABS, _COARSE_MAX_ABS = 1e-30, 1e30
_KIND_FAR, _KIND_PAD, _N_KINDS = 2, 3, 4
_VT_ROWS = ATTN_HEAD_DIM + 16
_SMASK_FULL, _SMASK_DIAG, _SMASK_PAD = 0, 1, 2
_LOG2E = math.log2(math.e)


def _t5_bucket(dist):
    max_exact = REL_BUCKETS // 2
    d32 = jnp.maximum(dist, 1).astype(F32)
    large = max_exact + (jnp.log(d32 / max_exact) / math.log(REL_MAX_DIST / max_exact)
                         * (REL_BUCKETS - max_exact)).astype(I32)
    large = jnp.minimum(large, REL_BUCKETS - 1)
    return jnp.where(dist < max_exact, dist, large)


def _dsa_kernel(relb_ref, qi_ref, wi_ref, q_ref, ki2_ref, k_ref, vt_ref, o_ref,
                s_sc, sb_sc, bias_sc, smask_sc, lg_sc, qim_sc, qm_sc, m_sc, acc_sc,
                *, tq, topk, idx_scale):
    tk = tq
    i = pl.program_id(0)
    krow = lax.broadcasted_iota(I32, (tk, tq), 0)
    qcol = lax.broadcasted_iota(I32, (tk, tq), 1)
    causal = krow <= qcol
    lane = lax.broadcasted_iota(I32, (tq, LANES), 1)
    lo_half = lane < ATTN_HEAD_DIM

    @pl.when(i == 0)
    def _():
        for which in range(2):
            bucket = _t5_bucket(jnp.maximum(qcol - krow + which * tq, 0))
            for h in range(ATTN_HEADS):
                b = jnp.zeros((tk, tq), F32)
                for bk in range(REL_BUCKETS):
                    b = jnp.where(bucket == bk, relb_ref[bk, h] * _LOG2E, b)
                bias_sc[h, which] = (jnp.where(causal, b, NEG_INF) if which == 0 else b).astype(BF16)
        for h in range(ATTN_HEADS):
            bias_sc[h, _KIND_FAR] = jnp.full((tk, tq), relb_ref[REL_BUCKETS - 1, h] * _LOG2E, F32).astype(BF16)
            bias_sc[h, _KIND_PAD] = jnp.full((tk, tq), NEG_INF, BF16)
        smask_sc[_SMASK_FULL] = jnp.zeros((tk, tq), F32)
        smask_sc[_SMASK_DIAG] = jnp.where(causal, 0.0, NEG_INF)
        smask_sc[_SMASK_PAD] = jnp.full((tk, tq), NEG_INF, F32)

    def head_operands(src_ref, dst_sc, n_heads):
        src = src_ref[...].astype(F32)
        for h in range(n_heads):
            pair = src[:, LANES * (h // 2):LANES * (h // 2 + 1)]
            pair = jnp.where(lo_half if h % 2 == 0 else ~lo_half, pair, 0.0)
            dst_sc[h] = jnp.transpose(pair).astype(BF16)

    head_operands(qi_ref, qim_sc, IDX_HEADS)
    head_operands(q_ref, qm_sc, ATTN_HEADS)

    wt = jnp.transpose(wi_ref[...])

    def key_rows(jb):
        return pl.ds(pl.multiple_of(jb * tk, tk), tk)

    def fold(x):
        return x.reshape(tk // SUBLANES, SUBLANES, tq)

    def index_scores(jb):
        kb = ki2_ref[key_rows(jb), :]
        acc = jnp.zeros((tk, tq), F32)
        for h in range(IDX_HEADS):
            acc = acc + wt[h:h + 1, :] * jnp.maximum(_dot(kb, qim_sc[h]), 0.0)
        return acc * idx_scale

    npair = (i + 2) // 2
    last_kv = k_ref.shape[0] // tk - 1

    def p1_pair(j, carry):
        rmin, rmax = carry
        for jb in (2 * j, 2 * j + 1):
            raw = index_scores(jnp.minimum(jb, last_kv))
            kind = jnp.where(jb > i, _SMASK_PAD, jnp.where(jb == i, _SMASK_DIAG, _SMASK_FULL))
            sc = raw + smask_sc[kind]
            s_sc[jb] = sc
            sb_sc[jb] = sc.astype(BF16)
            rmin = jnp.minimum(rmin, jnp.min(fold(raw), axis=0))
            rmax = jnp.maximum(rmax, jnp.max(fold(raw), axis=0))
        return rmin, rmax

    rmin, rmax = lax.fori_loop(0, npair, p1_pair, (jnp.full((SUBLANES, tq), POS_INF, F32),
                                                   jnp.full((SUBLANES, tq), NEG_INF, F32)))
    rowmin = jnp.min(rmin, axis=0, keepdims=True)
    rowmax = jnp.max(rmax, axis=0, keepdims=True)

    kf = float(topk)
    nfin = (i * tq + lax.broadcasted_iota(I32, (1, tq), 1) + 1).astype(F32)
    small = nfin <= kf

    def walk_blocks(blocks_fn, init):
        return lax.fori_loop(0, npair, lambda j, c: blocks_fn([2 * j, 2 * j + 1], c), init)

    def count_ge(x):
        def blocks_fn(jbs, cnt):
            for jb in jbs:
                cnt = cnt + jnp.sum(fold(jnp.where(s_sc[jb] >= x, 1.0, 0.0)), axis=0)
            return cnt
        return jnp.sum(walk_blocks(blocks_fn, jnp.zeros((SUBLANES, tq), F32)), axis=0, keepdims=True)

    def count_ge16(xb):
        one, zero = jnp.ones((), BF16), jnp.zeros((), BF16)
        rows16 = 2 * SUBLANES

        def blocks_fn(jbs, cnt):
            parts = []
            for jb in jbs:
                c = jnp.where(sb_sc[jb] >= xb, one, zero)
                parts += [c[r0:r0 + rows16, :] for r0 in range(0, tk, rows16)]
            while len(parts) > 1:
                parts = [a + b for a, b in zip(parts[0::2], parts[1::2])]
            return cnt + parts[0].astype(F32)
        return jnp.sum(walk_blocks(blocks_fn, jnp.zeros((rows16, tq), F32)), axis=0, keepdims=True)

    def coarse_probe(mid):
        xb = mid.astype(BF16)
        xbits = lax.bitcast_convert_type(xb.astype(F32), I32)
        neg = xbits < 0
        mu = jnp.where(neg, xbits + 0x8000, xbits - 0x8000)
        odd = (xbits & 0x10000) != 0
        tbits = jnp.where(odd, jnp.where(neg, mu - 1, mu + 1), mu)
        t = lax.bitcast_convert_type(tbits, F32)
        ok = jnp.logical_and(jnp.abs(xb.astype(F32)) > _COARSE_MIN_ABS, jnp.abs(xb.astype(F32)) < _COARSE_MAX_ABS)
        return xb, t, ok

    def narrow(st, probe, c, active):
        lo, hi, clo, chi, done = st
        up = jnp.logical_and(active, c >= kf)
        dn = jnp.logical_and(active, c < kf)
        return (jnp.where(up, probe, lo), jnp.where(dn, probe, hi), jnp.where(up, c, clo),
                jnp.where(dn, c, chi), jnp.where(jnp.logical_and(active, c == kf), 1.0, done))

    def coarse_body(_, st):
        lo, hi, _, _, done = st
        xb, t, ok = coarse_probe(0.5 * lo + 0.5 * hi)
        ok = jnp.logical_and(ok, jnp.logical_and(t > lo, t < hi))
        return narrow(st, t, count_ge16(xb), jnp.logical_and(done < 0.5, ok))

    def fine_cond(st):
        it, (_, _, _, _, done) = st
        return jnp.logical_and(it < _BISECT_MAX_ITERS, jnp.min(done) < 0.5)

    def fine_body(st):
        it, inner = st
        lo, hi, _, _, done = inner
        mid = 0.5 * lo + 0.5 * hi
        mid = jnp.where(jnp.logical_and(lo == 0.0, hi > _F32_TINY), _F32_TINY, mid)
        mid = jnp.where(jnp.logical_and(lo < 0.0, hi > 0.0), 0.0, mid)
        stuck = jnp.logical_or(jnp.logical_or(mid <= lo, mid >= hi),
                               jnp.logical_and(lo == 0.0, hi <= _F32_TINY))
        lo, hi, clo, chi, done = narrow(inner, mid, count_ge(mid),
                                        jnp.logical_and(done < 0.5, jnp.logical_not(stuck)))
        return it + 1, (lo, hi, clo, chi, jnp.where(stuck, 1.0, done))

    mbits = lax.bitcast_convert_type(rowmax, I32)
    above_max = jnp.where(rowmax == 0.0, _F32_TINY,
                          lax.bitcast_convert_type(jnp.where(rowmax > 0.0, mbits + 1, mbits - 1), F32))
    st0 = (jnp.where(small, NEG_INF, rowmin), above_max, nfin, jnp.zeros((1, tq), F32),
           jnp.where(small, 1.0, 0.0))
    st1 = lax.fori_loop(0, _COARSE_PASSES, coarse_body, st0)
    _, (lo, hi, clo, chi, _) = lax.while_loop(fine_cond, fine_body, (jnp.int32(0), st1))
    budget = kf - chi
    need_ties = jnp.max((clo - chi) - budget) > 0.5

    pairs = [slice(LANES * (h // 2), LANES * (h // 2 + 1)) for h in range(ATTN_HEADS)]

    def issue_qk(jb, slot):
        kb = k_ref[key_rows(jnp.minimum(jb, last_kv)), :]
        for h in range(ATTN_HEADS):
            lg_sc[slot, h] = _dot(kb[:, pairs[h]], qm_sc[h]).astype(BF16)

    def attend(jb, slot, thr, cum=None):
        s = s_sc[jb]
        if cum is None:
            sel = s >= thr
        else:
            cand = jnp.logical_and(s >= lo, s < hi)
            before = (qcol < krow).astype(BF16)
            rank = cum + _dot(before, jnp.where(cand, 1.0, 0.0).astype(BF16))
            sel = jnp.logical_or(s >= hi, jnp.logical_and(cand, rank < budget))
        kind = jnp.where(jb > i, _KIND_PAD, jnp.minimum(i - jb, _KIND_FAR))
        selm = jnp.where(sel, 0.0, NEG_INF).astype(BF16)
        probs, alphas = [], []
        for h in range(ATTN_HEADS):
            lg = lg_sc[slot, h] + bias_sc[h, kind] + selm
            m_old = m_sc[h]
            m_new = jnp.maximum(m_old, jnp.max(lg, axis=0, keepdims=True).astype(F32))
            m_safe = jnp.where(m_new == NEG_INF, 0.0, m_new)
            alphas.append(jnp.exp2(m_old - m_safe))
            probs.append(jnp.exp2(lg - m_safe.astype(BF16)))
            m_sc[h] = m_new
        vb = vt_ref[jnp.minimum(jb, last_kv)]
        for h in range(ATTN_HEADS):
            vrows = slice(_VT_ROWS * h, _VT_ROWS * (h + 1))
            acc_sc[h] = alphas[h] * acc_sc[h] + _dot(vb[vrows, :], probs[h])

    def run(with_ties):
        m_sc[...] = jnp.full(m_sc.shape, NEG_INF, F32)
        acc_sc[...] = jnp.zeros_like(acc_sc)
        issue_qk(0, 0)

        def pair(j, thr0, thr1, cum0=None, cum1=None):
            issue_qk(2 * j + 1, 1)
            attend(2 * j, 0, thr0, cum0)
            issue_qk(2 * j + 2, 0)
            attend(2 * j + 1, 1, thr1, cum1)

        if not with_ties:
            def pair_body(j, carry):
                pair(j, lo, lo)
                return carry
            lax.fori_loop(0, npair, pair_body, 0)
            return

        last_s = s_sc.shape[0] - 1

        def cand_count(jb):
            s = s_sc[jnp.minimum(jb, last_s)]
            cand = jnp.logical_and(s >= lo, s < hi)
            return jnp.sum(jnp.where(cand, 1.0, 0.0), axis=0, keepdims=True)

        def needs_ranking(cum0, ca, cb):
            cum1 = cum0 + ca
            runs_out = jnp.logical_or(jnp.logical_and(cum1 > budget, cum0 < budget),
                                      jnp.logical_and(cum1 + cb > budget, cum1 < budget))
            return (jnp.max(jnp.where(runs_out, 1.0, 0.0)) > 0.5).astype(I32)

        def tie_pair_body(j, carry):
            ranked, cum0, ca, cb = carry
            cum1 = cum0 + ca
            cum2 = cum1 + cb

            def variant(with_rank):
                def fn():
                    na, nb = cand_count(2 * j + 2), cand_count(2 * j + 3)
                    nxt = needs_ranking(cum2, na, nb)
                    if with_rank:
                        pair(j, None, None, cum0, cum1)
                    else:
                        pair(j, jnp.where(cum1 <= budget, lo, hi), jnp.where(cum2 <= budget, lo, hi))
                    return nxt, cum2, na, nb
                return fn
            return lax.cond(ranked == 1, variant(True), variant(False))

        c0, c1 = cand_count(0), cand_count(1)
        zero = jnp.zeros((1, tq), F32)
        lax.fori_loop(0, npair, tie_pair_body, (needs_ranking(zero, c0, c1), zero, c0, c1))

    @pl.when(need_ties)
    def _():
        run(True)

    @pl.when(jnp.logical_not(need_ties))
    def _():
        run(False)

    hd = ATTN_HEAD_DIM
    normed = [acc_sc[h, 0:hd, :] / acc_sc[h, hd:hd + 1, :] for h in range(ATTN_HEADS)]
    for p in range(ATTN_HEADS // 2):
        pair_t = jnp.concatenate([normed[2 * p], normed[2 * p + 1]], axis=0)
        o_ref[:, LANES * p:LANES * (p + 1)] = jnp.transpose(pair_t)


def _dsa_call(rel_bias, qi, wi, q, ki2, k, vt, *, tq):
    s = q.shape[0]
    assert tq > REL_MAX_DIST and s % tq == 0
    nb = s // tq
    topk = min(TOPK_MAX, s // 4)
    kern = functools.partial(_dsa_kernel, tq=tq, topk=topk,
                             idx_scale=(IDX_DIM ** -0.5) * (IDX_HEADS ** -0.5))
    whole = lambda width: pl.BlockSpec((s, width), lambda i: (0, 0), pipeline_mode=pl.Buffered(1))
    assert vt.shape == (nb, ATTN_HEADS * _VT_ROWS, tq)
    return pl.pallas_call(
        kern,
        grid=(nb,),
        in_specs=[pl.BlockSpec(memory_space=pltpu.SMEM),
                  pl.BlockSpec((tq, IDX_HEADS * IDX_DIM), lambda i: (i, 0)),
                  pl.BlockSpec((tq, LANES), lambda i: (i, 0)),
                  pl.BlockSpec((tq, ATTN_WIDTH), lambda i: (i, 0)),
                  whole(2 * IDX_DIM), whole(ATTN_WIDTH),
                  pl.BlockSpec(vt.shape, lambda i: (0, 0, 0), pipeline_mode=pl.Buffered(1))],
        out_specs=pl.BlockSpec((tq, ATTN_WIDTH), lambda i: (i, 0)),
        out_shape=jax.ShapeDtypeStruct((s, ATTN_WIDTH), F32),
        scratch_shapes=[pltpu.VMEM((nb + nb % 2, tq, tq), F32),
                        pltpu.VMEM((nb + nb % 2, tq, tq), BF16),
                        pltpu.VMEM((ATTN_HEADS, _N_KINDS, tq, tq), BF16),
                        pltpu.VMEM((3, tq, tq), F32),
                        pltpu.VMEM((2, ATTN_HEADS, tq, tq), BF16),
                        pltpu.VMEM((IDX_HEADS, LANES, tq), BF16),
                        pltpu.VMEM((ATTN_HEADS, LANES, tq), BF16),
                        pltpu.VMEM((ATTN_HEADS, 1, tq), F32),
                        pltpu.VMEM((ATTN_HEADS, _VT_ROWS, tq), F32)],
        compiler_params=_cparams("arbitrary"),
        name="dsa_attention",
    )(rel_bias, qi, wi, q, ki2, k, vt)


_HGRN_CHUNK = 128
_HGRN_BASE = SUBLANES


def _hgrn_kernel(h4_ref, lb_ref, ng_ref, o_ref, st_sc, *, layer, rows):
    c_len = _HGRN_CHUNK

    @pl.when(pl.program_id(0) == 0)
    def _():
        st_sc[...] = jnp.zeros_like(st_sc)

    lb_all = lb_ref[...]
    e = jnp.exp(lb_all - jnp.max(lb_all, axis=0, keepdims=True))
    sm = e / jnp.sum(e, axis=0, keepdims=True)
    lbs = jnp.sum(sm[0:layer + 1, :], axis=0, keepdims=True) - sm[0:1, :]

    r1 = lax.broadcasted_iota(I32, (c_len, 1), 0)
    rr = lax.broadcasted_iota(I32, (c_len, c_len), 0)
    cc = lax.broadcasted_iota(I32, (c_len, c_len), 1)

    def block_ref(a, blk, r):
        x = a.reshape(c_len // blk, blk, a.shape[-1])[:, r:r + 1, :]
        return jnp.broadcast_to(x, (c_len // blk, blk, a.shape[-1])).reshape(c_len, a.shape[-1])

    w = HGRN_WIDTH
    for c in range(rows // c_len):
        rs = slice(c * c_len, (c + 1) * c_len)
        hq = h4_ref[rs, 0:w]
        z = h4_ref[rs, w:2 * w]
        val = h4_ref[rs, 2 * w:3 * w]
        hg = h4_ref[rs, 3 * w:4 * w]
        key = (1.0 - lbs) * jax.nn.sigmoid(-z)
        g = jnp.log1p(-jnp.minimum(key, KEY_MAX))
        a_cum = g
        sh = 1
        while sh < c_len:
            a_cum = a_cum + jnp.where(r1 >= sh, pltpu.roll(a_cum, sh, 0), 0.0)
            sh *= 2
        qf = _silu(hq)
        outs = []
        for h in range(HGRN_HEADS):
            hs = slice(h * HGRN_DIM, (h + 1) * HGRN_DIM)
            ah, qh, kh, vh = a_cum[:, hs], qf[:, hs], key[:, hs], val[:, hs]
            vh16 = vh.astype(BF16)
            st = st_sc[h]
            o = _dot_nt((qh * jnp.exp(ah)).astype(BF16), st.astype(BF16))
            ref = block_ref(ah, _HGRN_BASE, _HGRN_BASE // 2 - 1)
            sc = _dot_nt((qh * jnp.exp(ah - ref)).astype(BF16), (kh * jnp.exp(ref - ah)).astype(BF16))
            same = (rr >> _log2(_HGRN_BASE)) == (cc >> _log2(_HGRN_BASE))
            scores = jnp.where(jnp.logical_and(same, cc <= rr), sc, 0.0)
            half = _HGRN_BASE
            while half < c_len:
                blk = 2 * half
                ref = block_ref(ah, blk, half - 1)
                second = (r1 & (blk - 1)) >= half
                ql = jnp.where(second, qh * jnp.exp(jnp.minimum(ah - ref, 0.0)), 0.0)
                kl = jnp.where(second, 0.0, kh * jnp.exp(jnp.minimum(ref - ah, 0.0)))
                sc = _dot_nt(ql.astype(BF16), kl.astype(BF16))
                scores = scores + jnp.where((rr >> _log2(blk)) == (cc >> _log2(blk)), sc, 0.0)
                half = blk
            o = o + _dot(scores.astype(BF16), vh16)
            a_last = ah[c_len - 1:c_len, :]
            kd = (kh * jnp.exp(a_last - ah)).astype(BF16)
            st_sc[h] = st * jnp.exp(a_last) + _dot_tn(vh16, kd)
            o = o * lax.rsqrt(jnp.mean(o * o, axis=-1, keepdims=True) + RMS_EPS)
            outs.append(o)
        o_ref[rs, :] = jnp.concatenate(outs, axis=1) * ng_ref[...] * _silu(hg)


def _hgrn_call(h4, lb, ng, *, layer, rows):
    s = h4.shape[0]
    kern = functools.partial(_hgrn_kernel, layer=layer, rows=rows)
    return pl.pallas_call(
        kern,
        grid=(s // rows,),
        in_specs=[pl.BlockSpec((rows, 4 * HGRN_WIDTH), lambda i: (i, 0)),
                  pl.BlockSpec(lb.shape, lambda i: (0, 0)),
                  _slab(ng, layer)],
        out_specs=pl.BlockSpec((rows, HGRN_WIDTH), lambda i: (i, 0)),
        out_shape=jax.ShapeDtypeStruct((s, HGRN_WIDTH), F32),
        scratch_shapes=[pltpu.VMEM((HGRN_HEADS, HGRN_DIM, HGRN_DIM), F32)],
        compiler_params=_cparams("arbitrary"),
        name="hgrn2",
    )(h4, lb, ng)


_POOL_HALO = 16


def _merge_kernel(x_ref, mod_ref, a_ref, ah_ref, yb_ref, yc_ref, g_ref, pw_ref, ps_ref, wb_ref, wo_ref,
                  lng_ref, lnb_ref, fwi_ref, fwo_ref, o_ref, *, tm, tf, alpha):
    i = pl.program_id(0)
    d = x_ref.shape[1]
    _, _, gate = _mod_rows(mod_ref, 1)
    a = a_ref[...]
    halo = jnp.where(i == 0, 0.0, ah_ref[...])
    ext = jnp.concatenate([halo, a], axis=0)
    lane = lax.broadcasted_iota(I32, (tm, POOL_WIDTH), 1)
    grp = lane >> _log2(POOL_GROUP)
    pos1 = (i * tm + lax.broadcasted_iota(I32, (tm, POOL_WIDTH), 0) + 1).astype(F32)
    win = jnp.zeros((tm, POOL_WIDTH), F32)
    cnt = jnp.zeros((tm, POOL_WIDTH), F32)
    ssum, have = ext, 1
    for gi, wd in enumerate(POOL_WINDOWS):
        while have < wd:
            ssum = ssum + pltpu.roll(ssum, have, 0)
            have *= 2
        assert have == wd
        win = jnp.where(grp == gi, ssum[_POOL_HALO:, :], win)
        cnt = jnp.where(grp == gi, jnp.minimum(pos1, float(wd)), cnt)
    dlt = (win / cnt - a).astype(BF16)
    ya = _dot(dlt, pw_ref[...]) * ps_ref[...]
    za = _dot(ya.astype(BF16), wb_ref[0:POOL_WIDTH, :])
    zb = _dot(yb_ref[...].astype(BF16), wb_ref[POOL_WIDTH:POOL_WIDTH + ATTN_WIDTH, :])
    zc = _dot(yc_ref[...].astype(BF16), wb_ref[POOL_WIDTH + ATTN_WIDTH:, :])
    g0, g1, g2 = (jax.nn.sigmoid(g_ref[:, j * d:(j + 1) * d].astype(F32)) for j in range(3))
    merged = g0 * za + g1 * zb + g2 * zc
    y = _dot(merged.astype(BF16), wo_ref[...])
    z = alpha * x_ref[...] + (1.0 + gate) * y
    xm = _layer_norm(z, lng_ref[1:2, :], lnb_ref[1:2, :])
    o_ref[...] = _ffn_rows(xm, mod_ref, fwi_ref, fwo_ref, lng_ref, lnb_ref, sub=2, tf=tf, alpha=alpha)


def _merge_call(x, mod, a, yb, yc, gates, pw_bd, ps, wb, wo, ln_g, ln_b, ffn_wi, ffn_wo, *, layer, tm, tf, alpha):
    s, d = x.shape
    kern = functools.partial(_merge_kernel, tm=tm, tf=tf, alpha=alpha)
    full = lambda arr: _slab(arr, layer)
    hb = tm // _POOL_HALO
    return pl.pallas_call(
        kern,
        grid=(s // tm,),
        in_specs=[pl.BlockSpec((tm, d), lambda i: (i, 0)),
                  pl.BlockSpec((3 * N_SUB, d), lambda i: (0, 0)),
                  pl.BlockSpec((tm, POOL_WIDTH), lambda i: (i, 0)),
                  pl.BlockSpec((_POOL_HALO, POOL_WIDTH), lambda i: (jnp.maximum(i * hb - 1, 0), 0)),
                  pl.BlockSpec((tm, ATTN_WIDTH), lambda i: (i, 0)),
                  pl.BlockSpec((tm, HGRN_WIDTH), lambda i: (i, 0)),
                  pl.BlockSpec((tm, 3 * d), lambda i: (i, 0)),
                  full(pw_bd), full(ps), full(wb), full(wo), full(ln_g), full(ln_b),
                  _slab(ffn_wi, layer, 1), _slab(ffn_wo, layer, 1)],
        out_specs=pl.BlockSpec((tm, d), lambda i: (i, 0)),
        out_shape=jax.ShapeDtypeStruct((s, d), F32),
        compiler_params=_cparams("arbitrary"),
        name="merge_ffn",
    )(x, mod, a, a, yb, yc, gates, pw_bd, ps, wb, wo, ln_g, ln_b, ffn_wi, ffn_wo)


def _pool_blockdiag(pool_w):
    depth, n, g, _ = pool_w.shape
    eye = jnp.eye(n, dtype=pool_w.dtype)
    return jnp.einsum("lgij,gh->lgihj", pool_w, eye).reshape(depth, n * g, n * g)


def kernel(x, c, w_ada, b_ada, ln_g, ln_b, ffn_w_in, ffn_w_out, mix_w_in, pool_w, pool_scale, rel_bias,
           hgrn_lb, hgrn_norm_g, w_branch, w_out):
    batch, s, d = x.shape
    assert batch == 1, "kernels are written for a single sequence"
    depth = w_ada.shape[0]
    alpha = (2 * depth) ** 0.25
    tm_ffn = min(512, s)
    tf = 256
    tq = min(256, s)
    tm_proj = tq
    tm_merge = min(512, s)
    hgrn_rows = min(1024, s)

    ffn_wi16, ffn_wo16 = ffn_w_in.astype(BF16), ffn_w_out.astype(BF16)
    mix_w16 = _proj_weight(mix_w_in)
    pool_bd16 = _pool_blockdiag(pool_w).astype(BF16)
    wb16, wo16 = w_branch.astype(BF16), w_out.astype(BF16)
    pool_scale3 = pool_scale.reshape(depth, 1, -1)
    norm_g3 = hgrn_norm_g.reshape(depth, 1, -1)

    mods = _ada_call(c, w_ada, b_ada)
    xs = x.reshape(s, d)
    for l in range(depth):
        mod = mods[l].reshape(3 * N_SUB, d)
        xs = _ffn_call(xs, mod, ffn_wi16, ffn_wo16, ln_g, ln_b, layer=l, which=0, sub=0, alpha=alpha,
                       tm=tm_ffn, tf=tf)
        gates, a, h4, qi, q, k, vt, ki2, wi = _proj_call(xs, mod, mix_w16, layer=l, tm=tm_proj)
        yb = _dsa_call(rel_bias, qi, wi, q, ki2, k, vt, tq=tq)
        yc = _hgrn_call(h4, hgrn_lb, norm_g3, layer=l, rows=hgrn_rows)
        xs = _merge_call(xs, mod, a, yb, yc, gates, pool_bd16, pool_scale3, wb16, wo16, ln_g, ln_b,
                         ffn_wi16, ffn_wo16, layer=l, tm=tm_merge, tf=tf, alpha=alpha)
    return xs.reshape(batch, s, d)
```

```python
import functools
import math

import jax
import jax.numpy as jnp
from jax import lax
from jax.experimental import pallas as pl
from jax.experimental.pallas import tpu as pltpu

F32 = jnp.float32
BF16 = jnp.bfloat16
I32 = jnp.int32

POOL_WINDOWS = (2, 4, 8, 16)
POOL_GROUP = 64
POOL_WIDTH = POOL_GROUP * len(POOL_WINDOWS)
ATTN_HEADS = 4
ATTN_HEAD_DIM = 64
ATTN_WIDTH = ATTN_HEADS * ATTN_HEAD_DIM
IDX_HEADS = 8
IDX_DIM = 64
TOPK_MAX = 256
HGRN_HEADS = 4
HGRN_DIM = 128
HGRN_WIDTH = HGRN_HEADS * HGRN_DIM
KEY_MAX = 1.0 - 1e-6
REL_BUCKETS = 32
REL_MAX_DIST = 128
N_SUB = 3
LN_EPS = 1e-5
RMS_EPS = 1e-6

LANES = 128
SUBLANES = 8
VMEM_LIMIT_BYTES = 58 * 1024 * 1024

NEG_INF = float("-inf")
POS_INF = float("inf")


def _cparams(*sem):
    return pltpu.CompilerParams(dimension_semantics=sem, vmem_limit_bytes=VMEM_LIMIT_BYTES)


def _dot(a, b):
    return jnp.dot(a, b, preferred_element_type=F32)


def _dot_nt(a, b):
    return lax.dot_general(a, b, (((1,), (1,)), ((), ())), preferred_element_type=F32)


def _dot_tn(a, b):
    return lax.dot_general(a, b, (((0,), (0,)), ((), ())), preferred_element_type=F32)


def _log2(n):
    assert n & (n - 1) == 0
    return n.bit_length() - 1


def _silu(x):
    return x * jax.nn.sigmoid(x)


def _mod_rows(mod_ref, j):
    return mod_ref[3 * j:3 * j + 1, :], mod_ref[3 * j + 1:3 * j + 2, :], mod_ref[3 * j + 2:3 * j + 3, :]


def _layer_norm(z, g, b):
    mu = jnp.mean(z, axis=-1, keepdims=True)
    zc = z - mu
    var = jnp.mean(zc * zc, axis=-1, keepdims=True)
    return zc * lax.rsqrt(var + LN_EPS) * g + b


def _ada_kernel(c_ref, w_ref, b_ref, o_ref):
    @pl.when(pl.program_id(1) == 0)
    def _():
        o_ref[0] = b_ref[0]

    c = c_ref[...]
    o_ref[0] += jnp.sum(_silu(c) * w_ref[0], axis=0, keepdims=True)


def _ada_call(c, w_ada, b_ada):
    depth, d, n = w_ada.shape
    rows = LANES
    return pl.pallas_call(
        _ada_kernel,
        grid=(depth, d // rows),
        in_specs=[pl.BlockSpec((rows, 1), lambda l, k: (k, 0)),
                  pl.BlockSpec((1, rows, n), lambda l, k: (l, k, 0)),
                  pl.BlockSpec((1, 1, n), lambda l, k: (l, 0, 0))],
        out_specs=pl.BlockSpec((1, 1, n), lambda l, k: (l, 0, 0)),
        out_shape=jax.ShapeDtypeStruct((depth, 1, n), F32),
        compiler_params=_cparams("arbitrary", "arbitrary"),
        name="ada_mod",
    )(c.reshape(d, 1), w_ada, b_ada.reshape(depth, 1, n))


def _ffn_rows(x, mod_ref, wi_ref, wo_ref, lng_ref, lnb_ref, *, sub, tf, alpha):
    shift, scale, gate = _mod_rows(mod_ref, sub)
    ff = wo_ref.shape[0]
    h = (x * (1.0 + scale) + shift).astype(BF16)
    y = jnp.zeros(x.shape, F32)
    for c0 in range(0, ff, tf):
        g = _dot(h, wi_ref[:, c0:c0 + tf])
        u = _dot(h, wi_ref[:, ff + c0:ff + c0 + tf])
        y = y + _dot((_silu(g) * u).astype(BF16), wo_ref[c0:c0 + tf, :])
    z = alpha * x + (0.5 * (1.0 + gate)) * y
    return _layer_norm(z, lng_ref[sub:sub + 1, :], lnb_ref[sub:sub + 1, :])


def _ffn_kernel(x_ref, mod_ref, wi_ref, wo_ref, lng_ref, lnb_ref, o_ref, *, sub, tf, alpha):
    o_ref[...] = _ffn_rows(x_ref[...], mod_ref, wi_ref, wo_ref, lng_ref, lnb_ref, sub=sub, tf=tf, alpha=alpha)


def _ffn_call(x, mod, w_in, w_out, ln_g, ln_b, *, layer, which, sub, alpha, tm, tf):
    s, d = x.shape
    assert w_out.shape[2] % tf == 0
    kern = functools.partial(_ffn_kernel, sub=sub, tf=tf, alpha=alpha)
    return pl.pallas_call(
        kern,
        grid=(s // tm,),
        in_specs=[pl.BlockSpec((tm, d), lambda i: (i, 0)),
                  pl.BlockSpec((3 * N_SUB, d), lambda i: (0, 0)),
                  _slab(w_in, layer, which), _slab(w_out, layer, which),
                  _slab(ln_g, layer), _slab(ln_b, layer)],
        out_specs=pl.BlockSpec((tm, d), lambda i: (i, 0)),
        out_shape=jax.ShapeDtypeStruct((s, d), F32),
        compiler_params=_cparams("arbitrary"),
        name="ffn",
    )(x, mod, w_in, w_out, ln_g, ln_b)


_PROJ_OUTS = (("gates", 3 * 1024, BF16), ("a", POOL_WIDTH, F32), ("h4", 4 * HGRN_WIDTH, F32),
              ("qi", IDX_HEADS * IDX_DIM, BF16), ("q", ATTN_WIDTH, BF16), ("k", ATTN_WIDTH, BF16),
              ("v", ATTN_WIDTH, BF16), ("ki2", 2 * IDX_DIM, BF16), ("wi", LANES, F32))
_PROJ_CHUNK = 1024


def _proj_kernel(x_ref, mod_ref, w_ref, *o_refs):
    shift, scale, _ = _mod_rows(mod_ref, 1)
    h = (x_ref[...] * (1.0 + scale) + shift).astype(BF16)
    c0 = 0
    for (name, width, dt), o_ref in zip(_PROJ_OUTS, o_refs):
        if name == "v":
            vt = lax.dot_general(w_ref[:, c0:c0 + width], h, (((0,), (1,)), ((), ())),
                                 preferred_element_type=F32)
            ones = jnp.ones((_VT_ROWS - ATTN_HEAD_DIM, vt.shape[1]), F32)
            rows = []
            for p0 in range(0, width, ATTN_HEAD_DIM):
                rows += [vt[p0:p0 + ATTN_HEAD_DIM, :], ones]
            o_ref[0] = jnp.concatenate(rows, axis=0).astype(dt)
        else:
            for cc in range(0, width, _PROJ_CHUNK):
                cw = min(_PROJ_CHUNK, width - cc)
                o_ref[:, cc:cc + cw] = _dot(h, w_ref[:, c0 + cc:c0 + cc + cw]).astype(dt)
        c0 += width


def _proj_weight_kernel(w_ref, o_ref):
    out_width = {name: wd for name, wd, _ in _PROJ_OUTS}
    src = {}
    c0 = 0
    for name, wd in (("a", POOL_WIDTH), ("q", ATTN_WIDTH), ("k", ATTN_WIDTH), ("v", ATTN_WIDTH),
                     ("qi", IDX_HEADS * IDX_DIM), ("ki", IDX_DIM), ("wi", IDX_HEADS),
                     ("h4", 4 * HGRN_WIDTH), ("gates", out_width["gates"])):
        src[name] = (c0, wd)
        c0 += wd
    assert c0 == w_ref.shape[-1]
    q_scale = ATTN_HEAD_DIM ** -0.5 * _LOG2E
    dst = 0
    for name, width, _ in _PROJ_OUTS:
        if name == "ki2":
            pieces = [("ki", 1.0), ("ki", 1.0)]
        elif name == "wi":
            o_ref[:, dst:dst + width] = jnp.zeros((o_ref.shape[0], width), BF16)
            pieces = [("wi", 1.0)]
        else:
            pieces = [(name, q_scale if name == "q" else 1.0)]
        off = dst
        for pname, scale in pieces:
            s0, wd = src[pname]
            o_ref[:, off:off + wd] = (w_ref[:, s0:s0 + wd] * scale).astype(BF16)
            off += wd
        dst += width


def _proj_weight(w):
    depth, d, n_in = w.shape
    n_out = sum(wd for _, wd, _ in _PROJ_OUTS)
    rows = 256
    return pl.pallas_call(
        _proj_weight_kernel,
        grid=(depth, d // rows),
        in_specs=[pl.BlockSpec((None, rows, n_in), lambda l, r: (l, r, 0))],
        out_specs=pl.BlockSpec((None, rows, n_out), lambda l, r: (l, r, 0)),
        out_shape=jax.ShapeDtypeStruct((depth, d, n_out), BF16),
        compiler_params=_cparams("arbitrary", "arbitrary"),
        name="proj_weight",
    )(w)


def _slab(arr, *lead):
    block = (None,) * len(lead) + tuple(arr.shape[len(lead):])
    index = tuple(lead) + (0,) * (arr.ndim - len(lead))
    return pl.BlockSpec(block, lambda i: index, pipeline_mode=pl.Buffered(1))


def _proj_call(x, mod, w_all, *, layer, tm):
    s, d = x.shape
    n = w_all.shape[-1]
    assert n == sum(wd for _, wd, _ in _PROJ_OUTS)
    n_vt = ATTN_HEADS * _VT_ROWS
    out_specs = [pl.BlockSpec((1, n_vt, tm), lambda i: (i, 0, 0)) if name == "v"
                 else pl.BlockSpec((tm, wd), lambda i: (i, 0)) for name, wd, _ in _PROJ_OUTS]
    out_shape = [jax.ShapeDtypeStruct((s // tm, n_vt, tm) if name == "v" else (s, wd), dt)
                 for name, wd, dt in _PROJ_OUTS]
    return pl.pallas_call(
        _proj_kernel,
        grid=(s // tm,),
        in_specs=[pl.BlockSpec((tm, d), lambda i: (i, 0)),
                  pl.BlockSpec((3 * N_SUB, d), lambda i: (0, 0)),
                  _slab(w_all, layer)],
        out_specs=out_specs,
        out_shape=out_shape,
        compiler_params=_cparams("arbitrary"),
        name="mix_proj",
    )(x, mod, w_all)


_BISECT_MAX_ITERS = 320
_F32_TINY = float(jnp.finfo(jnp.float32).tiny)
_COARSE_PASSES = 9
_COARSE_MIN_ABS, _COARSE_MAX_ABS = 1e-30, 1e30
_KIND_FAR, _KIND_PAD, _N_KINDS = 2, 3, 4
_VT_ROWS = ATTN_HEAD_DIM + 16
_SMASK_FULL, _SMASK_DIAG, _SMASK_PAD = 0, 1, 2
_LOG2E = math.log2(math.e)


def _t5_bucket(dist):
    max_exact = REL_BUCKETS // 2
    d32 = jnp.maximum(dist, 1).astype(F32)
    large = max_exact + (jnp.log(d32 / max_exact) / math.log(REL_MAX_DIST / max_exact)
                         * (REL_BUCKETS - max_exact)).astype(I32)
    large = jnp.minimum(large, REL_BUCKETS - 1)
    return jnp.where(dist < max_exact, dist, large)


def _dsa_kernel(relb_ref, qi_ref, wi_ref, q_ref, ki2_ref, k_ref, vt_ref, o_ref,
                s_sc, sb_sc, bias_sc, smask_sc, lg_sc, qim_sc, qm_sc, m_sc, acc_sc,
                *, tq, topk, idx_scale):
    tk = tq
    i = pl.program_id(0)
    krow = lax.broadcasted_iota(I32, (tk, tq), 0)
    qcol = lax.broadcasted_iota(I32, (tk, tq), 1)
    causal = krow <= qcol
    lane = lax.broadcasted_iota(I32, (tq, LANES), 1)
    lo_half = lane < ATTN_HEAD_DIM

    @pl.when(i == 0)
    def _():
        for which in range(2):
            bucket = _t5_bucket(jnp.maximum(qcol - krow + which * tq, 0))
            for h in range(ATTN_HEADS):
                b = jnp.zeros((tk, tq), F32)
                for bk in range(REL_BUCKETS):
                    b = jnp.where(bucket == bk, relb_ref[bk, h] * _LOG2E, b)
                bias_sc[h, which] = (jnp.where(causal, b, NEG_INF) if which == 0 else b).astype(BF16)
        for h in range(ATTN_HEADS):
            bias_sc[h, _KIND_FAR] = jnp.full((tk, tq), relb_ref[REL_BUCKETS - 1, h] * _LOG2E, F32).astype(BF16)
            bias_sc[h, _KIND_PAD] = jnp.full((tk, tq), NEG_INF, BF16)
        smask_sc[_SMASK_FULL] = jnp.zeros((tk, tq), F32)
        smask_sc[_SMASK_DIAG] = jnp.where(causal, 0.0, NEG_INF)
        smask_sc[_SMASK_PAD] = jnp.full((tk, tq), NEG_INF, F32)

    def head_operands(src_ref, dst_sc, n_heads):
        src = src_ref[...].astype(F32)
        for h in range(n_heads):
            pair = src[:, LANES * (h // 2):LANES * (h // 2 + 1)]
            pair = jnp.where(lo_half if h % 2 == 0 else ~lo_half, pair, 0.0)
            dst_sc[h] = jnp.transpose(pair).astype(BF16)

    head_operands(qi_ref, qim_sc, IDX_HEADS)
    head_operands(q_ref, qm_sc, ATTN_HEADS)

    wt = jnp.transpose(wi_ref[...])

    def key_rows(jb):
        return pl.ds(pl.multiple_of(jb * tk, tk), tk)

    def fold(x):
        return x.reshape(tk // SUBLANES, SUBLANES, tq)

    def index_scores(jb):
        kb = ki2_ref[key_rows(jb), :]
        acc = jnp.zeros((tk, tq), F32)
        for h in range(IDX_HEADS):
            acc = acc + wt[h:h + 1, :] * jnp.maximum(_dot(kb, qim_sc[h]), 0.0)
        return acc * idx_scale

    npair = (i + 2) // 2
    last_kv = k_ref.shape[0] // tk - 1

    def p1_blocks(jbs, carry):
        rmin, rmax = carry
        for jb in jbs:
            raw = index_scores(jnp.minimum(jb, last_kv))
            kind = jnp.where(jb > i, _SMASK_PAD, jnp.where(jb == i, _SMASK_DIAG, _SMASK_FULL))
            sc = raw + smask_sc[kind]
            s_sc[jb] = sc
            sb_sc[jb] = sc.astype(BF16)
            rmin = jnp.minimum(rmin, jnp.min(fold(raw), axis=0))
            rmax = jnp.maximum(rmax, jnp.max(fold(raw), axis=0))
        return rmin, rmax

    nquad = npair // 2
    p1_carry = lax.fori_loop(0, nquad, lambda q, c: p1_blocks([4 * q + r for r in range(4)], c),
                             (jnp.full((SUBLANES, tq), POS_INF, F32), jnp.full((SUBLANES, tq), NEG_INF, F32)))
    rmin, rmax = lax.fori_loop(2 * nquad, npair, lambda j, c: p1_blocks([2 * j, 2 * j + 1], c), p1_carry)
    rowmin = jnp.min(rmin, axis=0, keepdims=True)
    rowmax = jnp.max(rmax, axis=0, keepdims=True)

    kf = float(topk)
    nfin = (i * tq + lax.broadcasted_iota(I32, (1, tq), 1) + 1).astype(F32)
    small = nfin <= kf

    def walk_blocks(blocks_fn, init):
        return lax.fori_loop(0, npair, lambda j, c: blocks_fn([2 * j, 2 * j + 1], c), init)

    def count_ge(x):
        def blocks_fn(jbs, cnt):
            for jb in jbs:
                cnt = cnt + jnp.sum(fold(jnp.where(s_sc[jb] >= x, 1.0, 0.0)), axis=0)
            return cnt
        return jnp.sum(walk_blocks(blocks_fn, jnp.zeros((SUBLANES, tq), F32)), axis=0, keepdims=True)

    def count_ge16(xb):
        one, zero = jnp.ones((), BF16), jnp.zeros((), BF16)
        rows16 = 2 * SUBLANES

        def blocks_fn(jbs, cnt):
            parts = []
            for jb in jbs:
                c = jnp.where(sb_sc[jb] >= xb, one, zero)
                parts += [c[r0:r0 + rows16, :] for r0 in range(0, tk, rows16)]
            while len(parts) > 1:
                parts = [a + b for a, b in zip(parts[0::2], parts[1::2])]
            return cnt + parts[0].astype(F32)
        return jnp.sum(walk_blocks(blocks_fn, jnp.zeros((rows16, tq), F32)), axis=0, keepdims=True)

    def coarse_probe(mid):
        xb = mid.astype(BF16)
        xbits = lax.bitcast_convert_type(xb.astype(F32), I32)
        neg = xbits < 0
        mu = jnp.where(neg, xbits + 0x8000, xbits - 0x8000)
        odd = (xbits & 0x10000) != 0
        tbits = jnp.where(odd, jnp.where(neg, mu - 1, mu + 1), mu)
        t = lax.bitcast_convert_type(tbits, F32)
        ok = jnp.logical_and(jnp.abs(xb.astype(F32)) > _COARSE_MIN_ABS, jnp.abs(xb.astype(F32)) < _COARSE_MAX_ABS)
        return xb, t, ok

    def narrow(st, probe, c, active):
        lo, hi, clo, chi, done = st
        up = jnp.logical_and(active, c >= kf)
        dn = jnp.logical_and(active, c < kf)
        return (jnp.where(up, probe, lo), jnp.where(dn, probe, hi), jnp.where(up, c, clo),
                jnp.where(dn, c, chi), jnp.where(jnp.logical_and(active, c == kf), 1.0, done))

    def coarse_body(_, st):
        lo, hi, _, _, done = st
        xb, t, ok = coarse_probe(0.5 * lo + 0.5 * hi)
        ok = jnp.logical_and(ok, jnp.logical_and(t > lo, t < hi))
        return narrow(st, t, count_ge16(xb), jnp.logical_and(done < 0.5, ok))

    def fine_cond(st):
        it, (_, _, _, _, done) = st
        return jnp.logical_and(it < _BISECT_MAX_ITERS, jnp.min(done) < 0.5)

    def fine_body(st):
        it, inner = st
        lo, hi, _, _, done = inner
        mid = 0.5 * lo + 0.5 * hi
        mid = jnp.where(jnp.logical_and(lo == 0.0, hi > _F32_TINY), _F32_TINY, mid)
        mid = jnp.where(jnp.logical_and(lo < 0.0, hi > 0.0), 0.0, mid)
        stuck = jnp.logical_or(jnp.logical_or(mid <= lo, mid >= hi),
                               jnp.logical_and(lo == 0.0, hi <= _F32_TINY))
        lo, hi, clo, chi, done = narrow(inner, mid, count_ge(mid),
                                        jnp.logical_and(done < 0.5, jnp.logical_not(stuck)))
        return it + 1, (lo, hi, clo, chi, jnp.where(stuck, 1.0, done))

    mbits = lax.bitcast_convert_type(rowmax, I32)
    above_max = jnp.where(rowmax == 0.0, _F32_TINY,
                          lax.bitcast_convert_type(jnp.where(rowmax > 0.0, mbits + 1, mbits - 1), F32))
    st0 = (jnp.where(small, NEG_INF, rowmin), above_max, nfin, jnp.zeros((1, tq), F32),
           jnp.where(small, 1.0, 0.0))
    st1 = lax.fori_loop(0, _COARSE_PASSES, coarse_body, st0)
    _, (lo, hi, clo, chi, _) = lax.while_loop(fine_cond, fine_body, (jnp.int32(0), st1))
    budget = kf - chi
    need_ties = jnp.max((clo - chi) - budget) > 0.5

    pairs = [slice(LANES * (h // 2), LANES * (h // 2 + 1)) for h in range(ATTN_HEADS)]

    def issue_qk(jb, slot):
        kb = k_ref[key_rows(jnp.minimum(jb, last_kv)), :]
        for h in range(ATTN_HEADS):
            lg_sc[slot, h] = _dot(kb[:, pairs[h]], qm_sc[h]).astype(BF16)

    def attend(jb, slot, thr, cum=None):
        s = s_sc[jb]
        if cum is None:
            sel = s >= thr
        else:
            cand = jnp.logical_and(s >= lo, s < hi)
            before = (qcol < krow).astype(BF16)
            rank = cum + _dot(before, jnp.where(cand, 1.0, 0.0).astype(BF16))
            sel = jnp.logical_or(s >= hi, jnp.logical_and(cand, rank < budget))
        kind = jnp.where(jb > i, _KIND_PAD, jnp.minimum(i - jb, _KIND_FAR))
        selm = jnp.where(sel, 0.0, NEG_INF).astype(BF16)
        probs, alphas = [], []
        for h in range(ATTN_HEADS):
            lg = lg_sc[slot, h] + bias_sc[h, kind] + selm
            m_old = m_sc[h]
            m_new = jnp.maximum(m_old, jnp.max(lg, axis=0, keepdims=True).astype(F32))
            m_safe = jnp.where(m_new == NEG_INF, 0.0, m_new)
            alphas.append(jnp.exp2(m_old - m_safe))
            probs.append(jnp.exp2(lg - m_safe.astype(BF16)))
            m_sc[h] = m_new
        vb = vt_ref[jnp.minimum(jb, last_kv)]
        for h in range(ATTN_HEADS):
            vrows = slice(_VT_ROWS * h, _VT_ROWS * (h + 1))
            acc_sc[h] = alphas[h] * acc_sc[h] + _dot(vb[vrows, :], probs[h])

    def run(with_ties):
        m_sc[...] = jnp.full(m_sc.shape, NEG_INF, F32)
        acc_sc[...] = jnp.zeros_like(acc_sc)
        issue_qk(0, 0)

        def pair(j, thr0, thr1, cum0=None, cum1=None):
            issue_qk(2 * j + 1, 1)
            attend(2 * j, 0, thr0, cum0)
            issue_qk(2 * j + 2, 0)
            attend(2 * j + 1, 1, thr1, cum1)

        if not with_ties:
            def quad_body(q, carry):
                pair(2 * q, lo, lo)
                pair(2 * q + 1, lo, lo)
                return carry

            def pair_body(j, carry):
                pair(j, lo, lo)
                return carry
            lax.fori_loop(0, nquad, quad_body, 0)
            lax.fori_loop(2 * nquad, npair, pair_body, 0)
            return

        last_s = s_sc.shape[0] - 1

        def cand_count(jb):
            s = s_sc[jnp.minimum(jb, last_s)]
            cand = jnp.logical_and(s >= lo, s < hi)
            return jnp.sum(jnp.where(cand, 1.0, 0.0), axis=0, keepdims=True)

        def needs_ranking(cum0, ca, cb):
            cum1 = cum0 + ca
            runs_out = jnp.logical_or(jnp.logical_and(cum1 > budget, cum0 < budget),
                                      jnp.logical_and(cum1 + cb > budget, cum1 < budget))
            return (jnp.max(jnp.where(runs_out, 1.0, 0.0)) > 0.5).astype(I32)

        def tie_pair_body(j, carry):
            ranked, cum0, ca, cb = carry
            cum1 = cum0 + ca
            cum2 = cum1 + cb

            def variant(with_rank):
                def fn():
                    na, nb = cand_count(2 * j + 2), cand_count(2 * j + 3)
                    nxt = needs_ranking(cum2, na, nb)
                    if with_rank:
                        pair(j, None, None, cum0, cum1)
                    else:
                        pair(j, jnp.where(cum1 <= budget, lo, hi), jnp.where(cum2 <= budget, lo, hi))
                    return nxt, cum2, na, nb
                return fn
            return lax.cond(ranked == 1, variant(True), variant(False))

        c0, c1 = cand_count(0), cand_count(1)
        zero = jnp.zeros((1, tq), F32)
        lax.fori_loop(0, npair, tie_pair_body, (needs_ranking(zero, c0, c1), zero, c0, c1))

    @pl.when(need_ties)
    def _():
        run(True)

    @pl.when(jnp.logical_not(need_ties))
    def _():
        run(False)

    hd = ATTN_HEAD_DIM
    normed = [acc_sc[h, 0:hd, :] / acc_sc[h, hd:hd + 1, :] for h in range(ATTN_HEADS)]
    for p in range(ATTN_HEADS // 2):
        pair_t = jnp.concatenate([normed[2 * p], normed[2 * p + 1]], axis=0)
        o_ref[:, LANES * p:LANES * (p + 1)] = jnp.transpose(pair_t)


def _dsa_call(rel_bias, qi, wi, q, ki2, k, vt, *, tq):
    s = q.shape[0]
    assert tq > REL_MAX_DIST and s % tq == 0
    nb = s // tq
    topk = min(TOPK_MAX, s // 4)
    kern = functools.partial(_dsa_kernel, tq=tq, topk=topk,
                             idx_scale=(IDX_DIM ** -0.5) * (IDX_HEADS ** -0.5))
    whole = lambda width: pl.BlockSpec((s, width), lambda i: (0, 0), pipeline_mode=pl.Buffered(1))
    assert vt.shape == (nb, ATTN_HEADS * _VT_ROWS, tq)
    return pl.pallas_call(
        kern,
        grid=(nb,),
        in_specs=[pl.BlockSpec(memory_space=pltpu.SMEM),
                  pl.BlockSpec((tq, IDX_HEADS * IDX_DIM), lambda i: (i, 0)),
                  pl.BlockSpec((tq, LANES), lambda i: (i, 0)),
                  pl.BlockSpec((tq, ATTN_WIDTH), lambda i: (i, 0)),
                  whole(2 * IDX_DIM), whole(ATTN_WIDTH),
                  pl.BlockSpec(vt.shape, lambda i: (0, 0, 0), pipeline_mode=pl.Buffered(1))],
        out_specs=pl.BlockSpec((tq, ATTN_WIDTH), lambda i: (i, 0)),
        out_shape=jax.ShapeDtypeStruct((s, ATTN_WIDTH), F32),
        scratch_shapes=[pltpu.VMEM((nb + nb % 2, tq, tq), F32),
                        pltpu.VMEM((nb + nb % 2, tq, tq), BF16),
                        pltpu.VMEM((ATTN_HEADS, _N_KINDS, tq, tq), BF16),
                        pltpu.VMEM((3, tq, tq), F32),
                        pltpu.VMEM((2, ATTN_HEADS, tq, tq), BF16),
                        pltpu.VMEM((IDX_HEADS, LANES, tq), BF16),
                        pltpu.VMEM((ATTN_HEADS, LANES, tq), BF16),
                        pltpu.VMEM((ATTN_HEADS, 1, tq), F32),
                        pltpu.VMEM((ATTN_HEADS, _VT_ROWS, tq), F32)],
        compiler_params=_cparams("arbitrary"),
        name="dsa_attention",
    )(rel_bias, qi, wi, q, ki2, k, vt)


_HGRN_CHUNK = 128
_HGRN_BASE = SUBLANES


def _hgrn_kernel(h4_ref, lb_ref, ng_ref, o_ref, st_sc, *, layer, rows):
    c_len = _HGRN_CHUNK

    @pl.when(pl.program_id(0) == 0)
    def _():
        st_sc[...] = jnp.zeros_like(st_sc)

    lb_all = lb_ref[...]
    e = jnp.exp(lb_all - jnp.max(lb_all, axis=0, keepdims=True))
    sm = e / jnp.sum(e, axis=0, keepdims=True)
    lbs = jnp.sum(sm[0:layer + 1, :], axis=0, keepdims=True) - sm[0:1, :]

    r1 = lax.broadcasted_iota(I32, (c_len, 1), 0)
    rr = lax.broadcasted_iota(I32, (c_len, c_len), 0)
    cc = lax.broadcasted_iota(I32, (c_len, c_len), 1)

    def block_ref(a, blk, r):
        x = a.reshape(c_len // blk, blk, a.shape[-1])[:, r:r + 1, :]
        return jnp.broadcast_to(x, (c_len // blk, blk, a.shape[-1])).reshape(c_len, a.shape[-1])

    w = HGRN_WIDTH
    for c in range(rows // c_len):
        rs = slice(c * c_len, (c + 1) * c_len)
        hq = h4_ref[rs, 0:w]
        z = h4_ref[rs, w:2 * w]
        val = h4_ref[rs, 2 * w:3 * w]
        hg = h4_ref[rs, 3 * w:4 * w]
        key = (1.0 - lbs) * jax.nn.sigmoid(-z)
        g = jnp.log1p(-jnp.minimum(key, KEY_MAX))
        a_cum = g
        sh = 1
        while sh < c_len:
            a_cum = a_cum + jnp.where(r1 >= sh, pltpu.roll(a_cum, sh, 0), 0.0)
            sh *= 2
        qf = _silu(hq)
        outs = []
        for h in range(HGRN_HEADS):
            hs = slice(h * HGRN_DIM, (h + 1) * HGRN_DIM)
            ah, qh, kh, vh = a_cum[:, hs], qf[:, hs], key[:, hs], val[:, hs]
            vh16 = vh.astype(BF16)
            st = st_sc[h]
            o = _dot_nt((qh * jnp.exp(ah)).astype(BF16), st.astype(BF16))
            ref = block_ref(ah, _HGRN_BASE, _HGRN_BASE // 2 - 1)
            sc = _dot_nt((qh * jnp.exp(ah - ref)).astype(BF16), (kh * jnp.exp(ref - ah)).astype(BF16))
            same = (rr >> _log2(_HGRN_BASE)) == (cc >> _log2(_HGRN_BASE))
            scores = jnp.where(jnp.logical_and(same, cc <= rr), sc, 0.0)
            half = _HGRN_BASE
            while half < c_len:
                blk = 2 * half
                ref = block_ref(ah, blk, half - 1)
                second = (r1 & (blk - 1)) >= half
                ql = jnp.where(second, qh * jnp.exp(jnp.minimum(ah - ref, 0.0)), 0.0)
                kl = jnp.where(second, 0.0, kh * jnp.exp(jnp.minimum(ref - ah, 0.0)))
                sc = _dot_nt(ql.astype(BF16), kl.astype(BF16))
                scores = scores + jnp.where((rr >> _log2(blk)) == (cc >> _log2(blk)), sc, 0.0)
                half = blk
            o = o + _dot(scores.astype(BF16), vh16)
            a_last = ah[c_len - 1:c_len, :]
            kd = (kh * jnp.exp(a_last - ah)).astype(BF16)
            st_sc[h] = st * jnp.exp(a_last) + _dot_tn(vh16, kd)
            o = o * lax.rsqrt(jnp.mean(o * o, axis=-1, keepdims=True) + RMS_EPS)
            outs.append(o)
        o_ref[rs, :] = jnp.concatenate(outs, axis=1) * ng_ref[...] * _silu(hg)


def _hgrn_call(h4, lb, ng, *, layer, rows):
    s = h4.shape[0]
    kern = functools.partial(_hgrn_kernel, layer=layer, rows=rows)
    return pl.pallas_call(
        kern,
        grid=(s // rows,),
        in_specs=[pl.BlockSpec((rows, 4 * HGRN_WIDTH), lambda i: (i, 0)),
                  pl.BlockSpec(lb.shape, lambda i: (0, 0)),
                  _slab(ng, layer)],
        out_specs=pl.BlockSpec((rows, HGRN_WIDTH), lambda i: (i, 0)),
        out_shape=jax.ShapeDtypeStruct((s, HGRN_WIDTH), F32),
        scratch_shapes=[pltpu.VMEM((HGRN_HEADS, HGRN_DIM, HGRN_DIM), F32)],
        compiler_params=_cparams("arbitrary"),
        name="hgrn2",
    )(h4, lb, ng)


_POOL_HALO = 16


def _merge_kernel(x_ref, mod_ref, a_ref, ah_ref, yb_ref, yc_ref, g_ref, pw_ref, ps_ref, wb_ref, wo_ref,
                  lng_ref, lnb_ref, fwi_ref, fwo_ref, o_ref, *, tm, tf, alpha):
    i = pl.program_id(0)
    d = x_ref.shape[1]
    _, _, gate = _mod_rows(mod_ref, 1)
    a = a_ref[...]
    halo = jnp.where(i == 0, 0.0, ah_ref[...])
    ext = jnp.concatenate([halo, a], axis=0)
    lane = lax.broadcasted_iota(I32, (tm, POOL_WIDTH), 1)
    grp = lane >> _log2(POOL_GROUP)
    pos1 = (i * tm + lax.broadcasted_iota(I32, (tm, POOL_WIDTH), 0) + 1).astype(F32)
    win = jnp.zeros((tm, POOL_WIDTH), F32)
    cnt = jnp.zeros((tm, POOL_WIDTH), F32)
    ssum, have = ext, 1
    for gi, wd in enumerate(POOL_WINDOWS):
        while have < wd:
            ssum = ssum + pltpu.roll(ssum, have, 0)
            have *= 2
        assert have == wd
        win = jnp.where(grp == gi, ssum[_POOL_HALO:, :], win)
        cnt = jnp.where(grp == gi, jnp.minimum(pos1, float(wd)), cnt)
    dlt = (win / cnt - a).astype(BF16)
    ya = _dot(dlt, pw_ref[...]) * ps_ref[...]
    za = _dot(ya.astype(BF16), wb_ref[0:POOL_WIDTH, :])
    zb = _dot(yb_ref[...].astype(BF16), wb_ref[POOL_WIDTH:POOL_WIDTH + ATTN_WIDTH, :])
    zc = _dot(yc_ref[...].astype(BF16), wb_ref[POOL_WIDTH + ATTN_WIDTH:, :])
    g0, g1, g2 = (jax.nn.sigmoid(g_ref[:, j * d:(j + 1) * d].astype(F32)) for j in range(3))
    merged = g0 * za + g1 * zb + g2 * zc
    y = _dot(merged.astype(BF16), wo_ref[...])
    z = alpha * x_ref[...] + (1.0 + gate) * y
    xm = _layer_norm(z, lng_ref[1:2, :], lnb_ref[1:2, :])
    o_ref[...] = _ffn_rows(xm, mod_ref, fwi_ref, fwo_ref, lng_ref, lnb_ref, sub=2, tf=tf, alpha=alpha)


def _merge_call(x, mod, a, yb, yc, gates, pw_bd, ps, wb, wo, ln_g, ln_b, ffn_wi, ffn_wo, *, layer, tm, tf, alpha):
    s, d = x.shape
    kern = functools.partial(_merge_kernel, tm=tm, tf=tf, alpha=alpha)
    full = lambda arr: _slab(arr, layer)
    hb = tm // _POOL_HALO
    return pl.pallas_call(
        kern,
        grid=(s // tm,),
        in_specs=[pl.BlockSpec((tm, d), lambda i: (i, 0)),
                  pl.BlockSpec((3 * N_SUB, d), lambda i: (0, 0)),
                  pl.BlockSpec((tm, POOL_WIDTH), lambda i: (i, 0)),
                  pl.BlockSpec((_POOL_HALO, POOL_WIDTH), lambda i: (jnp.maximum(i * hb - 1, 0), 0)),
                  pl.BlockSpec((tm, ATTN_WIDTH), lambda i: (i, 0)),
                  pl.BlockSpec((tm, HGRN_WIDTH), lambda i: (i, 0)),
                  pl.BlockSpec((tm, 3 * d), lambda i: (i, 0)),
                  full(pw_bd), full(ps), full(wb), full(wo), full(ln_g), full(ln_b),
                  _slab(ffn_wi, layer, 1), _slab(ffn_wo, layer, 1)],
        out_specs=pl.BlockSpec((tm, d), lambda i: (i, 0)),
        out_shape=jax.ShapeDtypeStruct((s, d), F32),
        compiler_params=_cparams("arbitrary"),
        name="merge_ffn",
    )(x, mod, a, a, yb, yc, gates, pw_bd, ps, wb, wo, ln_g, ln_b, ffn_wi, ffn_wo)


def _pool_blockdiag(pool_w):
    depth, n, g, _ = pool_w.shape
    eye = jnp.eye(n, dtype=pool_w.dtype)
    return jnp.einsum("lgij,gh->lgihj", pool_w, eye).reshape(depth, n * g, n * g)


def kernel(x, c, w_ada, b_ada, ln_g, ln_b, ffn_w_in, ffn_w_out, mix_w_in, pool_w, pool_scale, rel_bias,
           hgrn_lb, hgrn_norm_g, w_branch, w_out):
    batch, s, d = x.shape
    assert batch == 1, "kernels are written for a single sequence"
    depth = w_ada.shape[0]
    alpha = (2 * depth) ** 0.25
    tm_ffn = min(512, s)
    tf = 256
    tq = min(256, s)
    tm_proj = tq
    tm_merge = min(512, s)
    hgrn_rows = min(1024, s)

    ffn_wi16, ffn_wo16 = ffn_w_in.astype(BF16), ffn_w_out.astype(BF16)
    mix_w16 = _proj_weight(mix_w_in)
    pool_bd16 = _pool_blockdiag(pool_w).astype(BF16)
    wb16, wo16 = w_branch.astype(BF16), w_out.astype(BF16)
    pool_scale3 = pool_scale.reshape(depth, 1, -1)
    norm_g3 = hgrn_norm_g.reshape(depth, 1, -1)

    mods = _ada_call(c, w_ada, b_ada)
    xs = x.reshape(s, d)
    for l in range(depth):
        mod = mods[l].reshape(3 * N_SUB, d)
        xs = _ffn_call(xs, mod, ffn_wi16, ffn_wo16, ln_g, ln_b, layer=l, which=0, sub=0, alpha=alpha,
                       tm=tm_ffn, tf=tf)
        gates, a, h4, qi, q, k, vt, ki2, wi = _proj_call(xs, mod, mix_w16, layer=l, tm=tm_proj)
        yb = _dsa_call(rel_bias, qi, wi, q, ki2, k, vt, tq=tq)
        yc = _hgrn_call(h4, hgrn_lb, norm_g3, layer=l, rows=hgrn_rows)
        xs = _merge_call(xs, mod, a, yb, yc, gates, pool_bd16, pool_scale3, wb16, wo16, ln_g, ln_b,
                         ffn_wi16, ffn_wo16, layer=l, tm=tm_merge, tf=tf, alpha=alpha)
    return xs.reshape(batch, s, d)
```

```python
import functools
import math

import jax
import jax.numpy as jnp
from jax import lax
from jax.experimental import pallas as pl
from jax.experimental.pallas import tpu as pltpu

F32 = jnp.float32
BF16 = jnp.bfloat16
I32 = jnp.int32

POOL_WINDOWS = (2, 4, 8, 16)
POOL_GROUP = 64
POOL_WIDTH = POOL_GROUP * len(POOL_WINDOWS)
ATTN_HEADS = 4
ATTN_HEAD_DIM = 64
ATTN_WIDTH = ATTN_HEADS * ATTN_HEAD_DIM
IDX_HEADS = 8
IDX_DIM = 64
TOPK_MAX = 256
HGRN_HEADS = 4
HGRN_DIM = 128
HGRN_WIDTH = HGRN_HEADS * HGRN_DIM
KEY_MAX = 1.0 - 1e-6
REL_BUCKETS = 32
REL_MAX_DIST = 128
N_SUB = 3
LN_EPS = 1e-5
RMS_EPS = 1e-6

LANES = 128
SUBLANES = 8
VMEM_LIMIT_BYTES = 58 * 1024 * 1024

NEG_INF = float("-inf")
POS_INF = float("inf")


def _cparams(*sem):
    return pltpu.CompilerParams(dimension_semantics=sem, vmem_limit_bytes=VMEM_LIMIT_BYTES)


def _dot(a, b):
    return jnp.dot(a, b, preferred_element_type=F32)


def _dot_nt(a, b):
    return lax.dot_general(a, b, (((1,), (1,)), ((), ())), preferred_element_type=F32)


def _dot_tn(a, b):
    return lax.dot_general(a, b, (((0,), (0,)), ((), ())), preferred_element_type=F32)


def _log2(n):
    assert n & (n - 1) == 0
    return n.bit_length() - 1


def _silu(x):
    return x * jax.nn.sigmoid(x)


def _mod_rows(mod_ref, j):
    return mod_ref[3 * j:3 * j + 1, :], mod_ref[3 * j + 1:3 * j + 2, :], mod_ref[3 * j + 2:3 * j + 3, :]


def _layer_norm(z, g, b):
    mu = jnp.mean(z, axis=-1, keepdims=True)
    zc = z - mu
    var = jnp.mean(zc * zc, axis=-1, keepdims=True)
    return zc * lax.rsqrt(var + LN_EPS) * g + b


def _ada_kernel(c_ref, w_ref, b_ref, o_ref):
    @pl.when(pl.program_id(1) == 0)
    def _():
        o_ref[0] = b_ref[0]

    c = c_ref[...]
    o_ref[0] += jnp.sum(_silu(c) * w_ref[0], axis=0, keepdims=True)


def _ada_call(c, w_ada, b_ada):
    depth, d, n = w_ada.shape
    rows = LANES
    return pl.pallas_call(
        _ada_kernel,
        grid=(depth, d // rows),
        in_specs=[pl.BlockSpec((rows, 1), lambda l, k: (k, 0)),
                  pl.BlockSpec((1, rows, n), lambda l, k: (l, k, 0)),
                  pl.BlockSpec((1, 1, n), lambda l, k: (l, 0, 0))],
        out_specs=pl.BlockSpec((1, 1, n), lambda l, k: (l, 0, 0)),
        out_shape=jax.ShapeDtypeStruct((depth, 1, n), F32),
        compiler_params=_cparams("arbitrary", "arbitrary"),
        name="ada_mod",
    )(c.reshape(d, 1), w_ada, b_ada.reshape(depth, 1, n))


def _ffn_rows(x, mod_ref, wi_ref, wo_ref, lng_ref, lnb_ref, *, sub, tf, alpha):
    shift, scale, gate = _mod_rows(mod_ref, sub)
    ff = wo_ref.shape[0]
    h = (x * (1.0 + scale) + shift).astype(BF16)
    y = jnp.zeros(x.shape, F32)
    for c0 in range(0, ff, tf):
        g = _dot(h, wi_ref[:, c0:c0 + tf])
        u = _dot(h, wi_ref[:, ff + c0:ff + c0 + tf])
        y = y + _dot((_silu(g) * u).astype(BF16), wo_ref[c0:c0 + tf, :])
    z = alpha * x + (0.5 * (1.0 + gate)) * y
    return _layer_norm(z, lng_ref[sub:sub + 1, :], lnb_ref[sub:sub + 1, :])


def _ffn_kernel(x_ref, mod_ref, wi_ref, wo_ref, lng_ref, lnb_ref, o_ref, *, sub, tf, alpha):
    o_ref[...] = _ffn_rows(x_ref[...], mod_ref, wi_ref, wo_ref, lng_ref, lnb_ref, sub=sub, tf=tf, alpha=alpha)


def _ffn_call(x, mod, w_in, w_out, ln_g, ln_b, *, layer, which, sub, alpha, tm, tf):
    s, d = x.shape
    assert w_out.shape[2] % tf == 0
    kern = functools.partial(_ffn_kernel, sub=sub, tf=tf, alpha=alpha)
    return pl.pallas_call(
        kern,
        grid=(s // tm,),
        in_specs=[pl.BlockSpec((tm, d), lambda i: (i, 0)),
                  pl.BlockSpec((3 * N_SUB, d), lambda i: (0, 0)),
                  _slab(w_in, layer, which), _slab(w_out, layer, which),
                  _slab(ln_g, layer), _slab(ln_b, layer)],
        out_specs=pl.BlockSpec((tm, d), lambda i: (i, 0)),
        out_shape=jax.ShapeDtypeStruct((s, d), F32),
        compiler_params=_cparams("arbitrary"),
        name="ffn",
    )(x, mod, w_in, w_out, ln_g, ln_b)


_PROJ_OUTS = (("gates", 3 * 1024, BF16), ("a", POOL_WIDTH, F32), ("h4", 4 * HGRN_WIDTH, F32),
              ("qi", IDX_HEADS * IDX_DIM, BF16), ("q", ATTN_WIDTH, BF16), ("k", ATTN_WIDTH, BF16),
              ("v", ATTN_WIDTH, BF16), ("ki2", 2 * IDX_DIM, BF16), ("wi", LANES, F32))
_PROJ_CHUNK = 1024


def _proj_kernel(x_ref, mod_ref, w_ref, *o_refs):
    shift, scale, _ = _mod_rows(mod_ref, 1)
    h = (x_ref[...] * (1.0 + scale) + shift).astype(BF16)
    c0 = 0
    for (name, width, dt), o_ref in zip(_PROJ_OUTS, o_refs):
        if name == "v":
            vt = lax.dot_general(w_ref[:, c0:c0 + width], h, (((0,), (1,)), ((), ())),
                                 preferred_element_type=F32)
            ones = jnp.ones((_VT_ROWS - ATTN_HEAD_DIM, vt.shape[1]), F32)
            rows = []
            for p0 in range(0, width, ATTN_HEAD_DIM):
                rows += [vt[p0:p0 + ATTN_HEAD_DIM, :], ones]
            o_ref[0] = jnp.concatenate(rows, axis=0).astype(dt)
        else:
            for cc in range(0, width, _PROJ_CHUNK):
                cw = min(_PROJ_CHUNK, width - cc)
                o_ref[:, cc:cc + cw] = _dot(h, w_ref[:, c0 + cc:c0 + cc + cw]).astype(dt)
        c0 += width


def _proj_weight_kernel(w_ref, o_ref):
    out_width = {name: wd for name, wd, _ in _PROJ_OUTS}
    src = {}
    c0 = 0
    for name, wd in (("a", POOL_WIDTH), ("q", ATTN_WIDTH), ("k", ATTN_WIDTH), ("v", ATTN_WIDTH),
                     ("qi", IDX_HEADS * IDX_DIM), ("ki", IDX_DIM), ("wi", IDX_HEADS),
                     ("h4", 4 * HGRN_WIDTH), ("gates", out_width["gates"])):
        src[name] = (c0, wd)
        c0 += wd
    assert c0 == w_ref.shape[-1]
    q_scale = ATTN_HEAD_DIM ** -0.5 * _LOG2E
    dst = 0
    for name, width, _ in _PROJ_OUTS:
        if name == "ki2":
            pieces = [("ki", 1.0), ("ki", 1.0)]
        elif name == "wi":
            o_ref[:, dst:dst + width] = jnp.zeros((o_ref.shape[0], width), BF16)
            pieces = [("wi", 1.0)]
        else:
            pieces = [(name, q_scale if name == "q" else 1.0)]
        off = dst
        for pname, scale in pieces:
            s0, wd = src[pname]
            o_ref[:, off:off + wd] = (w_ref[:, s0:s0 + wd] * scale).astype(BF16)
            off += wd
        dst += width


def _proj_weight(w):
    depth, d, n_in = w.shape
    n_out = sum(wd for _, wd, _ in _PROJ_OUTS)
    rows = 256
    return pl.pallas_call(
        _proj_weight_kernel,
        grid=(depth, d // rows),
        in_specs=[pl.BlockSpec((None, rows, n_in), lambda l, r: (l, r, 0))],
        out_specs=pl.BlockSpec((None, rows, n_out), lambda l, r: (l, r, 0)),
        out_shape=jax.ShapeDtypeStruct((depth, d, n_out), BF16),
        compiler_params=_cparams("arbitrary", "arbitrary"),
        name="proj_weight",
    )(w)


def _slab(arr, *lead):
    block = (None,) * len(lead) + tuple(arr.shape[len(lead):])
    index = tuple(lead) + (0,) * (arr.ndim - len(lead))
    return pl.BlockSpec(block, lambda i: index, pipeline_mode=pl.Buffered(1))


def _proj_call(x, mod, w_all, *, layer, tm):
    s, d = x.shape
    n = w_all.shape[-1]
    assert n == sum(wd for _, wd, _ in _PROJ_OUTS)
    n_vt = ATTN_HEADS * _VT_ROWS
    out_specs = [pl.BlockSpec((1, n_vt, tm), lambda i: (i, 0, 0)) if name == "v"
                 else pl.BlockSpec((tm, wd), lambda i: (i, 0)) for name, wd, _ in _PROJ_OUTS]
    out_shape = [jax.ShapeDtypeStruct((s // tm, n_vt, tm) if name == "v" else (s, wd), dt)
                 for name, wd, dt in _PROJ_OUTS]
    return pl.pallas_call(
        _proj_kernel,
        grid=(s // tm,),
        in_specs=[pl.BlockSpec((tm, d), lambda i: (i, 0)),
                  pl.BlockSpec((3 * N_SUB, d), lambda i: (0, 0)),
                  _slab(w_all, layer)],
        out_specs=out_specs,
        out_shape=out_shape,
        compiler_params=_cparams("arbitrary"),
        name="mix_proj",
    )(x, mod, w_all)


_BISECT_MAX_ITERS = 320
_F32_TINY = float(jnp.finfo(jnp.float32).tiny)
_COARSE_PASSES = 9
_COARSE_MIN_ABS, _COARSE_MAX_ABS = 1e-30, 1e30
_KIND_FAR, _KIND_PAD, _N_KINDS = 2, 3, 4
_VT_ROWS = ATTN_HEAD_DIM + 16
_SMASK_FULL, _SMASK_DIAG, _SMASK_PAD = 0, 1, 2
_LOG2E = math.log2(math.e)


def _t5_bucket(dist):
    max_exact = REL_BUCKETS // 2
    d32 = jnp.maximum(dist, 1).astype(F32)
    large = max_exact + (jnp.log(d32 / max_exact) / math.log(REL_MAX_DIST / max_exact)
                         * (REL_BUCKETS - max_exact)).astype(I32)
    large = jnp.minimum(large, REL_BUCKETS - 1)
    return jnp.where(dist < max_exact, dist, large)


def _dsa_kernel(relb_ref, qi_ref, wi_ref, q_ref, ki2_ref, k_ref, vt_ref, o_ref,
                s_sc, sb_sc, bias_sc, smask_sc, lg_sc, qim_sc, qm_sc, m_sc, acc_sc,
                *, tq, topk, idx_scale):
    tk = tq
    i = pl.program_id(0)
    krow = lax.broadcasted_iota(I32, (tk, tq), 0)
    qcol = lax.broadcasted_iota(I32, (tk, tq), 1)
    causal = krow <= qcol
    lane = lax.broadcasted_iota(I32, (tq, LANES), 1)
    lo_half = lane < ATTN_HEAD_DIM

    @pl.when(i == 0)
    def _():
        for which in range(2):
            bucket = _t5_bucket(jnp.maximum(qcol - krow + which * tq, 0))
            for h in range(ATTN_HEADS):
                b = jnp.zeros((tk, tq), F32)
                for bk in range(REL_BUCKETS):
                    b = jnp.where(bucket == bk, relb_ref[bk, h] * _LOG2E, b)
                bias_sc[h, which] = (jnp.where(causal, b, NEG_INF) if which == 0 else b).astype(BF16)
        for h in range(ATTN_HEADS):
            bias_sc[h, _KIND_FAR] = jnp.full((tk, tq), relb_ref[REL_BUCKETS - 1, h] * _LOG2E, F32).astype(BF16)
            bias_sc[h, _KIND_PAD] = jnp.full((tk, tq), NEG_INF, BF16)
        smask_sc[_SMASK_FULL] = jnp.zeros((tk, tq), F32)
        smask_sc[_SMASK_DIAG] = jnp.where(causal, 0.0, NEG_INF)
        smask_sc[_SMASK_PAD] = jnp.full((tk, tq), NEG_INF, F32)

    def head_operands(src_ref, dst_sc, n_heads):
        src = src_ref[...].astype(F32)
        for h in range(n_heads):
            pair = src[:, LANES * (h // 2):LANES * (h // 2 + 1)]
            pair = jnp.where(lo_half if h % 2 == 0 else ~lo_half, pair, 0.0)
            dst_sc[h] = jnp.transpose(pair).astype(BF16)

    head_operands(qi_ref, qim_sc, IDX_HEADS)
    head_operands(q_ref, qm_sc, ATTN_HEADS)

    wt = jnp.transpose(wi_ref[...])

    def key_rows(jb):
        return pl.ds(pl.multiple_of(jb * tk, tk), tk)

    def fold(x):
        return x.reshape(tk // SUBLANES, SUBLANES, tq)

    def index_scores(jb):
        kb = ki2_ref[key_rows(jb), :]
        acc = jnp.zeros((tk, tq), F32)
        for h in range(IDX_HEADS):
            acc = acc + wt[h:h + 1, :] * jnp.maximum(_dot(kb, qim_sc[h]), 0.0)
        return acc * idx_scale

    npair = (i + 2) // 2
    last_kv = k_ref.shape[0] // tk - 1

    def p1_blocks(jbs, carry):
        rmin, rmax = carry
        for jb in jbs:
            raw = index_scores(jnp.minimum(jb, last_kv))
            kind = jnp.where(jb > i, _SMASK_PAD, jnp.where(jb == i, _SMASK_DIAG, _SMASK_FULL))
            sc = raw + smask_sc[kind]
            s_sc[jb] = sc
            sb_sc[jb] = sc.astype(BF16)
            rmin = jnp.minimum(rmin, jnp.min(fold(raw), axis=0))
            rmax = jnp.maximum(rmax, jnp.max(fold(raw), axis=0))
        return rmin, rmax

    nquad = npair // 2
    p1_carry = lax.fori_loop(0, nquad, lambda q, c: p1_blocks([4 * q + r for r in range(4)], c),
                             (jnp.full((SUBLANES, tq), POS_INF, F32), jnp.full((SUBLANES, tq), NEG_INF, F32)))
    rmin, rmax = lax.fori_loop(2 * nquad, npair, lambda j, c: p1_blocks([2 * j, 2 * j + 1], c), p1_carry)
    rowmin = jnp.min(rmin, axis=0, keepdims=True)
    rowmax = jnp.max(rmax, axis=0, keepdims=True)

    kf = float(topk)
    nfin = (i * tq + lax.broadcasted_iota(I32, (1, tq), 1) + 1).astype(F32)
    small = nfin <= kf

    def walk_blocks(blocks_fn, init):
        return lax.fori_loop(0, npair, lambda j, c: blocks_fn([2 * j, 2 * j + 1], c), init)

    def count_ge(x):
        def blocks_fn(jbs, cnt):
            for jb in jbs:
                cnt = cnt + jnp.sum(fold(jnp.where(s_sc[jb] >= x, 1.0, 0.0)), axis=0)
            return cnt
        return jnp.sum(walk_blocks(blocks_fn, jnp.zeros((SUBLANES, tq), F32)), axis=0, keepdims=True)

    def count_ge16(xb):
        one, zero = jnp.ones((), BF16), jnp.zeros((), BF16)
        rows16 = 2 * SUBLANES

        def blocks_fn(jbs, cnt):
            parts = []
            for jb in jbs:
                c = jnp.where(sb_sc[jb] >= xb, one, zero)
                parts += [c[r0:r0 + rows16, :] for r0 in range(0, tk, rows16)]
            while len(parts) > 1:
                parts = [a + b for a, b in zip(parts[0::2], parts[1::2])]
            return cnt + parts[0].astype(F32)
        return jnp.sum(walk_blocks(blocks_fn, jnp.zeros((rows16, tq), F32)), axis=0, keepdims=True)

    def coarse_probe(mid):
        xb = mid.astype(BF16)
        xbits = lax.bitcast_convert_type(xb.astype(F32), I32)
        neg = xbits < 0
        mu = jnp.where(neg, xbits + 0x8000, xbits - 0x8000)
        odd = (xbits & 0x10000) != 0
        tbits = jnp.where(odd, jnp.where(neg, mu - 1, mu + 1), mu)
        t = lax.bitcast_convert_type(tbits, F32)
        ok = jnp.logical_and(jnp.abs(xb.astype(F32)) > _COARSE_MIN_ABS, jnp.abs(xb.astype(F32)) < _COARSE_MAX_ABS)
        return xb, t, ok

    def narrow(st, probe, c, active):
        lo, hi, clo, chi, done = st
        up = jnp.logical_and(active, c >= kf)
        dn = jnp.logical_and(active, c < kf)
        return (jnp.where(up, probe, lo), jnp.where(dn, probe, hi), jnp.where(up, c, clo),
                jnp.where(dn, c, chi), jnp.where(jnp.logical_and(active, c == kf), 1.0, done))

    def coarse_body(_, st):
        lo, hi, _, _, done = st
        xb, t, ok = coarse_probe(0.5 * lo + 0.5 * hi)
        ok = jnp.logical_and(ok, jnp.logical_and(t > lo, t < hi))
        return narrow(st, t, count_ge16(xb), jnp.logical_and(done < 0.5, ok))

    def fine_cond(st):
        it, (_, _, _, _, done) = st
        return jnp.logical_and(it < _BISECT_MAX_ITERS, jnp.min(done) < 0.5)

    def fine_body(st):
        it, inner = st
        lo, hi, _, _, done = inner
        mid = 0.5 * lo + 0.5 * hi
        mid = jnp.where(jnp.logical_and(lo == 0.0, hi > _F32_TINY), _F32_TINY, mid)
        mid = jnp.where(jnp.logical_and(lo < 0.0, hi > 0.0), 0.0, mid)
        stuck = jnp.logical_or(jnp.logical_or(mid <= lo, mid >= hi),
                               jnp.logical_and(lo == 0.0, hi <= _F32_TINY))
        lo, hi, clo, chi, done = narrow(inner, mid, count_ge(mid),
                                        jnp.logical_and(done < 0.5, jnp.logical_not(stuck)))
        return it + 1, (lo, hi, clo, chi, jnp.where(stuck, 1.0, done))

    mbits = lax.bitcast_convert_type(rowmax, I32)
    above_max = jnp.where(rowmax == 0.0, _F32_TINY,
                          lax.bitcast_convert_type(jnp.where(rowmax > 0.0, mbits + 1, mbits - 1), F32))
    st0 = (jnp.where(small, NEG_INF, rowmin), above_max, nfin, jnp.zeros((1, tq), F32),
           jnp.where(small, 1.0, 0.0))
    st1 = lax.fori_loop(0, _COARSE_PASSES, coarse_body, st0)
    _, (lo, hi, clo, chi, _) = lax.while_loop(fine_cond, fine_body, (jnp.int32(0), st1))
    budget = kf - chi
    need_ties = jnp.max((clo - chi) - budget) > 0.5

    pairs = [slice(LANES * (h // 2), LANES * (h // 2 + 1)) for h in range(ATTN_HEADS)]

    def issue_qk(jb, slot):
        kb = k_ref[key_rows(jnp.minimum(jb, last_kv)), :]
        for h in range(ATTN_HEADS):
            lg_sc[slot, h] = _dot(kb[:, pairs[h]], qm_sc[h]).astype(BF16)

    def attend(jb, slot, thr, cum=None):
        s = s_sc[jb]
        if cum is None:
            sel = s >= thr
        else:
            cand = jnp.logical_and(s >= lo, s < hi)
            before = (qcol < krow).astype(BF16)
            rank = cum + _dot(before, jnp.where(cand, 1.0, 0.0).astype(BF16))
            sel = jnp.logical_or(s >= hi, jnp.logical_and(cand, rank < budget))
        kind = jnp.where(jb > i, _KIND_PAD, jnp.minimum(i - jb, _KIND_FAR))
        selm = jnp.where(sel, 0.0, NEG_INF).astype(BF16)
        probs, alphas = [], []
        for h in range(ATTN_HEADS):
            lg = lg_sc[slot, h] + bias_sc[h, kind] + selm
            m_old = m_sc[h]
            m_new = jnp.maximum(m_old, jnp.max(lg, axis=0, keepdims=True).astype(F32))
            m_safe = jnp.where(m_new == NEG_INF, 0.0, m_new)
            alphas.append(jnp.exp2(m_old - m_safe))
            probs.append(jnp.exp2(lg - m_safe.astype(BF16)))
            m_sc[h] = m_new
        vb = vt_ref[jnp.minimum(jb, last_kv)]
        for h in range(ATTN_HEADS):
            vrows = slice(_VT_ROWS * h, _VT_ROWS * (h + 1))
            acc_sc[h] = alphas[h] * acc_sc[h] + _dot(vb[vrows, :], probs[h])

    def run(with_ties):
        m_sc[...] = jnp.full(m_sc.shape, NEG_INF, F32)
        acc_sc[...] = jnp.zeros_like(acc_sc)
        issue_qk(0, 0)

        def pair(j, thr0, thr1, cum0=None, cum1=None):
            issue_qk(2 * j + 1, 1)
            attend(2 * j, 0, thr0, cum0)
            issue_qk(2 * j + 2, 0)
            attend(2 * j + 1, 1, thr1, cum1)

        if not with_ties:
            def quad_body(q, carry):
                pair(2 * q, lo, lo)
                pair(2 * q + 1, lo, lo)
                return carry

            def pair_body(j, carry):
                pair(j, lo, lo)
                return carry
            lax.fori_loop(0, nquad, quad_body, 0)
            lax.fori_loop(2 * nquad, npair, pair_body, 0)
            return

        last_s = s_sc.shape[0] - 1

        def cand_count(jb):
            s = s_sc[jnp.minimum(jb, last_s)]
            cand = jnp.logical_and(s >= lo, s < hi)
            return jnp.sum(jnp.where(cand, 1.0, 0.0), axis=0, keepdims=True)

        def needs_ranking(cum, counts):
            runs_out = None
            for cnt in counts:
                here = jnp.logical_and(cum + cnt > budget, cum < budget)
                runs_out = here if runs_out is None else jnp.logical_or(runs_out, here)
                cum = cum + cnt
            return (jnp.max(jnp.where(runs_out, 1.0, 0.0)) > 0.5).astype(I32)

        def tie_group_body(n_blocks):
            def body(g, carry):
                ranked, cums = carry[0], [carry[1]]
                for cnt in carry[2:]:
                    cums.append(cums[-1] + cnt)
                base = n_blocks * g

                def variant(with_rank):
                    def fn():
                        ahead = [cand_count(base + n_blocks + r) for r in range(n_blocks)]
                        nxt = needs_ranking(cums[-1], ahead)
                        for p in range(n_blocks // 2):
                            if with_rank:
                                pair(base // 2 + p, None, None, cums[2 * p], cums[2 * p + 1])
                            else:
                                pair(base // 2 + p, jnp.where(cums[2 * p + 1] <= budget, lo, hi),
                                     jnp.where(cums[2 * p + 2] <= budget, lo, hi))
                        return (nxt, cums[-1], *ahead)
                    return fn
                return lax.cond(ranked == 1, variant(True), variant(False))
            return body

        zero = jnp.zeros((1, tq), F32)
        first = [cand_count(r) for r in range(4)]
        quads = lax.fori_loop(0, nquad, tie_group_body(4), (needs_ranking(zero, first), zero, *first))
        cum, ca, cb = quads[1], quads[2], quads[3]
        lax.fori_loop(2 * nquad, npair, tie_group_body(2), (needs_ranking(cum, [ca, cb]), cum, ca, cb))

    @pl.when(need_ties)
    def _():
        run(True)

    @pl.when(jnp.logical_not(need_ties))
    def _():
        run(False)

    hd = ATTN_HEAD_DIM
    normed = [acc_sc[h, 0:hd, :] / acc_sc[h, hd:hd + 1, :] for h in range(ATTN_HEADS)]
    for p in range(ATTN_HEADS // 2):
        pair_t = jnp.concatenate([normed[2 * p], normed[2 * p + 1]], axis=0)
        o_ref[:, LANES * p:LANES * (p + 1)] = jnp.transpose(pair_t)


def _dsa_call(rel_bias, qi, wi, q, ki2, k, vt, *, tq):
    s = q.shape[0]
    assert tq > REL_MAX_DIST and s % tq == 0
    nb = s // tq
    topk = min(TOPK_MAX, s // 4)
    kern = functools.partial(_dsa_kernel, tq=tq, topk=topk,
                             idx_scale=(IDX_DIM ** -0.5) * (IDX_HEADS ** -0.5))
    whole = lambda width: pl.BlockSpec((s, width), lambda i: (0, 0), pipeline_mode=pl.Buffered(1))
    assert vt.shape == (nb, ATTN_HEADS * _VT_ROWS, tq)
    return pl.pallas_call(
        kern,
        grid=(nb,),
        in_specs=[pl.BlockSpec(memory_space=pltpu.SMEM),
                  pl.BlockSpec((tq, IDX_HEADS * IDX_DIM), lambda i: (i, 0)),
                  pl.BlockSpec((tq, LANES), lambda i: (i, 0)),
                  pl.BlockSpec((tq, ATTN_WIDTH), lambda i: (i, 0)),
                  whole(2 * IDX_DIM), whole(ATTN_WIDTH),
                  pl.BlockSpec(vt.shape, lambda i: (0, 0, 0), pipeline_mode=pl.Buffered(1))],
        out_specs=pl.BlockSpec((tq, ATTN_WIDTH), lambda i: (i, 0)),
        out_shape=jax.ShapeDtypeStruct((s, ATTN_WIDTH), F32),
        scratch_shapes=[pltpu.VMEM((nb + nb % 2, tq, tq), F32),
                        pltpu.VMEM((nb + nb % 2, tq, tq), BF16),
                        pltpu.VMEM((ATTN_HEADS, _N_KINDS, tq, tq), BF16),
                        pltpu.VMEM((3, tq, tq), F32),
                        pltpu.VMEM((2, ATTN_HEADS, tq, tq), BF16),
                        pltpu.VMEM((IDX_HEADS, LANES, tq), BF16),
                        pltpu.VMEM((ATTN_HEADS, LANES, tq), BF16),
                        pltpu.VMEM((ATTN_HEADS, 1, tq), F32),
                        pltpu.VMEM((ATTN_HEADS, _VT_ROWS, tq), F32)],
        compiler_params=_cparams("arbitrary"),
        name="dsa_attention",
    )(rel_bias, qi, wi, q, ki2, k, vt)


_HGRN_CHUNK = 128
_HGRN_BASE = SUBLANES


def _hgrn_kernel(h4_ref, lb_ref, ng_ref, o_ref, st_sc, *, layer, rows):
    c_len = _HGRN_CHUNK

    @pl.when(pl.program_id(0) == 0)
    def _():
        st_sc[...] = jnp.zeros_like(st_sc)

    lb_all = lb_ref[...]
    e = jnp.exp(lb_all - jnp.max(lb_all, axis=0, keepdims=True))
    sm = e / jnp.sum(e, axis=0, keepdims=True)
    lbs = jnp.sum(sm[0:layer + 1, :], axis=0, keepdims=True) - sm[0:1, :]

    r1 = lax.broadcasted_iota(I32, (c_len, 1), 0)
    rr = lax.broadcasted_iota(I32, (c_len, c_len), 0)
    cc = lax.broadcasted_iota(I32, (c_len, c_len), 1)

    def block_ref(a, blk, r):
        x = a.reshape(c_len // blk, blk, a.shape[-1])[:, r:r + 1, :]
        return jnp.broadcast_to(x, (c_len // blk, blk, a.shape[-1])).reshape(c_len, a.shape[-1])

    w = HGRN_WIDTH
    for c in range(rows // c_len):
        rs = slice(c * c_len, (c + 1) * c_len)
        hq = h4_ref[rs, 0:w]
        z = h4_ref[rs, w:2 * w]
        val = h4_ref[rs, 2 * w:3 * w]
        hg = h4_ref[rs, 3 * w:4 * w]
        key = (1.0 - lbs) * jax.nn.sigmoid(-z)
        g = jnp.log1p(-jnp.minimum(key, KEY_MAX))
        a_cum = g
        sh = 1
        while sh < c_len:
            a_cum = a_cum + jnp.where(r1 >= sh, pltpu.roll(a_cum, sh, 0), 0.0)
            sh *= 2
        qf = _silu(hq)
        outs = []
        for h in range(HGRN_HEADS):
            hs = slice(h * HGRN_DIM, (h + 1) * HGRN_DIM)
            ah, qh, kh, vh = a_cum[:, hs], qf[:, hs], key[:, hs], val[:, hs]
            vh16 = vh.astype(BF16)
            st = st_sc[h]
            o = _dot_nt((qh * jnp.exp(ah)).astype(BF16), st.astype(BF16))
            ref = block_ref(ah, _HGRN_BASE, _HGRN_BASE // 2 - 1)
            sc = _dot_nt((qh * jnp.exp(ah - ref)).astype(BF16), (kh * jnp.exp(ref - ah)).astype(BF16))
            same = (rr >> _log2(_HGRN_BASE)) == (cc >> _log2(_HGRN_BASE))
            scores = jnp.where(jnp.logical_and(same, cc <= rr), sc, 0.0)
            half = _HGRN_BASE
            while half < c_len:
                blk = 2 * half
                ref = block_ref(ah, blk, half - 1)
                second = (r1 & (blk - 1)) >= half
                ql = jnp.where(second, qh * jnp.exp(jnp.minimum(ah - ref, 0.0)), 0.0)
                kl = jnp.where(second, 0.0, kh * jnp.exp(jnp.minimum(ref - ah, 0.0)))
                sc = _dot_nt(ql.astype(BF16), kl.astype(BF16))
                scores = scores + jnp.where((rr >> _log2(blk)) == (cc >> _log2(blk)), sc, 0.0)
                half = blk
            o = o + _dot(scores.astype(BF16), vh16)
            a_last = ah[c_len - 1:c_len, :]
            kd = (kh * jnp.exp(a_last - ah)).astype(BF16)
            st_sc[h] = st * jnp.exp(a_last) + _dot_tn(vh16, kd)
            o = o * lax.rsqrt(jnp.mean(o * o, axis=-1, keepdims=True) + RMS_EPS)
            outs.append(o)
        o_ref[rs, :] = jnp.concatenate(outs, axis=1) * ng_ref[...] * _silu(hg)


def _hgrn_call(h4, lb, ng, *, layer, rows):
    s = h4.shape[0]
    kern = functools.partial(_hgrn_kernel, layer=layer, rows=rows)
    return pl.pallas_call(
        kern,
        grid=(s // rows,),
        in_specs=[pl.BlockSpec((rows, 4 * HGRN_WIDTH), lambda i: (i, 0)),
                  pl.BlockSpec(lb.shape, lambda i: (0, 0)),
                  _slab(ng, layer)],
        out_specs=pl.BlockSpec((rows, HGRN_WIDTH), lambda i: (i, 0)),
        out_shape=jax.ShapeDtypeStruct((s, HGRN_WIDTH), F32),
        scratch_shapes=[pltpu.VMEM((HGRN_HEADS, HGRN_DIM, HGRN_DIM), F32)],
        compiler_params=_cparams("arbitrary"),
        name="hgrn2",
    )(h4, lb, ng)


_POOL_HALO = 16


def _merge_kernel(x_ref, mod_ref, a_ref, ah_ref, yb_ref, yc_ref, g_ref, pw_ref, ps_ref, wb_ref, wo_ref,
                  lng_ref, lnb_ref, fwi_ref, fwo_ref, o_ref, *, tm, tf, alpha):
    i = pl.program_id(0)
    d = x_ref.shape[1]
    _, _, gate = _mod_rows(mod_ref, 1)
    a = a_ref[...]
    halo = jnp.where(i == 0, 0.0, ah_ref[...])
    ext = jnp.concatenate([halo, a], axis=0)
    lane = lax.broadcasted_iota(I32, (tm, POOL_WIDTH), 1)
    grp = lane >> _log2(POOL_GROUP)
    pos1 = (i * tm + lax.broadcasted_iota(I32, (tm, POOL_WIDTH), 0) + 1).astype(F32)
    win = jnp.zeros((tm, POOL_WIDTH), F32)
    cnt = jnp.zeros((tm, POOL_WIDTH), F32)
    ssum, have = ext, 1
    for gi, wd in enumerate(POOL_WINDOWS):
        while have < wd:
            ssum = ssum + pltpu.roll(ssum, have, 0)
            have *= 2
        assert have == wd
        win = jnp.where(grp == gi, ssum[_POOL_HALO:, :], win)
        cnt = jnp.where(grp == gi, jnp.minimum(pos1, float(wd)), cnt)
    dlt = (win / cnt - a).astype(BF16)
    ya = _dot(dlt, pw_ref[...]) * ps_ref[...]
    za = _dot(ya.astype(BF16), wb_ref[0:POOL_WIDTH, :])
    zb = _dot(yb_ref[...].astype(BF16), wb_ref[POOL_WIDTH:POOL_WIDTH + ATTN_WIDTH, :])
    zc = _dot(yc_ref[...].astype(BF16), wb_ref[POOL_WIDTH + ATTN_WIDTH:, :])
    g0, g1, g2 = (jax.nn.sigmoid(g_ref[:, j * d:(j + 1) * d].astype(F32)) for j in range(3))
    merged = g0 * za + g1 * zb + g2 * zc
    y = _dot(merged.astype(BF16), wo_ref[...])
    z = alpha * x_ref[...] + (1.0 + gate) * y
    xm = _layer_norm(z, lng_ref[1:2, :], lnb_ref[1:2, :])
    o_ref[...] = _ffn_rows(xm, mod_ref, fwi_ref, fwo_ref, lng_ref, lnb_ref, sub=2, tf=tf, alpha=alpha)


def _merge_call(x, mod, a, yb, yc, gates, pw_bd, ps, wb, wo, ln_g, ln_b, ffn_wi, ffn_wo, *, layer, tm, tf, alpha):
    s, d = x.shape
    kern = functools.partial(_merge_kernel, tm=tm, tf=tf, alpha=alpha)
    full = lambda arr: _slab(arr, layer)
    hb = tm // _POOL_HALO
    return pl.pallas_call(
        kern,
        grid=(s // tm,),
        in_specs=[pl.BlockSpec((tm, d), lambda i: (i, 0)),
                  pl.BlockSpec((3 * N_SUB, d), lambda i: (0, 0)),
                  pl.BlockSpec((tm, POOL_WIDTH), lambda i: (i, 0)),
                  pl.BlockSpec((_POOL_HALO, POOL_WIDTH), lambda i: (jnp.maximum(i * hb - 1, 0), 0)),
                  pl.BlockSpec((tm, ATTN_WIDTH), lambda i: (i, 0)),
                  pl.BlockSpec((tm, HGRN_WIDTH), lambda i: (i, 0)),
                  pl.BlockSpec((tm, 3 * d), lambda i: (i, 0)),
                  full(pw_bd), full(ps), full(wb), full(wo), full(ln_g), full(ln_b),
                  _slab(ffn_wi, layer, 1), _slab(ffn_wo, layer, 1)],
        out_specs=pl.BlockSpec((tm, d), lambda i: (i, 0)),
        out_shape=jax.ShapeDtypeStruct((s, d), F32),
        compiler_params=_cparams("arbitrary"),
        name="merge_ffn",
    )(x, mod, a, a, yb, yc, gates, pw_bd, ps, wb, wo, ln_g, ln_b, ffn_wi, ffn_wo)


def _pool_blockdiag(pool_w):
    depth, n, g, _ = pool_w.shape
    eye = jnp.eye(n, dtype=pool_w.dtype)
    return jnp.einsum("lgij,gh->lgihj", pool_w, eye).reshape(depth, n * g, n * g)


def kernel(x, c, w_ada, b_ada, ln_g, ln_b, ffn_w_in, ffn_w_out, mix_w_in, pool_w, pool_scale, rel_bias,
           hgrn_lb, hgrn_norm_g, w_branch, w_out):
    batch, s, d = x.shape
    assert batch == 1, "kernels are written for a single sequence"
    depth = w_ada.shape[0]
    alpha = (2 * depth) ** 0.25
    tm_ffn = min(512, s)
    tf = 256
    tq = min(256, s)
    tm_proj = tq
    tm_merge = min(512, s)
    hgrn_rows = min(1024, s)

    ffn_wi16, ffn_wo16 = ffn_w_in.astype(BF16), ffn_w_out.astype(BF16)
    mix_w16 = _proj_weight(mix_w_in)
    pool_bd16 = _pool_blockdiag(pool_w).astype(BF16)
    wb16, wo16 = w_branch.astype(BF16), w_out.astype(BF16)
    pool_scale3 = pool_scale.reshape(depth, 1, -1)
    norm_g3 = hgrn_norm_g.reshape(depth, 1, -1)

    mods = _ada_call(c, w_ada, b_ada)
    xs = x.reshape(s, d)
    for l in range(depth):
        mod = mods[l].reshape(3 * N_SUB, d)
        xs = _ffn_call(xs, mod, ffn_wi16, ffn_wo16, ln_g, ln_b, layer=l, which=0, sub=0, alpha=alpha,
                       tm=tm_ffn, tf=tf)
        gates, a, h4, qi, q, k, vt, ki2, wi = _proj_call(xs, mod, mix_w16, layer=l, tm=tm_proj)
        yb = _dsa_call(rel_bias, qi, wi, q, ki2, k, vt, tq=tq)
        yc = _hgrn_call(h4, hgrn_lb, norm_g3, layer=l, rows=hgrn_rows)
        xs = _merge_call(xs, mod, a, yb, yc, gates, pool_bd16, pool_scale3, wb16, wo16, ln_g, ln_b,
                         ffn_wi16, ffn_wo16, layer=l, tm=tm_merge, tf=tf, alpha=alpha)
    return xs.reshape(batch, s, d)
```

```python
import functools
import math

import jax
import jax.numpy as jnp
from jax import lax
from jax.experimental import pallas as pl
from jax.experimental.pallas import tpu as pltpu

F32 = jnp.float32
BF16 = jnp.bfloat16
I32 = jnp.int32

POOL_WINDOWS = (2, 4, 8, 16)
POOL_GROUP = 64
POOL_WIDTH = POOL_GROUP * len(POOL_WINDOWS)
ATTN_HEADS = 4
ATTN_HEAD_DIM = 64
ATTN_WIDTH = ATTN_HEADS * ATTN_HEAD_DIM
IDX_HEADS = 8
IDX_DIM = 64
TOPK_MAX = 256
HGRN_HEADS = 4
HGRN_DIM = 128
HGRN_WIDTH = HGRN_HEADS * HGRN_DIM
KEY_MAX = 1.0 - 1e-6
REL_BUCKETS = 32
REL_MAX_DIST = 128
N_SUB = 3
LN_EPS = 1e-5
RMS_EPS = 1e-6

LANES = 128
SUBLANES = 8
VMEM_LIMIT_BYTES = 58 * 1024 * 1024

NEG_INF = float("-inf")
POS_INF = float("inf")


def _cparams(*sem):
    return pltpu.CompilerParams(dimension_semantics=sem, vmem_limit_bytes=VMEM_LIMIT_BYTES)


def _dot(a, b):
    return jnp.dot(a, b, preferred_element_type=F32)


def _dot_nt(a, b):
    return lax.dot_general(a, b, (((1,), (1,)), ((), ())), preferred_element_type=F32)


def _dot_tn(a, b):
    return lax.dot_general(a, b, (((0,), (0,)), ((), ())), preferred_element_type=F32)


def _log2(n):
    assert n & (n - 1) == 0
    return n.bit_length() - 1


def _silu(x):
    return x * jax.nn.sigmoid(x)


def _mod_rows(mod_ref, j):
    return mod_ref[3 * j:3 * j + 1, :], mod_ref[3 * j + 1:3 * j + 2, :], mod_ref[3 * j + 2:3 * j + 3, :]


def _layer_norm(z, g, b):
    mu = jnp.mean(z, axis=-1, keepdims=True)
    zc = z - mu
    var = jnp.mean(zc * zc, axis=-1, keepdims=True)
    return zc * lax.rsqrt(var + LN_EPS) * g + b


def _ada_kernel(c_ref, w_ref, b_ref, o_ref):
    @pl.when(pl.program_id(1) == 0)
    def _():
        o_ref[0] = b_ref[0]

    c = c_ref[...]
    o_ref[0] += jnp.sum(_silu(c) * w_ref[0], axis=0, keepdims=True)


def _ada_call(c, w_ada, b_ada):
    depth, d, n = w_ada.shape
    rows = LANES
    return pl.pallas_call(
        _ada_kernel,
        grid=(depth, d // rows),
        in_specs=[pl.BlockSpec((rows, 1), lambda l, k: (k, 0)),
                  pl.BlockSpec((1, rows, n), lambda l, k: (l, k, 0)),
                  pl.BlockSpec((1, 1, n), lambda l, k: (l, 0, 0))],
        out_specs=pl.BlockSpec((1, 1, n), lambda l, k: (l, 0, 0)),
        out_shape=jax.ShapeDtypeStruct((depth, 1, n), F32),
        compiler_params=_cparams("arbitrary", "arbitrary"),
        name="ada_mod",
    )(c.reshape(d, 1), w_ada, b_ada.reshape(depth, 1, n))


def _ffn_rows(x, mod_ref, wi_ref, wo_ref, lng_ref, lnb_ref, *, sub, tf, alpha):
    shift, scale, gate = _mod_rows(mod_ref, sub)
    ff = wo_ref.shape[0]
    h = (x * (1.0 + scale) + shift).astype(BF16)
    y = jnp.zeros(x.shape, F32)
    for c0 in range(0, ff, tf):
        g = _dot(h, wi_ref[:, c0:c0 + tf])
        u = _dot(h, wi_ref[:, ff + c0:ff + c0 + tf])
        y = y + _dot((_silu(g) * u).astype(BF16), wo_ref[c0:c0 + tf, :])
    z = alpha * x + (0.5 * (1.0 + gate)) * y
    return _layer_norm(z, lng_ref[sub:sub + 1, :], lnb_ref[sub:sub + 1, :])


def _ffn_kernel(x_ref, mod_ref, wi_ref, wo_ref, lng_ref, lnb_ref, o_ref, *, sub, tf, alpha):
    o_ref[...] = _ffn_rows(x_ref[...], mod_ref, wi_ref, wo_ref, lng_ref, lnb_ref, sub=sub, tf=tf, alpha=alpha)


def _ffn_call(x, mod, w_in, w_out, ln_g, ln_b, *, layer, which, sub, alpha, tm, tf):
    s, d = x.shape
    assert w_out.shape[2] % tf == 0
    kern = functools.partial(_ffn_kernel, sub=sub, tf=tf, alpha=alpha)
    return pl.pallas_call(
        kern,
        grid=(s // tm,),
        in_specs=[pl.BlockSpec((tm, d), lambda i: (i, 0)),
                  pl.BlockSpec((3 * N_SUB, d), lambda i: (0, 0)),
                  _slab(w_in, layer, which), _slab(w_out, layer, which),
                  _slab(ln_g, layer), _slab(ln_b, layer)],
        out_specs=pl.BlockSpec((tm, d), lambda i: (i, 0)),
        out_shape=jax.ShapeDtypeStruct((s, d), F32),
        compiler_params=_cparams("arbitrary"),
        name="ffn",
    )(x, mod, w_in, w_out, ln_g, ln_b)


_PROJ_OUTS = (("gates", 3 * 1024, BF16), ("a", POOL_WIDTH, F32), ("h4", 4 * HGRN_WIDTH, F32),
              ("qi", IDX_HEADS * IDX_DIM, BF16), ("q", ATTN_WIDTH, BF16), ("k", ATTN_WIDTH, BF16),
              ("v", ATTN_WIDTH, BF16), ("ki2", 2 * IDX_DIM, BF16), ("wi", LANES, F32))
_PROJ_CHUNK = 1024


def _proj_kernel(x_ref, mod_ref, w_ref, *o_refs):
    shift, scale, _ = _mod_rows(mod_ref, 1)
    h = (x_ref[...] * (1.0 + scale) + shift).astype(BF16)
    c0 = 0
    for (name, width, dt), o_ref in zip(_PROJ_OUTS, o_refs):
        if name == "v":
            vt = lax.dot_general(w_ref[:, c0:c0 + width], h, (((0,), (1,)), ((), ())),
                                 preferred_element_type=F32)
            ones = jnp.ones((_VT_ROWS - ATTN_HEAD_DIM, vt.shape[1]), F32)
            rows = []
            for p0 in range(0, width, ATTN_HEAD_DIM):
                rows += [vt[p0:p0 + ATTN_HEAD_DIM, :], ones]
            o_ref[0] = jnp.concatenate(rows, axis=0).astype(dt)
        else:
            for cc in range(0, width, _PROJ_CHUNK):
                cw = min(_PROJ_CHUNK, width - cc)
                o_ref[:, cc:cc + cw] = _dot(h, w_ref[:, c0 + cc:c0 + cc + cw]).astype(dt)
        c0 += width


def _proj_weight_kernel(w_ref, o_ref):
    out_width = {name: wd for name, wd, _ in _PROJ_OUTS}
    src = {}
    c0 = 0
    for name, wd in (("a", POOL_WIDTH), ("q", ATTN_WIDTH), ("k", ATTN_WIDTH), ("v", ATTN_WIDTH),
                     ("qi", IDX_HEADS * IDX_DIM), ("ki", IDX_DIM), ("wi", IDX_HEADS),
                     ("h4", 4 * HGRN_WIDTH), ("gates", out_width["gates"])):
        src[name] = (c0, wd)
        c0 += wd
    assert c0 == w_ref.shape[-1]
    q_scale = ATTN_HEAD_DIM ** -0.5 * _LOG2E
    dst = 0
    for name, width, _ in _PROJ_OUTS:
        if name == "ki2":
            pieces = [("ki", 1.0), ("ki", 1.0)]
        elif name == "wi":
            o_ref[:, dst:dst + width] = jnp.zeros((o_ref.shape[0], width), BF16)
            pieces = [("wi", 1.0)]
        else:
            pieces = [(name, q_scale if name == "q" else 1.0)]
        off = dst
        for pname, scale in pieces:
            s0, wd = src[pname]
            o_ref[:, off:off + wd] = (w_ref[:, s0:s0 + wd] * scale).astype(BF16)
            off += wd
        dst += width


def _proj_weight(w):
    depth, d, n_in = w.shape
    n_out = sum(wd for _, wd, _ in _PROJ_OUTS)
    rows = 256
    return pl.pallas_call(
        _proj_weight_kernel,
        grid=(depth, d // rows),
        in_specs=[pl.BlockSpec((None, rows, n_in), lambda l, r: (l, r, 0))],
        out_specs=pl.BlockSpec((None, rows, n_out), lambda l, r: (l, r, 0)),
        out_shape=jax.ShapeDtypeStruct((depth, d, n_out), BF16),
        compiler_params=_cparams("arbitrary", "arbitrary"),
        name="proj_weight",
    )(w)


def _slab(arr, *lead):
    block = (None,) * len(lead) + tuple(arr.shape[len(lead):])
    index = tuple(lead) + (0,) * (arr.ndim - len(lead))
    return pl.BlockSpec(block, lambda i: index, pipeline_mode=pl.Buffered(1))


def _proj_call(x, mod, w_all, *, layer, tm):
    s, d = x.shape
    n = w_all.shape[-1]
    assert n == sum(wd for _, wd, _ in _PROJ_OUTS)
    n_vt = ATTN_HEADS * _VT_ROWS
    out_specs = [pl.BlockSpec((1, n_vt, tm), lambda i: (i, 0, 0)) if name == "v"
                 else pl.BlockSpec((tm, wd), lambda i: (i, 0)) for name, wd, _ in _PROJ_OUTS]
    out_shape = [jax.ShapeDtypeStruct((s // tm, n_vt, tm) if name == "v" else (s, wd), dt)
                 for name, wd, dt in _PROJ_OUTS]
    return pl.pallas_call(
        _proj_kernel,
        grid=(s // tm,),
        in_specs=[pl.BlockSpec((tm, d), lambda i: (i, 0)),
                  pl.BlockSpec((3 * N_SUB, d), lambda i: (0, 0)),
                  _slab(w_all, layer)],
        out_specs=out_specs,
        out_shape=out_shape,
        compiler_params=_cparams("arbitrary"),
        name="mix_proj",
    )(x, mod, w_all)


_BISECT_MAX_ITERS = 320
_F32_TINY = float(jnp.finfo(jnp.float32).tiny)
_COARSE_PASSES = 9
_COARSE_MIN_ABS, _COARSE_MAX_ABS = 1e-30, 1e30
_KIND_FAR, _KIND_PAD, _N_KINDS = 2, 3, 4
_VT_ROWS = ATTN_HEAD_DIM + 16
_SMASK_FULL, _SMASK_DIAG, _SMASK_PAD = 0, 1, 2
_LOG2E = math.log2(math.e)


def _t5_bucket(dist):
    max_exact = REL_BUCKETS // 2
    d32 = jnp.maximum(dist, 1).astype(F32)
    large = max_exact + (jnp.log(d32 / max_exact) / math.log(REL_MAX_DIST / max_exact)
                         * (REL_BUCKETS - max_exact)).astype(I32)
    large = jnp.minimum(large, REL_BUCKETS - 1)
    return jnp.where(dist < max_exact, dist, large)


def _dsa_kernel(relb_ref, qi_ref, wi_ref, q_ref, ki2_ref, k_ref, vt_ref, o_ref,
                s_sc, sb_sc, bias_sc, smask_sc, lg_sc, qim_sc, qm_sc, m_sc, acc_sc,
                *, tq, topk, idx_scale):
    tk = tq
    i = pl.program_id(0)
    krow = lax.broadcasted_iota(I32, (tk, tq), 0)
    qcol = lax.broadcasted_iota(I32, (tk, tq), 1)
    causal = krow <= qcol
    lane = lax.broadcasted_iota(I32, (tq, LANES), 1)
    lo_half = lane < ATTN_HEAD_DIM

    @pl.when(i == 0)
    def _():
        for which in range(2):
            bucket = _t5_bucket(jnp.maximum(qcol - krow + which * tq, 0))
            for h in range(ATTN_HEADS):
                b = jnp.zeros((tk, tq), F32)
                for bk in range(REL_BUCKETS):
                    b = jnp.where(bucket == bk, relb_ref[bk, h] * _LOG2E, b)
                bias_sc[h, which] = (jnp.where(causal, b, NEG_INF) if which == 0 else b).astype(BF16)
        for h in range(ATTN_HEADS):
            bias_sc[h, _KIND_FAR] = jnp.full((tk, tq), relb_ref[REL_BUCKETS - 1, h] * _LOG2E, F32).astype(BF16)
            bias_sc[h, _KIND_PAD] = jnp.full((tk, tq), NEG_INF, BF16)
        smask_sc[_SMASK_FULL] = jnp.zeros((tk, tq), F32)
        smask_sc[_SMASK_DIAG] = jnp.where(causal, 0.0, NEG_INF)
        smask_sc[_SMASK_PAD] = jnp.full((tk, tq), NEG_INF, F32)

    def head_operands(src_ref, dst_sc, n_heads):
        src = src_ref[...].astype(F32)
        for h in range(n_heads):
            pair = src[:, LANES * (h // 2):LANES * (h // 2 + 1)]
            pair = jnp.where(lo_half if h % 2 == 0 else ~lo_half, pair, 0.0)
            dst_sc[h] = jnp.transpose(pair).astype(BF16)

    head_operands(qi_ref, qim_sc, IDX_HEADS)
    head_operands(q_ref, qm_sc, ATTN_HEADS)

    wt = jnp.transpose(wi_ref[...])

    def key_rows(jb):
        return pl.ds(pl.multiple_of(jb * tk, tk), tk)

    def fold(x):
        return x.reshape(tk // SUBLANES, SUBLANES, tq)

    def index_scores(jb):
        kb = ki2_ref[key_rows(jb), :]
        acc = jnp.zeros((tk, tq), F32)
        for h in range(IDX_HEADS):
            acc = acc + wt[h:h + 1, :] * jnp.maximum(_dot(kb, qim_sc[h]), 0.0)
        return acc * idx_scale

    npair = (i + 2) // 2
    last_kv = k_ref.shape[0] // tk - 1

    def p1_blocks(jbs, carry):
        rmin, rmax = carry
        for jb in jbs:
            raw = index_scores(jnp.minimum(jb, last_kv))
            kind = jnp.where(jb > i, _SMASK_PAD, jnp.where(jb == i, _SMASK_DIAG, _SMASK_FULL))
            sc = raw + smask_sc[kind]
            s_sc[jb] = sc
            sb_sc[jb] = sc.astype(BF16)
            rmin = jnp.minimum(rmin, jnp.min(fold(raw), axis=0))
            rmax = jnp.maximum(rmax, jnp.max(fold(raw), axis=0))
        return rmin, rmax

    nquad = npair // 2

    def walk_pairs(pairs_fn, init):
        n4 = npair // 4
        carry = lax.fori_loop(0, n4, lambda t, c: pairs_fn([4 * t + r for r in range(4)], c), init)
        n2 = (npair - 4 * n4) // 2
        carry = lax.fori_loop(2 * n4, 2 * n4 + n2, lambda t, c: pairs_fn([2 * t, 2 * t + 1], c), carry)
        return lax.fori_loop(4 * n4 + 2 * n2, npair, lambda t, c: pairs_fn([t], c), carry)

    rmin, rmax = walk_pairs(lambda js, c: p1_blocks([2 * j + r for j in js for r in range(2)], c),
                            (jnp.full((SUBLANES, tq), POS_INF, F32), jnp.full((SUBLANES, tq), NEG_INF, F32)))
    rowmin = jnp.min(rmin, axis=0, keepdims=True)
    rowmax = jnp.max(rmax, axis=0, keepdims=True)

    kf = float(topk)
    nfin = (i * tq + lax.broadcasted_iota(I32, (1, tq), 1) + 1).astype(F32)
    small = nfin <= kf

    def walk_blocks(blocks_fn, init):
        return lax.fori_loop(0, npair, lambda j, c: blocks_fn([2 * j, 2 * j + 1], c), init)

    def count_ge(x):
        def blocks_fn(jbs, cnt):
            for jb in jbs:
                cnt = cnt + jnp.sum(fold(jnp.where(s_sc[jb] >= x, 1.0, 0.0)), axis=0)
            return cnt
        return jnp.sum(walk_blocks(blocks_fn, jnp.zeros((SUBLANES, tq), F32)), axis=0, keepdims=True)

    def count_ge16(xb):
        one, zero = jnp.ones((), BF16), jnp.zeros((), BF16)
        rows16 = 2 * SUBLANES

        def blocks_fn(jbs, cnt):
            parts = []
            for jb in jbs:
                c = jnp.where(sb_sc[jb] >= xb, one, zero)
                parts += [c[r0:r0 + rows16, :] for r0 in range(0, tk, rows16)]
            while len(parts) > 1:
                parts = [a + b for a, b in zip(parts[0::2], parts[1::2])]
            return cnt + parts[0].astype(F32)
        return jnp.sum(walk_blocks(blocks_fn, jnp.zeros((rows16, tq), F32)), axis=0, keepdims=True)

    def coarse_probe(mid):
        xb = mid.astype(BF16)
        xbits = lax.bitcast_convert_type(xb.astype(F32), I32)
        neg = xbits < 0
        mu = jnp.where(neg, xbits + 0x8000, xbits - 0x8000)
        odd = (xbits & 0x10000) != 0
        tbits = jnp.where(odd, jnp.where(neg, mu - 1, mu + 1), mu)
        t = lax.bitcast_convert_type(tbits, F32)
        ok = jnp.logical_and(jnp.abs(xb.astype(F32)) > _COARSE_MIN_ABS, jnp.abs(xb.astype(F32)) < _COARSE_MAX_ABS)
        return xb, t, ok

    def narrow(st, probe, c, active):
        lo, hi, clo, chi, done = st
        up = jnp.logical_and(active, c >= kf)
        dn = jnp.logical_and(active, c < kf)
        return (jnp.where(up, probe, lo), jnp.where(dn, probe, hi), jnp.where(up, c, clo),
                jnp.where(dn, c, chi), jnp.where(jnp.logical_and(active, c == kf), 1.0, done))

    def coarse_body(_, st):
        lo, hi, _, _, done = st
        xb, t, ok = coarse_probe(0.5 * lo + 0.5 * hi)
        ok = jnp.logical_and(ok, jnp.logical_and(t > lo, t < hi))
        return narrow(st, t, count_ge16(xb), jnp.logical_and(done < 0.5, ok))

    def fine_cond(st):
        it, (_, _, _, _, done) = st
        return jnp.logical_and(it < _BISECT_MAX_ITERS, jnp.min(done) < 0.5)

    def fine_body(st):
        it, inner = st
        lo, hi, _, _, done = inner
        mid = 0.5 * lo + 0.5 * hi
        mid = jnp.where(jnp.logical_and(lo == 0.0, hi > _F32_TINY), _F32_TINY, mid)
        mid = jnp.where(jnp.logical_and(lo < 0.0, hi > 0.0), 0.0, mid)
        stuck = jnp.logical_or(jnp.logical_or(mid <= lo, mid >= hi),
                               jnp.logical_and(lo == 0.0, hi <= _F32_TINY))
        lo, hi, clo, chi, done = narrow(inner, mid, count_ge(mid),
                                        jnp.logical_and(done < 0.5, jnp.logical_not(stuck)))
        return it + 1, (lo, hi, clo, chi, jnp.where(stuck, 1.0, done))

    mbits = lax.bitcast_convert_type(rowmax, I32)
    above_max = jnp.where(rowmax == 0.0, _F32_TINY,
                          lax.bitcast_convert_type(jnp.where(rowmax > 0.0, mbits + 1, mbits - 1), F32))
    st0 = (jnp.where(small, NEG_INF, rowmin), above_max, nfin, jnp.zeros((1, tq), F32),
           jnp.where(small, 1.0, 0.0))
    st1 = lax.fori_loop(0, _COARSE_PASSES, coarse_body, st0)
    _, (lo, hi, clo, chi, _) = lax.while_loop(fine_cond, fine_body, (jnp.int32(0), st1))
    budget = kf - chi
    need_ties = jnp.max((clo - chi) - budget) > 0.5

    pairs = [slice(LANES * (h // 2), LANES * (h // 2 + 1)) for h in range(ATTN_HEADS)]

    def issue_qk(jb, slot):
        kb = k_ref[key_rows(jnp.minimum(jb, last_kv)), :]
        for h in range(ATTN_HEADS):
            lg_sc[slot, h] = _dot(kb[:, pairs[h]], qm_sc[h]).astype(BF16)

    def attend(jb, slot, thr, cum=None):
        s = s_sc[jb]
        if cum is None:
            sel = s >= thr
        else:
            cand = jnp.logical_and(s >= lo, s < hi)
            before = (qcol < krow).astype(BF16)
            rank = cum + _dot(before, jnp.where(cand, 1.0, 0.0).astype(BF16))
            sel = jnp.logical_or(s >= hi, jnp.logical_and(cand, rank < budget))
        kind = jnp.where(jb > i, _KIND_PAD, jnp.minimum(i - jb, _KIND_FAR))
        selm = jnp.where(sel, 0.0, NEG_INF).astype(BF16)
        probs, alphas = [], []
        for h in range(ATTN_HEADS):
            lg = lg_sc[slot, h] + bias_sc[h, kind] + selm
            m_old = m_sc[h]
            m_new = jnp.maximum(m_old, jnp.max(lg, axis=0, keepdims=True).astype(F32))
            m_safe = jnp.where(m_new == NEG_INF, 0.0, m_new)
            alphas.append(jnp.exp2(m_old - m_safe))
            probs.append(jnp.exp2(lg - m_safe.astype(BF16)))
            m_sc[h] = m_new
        vb = vt_ref[jnp.minimum(jb, last_kv)]
        for h in range(ATTN_HEADS):
            vrows = slice(_VT_ROWS * h, _VT_ROWS * (h + 1))
            acc_sc[h] = alphas[h] * acc_sc[h] + _dot(vb[vrows, :], probs[h])

    def run(with_ties):
        m_sc[...] = jnp.full(m_sc.shape, NEG_INF, F32)
        acc_sc[...] = jnp.zeros_like(acc_sc)
        issue_qk(0, 0)

        def pair(j, thr0, thr1, cum0=None, cum1=None):
            issue_qk(2 * j + 1, 1)
            attend(2 * j, 0, thr0, cum0)
            issue_qk(2 * j + 2, 0)
            attend(2 * j + 1, 1, thr1, cum1)

        if not with_ties:
            def pairs_fn(js, carry):
                for j in js:
                    pair(j, lo, lo)
                return carry
            walk_pairs(pairs_fn, 0)
            return

        last_s = s_sc.shape[0] - 1

        def cand_count(jb):
            s = s_sc[jnp.minimum(jb, last_s)]
            cand = jnp.logical_and(s >= lo, s < hi)
            return jnp.sum(jnp.where(cand, 1.0, 0.0), axis=0, keepdims=True)

        def needs_ranking(cum, counts):
            runs_out = None
            for cnt in counts:
                here = jnp.logical_and(cum + cnt > budget, cum < budget)
                runs_out = here if runs_out is None else jnp.logical_or(runs_out, here)
                cum = cum + cnt
            return (jnp.max(jnp.where(runs_out, 1.0, 0.0)) > 0.5).astype(I32)

        def tie_group_body(n_blocks):
            def body(g, carry):
                ranked, cums = carry[0], [carry[1]]
                for cnt in carry[2:]:
                    cums.append(cums[-1] + cnt)
                base = n_blocks * g

                def variant(with_rank):
                    def fn():
                        ahead = [cand_count(base + n_blocks + r) for r in range(n_blocks)]
                        nxt = needs_ranking(cums[-1], ahead)
                        for p in range(n_blocks // 2):
                            if with_rank:
                                pair(base // 2 + p, None, None, cums[2 * p], cums[2 * p + 1])
                            else:
                                pair(base // 2 + p, jnp.where(cums[2 * p + 1] <= budget, lo, hi),
                                     jnp.where(cums[2 * p + 2] <= budget, lo, hi))
                        return (nxt, cums[-1], *ahead)
                    return fn
                return lax.cond(ranked == 1, variant(True), variant(False))
            return body

        zero = jnp.zeros((1, tq), F32)
        first = [cand_count(r) for r in range(4)]
        quads = lax.fori_loop(0, nquad, tie_group_body(4), (needs_ranking(zero, first), zero, *first))
        cum, ca, cb = quads[1], quads[2], quads[3]
        lax.fori_loop(2 * nquad, npair, tie_group_body(2), (needs_ranking(cum, [ca, cb]), cum, ca, cb))

    @pl.when(need_ties)
    def _():
        run(True)

    @pl.when(jnp.logical_not(need_ties))
    def _():
        run(False)

    hd = ATTN_HEAD_DIM
    normed = [acc_sc[h, 0:hd, :] / acc_sc[h, hd:hd + 1, :] for h in range(ATTN_HEADS)]
    for p in range(ATTN_HEADS // 2):
        pair_t = jnp.concatenate([normed[2 * p], normed[2 * p + 1]], axis=0)
        o_ref[:, LANES * p:LANES * (p + 1)] = jnp.transpose(pair_t)


def _dsa_call(rel_bias, qi, wi, q, ki2, k, vt, *, tq):
    s = q.shape[0]
    assert tq > REL_MAX_DIST and s % tq == 0
    nb = s // tq
    topk = min(TOPK_MAX, s // 4)
    kern = functools.partial(_dsa_kernel, tq=tq, topk=topk,
                             idx_scale=(IDX_DIM ** -0.5) * (IDX_HEADS ** -0.5))
    whole = lambda width: pl.BlockSpec((s, width), lambda i: (0, 0), pipeline_mode=pl.Buffered(1))
    assert vt.shape == (nb, ATTN_HEADS * _VT_ROWS, tq)
    return pl.pallas_call(
        kern,
        grid=(nb,),
        in_specs=[pl.BlockSpec(memory_space=pltpu.SMEM),
                  pl.BlockSpec((tq, IDX_HEADS * IDX_DIM), lambda i: (i, 0)),
                  pl.BlockSpec((tq, LANES), lambda i: (i, 0)),
                  pl.BlockSpec((tq, ATTN_WIDTH), lambda i: (i, 0)),
                  whole(2 * IDX_DIM), whole(ATTN_WIDTH),
                  pl.BlockSpec(vt.shape, lambda i: (0, 0, 0), pipeline_mode=pl.Buffered(1))],
        out_specs=pl.BlockSpec((tq, ATTN_WIDTH), lambda i: (i, 0)),
        out_shape=jax.ShapeDtypeStruct((s, ATTN_WIDTH), F32),
        scratch_shapes=[pltpu.VMEM((nb + nb % 2, tq, tq), F32),
                        pltpu.VMEM((nb + nb % 2, tq, tq), BF16),
                        pltpu.VMEM((ATTN_HEADS, _N_KINDS, tq, tq), BF16),
                        pltpu.VMEM((3, tq, tq), F32),
                        pltpu.VMEM((2, ATTN_HEADS, tq, tq), BF16),
                        pltpu.VMEM((IDX_HEADS, LANES, tq), BF16),
                        pltpu.VMEM((ATTN_HEADS, LANES, tq), BF16),
                        pltpu.VMEM((ATTN_HEADS, 1, tq), F32),
                        pltpu.VMEM((ATTN_HEADS, _VT_ROWS, tq), F32)],
        compiler_params=_cparams("arbitrary"),
        name="dsa_attention",
    )(rel_bias, qi, wi, q, ki2, k, vt)


_HGRN_CHUNK = 128
_HGRN_BASE = SUBLANES


def _hgrn_kernel(h4_ref, lb_ref, ng_ref, o_ref, st_sc, *, layer, rows):
    c_len = _HGRN_CHUNK

    @pl.when(pl.program_id(0) == 0)
    def _():
        st_sc[...] = jnp.zeros_like(st_sc)

    lb_all = lb_ref[...]
    e = jnp.exp(lb_all - jnp.max(lb_all, axis=0, keepdims=True))
    sm = e / jnp.sum(e, axis=0, keepdims=True)
    lbs = jnp.sum(sm[0:layer + 1, :], axis=0, keepdims=True) - sm[0:1, :]

    r1 = lax.broadcasted_iota(I32, (c_len, 1), 0)
    rr = lax.broadcasted_iota(I32, (c_len, c_len), 0)
    cc = lax.broadcasted_iota(I32, (c_len, c_len), 1)

    def block_ref(a, blk, r):
        x = a.reshape(c_len // blk, blk, a.shape[-1])[:, r:r + 1, :]
        return jnp.broadcast_to(x, (c_len // blk, blk, a.shape[-1])).reshape(c_len, a.shape[-1])

    w = HGRN_WIDTH
    for c in range(rows // c_len):
        rs = slice(c * c_len, (c + 1) * c_len)
        hq = h4_ref[rs, 0:w]
        z = h4_ref[rs, w:2 * w]
        val = h4_ref[rs, 2 * w:3 * w]
        hg = h4_ref[rs, 3 * w:4 * w]
        key = (1.0 - lbs) * jax.nn.sigmoid(-z)
        g = jnp.log1p(-jnp.minimum(key, KEY_MAX))
        a_cum = g
        sh = 1
        while sh < c_len:
            a_cum = a_cum + jnp.where(r1 >= sh, pltpu.roll(a_cum, sh, 0), 0.0)
            sh *= 2
        qf = _silu(hq)
        outs = []
        for h in range(HGRN_HEADS):
            hs = slice(h * HGRN_DIM, (h + 1) * HGRN_DIM)
            ah, qh, kh, vh = a_cum[:, hs], qf[:, hs], key[:, hs], val[:, hs]
            vh16 = vh.astype(BF16)
            st = st_sc[h]
            o = _dot_nt((qh * jnp.exp(ah)).astype(BF16), st.astype(BF16))
            ref = block_ref(ah, _HGRN_BASE, _HGRN_BASE // 2 - 1)
            sc = _dot_nt((qh * jnp.exp(ah - ref)).astype(BF16), (kh * jnp.exp(ref - ah)).astype(BF16))
            same = (rr >> _log2(_HGRN_BASE)) == (cc >> _log2(_HGRN_BASE))
            scores = jnp.where(jnp.logical_and(same, cc <= rr), sc, 0.0)
            half = _HGRN_BASE
            while half < c_len:
                blk = 2 * half
                ref = block_ref(ah, blk, half - 1)
                second = (r1 & (blk - 1)) >= half
                ql = jnp.where(second, qh * jnp.exp(jnp.minimum(ah - ref, 0.0)), 0.0)
                kl = jnp.where(second, 0.0, kh * jnp.exp(jnp.minimum(ref - ah, 0.0)))
                sc = _dot_nt(ql.astype(BF16), kl.astype(BF16))
                scores = scores + jnp.where((rr >> _log2(blk)) == (cc >> _log2(blk)), sc, 0.0)
                half = blk
            o = o + _dot(scores.astype(BF16), vh16)
            a_last = ah[c_len - 1:c_len, :]
            kd = (kh * jnp.exp(a_last - ah)).astype(BF16)
            st_sc[h] = st * jnp.exp(a_last) + _dot_tn(vh16, kd)
            o = o * lax.rsqrt(jnp.mean(o * o, axis=-1, keepdims=True) + RMS_EPS)
            outs.append(o)
        o_ref[rs, :] = jnp.concatenate(outs, axis=1) * ng_ref[...] * _silu(hg)


def _hgrn_call(h4, lb, ng, *, layer, rows):
    s = h4.shape[0]
    kern = functools.partial(_hgrn_kernel, layer=layer, rows=rows)
    return pl.pallas_call(
        kern,
        grid=(s // rows,),
        in_specs=[pl.BlockSpec((rows, 4 * HGRN_WIDTH), lambda i: (i, 0)),
                  pl.BlockSpec(lb.shape, lambda i: (0, 0)),
                  _slab(ng, layer)],
        out_specs=pl.BlockSpec((rows, HGRN_WIDTH), lambda i: (i, 0)),
        out_shape=jax.ShapeDtypeStruct((s, HGRN_WIDTH), F32),
        scratch_shapes=[pltpu.VMEM((HGRN_HEADS, HGRN_DIM, HGRN_DIM), F32)],
        compiler_params=_cparams("arbitrary"),
        name="hgrn2",
    )(h4, lb, ng)


_POOL_HALO = 16


def _merge_kernel(x_ref, mod_ref, a_ref, ah_ref, yb_ref, yc_ref, g_ref, pw_ref, ps_ref, wb_ref, wo_ref,
                  lng_ref, lnb_ref, fwi_ref, fwo_ref, o_ref, *, tm, tf, alpha):
    i = pl.program_id(0)
    d = x_ref.shape[1]
    _, _, gate = _mod_rows(mod_ref, 1)
    a = a_ref[...]
    halo = jnp.where(i == 0, 0.0, ah_ref[...])
    ext = jnp.concatenate([halo, a], axis=0)
    lane = lax.broadcasted_iota(I32, (tm, POOL_WIDTH), 1)
    grp = lane >> _log2(POOL_GROUP)
    pos1 = (i * tm + lax.broadcasted_iota(I32, (tm, POOL_WIDTH), 0) + 1).astype(F32)
    win = jnp.zeros((tm, POOL_WIDTH), F32)
    cnt = jnp.zeros((tm, POOL_WIDTH), F32)
    ssum, have = ext, 1
    for gi, wd in enumerate(POOL_WINDOWS):
        while have < wd:
            ssum = ssum + pltpu.roll(ssum, have, 0)
            have *= 2
        assert have == wd
        win = jnp.where(grp == gi, ssum[_POOL_HALO:, :], win)
        cnt = jnp.where(grp == gi, jnp.minimum(pos1, float(wd)), cnt)
    dlt = (win / cnt - a).astype(BF16)
    ya = _dot(dlt, pw_ref[...]) * ps_ref[...]
    za = _dot(ya.astype(BF16), wb_ref[0:POOL_WIDTH, :])
    zb = _dot(yb_ref[...].astype(BF16), wb_ref[POOL_WIDTH:POOL_WIDTH + ATTN_WIDTH, :])
    zc = _dot(yc_ref[...].astype(BF16), wb_ref[POOL_WIDTH + ATTN_WIDTH:, :])
    g0, g1, g2 = (jax.nn.sigmoid(g_ref[:, j * d:(j + 1) * d].astype(F32)) for j in range(3))
    merged = g0 * za + g1 * zb + g2 * zc
    y = _dot(merged.astype(BF16), wo_ref[...])
    z = alpha * x_ref[...] + (1.0 + gate) * y
    xm = _layer_norm(z, lng_ref[1:2, :], lnb_ref[1:2, :])
    o_ref[...] = _ffn_rows(xm, mod_ref, fwi_ref, fwo_ref, lng_ref, lnb_ref, sub=2, tf=tf, alpha=alpha)


def _merge_call(x, mod, a, yb, yc, gates, pw_bd, ps, wb, wo, ln_g, ln_b, ffn_wi, ffn_wo, *, layer, tm, tf, alpha):
    s, d = x.shape
    kern = functools.partial(_merge_kernel, tm=tm, tf=tf, alpha=alpha)
    full = lambda arr: _slab(arr, layer)
    hb = tm // _POOL_HALO
    return pl.pallas_call(
        kern,
        grid=(s // tm,),
        in_specs=[pl.BlockSpec((tm, d), lambda i: (i, 0)),
                  pl.BlockSpec((3 * N_SUB, d), lambda i: (0, 0)),
                  pl.BlockSpec((tm, POOL_WIDTH), lambda i: (i, 0)),
                  pl.BlockSpec((_POOL_HALO, POOL_WIDTH), lambda i: (jnp.maximum(i * hb - 1, 0), 0)),
                  pl.BlockSpec((tm, ATTN_WIDTH), lambda i: (i, 0)),
                  pl.BlockSpec((tm, HGRN_WIDTH), lambda i: (i, 0)),
                  pl.BlockSpec((tm, 3 * d), lambda i: (i, 0)),
                  full(pw_bd), full(ps), full(wb), full(wo), full(ln_g), full(ln_b),
                  _slab(ffn_wi, layer, 1), _slab(ffn_wo, layer, 1)],
        out_specs=pl.BlockSpec((tm, d), lambda i: (i, 0)),
        out_shape=jax.ShapeDtypeStruct((s, d), F32),
        compiler_params=_cparams("arbitrary"),
        name="merge_ffn",
    )(x, mod, a, a, yb, yc, gates, pw_bd, ps, wb, wo, ln_g, ln_b, ffn_wi, ffn_wo)


def _pool_blockdiag(pool_w):
    depth, n, g, _ = pool_w.shape
    eye = jnp.eye(n, dtype=pool_w.dtype)
    return jnp.einsum("lgij,gh->lgihj", pool_w, eye).reshape(depth, n * g, n * g)


def kernel(x, c, w_ada, b_ada, ln_g, ln_b, ffn_w_in, ffn_w_out, mix_w_in, pool_w, pool_scale, rel_bias,
           hgrn_lb, hgrn_norm_g, w_branch, w_out):
    batch, s, d = x.shape
    assert batch == 1, "kernels are written for a single sequence"
    depth = w_ada.shape[0]
    alpha = (2 * depth) ** 0.25
    tm_ffn = min(512, s)
    tf = 256
    tq = min(256, s)
    tm_proj = tq
    tm_merge = min(512, s)
    hgrn_rows = min(1024, s)

    ffn_wi16, ffn_wo16 = ffn_w_in.astype(BF16), ffn_w_out.astype(BF16)
    mix_w16 = _proj_weight(mix_w_in)
    pool_bd16 = _pool_blockdiag(pool_w).astype(BF16)
    wb16, wo16 = w_branch.astype(BF16), w_out.astype(BF16)
    pool_scale3 = pool_scale.reshape(depth, 1, -1)
    norm_g3 = hgrn_norm_g.reshape(depth, 1, -1)

    mods = _ada_call(c, w_ada, b_ada)
    xs = x.reshape(s, d)
    for l in range(depth):
        mod = mods[l].reshape(3 * N_SUB, d)
        xs = _ffn_call(xs, mod, ffn_wi16, ffn_wo16, ln_g, ln_b, layer=l, which=0, sub=0, alpha=alpha,
                       tm=tm_ffn, tf=tf)
        gates, a, h4, qi, q, k, vt, ki2, wi = _proj_call(xs, mod, mix_w16, layer=l, tm=tm_proj)
        yb = _dsa_call(rel_bias, qi, wi, q, ki2, k, vt, tq=tq)
        yc = _hgrn_call(h4, hgrn_lb, norm_g3, layer=l, rows=hgrn_rows)
        xs = _merge_call(xs, mod, a, yb, yc, gates, pool_bd16, pool_scale3, wb16, wo16, ln_g, ln_b,
                         ffn_wi16, ffn_wo16, layer=l, tm=tm_merge, tf=tf, alpha=alpha)
    return xs.reshape(batch, s, d)
```

```python
import functools
import math

import jax
import jax.numpy as jnp
from jax import lax
from jax.experimental import pallas as pl
from jax.experimental.pallas import tpu as pltpu

F32 = jnp.float32
BF16 = jnp.bfloat16
I32 = jnp.int32

POOL_WINDOWS = (2, 4, 8, 16)
POOL_GROUP = 64
POOL_WIDTH = POOL_GROUP * len(POOL_WINDOWS)
ATTN_HEADS = 4
ATTN_HEAD_DIM = 64
ATTN_WIDTH = ATTN_HEADS * ATTN_HEAD_DIM
IDX_HEADS = 8
IDX_DIM = 64
TOPK_MAX = 256
HGRN_HEADS = 4
HGRN_DIM = 128
HGRN_WIDTH = HGRN_HEADS * HGRN_DIM
KEY_MAX = 1.0 - 1e-6
REL_BUCKETS = 32
REL_MAX_DIST = 128
N_SUB = 3
LN_EPS = 1e-5
RMS_EPS = 1e-6

LANES = 128
SUBLANES = 8
VMEM_LIMIT_BYTES = 58 * 1024 * 1024

NEG_INF = float("-inf")
POS_INF = float("inf")


def _cparams(*sem):
    return pltpu.CompilerParams(dimension_semantics=sem, vmem_limit_bytes=VMEM_LIMIT_BYTES)


def _dot(a, b):
    return jnp.dot(a, b, preferred_element_type=F32)


def _dot_nt(a, b):
    return lax.dot_general(a, b, (((1,), (1,)), ((), ())), preferred_element_type=F32)


def _dot_tn(a, b):
    return lax.dot_general(a, b, (((0,), (0,)), ((), ())), preferred_element_type=F32)


def _log2(n):
    assert n & (n - 1) == 0
    return n.bit_length() - 1


def _silu(x):
    return x * jax.nn.sigmoid(x)


def _mod_rows(mod_ref, j):
    return mod_ref[3 * j:3 * j + 1, :], mod_ref[3 * j + 1:3 * j + 2, :], mod_ref[3 * j + 2:3 * j + 3, :]


def _layer_norm(z, g, b):
    mu = jnp.mean(z, axis=-1, keepdims=True)
    zc = z - mu
    var = jnp.mean(zc * zc, axis=-1, keepdims=True)
    return zc * lax.rsqrt(var + LN_EPS) * g + b


def _ada_kernel(c_ref, w_ref, b_ref, o_ref):
    @pl.when(pl.program_id(1) == 0)
    def _():
        o_ref[0] = b_ref[0]

    c = c_ref[...]
    o_ref[0] += jnp.sum(_silu(c) * w_ref[0], axis=0, keepdims=True)


def _ada_call(c, w_ada, b_ada):
    depth, d, n = w_ada.shape
    rows = LANES
    return pl.pallas_call(
        _ada_kernel,
        grid=(depth, d // rows),
        in_specs=[pl.BlockSpec((rows, 1), lambda l, k: (k, 0)),
                  pl.BlockSpec((1, rows, n), lambda l, k: (l, k, 0)),
                  pl.BlockSpec((1, 1, n), lambda l, k: (l, 0, 0))],
        out_specs=pl.BlockSpec((1, 1, n), lambda l, k: (l, 0, 0)),
        out_shape=jax.ShapeDtypeStruct((depth, 1, n), F32),
        compiler_params=_cparams("arbitrary", "arbitrary"),
        name="ada_mod",
    )(c.reshape(d, 1), w_ada, b_ada.reshape(depth, 1, n))


def _ffn_rows(x, mod_ref, wi_ref, wo_ref, lng_ref, lnb_ref, *, sub, tf, alpha):
    shift, scale, gate = _mod_rows(mod_ref, sub)
    ff = wo_ref.shape[0]
    h = (x * (1.0 + scale) + shift).astype(BF16)
    y = jnp.zeros(x.shape, F32)
    for c0 in range(0, ff, tf):
        g = _dot(h, wi_ref[:, c0:c0 + tf])
        u = _dot(h, wi_ref[:, ff + c0:ff + c0 + tf])
        y = y + _dot((_silu(g) * u).astype(BF16), wo_ref[c0:c0 + tf, :])
    z = alpha * x + (0.5 * (1.0 + gate)) * y
    return _layer_norm(z, lng_ref[sub:sub + 1, :], lnb_ref[sub:sub + 1, :])


def _ffn_kernel(x_ref, mod_ref, wi_ref, wo_ref, lng_ref, lnb_ref, o_ref, *, sub, tf, alpha):
    o_ref[...] = _ffn_rows(x_ref[...], mod_ref, wi_ref, wo_ref, lng_ref, lnb_ref, sub=sub, tf=tf, alpha=alpha)


def _ffn_call(x, mod, w_in, w_out, ln_g, ln_b, *, layer, which, sub, alpha, tm, tf):
    s, d = x.shape
    assert w_out.shape[2] % tf == 0
    kern = functools.partial(_ffn_kernel, sub=sub, tf=tf, alpha=alpha)
    return pl.pallas_call(
        kern,
        grid=(s // tm,),
        in_specs=[pl.BlockSpec((tm, d), lambda i: (i, 0)),
                  pl.BlockSpec((3 * N_SUB, d), lambda i: (0, 0)),
                  _slab(w_in, layer, which), _slab(w_out, layer, which),
                  _slab(ln_g, layer), _slab(ln_b, layer)],
        out_specs=pl.BlockSpec((tm, d), lambda i: (i, 0)),
        out_shape=jax.ShapeDtypeStruct((s, d), F32),
        compiler_params=_cparams("arbitrary"),
        name="ffn",
    )(x, mod, w_in, w_out, ln_g, ln_b)


_PROJ_OUTS = (("gates", 3 * 1024, BF16), ("a", POOL_WIDTH, F32), ("h4", 4 * HGRN_WIDTH, F32),
              ("qi", IDX_HEADS * IDX_DIM, BF16), ("q", ATTN_WIDTH, BF16), ("k", ATTN_WIDTH, BF16),
              ("v", ATTN_WIDTH, BF16), ("ki2", 2 * IDX_DIM, BF16), ("wi", LANES, F32))
_PROJ_CHUNK = 1024


def _proj_kernel(x_ref, mod_ref, w_ref, *o_refs):
    shift, scale, _ = _mod_rows(mod_ref, 1)
    h = (x_ref[...] * (1.0 + scale) + shift).astype(BF16)
    c0 = 0
    for (name, width, dt), o_ref in zip(_PROJ_OUTS, o_refs):
        if name == "v":
            vt = lax.dot_general(w_ref[:, c0:c0 + width], h, (((0,), (1,)), ((), ())),
                                 preferred_element_type=F32)
            ones = jnp.ones((_VT_ROWS - ATTN_HEAD_DIM, vt.shape[1]), F32)
            rows = []
            for p0 in range(0, width, ATTN_HEAD_DIM):
                rows += [vt[p0:p0 + ATTN_HEAD_DIM, :], ones]
            o_ref[0] = jnp.concatenate(rows, axis=0).astype(dt)
        else:
            for cc in range(0, width, _PROJ_CHUNK):
                cw = min(_PROJ_CHUNK, width - cc)
                o_ref[:, cc:cc + cw] = _dot(h, w_ref[:, c0 + cc:c0 + cc + cw]).astype(dt)
        c0 += width


def _proj_weight_kernel(w_ref, o_ref):
    out_width = {name: wd for name, wd, _ in _PROJ_OUTS}
    src = {}
    c0 = 0
    for name, wd in (("a", POOL_WIDTH), ("q", ATTN_WIDTH), ("k", ATTN_WIDTH), ("v", ATTN_WIDTH),
                     ("qi", IDX_HEADS * IDX_DIM), ("ki", IDX_DIM), ("wi", IDX_HEADS),
                     ("h4", 4 * HGRN_WIDTH), ("gates", out_width["gates"])):
        src[name] = (c0, wd)
        c0 += wd
    assert c0 == w_ref.shape[-1]
    q_scale = ATTN_HEAD_DIM ** -0.5 * _LOG2E
    dst = 0
    for name, width, _ in _PROJ_OUTS:
        if name == "ki2":
            pieces = [("ki", 1.0), ("ki", 1.0)]
        elif name == "wi":
            o_ref[:, dst:dst + width] = jnp.zeros((o_ref.shape[0], width), BF16)
            pieces = [("wi", 1.0)]
        else:
            pieces = [(name, q_scale if name == "q" else 1.0)]
        off = dst
        for pname, scale in pieces:
            s0, wd = src[pname]
            o_ref[:, off:off + wd] = (w_ref[:, s0:s0 + wd] * scale).astype(BF16)
            off += wd
        dst += width


def _proj_weight(w):
    depth, d, n_in = w.shape
    n_out = sum(wd for _, wd, _ in _PROJ_OUTS)
    rows = 256
    return pl.pallas_call(
        _proj_weight_kernel,
        grid=(depth, d // rows),
        in_specs=[pl.BlockSpec((None, rows, n_in), lambda l, r: (l, r, 0))],
        out_specs=pl.BlockSpec((None, rows, n_out), lambda l, r: (l, r, 0)),
        out_shape=jax.ShapeDtypeStruct((depth, d, n_out), BF16),
        compiler_params=_cparams("arbitrary", "arbitrary"),
        name="proj_weight",
    )(w)


def _slab(arr, *lead):
    block = (None,) * len(lead) + tuple(arr.shape[len(lead):])
    index = tuple(lead) + (0,) * (arr.ndim - len(lead))
    return pl.BlockSpec(block, lambda i: index, pipeline_mode=pl.Buffered(1))


def _proj_call(x, mod, w_all, *, layer, tm):
    s, d = x.shape
    n = w_all.shape[-1]
    assert n == sum(wd for _, wd, _ in _PROJ_OUTS)
    n_vt = ATTN_HEADS * _VT_ROWS
    out_specs = [pl.BlockSpec((1, n_vt, tm), lambda i: (i, 0, 0)) if name == "v"
                 else pl.BlockSpec((tm, wd), lambda i: (i, 0)) for name, wd, _ in _PROJ_OUTS]
    out_shape = [jax.ShapeDtypeStruct((s // tm, n_vt, tm) if name == "v" else (s, wd), dt)
                 for name, wd, dt in _PROJ_OUTS]
    return pl.pallas_call(
        _proj_kernel,
        grid=(s // tm,),
        in_specs=[pl.BlockSpec((tm, d), lambda i: (i, 0)),
                  pl.BlockSpec((3 * N_SUB, d), lambda i: (0, 0)),
                  _slab(w_all, layer)],
        out_specs=out_specs,
        out_shape=out_shape,
        compiler_params=_cparams("arbitrary"),
        name="mix_proj",
    )(x, mod, w_all)


_BISECT_MAX_ITERS = 320
_F32_TINY = float(jnp.finfo(jnp.float32).tiny)
_COARSE_PASSES = 9
_COARSE_MIN_ABS, _COARSE_MAX_ABS = 1e-30, 1e30
_KIND_FAR, _KIND_PAD, _N_KINDS = 2, 3, 4
_VT_ROWS = ATTN_HEAD_DIM + 16
_SMASK_FULL, _SMASK_DIAG, _SMASK_PAD = 0, 1, 2
_LOG2E = math.log2(math.e)


def _t5_bucket(dist):
    max_exact = REL_BUCKETS // 2
    d32 = jnp.maximum(dist, 1).astype(F32)
    large = max_exact + (jnp.log(d32 / max_exact) / math.log(REL_MAX_DIST / max_exact)
                         * (REL_BUCKETS - max_exact)).astype(I32)
    large = jnp.minimum(large, REL_BUCKETS - 1)
    return jnp.where(dist < max_exact, dist, large)


def _dsa_kernel(relb_ref, qi_ref, wi_ref, q_ref, ki2_ref, k_ref, vt_ref, o_ref,
                s_sc, sb_sc, bias_sc, smask_sc, lg_sc, qim_sc, qm_sc, m_sc, acc_sc,
                *, tq, topk, idx_scale):
    tk = tq
    i = pl.program_id(0)
    krow = lax.broadcasted_iota(I32, (tk, tq), 0)
    qcol = lax.broadcasted_iota(I32, (tk, tq), 1)
    causal = krow <= qcol
    lane = lax.broadcasted_iota(I32, (tq, LANES), 1)
    lo_half = lane < ATTN_HEAD_DIM

    @pl.when(i == 0)
    def _():
        for which in range(2):
            bucket = _t5_bucket(jnp.maximum(qcol - krow + which * tq, 0))
            for h in range(ATTN_HEADS):
                b = jnp.zeros((tk, tq), F32)
                for bk in range(REL_BUCKETS):
                    b = jnp.where(bucket == bk, relb_ref[bk, h] * _LOG2E, b)
                bias_sc[h, which] = (jnp.where(causal, b, NEG_INF) if which == 0 else b).astype(BF16)
        for h in range(ATTN_HEADS):
            bias_sc[h, _KIND_FAR] = jnp.full((tk, tq), relb_ref[REL_BUCKETS - 1, h] * _LOG2E, F32).astype(BF16)
            bias_sc[h, _KIND_PAD] = jnp.full((tk, tq), NEG_INF, BF16)
        smask_sc[_SMASK_FULL] = jnp.zeros((tk, tq), F32)
        smask_sc[_SMASK_DIAG] = jnp.where(causal, 0.0, NEG_INF)
        smask_sc[_SMASK_PAD] = jnp.full((tk, tq), NEG_INF, F32)

    def head_operands(src_ref, dst_sc, n_heads):
        src = src_ref[...].astype(F32)
        for h in range(n_heads):
            pair = src[:, LANES * (h // 2):LANES * (h // 2 + 1)]
            pair = jnp.where(lo_half if h % 2 == 0 else ~lo_half, pair, 0.0)
            dst_sc[h] = jnp.transpose(pair).astype(BF16)

    head_operands(qi_ref, qim_sc, IDX_HEADS)
    head_operands(q_ref, qm_sc, ATTN_HEADS)

    wt = jnp.transpose(wi_ref[...])

    def key_rows(jb):
        return pl.ds(pl.multiple_of(jb * tk, tk), tk)

    def fold(x):
        return x.reshape(tk // SUBLANES, SUBLANES, tq)

    def index_scores(jb):
        kb = ki2_ref[key_rows(jb), :]
        acc = jnp.zeros((tk, tq), F32)
        for h in range(IDX_HEADS):
            acc = acc + wt[h:h + 1, :] * jnp.maximum(_dot(kb, qim_sc[h]), 0.0)
        return acc * idx_scale

    npair = (i + 2) // 2
    last_kv = k_ref.shape[0] // tk - 1

    def p1_blocks(jbs, carry):
        rmin, rmax = carry
        for jb in jbs:
            raw = index_scores(jnp.minimum(jb, last_kv))
            kind = jnp.where(jb > i, _SMASK_PAD, jnp.where(jb == i, _SMASK_DIAG, _SMASK_FULL))
            sc = raw + smask_sc[kind]
            s_sc[jb] = sc
            sb_sc[jb] = sc.astype(BF16)
            rmin = jnp.minimum(rmin, jnp.min(fold(raw), axis=0))
            rmax = jnp.maximum(rmax, jnp.max(fold(raw), axis=0))
        return rmin, rmax

    nquad = npair // 2

    def walk_pairs(pairs_fn, init):
        n4 = npair // 4
        carry = lax.fori_loop(0, n4, lambda t, c: pairs_fn([4 * t + r for r in range(4)], c), init)
        n2 = (npair - 4 * n4) // 2
        carry = lax.fori_loop(2 * n4, 2 * n4 + n2, lambda t, c: pairs_fn([2 * t, 2 * t + 1], c), carry)
        return lax.fori_loop(4 * n4 + 2 * n2, npair, lambda t, c: pairs_fn([t], c), carry)

    rmin, rmax = walk_pairs(lambda js, c: p1_blocks([2 * j + r for j in js for r in range(2)], c),
                            (jnp.full((SUBLANES, tq), POS_INF, F32), jnp.full((SUBLANES, tq), NEG_INF, F32)))
    rowmin = jnp.min(rmin, axis=0, keepdims=True)
    rowmax = jnp.max(rmax, axis=0, keepdims=True)

    kf = float(topk)
    nfin = (i * tq + lax.broadcasted_iota(I32, (1, tq), 1) + 1).astype(F32)
    small = nfin <= kf

    def walk_blocks(blocks_fn, init):
        return lax.fori_loop(0, npair, lambda j, c: blocks_fn([2 * j, 2 * j + 1], c), init)

    def count_ge(x):
        def blocks_fn(jbs, cnt):
            for jb in jbs:
                cnt = cnt + jnp.sum(fold(jnp.where(s_sc[jb] >= x, 1.0, 0.0)), axis=0)
            return cnt
        return jnp.sum(walk_blocks(blocks_fn, jnp.zeros((SUBLANES, tq), F32)), axis=0, keepdims=True)

    def count_ge16(xb):
        one, zero = jnp.ones((), BF16), jnp.zeros((), BF16)
        rows16 = 2 * SUBLANES

        def blocks_fn(jbs, cnt):
            parts = []
            for jb in jbs:
                c = jnp.where(sb_sc[jb] >= xb, one, zero)
                parts += [c[r0:r0 + rows16, :] for r0 in range(0, tk, rows16)]
            while len(parts) > 1:
                parts = [a + b for a, b in zip(parts[0::2], parts[1::2])]
            return cnt + parts[0].astype(F32)
        return jnp.sum(walk_blocks(blocks_fn, jnp.zeros((rows16, tq), F32)), axis=0, keepdims=True)

    def coarse_probe(mid):
        xb = mid.astype(BF16)
        xbits = lax.bitcast_convert_type(xb.astype(F32), I32)
        neg = xbits < 0
        mu = jnp.where(neg, xbits + 0x8000, xbits - 0x8000)
        odd = (xbits & 0x10000) != 0
        tbits = jnp.where(odd, jnp.where(neg, mu - 1, mu + 1), mu)
        t = lax.bitcast_convert_type(tbits, F32)
        ok = jnp.logical_and(jnp.abs(xb.astype(F32)) > _COARSE_MIN_ABS, jnp.abs(xb.astype(F32)) < _COARSE_MAX_ABS)
        return xb, t, ok

    def narrow(st, probe, c, active):
        lo, hi, clo, chi, done = st
        up = jnp.logical_and(active, c >= kf)
        dn = jnp.logical_and(active, c < kf)
        return (jnp.where(up, probe, lo), jnp.where(dn, probe, hi), jnp.where(up, c, clo),
                jnp.where(dn, c, chi), jnp.where(jnp.logical_and(active, c == kf), 1.0, done))

    def coarse_body(_, st):
        lo, hi, _, _, done = st
        xb, t, ok = coarse_probe(0.5 * lo + 0.5 * hi)
        ok = jnp.logical_and(ok, jnp.logical_and(t > lo, t < hi))
        return narrow(st, t, count_ge16(xb), jnp.logical_and(done < 0.5, ok))

    def fine_cond(st):
        it, (_, _, _, _, done) = st
        return jnp.logical_and(it < _BISECT_MAX_ITERS, jnp.min(done) < 0.5)

    def fine_body(st):
        it, inner = st
        lo, hi, _, _, done = inner
        mid = 0.5 * lo + 0.5 * hi
        mid = jnp.where(jnp.logical_and(lo == 0.0, hi > _F32_TINY), _F32_TINY, mid)
        mid = jnp.where(jnp.logical_and(lo < 0.0, hi > 0.0), 0.0, mid)
        stuck = jnp.logical_or(jnp.logical_or(mid <= lo, mid >= hi),
                               jnp.logical_and(lo == 0.0, hi <= _F32_TINY))
        lo, hi, clo, chi, done = narrow(inner, mid, count_ge(mid),
                                        jnp.logical_and(done < 0.5, jnp.logical_not(stuck)))
        return it + 1, (lo, hi, clo, chi, jnp.where(stuck, 1.0, done))

    mbits = lax.bitcast_convert_type(rowmax, I32)
    above_max = jnp.where(rowmax == 0.0, _F32_TINY,
                          lax.bitcast_convert_type(jnp.where(rowmax > 0.0, mbits + 1, mbits - 1), F32))
    st0 = (jnp.where(small, NEG_INF, rowmin), above_max, nfin, jnp.zeros((1, tq), F32),
           jnp.where(small, 1.0, 0.0))
    st1 = lax.fori_loop(0, _COARSE_PASSES, coarse_body, st0)
    _, (lo, hi, clo, chi, _) = lax.while_loop(fine_cond, fine_body, (jnp.int32(0), st1))
    budget = kf - chi
    need_ties = jnp.max((clo - chi) - budget) > 0.5

    pairs = [slice(LANES * (h // 2), LANES * (h // 2 + 1)) for h in range(ATTN_HEADS)]

    def issue_qk(jb, slot):
        kb = k_ref[key_rows(jnp.minimum(jb, last_kv)), :]
        for h in range(ATTN_HEADS):
            lg_sc[slot, h] = _dot(kb[:, pairs[h]], qm_sc[h]).astype(BF16)

    def attend(jb, slot, thr, cum=None):
        s = s_sc[jb]
        if cum is None:
            sel = s >= thr
        else:
            cand = jnp.logical_and(s >= lo, s < hi)
            before = (qcol < krow).astype(BF16)
            rank = cum + _dot(before, jnp.where(cand, 1.0, 0.0).astype(BF16))
            sel = jnp.logical_or(s >= hi, jnp.logical_and(cand, rank < budget))
        kind = jnp.where(jb > i, _KIND_PAD, jnp.minimum(i - jb, _KIND_FAR))
        selm = jnp.where(sel, 0.0, NEG_INF).astype(BF16)
        probs, alphas = [], []
        for h in range(ATTN_HEADS):
            lg = lg_sc[slot, h] + bias_sc[h, kind] + selm
            m_old = m_sc[h]
            m_new = jnp.maximum(m_old, jnp.max(lg, axis=0, keepdims=True).astype(F32))
            m_safe = jnp.where(m_new == NEG_INF, 0.0, m_new)
            alphas.append(jnp.exp2(m_old - m_safe))
            probs.append(jnp.exp2(lg - m_safe.astype(BF16)))
            m_sc[h] = m_new
        vb = vt_ref[jnp.minimum(jb, last_kv)]
        for h in range(ATTN_HEADS):
            vrows = slice(_VT_ROWS * h, _VT_ROWS * (h + 1))
            acc_sc[h] = alphas[h] * acc_sc[h] + _dot(vb[vrows, :], probs[h])

    def run(with_ties):
        m_sc[...] = jnp.full(m_sc.shape, NEG_INF, F32)
        acc_sc[...] = jnp.zeros_like(acc_sc)
        issue_qk(0, 0)

        def pair(j, thr0, thr1, cum0=None, cum1=None):
            issue_qk(2 * j + 1, 1)
            attend(2 * j, 0, thr0, cum0)
            issue_qk(2 * j + 2, 0)
            attend(2 * j + 1, 1, thr1, cum1)

        if not with_ties:
            def pairs_fn(js, carry):
                for j in js:
                    pair(j, lo, lo)
                return carry
            walk_pairs(pairs_fn, 0)
            return

        last_s = s_sc.shape[0] - 1

        def cand_count(jb):
            s = s_sc[jnp.minimum(jb, last_s)]
            cand = jnp.logical_and(s >= lo, s < hi)
            return jnp.sum(jnp.where(cand, 1.0, 0.0), axis=0, keepdims=True)

        def needs_ranking(cum, counts):
            runs_out = None
            for cnt in counts:
                here = jnp.logical_and(cum + cnt > budget, cum < budget)
                runs_out = here if runs_out is None else jnp.logical_or(runs_out, here)
                cum = cum + cnt
            return (jnp.max(jnp.where(runs_out, 1.0, 0.0)) > 0.5).astype(I32)

        def tie_group_body(n_blocks):
            def body(g, carry):
                ranked, cums = carry[0], [carry[1]]
                for cnt in carry[2:]:
                    cums.append(cums[-1] + cnt)
                base = n_blocks * g

                def variant(with_rank):
                    def fn():
                        ahead = [cand_count(base + n_blocks + r) for r in range(n_blocks)]
                        nxt = needs_ranking(cums[-1], ahead)
                        for p in range(n_blocks // 2):
                            if with_rank:
                                pair(base // 2 + p, None, None, cums[2 * p], cums[2 * p + 1])
                            else:
                                pair(base // 2 + p, jnp.where(cums[2 * p + 1] <= budget, lo, hi),
                                     jnp.where(cums[2 * p + 2] <= budget, lo, hi))
                        return (nxt, cums[-1], *ahead)
                    return fn
                return lax.cond(ranked == 1, variant(True), variant(False))
            return body

        zero = jnp.zeros((1, tq), F32)
        first = [cand_count(r) for r in range(8)]
        n8 = npair // 4
        n4 = (npair - 4 * n8) // 2
        st = lax.fori_loop(0, n8, tie_group_body(8), (needs_ranking(zero, first), zero, *first))
        st = lax.fori_loop(2 * n8, 2 * n8 + n4, tie_group_body(4),
                           (needs_ranking(st[1], st[2:6]), st[1], *st[2:6]))
        lax.fori_loop(4 * n8 + 2 * n4, npair, tie_group_body(2),
                      (needs_ranking(st[1], st[2:4]), st[1], *st[2:4]))

    @pl.when(need_ties)
    def _():
        run(True)

    @pl.when(jnp.logical_not(need_ties))
    def _():
        run(False)

    hd = ATTN_HEAD_DIM
    normed = [acc_sc[h, 0:hd, :] / acc_sc[h, hd:hd + 1, :] for h in range(ATTN_HEADS)]
    for p in range(ATTN_HEADS // 2):
        pair_t = jnp.concatenate([normed[2 * p], normed[2 * p + 1]], axis=0)
        o_ref[:, LANES * p:LANES * (p + 1)] = jnp.transpose(pair_t)


def _dsa_call(rel_bias, qi, wi, q, ki2, k, vt, *, tq):
    s = q.shape[0]
    assert tq > REL_MAX_DIST and s % tq == 0
    nb = s // tq
    topk = min(TOPK_MAX, s // 4)
    kern = functools.partial(_dsa_kernel, tq=tq, topk=topk,
                             idx_scale=(IDX_DIM ** -0.5) * (IDX_HEADS ** -0.5))
    whole = lambda width: pl.BlockSpec((s, width), lambda i: (0, 0), pipeline_mode=pl.Buffered(1))
    assert vt.shape == (nb, ATTN_HEADS * _VT_ROWS, tq)
    return pl.pallas_call(
        kern,
        grid=(nb,),
        in_specs=[pl.BlockSpec(memory_space=pltpu.SMEM),
                  pl.BlockSpec((tq, IDX_HEADS * IDX_DIM), lambda i: (i, 0)),
                  pl.BlockSpec((tq, LANES), lambda i: (i, 0)),
                  pl.BlockSpec((tq, ATTN_WIDTH), lambda i: (i, 0)),
                  whole(2 * IDX_DIM), whole(ATTN_WIDTH),
                  pl.BlockSpec(vt.shape, lambda i: (0, 0, 0), pipeline_mode=pl.Buffered(1))],
        out_specs=pl.BlockSpec((tq, ATTN_WIDTH), lambda i: (i, 0)),
        out_shape=jax.ShapeDtypeStruct((s, ATTN_WIDTH), F32),
        scratch_shapes=[pltpu.VMEM((nb + nb % 2, tq, tq), F32),
                        pltpu.VMEM((nb + nb % 2, tq, tq), BF16),
                        pltpu.VMEM((ATTN_HEADS, _N_KINDS, tq, tq), BF16),
                        pltpu.VMEM((3, tq, tq), F32),
                        pltpu.VMEM((2, ATTN_HEADS, tq, tq), BF16),
                        pltpu.VMEM((IDX_HEADS, LANES, tq), BF16),
                        pltpu.VMEM((ATTN_HEADS, LANES, tq), BF16),
                        pltpu.VMEM((ATTN_HEADS, 1, tq), F32),
                        pltpu.VMEM((ATTN_HEADS, _VT_ROWS, tq), F32)],
        compiler_params=_cparams("arbitrary"),
        name="dsa_attention",
    )(rel_bias, qi, wi, q, ki2, k, vt)


_HGRN_CHUNK = 128
_HGRN_BASE = SUBLANES


def _hgrn_kernel(h4_ref, lb_ref, ng_ref, o_ref, st_sc, *, layer, rows):
    c_len = _HGRN_CHUNK

    @pl.when(pl.program_id(0) == 0)
    def _():
        st_sc[...] = jnp.zeros_like(st_sc)

    lb_all = lb_ref[...]
    e = jnp.exp(lb_all - jnp.max(lb_all, axis=0, keepdims=True))
    sm = e / jnp.sum(e, axis=0, keepdims=True)
    lbs = jnp.sum(sm[0:layer + 1, :], axis=0, keepdims=True) - sm[0:1, :]

    r1 = lax.broadcasted_iota(I32, (c_len, 1), 0)
    rr = lax.broadcasted_iota(I32, (c_len, c_len), 0)
    cc = lax.broadcasted_iota(I32, (c_len, c_len), 1)

    def block_ref(a, blk, r):
        x = a.reshape(c_len // blk, blk, a.shape[-1])[:, r:r + 1, :]
        return jnp.broadcast_to(x, (c_len // blk, blk, a.shape[-1])).reshape(c_len, a.shape[-1])

    w = HGRN_WIDTH
    for c in range(rows // c_len):
        rs = slice(c * c_len, (c + 1) * c_len)
        hq = h4_ref[rs, 0:w]
        z = h4_ref[rs, w:2 * w]
        val = h4_ref[rs, 2 * w:3 * w]
        hg = h4_ref[rs, 3 * w:4 * w]
        key = (1.0 - lbs) * jax.nn.sigmoid(-z)
        g = jnp.log1p(-jnp.minimum(key, KEY_MAX))
        a_cum = g
        sh = 1
        while sh < c_len:
            a_cum = a_cum + jnp.where(r1 >= sh, pltpu.roll(a_cum, sh, 0), 0.0)
            sh *= 2
        qf = _silu(hq)
        outs = []
        for h in range(HGRN_HEADS):
            hs = slice(h * HGRN_DIM, (h + 1) * HGRN_DIM)
            ah, qh, kh, vh = a_cum[:, hs], qf[:, hs], key[:, hs], val[:, hs]
            vh16 = vh.astype(BF16)
            st = st_sc[h]
            o = _dot_nt((qh * jnp.exp(ah)).astype(BF16), st.astype(BF16))
            ref = block_ref(ah, _HGRN_BASE, _HGRN_BASE // 2 - 1)
            sc = _dot_nt((qh * jnp.exp(ah - ref)).astype(BF16), (kh * jnp.exp(ref - ah)).astype(BF16))
            same = (rr >> _log2(_HGRN_BASE)) == (cc >> _log2(_HGRN_BASE))
            scores = jnp.where(jnp.logical_and(same, cc <= rr), sc, 0.0)
            half = _HGRN_BASE
            while half < c_len:
                blk = 2 * half
                ref = block_ref(ah, blk, half - 1)
                second = (r1 & (blk - 1)) >= half
                ql = jnp.where(second, qh * jnp.exp(jnp.minimum(ah - ref, 0.0)), 0.0)
                kl = jnp.where(second, 0.0, kh * jnp.exp(jnp.minimum(ref - ah, 0.0)))
                sc = _dot_nt(ql.astype(BF16), kl.astype(BF16))
                scores = scores + jnp.where((rr >> _log2(blk)) == (cc >> _log2(blk)), sc, 0.0)
                half = blk
            o = o + _dot(scores.astype(BF16), vh16)
            a_last = ah[c_len - 1:c_len, :]
            kd = (kh * jnp.exp(a_last - ah)).astype(BF16)
            st_sc[h] = st * jnp.exp(a_last) + _dot_tn(vh16, kd)
            o = o * lax.rsqrt(jnp.mean(o * o, axis=-1, keepdims=True) + RMS_EPS)
            outs.append(o)
        o_ref[rs, :] = jnp.concatenate(outs, axis=1) * ng_ref[...] * _silu(hg)


def _hgrn_call(h4, lb, ng, *, layer, rows):
    s = h4.shape[0]
    kern = functools.partial(_hgrn_kernel, layer=layer, rows=rows)
    return pl.pallas_call(
        kern,
        grid=(s // rows,),
        in_specs=[pl.BlockSpec((rows, 4 * HGRN_WIDTH), lambda i: (i, 0)),
                  pl.BlockSpec(lb.shape, lambda i: (0, 0)),
                  _slab(ng, layer)],
        out_specs=pl.BlockSpec((rows, HGRN_WIDTH), lambda i: (i, 0)),
        out_shape=jax.ShapeDtypeStruct((s, HGRN_WIDTH), F32),
        scratch_shapes=[pltpu.VMEM((HGRN_HEADS, HGRN_DIM, HGRN_DIM), F32)],
        compiler_params=_cparams("arbitrary"),
        name="hgrn2",
    )(h4, lb, ng)


_POOL_HALO = 16


def _merge_kernel(x_ref, mod_ref, a_ref, ah_ref, yb_ref, yc_ref, g_ref, pw_ref, ps_ref, wb_ref, wo_ref,
                  lng_ref, lnb_ref, fwi_ref, fwo_ref, o_ref, *, tm, tf, alpha):
    i = pl.program_id(0)
    d = x_ref.shape[1]
    _, _, gate = _mod_rows(mod_ref, 1)
    a = a_ref[...]
    halo = jnp.where(i == 0, 0.0, ah_ref[...])
    ext = jnp.concatenate([halo, a], axis=0)
    lane = lax.broadcasted_iota(I32, (tm, POOL_WIDTH), 1)
    grp = lane >> _log2(POOL_GROUP)
    pos1 = (i * tm + lax.broadcasted_iota(I32, (tm, POOL_WIDTH), 0) + 1).astype(F32)
    win = jnp.zeros((tm, POOL_WIDTH), F32)
    cnt = jnp.zeros((tm, POOL_WIDTH), F32)
    ssum, have = ext, 1
    for gi, wd in enumerate(POOL_WINDOWS):
        while have < wd:
            ssum = ssum + pltpu.roll(ssum, have, 0)
            have *= 2
        assert have == wd
        win = jnp.where(grp == gi, ssum[_POOL_HALO:, :], win)
        cnt = jnp.where(grp == gi, jnp.minimum(pos1, float(wd)), cnt)
    dlt = (win / cnt - a).astype(BF16)
    ya = _dot(dlt, pw_ref[...]) * ps_ref[...]
    za = _dot(ya.astype(BF16), wb_ref[0:POOL_WIDTH, :])
    zb = _dot(yb_ref[...].astype(BF16), wb_ref[POOL_WIDTH:POOL_WIDTH + ATTN_WIDTH, :])
    zc = _dot(yc_ref[...].astype(BF16), wb_ref[POOL_WIDTH + ATTN_WIDTH:, :])
    g0, g1, g2 = (jax.nn.sigmoid(g_ref[:, j * d:(j + 1) * d].astype(F32)) for j in range(3))
    merged = g0 * za + g1 * zb + g2 * zc
    y = _dot(merged.astype(BF16), wo_ref[...])
    z = alpha * x_ref[...] + (1.0 + gate) * y
    xm = _layer_norm(z, lng_ref[1:2, :], lnb_ref[1:2, :])
    o_ref[...] = _ffn_rows(xm, mod_ref, fwi_ref, fwo_ref, lng_ref, lnb_ref, sub=2, tf=tf, alpha=alpha)


def _merge_call(x, mod, a, yb, yc, gates, pw_bd, ps, wb, wo, ln_g, ln_b, ffn_wi, ffn_wo, *, layer, tm, tf, alpha):
    s, d = x.shape
    kern = functools.partial(_merge_kernel, tm=tm, tf=tf, alpha=alpha)
    full = lambda arr: _slab(arr, layer)
    hb = tm // _POOL_HALO
    return pl.pallas_call(
        kern,
        grid=(s // tm,),
        in_specs=[pl.BlockSpec((tm, d), lambda i: (i, 0)),
                  pl.BlockSpec((3 * N_SUB, d), lambda i: (0, 0)),
                  pl.BlockSpec((tm, POOL_WIDTH), lambda i: (i, 0)),
                  pl.BlockSpec((_POOL_HALO, POOL_WIDTH), lambda i: (jnp.maximum(i * hb - 1, 0), 0)),
                  pl.BlockSpec((tm, ATTN_WIDTH), lambda i: (i, 0)),
                  pl.BlockSpec((tm, HGRN_WIDTH), lambda i: (i, 0)),
                  pl.BlockSpec((tm, 3 * d), lambda i: (i, 0)),
                  full(pw_bd), full(ps), full(wb), full(wo), full(ln_g), full(ln_b),
                  _slab(ffn_wi, layer, 1), _slab(ffn_wo, layer, 1)],
        out_specs=pl.BlockSpec((tm, d), lambda i: (i, 0)),
        out_shape=jax.ShapeDtypeStruct((s, d), F32),
        compiler_params=_cparams("arbitrary"),
        name="merge_ffn",
    )(x, mod, a, a, yb, yc, gates, pw_bd, ps, wb, wo, ln_g, ln_b, ffn_wi, ffn_wo)


def _pool_blockdiag(pool_w):
    depth, n, g, _ = pool_w.shape
    eye = jnp.eye(n, dtype=pool_w.dtype)
    return jnp.einsum("lgij,gh->lgihj", pool_w, eye).reshape(depth, n * g, n * g)


def kernel(x, c, w_ada, b_ada, ln_g, ln_b, ffn_w_in, ffn_w_out, mix_w_in, pool_w, pool_scale, rel_bias,
           hgrn_lb, hgrn_norm_g, w_branch, w_out):
    batch, s, d = x.shape
    assert batch == 1, "kernels are written for a single sequence"
    depth = w_ada.shape[0]
    alpha = (2 * depth) ** 0.25
    tm_ffn = min(512, s)
    tf = 256
    tq = min(256, s)
    tm_proj = tq
    tm_merge = min(512, s)
    hgrn_rows = min(1024, s)

    ffn_wi16, ffn_wo16 = ffn_w_in.astype(BF16), ffn_w_out.astype(BF16)
    mix_w16 = _proj_weight(mix_w_in)
    pool_bd16 = _pool_blockdiag(pool_w).astype(BF16)
    wb16, wo16 = w_branch.astype(BF16), w_out.astype(BF16)
    pool_scale3 = pool_scale.reshape(depth, 1, -1)
    norm_g3 = hgrn_norm_g.reshape(depth, 1, -1)

    mods = _ada_call(c, w_ada, b_ada)
    xs = x.reshape(s, d)
    for l in range(depth):
        mod = mods[l].reshape(3 * N_SUB, d)
        xs = _ffn_call(xs, mod, ffn_wi16, ffn_wo16, ln_g, ln_b, layer=l, which=0, sub=0, alpha=alpha,
                       tm=tm_ffn, tf=tf)
        gates, a, h4, qi, q, k, vt, ki2, wi = _proj_call(xs, mod, mix_w16, layer=l, tm=tm_proj)
        yb = _dsa_call(rel_bias, qi, wi, q, ki2, k, vt, tq=tq)
        yc = _hgrn_call(h4, hgrn_lb, norm_g3, layer=l, rows=hgrn_rows)
        xs = _merge_call(xs, mod, a, yb, yc, gates, pool_bd16, pool_scale3, wb16, wo16, ln_g, ln_b,
                         ffn_wi16, ffn_wo16, layer=l, tm=tm_merge, tf=tf, alpha=alpha)
    return xs.reshape(batch, s, d)
```

```python
import functools
import math

import jax
import jax.numpy as jnp
from jax import lax
from jax.experimental import pallas as pl
from jax.experimental.pallas import tpu as pltpu

F32 = jnp.float32
BF16 = jnp.bfloat16
I32 = jnp.int32

POOL_WINDOWS = (2, 4, 8, 16)
POOL_GROUP = 64
POOL_WIDTH = POOL_GROUP * len(POOL_WINDOWS)
ATTN_HEADS = 4
ATTN_HEAD_DIM = 64
ATTN_WIDTH = ATTN_HEADS * ATTN_HEAD_DIM
IDX_HEADS = 8
IDX_DIM = 64
TOPK_MAX = 256
HGRN_HEADS = 4
HGRN_DIM = 128
HGRN_WIDTH = HGRN_HEADS * HGRN_DIM
KEY_MAX = 1.0 - 1e-6
REL_BUCKETS = 32
REL_MAX_DIST = 128
N_SUB = 3
LN_EPS = 1e-5
RMS_EPS = 1e-6

LANES = 128
SUBLANES = 8
VMEM_LIMIT_BYTES = 58 * 1024 * 1024

NEG_INF = float("-inf")
POS_INF = float("inf")


def _cparams(*sem):
    return pltpu.CompilerParams(dimension_semantics=sem, vmem_limit_bytes=VMEM_LIMIT_BYTES)


def _dot(a, b):
    return jnp.dot(a, b, preferred_element_type=F32)


def _dot_nt(a, b):
    return lax.dot_general(a, b, (((1,), (1,)), ((), ())), preferred_element_type=F32)


def _dot_tn(a, b):
    return lax.dot_general(a, b, (((0,), (0,)), ((), ())), preferred_element_type=F32)


def _log2(n):
    assert n & (n - 1) == 0
    return n.bit_length() - 1


def _silu(x):
    return x * jax.nn.sigmoid(x)


def _mod_rows(mod_ref, j):
    return mod_ref[3 * j:3 * j + 1, :], mod_ref[3 * j + 1:3 * j + 2, :], mod_ref[3 * j + 2:3 * j + 3, :]


def _layer_norm(z, g, b):
    mu = jnp.mean(z, axis=-1, keepdims=True)
    zc = z - mu
    var = jnp.mean(zc * zc, axis=-1, keepdims=True)
    return zc * lax.rsqrt(var + LN_EPS) * g + b


def _ada_kernel(c_ref, w_ref, b_ref, o_ref):
    @pl.when(pl.program_id(1) == 0)
    def _():
        o_ref[0] = b_ref[0]

    c = c_ref[...]
    o_ref[0] += jnp.sum(_silu(c) * w_ref[0], axis=0, keepdims=True)


def _ada_call(c, w_ada, b_ada):
    depth, d, n = w_ada.shape
    rows = LANES
    return pl.pallas_call(
        _ada_kernel,
        grid=(depth, d // rows),
        in_specs=[pl.BlockSpec((rows, 1), lambda l, k: (k, 0)),
                  pl.BlockSpec((1, rows, n), lambda l, k: (l, k, 0)),
                  pl.BlockSpec((1, 1, n), lambda l, k: (l, 0, 0))],
        out_specs=pl.BlockSpec((1, 1, n), lambda l, k: (l, 0, 0)),
        out_shape=jax.ShapeDtypeStruct((depth, 1, n), F32),
        compiler_params=_cparams("arbitrary", "arbitrary"),
        name="ada_mod",
    )(c.reshape(d, 1), w_ada, b_ada.reshape(depth, 1, n))


def _ffn_rows(x, mod_ref, wi_ref, wo_ref, lng_ref, lnb_ref, *, sub, tf, alpha):
    shift, scale, gate = _mod_rows(mod_ref, sub)
    ff = wo_ref.shape[0]
    h = (x * (1.0 + scale) + shift).astype(BF16)
    y = jnp.zeros(x.shape, F32)
    for c0 in range(0, ff, tf):
        g = _dot(h, wi_ref[:, c0:c0 + tf])
        u = _dot(h, wi_ref[:, ff + c0:ff + c0 + tf])
        y = y + _dot((_silu(g) * u).astype(BF16), wo_ref[c0:c0 + tf, :])
    z = alpha * x + (0.5 * (1.0 + gate)) * y
    return _layer_norm(z, lng_ref[sub:sub + 1, :], lnb_ref[sub:sub + 1, :])


def _ffn_kernel(x_ref, mod_ref, wi_ref, wo_ref, lng_ref, lnb_ref, o_ref, *, sub, tf, alpha):
    o_ref[...] = _ffn_rows(x_ref[...], mod_ref, wi_ref, wo_ref, lng_ref, lnb_ref, sub=sub, tf=tf, alpha=alpha)


def _ffn_call(x, mod, w_in, w_out, ln_g, ln_b, *, layer, which, sub, alpha, tm, tf):
    s, d = x.shape
    assert w_out.shape[2] % tf == 0
    kern = functools.partial(_ffn_kernel, sub=sub, tf=tf, alpha=alpha)
    return pl.pallas_call(
        kern,
        grid=(s // tm,),
        in_specs=[pl.BlockSpec((tm, d), lambda i: (i, 0)),
                  pl.BlockSpec((3 * N_SUB, d), lambda i: (0, 0)),
                  _slab(w_in, layer, which), _slab(w_out, layer, which),
                  _slab(ln_g, layer), _slab(ln_b, layer)],
        out_specs=pl.BlockSpec((tm, d), lambda i: (i, 0)),
        out_shape=jax.ShapeDtypeStruct((s, d), F32),
        compiler_params=_cparams("arbitrary"),
        name="ffn",
    )(x, mod, w_in, w_out, ln_g, ln_b)


_PROJ_OUTS = (("gates", 3 * 1024, BF16), ("a", POOL_WIDTH, F32), ("h4", 4 * HGRN_WIDTH, F32),
              ("qi", IDX_HEADS * IDX_DIM, BF16), ("q", ATTN_WIDTH, BF16), ("k", ATTN_WIDTH, BF16),
              ("v", ATTN_WIDTH, BF16), ("ki2", 2 * IDX_DIM, BF16), ("wi", LANES, F32))
_PROJ_CHUNK = 1024


def _proj_kernel(x_ref, mod_ref, w_ref, *o_refs):
    shift, scale, _ = _mod_rows(mod_ref, 1)
    h = (x_ref[...] * (1.0 + scale) + shift).astype(BF16)
    c0 = 0
    for (name, width, dt), o_ref in zip(_PROJ_OUTS, o_refs):
        if name == "v":
            vt = lax.dot_general(w_ref[:, c0:c0 + width], h, (((0,), (1,)), ((), ())),
                                 preferred_element_type=F32)
            ones = jnp.ones((_VT_ROWS - ATTN_HEAD_DIM, vt.shape[1]), F32)
            rows = []
            for p0 in range(0, width, ATTN_HEAD_DIM):
                rows += [vt[p0:p0 + ATTN_HEAD_DIM, :], ones]
            o_ref[0] = jnp.concatenate(rows, axis=0).astype(dt)
        else:
            for cc in range(0, width, _PROJ_CHUNK):
                cw = min(_PROJ_CHUNK, width - cc)
                o_ref[:, cc:cc + cw] = _dot(h, w_ref[:, c0 + cc:c0 + cc + cw]).astype(dt)
        c0 += width


def _proj_weight_kernel(w_ref, o_ref):
    out_width = {name: wd for name, wd, _ in _PROJ_OUTS}
    src = {}
    c0 = 0
    for name, wd in (("a", POOL_WIDTH), ("q", ATTN_WIDTH), ("k", ATTN_WIDTH), ("v", ATTN_WIDTH),
                     ("qi", IDX_HEADS * IDX_DIM), ("ki", IDX_DIM), ("wi", IDX_HEADS),
                     ("h4", 4 * HGRN_WIDTH), ("gates", out_width["gates"])):
        src[name] = (c0, wd)
        c0 += wd
    assert c0 == w_ref.shape[-1]
    q_scale = ATTN_HEAD_DIM ** -0.5 * _LOG2E
    dst = 0
    for name, width, _ in _PROJ_OUTS:
        if name == "ki2":
            pieces = [("ki", 1.0), ("ki", 1.0)]
        elif name == "wi":
            o_ref[:, dst:dst + width] = jnp.zeros((o_ref.shape[0], width), BF16)
            pieces = [("wi", 1.0)]
        else:
            pieces = [(name, q_scale if name == "q" else 1.0)]
        off = dst
        for pname, scale in pieces:
            s0, wd = src[pname]
            o_ref[:, off:off + wd] = (w_ref[:, s0:s0 + wd] * scale).astype(BF16)
            off += wd
        dst += width


def _proj_weight(w):
    depth, d, n_in = w.shape
    n_out = sum(wd for _, wd, _ in _PROJ_OUTS)
    rows = 256
    return pl.pallas_call(
        _proj_weight_kernel,
        grid=(depth, d // rows),
        in_specs=[pl.BlockSpec((None, rows, n_in), lambda l, r: (l, r, 0))],
        out_specs=pl.BlockSpec((None, rows, n_out), lambda l, r: (l, r, 0)),
        out_shape=jax.ShapeDtypeStruct((depth, d, n_out), BF16),
        compiler_params=_cparams("arbitrary", "arbitrary"),
        name="proj_weight",
    )(w)


def _slab(arr, *lead):
    block = (None,) * len(lead) + tuple(arr.shape[len(lead):])
    index = tuple(lead) + (0,) * (arr.ndim - len(lead))
    return pl.BlockSpec(block, lambda i: index, pipeline_mode=pl.Buffered(1))


def _proj_call(x, mod, w_all, *, layer, tm):
    s, d = x.shape
    n = w_all.shape[-1]
    assert n == sum(wd for _, wd, _ in _PROJ_OUTS)
    n_vt = ATTN_HEADS * _VT_ROWS
    out_specs = [pl.BlockSpec((1, n_vt, tm), lambda i: (i, 0, 0)) if name == "v"
                 else pl.BlockSpec((tm, wd), lambda i: (i, 0)) for name, wd, _ in _PROJ_OUTS]
    out_shape = [jax.ShapeDtypeStruct((s // tm, n_vt, tm) if name == "v" else (s, wd), dt)
                 for name, wd, dt in _PROJ_OUTS]
    return pl.pallas_call(
        _proj_kernel,
        grid=(s // tm,),
        in_specs=[pl.BlockSpec((tm, d), lambda i: (i, 0)),
                  pl.BlockSpec((3 * N_SUB, d), lambda i: (0, 0)),
                  _slab(w_all, layer)],
        out_specs=out_specs,
        out_shape=out_shape,
        compiler_params=_cparams("arbitrary"),
        name="mix_proj",
    )(x, mod, w_all)


_BISECT_MAX_ITERS = 320
_F32_TINY = float(jnp.finfo(jnp.float32).tiny)
_COARSE_PASSES = 9
_COARSE_MIN_ABS, _COARSE_MAX_ABS = 1e-30, 1e30
_KIND_FAR, _KIND_PAD, _N_KINDS = 2, 3, 4
_VT_ROWS = ATTN_HEAD_DIM + 16
_SMASK_FULL, _SMASK_DIAG, _SMASK_PAD = 0, 1, 2
_LOG2E = math.log2(math.e)


def _t5_bucket(dist):
    max_exact = REL_BUCKETS // 2
    d32 = jnp.maximum(dist, 1).astype(F32)
    large = max_exact + (jnp.log(d32 / max_exact) / math.log(REL_MAX_DIST / max_exact)
                         * (REL_BUCKETS - max_exact)).astype(I32)
    large = jnp.minimum(large, REL_BUCKETS - 1)
    return jnp.where(dist < max_exact, dist, large)


def _dsa_kernel(relb_ref, qi_ref, wi_ref, q_ref, ki2_ref, k_ref, vt_ref, o_ref,
                s_sc, sb_sc, bias_sc, smask_sc, lg_sc, qim_sc, qm_sc, m_sc, acc_sc,
                *, tq, topk, idx_scale):
    tk = tq
    i = pl.program_id(0)
    krow = lax.broadcasted_iota(I32, (tk, tq), 0)
    qcol = lax.broadcasted_iota(I32, (tk, tq), 1)
    causal = krow <= qcol
    lane = lax.broadcasted_iota(I32, (tq, LANES), 1)
    lo_half = lane < ATTN_HEAD_DIM

    @pl.when(i == 0)
    def _():
        for which in range(2):
            bucket = _t5_bucket(jnp.maximum(qcol - krow + which * tq, 0))
            for h in range(ATTN_HEADS):
                b = jnp.zeros((tk, tq), F32)
                for bk in range(REL_BUCKETS):
                    b = jnp.where(bucket == bk, relb_ref[bk, h] * _LOG2E, b)
                bias_sc[h, which] = (jnp.where(causal, b, NEG_INF) if which == 0 else b).astype(BF16)
        for h in range(ATTN_HEADS):
            bias_sc[h, _KIND_FAR] = jnp.full((tk, tq), relb_ref[REL_BUCKETS - 1, h] * _LOG2E, F32).astype(BF16)
            bias_sc[h, _KIND_PAD] = jnp.full((tk, tq), NEG_INF, BF16)
        smask_sc[_SMASK_FULL] = jnp.zeros((tk, tq), F32)
        smask_sc[_SMASK_DIAG] = jnp.where(causal, 0.0, NEG_INF)
        smask_sc[_SMASK_PAD] = jnp.full((tk, tq), NEG_INF, F32)

    def head_operands(src_ref, dst_sc, n_heads):
        src = src_ref[...].astype(F32)
        for h in range(n_heads):
            pair = src[:, LANES * (h // 2):LANES * (h // 2 + 1)]
            pair = jnp.where(lo_half if h % 2 == 0 else ~lo_half, pair, 0.0)
            dst_sc[h] = jnp.transpose(pair).astype(BF16)

    head_operands(qi_ref, qim_sc, IDX_HEADS)
    head_operands(q_ref, qm_sc, ATTN_HEADS)

    wt = jnp.transpose(wi_ref[...])

    def key_rows(jb):
        return pl.ds(pl.multiple_of(jb * tk, tk), tk)

    def fold(x):
        return x.reshape(tk // SUBLANES, SUBLANES, tq)

    def index_scores(jb):
        kb = ki2_ref[key_rows(jb), :]
        acc = jnp.zeros((tk, tq), F32)
        for h in range(IDX_HEADS):
            acc = acc + wt[h:h + 1, :] * jnp.maximum(_dot(kb, qim_sc[h]), 0.0)
        return acc * idx_scale

    npair = (i + 2) // 2
    last_kv = k_ref.shape[0] // tk - 1

    def p1_blocks(jbs, carry):
        rmin, rmax = carry
        for jb in jbs:
            raw = index_scores(jnp.minimum(jb, last_kv))
            kind = jnp.where(jb > i, _SMASK_PAD, jnp.where(jb == i, _SMASK_DIAG, _SMASK_FULL))
            sc = raw + smask_sc[kind]
            s_sc[jb] = sc
            sb_sc[jb] = sc.astype(BF16)
            rmin = jnp.minimum(rmin, jnp.min(fold(raw), axis=0))
            rmax = jnp.maximum(rmax, jnp.max(fold(raw), axis=0))
        return rmin, rmax

    nquad = npair // 2

    def walk_pairs(pairs_fn, init):
        n4 = npair // 4
        carry = lax.fori_loop(0, n4, lambda t, c: pairs_fn([4 * t + r for r in range(4)], c), init)
        n2 = (npair - 4 * n4) // 2
        carry = lax.fori_loop(2 * n4, 2 * n4 + n2, lambda t, c: pairs_fn([2 * t, 2 * t + 1], c), carry)
        return lax.fori_loop(4 * n4 + 2 * n2, npair, lambda t, c: pairs_fn([t], c), carry)

    rmin, rmax = walk_pairs(lambda js, c: p1_blocks([2 * j + r for j in js for r in range(2)], c),
                            (jnp.full((SUBLANES, tq), POS_INF, F32), jnp.full((SUBLANES, tq), NEG_INF, F32)))
    rowmin = jnp.min(rmin, axis=0, keepdims=True)
    rowmax = jnp.max(rmax, axis=0, keepdims=True)

    kf = float(topk)
    nfin = (i * tq + lax.broadcasted_iota(I32, (1, tq), 1) + 1).astype(F32)
    small = nfin <= kf

    def walk_blocks(blocks_fn, init):
        return lax.fori_loop(0, npair, lambda j, c: blocks_fn([2 * j, 2 * j + 1], c), init)

    def count_ge(x):
        def blocks_fn(jbs, cnt):
            for jb in jbs:
                cnt = cnt + jnp.sum(fold(jnp.where(s_sc[jb] >= x, 1.0, 0.0)), axis=0)
            return cnt
        return jnp.sum(walk_blocks(blocks_fn, jnp.zeros((SUBLANES, tq), F32)), axis=0, keepdims=True)

    def count_ge16(xb):
        one, zero = jnp.ones((), BF16), jnp.zeros((), BF16)
        rows16 = 2 * SUBLANES

        def blocks_fn(jbs, cnt):
            parts = []
            for jb in jbs:
                c = jnp.where(sb_sc[jb] >= xb, one, zero)
                parts += [c[r0:r0 + rows16, :] for r0 in range(0, tk, rows16)]
            while len(parts) > 1:
                parts = [a + b for a, b in zip(parts[0::2], parts[1::2])]
            return cnt + parts[0].astype(F32)
        return jnp.sum(walk_blocks(blocks_fn, jnp.zeros((rows16, tq), F32)), axis=0, keepdims=True)

    def coarse_probe(mid):
        xb = mid.astype(BF16)
        xbits = lax.bitcast_convert_type(xb.astype(F32), I32)
        neg = xbits < 0
        mu = jnp.where(neg, xbits + 0x8000, xbits - 0x8000)
        odd = (xbits & 0x10000) != 0
        tbits = jnp.where(odd, jnp.where(neg, mu - 1, mu + 1), mu)
        t = lax.bitcast_convert_type(tbits, F32)
        ok = jnp.logical_and(jnp.abs(xb.astype(F32)) > _COARSE_MIN_ABS, jnp.abs(xb.astype(F32)) < _COARSE_MAX_ABS)
        return xb, t, ok

    def narrow(st, probe, c, active):
        lo, hi, clo, chi, done = st
        up = jnp.logical_and(active, c >= kf)
        dn = jnp.logical_and(active, c < kf)
        return (jnp.where(up, probe, lo), jnp.where(dn, probe, hi), jnp.where(up, c, clo),
                jnp.where(dn, c, chi), jnp.where(jnp.logical_and(active, c == kf), 1.0, done))

    def coarse_body(_, st):
        lo, hi, _, _, done = st
        xb, t, ok = coarse_probe(0.5 * lo + 0.5 * hi)
        ok = jnp.logical_and(ok, jnp.logical_and(t > lo, t < hi))
        return narrow(st, t, count_ge16(xb), jnp.logical_and(done < 0.5, ok))

    def fine_cond(st):
        it, (_, _, _, _, done) = st
        return jnp.logical_and(it < _BISECT_MAX_ITERS, jnp.min(done) < 0.5)

    def fine_body(st):
        it, inner = st
        lo, hi, _, _, done = inner
        mid = 0.5 * lo + 0.5 * hi
        mid = jnp.where(jnp.logical_and(lo == 0.0, hi > _F32_TINY), _F32_TINY, mid)
        mid = jnp.where(jnp.logical_and(lo < 0.0, hi > 0.0), 0.0, mid)
        stuck = jnp.logical_or(jnp.logical_or(mid <= lo, mid >= hi),
                               jnp.logical_and(lo == 0.0, hi <= _F32_TINY))
        lo, hi, clo, chi, done = narrow(inner, mid, count_ge(mid),
                                        jnp.logical_and(done < 0.5, jnp.logical_not(stuck)))
        return it + 1, (lo, hi, clo, chi, jnp.where(stuck, 1.0, done))

    mbits = lax.bitcast_convert_type(rowmax, I32)
    above_max = jnp.where(rowmax == 0.0, _F32_TINY,
                          lax.bitcast_convert_type(jnp.where(rowmax > 0.0, mbits + 1, mbits - 1), F32))
    st0 = (jnp.where(small, NEG_INF, rowmin), above_max, nfin, jnp.zeros((1, tq), F32),
           jnp.where(small, 1.0, 0.0))
    st1 = lax.fori_loop(0, _COARSE_PASSES, coarse_body, st0)
    _, (lo, hi, clo, chi, _) = lax.while_loop(fine_cond, fine_body, (jnp.int32(0), st1))
    budget = kf - chi
    need_ties = jnp.max((clo - chi) - budget) > 0.5

    pairs = [slice(LANES * (h // 2), LANES * (h // 2 + 1)) for h in range(ATTN_HEADS)]

    def issue_qk(jb, slot):
        kb = k_ref[key_rows(jnp.minimum(jb, last_kv)), :]
        for h in range(ATTN_HEADS):
            lg_sc[slot, h] = _dot(kb[:, pairs[h]], qm_sc[h]).astype(BF16)

    def attend(jb, slot, thr, cum=None):
        s = s_sc[jb]
        if cum is None:
            sel = s >= thr
        else:
            cand = jnp.logical_and(s >= lo, s < hi)
            before = (qcol < krow).astype(BF16)
            rank = cum + _dot(before, jnp.where(cand, 1.0, 0.0).astype(BF16))
            sel = jnp.logical_or(s >= hi, jnp.logical_and(cand, rank < budget))
        kind = jnp.where(jb > i, _KIND_PAD, jnp.minimum(i - jb, _KIND_FAR))
        selm = jnp.where(sel, 0.0, NEG_INF).astype(BF16)
        probs, alphas = [], []
        for h in range(ATTN_HEADS):
            lg = lg_sc[slot, h] + bias_sc[h, kind] + selm
            m_old = m_sc[h]
            m_new = jnp.maximum(m_old, jnp.max(lg, axis=0, keepdims=True).astype(F32))
            m_safe = jnp.where(m_new == NEG_INF, 0.0, m_new)
            alphas.append(jnp.exp2(m_old - m_safe))
            probs.append(jnp.exp2(lg - m_safe.astype(BF16)))
            m_sc[h] = m_new
        vb = vt_ref[jnp.minimum(jb, last_kv)]
        for h in range(ATTN_HEADS):
            vrows = slice(_VT_ROWS * h, _VT_ROWS * (h + 1))
            acc_sc[h] = alphas[h] * acc_sc[h] + _dot(vb[vrows, :], probs[h])

    def run(with_ties):
        m_sc[...] = jnp.full(m_sc.shape, NEG_INF, F32)
        acc_sc[...] = jnp.zeros_like(acc_sc)
        issue_qk(0, 0)

        def pair(j, thr0, thr1, cum0=None, cum1=None):
            issue_qk(2 * j + 1, 1)
            attend(2 * j, 0, thr0, cum0)
            issue_qk(2 * j + 2, 0)
            attend(2 * j + 1, 1, thr1, cum1)

        if not with_ties:
            def pairs_fn(js, carry):
                for j in js:
                    pair(j, lo, lo)
                return carry
            walk_pairs(pairs_fn, 0)
            return

        last_s = s_sc.shape[0] - 1

        def cand_count(jb):
            s = s_sc[jnp.minimum(jb, last_s)]
            cand = jnp.logical_and(s >= lo, s < hi)
            return jnp.sum(jnp.where(cand, 1.0, 0.0), axis=0, keepdims=True)

        def needs_ranking(cum, counts):
            runs_out = None
            for cnt in counts:
                here = jnp.logical_and(cum + cnt > budget, cum < budget)
                runs_out = here if runs_out is None else jnp.logical_or(runs_out, here)
                cum = cum + cnt
            return (jnp.max(jnp.where(runs_out, 1.0, 0.0)) > 0.5).astype(I32)

        def tie_group_body(n_blocks):
            def body(g, carry):
                ranked, cums = carry[0], [carry[1]]
                for cnt in carry[2:]:
                    cums.append(cums[-1] + cnt)
                base = n_blocks * g

                def variant(with_rank):
                    def fn():
                        ahead = [cand_count(base + n_blocks + r) for r in range(n_blocks)]
                        nxt = needs_ranking(cums[-1], ahead)
                        for p in range(n_blocks // 2):
                            if with_rank:
                                pair(base // 2 + p, None, None, cums[2 * p], cums[2 * p + 1])
                            else:
                                pair(base // 2 + p, jnp.where(cums[2 * p + 1] <= budget, lo, hi),
                                     jnp.where(cums[2 * p + 2] <= budget, lo, hi))
                        return (nxt, cums[-1], *ahead)
                    return fn
                return lax.cond(ranked == 1, variant(True), variant(False))
            return body

        zero = jnp.zeros((1, tq), F32)
        first = [cand_count(r) for r in range(4)]
        quads = lax.fori_loop(0, nquad, tie_group_body(4), (needs_ranking(zero, first), zero, *first))
        cum, ca, cb = quads[1], quads[2], quads[3]
        lax.fori_loop(2 * nquad, npair, tie_group_body(2), (needs_ranking(cum, [ca, cb]), cum, ca, cb))

    @pl.when(need_ties)
    def _():
        run(True)

    @pl.when(jnp.logical_not(need_ties))
    def _():
        run(False)

    hd = ATTN_HEAD_DIM
    normed = [acc_sc[h, 0:hd, :] / acc_sc[h, hd:hd + 1, :] for h in range(ATTN_HEADS)]
    for p in range(ATTN_HEADS // 2):
        pair_t = jnp.concatenate([normed[2 * p], normed[2 * p + 1]], axis=0)
        o_ref[:, LANES * p:LANES * (p + 1)] = jnp.transpose(pair_t)


def _dsa_call(rel_bias, qi, wi, q, ki2, k, vt, *, tq):
    s = q.shape[0]
    assert tq > REL_MAX_DIST and s % tq == 0
    nb = s // tq
    topk = min(TOPK_MAX, s // 4)
    kern = functools.partial(_dsa_kernel, tq=tq, topk=topk,
                             idx_scale=(IDX_DIM ** -0.5) * (IDX_HEADS ** -0.5))
    whole = lambda width: pl.BlockSpec((s, width), lambda i: (0, 0), pipeline_mode=pl.Buffered(1))
    assert vt.shape == (nb, ATTN_HEADS * _VT_ROWS, tq)
    return pl.pallas_call(
        kern,
        grid=(nb,),
        in_specs=[pl.BlockSpec(memory_space=pltpu.SMEM),
                  pl.BlockSpec((tq, IDX_HEADS * IDX_DIM), lambda i: (i, 0)),
                  pl.BlockSpec((tq, LANES), lambda i: (i, 0)),
                  pl.BlockSpec((tq, ATTN_WIDTH), lambda i: (i, 0)),
                  whole(2 * IDX_DIM), whole(ATTN_WIDTH),
                  pl.BlockSpec(vt.shape, lambda i: (0, 0, 0), pipeline_mode=pl.Buffered(1))],
        out_specs=pl.BlockSpec((tq, ATTN_WIDTH), lambda i: (i, 0)),
        out_shape=jax.ShapeDtypeStruct((s, ATTN_WIDTH), F32),
        scratch_shapes=[pltpu.VMEM((nb + nb % 2, tq, tq), F32),
                        pltpu.VMEM((nb + nb % 2, tq, tq), BF16),
                        pltpu.VMEM((ATTN_HEADS, _N_KINDS, tq, tq), BF16),
                        pltpu.VMEM((3, tq, tq), F32),
                        pltpu.VMEM((2, ATTN_HEADS, tq, tq), BF16),
                        pltpu.VMEM((IDX_HEADS, LANES, tq), BF16),
                        pltpu.VMEM((ATTN_HEADS, LANES, tq), BF16),
                        pltpu.VMEM((ATTN_HEADS, 1, tq), F32),
                        pltpu.VMEM((ATTN_HEADS, _VT_ROWS, tq), F32)],
        compiler_params=_cparams("arbitrary"),
        name="dsa_attention",
    )(rel_bias, qi, wi, q, ki2, k, vt)


_HGRN_CHUNK = 128
_HGRN_BASE = SUBLANES


def _hgrn_kernel(h4_ref, lb_ref, ng_ref, o_ref, st_sc, *, layer, rows):
    c_len = _HGRN_CHUNK

    @pl.when(pl.program_id(0) == 0)
    def _():
        st_sc[...] = jnp.zeros_like(st_sc)

    lb_all = lb_ref[...]
    e = jnp.exp(lb_all - jnp.max(lb_all, axis=0, keepdims=True))
    sm = e / jnp.sum(e, axis=0, keepdims=True)
    lbs = jnp.sum(sm[0:layer + 1, :], axis=0, keepdims=True) - sm[0:1, :]

    r1 = lax.broadcasted_iota(I32, (c_len, 1), 0)
    rr = lax.broadcasted_iota(I32, (c_len, c_len), 0)
    cc = lax.broadcasted_iota(I32, (c_len, c_len), 1)

    def block_ref(a, blk, r):
        x = a.reshape(c_len // blk, blk, a.shape[-1])[:, r:r + 1, :]
        return jnp.broadcast_to(x, (c_len // blk, blk, a.shape[-1])).reshape(c_len, a.shape[-1])

    w = HGRN_WIDTH
    for c in range(rows // c_len):
        rs = slice(c * c_len, (c + 1) * c_len)
        hq = h4_ref[rs, 0:w]
        z = h4_ref[rs, w:2 * w]
        val = h4_ref[rs, 2 * w:3 * w]
        hg = h4_ref[rs, 3 * w:4 * w]
        key = (1.0 - lbs) * jax.nn.sigmoid(-z)
        g = jnp.log1p(-jnp.minimum(key, KEY_MAX))
        a_cum = g
        sh = 1
        while sh < c_len:
            a_cum = a_cum + jnp.where(r1 >= sh, pltpu.roll(a_cum, sh, 0), 0.0)
            sh *= 2
        qf = _silu(hq)
        outs = []
        for h in range(HGRN_HEADS):
            hs = slice(h * HGRN_DIM, (h + 1) * HGRN_DIM)
            ah, qh, kh, vh = a_cum[:, hs], qf[:, hs], key[:, hs], val[:, hs]
            vh16 = vh.astype(BF16)
            st = st_sc[h]
            o = _dot_nt((qh * jnp.exp(ah)).astype(BF16), st.astype(BF16))
            ref = block_ref(ah, _HGRN_BASE, _HGRN_BASE // 2 - 1)
            sc = _dot_nt((qh * jnp.exp(ah - ref)).astype(BF16), (kh * jnp.exp(ref - ah)).astype(BF16))
            same = (rr >> _log2(_HGRN_BASE)) == (cc >> _log2(_HGRN_BASE))
            scores = jnp.where(jnp.logical_and(same, cc <= rr), sc, 0.0)
            half = _HGRN_BASE
            while half < c_len:
                blk = 2 * half
                ref = block_ref(ah, blk, half - 1)
                second = (r1 & (blk - 1)) >= half
                ql = jnp.where(second, qh * jnp.exp(jnp.minimum(ah - ref, 0.0)), 0.0)
                kl = jnp.where(second, 0.0, kh * jnp.exp(jnp.minimum(ref - ah, 0.0)))
                sc = _dot_nt(ql.astype(BF16), kl.astype(BF16))
                scores = scores + jnp.where((rr >> _log2(blk)) == (cc >> _log2(blk)), sc, 0.0)
                half = blk
            o = o + _dot(scores.astype(BF16), vh16)
            a_last = ah[c_len - 1:c_len, :]
            kd = (kh * jnp.exp(a_last - ah)).astype(BF16)
            st_sc[h] = st * jnp.exp(a_last) + _dot_tn(vh16, kd)
            o = o * lax.rsqrt(jnp.mean(o * o, axis=-1, keepdims=True) + RMS_EPS)
            outs.append(o)
        o_ref[rs, :] = jnp.concatenate(outs, axis=1) * ng_ref[...] * _silu(hg)


def _hgrn_call(h4, lb, ng, *, layer, rows):
    s = h4.shape[0]
    kern = functools.partial(_hgrn_kernel, layer=layer, rows=rows)
    return pl.pallas_call(
        kern,
        grid=(s // rows,),
        in_specs=[pl.BlockSpec((rows, 4 * HGRN_WIDTH), lambda i: (i, 0)),
                  pl.BlockSpec(lb.shape, lambda i: (0, 0)),
                  _slab(ng, layer)],
        out_specs=pl.BlockSpec((rows, HGRN_WIDTH), lambda i: (i, 0)),
        out_shape=jax.ShapeDtypeStruct((s, HGRN_WIDTH), F32),
        scratch_shapes=[pltpu.VMEM((HGRN_HEADS, HGRN_DIM, HGRN_DIM), F32)],
        compiler_params=_cparams("arbitrary"),
        name="hgrn2",
    )(h4, lb, ng)


_POOL_HALO = 16


def _merge_kernel(x_ref, mod_ref, a_ref, ah_ref, yb_ref, yc_ref, g_ref, pw_ref, ps_ref, wb_ref, wo_ref,
                  lng_ref, lnb_ref, fwi_ref, fwo_ref, o_ref, *, tm, tf, alpha):
    i = pl.program_id(0)
    d = x_ref.shape[1]
    _, _, gate = _mod_rows(mod_ref, 1)
    a = a_ref[...]
    halo = jnp.where(i == 0, 0.0, ah_ref[...])
    ext = jnp.concatenate([halo, a], axis=0)
    lane = lax.broadcasted_iota(I32, (tm, POOL_WIDTH), 1)
    grp = lane >> _log2(POOL_GROUP)
    pos1 = (i * tm + lax.broadcasted_iota(I32, (tm, POOL_WIDTH), 0) + 1).astype(F32)
    win = jnp.zeros((tm, POOL_WIDTH), F32)
    cnt = jnp.zeros((tm, POOL_WIDTH), F32)
    ssum, have = ext, 1
    for gi, wd in enumerate(POOL_WINDOWS):
        while have < wd:
            ssum = ssum + pltpu.roll(ssum, have, 0)
            have *= 2
        assert have == wd
        win = jnp.where(grp == gi, ssum[_POOL_HALO:, :], win)
        cnt = jnp.where(grp == gi, jnp.minimum(pos1, float(wd)), cnt)
    dlt = (win / cnt - a).astype(BF16)
    ya = _dot(dlt, pw_ref[...]) * ps_ref[...]
    za = _dot(ya.astype(BF16), wb_ref[0:POOL_WIDTH, :])
    zb = _dot(yb_ref[...].astype(BF16), wb_ref[POOL_WIDTH:POOL_WIDTH + ATTN_WIDTH, :])
    zc = _dot(yc_ref[...].astype(BF16), wb_ref[POOL_WIDTH + ATTN_WIDTH:, :])
    g0, g1, g2 = (jax.nn.sigmoid(g_ref[:, j * d:(j + 1) * d].astype(F32)) for j in range(3))
    merged = g0 * za + g1 * zb + g2 * zc
    y = _dot(merged.astype(BF16), wo_ref[...])
    z = alpha * x_ref[...] + (1.0 + gate) * y
    xm = _layer_norm(z, lng_ref[1:2, :], lnb_ref[1:2, :])
    o_ref[...] = _ffn_rows(xm, mod_ref, fwi_ref, fwo_ref, lng_ref, lnb_ref, sub=2, tf=tf, alpha=alpha)


def _merge_call(x, mod, a, yb, yc, gates, pw_bd, ps, wb, wo, ln_g, ln_b, ffn_wi, ffn_wo, *, layer, tm, tf, alpha):
    s, d = x.shape
    kern = functools.partial(_merge_kernel, tm=tm, tf=tf, alpha=alpha)
    full = lambda arr: _slab(arr, layer)
    hb = tm // _POOL_HALO
    return pl.pallas_call(
        kern,
        grid=(s // tm,),
        in_specs=[pl.BlockSpec((tm, d), lambda i: (i, 0)),
                  pl.BlockSpec((3 * N_SUB, d), lambda i: (0, 0)),
                  pl.BlockSpec((tm, POOL_WIDTH), lambda i: (i, 0)),
                  pl.BlockSpec((_POOL_HALO, POOL_WIDTH), lambda i: (jnp.maximum(i * hb - 1, 0), 0)),
                  pl.BlockSpec((tm, ATTN_WIDTH), lambda i: (i, 0)),
                  pl.BlockSpec((tm, HGRN_WIDTH), lambda i: (i, 0)),
                  pl.BlockSpec((tm, 3 * d), lambda i: (i, 0)),
                  full(pw_bd), full(ps), full(wb), full(wo), full(ln_g), full(ln_b),
                  _slab(ffn_wi, layer, 1), _slab(ffn_wo, layer, 1)],
        out_specs=pl.BlockSpec((tm, d), lambda i: (i, 0)),
        out_shape=jax.ShapeDtypeStruct((s, d), F32),
        compiler_params=_cparams("arbitrary"),
        name="merge_ffn",
    )(x, mod, a, a, yb, yc, gates, pw_bd, ps, wb, wo, ln_g, ln_b, ffn_wi, ffn_wo)


def _pool_blockdiag(pool_w):
    depth, n, g, _ = pool_w.shape
    eye = jnp.eye(n, dtype=pool_w.dtype)
    return jnp.einsum("lgij,gh->lgihj", pool_w, eye).reshape(depth, n * g, n * g)


def kernel(x, c, w_ada, b_ada, ln_g, ln_b, ffn_w_in, ffn_w_out, mix_w_in, pool_w, pool_scale, rel_bias,
           hgrn_lb, hgrn_norm_g, w_branch, w_out):
    batch, s, d = x.shape
    assert batch == 1, "kernels are written for a single sequence"
    depth = w_ada.shape[0]
    alpha = (2 * depth) ** 0.25
    tm_ffn = min(512, s)
    tf = 256
    tq = min(256, s)
    tm_proj = tq
    tm_merge = min(512, s)
    hgrn_rows = min(1024, s)

    ffn_wi16, ffn_wo16 = ffn_w_in.astype(BF16), ffn_w_out.astype(BF16)
    mix_w16 = _proj_weight(mix_w_in)
    pool_bd16 = _pool_blockdiag(pool_w).astype(BF16)
    wb16, wo16 = w_branch.astype(BF16), w_out.astype(BF16)
    pool_scale3 = pool_scale.reshape(depth, 1, -1)
    norm_g3 = hgrn_norm_g.reshape(depth, 1, -1)

    mods = _ada_call(c, w_ada, b_ada)
    xs = x.reshape(s, d)
    for l in range(depth):
        mod = mods[l].reshape(3 * N_SUB, d)
        xs = _ffn_call(xs, mod, ffn_wi16, ffn_wo16, ln_g, ln_b, layer=l, which=0, sub=0, alpha=alpha,
                       tm=tm_ffn, tf=tf)
        gates, a, h4, qi, q, k, vt, ki2, wi = _proj_call(xs, mod, mix_w16, layer=l, tm=tm_proj)
        yb = _dsa_call(rel_bias, qi, wi, q, ki2, k, vt, tq=tq)
        yc = _hgrn_call(h4, hgrn_lb, norm_g3, layer=l, rows=hgrn_rows)
        xs = _merge_call(xs, mod, a, yb, yc, gates, pool_bd16, pool_scale3, wb16, wo16, ln_g, ln_b,
                         ffn_wi16, ffn_wo16, layer=l, tm=tm_merge, tf=tf, alpha=alpha)
    return xs.reshape(batch, s, d)
```

```python
import functools
import math

import jax
import jax.numpy as jnp
from jax import lax
from jax.experimental import pallas as pl
from jax.experimental.pallas import tpu as pltpu

F32 = jnp.float32
BF16 = jnp.bfloat16
I32 = jnp.int32

POOL_WINDOWS = (2, 4, 8, 16)
POOL_GROUP = 64
POOL_WIDTH = POOL_GROUP * len(POOL_WINDOWS)
ATTN_HEADS = 4
ATTN_HEAD_DIM = 64
ATTN_WIDTH = ATTN_HEADS * ATTN_HEAD_DIM
IDX_HEADS = 8
IDX_DIM = 64
TOPK_MAX = 256
HGRN_HEADS = 4
HGRN_DIM = 128
HGRN_WIDTH = HGRN_HEADS * HGRN_DIM
KEY_MAX = 1.0 - 1e-6
REL_BUCKETS = 32
REL_MAX_DIST = 128
N_SUB = 3
LN_EPS = 1e-5
RMS_EPS = 1e-6

LANES = 128
SUBLANES = 8
VMEM_LIMIT_BYTES = 58 * 1024 * 1024

NEG_INF = float("-inf")
POS_INF = float("inf")


def _cparams(*sem):
    return pltpu.CompilerParams(dimension_semantics=sem, vmem_limit_bytes=VMEM_LIMIT_BYTES)


def _dot(a, b):
    return jnp.dot(a, b, preferred_element_type=F32)


def _dot_nt(a, b):
    return lax.dot_general(a, b, (((1,), (1,)), ((), ())), preferred_element_type=F32)


def _dot_tn(a, b):
    return lax.dot_general(a, b, (((0,), (0,)), ((), ())), preferred_element_type=F32)


def _log2(n):
    assert n & (n - 1) == 0
    return n.bit_length() - 1


def _silu(x):
    return x * jax.nn.sigmoid(x)


def _mod_rows(mod_ref, j):
    return mod_ref[3 * j:3 * j + 1, :], mod_ref[3 * j + 1:3 * j + 2, :], mod_ref[3 * j + 2:3 * j + 3, :]


def _layer_norm(z, g, b):
    mu = jnp.mean(z, axis=-1, keepdims=True)
    zc = z - mu
    var = jnp.mean(zc * zc, axis=-1, keepdims=True)
    return zc * lax.rsqrt(var + LN_EPS) * g + b


def _ada_kernel(c_ref, w_ref, b_ref, o_ref):
    @pl.when(pl.program_id(1) == 0)
    def _():
        o_ref[0] = b_ref[0]

    c = c_ref[...]
    o_ref[0] += jnp.sum(_silu(c) * w_ref[0], axis=0, keepdims=True)


def _ada_call(c, w_ada, b_ada):
    depth, d, n = w_ada.shape
    rows = LANES
    return pl.pallas_call(
        _ada_kernel,
        grid=(depth, d // rows),
        in_specs=[pl.BlockSpec((rows, 1), lambda l, k: (k, 0)),
                  pl.BlockSpec((1, rows, n), lambda l, k: (l, k, 0)),
                  pl.BlockSpec((1, 1, n), lambda l, k: (l, 0, 0))],
        out_specs=pl.BlockSpec((1, 1, n), lambda l, k: (l, 0, 0)),
        out_shape=jax.ShapeDtypeStruct((depth, 1, n), F32),
        compiler_params=_cparams("arbitrary", "arbitrary"),
        name="ada_mod",
    )(c.reshape(d, 1), w_ada, b_ada.reshape(depth, 1, n))


def _ffn_rows(x, mod_ref, wi_ref, wo_ref, lng_ref, lnb_ref, *, sub, tf, alpha, before_chunk=None):
    shift, scale, gate = _mod_rows(mod_ref, sub)
    ff = wo_ref.shape[0]
    h = (x * (1.0 + scale) + shift).astype(BF16)
    y = jnp.zeros(x.shape, F32)
    for c0 in range(0, ff, tf):
        if before_chunk is not None:
            before_chunk(c0 // tf)
        g = _dot(h, wi_ref[:, c0:c0 + tf])
        u = _dot(h, wi_ref[:, ff + c0:ff + c0 + tf])
        y = y + _dot((_silu(g) * u).astype(BF16), wo_ref[c0:c0 + tf, :])
    z = alpha * x + (0.5 * (1.0 + gate)) * y
    return _layer_norm(z, lng_ref[sub:sub + 1, :], lnb_ref[sub:sub + 1, :])


def _ffn_kernel(x_ref, mod_ref, wi_hbm, wo_hbm, lng_ref, lnb_ref, o_ref, wi_sc, wo_sc, sem,
                *, layer, which, sub, tf, alpha):
    ff = wo_sc.shape[0]
    kw = dict(sub=sub, tf=tf, alpha=alpha)

    def chunk_copies(ci):
        c0 = ci * tf
        return (pltpu.make_async_copy(wi_hbm.at[layer, which, :, c0:c0 + tf], wi_sc.at[:, c0:c0 + tf],
                                      sem.at[ci, 0]),
                pltpu.make_async_copy(wi_hbm.at[layer, which, :, ff + c0:ff + c0 + tf],
                                      wi_sc.at[:, ff + c0:ff + c0 + tf], sem.at[ci, 1]),
                pltpu.make_async_copy(wo_hbm.at[layer, which, c0:c0 + tf, :], wo_sc.at[c0:c0 + tf, :],
                                      sem.at[ci, 2]))

    @pl.when(pl.program_id(0) == 0)
    def _():
        for ci in range(ff // tf):
            for cp in chunk_copies(ci):
                cp.start()

        def wait_chunk(ci):
            for cp in chunk_copies(ci):
                cp.wait()
        o_ref[...] = _ffn_rows(x_ref[...], mod_ref, wi_sc, wo_sc, lng_ref, lnb_ref, before_chunk=wait_chunk, **kw)

    @pl.when(pl.program_id(0) != 0)
    def _():
        o_ref[...] = _ffn_rows(x_ref[...], mod_ref, wi_sc, wo_sc, lng_ref, lnb_ref, **kw)


def _ffn_call(x, mod, w_in, w_out, ln_g, ln_b, *, layer, which, sub, alpha, tm, tf):
    s, d = x.shape
    ff = w_out.shape[2]
    assert ff % tf == 0
    kern = functools.partial(_ffn_kernel, layer=layer, which=which, sub=sub, tf=tf, alpha=alpha)
    return pl.pallas_call(
        kern,
        grid=(s // tm,),
        in_specs=[pl.BlockSpec((tm, d), lambda i: (i, 0)),
                  pl.BlockSpec((3 * N_SUB, d), lambda i: (0, 0)),
                  pl.BlockSpec(memory_space=pl.ANY), pl.BlockSpec(memory_space=pl.ANY),
                  _slab(ln_g, layer), _slab(ln_b, layer)],
        out_specs=pl.BlockSpec((tm, d), lambda i: (i, 0)),
        out_shape=jax.ShapeDtypeStruct((s, d), F32),
        scratch_shapes=[pltpu.VMEM((d, 2 * ff), BF16), pltpu.VMEM((ff, d), BF16),
                        pltpu.SemaphoreType.DMA((ff // tf, 3))],
        compiler_params=_cparams("arbitrary"),
        name="ffn",
    )(x, mod, w_in, w_out, ln_g, ln_b)


_PROJ_OUTS = (("gates", 3 * 1024, BF16), ("a", POOL_WIDTH, F32), ("h4", 4 * HGRN_WIDTH, F32),
              ("qi", IDX_HEADS * IDX_DIM, BF16), ("q", ATTN_WIDTH, BF16), ("k", ATTN_WIDTH, BF16),
              ("v", ATTN_WIDTH, BF16), ("ki2", 2 * IDX_DIM, BF16), ("wi", LANES, F32))
_PROJ_CHUNK = 1024


def _proj_kernel(x_ref, mod_ref, w_ref, *o_refs):
    shift, scale, _ = _mod_rows(mod_ref, 1)
    h = (x_ref[...] * (1.0 + scale) + shift).astype(BF16)
    c0 = 0
    for (name, width, dt), o_ref in zip(_PROJ_OUTS, o_refs):
        if name == "v":
            vt = lax.dot_general(w_ref[:, c0:c0 + width], h, (((0,), (1,)), ((), ())),
                                 preferred_element_type=F32)
            ones = jnp.ones((_VT_ROWS - ATTN_HEAD_DIM, vt.shape[1]), F32)
            rows = []
            for p0 in range(0, width, ATTN_HEAD_DIM):
                rows += [vt[p0:p0 + ATTN_HEAD_DIM, :], ones]
            o_ref[0] = jnp.concatenate(rows, axis=0).astype(dt)
        else:
            for cc in range(0, width, _PROJ_CHUNK):
                cw = min(_PROJ_CHUNK, width - cc)
                o_ref[:, cc:cc + cw] = _dot(h, w_ref[:, c0 + cc:c0 + cc + cw]).astype(dt)
        c0 += width


def _proj_weight_kernel(w_ref, o_ref):
    out_width = {name: wd for name, wd, _ in _PROJ_OUTS}
    src = {}
    c0 = 0
    for name, wd in (("a", POOL_WIDTH), ("q", ATTN_WIDTH), ("k", ATTN_WIDTH), ("v", ATTN_WIDTH),
                     ("qi", IDX_HEADS * IDX_DIM), ("ki", IDX_DIM), ("wi", IDX_HEADS),
                     ("h4", 4 * HGRN_WIDTH), ("gates", out_width["gates"])):
        src[name] = (c0, wd)
        c0 += wd
    assert c0 == w_ref.shape[-1]
    q_scale = ATTN_HEAD_DIM ** -0.5 * _LOG2E
    dst = 0
    for name, width, _ in _PROJ_OUTS:
        if name == "ki2":
            pieces = [("ki", 1.0), ("ki", 1.0)]
        elif name == "wi":
            o_ref[:, dst:dst + width] = jnp.zeros((o_ref.shape[0], width), BF16)
            pieces = [("wi", 1.0)]
        else:
            pieces = [(name, q_scale if name == "q" else 1.0)]
        off = dst
        for pname, scale in pieces:
            s0, wd = src[pname]
            o_ref[:, off:off + wd] = (w_ref[:, s0:s0 + wd] * scale).astype(BF16)
            off += wd
        dst += width


def _proj_weight(w):
    depth, d, n_in = w.shape
    n_out = sum(wd for _, wd, _ in _PROJ_OUTS)
    rows = 256
    return pl.pallas_call(
        _proj_weight_kernel,
        grid=(depth, d // rows),
        in_specs=[pl.BlockSpec((None, rows, n_in), lambda l, r: (l, r, 0))],
        out_specs=pl.BlockSpec((None, rows, n_out), lambda l, r: (l, r, 0)),
        out_shape=jax.ShapeDtypeStruct((depth, d, n_out), BF16),
        compiler_params=_cparams("arbitrary", "arbitrary"),
        name="proj_weight",
    )(w)


def _slab(arr, *lead):
    block = (None,) * len(lead) + tuple(arr.shape[len(lead):])
    index = tuple(lead) + (0,) * (arr.ndim - len(lead))
    return pl.BlockSpec(block, lambda i: index, pipeline_mode=pl.Buffered(1))


def _proj_call(x, mod, w_all, *, layer, tm):
    s, d = x.shape
    n = w_all.shape[-1]
    assert n == sum(wd for _, wd, _ in _PROJ_OUTS)
    n_vt = ATTN_HEADS * _VT_ROWS
    out_specs = [pl.BlockSpec((1, n_vt, tm), lambda i: (i, 0, 0)) if name == "v"
                 else pl.BlockSpec((tm, wd), lambda i: (i, 0)) for name, wd, _ in _PROJ_OUTS]
    out_shape = [jax.ShapeDtypeStruct((s // tm, n_vt, tm) if name == "v" else (s, wd), dt)
                 for name, wd, dt in _PROJ_OUTS]
    return pl.pallas_call(
        _proj_kernel,
        grid=(s // tm,),
        in_specs=[pl.BlockSpec((tm, d), lambda i: (i, 0)),
                  pl.BlockSpec((3 * N_SUB, d), lambda i: (0, 0)),
                  _slab(w_all, layer)],
        out_specs=out_specs,
        out_shape=out_shape,
        compiler_params=_cparams("arbitrary"),
        name="mix_proj",
    )(x, mod, w_all)


_BISECT_MAX_ITERS = 320
_F32_TINY = float(jnp.finfo(jnp.float32).tiny)
_COARSE_PASSES = 9
_COARSE_MIN_ABS, _COARSE_MAX_ABS = 1e-30, 1e30
_KIND_FAR, _KIND_PAD, _N_KINDS = 2, 3, 4
_VT_ROWS = ATTN_HEAD_DIM + 16
_SMASK_FULL, _SMASK_DIAG, _SMASK_PAD = 0, 1, 2
_LOG2E = math.log2(math.e)


def _t5_bucket(dist):
    max_exact = REL_BUCKETS // 2
    d32 = jnp.maximum(dist, 1).astype(F32)
    large = max_exact + (jnp.log(d32 / max_exact) / math.log(REL_MAX_DIST / max_exact)
                         * (REL_BUCKETS - max_exact)).astype(I32)
    large = jnp.minimum(large, REL_BUCKETS - 1)
    return jnp.where(dist < max_exact, dist, large)


def _dsa_kernel(relb_ref, qi_ref, wi_ref, q_ref, ki2_ref, k_ref, vt_ref, o_ref,
                s_sc, sb_sc, bias_sc, smask_sc, lg_sc, qim_sc, qm_sc, m_sc, acc_sc,
                *, tq, topk, idx_scale):
    tk = tq
    i = pl.program_id(0)
    krow = lax.broadcasted_iota(I32, (tk, tq), 0)
    qcol = lax.broadcasted_iota(I32, (tk, tq), 1)
    causal = krow <= qcol
    lane = lax.broadcasted_iota(I32, (tq, LANES), 1)
    lo_half = lane < ATTN_HEAD_DIM

    @pl.when(i == 0)
    def _():
        for which in range(2):
            bucket = _t5_bucket(jnp.maximum(qcol - krow + which * tq, 0))
            for h in range(ATTN_HEADS):
                b = jnp.zeros((tk, tq), F32)
                for bk in range(REL_BUCKETS):
                    b = jnp.where(bucket == bk, relb_ref[bk, h] * _LOG2E, b)
                bias_sc[h, which] = (jnp.where(causal, b, NEG_INF) if which == 0 else b).astype(BF16)
        for h in range(ATTN_HEADS):
            bias_sc[h, _KIND_FAR] = jnp.full((tk, tq), relb_ref[REL_BUCKETS - 1, h] * _LOG2E, F32).astype(BF16)
            bias_sc[h, _KIND_PAD] = jnp.full((tk, tq), NEG_INF, BF16)
        smask_sc[_SMASK_FULL] = jnp.zeros((tk, tq), F32)
        smask_sc[_SMASK_DIAG] = jnp.where(causal, 0.0, NEG_INF)
        smask_sc[_SMASK_PAD] = jnp.full((tk, tq), NEG_INF, F32)

    def head_operands(src_ref, dst_sc, n_heads):
        src = src_ref[...].astype(F32)
        for h in range(n_heads):
            pair = src[:, LANES * (h // 2):LANES * (h // 2 + 1)]
            pair = jnp.where(lo_half if h % 2 == 0 else ~lo_half, pair, 0.0)
            dst_sc[h] = jnp.transpose(pair).astype(BF16)

    head_operands(qi_ref, qim_sc, IDX_HEADS)
    head_operands(q_ref, qm_sc, ATTN_HEADS)

    wt = jnp.transpose(wi_ref[...])

    def key_rows(jb):
        return pl.ds(pl.multiple_of(jb * tk, tk), tk)

    def fold(x):
        return x.reshape(tk // SUBLANES, SUBLANES, tq)

    def index_scores(jb):
        kb = ki2_ref[key_rows(jb), :]
        acc = jnp.zeros((tk, tq), F32)
        for h in range(IDX_HEADS):
            acc = acc + wt[h:h + 1, :] * jnp.maximum(_dot(kb, qim_sc[h]), 0.0)
        return acc * idx_scale

    npair = (i + 2) // 2
    last_kv = k_ref.shape[0] // tk - 1

    def p1_blocks(jbs, carry):
        rmin, rmax = carry
        for jb in jbs:
            raw = index_scores(jnp.minimum(jb, last_kv))
            kind = jnp.where(jb > i, _SMASK_PAD, jnp.where(jb == i, _SMASK_DIAG, _SMASK_FULL))
            sc = raw + smask_sc[kind]
            s_sc[jb] = sc
            sb_sc[jb] = sc.astype(BF16)
            rmin = jnp.minimum(rmin, jnp.min(fold(raw), axis=0))
            rmax = jnp.maximum(rmax, jnp.max(fold(raw), axis=0))
        return rmin, rmax

    nquad = npair // 2

    def walk_pairs(pairs_fn, init):
        n4 = npair // 4
        carry = lax.fori_loop(0, n4, lambda t, c: pairs_fn([4 * t + r for r in range(4)], c), init)
        n2 = (npair - 4 * n4) // 2
        carry = lax.fori_loop(2 * n4, 2 * n4 + n2, lambda t, c: pairs_fn([2 * t, 2 * t + 1], c), carry)
        return lax.fori_loop(4 * n4 + 2 * n2, npair, lambda t, c: pairs_fn([t], c), carry)

    rmin, rmax = walk_pairs(lambda js, c: p1_blocks([2 * j + r for j in js for r in range(2)], c),
                            (jnp.full((SUBLANES, tq), POS_INF, F32), jnp.full((SUBLANES, tq), NEG_INF, F32)))
    rowmin = jnp.min(rmin, axis=0, keepdims=True)
    rowmax = jnp.max(rmax, axis=0, keepdims=True)

    kf = float(topk)
    nfin = (i * tq + lax.broadcasted_iota(I32, (1, tq), 1) + 1).astype(F32)
    small = nfin <= kf

    def walk_blocks(blocks_fn, init):
        return lax.fori_loop(0, npair, lambda j, c: blocks_fn([2 * j, 2 * j + 1], c), init)

    def count_ge(x):
        def blocks_fn(jbs, cnt):
            for jb in jbs:
                cnt = cnt + jnp.sum(fold(jnp.where(s_sc[jb] >= x, 1.0, 0.0)), axis=0)
            return cnt
        return jnp.sum(walk_blocks(blocks_fn, jnp.zeros((SUBLANES, tq), F32)), axis=0, keepdims=True)

    def count_ge16(xb):
        one, zero = jnp.ones((), BF16), jnp.zeros((), BF16)
        rows16 = 2 * SUBLANES

        def blocks_fn(jbs, cnt):
            parts = []
            for jb in jbs:
                c = jnp.where(sb_sc[jb] >= xb, one, zero)
                parts += [c[r0:r0 + rows16, :] for r0 in range(0, tk, rows16)]
            while len(parts) > 1:
                parts = [a + b for a, b in zip(parts[0::2], parts[1::2])]
            return cnt + parts[0].astype(F32)
        return jnp.sum(walk_blocks(blocks_fn, jnp.zeros((rows16, tq), F32)), axis=0, keepdims=True)

    def coarse_probe(mid):
        xb = mid.astype(BF16)
        xbits = lax.bitcast_convert_type(xb.astype(F32), I32)
        neg = xbits < 0
        mu = jnp.where(neg, xbits + 0x8000, xbits - 0x8000)
        odd = (xbits & 0x10000) != 0
        tbits = jnp.where(odd, jnp.where(neg, mu - 1, mu + 1), mu)
        t = lax.bitcast_convert_type(tbits, F32)
        ok = jnp.logical_and(jnp.abs(xb.astype(F32)) > _COARSE_MIN_ABS, jnp.abs(xb.astype(F32)) < _COARSE_MAX_ABS)
        return xb, t, ok

    def narrow(st, probe, c, active):
        lo, hi, clo, chi, done = st
        up = jnp.logical_and(active, c >= kf)
        dn = jnp.logical_and(active, c < kf)
        return (jnp.where(up, probe, lo), jnp.where(dn, probe, hi), jnp.where(up, c, clo),
                jnp.where(dn, c, chi), jnp.where(jnp.logical_and(active, c == kf), 1.0, done))

    def coarse_body(_, st):
        lo, hi, _, _, done = st
        xb, t, ok = coarse_probe(0.5 * lo + 0.5 * hi)
        ok = jnp.logical_and(ok, jnp.logical_and(t > lo, t < hi))
        return narrow(st, t, count_ge16(xb), jnp.logical_and(done < 0.5, ok))

    def fine_cond(st):
        it, (_, _, _, _, done) = st
        return jnp.logical_and(it < _BISECT_MAX_ITERS, jnp.min(done) < 0.5)

    def fine_body(st):
        it, inner = st
        lo, hi, _, _, done = inner
        mid = 0.5 * lo + 0.5 * hi
        mid = jnp.where(jnp.logical_and(lo == 0.0, hi > _F32_TINY), _F32_TINY, mid)
        mid = jnp.where(jnp.logical_and(lo < 0.0, hi > 0.0), 0.0, mid)
        stuck = jnp.logical_or(jnp.logical_or(mid <= lo, mid >= hi),
                               jnp.logical_and(lo == 0.0, hi <= _F32_TINY))
        lo, hi, clo, chi, done = narrow(inner, mid, count_ge(mid),
                                        jnp.logical_and(done < 0.5, jnp.logical_not(stuck)))
        return it + 1, (lo, hi, clo, chi, jnp.where(stuck, 1.0, done))

    mbits = lax.bitcast_convert_type(rowmax, I32)
    above_max = jnp.where(rowmax == 0.0, _F32_TINY,
                          lax.bitcast_convert_type(jnp.where(rowmax > 0.0, mbits + 1, mbits - 1), F32))
    st0 = (jnp.where(small, NEG_INF, rowmin), above_max, nfin, jnp.zeros((1, tq), F32),
           jnp.where(small, 1.0, 0.0))
    st1 = lax.fori_loop(0, _COARSE_PASSES, coarse_body, st0)
    _, (lo, hi, clo, chi, _) = lax.while_loop(fine_cond, fine_body, (jnp.int32(0), st1))
    budget = kf - chi
    need_ties = jnp.max((clo - chi) - budget) > 0.5

    pairs = [slice(LANES * (h // 2), LANES * (h // 2 + 1)) for h in range(ATTN_HEADS)]

    def issue_qk(jb, slot):
        kb = k_ref[key_rows(jnp.minimum(jb, last_kv)), :]
        for h in range(ATTN_HEADS):
            lg_sc[slot, h] = _dot(kb[:, pairs[h]], qm_sc[h]).astype(BF16)

    def attend(jb, slot, thr, cum=None):
        s = s_sc[jb]
        if cum is None:
            sel = s >= thr
        else:
            cand = jnp.logical_and(s >= lo, s < hi)
            before = (qcol < krow).astype(BF16)
            rank = cum + _dot(before, jnp.where(cand, 1.0, 0.0).astype(BF16))
            sel = jnp.logical_or(s >= hi, jnp.logical_and(cand, rank < budget))
        kind = jnp.where(jb > i, _KIND_PAD, jnp.minimum(i - jb, _KIND_FAR))
        selm = jnp.where(sel, 0.0, NEG_INF).astype(BF16)
        probs, alphas = [], []
        for h in range(ATTN_HEADS):
            lg = lg_sc[slot, h] + bias_sc[h, kind] + selm
            m_old = m_sc[h]
            m_new = jnp.maximum(m_old, jnp.max(lg, axis=0, keepdims=True).astype(F32))
            m_safe = jnp.where(m_new == NEG_INF, 0.0, m_new)
            alphas.append(jnp.exp2(m_old - m_safe))
            probs.append(jnp.exp2(lg - m_safe.astype(BF16)))
            m_sc[h] = m_new
        vb = vt_ref[jnp.minimum(jb, last_kv)]
        for h in range(ATTN_HEADS):
            vrows = slice(_VT_ROWS * h, _VT_ROWS * (h + 1))
            acc_sc[h] = alphas[h] * acc_sc[h] + _dot(vb[vrows, :], probs[h])

    def run(with_ties):
        m_sc[...] = jnp.full(m_sc.shape, NEG_INF, F32)
        acc_sc[...] = jnp.zeros_like(acc_sc)
        issue_qk(0, 0)

        def pair(j, thr0, thr1, cum0=None, cum1=None):
            issue_qk(2 * j + 1, 1)
            attend(2 * j, 0, thr0, cum0)
            issue_qk(2 * j + 2, 0)
            attend(2 * j + 1, 1, thr1, cum1)

        if not with_ties:
            def pairs_fn(js, carry):
                for j in js:
                    pair(j, lo, lo)
                return carry
            walk_pairs(pairs_fn, 0)
            return

        last_s = s_sc.shape[0] - 1

        def cand_count(jb):
            s = s_sc[jnp.minimum(jb, last_s)]
            cand = jnp.logical_and(s >= lo, s < hi)
            return jnp.sum(jnp.where(cand, 1.0, 0.0), axis=0, keepdims=True)

        def needs_ranking(cum, counts):
            runs_out = None
            for cnt in counts:
                here = jnp.logical_and(cum + cnt > budget, cum < budget)
                runs_out = here if runs_out is None else jnp.logical_or(runs_out, here)
                cum = cum + cnt
            return (jnp.max(jnp.where(runs_out, 1.0, 0.0)) > 0.5).astype(I32)

        def tie_group_body(n_blocks):
            def body(g, carry):
                ranked, cums = carry[0], [carry[1]]
                for cnt in carry[2:]:
                    cums.append(cums[-1] + cnt)
                base = n_blocks * g

                def variant(with_rank):
                    def fn():
                        ahead = [cand_count(base + n_blocks + r) for r in range(n_blocks)]
                        nxt = needs_ranking(cums[-1], ahead)
                        for p in range(n_blocks // 2):
                            if with_rank:
                                pair(base // 2 + p, None, None, cums[2 * p], cums[2 * p + 1])
                            else:
                                pair(base // 2 + p, jnp.where(cums[2 * p + 1] <= budget, lo, hi),
                                     jnp.where(cums[2 * p + 2] <= budget, lo, hi))
                        return (nxt, cums[-1], *ahead)
                    return fn
                return lax.cond(ranked == 1, variant(True), variant(False))
            return body

        zero = jnp.zeros((1, tq), F32)
        first = [cand_count(r) for r in range(4)]
        quads = lax.fori_loop(0, nquad, tie_group_body(4), (needs_ranking(zero, first), zero, *first))
        cum, ca, cb = quads[1], quads[2], quads[3]
        lax.fori_loop(2 * nquad, npair, tie_group_body(2), (needs_ranking(cum, [ca, cb]), cum, ca, cb))

    @pl.when(need_ties)
    def _():
        run(True)

    @pl.when(jnp.logical_not(need_ties))
    def _():
        run(False)

    hd = ATTN_HEAD_DIM
    normed = [acc_sc[h, 0:hd, :] / acc_sc[h, hd:hd + 1, :] for h in range(ATTN_HEADS)]
    for p in range(ATTN_HEADS // 2):
        pair_t = jnp.concatenate([normed[2 * p], normed[2 * p + 1]], axis=0)
        o_ref[:, LANES * p:LANES * (p + 1)] = jnp.transpose(pair_t)


def _dsa_call(rel_bias, qi, wi, q, ki2, k, vt, *, tq):
    s = q.shape[0]
    assert tq > REL_MAX_DIST and s % tq == 0
    nb = s // tq
    topk = min(TOPK_MAX, s // 4)
    kern = functools.partial(_dsa_kernel, tq=tq, topk=topk,
                             idx_scale=(IDX_DIM ** -0.5) * (IDX_HEADS ** -0.5))
    whole = lambda width: pl.BlockSpec((s, width), lambda i: (0, 0), pipeline_mode=pl.Buffered(1))
    assert vt.shape == (nb, ATTN_HEADS * _VT_ROWS, tq)
    return pl.pallas_call(
        kern,
        grid=(nb,),
        in_specs=[pl.BlockSpec(memory_space=pltpu.SMEM),
                  pl.BlockSpec((tq, IDX_HEADS * IDX_DIM), lambda i: (i, 0)),
                  pl.BlockSpec((tq, LANES), lambda i: (i, 0)),
                  pl.BlockSpec((tq, ATTN_WIDTH), lambda i: (i, 0)),
                  whole(2 * IDX_DIM), whole(ATTN_WIDTH),
                  pl.BlockSpec(vt.shape, lambda i: (0, 0, 0), pipeline_mode=pl.Buffered(1))],
        out_specs=pl.BlockSpec((tq, ATTN_WIDTH), lambda i: (i, 0)),
        out_shape=jax.ShapeDtypeStruct((s, ATTN_WIDTH), F32),
        scratch_shapes=[pltpu.VMEM((nb + nb % 2, tq, tq), F32),
                        pltpu.VMEM((nb + nb % 2, tq, tq), BF16),
                        pltpu.VMEM((ATTN_HEADS, _N_KINDS, tq, tq), BF16),
                        pltpu.VMEM((3, tq, tq), F32),
                        pltpu.VMEM((2, ATTN_HEADS, tq, tq), BF16),
                        pltpu.VMEM((IDX_HEADS, LANES, tq), BF16),
                        pltpu.VMEM((ATTN_HEADS, LANES, tq), BF16),
                        pltpu.VMEM((ATTN_HEADS, 1, tq), F32),
                        pltpu.VMEM((ATTN_HEADS, _VT_ROWS, tq), F32)],
        compiler_params=_cparams("arbitrary"),
        name="dsa_attention",
    )(rel_bias, qi, wi, q, ki2, k, vt)


_HGRN_CHUNK = 128
_HGRN_BASE = SUBLANES


def _hgrn_kernel(h4_ref, lb_ref, ng_ref, o_ref, st_sc, *, layer, rows):
    c_len = _HGRN_CHUNK

    @pl.when(pl.program_id(0) == 0)
    def _():
        st_sc[...] = jnp.zeros_like(st_sc)

    lb_all = lb_ref[...]
    e = jnp.exp(lb_all - jnp.max(lb_all, axis=0, keepdims=True))
    sm = e / jnp.sum(e, axis=0, keepdims=True)
    lbs = jnp.sum(sm[0:layer + 1, :], axis=0, keepdims=True) - sm[0:1, :]

    r1 = lax.broadcasted_iota(I32, (c_len, 1), 0)
    rr = lax.broadcasted_iota(I32, (c_len, c_len), 0)
    cc = lax.broadcasted_iota(I32, (c_len, c_len), 1)

    def block_ref(a, blk, r):
        x = a.reshape(c_len // blk, blk, a.shape[-1])[:, r:r + 1, :]
        return jnp.broadcast_to(x, (c_len // blk, blk, a.shape[-1])).reshape(c_len, a.shape[-1])

    w = HGRN_WIDTH
    for c in range(rows // c_len):
        rs = slice(c * c_len, (c + 1) * c_len)
        hq = h4_ref[rs, 0:w]
        z = h4_ref[rs, w:2 * w]
        val = h4_ref[rs, 2 * w:3 * w]
        hg = h4_ref[rs, 3 * w:4 * w]
        key = (1.0 - lbs) * jax.nn.sigmoid(-z)
        g = jnp.log1p(-jnp.minimum(key, KEY_MAX))
        a_cum = g
        sh = 1
        while sh < c_len:
            a_cum = a_cum + jnp.where(r1 >= sh, pltpu.roll(a_cum, sh, 0), 0.0)
            sh *= 2
        qf = _silu(hq)
        outs = []
        for h in range(HGRN_HEADS):
            hs = slice(h * HGRN_DIM, (h + 1) * HGRN_DIM)
            ah, qh, kh, vh = a_cum[:, hs], qf[:, hs], key[:, hs], val[:, hs]
            vh16 = vh.astype(BF16)
            st = st_sc[h]
            o = _dot_nt((qh * jnp.exp(ah)).astype(BF16), st.astype(BF16))
            ref = block_ref(ah, _HGRN_BASE, _HGRN_BASE // 2 - 1)
            sc = _dot_nt((qh * jnp.exp(ah - ref)).astype(BF16), (kh * jnp.exp(ref - ah)).astype(BF16))
            same = (rr >> _log2(_HGRN_BASE)) == (cc >> _log2(_HGRN_BASE))
            scores = jnp.where(jnp.logical_and(same, cc <= rr), sc, 0.0)
            half = _HGRN_BASE
            while half < c_len:
                blk = 2 * half
                ref = block_ref(ah, blk, half - 1)
                second = (r1 & (blk - 1)) >= half
                ql = jnp.where(second, qh * jnp.exp(jnp.minimum(ah - ref, 0.0)), 0.0)
                kl = jnp.where(second, 0.0, kh * jnp.exp(jnp.minimum(ref - ah, 0.0)))
                sc = _dot_nt(ql.astype(BF16), kl.astype(BF16))
                scores = scores + jnp.where((rr >> _log2(blk)) == (cc >> _log2(blk)), sc, 0.0)
                half = blk
            o = o + _dot(scores.astype(BF16), vh16)
            a_last = ah[c_len - 1:c_len, :]
            kd = (kh * jnp.exp(a_last - ah)).astype(BF16)
            st_sc[h] = st * jnp.exp(a_last) + _dot_tn(vh16, kd)
            o = o * lax.rsqrt(jnp.mean(o * o, axis=-1, keepdims=True) + RMS_EPS)
            outs.append(o)
        o_ref[rs, :] = jnp.concatenate(outs, axis=1) * ng_ref[...] * _silu(hg)


def _hgrn_call(h4, lb, ng, *, layer, rows):
    s = h4.shape[0]
    kern = functools.partial(_hgrn_kernel, layer=layer, rows=rows)
    return pl.pallas_call(
        kern,
        grid=(s // rows,),
        in_specs=[pl.BlockSpec((rows, 4 * HGRN_WIDTH), lambda i: (i, 0)),
                  pl.BlockSpec(lb.shape, lambda i: (0, 0)),
                  _slab(ng, layer)],
        out_specs=pl.BlockSpec((rows, HGRN_WIDTH), lambda i: (i, 0)),
        out_shape=jax.ShapeDtypeStruct((s, HGRN_WIDTH), F32),
        scratch_shapes=[pltpu.VMEM((HGRN_HEADS, HGRN_DIM, HGRN_DIM), F32)],
        compiler_params=_cparams("arbitrary"),
        name="hgrn2",
    )(h4, lb, ng)


_POOL_HALO = 16


def _merge_kernel(x_ref, mod_ref, a_ref, ah_ref, yb_ref, yc_ref, g_ref, pw_ref, ps_ref, wb_ref, wo_ref,
                  lng_ref, lnb_ref, fwi_ref, fwo_ref, o_ref, *, tm, tf, alpha):
    i = pl.program_id(0)
    d = x_ref.shape[1]
    _, _, gate = _mod_rows(mod_ref, 1)
    a = a_ref[...]
    halo = jnp.where(i == 0, 0.0, ah_ref[...])
    ext = jnp.concatenate([halo, a], axis=0)
    lane = lax.broadcasted_iota(I32, (tm, POOL_WIDTH), 1)
    grp = lane >> _log2(POOL_GROUP)
    pos1 = (i * tm + lax.broadcasted_iota(I32, (tm, POOL_WIDTH), 0) + 1).astype(F32)
    win = jnp.zeros((tm, POOL_WIDTH), F32)
    cnt = jnp.zeros((tm, POOL_WIDTH), F32)
    ssum, have = ext, 1
    for gi, wd in enumerate(POOL_WINDOWS):
        while have < wd:
            ssum = ssum + pltpu.roll(ssum, have, 0)
            have *= 2
        assert have == wd
        win = jnp.where(grp == gi, ssum[_POOL_HALO:, :], win)
        cnt = jnp.where(grp == gi, jnp.minimum(pos1, float(wd)), cnt)
    dlt = (win / cnt - a).astype(BF16)
    ya = _dot(dlt, pw_ref[...]) * ps_ref[...]
    za = _dot(ya.astype(BF16), wb_ref[0:POOL_WIDTH, :])
    zb = _dot(yb_ref[...].astype(BF16), wb_ref[POOL_WIDTH:POOL_WIDTH + ATTN_WIDTH, :])
    zc = _dot(yc_ref[...].astype(BF16), wb_ref[POOL_WIDTH + ATTN_WIDTH:, :])
    g0, g1, g2 = (jax.nn.sigmoid(g_ref[:, j * d:(j + 1) * d].astype(F32)) for j in range(3))
    merged = g0 * za + g1 * zb + g2 * zc
    y = _dot(merged.astype(BF16), wo_ref[...])
    z = alpha * x_ref[...] + (1.0 + gate) * y
    xm = _layer_norm(z, lng_ref[1:2, :], lnb_ref[1:2, :])
    o_ref[...] = _ffn_rows(xm, mod_ref, fwi_ref, fwo_ref, lng_ref, lnb_ref, sub=2, tf=tf, alpha=alpha)


def _merge_call(x, mod, a, yb, yc, gates, pw_bd, ps, wb, wo, ln_g, ln_b, ffn_wi, ffn_wo, *, layer, tm, tf, alpha):
    s, d = x.shape
    kern = functools.partial(_merge_kernel, tm=tm, tf=tf, alpha=alpha)
    full = lambda arr: _slab(arr, layer)
    hb = tm // _POOL_HALO
    return pl.pallas_call(
        kern,
        grid=(s // tm,),
        in_specs=[pl.BlockSpec((tm, d), lambda i: (i, 0)),
                  pl.BlockSpec((3 * N_SUB, d), lambda i: (0, 0)),
                  pl.BlockSpec((tm, POOL_WIDTH), lambda i: (i, 0)),
                  pl.BlockSpec((_POOL_HALO, POOL_WIDTH), lambda i: (jnp.maximum(i * hb - 1, 0), 0)),
                  pl.BlockSpec((tm, ATTN_WIDTH), lambda i: (i, 0)),
                  pl.BlockSpec((tm, HGRN_WIDTH), lambda i: (i, 0)),
                  pl.BlockSpec((tm, 3 * d), lambda i: (i, 0)),
                  full(pw_bd), full(ps), full(wb), full(wo), full(ln_g), full(ln_b),
                  _slab(ffn_wi, layer, 1), _slab(ffn_wo, layer, 1)],
        out_specs=pl.BlockSpec((tm, d), lambda i: (i, 0)),
        out_shape=jax.ShapeDtypeStruct((s, d), F32),
        compiler_params=_cparams("arbitrary"),
        name="merge_ffn",
    )(x, mod, a, a, yb, yc, gates, pw_bd, ps, wb, wo, ln_g, ln_b, ffn_wi, ffn_wo)


def _pool_blockdiag(pool_w):
    depth, n, g, _ = pool_w.shape
    eye = jnp.eye(n, dtype=pool_w.dtype)
    return jnp.einsum("lgij,gh->lgihj", pool_w, eye).reshape(depth, n * g, n * g)


def kernel(x, c, w_ada, b_ada, ln_g, ln_b, ffn_w_in, ffn_w_out, mix_w_in, pool_w, pool_scale, rel_bias,
           hgrn_lb, hgrn_norm_g, w_branch, w_out):
    batch, s, d = x.shape
    assert batch == 1, "kernels are written for a single sequence"
    depth = w_ada.shape[0]
    alpha = (2 * depth) ** 0.25
    tm_ffn = min(512, s)
    tf = 256
    tq = min(256, s)
    tm_proj = tq
    tm_merge = min(512, s)
    hgrn_rows = min(1024, s)

    ffn_wi16, ffn_wo16 = ffn_w_in.astype(BF16), ffn_w_out.astype(BF16)
    mix_w16 = _proj_weight(mix_w_in)
    pool_bd16 = _pool_blockdiag(pool_w).astype(BF16)
    wb16, wo16 = w_branch.astype(BF16), w_out.astype(BF16)
    pool_scale3 = pool_scale.reshape(depth, 1, -1)
    norm_g3 = hgrn_norm_g.reshape(depth, 1, -1)

    mods = _ada_call(c, w_ada, b_ada)
    xs = x.reshape(s, d)
    for l in range(depth):
        mod = mods[l].reshape(3 * N_SUB, d)
        xs = _ffn_call(xs, mod, ffn_wi16, ffn_wo16, ln_g, ln_b, layer=l, which=0, sub=0, alpha=alpha,
                       tm=tm_ffn, tf=tf)
        gates, a, h4, qi, q, k, vt, ki2, wi = _proj_call(xs, mod, mix_w16, layer=l, tm=tm_proj)
        yb = _dsa_call(rel_bias, qi, wi, q, ki2, k, vt, tq=tq)
        yc = _hgrn_call(h4, hgrn_lb, norm_g3, layer=l, rows=hgrn_rows)
        xs = _merge_call(xs, mod, a, yb, yc, gates, pool_bd16, pool_scale3, wb16, wo16, ln_g, ln_b,
                         ffn_wi16, ffn_wo16, layer=l, tm=tm_merge, tf=tf, alpha=alpha)
    return xs.reshape(batch, s, d)
```

```python
import functools
import math

import jax
import jax.numpy as jnp
from jax import lax
from jax.experimental import pallas as pl
from jax.experimental.pallas import tpu as pltpu

F32 = jnp.float32
BF16 = jnp.bfloat16
I32 = jnp.int32

POOL_WINDOWS = (2, 4, 8, 16)
POOL_GROUP = 64
POOL_WIDTH = POOL_GROUP * len(POOL_WINDOWS)
ATTN_HEADS = 4
ATTN_HEAD_DIM = 64
ATTN_WIDTH = ATTN_HEADS * ATTN_HEAD_DIM
IDX_HEADS = 8
IDX_DIM = 64
TOPK_MAX = 256
HGRN_HEADS = 4
HGRN_DIM = 128
HGRN_WIDTH = HGRN_HEADS * HGRN_DIM
KEY_MAX = 1.0 - 1e-6
REL_BUCKETS = 32
REL_MAX_DIST = 128
N_SUB = 3
LN_EPS = 1e-5
RMS_EPS = 1e-6

LANES = 128
SUBLANES = 8
VMEM_LIMIT_BYTES = 58 * 1024 * 1024

NEG_INF = float("-inf")
POS_INF = float("inf")


def _cparams(*sem):
    return pltpu.CompilerParams(dimension_semantics=sem, vmem_limit_bytes=VMEM_LIMIT_BYTES)


def _dot(a, b):
    return jnp.dot(a, b, preferred_element_type=F32)


def _dot_nt(a, b):
    return lax.dot_general(a, b, (((1,), (1,)), ((), ())), preferred_element_type=F32)


def _dot_tn(a, b):
    return lax.dot_general(a, b, (((0,), (0,)), ((), ())), preferred_element_type=F32)


def _log2(n):
    assert n & (n - 1) == 0
    return n.bit_length() - 1


def _silu(x):
    return x * jax.nn.sigmoid(x)


def _mod_rows(mod_ref, j):
    return mod_ref[3 * j:3 * j + 1, :], mod_ref[3 * j + 1:3 * j + 2, :], mod_ref[3 * j + 2:3 * j + 3, :]


def _layer_norm(z, g, b):
    mu = jnp.mean(z, axis=-1, keepdims=True)
    zc = z - mu
    var = jnp.mean(zc * zc, axis=-1, keepdims=True)
    return zc * lax.rsqrt(var + LN_EPS) * g + b


def _ada_kernel(c_ref, w_ref, b_ref, o_ref):
    @pl.when(pl.program_id(1) == 0)
    def _():
        o_ref[0] = b_ref[0]

    c = c_ref[...]
    o_ref[0] += jnp.sum(_silu(c) * w_ref[0], axis=0, keepdims=True)


def _ada_call(c, w_ada, b_ada):
    depth, d, n = w_ada.shape
    rows = LANES
    return pl.pallas_call(
        _ada_kernel,
        grid=(depth, d // rows),
        in_specs=[pl.BlockSpec((rows, 1), lambda l, k: (k, 0)),
                  pl.BlockSpec((1, rows, n), lambda l, k: (l, k, 0)),
                  pl.BlockSpec((1, 1, n), lambda l, k: (l, 0, 0))],
        out_specs=pl.BlockSpec((1, 1, n), lambda l, k: (l, 0, 0)),
        out_shape=jax.ShapeDtypeStruct((depth, 1, n), F32),
        compiler_params=_cparams("arbitrary", "arbitrary"),
        name="ada_mod",
    )(c.reshape(d, 1), w_ada, b_ada.reshape(depth, 1, n))


def _ffn_rows(x, mod_ref, wi_ref, wo_ref, lng_ref, lnb_ref, *, sub, tf, alpha):
    shift, scale, gate = _mod_rows(mod_ref, sub)
    ff = wo_ref.shape[0]
    h = (x * (1.0 + scale) + shift).astype(BF16)
    y = jnp.zeros(x.shape, F32)
    for c0 in range(0, ff, tf):
        g = _dot(h, wi_ref[:, c0:c0 + tf])
        u = _dot(h, wi_ref[:, ff + c0:ff + c0 + tf])
        y = y + _dot((_silu(g) * u).astype(BF16), wo_ref[c0:c0 + tf, :])
    z = alpha * x + (0.5 * (1.0 + gate)) * y
    return _layer_norm(z, lng_ref[sub:sub + 1, :], lnb_ref[sub:sub + 1, :])


def _ffn_kernel(x_ref, mod_ref, wi_ref, wo_ref, lng_ref, lnb_ref, o_ref, *, sub, tf, alpha):
    o_ref[...] = _ffn_rows(x_ref[...], mod_ref, wi_ref, wo_ref, lng_ref, lnb_ref, sub=sub, tf=tf, alpha=alpha)


def _ffn_call(x, mod, w_in, w_out, ln_g, ln_b, *, layer, which, sub, alpha, tm, tf):
    s, d = x.shape
    assert w_out.shape[2] % tf == 0
    kern = functools.partial(_ffn_kernel, sub=sub, tf=tf, alpha=alpha)
    return pl.pallas_call(
        kern,
        grid=(s // tm,),
        in_specs=[pl.BlockSpec((tm, d), lambda i: (i, 0)),
                  pl.BlockSpec((3 * N_SUB, d), lambda i: (0, 0)),
                  _slab(w_in, layer, which), _slab(w_out, layer, which),
                  _slab(ln_g, layer), _slab(ln_b, layer)],
        out_specs=pl.BlockSpec((tm, d), lambda i: (i, 0)),
        out_shape=jax.ShapeDtypeStruct((s, d), F32),
        compiler_params=_cparams("arbitrary"),
        name="ffn",
    )(x, mod, w_in, w_out, ln_g, ln_b)


_PROJ_OUTS = (("gates", 3 * 1024, BF16), ("a", POOL_WIDTH, F32), ("h4", 4 * HGRN_WIDTH, F32),
              ("qi", IDX_HEADS * IDX_DIM, BF16), ("q", ATTN_WIDTH, BF16), ("k", ATTN_WIDTH, BF16),
              ("v", ATTN_WIDTH, BF16), ("ki2", 2 * IDX_DIM, BF16), ("wi", LANES, F32))
_PROJ_CHUNK = 1024


def _proj_kernel(x_ref, mod_ref, w_ref, *o_refs):
    shift, scale, _ = _mod_rows(mod_ref, 1)
    h = (x_ref[...] * (1.0 + scale) + shift).astype(BF16)
    c0 = 0
    for (name, width, dt), o_ref in zip(_PROJ_OUTS, o_refs):
        if name == "v":
            vt = lax.dot_general(w_ref[:, c0:c0 + width], h, (((0,), (1,)), ((), ())),
                                 preferred_element_type=F32)
            ones = jnp.ones((_VT_ROWS - ATTN_HEAD_DIM, vt.shape[1]), F32)
            rows = []
            for p0 in range(0, width, ATTN_HEAD_DIM):
                rows += [vt[p0:p0 + ATTN_HEAD_DIM, :], ones]
            o_ref[0] = jnp.concatenate(rows, axis=0).astype(dt)
        else:
            for cc in range(0, width, _PROJ_CHUNK):
                cw = min(_PROJ_CHUNK, width - cc)
                o_ref[:, cc:cc + cw] = _dot(h, w_ref[:, c0 + cc:c0 + cc + cw]).astype(dt)
        c0 += width


def _proj_weight_kernel(w_ref, o_ref):
    out_width = {name: wd for name, wd, _ in _PROJ_OUTS}
    src = {}
    c0 = 0
    for name, wd in (("a", POOL_WIDTH), ("q", ATTN_WIDTH), ("k", ATTN_WIDTH), ("v", ATTN_WIDTH),
                     ("qi", IDX_HEADS * IDX_DIM), ("ki", IDX_DIM), ("wi", IDX_HEADS),
                     ("h4", 4 * HGRN_WIDTH), ("gates", out_width["gates"])):
        src[name] = (c0, wd)
        c0 += wd
    assert c0 == w_ref.shape[-1]
    q_scale = ATTN_HEAD_DIM ** -0.5 * _LOG2E
    dst = 0
    for name, width, _ in _PROJ_OUTS:
        if name == "ki2":
            pieces = [("ki", 1.0), ("ki", 1.0)]
        elif name == "wi":
            o_ref[:, dst:dst + width] = jnp.zeros((o_ref.shape[0], width), BF16)
            pieces = [("wi", 1.0)]
        else:
            pieces = [(name, q_scale if name == "q" else 1.0)]
        off = dst
        for pname, scale in pieces:
            s0, wd = src[pname]
            o_ref[:, off:off + wd] = (w_ref[:, s0:s0 + wd] * scale).astype(BF16)
            off += wd
        dst += width


_PW_CHUNK = 256


def _proj_weight_t_kernel(wt_ref, o_ref):
    out_width = {name: wd for name, wd, _ in _PROJ_OUTS}
    src, c0 = {}, 0
    for name, wd in (("a", POOL_WIDTH), ("q", ATTN_WIDTH), ("k", ATTN_WIDTH), ("v", ATTN_WIDTH),
                     ("qi", IDX_HEADS * IDX_DIM), ("ki", IDX_DIM), ("wi", IDX_HEADS),
                     ("h4", 4 * HGRN_WIDTH), ("gates", out_width["gates"])):
        src[name] = (c0, wd)
        c0 += wd
    assert c0 == wt_ref.shape[0]
    q_scale = ATTN_HEAD_DIM ** -0.5 * _LOG2E
    flat = []
    for name, width, _ in _PROJ_OUTS:
        if name == "ki2":
            flat += [(src["ki"][0], IDX_DIM, 1.0)] * 2
        elif name == "wi":
            flat += [(src["wi"][0], IDX_HEADS, 1.0), (None, width - IDX_HEADS, 0.0)]
        else:
            flat.append((src[name][0], width, q_scale if name == "q" else 1.0))
    chunks, cur, room = [], [], _PW_CHUNK
    for s0, n, sc in flat:
        while n > 0:
            take = min(n, room)
            cur.append((s0, take, sc))
            s0 = None if s0 is None else s0 + take
            n -= take
            room -= take
            if room == 0:
                chunks.append(cur)
                cur, room = [], _PW_CHUNK
    assert not cur
    step = pl.program_id(1)
    for j, pieces in enumerate(chunks):
        @pl.when(step == j)
        def _(pieces=pieces):
            rows = [jnp.zeros((n, wt_ref.shape[1]), F32) if s0 is None else wt_ref[s0:s0 + n, :] * sc
                    for s0, n, sc in pieces]
            blk = rows[0] if len(rows) == 1 else jnp.concatenate(rows, axis=0)
            o_ref[...] = jnp.transpose(blk).astype(BF16)


def _proj_weight(w):
    depth, d, n_in = w.shape
    n_out = sum(wd for _, wd, _ in _PROJ_OUTS)
    assert n_out % _PW_CHUNK == 0
    return pl.pallas_call(
        _proj_weight_t_kernel,
        grid=(depth, n_out // _PW_CHUNK),
        in_specs=[pl.BlockSpec((None, n_in, d), lambda l, c: (l, 0, 0), pipeline_mode=pl.Buffered(1))],
        out_specs=pl.BlockSpec((None, d, _PW_CHUNK), lambda l, c: (l, 0, c)),
        out_shape=jax.ShapeDtypeStruct((depth, d, n_out), BF16),
        compiler_params=_cparams("arbitrary", "arbitrary"),
        name="proj_weight",
    )(jnp.transpose(w, (0, 2, 1)))


def _slab(arr, *lead):
    block = (None,) * len(lead) + tuple(arr.shape[len(lead):])
    index = tuple(lead) + (0,) * (arr.ndim - len(lead))
    return pl.BlockSpec(block, lambda i: index, pipeline_mode=pl.Buffered(1))


def _proj_call(x, mod, w_all, *, layer, tm):
    s, d = x.shape
    n = w_all.shape[-1]
    assert n == sum(wd for _, wd, _ in _PROJ_OUTS)
    n_vt = ATTN_HEADS * _VT_ROWS
    out_specs = [pl.BlockSpec((1, n_vt, tm), lambda i: (i, 0, 0)) if name == "v"
                 else pl.BlockSpec((tm, wd), lambda i: (i, 0)) for name, wd, _ in _PROJ_OUTS]
    out_shape = [jax.ShapeDtypeStruct((s // tm, n_vt, tm) if name == "v" else (s, wd), dt)
                 for name, wd, dt in _PROJ_OUTS]
    return pl.pallas_call(
        _proj_kernel,
        grid=(s // tm,),
        in_specs=[pl.BlockSpec((tm, d), lambda i: (i, 0)),
                  pl.BlockSpec((3 * N_SUB, d), lambda i: (0, 0)),
                  _slab(w_all, layer)],
        out_specs=out_specs,
        out_shape=out_shape,
        compiler_params=_cparams("arbitrary"),
        name="mix_proj",
    )(x, mod, w_all)


_BISECT_MAX_ITERS = 320
_F32_TINY = float(jnp.finfo(jnp.float32).tiny)
_COARSE_PASSES = 9
_COARSE_MIN_ABS, _COARSE_MAX_ABS = 1e-30, 1e30
_KIND_FAR, _KIND_PAD, _N_KINDS = 2, 3, 4
_VT_ROWS = ATTN_HEAD_DIM + 16
_SMASK_FULL, _SMASK_DIAG, _SMASK_PAD = 0, 1, 2
_LOG2E = math.log2(math.e)


def _t5_bucket(dist):
    max_exact = REL_BUCKETS // 2
    d32 = jnp.maximum(dist, 1).astype(F32)
    large = max_exact + (jnp.log(d32 / max_exact) / math.log(REL_MAX_DIST / max_exact)
                         * (REL_BUCKETS - max_exact)).astype(I32)
    large = jnp.minimum(large, REL_BUCKETS - 1)
    return jnp.where(dist < max_exact, dist, large)


def _dsa_kernel(relb_ref, qi_ref, wi_ref, q_ref, ki2_ref, k_ref, vt_ref, o_ref,
                s_sc, sb_sc, bias_sc, smask_sc, lg_sc, qim_sc, qm_sc, m_sc, acc_sc,
                *, tq, topk, idx_scale):
    tk = tq
    i = pl.program_id(0)
    krow = lax.broadcasted_iota(I32, (tk, tq), 0)
    qcol = lax.broadcasted_iota(I32, (tk, tq), 1)
    causal = krow <= qcol
    lane = lax.broadcasted_iota(I32, (tq, LANES), 1)
    lo_half = lane < ATTN_HEAD_DIM

    @pl.when(i == 0)
    def _():
        for which in range(2):
            bucket = _t5_bucket(jnp.maximum(qcol - krow + which * tq, 0))
            for h in range(ATTN_HEADS):
                b = jnp.zeros((tk, tq), F32)
                for bk in range(REL_BUCKETS):
                    b = jnp.where(bucket == bk, relb_ref[bk, h] * _LOG2E, b)
                bias_sc[h, which] = (jnp.where(causal, b, NEG_INF) if which == 0 else b).astype(BF16)
        for h in range(ATTN_HEADS):
            bias_sc[h, _KIND_FAR] = jnp.full((tk, tq), relb_ref[REL_BUCKETS - 1, h] * _LOG2E, F32).astype(BF16)
            bias_sc[h, _KIND_PAD] = jnp.full((tk, tq), NEG_INF, BF16)
        smask_sc[_SMASK_FULL] = jnp.zeros((tk, tq), F32)
        smask_sc[_SMASK_DIAG] = jnp.where(causal, 0.0, NEG_INF)
        smask_sc[_SMASK_PAD] = jnp.full((tk, tq), NEG_INF, F32)

    def head_operands(src_ref, dst_sc, n_heads):
        src = src_ref[...].astype(F32)
        for h in range(n_heads):
            pair = src[:, LANES * (h // 2):LANES * (h // 2 + 1)]
            pair = jnp.where(lo_half if h % 2 == 0 else ~lo_half, pair, 0.0)
            dst_sc[h] = jnp.transpose(pair).astype(BF16)

    head_operands(qi_ref, qim_sc, IDX_HEADS)
    head_operands(q_ref, qm_sc, ATTN_HEADS)

    wt = jnp.transpose(wi_ref[...])

    def key_rows(jb):
        return pl.ds(pl.multiple_of(jb * tk, tk), tk)

    def fold(x):
        return x.reshape(tk // SUBLANES, SUBLANES, tq)

    def index_scores(jb):
        kb = ki2_ref[key_rows(jb), :]
        acc = jnp.zeros((tk, tq), F32)
        for h in range(IDX_HEADS):
            acc = acc + wt[h:h + 1, :] * jnp.maximum(_dot(kb, qim_sc[h]), 0.0)
        return acc * idx_scale

    npair = (i + 2) // 2
    last_kv = k_ref.shape[0] // tk - 1

    def p1_blocks(jbs, carry):
        rmin, rmax = carry
        for jb in jbs:
            raw = index_scores(jnp.minimum(jb, last_kv))
            kind = jnp.where(jb > i, _SMASK_PAD, jnp.where(jb == i, _SMASK_DIAG, _SMASK_FULL))
            sc = raw + smask_sc[kind]
            s_sc[jb] = sc
            sb_sc[jb] = sc.astype(BF16)
            rmin = jnp.minimum(rmin, jnp.min(fold(raw), axis=0))
            rmax = jnp.maximum(rmax, jnp.max(fold(raw), axis=0))
        return rmin, rmax

    nquad = npair // 2

    def walk_pairs(pairs_fn, init):
        n4 = npair // 4
        carry = lax.fori_loop(0, n4, lambda t, c: pairs_fn([4 * t + r for r in range(4)], c), init)
        n2 = (npair - 4 * n4) // 2
        carry = lax.fori_loop(2 * n4, 2 * n4 + n2, lambda t, c: pairs_fn([2 * t, 2 * t + 1], c), carry)
        return lax.fori_loop(4 * n4 + 2 * n2, npair, lambda t, c: pairs_fn([t], c), carry)

    rmin, rmax = walk_pairs(lambda js, c: p1_blocks([2 * j + r for j in js for r in range(2)], c),
                            (jnp.full((SUBLANES, tq), POS_INF, F32), jnp.full((SUBLANES, tq), NEG_INF, F32)))
    rowmin = jnp.min(rmin, axis=0, keepdims=True)
    rowmax = jnp.max(rmax, axis=0, keepdims=True)

    kf = float(topk)
    nfin = (i * tq + lax.broadcasted_iota(I32, (1, tq), 1) + 1).astype(F32)
    small = nfin <= kf

    def walk_blocks(blocks_fn, init):
        return lax.fori_loop(0, npair, lambda j, c: blocks_fn([2 * j, 2 * j + 1], c), init)

    def count_ge(x):
        def blocks_fn(jbs, cnt):
            for jb in jbs:
                cnt = cnt + jnp.sum(fold(jnp.where(s_sc[jb] >= x, 1.0, 0.0)), axis=0)
            return cnt
        return jnp.sum(walk_blocks(blocks_fn, jnp.zeros((SUBLANES, tq), F32)), axis=0, keepdims=True)

    def count_ge16(xb):
        one, zero = jnp.ones((), BF16), jnp.zeros((), BF16)
        rows16 = 2 * SUBLANES

        def blocks_fn(jbs, cnt):
            parts = []
            for jb in jbs:
                c = jnp.where(sb_sc[jb] >= xb, one, zero)
                parts += [c[r0:r0 + rows16, :] for r0 in range(0, tk, rows16)]
            while len(parts) > 1:
                parts = [a + b for a, b in zip(parts[0::2], parts[1::2])]
            return cnt + parts[0].astype(F32)
        return jnp.sum(walk_blocks(blocks_fn, jnp.zeros((rows16, tq), F32)), axis=0, keepdims=True)

    def coarse_probe(mid):
        xb = mid.astype(BF16)
        xbits = lax.bitcast_convert_type(xb.astype(F32), I32)
        neg = xbits < 0
        mu = jnp.where(neg, xbits + 0x8000, xbits - 0x8000)
        odd = (xbits & 0x10000) != 0
        tbits = jnp.where(odd, jnp.where(neg, mu - 1, mu + 1), mu)
        t = lax.bitcast_convert_type(tbits, F32)
        ok = jnp.logical_and(jnp.abs(xb.astype(F32)) > _COARSE_MIN_ABS, jnp.abs(xb.astype(F32)) < _COARSE_MAX_ABS)
        return xb, t, ok

    def narrow(st, probe, c, active):
        lo, hi, clo, chi, done = st
        up = jnp.logical_and(active, c >= kf)
        dn = jnp.logical_and(active, c < kf)
        return (jnp.where(up, probe, lo), jnp.where(dn, probe, hi), jnp.where(up, c, clo),
                jnp.where(dn, c, chi), jnp.where(jnp.logical_and(active, c == kf), 1.0, done))

    def coarse_body(_, st):
        lo, hi, _, _, done = st
        xb, t, ok = coarse_probe(0.5 * lo + 0.5 * hi)
        ok = jnp.logical_and(ok, jnp.logical_and(t > lo, t < hi))
        return narrow(st, t, count_ge16(xb), jnp.logical_and(done < 0.5, ok))

    def fine_cond(st):
        it, (_, _, _, _, done) = st
        return jnp.logical_and(it < _BISECT_MAX_ITERS, jnp.min(done) < 0.5)

    def fine_body(st):
        it, inner = st
        lo, hi, _, _, done = inner
        mid = 0.5 * lo + 0.5 * hi
        mid = jnp.where(jnp.logical_and(lo == 0.0, hi > _F32_TINY), _F32_TINY, mid)
        mid = jnp.where(jnp.logical_and(lo < 0.0, hi > 0.0), 0.0, mid)
        stuck = jnp.logical_or(jnp.logical_or(mid <= lo, mid >= hi),
                               jnp.logical_and(lo == 0.0, hi <= _F32_TINY))
        lo, hi, clo, chi, done = narrow(inner, mid, count_ge(mid),
                                        jnp.logical_and(done < 0.5, jnp.logical_not(stuck)))
        return it + 1, (lo, hi, clo, chi, jnp.where(stuck, 1.0, done))

    mbits = lax.bitcast_convert_type(rowmax, I32)
    above_max = jnp.where(rowmax == 0.0, _F32_TINY,
                          lax.bitcast_convert_type(jnp.where(rowmax > 0.0, mbits + 1, mbits - 1), F32))
    st0 = (jnp.where(small, NEG_INF, rowmin), above_max, nfin, jnp.zeros((1, tq), F32),
           jnp.where(small, 1.0, 0.0))
    st1 = lax.fori_loop(0, _COARSE_PASSES, coarse_body, st0)
    _, (lo, hi, clo, chi, _) = lax.while_loop(fine_cond, fine_body, (jnp.int32(0), st1))
    budget = kf - chi
    need_ties = jnp.max((clo - chi) - budget) > 0.5

    pairs = [slice(LANES * (h // 2), LANES * (h // 2 + 1)) for h in range(ATTN_HEADS)]

    def issue_qk(jb, slot):
        kb = k_ref[key_rows(jnp.minimum(jb, last_kv)), :]
        for h in range(ATTN_HEADS):
            lg_sc[slot, h] = _dot(kb[:, pairs[h]], qm_sc[h]).astype(BF16)

    def attend(jb, slot, thr, cum=None):
        s = s_sc[jb]
        if cum is None:
            sel = s >= thr
        else:
            cand = jnp.logical_and(s >= lo, s < hi)
            before = (qcol < krow).astype(BF16)
            rank = cum + _dot(before, jnp.where(cand, 1.0, 0.0).astype(BF16))
            sel = jnp.logical_or(s >= hi, jnp.logical_and(cand, rank < budget))
        kind = jnp.where(jb > i, _KIND_PAD, jnp.minimum(i - jb, _KIND_FAR))
        selm = jnp.where(sel, 0.0, NEG_INF).astype(BF16)
        probs, alphas = [], []
        for h in range(ATTN_HEADS):
            lg = lg_sc[slot, h] + bias_sc[h, kind] + selm
            m_old = m_sc[h]
            m_new = jnp.maximum(m_old, jnp.max(lg, axis=0, keepdims=True).astype(F32))
            m_safe = jnp.where(m_new == NEG_INF, 0.0, m_new)
            alphas.append(jnp.exp2(m_old - m_safe))
            probs.append(jnp.exp2(lg - m_safe.astype(BF16)))
            m_sc[h] = m_new
        vb = vt_ref[jnp.minimum(jb, last_kv)]
        for h in range(ATTN_HEADS):
            vrows = slice(_VT_ROWS * h, _VT_ROWS * (h + 1))
            acc_sc[h] = alphas[h] * acc_sc[h] + _dot(vb[vrows, :], probs[h])

    def run(with_ties):
        m_sc[...] = jnp.full(m_sc.shape, NEG_INF, F32)
        acc_sc[...] = jnp.zeros_like(acc_sc)
        issue_qk(0, 0)

        def pair(j, thr0, thr1, cum0=None, cum1=None):
            issue_qk(2 * j + 1, 1)
            attend(2 * j, 0, thr0, cum0)
            issue_qk(2 * j + 2, 0)
            attend(2 * j + 1, 1, thr1, cum1)

        if not with_ties:
            def pairs_fn(js, carry):
                for j in js:
                    pair(j, lo, lo)
                return carry
            walk_pairs(pairs_fn, 0)
            return

        last_s = s_sc.shape[0] - 1

        def cand_count(jb):
            s = s_sc[jnp.minimum(jb, last_s)]
            cand = jnp.logical_and(s >= lo, s < hi)
            return jnp.sum(jnp.where(cand, 1.0, 0.0), axis=0, keepdims=True)

        def needs_ranking(cum, counts):
            runs_out = None
            for cnt in counts:
                here = jnp.logical_and(cum + cnt > budget, cum < budget)
                runs_out = here if runs_out is None else jnp.logical_or(runs_out, here)
                cum = cum + cnt
            return (jnp.max(jnp.where(runs_out, 1.0, 0.0)) > 0.5).astype(I32)

        def tie_group_body(n_blocks):
            def body(g, carry):
                ranked, cums = carry[0], [carry[1]]
                for cnt in carry[2:]:
                    cums.append(cums[-1] + cnt)
                base = n_blocks * g

                def variant(with_rank):
                    def fn():
                        ahead = [cand_count(base + n_blocks + r) for r in range(n_blocks)]
                        nxt = needs_ranking(cums[-1], ahead)
                        for p in range(n_blocks // 2):
                            if with_rank:
                                pair(base // 2 + p, None, None, cums[2 * p], cums[2 * p + 1])
                            else:
                                pair(base // 2 + p, jnp.where(cums[2 * p + 1] <= budget, lo, hi),
                                     jnp.where(cums[2 * p + 2] <= budget, lo, hi))
                        return (nxt, cums[-1], *ahead)
                    return fn
                return lax.cond(ranked == 1, variant(True), variant(False))
            return body

        zero = jnp.zeros((1, tq), F32)
        first = [cand_count(r) for r in range(4)]
        quads = lax.fori_loop(0, nquad, tie_group_body(4), (needs_ranking(zero, first), zero, *first))
        cum, ca, cb = quads[1], quads[2], quads[3]
        lax.fori_loop(2 * nquad, npair, tie_group_body(2), (needs_ranking(cum, [ca, cb]), cum, ca, cb))

    @pl.when(need_ties)
    def _():
        run(True)

    @pl.when(jnp.logical_not(need_ties))
    def _():
        run(False)

    hd = ATTN_HEAD_DIM
    normed = [acc_sc[h, 0:hd, :] / acc_sc[h, hd:hd + 1, :] for h in range(ATTN_HEADS)]
    for p in range(ATTN_HEADS // 2):
        pair_t = jnp.concatenate([normed[2 * p], normed[2 * p + 1]], axis=0)
        o_ref[:, LANES * p:LANES * (p + 1)] = jnp.transpose(pair_t)


def _dsa_call(rel_bias, qi, wi, q, ki2, k, vt, *, tq):
    s = q.shape[0]
    assert tq > REL_MAX_DIST and s % tq == 0
    nb = s // tq
    topk = min(TOPK_MAX, s // 4)
    kern = functools.partial(_dsa_kernel, tq=tq, topk=topk,
                             idx_scale=(IDX_DIM ** -0.5) * (IDX_HEADS ** -0.5))
    whole = lambda width: pl.BlockSpec((s, width), lambda i: (0, 0), pipeline_mode=pl.Buffered(1))
    assert vt.shape == (nb, ATTN_HEADS * _VT_ROWS, tq)
    return pl.pallas_call(
        kern,
        grid=(nb,),
        in_specs=[pl.BlockSpec(memory_space=pltpu.SMEM),
                  pl.BlockSpec((tq, IDX_HEADS * IDX_DIM), lambda i: (i, 0)),
                  pl.BlockSpec((tq, LANES), lambda i: (i, 0)),
                  pl.BlockSpec((tq, ATTN_WIDTH), lambda i: (i, 0)),
                  whole(2 * IDX_DIM), whole(ATTN_WIDTH),
                  pl.BlockSpec(vt.shape, lambda i: (0, 0, 0), pipeline_mode=pl.Buffered(1))],
        out_specs=pl.BlockSpec((tq, ATTN_WIDTH), lambda i: (i, 0)),
        out_shape=jax.ShapeDtypeStruct((s, ATTN_WIDTH), F32),
        scratch_shapes=[pltpu.VMEM((nb + nb % 2, tq, tq), F32),
                        pltpu.VMEM((nb + nb % 2, tq, tq), BF16),
                        pltpu.VMEM((ATTN_HEADS, _N_KINDS, tq, tq), BF16),
                        pltpu.VMEM((3, tq, tq), F32),
                        pltpu.VMEM((2, ATTN_HEADS, tq, tq), BF16),
                        pltpu.VMEM((IDX_HEADS, LANES, tq), BF16),
                        pltpu.VMEM((ATTN_HEADS, LANES, tq), BF16),
                        pltpu.VMEM((ATTN_HEADS, 1, tq), F32),
                        pltpu.VMEM((ATTN_HEADS, _VT_ROWS, tq), F32)],
        compiler_params=_cparams("arbitrary"),
        name="dsa_attention",
    )(rel_bias, qi, wi, q, ki2, k, vt)


_HGRN_CHUNK = 128
_HGRN_BASE = SUBLANES


def _hgrn_kernel(h4_ref, lb_ref, ng_ref, o_ref, st_sc, *, layer, rows):
    c_len = _HGRN_CHUNK

    @pl.when(pl.program_id(0) == 0)
    def _():
        st_sc[...] = jnp.zeros_like(st_sc)

    lb_all = lb_ref[...]
    e = jnp.exp(lb_all - jnp.max(lb_all, axis=0, keepdims=True))
    sm = e / jnp.sum(e, axis=0, keepdims=True)
    lbs = jnp.sum(sm[0:layer + 1, :], axis=0, keepdims=True) - sm[0:1, :]

    r1 = lax.broadcasted_iota(I32, (c_len, 1), 0)
    rr = lax.broadcasted_iota(I32, (c_len, c_len), 0)
    cc = lax.broadcasted_iota(I32, (c_len, c_len), 1)

    def block_ref(a, blk, r):
        x = a.reshape(c_len // blk, blk, a.shape[-1])[:, r:r + 1, :]
        return jnp.broadcast_to(x, (c_len // blk, blk, a.shape[-1])).reshape(c_len, a.shape[-1])

    w = HGRN_WIDTH
    for c in range(rows // c_len):
        rs = slice(c * c_len, (c + 1) * c_len)
        hq = h4_ref[rs, 0:w]
        z = h4_ref[rs, w:2 * w]
        val = h4_ref[rs, 2 * w:3 * w]
        hg = h4_ref[rs, 3 * w:4 * w]
        key = (1.0 - lbs) * jax.nn.sigmoid(-z)
        g = jnp.log1p(-jnp.minimum(key, KEY_MAX))
        a_cum = g
        sh = 1
        while sh < c_len:
            a_cum = a_cum + jnp.where(r1 >= sh, pltpu.roll(a_cum, sh, 0), 0.0)
            sh *= 2
        qf = _silu(hq)
        outs = []
        for h in range(HGRN_HEADS):
            hs = slice(h * HGRN_DIM, (h + 1) * HGRN_DIM)
            ah, qh, kh, vh = a_cum[:, hs], qf[:, hs], key[:, hs], val[:, hs]
            vh16 = vh.astype(BF16)
            st = st_sc[h]
            o = _dot_nt((qh * jnp.exp(ah)).astype(BF16), st.astype(BF16))
            ref = block_ref(ah, _HGRN_BASE, _HGRN_BASE // 2 - 1)
            sc = _dot_nt((qh * jnp.exp(ah - ref)).astype(BF16), (kh * jnp.exp(ref - ah)).astype(BF16))
            same = (rr >> _log2(_HGRN_BASE)) == (cc >> _log2(_HGRN_BASE))
            scores = jnp.where(jnp.logical_and(same, cc <= rr), sc, 0.0)
            half = _HGRN_BASE
            while half < c_len:
                blk = 2 * half
                ref = block_ref(ah, blk, half - 1)
                second = (r1 & (blk - 1)) >= half
                ql = jnp.where(second, qh * jnp.exp(jnp.minimum(ah - ref, 0.0)), 0.0)
                kl = jnp.where(second, 0.0, kh * jnp.exp(jnp.minimum(ref - ah, 0.0)))
                sc = _dot_nt(ql.astype(BF16), kl.astype(BF16))
                scores = scores + jnp.where((rr >> _log2(blk)) == (cc >> _log2(blk)), sc, 0.0)
                half = blk
            o = o + _dot(scores.astype(BF16), vh16)
            a_last = ah[c_len - 1:c_len, :]
            kd = (kh * jnp.exp(a_last - ah)).astype(BF16)
            st_sc[h] = st * jnp.exp(a_last) + _dot_tn(vh16, kd)
            o = o * lax.rsqrt(jnp.mean(o * o, axis=-1, keepdims=True) + RMS_EPS)
            outs.append(o)
        o_ref[rs, :] = jnp.concatenate(outs, axis=1) * ng_ref[...] * _silu(hg)


def _hgrn_call(h4, lb, ng, *, layer, rows):
    s = h4.shape[0]
    kern = functools.partial(_hgrn_kernel, layer=layer, rows=rows)
    return pl.pallas_call(
        kern,
        grid=(s // rows,),
        in_specs=[pl.BlockSpec((rows, 4 * HGRN_WIDTH), lambda i: (i, 0)),
                  pl.BlockSpec(lb.shape, lambda i: (0, 0)),
                  _slab(ng, layer)],
        out_specs=pl.BlockSpec((rows, HGRN_WIDTH), lambda i: (i, 0)),
        out_shape=jax.ShapeDtypeStruct((s, HGRN_WIDTH), F32),
        scratch_shapes=[pltpu.VMEM((HGRN_HEADS, HGRN_DIM, HGRN_DIM), F32)],
        compiler_params=_cparams("arbitrary"),
        name="hgrn2",
    )(h4, lb, ng)


_POOL_HALO = 16


def _merge_kernel(x_ref, mod_ref, a_ref, ah_ref, yb_ref, yc_ref, g_ref, pw_ref, ps_ref, wb_ref, wo_ref,
                  lng_ref, lnb_ref, fwi_ref, fwo_ref, o_ref, *, tm, tf, alpha):
    i = pl.program_id(0)
    d = x_ref.shape[1]
    _, _, gate = _mod_rows(mod_ref, 1)
    a = a_ref[...]
    halo = jnp.where(i == 0, 0.0, ah_ref[...])
    ext = jnp.concatenate([halo, a], axis=0)
    lane = lax.broadcasted_iota(I32, (tm, POOL_WIDTH), 1)
    grp = lane >> _log2(POOL_GROUP)
    pos1 = (i * tm + lax.broadcasted_iota(I32, (tm, POOL_WIDTH), 0) + 1).astype(F32)
    win = jnp.zeros((tm, POOL_WIDTH), F32)
    cnt = jnp.zeros((tm, POOL_WIDTH), F32)
    ssum, have = ext, 1
    for gi, wd in enumerate(POOL_WINDOWS):
        while have < wd:
            ssum = ssum + pltpu.roll(ssum, have, 0)
            have *= 2
        assert have == wd
        win = jnp.where(grp == gi, ssum[_POOL_HALO:, :], win)
        cnt = jnp.where(grp == gi, jnp.minimum(pos1, float(wd)), cnt)
    dlt = (win / cnt - a).astype(BF16)
    ya = _dot(dlt, pw_ref[...]) * ps_ref[...]
    za = _dot(ya.astype(BF16), wb_ref[0:POOL_WIDTH, :])
    zb = _dot(yb_ref[...].astype(BF16), wb_ref[POOL_WIDTH:POOL_WIDTH + ATTN_WIDTH, :])
    zc = _dot(yc_ref[...].astype(BF16), wb_ref[POOL_WIDTH + ATTN_WIDTH:, :])
    g0, g1, g2 = (jax.nn.sigmoid(g_ref[:, j * d:(j + 1) * d].astype(F32)) for j in range(3))
    merged = g0 * za + g1 * zb + g2 * zc
    y = _dot(merged.astype(BF16), wo_ref[...])
    z = alpha * x_ref[...] + (1.0 + gate) * y
    xm = _layer_norm(z, lng_ref[1:2, :], lnb_ref[1:2, :])
    o_ref[...] = _ffn_rows(xm, mod_ref, fwi_ref, fwo_ref, lng_ref, lnb_ref, sub=2, tf=tf, alpha=alpha)


def _merge_call(x, mod, a, yb, yc, gates, pw_bd, ps, wb, wo, ln_g, ln_b, ffn_wi, ffn_wo, *, layer, tm, tf, alpha):
    s, d = x.shape
    kern = functools.partial(_merge_kernel, tm=tm, tf=tf, alpha=alpha)
    full = lambda arr: _slab(arr, layer)
    hb = tm // _POOL_HALO
    return pl.pallas_call(
        kern,
        grid=(s // tm,),
        in_specs=[pl.BlockSpec((tm, d), lambda i: (i, 0)),
                  pl.BlockSpec((3 * N_SUB, d), lambda i: (0, 0)),
                  pl.BlockSpec((tm, POOL_WIDTH), lambda i: (i, 0)),
                  pl.BlockSpec((_POOL_HALO, POOL_WIDTH), lambda i: (jnp.maximum(i * hb - 1, 0), 0)),
                  pl.BlockSpec((tm, ATTN_WIDTH), lambda i: (i, 0)),
                  pl.BlockSpec((tm, HGRN_WIDTH), lambda i: (i, 0)),
                  pl.BlockSpec((tm, 3 * d), lambda i: (i, 0)),
                  full(pw_bd), full(ps), full(wb), full(wo), full(ln_g), full(ln_b),
                  _slab(ffn_wi, layer, 1), _slab(ffn_wo, layer, 1)],
        out_specs=pl.BlockSpec((tm, d), lambda i: (i, 0)),
        out_shape=jax.ShapeDtypeStruct((s, d), F32),
        compiler_params=_cparams("arbitrary"),
        name="merge_ffn",
    )(x, mod, a, a, yb, yc, gates, pw_bd, ps, wb, wo, ln_g, ln_b, ffn_wi, ffn_wo)


def _pool_blockdiag(pool_w):
    depth, n, g, _ = pool_w.shape
    eye = jnp.eye(n, dtype=pool_w.dtype)
    return jnp.einsum("lgij,gh->lgihj", pool_w, eye).reshape(depth, n * g, n * g)


def kernel(x, c, w_ada, b_ada, ln_g, ln_b, ffn_w_in, ffn_w_out, mix_w_in, pool_w, pool_scale, rel_bias,
           hgrn_lb, hgrn_norm_g, w_branch, w_out):
    batch, s, d = x.shape
    assert batch == 1, "kernels are written for a single sequence"
    depth = w_ada.shape[0]
    alpha = (2 * depth) ** 0.25
    tm_ffn = min(512, s)
    tf = 256
    tq = min(256, s)
    tm_proj = tq
    tm_merge = min(512, s)
    hgrn_rows = min(1024, s)

    ffn_wi16, ffn_wo16 = ffn_w_in.astype(BF16), ffn_w_out.astype(BF16)
    mix_w16 = _proj_weight(mix_w_in)
    pool_bd16 = _pool_blockdiag(pool_w).astype(BF16)
    wb16, wo16 = w_branch.astype(BF16), w_out.astype(BF16)
    pool_scale3 = pool_scale.reshape(depth, 1, -1)
    norm_g3 = hgrn_norm_g.reshape(depth, 1, -1)

    mods = _ada_call(c, w_ada, b_ada)
    xs = x.reshape(s, d)
    for l in range(depth):
        mod = mods[l].reshape(3 * N_SUB, d)
        xs = _ffn_call(xs, mod, ffn_wi16, ffn_wo16, ln_g, ln_b, layer=l, which=0, sub=0, alpha=alpha,
                       tm=tm_ffn, tf=tf)
        gates, a, h4, qi, q, k, vt, ki2, wi = _proj_call(xs, mod, mix_w16, layer=l, tm=tm_proj)
        yb = _dsa_call(rel_bias, qi, wi, q, ki2, k, vt, tq=tq)
        yc = _hgrn_call(h4, hgrn_lb, norm_g3, layer=l, rows=hgrn_rows)
        xs = _merge_call(xs, mod, a, yb, yc, gates, pool_bd16, pool_scale3, wb16, wo16, ln_g, ln_b,
                         ffn_wi16, ffn_wo16, layer=l, tm=tm_merge, tf=tf, alpha=alpha)
    return xs.reshape(batch, s, d)
```
